```python
import jax, jax.numpy as jnp
from jax import lax
import numpy as np

D_MODEL = 1024
BATCH = 1
SEQ = 16384
DEPTH = 1
DEC_BATCH = 8
DEC_SEQ = 16
PAST_LEN = 4096

CHUNK = 64
FOX_HEADS = 8
FOX_HEAD_DIM = 64
FOX_WIDTH = FOX_HEADS * FOX_HEAD_DIM
FOX_Q_BLOCK = 128
HG_HEADS = 4
HG_KEY_DIM = 128
HG_VAL_DIM = 128
HG_WIDTH = HG_HEADS * HG_KEY_DIM
D_FF = 2816
FFN_CONV_WIDTH = 3
N_BRANCHES = 2
RMS_EPS = 1e-6
NEG_INF = -1e30
IN_SIZES = [FOX_WIDTH, FOX_WIDTH, FOX_WIDTH, FOX_HEADS,
            HG_WIDTH, HG_WIDTH, HG_HEADS * HG_VAL_DIM, HG_HEADS * HG_VAL_DIM,
            D_MODEL, D_MODEL]
N_IN = sum(IN_SIZES)

kernel_name = "fox_hgrn2_convffn_stream_step"


def rmsnorm(x, gain):
    xf = x.astype(jnp.float32)
    inv = lax.rsqrt(jnp.mean(xf * xf, axis=-1, keepdims=True) + RMS_EPS)
    return (xf * inv).astype(x.dtype) * gain


def split_in(z):
    offs = np.cumsum(IN_SIZES)[:-1].tolist()
    return jnp.split(z, offs, axis=-1)


def fox_block(q, k, v, cq, ck, qpos, kpos):
    scale = q.shape[-1] ** -0.5
    logits = jnp.einsum('bqhd,bkhd->bhqk', q, k).astype(jnp.float32) * scale
    bias = jnp.transpose(cq, (0, 2, 1))[..., :, None] - jnp.transpose(ck, (0, 2, 1))[..., None, :]
    mask = kpos[None, :] <= qpos[:, None]
    logits = jnp.where(mask, logits + bias, NEG_INF)
    p = jax.nn.softmax(logits, axis=-1)
    return jnp.einsum('bhqk,bkhd->bqhd', p.astype(v.dtype), v)


def fox_prompt(q, k, v, logf):
    B, S, H, Dh = q.shape
    cum = jnp.cumsum(logf.astype(jnp.float32), axis=1)
    kpos = jnp.arange(S)

    def block(n):
        start = n * FOX_Q_BLOCK
        qb = lax.dynamic_slice_in_dim(q, start, FOX_Q_BLOCK, axis=1)
        cb = lax.dynamic_slice_in_dim(cum, start, FOX_Q_BLOCK, axis=1)
        qpos = start + jnp.arange(FOX_Q_BLOCK)
        return fox_block(qb, k, v, cb, cum, qpos, kpos)

    out = lax.map(block, jnp.arange(S // FOX_Q_BLOCK))
    return jnp.moveaxis(out, 0, 1).reshape(B, S, H, Dh)


def fox_sample(q, k, v, logf, ck, cv, clogf):
    P = ck.shape[1]
    T = q.shape[1]
    k_all = jnp.concatenate([ck.astype(k.dtype), k], axis=1)
    v_all = jnp.concatenate([cv.astype(v.dtype), v], axis=1)
    cum = jnp.cumsum(jnp.concatenate([clogf.astype(jnp.float32), logf.astype(jnp.float32)], axis=1), axis=1)
    qpos = P + jnp.arange(T)
    kpos = jnp.arange(P + T)
    return fox_block(q, k_all, v_all, cum[:, P:], cum, qpos, kpos)


def hgrn_chunk(S0, q, k, i, logf):
    T = q.shape[1]
    b = jnp.cumsum(logf, axis=1)
    causal = jnp.tril(jnp.ones((T, T), dtype=bool))
    diff = b[:, :, None] - b[:, None, :]
    decay = jnp.exp(jnp.where(causal[None, :, :, None, None], diff, -jnp.inf))
    A = jnp.einsum('bthd,bshd,btshd->bhts', q, k, decay)
    o = jnp.einsum('bhts,bshv->bthv', A, i) + jnp.einsum('bthd,bhdv->bthv', q * jnp.exp(b), S0)
    b_last = b[:, -1]
    S_new = jnp.exp(b_last)[..., None] * S0 + jnp.einsum('bshd,bshv->bhdv', k * jnp.exp(b_last[:, None] - b), i)
    return S_new, o


def hgrn_scan(S0, q, k, i, logf):
    B, T, H, dk = q.shape
    blk = CHUNK if T % CHUNK == 0 else T
    nc = T // blk

    def to_chunks(a):
        return jnp.moveaxis(a.reshape(B, nc, blk, *a.shape[2:]), 1, 0)

    S_fin, o = lax.scan(lambda s, c: hgrn_chunk(s, *c), S0,
                        (to_chunks(q), to_chunks(k), to_chunks(i), to_chunks(logf)))
    return S_fin, jnp.moveaxis(o, 0, 1).reshape(B, T, H, i.shape[-1])


def token_mixer(h, fox_ctx, S0, w_in, fox_f_bias, lb, hgrn_norm, w_bf, w_bh, w_out):
    B, T, _ = h.shape
    fq, fk, fv, ff, hq, hf, hi, hg, ga, gb = split_in(jnp.einsum('btd,dn->btn', h, w_in))
    q = fq.reshape(B, T, FOX_HEADS, FOX_HEAD_DIM)
    k = fk.reshape(B, T, FOX_HEADS, FOX_HEAD_DIM)
    v = fv.reshape(B, T, FOX_HEADS, FOX_HEAD_DIM)
    logf = jax.nn.log_sigmoid((ff + fox_f_bias).astype(jnp.float32))
    if fox_ctx is None:
        o_fox = fox_prompt(q, k, v, logf)
    else:
        o_fox = fox_sample(q, k, v, logf, *fox_ctx)
    f = lb + (1.0 - lb) * jax.nn.sigmoid(hf.astype(jnp.float32).reshape(B, T, HG_HEADS, HG_KEY_DIM))
    hq4 = hq.astype(jnp.float32).reshape(B, T, HG_HEADS, HG_KEY_DIM)
    hi4 = hi.astype(jnp.float32).reshape(B, T, HG_HEADS, HG_VAL_DIM)
    if S0 is None:
        S0 = jnp.zeros((B, HG_HEADS, HG_KEY_DIM, HG_VAL_DIM), jnp.float32)
    S_new, o_hg = hgrn_scan(S0.astype(jnp.float32), hq4, 1.0 - f, hi4, jnp.log(f))
    o_hg = rmsnorm(o_hg, hgrn_norm.astype(jnp.float32)) * jax.nn.sigmoid(
        hg.astype(jnp.float32).reshape(B, T, HG_HEADS, HG_VAL_DIM))
    br_f = jnp.einsum('btk,kd->btd', o_fox.reshape(B, T, FOX_WIDTH), w_bf)
    br_h = jnp.einsum('btk,kd->btd', o_hg.reshape(B, T, HG_HEADS * HG_VAL_DIM).astype(h.dtype), w_bh)
    merged = jax.nn.sigmoid(ga) * br_f + jax.nn.sigmoid(gb) * br_h
    out = jnp.einsum('btd,de->bte', merged, w_out)
    return out, k, v, logf, S_new


def conv_ffn(h, hist, w_up, conv_w, conv_b, w_down):
    up = jnp.einsum('btd,df->btf', h, w_up)
    a, g = jnp.split(up, 2, axis=-1)
    T = a.shape[1]
    a_pad = jnp.concatenate([hist.astype(a.dtype), a], axis=1)
    c = conv_b + sum(conv_w[j] * a_pad[:, j:j + T] for j in range(FFN_CONV_WIDTH))
    out = jnp.einsum('btf,fd->btd', jax.nn.gelu(c) * g, w_down)
    return out, a_pad[:, -(FFN_CONV_WIDTH - 1):]


def setup_inputs(seed: int = 0) -> dict:
    key = jax.random.key(seed)
    ks = jax.random.split(key, 24)
    n = jax.random.normal
    f32 = jnp.float32
    return {
        "x_prompt": n(ks[0], (BATCH, SEQ, D_MODEL), f32),
        "x_sample": n(ks[1], (DEC_BATCH, DEC_SEQ, D_MODEL), f32),
        "cache_fox_k": n(ks[2], (DEPTH, DEC_BATCH, PAST_LEN, FOX_HEADS, FOX_HEAD_DIM), f32),
        "cache_fox_v": n(ks[3], (DEPTH, DEC_BATCH, PAST_LEN, FOX_HEADS, FOX_HEAD_DIM), f32),
        "cache_fox_logf": jax.nn.log_sigmoid(3.0 + n(ks[4], (DEPTH, DEC_BATCH, PAST_LEN, FOX_HEADS), f32)),
        "state_hgrn": 0.5 * n(ks[5], (DEPTH, DEC_BATCH, HG_HEADS, HG_KEY_DIM, HG_VAL_DIM), f32),
        "state_ffn_conv": n(ks[6], (DEPTH, DEC_BATCH, FFN_CONV_WIDTH - 1, D_FF), f32),
        "norm_mix_pre": 1.0 + 0.02 * n(ks[7], (DEPTH, D_MODEL), f32),
        "norm_mix_post": 1.0 + 0.02 * n(ks[8], (DEPTH, D_MODEL), f32),
        "w_in": n(ks[9], (DEPTH, D_MODEL, N_IN), f32) * D_MODEL ** -0.5,
        "fox_f_bias": 3.0 + 0.5 * n(ks[10], (DEPTH, FOX_HEADS), f32),
        "hgrn_lb_logits": n(ks[11], (DEPTH + 1, HG_WIDTH), f32),
        "hgrn_norm": 1.0 + 0.02 * n(ks[12], (DEPTH, HG_VAL_DIM), f32),
        "w_branch_fox": n(ks[13], (DEPTH, FOX_WIDTH, D_MODEL), f32) * FOX_WIDTH ** -0.5,
        "w_branch_hgrn": n(ks[14], (DEPTH, HG_HEADS * HG_VAL_DIM, D_MODEL), f32) * (HG_HEADS * HG_VAL_DIM) ** -0.5,
        "w_out": n(ks[15], (DEPTH, D_MODEL, D_MODEL), f32) * D_MODEL ** -0.5,
        "norm_ffn_pre": 1.0 + 0.02 * n(ks[16], (DEPTH, D_MODEL), f32),
        "norm_ffn_post": 1.0 + 0.02 * n(ks[17], (DEPTH, D_MODEL), f32),
        "w_up": n(ks[18], (DEPTH, D_MODEL, 2 * D_FF), f32) * D_MODEL ** -0.5,
        "ffn_conv_w": n(ks[19], (DEPTH, FFN_CONV_WIDTH, D_FF), f32) * FFN_CONV_WIDTH ** -0.5,
        "ffn_conv_b": 0.02 * n(ks[20], (DEPTH, D_FF), f32),
        "w_down": n(ks[21], (DEPTH, D_FF, D_MODEL), f32) * D_FF ** -0.5,
    }


def reference(x_prompt, x_sample, cache_fox_k, cache_fox_v, cache_fox_logf, state_hgrn, state_ffn_conv,
              norm_mix_pre, norm_mix_post, w_in, fox_f_bias, hgrn_lb_logits, hgrn_norm,
              w_branch_fox, w_branch_hgrn, w_out, norm_ffn_pre, norm_ffn_post,
              w_up, ffn_conv_w, ffn_conv_b, w_down):
    xp, xs = x_prompt, x_sample
    lb_all = jnp.cumsum(jax.nn.softmax(hgrn_lb_logits.astype(jnp.float32), axis=0), axis=0)
    pk_l, pv_l, plf_l, ps_l, pc_l = [], [], [], [], []
    sk_l, sv_l, slf_l, ss_l, sc_l = [], [], [], [], []
    for l in range(DEPTH):
        lb = lb_all[l].reshape(HG_HEADS, HG_KEY_DIM)
        mix_w = (w_in[l], fox_f_bias[l], lb, hgrn_norm[l], w_branch_fox[l], w_branch_hgrn[l], w_out[l])
        ffn_w = (w_up[l], ffn_conv_w[l], ffn_conv_b[l], w_down[l])
        h = rmsnorm(xp, norm_mix_pre[l])
        mix, pk, pv, plf, pS = token_mixer(h, None, None, *mix_w)
        xp = xp + rmsnorm(mix, norm_mix_post[l])
        h = rmsnorm(xp, norm_ffn_pre[l])
        hist0 = jnp.zeros((xp.shape[0], FFN_CONV_WIDTH - 1, D_FF), xp.dtype)
        ff, pconv = conv_ffn(h, hist0, *ffn_w)
        xp = xp + rmsnorm(ff, norm_ffn_post[l])
        h = rmsnorm(xs, norm_mix_pre[l])
        mix, sk, sv, slf, sS = token_mixer(h, (cache_fox_k[l], cache_fox_v[l], cache_fox_logf[l]),
                                           state_hgrn[l], *mix_w)
        xs = xs + rmsnorm(mix, norm_mix_post[l])
        h = rmsnorm(xs, norm_ffn_pre[l])
        ff, sconv = conv_ffn(h, state_ffn_conv[l], *ffn_w)
        xs = xs + rmsnorm(ff, norm_ffn_post[l])
        pk_l.append(pk); pv_l.append(pv); plf_l.append(plf); ps_l.append(pS); pc_l.append(pconv)
        sk_l.append(sk); sv_l.append(sv); slf_l.append(slf); ss_l.append(sS); sc_l.append(sconv)
    p_fox_k = jnp.stack(pk_l); p_fox_v = jnp.stack(pv_l); p_fox_logf = jnp.stack(plf_l)
    p_hgrn = jnp.stack(ps_l); p_ffn_conv = jnp.stack(pc_l)
    s_fox_k = jnp.stack(sk_l); s_fox_v = jnp.stack(sv_l); s_fox_logf = jnp.stack(slf_l)
    s_hgrn = jnp.stack(ss_l); s_ffn_conv = jnp.stack(sc_l)
    return (xp, xs, p_fox_k, p_fox_v, p_fox_logf, p_hgrn, p_ffn_conv,
            s_fox_k, s_fox_v, s_fox_logf, s_hgrn, s_ffn_conv)
```

```python
import functools

import numpy as np
import jax
import jax.numpy as jnp
from jax import lax
from jax.experimental import pallas as pl
from jax.experimental.pallas import tpu as pltpu

F32 = jnp.float32
BF16 = jnp.bfloat16

D_MODEL = 1024
FOX_HEADS = 8
FOX_HEAD_DIM = 64
FOX_WIDTH = FOX_HEADS * FOX_HEAD_DIM
HG_HEADS = 4
HG_DIM = 128
HG_WIDTH = HG_HEADS * HG_DIM
D_FF = 2816
RMS_EPS = 1e-6
NEG_INF = -1e30
IN_SIZES = [FOX_WIDTH, FOX_WIDTH, FOX_WIDTH, FOX_HEADS, HG_WIDTH, HG_WIDTH, HG_WIDTH, HG_WIDTH, D_MODEL, D_MODEL]

LANES = 128
FOX_PAD = 2 * FOX_HEAD_DIM
HG_BLOCK = 16
VMEM_LIMIT = 56 * 1024 * 1024

PROJ_TM = 256
FOX_TQ = 512
FOX_TK = 512
HG_TC = 256
FFN_TM = 256
SAMPLE_TK = 1024

_C_Q, _C_K, _C_V, _C_F = 0, 512, 1024, 1536
_C_HQ, _C_HF, _C_HI, _C_HG = 1664, 2176, 2688, 3200
_C_GA, _C_GB, _C_END = 3712, 4736, 5760


def _split3(x):
    hi = x.astype(BF16)
    r = x - hi.astype(F32)
    mid = r.astype(BF16)
    lo = (r - mid.astype(F32)).astype(BF16)
    return hi, mid, lo


def _sum_by_01_matrix(mat01, x):
    cat = jnp.concatenate(_split3(x), axis=1)
    y = jnp.dot(mat01, cat, preferred_element_type=F32)
    return y[:, :LANES] + y[:, LANES:2 * LANES] + y[:, 2 * LANES:]


def _rms_scale(x):
    return x * lax.rsqrt(jnp.mean(x * x, axis=-1, keepdims=True) + RMS_EPS)


def _log_sigmoid(x):
    return jnp.minimum(x, 0.0) - jnp.log1p(jnp.exp(-jnp.abs(x)))


def _sigmoid(x):
    return 1.0 / (1.0 + jnp.exp(-x))


def _proj_kernel(*refs, fold):
    if fold:
        (x_ref, g_ref, w_ref, fb_ref, lbl_ref, tri_ref, pq_ref, pk_ref, cq_ref, ck_ref, cv_ref,
         qh_ref, kh_ref, vh_ref, kout_ref, vout_ref, lf_ref, hq_ref, hk_ref, lfh_ref, hi_ref, shg_ref,
         sga_ref, sgb_ref, carry_ref) = refs
    else:
        (x_ref, g_ref, w_ref, fb_ref, lbl_ref,
         qs_ref, kout_ref, vout_ref, lf_ref, hq_ref, hk_ref, lfh_ref, hi_ref, shg_ref,
         sga_ref, sgb_ref) = refs

    h = (_rms_scale(x_ref[...]) * g_ref[...]).astype(BF16)
    z = jnp.dot(h, w_ref[...], preferred_element_type=F32)

    zq = z[:, _C_Q:_C_K] * (FOX_HEAD_DIM ** -0.5)
    zk = z[:, _C_K:_C_V]
    zv = z[:, _C_V:_C_F]
    kout_ref[...] = zk
    vout_ref[...] = zv
    logf = _log_sigmoid(z[:, _C_F:_C_HQ] + fb_ref[...])
    lf_ref[...] = logf[:, :FOX_HEADS]

    l0 = lbl_ref[0:1, :]
    l1 = lbl_ref[1:2, :]
    lmax = jnp.maximum(l0, l1)
    e0 = jnp.exp(l0 - lmax)
    lb = e0 / (e0 + jnp.exp(l1 - lmax))
    f = lb + (1.0 - lb) * _sigmoid(z[:, _C_HF:_C_HI])
    hq_ref[...] = z[:, _C_HQ:_C_HF].astype(BF16)
    hk_ref[...] = (1.0 - f).astype(BF16)
    lfh_ref[...] = jnp.log(f)
    hi_ref[...] = z[:, _C_HI:_C_HG].astype(BF16)
    shg_ref[...] = _sigmoid(z[:, _C_HG:_C_GA]).astype(BF16)
    sga_ref[...] = _sigmoid(z[:, _C_GA:_C_GB]).astype(BF16)
    sgb_ref[...] = _sigmoid(z[:, _C_GB:_C_END]).astype(BF16)

    if not fold:
        qs_ref[...] = zq.astype(BF16)
        return

    @pl.when(pl.program_id(0) == 0)
    def _():
        carry_ref[...] = jnp.zeros_like(carry_ref)

    cum = carry_ref[...] + _sum_by_01_matrix(tri_ref[...], logf)
    carry_ref[...] = cum[-1:, :]
    c_hi, c_mid, c_lo = _split3(cum)
    lane = lax.broadcasted_iota(jnp.int32, cum.shape, 1)
    pieces = jnp.where(lane < 8, c_hi, jnp.where(lane < 16, c_mid, c_lo))
    pieces = jnp.where(lane < 24, pieces, jnp.zeros_like(pieces))
    ex_q = jnp.dot(pieces, pq_ref[...], preferred_element_type=F32) + cq_ref[...]
    ex_k = jnp.dot(pieces, pk_ref[...], preferred_element_type=F32) + ck_ref[...]
    ex_v = cv_ref[...]

    low = lax.broadcasted_iota(jnp.int32, (zq.shape[0], LANES), 1) < FOX_HEAD_DIM
    for src, ex, dst in ((zq, ex_q, qh_ref), (zk, ex_k, kh_ref), (zv, ex_v, vh_ref)):
        for c in range(FOX_WIDTH // LANES):
            pair = src[:, c * LANES:(c + 1) * LANES]
            swapped = pltpu.roll(pair, FOX_HEAD_DIM, axis=1)
            for j, data in enumerate((pair, swapped)):
                hd = 2 * c + j
                blk = jnp.where(low, data, ex[:, hd * LANES:(hd + 1) * LANES])
                dst[hd] = blk.astype(BF16)


def _bias_fold_constants():
    pq = np.zeros((LANES, FOX_HEADS * LANES), np.float32)
    pk = np.zeros((LANES, FOX_HEADS * LANES), np.float32)
    cq = np.zeros((1, FOX_HEADS * LANES), np.float32)
    ck = np.zeros((1, FOX_HEADS * LANES), np.float32)
    cv = np.zeros((1, FOX_HEADS * LANES), np.float32)
    for h in range(FOX_HEADS):
        base = h * LANES + FOX_HEAD_DIM
        for p in range(3):
            pq[p * 8 + h, base + p] = 1.0
            ck[0, base + p] = 1.0
            pk[p * 8 + h, base + 3 + p] = -1.0
            cq[0, base + 3 + p] = 1.0
        cv[0, base] = 1.0
    return (jnp.asarray(pq, BF16), jnp.asarray(pk, BF16), jnp.asarray(cq), jnp.asarray(ck), jnp.asarray(cv))


def _const_spec(shape, single=True):
    nd = len(shape)
    if single:
        return pl.BlockSpec(shape, lambda *_: (0,) * nd, pipeline_mode=pl.Buffered(1))
    return pl.BlockSpec(shape, lambda *_: (0,) * nd)


def _proj(x, gain, w_all, fb3, lb_logits, *, fold):
    rows = x.shape[0]
    tm = min(PROJ_TM, rows)
    n = rows // tm
    row = lambda width: pl.BlockSpec((tm, width), lambda i: (i, 0))
    in_specs = [row(D_MODEL), _const_spec((1, D_MODEL)), _const_spec(w_all.shape), _const_spec((1, LANES)),
                _const_spec(lb_logits.shape)]
    args = [x, gain, w_all, fb3, lb_logits]
    common_out = [
        (jax.ShapeDtypeStruct((rows, FOX_WIDTH), F32), row(FOX_WIDTH)),
        (jax.ShapeDtypeStruct((rows, FOX_WIDTH), F32), row(FOX_WIDTH)),
        (jax.ShapeDtypeStruct((rows, FOX_HEADS), F32), row(FOX_HEADS)),
        (jax.ShapeDtypeStruct((rows, HG_WIDTH), BF16), row(HG_WIDTH)),
        (jax.ShapeDtypeStruct((rows, HG_WIDTH), BF16), row(HG_WIDTH)),
        (jax.ShapeDtypeStruct((rows, HG_WIDTH), F32), row(HG_WIDTH)),
        (jax.ShapeDtypeStruct((rows, HG_WIDTH), BF16), row(HG_WIDTH)),
        (jax.ShapeDtypeStruct((rows, HG_WIDTH), BF16), row(HG_WIDTH)),
        (jax.ShapeDtypeStruct((rows, D_MODEL), BF16), row(D_MODEL)),
        (jax.ShapeDtypeStruct((rows, D_MODEL), BF16), row(D_MODEL)),
    ]
    scratch = []
    if fold:
        tri = jnp.asarray(np.tril(np.ones((tm, tm), np.float32)), BF16)
        consts = _bias_fold_constants()
        in_specs += [_const_spec(tri.shape)] + [_const_spec(c.shape) for c in consts]
        args += [tri, *consts]
        head_major = (jax.ShapeDtypeStruct((FOX_HEADS, rows, FOX_PAD), BF16),
                      pl.BlockSpec((FOX_HEADS, tm, FOX_PAD), lambda i: (0, i, 0)))
        outs = [head_major] * 3 + common_out
        scratch = [pltpu.VMEM((1, LANES), F32)]
    else:
        outs = [(jax.ShapeDtypeStruct((rows, FOX_WIDTH), BF16), row(FOX_WIDTH))] + common_out
    return pl.pallas_call(
        functools.partial(_proj_kernel, fold=fold),
        grid=(n,),
        in_specs=in_specs,
        out_specs=[o[1] for o in outs],
        out_shape=[o[0] for o in outs],
        scratch_shapes=scratch,
        compiler_params=pltpu.CompilerParams(dimension_semantics=("arbitrary",), vmem_limit_bytes=VMEM_LIMIT),
        name="proj_fold" if fold else "proj",
    )(*args)


def _fox_kernel(qi_ref, ki_ref, q_ref, k_ref, v_ref, o_ref, m_ref, acc_ref):
    step = pl.program_id(0)
    qi = qi_ref[step]
    ki = ki_ref[step]
    tq = q_ref.shape[1]
    tk = k_ref.shape[1]

    @pl.when(ki == 0)
    def _():
        m_ref[...] = jnp.full_like(m_ref, NEG_INF)
        acc_ref[...] = jnp.zeros_like(acc_ref)

    def head_step(h, masked):
        s = lax.dot_general(q_ref[h], k_ref[h], (((1,), (1,)), ((), ())), preferred_element_type=F32)
        if masked:
            rowi = lax.broadcasted_iota(jnp.int32, (tq, tk), 0)
            coli = lax.broadcasted_iota(jnp.int32, (tq, tk), 1)
            s = jnp.where(coli <= rowi, s, NEG_INF)
        m_prev = m_ref[h]
        m_new = jnp.maximum(m_prev, jnp.max(s, axis=1, keepdims=True))
        alpha = jnp.exp(m_prev - m_new)
        p = jnp.exp(s - m_new[:, :1])
        acc_ref[h] = alpha * acc_ref[h] + jnp.dot(p.astype(BF16), v_ref[h], preferred_element_type=F32)
        m_ref[h] = m_new

    @pl.when(ki < qi)
    def _():
        lax.fori_loop(0, FOX_HEADS, lambda h, c: (head_step(h, False), c)[1], 0)

    @pl.when(ki == qi)
    def _():
        lax.fori_loop(0, FOX_HEADS, lambda h, c: (head_step(h, True), c)[1], 0)
        for c in range(FOX_WIDTH // LANES):
            halves = []
            for hd in (2 * c, 2 * c + 1):
                acc = acc_ref[hd]
                denom = acc[:, FOX_HEAD_DIM:FOX_HEAD_DIM + 1]
                halves.append(acc[:, :FOX_HEAD_DIM] / denom)
            o_ref[:, c * LANES:(c + 1) * LANES] = jnp.concatenate(halves, axis=1).astype(o_ref.dtype)


def _fox_prompt(qh, kh, vh):
    seq = qh.shape[1]
    assert FOX_TQ == FOX_TK
    nq = seq // FOX_TQ
    qi = np.concatenate([np.full(i + 1, i) for i in range(nq)]).astype(np.int32)
    ki = np.concatenate([np.arange(i + 1) for i in range(nq)]).astype(np.int32)
    grid_spec = pltpu.PrefetchScalarGridSpec(
        num_scalar_prefetch=2,
        grid=(len(qi),),
        in_specs=[
            pl.BlockSpec((FOX_HEADS, FOX_TQ, FOX_PAD), lambda s, qi, ki: (0, qi[s], 0)),
            pl.BlockSpec((FOX_HEADS, FOX_TK, FOX_PAD), lambda s, qi, ki: (0, ki[s], 0)),
            pl.BlockSpec((FOX_HEADS, FOX_TK, FOX_PAD), lambda s, qi, ki: (0, ki[s], 0)),
        ],
        out_specs=pl.BlockSpec((FOX_TQ, FOX_WIDTH), lambda s, qi, ki: (qi[s], 0)),
        scratch_shapes=[pltpu.VMEM((FOX_HEADS, FOX_TQ, LANES), F32),
                        pltpu.VMEM((FOX_HEADS, FOX_TQ, FOX_PAD), F32)],
    )
    return pl.pallas_call(
        _fox_kernel,
        grid_spec=grid_spec,
        out_shape=jax.ShapeDtypeStruct((seq, FOX_WIDTH), BF16),
        compiler_params=pltpu.CompilerParams(dimension_semantics=("arbitrary",), vmem_limit_bytes=VMEM_LIMIT),
        name="fox_prompt",
    )(jnp.asarray(qi), jnp.asarray(ki), qh, kh, vh)


def _lane_cumsum(x):
    n = x.shape[1]
    lane = lax.broadcasted_iota(jnp.int32, x.shape, 1)
    shift = 1
    while shift < n:
        x = x + jnp.where(lane >= shift, pltpu.roll(x, shift, axis=1), 0.0)
        shift *= 2
    return x


def _fox_sample_kernel(q_ref, kn_ref, vn_ref, lft_ref, ck_ref, cv_ref, o_ref,
                       qbd_ref, cq_ref, cum_ref, m_ref, l_ref, acc_ref, *, n_past_chunks, tk, t_new):
    c = pl.program_id(1)
    rows = FOX_HEADS * t_new
    lane_head = lax.broadcasted_iota(jnp.int32, (t_new, FOX_WIDTH), 1) // FOX_HEAD_DIM

    @pl.when(c == 0)
    def _():
        cum = _lane_cumsum(lft_ref[0])
        for j in range(n_past_chunks):
            cum_ref[j] = cum[:, j * tk:(j + 1) * tk]
        new_cum = cum[:, n_past_chunks * tk:n_past_chunks * tk + LANES]
        cum_ref[n_past_chunks, :, :LANES] = new_cum
        new_cum_t = jnp.concatenate([new_cum] * (LANES // FOX_HEADS), axis=0).T
        q = q_ref[...]
        for h in range(FOX_HEADS):
            qbd_ref[h * t_new:(h + 1) * t_new, :] = jnp.where(lane_head == h, q, jnp.zeros_like(q))
            cq_ref[h * t_new:(h + 1) * t_new, :] = jnp.broadcast_to(new_cum_t[:t_new, h:h + 1], (t_new, LANES))
        m_ref[...] = jnp.full_like(m_ref, NEG_INF)
        l_ref[...] = jnp.zeros_like(l_ref)
        acc_ref[...] = jnp.zeros_like(acc_ref)

    def update(k, v, ck_rows, mask):
        s = lax.dot_general(qbd_ref[...], k, (((1,), (1,)), ((), ())), preferred_element_type=F32)
        s = s + (cq_ref[:, :1] - ck_rows)
        if mask is not None:
            s = jnp.where(mask, s, NEG_INF)
        m_prev = m_ref[...]
        m_new = jnp.maximum(m_prev, jnp.max(s, axis=1, keepdims=True))
        alpha = jnp.exp(m_prev - m_new)
        p = jnp.exp(s - m_new[:, :1])
        l_ref[...] = alpha * l_ref[...] + jnp.sum(p, axis=1, keepdims=True)
        acc_ref[...] = alpha[:, :1] * acc_ref[...] + jnp.dot(p.astype(BF16), v, preferred_element_type=F32)
        m_ref[...] = m_new

    def expand_rows(x, width):
        return jnp.concatenate([jnp.broadcast_to(x[h:h + 1, :], (t_new, width)) for h in range(FOX_HEADS)], axis=0)

    @pl.when(c < n_past_chunks)
    def _():
        update(ck_ref[0].astype(BF16), cv_ref[0].astype(BF16), expand_rows(cum_ref[c], tk), None)

    @pl.when(c == n_past_chunks)
    def _():
        ck_rows = expand_rows(cum_ref[n_past_chunks, :, :LANES], LANES)[:, :t_new]
        rowt = lax.broadcasted_iota(jnp.int32, (rows, t_new), 0) % t_new
        coli = lax.broadcasted_iota(jnp.int32, (rows, t_new), 1)
        update(kn_ref[...].astype(BF16), vn_ref[...].astype(BF16), ck_rows, coli <= rowt)
        out = acc_ref[...] / l_ref[:, :1]
        o = jnp.zeros((t_new, FOX_WIDTH), F32)
        for h in range(FOX_HEADS):
            o = o + jnp.where(lane_head == h, out[h * t_new:(h + 1) * t_new, :], 0.0)
        o_ref[...] = o.astype(o_ref.dtype)


def _fox_sample(qs, k_new, v_new, lf_all_t, cache_k, cache_v, *, t_new):
    nb, past = cache_k.shape[0], cache_k.shape[1]
    tk = SAMPLE_TK
    npc = past // tk
    rows = FOX_HEADS * t_new
    last = npc - 1
    kern = functools.partial(_fox_sample_kernel, n_past_chunks=npc, tk=tk, t_new=t_new)
    return pl.pallas_call(
        kern,
        grid=(nb, npc + 1),
        in_specs=[
            pl.BlockSpec((t_new, FOX_WIDTH), lambda b, c: (b, 0)),
            pl.BlockSpec((t_new, FOX_WIDTH), lambda b, c: (b, 0)),
            pl.BlockSpec((t_new, FOX_WIDTH), lambda b, c: (b, 0)),
            pl.BlockSpec((1, FOX_HEADS, past + LANES), lambda b, c: (b, 0, 0)),
            pl.BlockSpec((1, tk, FOX_WIDTH), lambda b, c: (b, jnp.minimum(c, last), 0)),
            pl.BlockSpec((1, tk, FOX_WIDTH), lambda b, c: (b, jnp.minimum(c, last), 0)),
        ],
        out_specs=pl.BlockSpec((t_new, FOX_WIDTH), lambda b, c: (b, 0)),
        out_shape=jax.ShapeDtypeStruct((nb * t_new, FOX_WIDTH), BF16),
        scratch_shapes=[
            pltpu.VMEM((rows, FOX_WIDTH), BF16),
            pltpu.VMEM((rows, LANES), F32),
            pltpu.VMEM((npc + 1, FOX_HEADS, tk), F32),
            pltpu.VMEM((rows, LANES), F32),
            pltpu.VMEM((rows, LANES), F32),
            pltpu.VMEM((rows, FOX_WIDTH), F32),
        ],
        compiler_params=pltpu.CompilerParams(dimension_semantics=("arbitrary", "arbitrary"),
                                             vmem_limit_bytes=VMEM_LIMIT),
        name="fox_sample",
    )(qs, k_new, v_new, lf_all_t, cache_k, cache_v)


def _hgrn_kernel(hq_ref, hk_ref, lfh_ref, hi_ref, shg_ref, s0_ref, norm_ref, tri_ref, ones_ref,
                 o_ref, sout_ref, st_ref, qq_ref, kk_ref, eb_ref, oin_ref, od_ref):
    t = pl.program_id(1)
    tc = hq_ref.shape[0]
    blk = min(HG_BLOCK, tc)

    @pl.when(t == 0)
    def _():
        for h in range(HG_HEADS):
            st_ref[h] = s0_ref[0, h].T

    row_in_blk = lax.broadcasted_iota(jnp.int32, (tc, HG_DIM), 0) % blk
    for h in range(HG_HEADS):
        sl = slice(h * HG_DIM, (h + 1) * HG_DIM)
        lf = lfh_ref[:, sl]
        b = _sum_by_01_matrix(tri_ref[...], lf)
        b_end = _sum_by_01_matrix(ones_ref[...], lf)
        eb = jnp.exp(b)
        q = hq_ref[:, sl].astype(F32)
        k = hk_ref[:, sl].astype(F32)
        v = hi_ref[:, sl].astype(F32)
        qq_ref[:, sl] = (q * eb).astype(BF16)
        kk_ref[:, sl] = (k * jnp.exp(b_end - b)).astype(BF16)
        eb_ref[:, sl] = eb
        od = jnp.sum(q * k, axis=1, keepdims=True) * v
        for lag in range(1, blk):
            k_l = pltpu.roll(k, lag, axis=0)
            b_l = pltpu.roll(b, lag, axis=0)
            v_l = pltpu.roll(v, lag, axis=0)
            w = q * k_l * jnp.exp(jnp.minimum(b - b_l, 0.0))
            w = jnp.where(row_in_blk >= lag, w, 0.0)
            od = od + jnp.sum(w, axis=1, keepdims=True) * v_l
        od_ref[:, sl] = od

    def block_step(j, carry):
        r0 = pl.multiple_of(j * blk, blk)
        for h in range(HG_HEADS):
            sl = slice(h * HG_DIM, (h + 1) * HG_DIM)
            st = st_ref[h]
            oin_ref[pl.ds(r0, blk), sl] = lax.dot_general(
                qq_ref[pl.ds(r0, blk), sl], st.astype(BF16), (((1,), (1,)), ((), ())), preferred_element_type=F32)
            upd = lax.dot_general(hi_ref[pl.ds(r0, blk), sl], kk_ref[pl.ds(r0, blk), sl],
                                  (((0,), (0,)), ((), ())), preferred_element_type=F32)
            tail = eb_ref[pl.ds(pl.multiple_of(r0 + blk - 8, 8), 8), sl]
            st_ref[h] = st * tail[7:8, :] + upd
        return carry

    lax.fori_loop(0, tc // blk, block_step, 0)

    for h in range(HG_HEADS):
        sl = slice(h * HG_DIM, (h + 1) * HG_DIM)
        o = oin_ref[:, sl] + od_ref[:, sl]
        y = (_rms_scale(o) * norm_ref[...]) * shg_ref[:, sl].astype(F32)
        o_ref[:, sl] = y.astype(o_ref.dtype)

    @pl.when(t == pl.num_programs(1) - 1)
    def _():
        for h in range(HG_HEADS):
            sout_ref[0, h] = st_ref[h].T


def _block_diag_01(n, blk, lower):
    r = np.arange(n)
    same = (r[:, None] // blk) == (r[None, :] // blk)
    if lower:
        same = same & (r[None, :] <= r[:, None])
    return jnp.asarray(same.astype(np.float32), BF16)


def _hgrn(hq, hk, lfh, hi, shg, s0, norm, *, nseq):
    rows = hq.shape[0]
    t_len = rows // nseq
    tc = min(HG_TC, t_len)
    nt = t_len // tc
    blk = min(HG_BLOCK, tc)
    tri = _block_diag_01(tc, blk, True)
    ones = _block_diag_01(tc, blk, False)
    row = pl.BlockSpec((tc, HG_WIDTH), lambda b, t: (b * nt + t, 0))
    state = pl.BlockSpec((1, HG_HEADS, HG_DIM, HG_DIM), lambda b, t: (b, 0, 0, 0))
    return pl.pallas_call(
        _hgrn_kernel,
        grid=(nseq, nt),
        in_specs=[row, row, row, row, row, state,
                  pl.BlockSpec((1, HG_DIM), lambda b, t: (0, 0)),
                  pl.BlockSpec((tc, tc), lambda b, t: (0, 0)),
                  pl.BlockSpec((tc, tc), lambda b, t: (0, 0))],
        out_specs=[row, state],
        out_shape=[jax.ShapeDtypeStruct((rows, HG_WIDTH), BF16),
                   jax.ShapeDtypeStruct((nseq, HG_HEADS, HG_DIM, HG_DIM), F32)],
        scratch_shapes=[
            pltpu.VMEM((HG_HEADS, HG_DIM, HG_DIM), F32),
            pltpu.VMEM((tc, HG_WIDTH), BF16),
            pltpu.VMEM((tc, HG_WIDTH), BF16),
            pltpu.VMEM((tc, HG_WIDTH), F32),
            pltpu.VMEM((tc, HG_WIDTH), F32),
            pltpu.VMEM((tc, HG_WIDTH), F32),
        ],
        compiler_params=pltpu.CompilerParams(dimension_semantics=("arbitrary", "arbitrary"),
                                             vmem_limit_bytes=VMEM_LIMIT),
        name="hgrn",
    )(hq, hk, lfh, hi, shg, s0, norm, tri, ones)


def _mixffn_kernel(x_ref, of_ref, oh_ref, sga_ref, sgb_ref, hist_ref,
                   wbf_ref, wbh_ref, wout_ref, wup_ref, wdn_ref,
                   npost_ref, npre2_ref, npost2_ref, cw_ref, cb_ref,
                   y_ref, conv_ref, tail_ref, *, seg_len):
    tm = x_ref.shape[0]
    br_f = jnp.dot(of_ref[...], wbf_ref[...], preferred_element_type=F32)
    br_h = jnp.dot(oh_ref[...], wbh_ref[...], preferred_element_type=F32)
    merged = sga_ref[...].astype(F32) * br_f + sgb_ref[...].astype(F32) * br_h
    mix = jnp.dot(merged.astype(BF16), wout_ref[...], preferred_element_type=F32)
    x1 = x_ref[...] + _rms_scale(mix) * npost_ref[...]

    h2 = (_rms_scale(x1) * npre2_ref[...]).astype(BF16)
    up = jnp.dot(h2, wup_ref[...], preferred_element_type=F32)
    a = up[:, :D_FF]
    g = up[:, D_FF:]

    prev1 = pltpu.roll(a, 1, axis=0)
    prev2 = pltpu.roll(a, 2, axis=0)
    rowi = lax.broadcasted_iota(jnp.int32, a.shape, 0)
    if seg_len >= tm:
        @pl.when(pl.program_id(0) == 0)
        def _():
            tail_ref[...] = hist_ref[0]
        t0 = tail_ref[0:1, :]
        t1 = tail_ref[1:2, :]
        prev1 = jnp.where(rowi == 0, t1, prev1)
        prev2 = jnp.where(rowi == 0, t0, jnp.where(rowi == 1, t1, prev2))
        tail_ref[...] = a[tm - 2:, :]
        conv_ref[0] = a[tm - 2:, :]
    else:
        for s in range(tm // seg_len):
            h0 = hist_ref[s, 0:1, :]
            h1 = hist_ref[s, 1:2, :]
            prev1 = jnp.where(rowi == s * seg_len, h1, prev1)
            prev2 = jnp.where(rowi == s * seg_len, h0, jnp.where(rowi == s * seg_len + 1, h1, prev2))
            conv_ref[s] = a[(s + 1) * seg_len - 2:(s + 1) * seg_len, :]
    c = cb_ref[...] + cw_ref[0:1, :] * prev2 + cw_ref[1:2, :] * prev1 + cw_ref[2:3, :] * a
    act = (jax.nn.gelu(c, approximate=True) * g).astype(BF16)
    ff = jnp.dot(act, wdn_ref[...], preferred_element_type=F32)
    y_ref[...] = x1 + _rms_scale(ff) * npost2_ref[...]


def _mixffn(x, o_fox, o_hg, sga, sgb, hist, w, *, seg_len):
    rows = x.shape[0]
    tm = min(FFN_TM, rows)
    n = rows // tm
    nseg = hist.shape[0]
    row = lambda width: pl.BlockSpec((tm, width), lambda i: (i, 0))
    weights = [w["bf"], w["bh"], w["out"], w["up"], w["down"]]
    smalls = [w["npost"], w["npre2"], w["npost2"], w["conv_w"], w["conv_b"]]
    hist_spec = pl.BlockSpec(hist.shape, lambda i: (0, 0, 0))
    scratch = [pltpu.VMEM((2, D_FF), F32)]
    return pl.pallas_call(
        functools.partial(_mixffn_kernel, seg_len=seg_len),
        grid=(n,),
        in_specs=[row(D_MODEL), row(FOX_WIDTH), row(HG_WIDTH), row(D_MODEL), row(D_MODEL), hist_spec]
                 + [_const_spec(a.shape) for a in weights] + [_const_spec(a.shape) for a in smalls],
        out_specs=[row(D_MODEL), pl.BlockSpec((nseg, 2, D_FF), lambda i: (0, 0, 0))],
        out_shape=[jax.ShapeDtypeStruct((rows, D_MODEL), F32), jax.ShapeDtypeStruct((nseg, 2, D_FF), F32)],
        scratch_shapes=scratch,
        compiler_params=pltpu.CompilerParams(dimension_semantics=("arbitrary",), vmem_limit_bytes=VMEM_LIMIT),
        name="mixffn",
    )(x, o_fox, o_hg, sga, sgb, hist, *weights, *smalls)


def _prep_w_in(w_in, fox_f_bias):
    offs = np.cumsum([0] + IN_SIZES)
    seg = [w_in[:, int(offs[i]):int(offs[i + 1])] for i in range(len(IN_SIZES))]
    pad = jnp.zeros((D_MODEL, LANES - 3 * FOX_HEADS), w_in.dtype)
    f3 = jnp.concatenate([seg[3], seg[3], seg[3], pad], axis=1)
    w_all = jnp.concatenate(seg[:3] + [f3] + seg[4:], axis=1).astype(BF16)
    fb = fox_f_bias.astype(F32)
    fb3 = jnp.concatenate([fb, fb, fb, jnp.zeros((LANES - 3 * FOX_HEADS,), F32)]).reshape(1, LANES)
    return w_all, fb3


def kernel(x_prompt, x_sample, cache_fox_k, cache_fox_v, cache_fox_logf, state_hgrn, state_ffn_conv, norm_mix_pre, norm_mix_post, w_in, fox_f_bias, hgrn_lb_logits, hgrn_norm, w_branch_fox, w_branch_hgrn, w_out, norm_ffn_pre, norm_ffn_post, w_up, ffn_conv_w, ffn_conv_b, w_down):
    depth = w_in.shape[0]
    assert depth == 1 and hgrn_lb_logits.shape[0] == 2
    bp, seq, _ = x_prompt.shape
    assert bp == 1
    nb, t_new, _ = x_sample.shape
    past = cache_fox_k.shape[2]

    w_all, fb3 = _prep_w_in(w_in[0], fox_f_bias[0])
    g_pre = norm_mix_pre[0].reshape(1, D_MODEL)
    lbl = hgrn_lb_logits.astype(F32)
    hnorm = hgrn_norm[0].astype(F32).reshape(1, HG_DIM)
    w = {
        "bf": w_branch_fox[0].astype(BF16), "bh": w_branch_hgrn[0].astype(BF16), "out": w_out[0].astype(BF16),
        "up": w_up[0].astype(BF16), "down": w_down[0].astype(BF16),
        "npost": norm_mix_post[0].reshape(1, D_MODEL), "npre2": norm_ffn_pre[0].reshape(1, D_MODEL),
        "npost2": norm_ffn_post[0].reshape(1, D_MODEL),
        "conv_w": ffn_conv_w[0], "conv_b": ffn_conv_b[0].reshape(1, D_FF),
    }

    xp = x_prompt.reshape(seq, D_MODEL)
    (qh, kh, vh, pk, pv, plf, hq, hk, lfh, hi, shg, sga, sgb) = _proj(xp, g_pre, w_all, fb3, lbl, fold=True)
    o_fox = _fox_prompt(qh, kh, vh)
    s0 = jnp.zeros((1, HG_HEADS, HG_DIM, HG_DIM), F32)
    o_hg, p_state = _hgrn(hq, hk, lfh, hi, shg, s0, hnorm, nseq=1)
    hist0 = jnp.zeros((1, 2, D_FF), F32)
    yp, pconv = _mixffn(xp, o_fox, o_hg, sga, sgb, hist0, w, seg_len=seq)

    xs = x_sample.reshape(nb * t_new, D_MODEL)
    (qs, sk, sv, slf, hq, hk, lfh, hi, shg, sga, sgb) = _proj(xs, g_pre, w_all, fb3, lbl, fold=False)
    lf_all_t = jnp.concatenate([
        jnp.swapaxes(cache_fox_logf[0].astype(F32), 1, 2),
        jnp.swapaxes(slf.reshape(nb, t_new, FOX_HEADS), 1, 2),
        jnp.zeros((nb, FOX_HEADS, LANES - t_new), F32)], axis=2)
    o_fox_s = _fox_sample(qs, sk, sv, lf_all_t,
                          cache_fox_k[0].reshape(nb, past, FOX_WIDTH), cache_fox_v[0].reshape(nb, past, FOX_WIDTH),
                          t_new=t_new)
    o_hg_s, s_state = _hgrn(hq, hk, lfh, hi, shg, state_hgrn[0].astype(F32), hnorm, nseq=nb)
    ys, sconv = _mixffn(xs, o_fox_s, o_hg_s, sga, sgb, state_ffn_conv[0], w, seg_len=t_new)

    return (
        yp.reshape(bp, seq, D_MODEL),
        ys.reshape(nb, t_new, D_MODEL),
        pk.reshape(1, bp, seq, FOX_HEADS, FOX_HEAD_DIM),
        pv.reshape(1, bp, seq, FOX_HEADS, FOX_HEAD_DIM),
        plf.reshape(1, bp, seq, FOX_HEADS),
        p_state.reshape(1, bp, HG_HEADS, HG_DIM, HG_DIM),
        pconv.reshape(1, bp, 2, D_FF),
        sk.reshape(1, nb, t_new, FOX_HEADS, FOX_HEAD_DIM),
        sv.reshape(1, nb, t_new, FOX_HEADS, FOX_HEAD_DIM),
        slf.reshape(1, nb, t_new, FOX_HEADS),
        s_state.reshape(1, nb, HG_HEADS, HG_DIM, HG_DIM),
        sconv.reshape(1, nb, 2, D_FF),
    )
```

```python
import functools

import numpy as np
import jax
import jax.numpy as jnp
from jax import lax
from jax.experimental import pallas as pl
from jax.experimental.pallas import tpu as pltpu

F32 = jnp.float32
BF16 = jnp.bfloat16

D_MODEL = 1024
FOX_HEADS = 8
FOX_HEAD_DIM = 64
FOX_WIDTH = FOX_HEADS * FOX_HEAD_DIM
HG_HEADS = 4
HG_DIM = 128
HG_WIDTH = HG_HEADS * HG_DIM
D_FF = 2816
RMS_EPS = 1e-6
NEG_INF = -1e30
IN_SIZES = [FOX_WIDTH, FOX_WIDTH, FOX_WIDTH, FOX_HEADS, HG_WIDTH, HG_WIDTH, HG_WIDTH, HG_WIDTH, D_MODEL, D_MODEL]

LANES = 128
FOX_PAD = 2 * FOX_HEAD_DIM
HG_BLOCK = 16
VMEM_LIMIT = 56 * 1024 * 1024

PROJ_TM = 256
FOX_TQ = 512
FOX_TK = 512
HG_TC = 256
FFN_TM = 256
SAMPLE_TK = 1024

_C_Q, _C_K, _C_V, _C_F = 0, 512, 1024, 1536
_C_HQ, _C_HF, _C_HI, _C_HG = 1664, 2176, 2688, 3200
_C_GA, _C_GB, _C_END = 3712, 4736, 5760


def _split3(x):
    hi = x.astype(BF16)
    r = x - hi.astype(F32)
    mid = r.astype(BF16)
    lo = (r - mid.astype(F32)).astype(BF16)
    return hi, mid, lo


def _sum_by_01_matrix(mat01, x):
    cat = jnp.concatenate(_split3(x), axis=1)
    y = jnp.dot(mat01, cat, preferred_element_type=F32)
    return y[:, :LANES] + y[:, LANES:2 * LANES] + y[:, 2 * LANES:]


def _rms_scale(x):
    return x * lax.rsqrt(jnp.mean(x * x, axis=-1, keepdims=True) + RMS_EPS)


def _log_sigmoid(x):
    return jnp.minimum(x, 0.0) - jnp.log1p(jnp.exp(-jnp.abs(x)))


def _sigmoid(x):
    return 1.0 / (1.0 + jnp.exp(-x))


def _proj_kernel(*refs, fold):
    if fold:
        (x_ref, g_ref, w_ref, fb_ref, lbl_ref, tri_ref, pq_ref, pk_ref, cq_ref, ck_ref, cv_ref,
         qh_ref, kh_ref, vh_ref, kout_ref, vout_ref, lf_ref, hq_ref, hk_ref, lfh_ref, hi_ref, shg_ref,
         sga_ref, sgb_ref, carry_ref) = refs
    else:
        (x_ref, g_ref, w_ref, fb_ref, lbl_ref,
         qs_ref, kout_ref, vout_ref, lf_ref, hq_ref, hk_ref, lfh_ref, hi_ref, shg_ref,
         sga_ref, sgb_ref) = refs

    h = (_rms_scale(x_ref[...]) * g_ref[...]).astype(BF16)
    z = jnp.dot(h, w_ref[...], preferred_element_type=F32)

    zq = z[:, _C_Q:_C_K] * (FOX_HEAD_DIM ** -0.5)
    zk = z[:, _C_K:_C_V]
    zv = z[:, _C_V:_C_F]
    kout_ref[...] = zk
    vout_ref[...] = zv
    logf = _log_sigmoid(z[:, _C_F:_C_HQ] + fb_ref[...])
    lf_ref[...] = logf[:, :FOX_HEADS]

    l0 = lbl_ref[0:1, :]
    l1 = lbl_ref[1:2, :]
    lmax = jnp.maximum(l0, l1)
    e0 = jnp.exp(l0 - lmax)
    lb = e0 / (e0 + jnp.exp(l1 - lmax))
    f = lb + (1.0 - lb) * _sigmoid(z[:, _C_HF:_C_HI])
    hq_ref[...] = z[:, _C_HQ:_C_HF].astype(BF16)
    hk_ref[...] = (1.0 - f).astype(BF16)
    lfh_ref[...] = jnp.log(f)
    hi_ref[...] = z[:, _C_HI:_C_HG].astype(BF16)
    shg_ref[...] = _sigmoid(z[:, _C_HG:_C_GA]).astype(BF16)
    sga_ref[...] = _sigmoid(z[:, _C_GA:_C_GB]).astype(BF16)
    sgb_ref[...] = _sigmoid(z[:, _C_GB:_C_END]).astype(BF16)

    if not fold:
        qs_ref[...] = zq.astype(BF16)
        return

    @pl.when(pl.program_id(0) == 0)
    def _():
        carry_ref[...] = jnp.zeros_like(carry_ref)

    cum = carry_ref[...] + _sum_by_01_matrix(tri_ref[...], logf)
    carry_ref[...] = cum[-1:, :]
    c_hi, c_mid, c_lo = _split3(cum)
    lane = lax.broadcasted_iota(jnp.int32, cum.shape, 1)
    pieces = jnp.where(lane < 8, c_hi, jnp.where(lane < 16, c_mid, c_lo))
    pieces = jnp.where(lane < 24, pieces, jnp.zeros_like(pieces))
    ex_q = jnp.dot(pieces, pq_ref[...], preferred_element_type=F32) + cq_ref[...]
    ex_k = jnp.dot(pieces, pk_ref[...], preferred_element_type=F32) + ck_ref[...]
    ex_v = cv_ref[...]

    low = lax.broadcasted_iota(jnp.int32, (zq.shape[0], LANES), 1) < FOX_HEAD_DIM
    for src, ex, dst, transposed in ((zq, ex_q, qh_ref, True), (zk, ex_k, kh_ref, False), (zv, ex_v, vh_ref, True)):
        for c in range(FOX_WIDTH // LANES):
            pair = src[:, c * LANES:(c + 1) * LANES]
            swapped = pltpu.roll(pair, FOX_HEAD_DIM, axis=1)
            for j, data in enumerate((pair, swapped)):
                hd = 2 * c + j
                blk = jnp.where(low, data, ex[:, hd * LANES:(hd + 1) * LANES])
                dst[hd] = (blk.T if transposed else blk).astype(BF16)


def _bias_fold_constants():
    pq = np.zeros((LANES, FOX_HEADS * LANES), np.float32)
    pk = np.zeros((LANES, FOX_HEADS * LANES), np.float32)
    cq = np.zeros((1, FOX_HEADS * LANES), np.float32)
    ck = np.zeros((1, FOX_HEADS * LANES), np.float32)
    cv = np.zeros((1, FOX_HEADS * LANES), np.float32)
    for h in range(FOX_HEADS):
        base = h * LANES + FOX_HEAD_DIM
        for p in range(3):
            pq[p * 8 + h, base + p] = 1.0
            ck[0, base + p] = 1.0
            pk[p * 8 + h, base + 3 + p] = -1.0
            cq[0, base + 3 + p] = 1.0
        cv[0, base] = 1.0
    return (jnp.asarray(pq, BF16), jnp.asarray(pk, BF16), jnp.asarray(cq), jnp.asarray(ck), jnp.asarray(cv))


def _const_spec(shape, single=True):
    nd = len(shape)
    if single:
        return pl.BlockSpec(shape, lambda *_: (0,) * nd, pipeline_mode=pl.Buffered(1))
    return pl.BlockSpec(shape, lambda *_: (0,) * nd)


def _proj(x, gain, w_all, fb3, lb_logits, *, fold):
    rows = x.shape[0]
    tm = min(PROJ_TM, rows)
    n = rows // tm
    row = lambda width: pl.BlockSpec((tm, width), lambda i: (i, 0))
    in_specs = [row(D_MODEL), _const_spec((1, D_MODEL)), _const_spec(w_all.shape), _const_spec((1, LANES)),
                _const_spec(lb_logits.shape)]
    args = [x, gain, w_all, fb3, lb_logits]
    common_out = [
        (jax.ShapeDtypeStruct((rows, FOX_WIDTH), F32), row(FOX_WIDTH)),
        (jax.ShapeDtypeStruct((rows, FOX_WIDTH), F32), row(FOX_WIDTH)),
        (jax.ShapeDtypeStruct((rows, FOX_HEADS), F32), row(FOX_HEADS)),
        (jax.ShapeDtypeStruct((rows, HG_WIDTH), BF16), row(HG_WIDTH)),
        (jax.ShapeDtypeStruct((rows, HG_WIDTH), BF16), row(HG_WIDTH)),
        (jax.ShapeDtypeStruct((rows, HG_WIDTH), F32), row(HG_WIDTH)),
        (jax.ShapeDtypeStruct((rows, HG_WIDTH), BF16), row(HG_WIDTH)),
        (jax.ShapeDtypeStruct((rows, HG_WIDTH), BF16), row(HG_WIDTH)),
        (jax.ShapeDtypeStruct((rows, D_MODEL), BF16), row(D_MODEL)),
        (jax.ShapeDtypeStruct((rows, D_MODEL), BF16), row(D_MODEL)),
    ]
    scratch = []
    if fold:
        tri = jnp.asarray(np.tril(np.ones((tm, tm), np.float32)), BF16)
        consts = _bias_fold_constants()
        in_specs += [_const_spec(tri.shape)] + [_const_spec(c.shape) for c in consts]
        args += [tri, *consts]
        head_major = (jax.ShapeDtypeStruct((FOX_HEADS, rows, FOX_PAD), BF16),
                      pl.BlockSpec((FOX_HEADS, tm, FOX_PAD), lambda i: (0, i, 0)))
        head_major_t = (jax.ShapeDtypeStruct((FOX_HEADS, FOX_PAD, rows), BF16),
                        pl.BlockSpec((FOX_HEADS, FOX_PAD, tm), lambda i: (0, 0, i)))
        outs = [head_major_t, head_major, head_major_t] + common_out
        scratch = [pltpu.VMEM((1, LANES), F32)]
    else:
        outs = [(jax.ShapeDtypeStruct((rows, FOX_WIDTH), BF16), row(FOX_WIDTH))] + common_out
    return pl.pallas_call(
        functools.partial(_proj_kernel, fold=fold),
        grid=(n,),
        in_specs=in_specs,
        out_specs=[o[1] for o in outs],
        out_shape=[o[0] for o in outs],
        scratch_shapes=scratch,
        compiler_params=pltpu.CompilerParams(dimension_semantics=("arbitrary",), vmem_limit_bytes=VMEM_LIMIT),
        name="proj_fold" if fold else "proj",
    )(*args)


def _fox_kernel(qi_ref, ki_ref, qt_ref, k_ref, vt_ref, o_ref, m_ref, acc_ref):
    step = pl.program_id(0)
    qi = qi_ref[step]
    ki = ki_ref[step]
    tq = qt_ref.shape[2]
    tk = k_ref.shape[1]

    @pl.when(ki == 0)
    def _():
        m_ref[...] = jnp.full_like(m_ref, NEG_INF)
        acc_ref[...] = jnp.zeros_like(acc_ref)

    def head_step(h, masked):
        s = jnp.dot(k_ref[h], qt_ref[h], preferred_element_type=F32)
        if masked:
            key = lax.broadcasted_iota(jnp.int32, (tk, tq), 0)
            qry = lax.broadcasted_iota(jnp.int32, (tk, tq), 1)
            s = jnp.where(key <= qry, s, NEG_INF)
        m_prev = m_ref[h]
        m_new = jnp.maximum(m_prev, jnp.max(s, axis=0, keepdims=True))
        alpha = jnp.exp(m_prev - m_new)
        p = jnp.exp(s - m_new).astype(BF16)
        acc_ref[h] = alpha * acc_ref[h] + jnp.dot(vt_ref[h], p, preferred_element_type=F32)
        m_ref[h] = m_new

    @pl.when(ki < qi)
    def _():
        for h in range(FOX_HEADS):
            head_step(h, False)

    @pl.when(ki == qi)
    def _():
        for h in range(FOX_HEADS):
            head_step(h, True)
        for c in range(FOX_WIDTH // LANES):
            halves = []
            for hd in (2 * c, 2 * c + 1):
                acc = acc_ref[hd]
                halves.append(acc[:FOX_HEAD_DIM, :] / acc[FOX_HEAD_DIM:FOX_HEAD_DIM + 1, :])
            o_ref[:, c * LANES:(c + 1) * LANES] = jnp.concatenate(halves, axis=0).T.astype(o_ref.dtype)


def _fox_prompt(qt, kh, vt):
    seq = kh.shape[1]
    assert FOX_TQ == FOX_TK
    nq = seq // FOX_TQ
    qi = np.concatenate([np.full(i + 1, i) for i in range(nq)]).astype(np.int32)
    ki = np.concatenate([np.arange(i + 1) for i in range(nq)]).astype(np.int32)
    grid_spec = pltpu.PrefetchScalarGridSpec(
        num_scalar_prefetch=2,
        grid=(len(qi),),
        in_specs=[
            pl.BlockSpec((FOX_HEADS, FOX_PAD, FOX_TQ), lambda s, qi, ki: (0, 0, qi[s])),
            pl.BlockSpec((FOX_HEADS, FOX_TK, FOX_PAD), lambda s, qi, ki: (0, ki[s], 0)),
            pl.BlockSpec((FOX_HEADS, FOX_PAD, FOX_TK), lambda s, qi, ki: (0, 0, ki[s])),
        ],
        out_specs=pl.BlockSpec((FOX_TQ, FOX_WIDTH), lambda s, qi, ki: (qi[s], 0)),
        scratch_shapes=[pltpu.VMEM((FOX_HEADS, 1, FOX_TQ), F32),
                        pltpu.VMEM((FOX_HEADS, FOX_PAD, FOX_TQ), F32)],
    )
    return pl.pallas_call(
        _fox_kernel,
        grid_spec=grid_spec,
        out_shape=jax.ShapeDtypeStruct((seq, FOX_WIDTH), BF16),
        compiler_params=pltpu.CompilerParams(dimension_semantics=("arbitrary",), vmem_limit_bytes=VMEM_LIMIT),
        name="fox_prompt",
    )(jnp.asarray(qi), jnp.asarray(ki), qt, kh, vt)


def _lane_cumsum(x):
    n = x.shape[1]
    lane = lax.broadcasted_iota(jnp.int32, x.shape, 1)
    shift = 1
    while shift < n:
        x = x + jnp.where(lane >= shift, pltpu.roll(x, shift, axis=1), 0.0)
        shift *= 2
    return x


def _fox_sample_kernel(q_ref, kn_ref, vn_ref, lft_ref, ck_ref, cv_ref, o_ref,
                       qbd_ref, cq_ref, cum_ref, m_ref, l_ref, acc_ref, *, n_past_chunks, tk, t_new):
    c = pl.program_id(1)
    rows = FOX_HEADS * t_new
    lane_head = lax.broadcasted_iota(jnp.int32, (t_new, FOX_WIDTH), 1) // FOX_HEAD_DIM

    @pl.when(c == 0)
    def _():
        cum = _lane_cumsum(lft_ref[0])
        for j in range(n_past_chunks):
            cum_ref[j] = cum[:, j * tk:(j + 1) * tk]
        new_cum = cum[:, n_past_chunks * tk:n_past_chunks * tk + LANES]
        cum_ref[n_past_chunks, :, :LANES] = new_cum
        new_cum_t = jnp.concatenate([new_cum] * (LANES // FOX_HEADS), axis=0).T
        q = q_ref[...]
        for h in range(FOX_HEADS):
            qbd_ref[h * t_new:(h + 1) * t_new, :] = jnp.where(lane_head == h, q, jnp.zeros_like(q))
            cq_ref[h * t_new:(h + 1) * t_new, :] = jnp.broadcast_to(new_cum_t[:t_new, h:h + 1], (t_new, LANES))
        m_ref[...] = jnp.full_like(m_ref, NEG_INF)
        l_ref[...] = jnp.zeros_like(l_ref)
        acc_ref[...] = jnp.zeros_like(acc_ref)

    def update(k, v, ck_rows, mask):
        s = lax.dot_general(qbd_ref[...], k, (((1,), (1,)), ((), ())), preferred_element_type=F32)
        s = s + (cq_ref[:, :1] - ck_rows)
        if mask is not None:
            s = jnp.where(mask, s, NEG_INF)
        m_prev = m_ref[...]
        m_new = jnp.maximum(m_prev, jnp.max(s, axis=1, keepdims=True))
        alpha = jnp.exp(m_prev - m_new)
        p = jnp.exp(s - m_new[:, :1])
        l_ref[...] = alpha * l_ref[...] + jnp.sum(p, axis=1, keepdims=True)
        acc_ref[...] = alpha[:, :1] * acc_ref[...] + jnp.dot(p.astype(BF16), v, preferred_element_type=F32)
        m_ref[...] = m_new

    def expand_rows(x, width):
        return jnp.concatenate([jnp.broadcast_to(x[h:h + 1, :], (t_new, width)) for h in range(FOX_HEADS)], axis=0)

    @pl.when(c < n_past_chunks)
    def _():
        update(ck_ref[0].astype(BF16), cv_ref[0].astype(BF16), expand_rows(cum_ref[c], tk), None)

    @pl.when(c == n_past_chunks)
    def _():
        ck_rows = expand_rows(cum_ref[n_past_chunks, :, :LANES], LANES)[:, :t_new]
        rowt = lax.broadcasted_iota(jnp.int32, (rows, t_new), 0) % t_new
        coli = lax.broadcasted_iota(jnp.int32, (rows, t_new), 1)
        update(kn_ref[...].astype(BF16), vn_ref[...].astype(BF16), ck_rows, coli <= rowt)
        out = acc_ref[...] / l_ref[:, :1]
        o = jnp.zeros((t_new, FOX_WIDTH), F32)
        for h in range(FOX_HEADS):
            o = o + jnp.where(lane_head == h, out[h * t_new:(h + 1) * t_new, :], 0.0)
        o_ref[...] = o.astype(o_ref.dtype)


def _fox_sample(qs, k_new, v_new, lf_all_t, cache_k, cache_v, *, t_new):
    nb, past = cache_k.shape[0], cache_k.shape[1]
    tk = SAMPLE_TK
    npc = past // tk
    rows = FOX_HEADS * t_new
    last = npc - 1
    kern = functools.partial(_fox_sample_kernel, n_past_chunks=npc, tk=tk, t_new=t_new)
    return pl.pallas_call(
        kern,
        grid=(nb, npc + 1),
        in_specs=[
            pl.BlockSpec((t_new, FOX_WIDTH), lambda b, c: (b, 0)),
            pl.BlockSpec((t_new, FOX_WIDTH), lambda b, c: (b, 0)),
            pl.BlockSpec((t_new, FOX_WIDTH), lambda b, c: (b, 0)),
            pl.BlockSpec((1, FOX_HEADS, past + LANES), lambda b, c: (b, 0, 0)),
            pl.BlockSpec((1, tk, FOX_WIDTH), lambda b, c: (b, jnp.minimum(c, last), 0)),
            pl.BlockSpec((1, tk, FOX_WIDTH), lambda b, c: (b, jnp.minimum(c, last), 0)),
        ],
        out_specs=pl.BlockSpec((t_new, FOX_WIDTH), lambda b, c: (b, 0)),
        out_shape=jax.ShapeDtypeStruct((nb * t_new, FOX_WIDTH), BF16),
        scratch_shapes=[
            pltpu.VMEM((rows, FOX_WIDTH), BF16),
            pltpu.VMEM((rows, LANES), F32),
            pltpu.VMEM((npc + 1, FOX_HEADS, tk), F32),
            pltpu.VMEM((rows, LANES), F32),
            pltpu.VMEM((rows, LANES), F32),
            pltpu.VMEM((rows, FOX_WIDTH), F32),
        ],
        compiler_params=pltpu.CompilerParams(dimension_semantics=("arbitrary", "arbitrary"),
                                             vmem_limit_bytes=VMEM_LIMIT),
        name="fox_sample",
    )(qs, k_new, v_new, lf_all_t, cache_k, cache_v)


def _hgrn_kernel(hq_ref, hk_ref, lfh_ref, hi_ref, shg_ref, s0_ref, norm_ref, tri_ref, ones_ref,
                 o_ref, sout_ref, st_ref, qq_ref, kk_ref, eb_ref, oin_ref, od_ref):
    t = pl.program_id(1)
    tc = hq_ref.shape[0]
    blk = min(HG_BLOCK, tc)

    @pl.when(t == 0)
    def _():
        for h in range(HG_HEADS):
            st_ref[h] = s0_ref[0, h].T

    row_in_blk = lax.broadcasted_iota(jnp.int32, (tc, HG_DIM), 0) % blk
    for h in range(HG_HEADS):
        sl = slice(h * HG_DIM, (h + 1) * HG_DIM)
        lf = lfh_ref[:, sl]
        b = _sum_by_01_matrix(tri_ref[...], lf)
        b_end = _sum_by_01_matrix(ones_ref[...], lf)
        eb = jnp.exp(b)
        q = hq_ref[:, sl].astype(F32)
        k = hk_ref[:, sl].astype(F32)
        v = hi_ref[:, sl].astype(F32)
        qq_ref[:, sl] = (q * eb).astype(BF16)
        kk_ref[:, sl] = (k * jnp.exp(b_end - b)).astype(BF16)
        eb_ref[:, sl] = eb
        od = jnp.sum(q * k, axis=1, keepdims=True) * v
        for lag in range(1, blk):
            k_l = pltpu.roll(k, lag, axis=0)
            b_l = pltpu.roll(b, lag, axis=0)
            v_l = pltpu.roll(v, lag, axis=0)
            w = q * k_l * jnp.exp(jnp.minimum(b - b_l, 0.0))
            w = jnp.where(row_in_blk >= lag, w, 0.0)
            od = od + jnp.sum(w, axis=1, keepdims=True) * v_l
        od_ref[:, sl] = od

    def block_step(j, carry):
        r0 = pl.multiple_of(j * blk, blk)
        for h in range(HG_HEADS):
            sl = slice(h * HG_DIM, (h + 1) * HG_DIM)
            st = st_ref[h]
            oin_ref[pl.ds(r0, blk), sl] = lax.dot_general(
                qq_ref[pl.ds(r0, blk), sl], st.astype(BF16), (((1,), (1,)), ((), ())), preferred_element_type=F32)
            upd = lax.dot_general(hi_ref[pl.ds(r0, blk), sl], kk_ref[pl.ds(r0, blk), sl],
                                  (((0,), (0,)), ((), ())), preferred_element_type=F32)
            tail = eb_ref[pl.ds(pl.multiple_of(r0 + blk - 8, 8), 8), sl]
            st_ref[h] = st * tail[7:8, :] + upd
        return carry

    lax.fori_loop(0, tc // blk, block_step, 0)

    for h in range(HG_HEADS):
        sl = slice(h * HG_DIM, (h + 1) * HG_DIM)
        o = oin_ref[:, sl] + od_ref[:, sl]
        y = (_rms_scale(o) * norm_ref[...]) * shg_ref[:, sl].astype(F32)
        o_ref[:, sl] = y.astype(o_ref.dtype)

    @pl.when(t == pl.num_programs(1) - 1)
    def _():
        for h in range(HG_HEADS):
            sout_ref[0, h] = st_ref[h].T


def _block_diag_01(n, blk, lower):
    r = np.arange(n)
    same = (r[:, None] // blk) == (r[None, :] // blk)
    if lower:
        same = same & (r[None, :] <= r[:, None])
    return jnp.asarray(same.astype(np.float32), BF16)


def _hgrn(hq, hk, lfh, hi, shg, s0, norm, *, nseq):
    rows = hq.shape[0]
    t_len = rows // nseq
    tc = min(HG_TC, t_len)
    nt = t_len // tc
    blk = min(HG_BLOCK, tc)
    tri = _block_diag_01(tc, blk, True)
    ones = _block_diag_01(tc, blk, False)
    row = pl.BlockSpec((tc, HG_WIDTH), lambda b, t: (b * nt + t, 0))
    state = pl.BlockSpec((1, HG_HEADS, HG_DIM, HG_DIM), lambda b, t: (b, 0, 0, 0))
    return pl.pallas_call(
        _hgrn_kernel,
        grid=(nseq, nt),
        in_specs=[row, row, row, row, row, state,
                  pl.BlockSpec((1, HG_DIM), lambda b, t: (0, 0)),
                  pl.BlockSpec((tc, tc), lambda b, t: (0, 0)),
                  pl.BlockSpec((tc, tc), lambda b, t: (0, 0))],
        out_specs=[row, state],
        out_shape=[jax.ShapeDtypeStruct((rows, HG_WIDTH), BF16),
                   jax.ShapeDtypeStruct((nseq, HG_HEADS, HG_DIM, HG_DIM), F32)],
        scratch_shapes=[
            pltpu.VMEM((HG_HEADS, HG_DIM, HG_DIM), F32),
            pltpu.VMEM((tc, HG_WIDTH), BF16),
            pltpu.VMEM((tc, HG_WIDTH), BF16),
            pltpu.VMEM((tc, HG_WIDTH), F32),
            pltpu.VMEM((tc, HG_WIDTH), F32),
            pltpu.VMEM((tc, HG_WIDTH), F32),
        ],
        compiler_params=pltpu.CompilerParams(dimension_semantics=("arbitrary", "arbitrary"),
                                             vmem_limit_bytes=VMEM_LIMIT),
        name="hgrn",
    )(hq, hk, lfh, hi, shg, s0, norm, tri, ones)


def _mixffn_kernel(x_ref, of_ref, oh_ref, sga_ref, sgb_ref, hist_ref,
                   wbf_ref, wbh_ref, wout_ref, wup_ref, wdn_ref,
                   npost_ref, npre2_ref, npost2_ref, cw_ref, cb_ref,
                   y_ref, conv_ref, tail_ref, *, seg_len):
    tm = x_ref.shape[0]
    br_f = jnp.dot(of_ref[...], wbf_ref[...], preferred_element_type=F32)
    br_h = jnp.dot(oh_ref[...], wbh_ref[...], preferred_element_type=F32)
    merged = sga_ref[...].astype(F32) * br_f + sgb_ref[...].astype(F32) * br_h
    mix = jnp.dot(merged.astype(BF16), wout_ref[...], preferred_element_type=F32)
    x1 = x_ref[...] + _rms_scale(mix) * npost_ref[...]

    h2 = (_rms_scale(x1) * npre2_ref[...]).astype(BF16)
    up = jnp.dot(h2, wup_ref[...], preferred_element_type=F32)
    a = up[:, :D_FF]
    g = up[:, D_FF:]

    prev1 = pltpu.roll(a, 1, axis=0)
    prev2 = pltpu.roll(a, 2, axis=0)
    rowi = lax.broadcasted_iota(jnp.int32, a.shape, 0)
    if seg_len >= tm:
        @pl.when(pl.program_id(0) == 0)
        def _():
            tail_ref[...] = hist_ref[0]
        t0 = tail_ref[0:1, :]
        t1 = tail_ref[1:2, :]
        prev1 = jnp.where(rowi == 0, t1, prev1)
        prev2 = jnp.where(rowi == 0, t0, jnp.where(rowi == 1, t1, prev2))
        tail_ref[...] = a[tm - 2:, :]
        conv_ref[0] = a[tm - 2:, :]
    else:
        for s in range(tm // seg_len):
            h0 = hist_ref[s, 0:1, :]
            h1 = hist_ref[s, 1:2, :]
            prev1 = jnp.where(rowi == s * seg_len, h1, prev1)
            prev2 = jnp.where(rowi == s * seg_len, h0, jnp.where(rowi == s * seg_len + 1, h1, prev2))
            conv_ref[s] = a[(s + 1) * seg_len - 2:(s + 1) * seg_len, :]
    c = cb_ref[...] + cw_ref[0:1, :] * prev2 + cw_ref[1:2, :] * prev1 + cw_ref[2:3, :] * a
    act = (jax.nn.gelu(c, approximate=True) * g).astype(BF16)
    ff = jnp.dot(act, wdn_ref[...], preferred_element_type=F32)
    y_ref[...] = x1 + _rms_scale(ff) * npost2_ref[...]


def _mixffn(x, o_fox, o_hg, sga, sgb, hist, w, *, seg_len):
    rows = x.shape[0]
    tm = min(FFN_TM, rows)
    n = rows // tm
    nseg = hist.shape[0]
    row = lambda width: pl.BlockSpec((tm, width), lambda i: (i, 0))
    weights = [w["bf"], w["bh"], w["out"], w["up"], w["down"]]
    smalls = [w["npost"], w["npre2"], w["npost2"], w["conv_w"], w["conv_b"]]
    hist_spec = pl.BlockSpec(hist.shape, lambda i: (0, 0, 0))
    scratch = [pltpu.VMEM((2, D_FF), F32)]
    return pl.pallas_call(
        functools.partial(_mixffn_kernel, seg_len=seg_len),
        grid=(n,),
        in_specs=[row(D_MODEL), row(FOX_WIDTH), row(HG_WIDTH), row(D_MODEL), row(D_MODEL), hist_spec]
                 + [_const_spec(a.shape) for a in weights] + [_const_spec(a.shape) for a in smalls],
        out_specs=[row(D_MODEL), pl.BlockSpec((nseg, 2, D_FF), lambda i: (0, 0, 0))],
        out_shape=[jax.ShapeDtypeStruct((rows, D_MODEL), F32), jax.ShapeDtypeStruct((nseg, 2, D_FF), F32)],
        scratch_shapes=scratch,
        compiler_params=pltpu.CompilerParams(dimension_semantics=("arbitrary",), vmem_limit_bytes=VMEM_LIMIT),
        name="mixffn",
    )(x, o_fox, o_hg, sga, sgb, hist, *weights, *smalls)


def _prep_w_in(w_in, fox_f_bias):
    offs = np.cumsum([0] + IN_SIZES)
    seg = [w_in[:, int(offs[i]):int(offs[i + 1])] for i in range(len(IN_SIZES))]
    pad = jnp.zeros((D_MODEL, LANES - 3 * FOX_HEADS), w_in.dtype)
    f3 = jnp.concatenate([seg[3], seg[3], seg[3], pad], axis=1)
    w_all = jnp.concatenate(seg[:3] + [f3] + seg[4:], axis=1).astype(BF16)
    fb = fox_f_bias.astype(F32)
    fb3 = jnp.concatenate([fb, fb, fb, jnp.zeros((LANES - 3 * FOX_HEADS,), F32)]).reshape(1, LANES)
    return w_all, fb3


def kernel(x_prompt, x_sample, cache_fox_k, cache_fox_v, cache_fox_logf, state_hgrn, state_ffn_conv, norm_mix_pre, norm_mix_post, w_in, fox_f_bias, hgrn_lb_logits, hgrn_norm, w_branch_fox, w_branch_hgrn, w_out, norm_ffn_pre, norm_ffn_post, w_up, ffn_conv_w, ffn_conv_b, w_down):
    depth = w_in.shape[0]
    assert depth == 1 and hgrn_lb_logits.shape[0] == 2
    bp, seq, _ = x_prompt.shape
    assert bp == 1
    nb, t_new, _ = x_sample.shape
    past = cache_fox_k.shape[2]

    w_all, fb3 = _prep_w_in(w_in[0], fox_f_bias[0])
    g_pre = norm_mix_pre[0].reshape(1, D_MODEL)
    lbl = hgrn_lb_logits.astype(F32)
    hnorm = hgrn_norm[0].astype(F32).reshape(1, HG_DIM)
    w = {
        "bf": w_branch_fox[0].astype(BF16), "bh": w_branch_hgrn[0].astype(BF16), "out": w_out[0].astype(BF16),
        "up": w_up[0].astype(BF16), "down": w_down[0].astype(BF16),
        "npost": norm_mix_post[0].reshape(1, D_MODEL), "npre2": norm_ffn_pre[0].reshape(1, D_MODEL),
        "npost2": norm_ffn_post[0].reshape(1, D_MODEL),
        "conv_w": ffn_conv_w[0], "conv_b": ffn_conv_b[0].reshape(1, D_FF),
    }

    xp = x_prompt.reshape(seq, D_MODEL)
    (qh, kh, vh, pk, pv, plf, hq, hk, lfh, hi, shg, sga, sgb) = _proj(xp, g_pre, w_all, fb3, lbl, fold=True)
    o_fox = _fox_prompt(qh, kh, vh)
    s0 = jnp.zeros((1, HG_HEADS, HG_DIM, HG_DIM), F32)
    o_hg, p_state = _hgrn(hq, hk, lfh, hi, shg, s0, hnorm, nseq=1)
    hist0 = jnp.zeros((1, 2, D_FF), F32)
    yp, pconv = _mixffn(xp, o_fox, o_hg, sga, sgb, hist0, w, seg_len=seq)

    xs = x_sample.reshape(nb * t_new, D_MODEL)
    (qs, sk, sv, slf, hq, hk, lfh, hi, shg, sga, sgb) = _proj(xs, g_pre, w_all, fb3, lbl, fold=False)
    lf_all_t = jnp.concatenate([
        jnp.swapaxes(cache_fox_logf[0].astype(F32), 1, 2),
        jnp.swapaxes(slf.reshape(nb, t_new, FOX_HEADS), 1, 2),
        jnp.zeros((nb, FOX_HEADS, LANES - t_new), F32)], axis=2)
    o_fox_s = _fox_sample(qs, sk, sv, lf_all_t,
                          cache_fox_k[0].reshape(nb, past, FOX_WIDTH), cache_fox_v[0].reshape(nb, past, FOX_WIDTH),
                          t_new=t_new)
    o_hg_s, s_state = _hgrn(hq, hk, lfh, hi, shg, state_hgrn[0].astype(F32), hnorm, nseq=nb)
    ys, sconv = _mixffn(xs, o_fox_s, o_hg_s, sga, sgb, state_ffn_conv[0], w, seg_len=t_new)

    return (
        yp.reshape(bp, seq, D_MODEL),
        ys.reshape(nb, t_new, D_MODEL),
        pk.reshape(1, bp, seq, FOX_HEADS, FOX_HEAD_DIM),
        pv.reshape(1, bp, seq, FOX_HEADS, FOX_HEAD_DIM),
        plf.reshape(1, bp, seq, FOX_HEADS),
        p_state.reshape(1, bp, HG_HEADS, HG_DIM, HG_DIM),
        pconv.reshape(1, bp, 2, D_FF),
        sk.reshape(1, nb, t_new, FOX_HEADS, FOX_HEAD_DIM),
        sv.reshape(1, nb, t_new, FOX_HEADS, FOX_HEAD_DIM),
        slf.reshape(1, nb, t_new, FOX_HEADS),
        s_state.reshape(1, nb, HG_HEADS, HG_DIM, HG_DIM),
        sconv.reshape(1, nb, 2, D_FF),
    )
```

```python
import functools

import numpy as np
import jax
import jax.numpy as jnp
from jax import lax
from jax.experimental import pallas as pl
from jax.experimental.pallas import tpu as pltpu

F32 = jnp.float32
BF16 = jnp.bfloat16

D_MODEL = 1024
FOX_HEADS = 8
FOX_HEAD_DIM = 64
FOX_WIDTH = FOX_HEADS * FOX_HEAD_DIM
HG_HEADS = 4
HG_DIM = 128
HG_WIDTH = HG_HEADS * HG_DIM
D_FF = 2816
RMS_EPS = 1e-6
NEG_INF = -1e30
LOG2E = 1.4426950408889634
FOX_SKIP_NATS = 110.0
IN_SIZES = [FOX_WIDTH, FOX_WIDTH, FOX_WIDTH, FOX_HEADS, HG_WIDTH, HG_WIDTH, HG_WIDTH, HG_WIDTH, D_MODEL, D_MODEL]

LANES = 128
FOX_PAD = 2 * FOX_HEAD_DIM
HG_BLOCK = 16
VMEM_LIMIT = 56 * 1024 * 1024

PROJ_TM = 256
FOX_TQ = 512
FOX_TK = 512
HG_TC = 256
FFN_TM = 256
SAMPLE_TK = 1024

_C_Q, _C_K, _C_V, _C_F = 0, 512, 1024, 1536
_C_HQ, _C_HF, _C_HI, _C_HG = 1664, 2176, 2688, 3200
_C_GA, _C_GB, _C_END = 3712, 4736, 5760


def _split3(x):
    hi = x.astype(BF16)
    r = x - hi.astype(F32)
    mid = r.astype(BF16)
    lo = (r - mid.astype(F32)).astype(BF16)
    return hi, mid, lo


def _sum_by_01_matrix(mat01, x):
    cat = jnp.concatenate(_split3(x), axis=1)
    y = jnp.dot(mat01, cat, preferred_element_type=F32)
    return y[:, :LANES] + y[:, LANES:2 * LANES] + y[:, 2 * LANES:]


def _rms_scale(x):
    return x * lax.rsqrt(jnp.mean(x * x, axis=-1, keepdims=True) + RMS_EPS)


def _log_sigmoid(x):
    return jnp.minimum(x, 0.0) - jnp.log1p(jnp.exp(-jnp.abs(x)))


def _sigmoid(x):
    return 1.0 / (1.0 + jnp.exp(-x))


def _proj_kernel(*refs, fold):
    if fold:
        (x_ref, g_ref, w_ref, fb_ref, lbl_ref, tri_ref, pq_ref, pk_ref, cq_ref, ck_ref, cv_ref, seg_ref,
         qh_ref, kh_ref, vh_ref, stat_ref, kout_ref, vout_ref, lf_ref, hq_ref, hk_ref, lfh_ref, hi_ref, shg_ref,
         sga_ref, sgb_ref, carry_ref) = refs
    else:
        (x_ref, g_ref, w_ref, fb_ref, lbl_ref,
         qs_ref, kout_ref, vout_ref, lf_ref, hq_ref, hk_ref, lfh_ref, hi_ref, shg_ref,
         sga_ref, sgb_ref) = refs

    h = (_rms_scale(x_ref[...]) * g_ref[...]).astype(BF16)
    z = jnp.dot(h, w_ref[...], preferred_element_type=F32)

    zq = z[:, _C_Q:_C_K] * (FOX_HEAD_DIM ** -0.5)
    zk = z[:, _C_K:_C_V]
    zv = z[:, _C_V:_C_F]
    kout_ref[...] = zk
    vout_ref[...] = zv
    logf = _log_sigmoid(z[:, _C_F:_C_HQ] + fb_ref[...])
    lf_ref[...] = logf[:, :FOX_HEADS]

    l0 = lbl_ref[0:1, :]
    l1 = lbl_ref[1:2, :]
    lmax = jnp.maximum(l0, l1)
    e0 = jnp.exp(l0 - lmax)
    lb = e0 / (e0 + jnp.exp(l1 - lmax))
    f = lb + (1.0 - lb) * _sigmoid(z[:, _C_HF:_C_HI])
    hq_ref[...] = z[:, _C_HQ:_C_HF].astype(BF16)
    hk_ref[...] = (1.0 - f).astype(BF16)
    lfh_ref[...] = jnp.log(f)
    hi_ref[...] = z[:, _C_HI:_C_HG].astype(BF16)
    shg_ref[...] = _sigmoid(z[:, _C_HG:_C_GA]).astype(BF16)
    sga_ref[...] = _sigmoid(z[:, _C_GA:_C_GB]).astype(BF16)
    sgb_ref[...] = _sigmoid(z[:, _C_GB:_C_END]).astype(BF16)

    if not fold:
        qs_ref[...] = zq.astype(BF16)
        return

    @pl.when(pl.program_id(0) == 0)
    def _():
        carry_ref[...] = jnp.zeros_like(carry_ref)

    cum = carry_ref[...] + _sum_by_01_matrix(tri_ref[...], logf)
    carry_ref[...] = cum[-1:, :]

    seg = seg_ref[...]
    qn2 = jnp.dot((zq * zq).astype(BF16), seg, preferred_element_type=F32)
    kn2 = jnp.dot((zk * zk).astype(BF16), seg, preferred_element_type=F32)
    dg = jnp.dot((zq * zk).astype(BF16), seg, preferred_element_type=F32)
    stat_ref[0, 0:1, :] = jnp.max(qn2, axis=0, keepdims=True)
    stat_ref[0, 1:2, :] = jnp.max(kn2, axis=0, keepdims=True)
    stat_ref[0, 2:3, :] = jnp.min(dg, axis=0, keepdims=True)
    stat_ref[0, 3:4, :] = cum[0:1, :]
    stat_ref[0, 4:5, :] = cum[-1:, :]
    stat_ref[0, 5:8, :] = jnp.zeros((3, LANES), F32)

    zq = zq * LOG2E
    c_hi, c_mid, c_lo = _split3(cum * LOG2E)
    lane = lax.broadcasted_iota(jnp.int32, cum.shape, 1)
    pieces = jnp.where(lane < 8, c_hi, jnp.where(lane < 16, c_mid, c_lo))
    pieces = jnp.where(lane < 24, pieces, jnp.zeros_like(pieces))
    ex_q = jnp.dot(pieces, pq_ref[...], preferred_element_type=F32) + cq_ref[...]
    ex_k = jnp.dot(pieces, pk_ref[...], preferred_element_type=F32) + ck_ref[...]
    ex_v = cv_ref[...]

    low = lax.broadcasted_iota(jnp.int32, (zq.shape[0], LANES), 1) < FOX_HEAD_DIM
    for src, ex, dst, transposed in ((zq, ex_q, qh_ref, True), (zk, ex_k, kh_ref, False), (zv, ex_v, vh_ref, True)):
        for c in range(FOX_WIDTH // LANES):
            pair = src[:, c * LANES:(c + 1) * LANES]
            swapped = pltpu.roll(pair, FOX_HEAD_DIM, axis=1)
            for j, data in enumerate((pair, swapped)):
                hd = 2 * c + j
                blk = jnp.where(low, data, ex[:, hd * LANES:(hd + 1) * LANES])
                dst[hd] = (blk.T if transposed else blk).astype(BF16)


def _bias_fold_constants():
    pq = np.zeros((LANES, FOX_HEADS * LANES), np.float32)
    pk = np.zeros((LANES, FOX_HEADS * LANES), np.float32)
    cq = np.zeros((1, FOX_HEADS * LANES), np.float32)
    ck = np.zeros((1, FOX_HEADS * LANES), np.float32)
    cv = np.zeros((1, FOX_HEADS * LANES), np.float32)
    for h in range(FOX_HEADS):
        base = h * LANES + FOX_HEAD_DIM
        for p in range(3):
            pq[p * 8 + h, base + p] = 1.0
            ck[0, base + p] = 1.0
            pk[p * 8 + h, base + 3 + p] = -1.0
            cq[0, base + 3 + p] = 1.0
        cv[0, base] = 1.0
    return (jnp.asarray(pq, BF16), jnp.asarray(pk, BF16), jnp.asarray(cq), jnp.asarray(ck), jnp.asarray(cv))


def _const_spec(shape, single=True):
    nd = len(shape)
    if single:
        return pl.BlockSpec(shape, lambda *_: (0,) * nd, pipeline_mode=pl.Buffered(1))
    return pl.BlockSpec(shape, lambda *_: (0,) * nd)


def _proj(x, gain, w_all, fb3, lb_logits, *, fold):
    rows = x.shape[0]
    tm = min(PROJ_TM, rows)
    n = rows // tm
    row = lambda width: pl.BlockSpec((tm, width), lambda i: (i, 0))
    in_specs = [row(D_MODEL), _const_spec((1, D_MODEL)), _const_spec(w_all.shape), _const_spec((1, LANES)),
                _const_spec(lb_logits.shape)]
    args = [x, gain, w_all, fb3, lb_logits]
    common_out = [
        (jax.ShapeDtypeStruct((rows, FOX_WIDTH), F32), row(FOX_WIDTH)),
        (jax.ShapeDtypeStruct((rows, FOX_WIDTH), F32), row(FOX_WIDTH)),
        (jax.ShapeDtypeStruct((rows, FOX_HEADS), F32), row(FOX_HEADS)),
        (jax.ShapeDtypeStruct((rows, HG_WIDTH), BF16), row(HG_WIDTH)),
        (jax.ShapeDtypeStruct((rows, HG_WIDTH), BF16), row(HG_WIDTH)),
        (jax.ShapeDtypeStruct((rows, HG_WIDTH), F32), row(HG_WIDTH)),
        (jax.ShapeDtypeStruct((rows, HG_WIDTH), BF16), row(HG_WIDTH)),
        (jax.ShapeDtypeStruct((rows, HG_WIDTH), BF16), row(HG_WIDTH)),
        (jax.ShapeDtypeStruct((rows, D_MODEL), BF16), row(D_MODEL)),
        (jax.ShapeDtypeStruct((rows, D_MODEL), BF16), row(D_MODEL)),
    ]
    scratch = []
    if fold:
        tri = jnp.asarray(np.tril(np.ones((tm, tm), np.float32)), BF16)
        seg = np.zeros((FOX_WIDTH, LANES), np.float32)
        seg[np.arange(FOX_WIDTH), np.arange(FOX_WIDTH) // FOX_HEAD_DIM] = 1.0
        consts = _bias_fold_constants() + (jnp.asarray(seg, BF16),)
        in_specs += [_const_spec(tri.shape)] + [_const_spec(c.shape) for c in consts]
        args += [tri, *consts]
        head_major = (jax.ShapeDtypeStruct((FOX_HEADS, rows, FOX_PAD), BF16),
                      pl.BlockSpec((FOX_HEADS, tm, FOX_PAD), lambda i: (0, i, 0)))
        head_major_t = (jax.ShapeDtypeStruct((FOX_HEADS, FOX_PAD, rows), BF16),
                        pl.BlockSpec((FOX_HEADS, FOX_PAD, tm), lambda i: (0, 0, i)))
        stats = (jax.ShapeDtypeStruct((n, 8, LANES), F32), pl.BlockSpec((1, 8, LANES), lambda i: (i, 0, 0)))
        outs = [head_major_t, head_major, head_major_t, stats] + common_out
        scratch = [pltpu.VMEM((1, LANES), F32)]
    else:
        outs = [(jax.ShapeDtypeStruct((rows, FOX_WIDTH), BF16), row(FOX_WIDTH))] + common_out
    return pl.pallas_call(
        functools.partial(_proj_kernel, fold=fold),
        grid=(n,),
        in_specs=in_specs,
        out_specs=[o[1] for o in outs],
        out_shape=[o[0] for o in outs],
        scratch_shapes=scratch,
        compiler_params=pltpu.CompilerParams(dimension_semantics=("arbitrary",), vmem_limit_bytes=VMEM_LIMIT),
        name="proj_fold" if fold else "proj",
    )(*args)


def _fox_kernel(qi_ref, ki_ref, mode_ref, qt_ref, k_ref, vt_ref, o_ref, m_ref, acc_ref):
    mode = mode_ref[pl.program_id(0)]
    tq = qt_ref.shape[2]
    tk = k_ref.shape[1]

    @pl.when((mode & _MODE_FIRST) != 0)
    def _():
        m_ref[...] = jnp.full_like(m_ref, NEG_INF)
        acc_ref[...] = jnp.zeros_like(acc_ref)

    def head_step(h, masked):
        s = jnp.dot(k_ref[h], qt_ref[h], preferred_element_type=F32)
        if masked:
            key = lax.broadcasted_iota(jnp.int32, (tk, tq), 0)
            qry = lax.broadcasted_iota(jnp.int32, (tk, tq), 1)
            s = jnp.where(key <= qry, s, NEG_INF)
        m_prev = m_ref[h]
        m_new = jnp.maximum(m_prev, jnp.max(s, axis=0, keepdims=True))
        alpha = jnp.exp2(m_prev - m_new)
        p = jnp.exp2(s - m_new).astype(BF16)
        acc_ref[h] = alpha * acc_ref[h] + jnp.dot(vt_ref[h], p, preferred_element_type=F32)
        m_ref[h] = m_new

    @pl.when((mode & _MODE_PAIR) != 0)
    def _():
        for h in range(FOX_HEADS):
            head_step(h, False)

    @pl.when((mode & _MODE_DIAG) != 0)
    def _():
        for h in range(FOX_HEADS):
            head_step(h, True)
        for c in range(FOX_WIDTH // LANES):
            halves = []
            for hd in (2 * c, 2 * c + 1):
                acc = acc_ref[hd]
                halves.append(acc[:FOX_HEAD_DIM, :] / acc[FOX_HEAD_DIM:FOX_HEAD_DIM + 1, :])
            o_ref[:, c * LANES:(c + 1) * LANES] = jnp.concatenate(halves, axis=0).T.astype(o_ref.dtype)


_MODE_FIRST, _MODE_PAIR, _MODE_DIAG = 1, 2, 4


def _fox_schedule(stats, nq):
    st = stats[:, :5, :FOX_HEADS].reshape(nq, -1, 5, FOX_HEADS)
    qn = jnp.sqrt(jnp.max(st[:, :, 0], axis=1)) * 1.01
    kn = jnp.sqrt(jnp.max(st[:, :, 1], axis=1)) * 1.01
    dmin = jnp.min(st[:, :, 2], axis=1)
    c_first = st[:, 0, 3]
    c_last = st[:, -1, 4]
    bound = qn[:, None] * kn[None, :] + (c_first - dmin)[:, None] - c_last[None, :]
    blk = jnp.arange(nq, dtype=jnp.int32)
    drop = jnp.all(bound < -FOX_SKIP_NATS, axis=-1) & (blk[None, :] < blk[:, None])
    prefix = jnp.cumsum(jnp.logical_not(drop).astype(jnp.int32), axis=1) == 0
    kstart = jnp.sum(prefix.astype(jnp.int32), axis=1)
    ends = jnp.cumsum(blk - kstart + 1)
    nsteps = nq * (nq + 1) // 2
    step = jnp.arange(nsteps, dtype=jnp.int32)
    valid = step < ends[-1]
    q_of = jnp.minimum(jnp.searchsorted(ends, step, side="right").astype(jnp.int32), nq - 1)
    begin = jnp.where(q_of > 0, ends[jnp.maximum(q_of - 1, 0)], 0)
    k_of = kstart[q_of] + (step - begin)
    q_of = jnp.where(valid, q_of, nq - 1)
    k_of = jnp.where(valid, k_of, nq - 1)
    mode = jnp.where(k_of == q_of, _MODE_DIAG, _MODE_PAIR) + jnp.where(k_of == kstart[q_of], _MODE_FIRST, 0)
    mode = jnp.where(valid, mode, 0)
    return q_of.astype(jnp.int32), k_of.astype(jnp.int32), mode.astype(jnp.int32)


def _fox_prompt(qt, kh, vt, stats):
    seq = kh.shape[1]
    assert FOX_TQ == FOX_TK
    nq = seq // FOX_TQ
    qi, ki, mode = _fox_schedule(stats, nq)
    grid_spec = pltpu.PrefetchScalarGridSpec(
        num_scalar_prefetch=3,
        grid=(nq * (nq + 1) // 2,),
        in_specs=[
            pl.BlockSpec((FOX_HEADS, FOX_PAD, FOX_TQ), lambda s, qi, ki, mode: (0, 0, qi[s])),
            pl.BlockSpec((FOX_HEADS, FOX_TK, FOX_PAD), lambda s, qi, ki, mode: (0, ki[s], 0)),
            pl.BlockSpec((FOX_HEADS, FOX_PAD, FOX_TK), lambda s, qi, ki, mode: (0, 0, ki[s])),
        ],
        out_specs=pl.BlockSpec((FOX_TQ, FOX_WIDTH), lambda s, qi, ki, mode: (qi[s], 0)),
        scratch_shapes=[pltpu.VMEM((FOX_HEADS, 1, FOX_TQ), F32),
                        pltpu.VMEM((FOX_HEADS, FOX_PAD, FOX_TQ), F32)],
    )
    return pl.pallas_call(
        _fox_kernel,
        grid_spec=grid_spec,
        out_shape=jax.ShapeDtypeStruct((seq, FOX_WIDTH), BF16),
        compiler_params=pltpu.CompilerParams(dimension_semantics=("arbitrary",), vmem_limit_bytes=VMEM_LIMIT),
        name="fox_prompt",
    )(qi, ki, mode, qt, kh, vt)


def _lane_cumsum(x):
    n = x.shape[1]
    lane = lax.broadcasted_iota(jnp.int32, x.shape, 1)
    shift = 1
    while shift < n:
        x = x + jnp.where(lane >= shift, pltpu.roll(x, shift, axis=1), 0.0)
        shift *= 2
    return x


def _fox_sample_kernel(q_ref, kn_ref, vn_ref, lft_ref, ck_ref, cv_ref, o_ref,
                       qbd_ref, cq_ref, cum_ref, m_ref, l_ref, acc_ref, *, n_past_chunks, tk, t_new):
    c = pl.program_id(1)
    rows = FOX_HEADS * t_new
    lane_head = lax.broadcasted_iota(jnp.int32, (t_new, FOX_WIDTH), 1) // FOX_HEAD_DIM

    @pl.when(c == 0)
    def _():
        cum = _lane_cumsum(lft_ref[0])
        for j in range(n_past_chunks):
            cum_ref[j] = cum[:, j * tk:(j + 1) * tk]
        new_cum = cum[:, n_past_chunks * tk:n_past_chunks * tk + LANES]
        cum_ref[n_past_chunks, :, :LANES] = new_cum
        new_cum_t = jnp.concatenate([new_cum] * (LANES // FOX_HEADS), axis=0).T
        q = q_ref[...]
        for h in range(FOX_HEADS):
            qbd_ref[h * t_new:(h + 1) * t_new, :] = jnp.where(lane_head == h, q, jnp.zeros_like(q))
            cq_ref[h * t_new:(h + 1) * t_new, :] = jnp.broadcast_to(new_cum_t[:t_new, h:h + 1], (t_new, LANES))
        m_ref[...] = jnp.full_like(m_ref, NEG_INF)
        l_ref[...] = jnp.zeros_like(l_ref)
        acc_ref[...] = jnp.zeros_like(acc_ref)

    def update(k, v, ck_rows, mask):
        s = lax.dot_general(qbd_ref[...], k, (((1,), (1,)), ((), ())), preferred_element_type=F32)
        s = s + (cq_ref[:, :1] - ck_rows)
        if mask is not None:
            s = jnp.where(mask, s, NEG_INF)
        m_prev = m_ref[...]
        m_new = jnp.maximum(m_prev, jnp.max(s, axis=1, keepdims=True))
        alpha = jnp.exp(m_prev - m_new)
        p = jnp.exp(s - m_new[:, :1])
        l_ref[...] = alpha * l_ref[...] + jnp.sum(p, axis=1, keepdims=True)
        acc_ref[...] = alpha[:, :1] * acc_ref[...] + jnp.dot(p.astype(BF16), v, preferred_element_type=F32)
        m_ref[...] = m_new

    def expand_rows(x, width):
        return jnp.concatenate([jnp.broadcast_to(x[h:h + 1, :], (t_new, width)) for h in range(FOX_HEADS)], axis=0)

    @pl.when(c < n_past_chunks)
    def _():
        update(ck_ref[0].astype(BF16), cv_ref[0].astype(BF16), expand_rows(cum_ref[c], tk), None)

    @pl.when(c == n_past_chunks)
    def _():
        ck_rows = expand_rows(cum_ref[n_past_chunks, :, :LANES], LANES)[:, :t_new]
        rowt = lax.broadcasted_iota(jnp.int32, (rows, t_new), 0) % t_new
        coli = lax.broadcasted_iota(jnp.int32, (rows, t_new), 1)
        update(kn_ref[...].astype(BF16), vn_ref[...].astype(BF16), ck_rows, coli <= rowt)
        out = acc_ref[...] / l_ref[:, :1]
        o = jnp.zeros((t_new, FOX_WIDTH), F32)
        for h in range(FOX_HEADS):
            o = o + jnp.where(lane_head == h, out[h * t_new:(h + 1) * t_new, :], 0.0)
        o_ref[...] = o.astype(o_ref.dtype)


def _fox_sample(qs, k_new, v_new, lf_all_t, cache_k, cache_v, *, t_new):
    nb, past = cache_k.shape[0], cache_k.shape[1]
    tk = SAMPLE_TK
    npc = past // tk
    rows = FOX_HEADS * t_new
    last = npc - 1
    kern = functools.partial(_fox_sample_kernel, n_past_chunks=npc, tk=tk, t_new=t_new)
    return pl.pallas_call(
        kern,
        grid=(nb, npc + 1),
        in_specs=[
            pl.BlockSpec((t_new, FOX_WIDTH), lambda b, c: (b, 0)),
            pl.BlockSpec((t_new, FOX_WIDTH), lambda b, c: (b, 0)),
            pl.BlockSpec((t_new, FOX_WIDTH), lambda b, c: (b, 0)),
            pl.BlockSpec((1, FOX_HEADS, past + LANES), lambda b, c: (b, 0, 0)),
            pl.BlockSpec((1, tk, FOX_WIDTH), lambda b, c: (b, jnp.minimum(c, last), 0)),
            pl.BlockSpec((1, tk, FOX_WIDTH), lambda b, c: (b, jnp.minimum(c, last), 0)),
        ],
        out_specs=pl.BlockSpec((t_new, FOX_WIDTH), lambda b, c: (b, 0)),
        out_shape=jax.ShapeDtypeStruct((nb * t_new, FOX_WIDTH), BF16),
        scratch_shapes=[
            pltpu.VMEM((rows, FOX_WIDTH), BF16),
            pltpu.VMEM((rows, LANES), F32),
            pltpu.VMEM((npc + 1, FOX_HEADS, tk), F32),
            pltpu.VMEM((rows, LANES), F32),
            pltpu.VMEM((rows, LANES), F32),
            pltpu.VMEM((rows, FOX_WIDTH), F32),
        ],
        compiler_params=pltpu.CompilerParams(dimension_semantics=("arbitrary", "arbitrary"),
                                             vmem_limit_bytes=VMEM_LIMIT),
        name="fox_sample",
    )(qs, k_new, v_new, lf_all_t, cache_k, cache_v)


def _hgrn_kernel(hq_ref, hk_ref, lfh_ref, hi_ref, shg_ref, s0_ref, norm_ref, tri_ref, ones_ref,
                 o_ref, sout_ref, st_ref, qq_ref, kk_ref, eb_ref, oin_ref, od_ref):
    t = pl.program_id(1)
    tc = hq_ref.shape[0]
    blk = min(HG_BLOCK, tc)

    @pl.when(t == 0)
    def _():
        for h in range(HG_HEADS):
            st_ref[h] = s0_ref[0, h].T

    row_in_blk = lax.broadcasted_iota(jnp.int32, (tc, HG_DIM), 0) % blk
    for h in range(HG_HEADS):
        sl = slice(h * HG_DIM, (h + 1) * HG_DIM)
        lf = lfh_ref[:, sl]
        b = _sum_by_01_matrix(tri_ref[...], lf)
        b_end = _sum_by_01_matrix(ones_ref[...], lf)
        eb = jnp.exp(b)
        q = hq_ref[:, sl].astype(F32)
        k = hk_ref[:, sl].astype(F32)
        v = hi_ref[:, sl].astype(F32)
        qq_ref[:, sl] = (q * eb).astype(BF16)
        kk_ref[:, sl] = (k * jnp.exp(b_end - b)).astype(BF16)
        eb_ref[:, sl] = eb
        od = jnp.sum(q * k, axis=1, keepdims=True) * v
        for lag in range(1, blk):
            k_l = pltpu.roll(k, lag, axis=0)
            b_l = pltpu.roll(b, lag, axis=0)
            v_l = pltpu.roll(v, lag, axis=0)
            w = q * k_l * jnp.exp(jnp.minimum(b - b_l, 0.0))
            w = jnp.where(row_in_blk >= lag, w, 0.0)
            od = od + jnp.sum(w, axis=1, keepdims=True) * v_l
        od_ref[:, sl] = od

    def block_step(j, carry):
        r0 = pl.multiple_of(j * blk, blk)
        for h in range(HG_HEADS):
            sl = slice(h * HG_DIM, (h + 1) * HG_DIM)
            st = st_ref[h]
            oin_ref[pl.ds(r0, blk), sl] = lax.dot_general(
                qq_ref[pl.ds(r0, blk), sl], st.astype(BF16), (((1,), (1,)), ((), ())), preferred_element_type=F32)
            upd = lax.dot_general(hi_ref[pl.ds(r0, blk), sl], kk_ref[pl.ds(r0, blk), sl],
                                  (((0,), (0,)), ((), ())), preferred_element_type=F32)
            tail = eb_ref[pl.ds(pl.multiple_of(r0 + blk - 8, 8), 8), sl]
            st_ref[h] = st * tail[7:8, :] + upd
        return carry

    lax.fori_loop(0, tc // blk, block_step, 0)

    for h in range(HG_HEADS):
        sl = slice(h * HG_DIM, (h + 1) * HG_DIM)
        o = oin_ref[:, sl] + od_ref[:, sl]
        y = (_rms_scale(o) * norm_ref[...]) * shg_ref[:, sl].astype(F32)
        o_ref[:, sl] = y.astype(o_ref.dtype)

    @pl.when(t == pl.num_programs(1) - 1)
    def _():
        for h in range(HG_HEADS):
            sout_ref[0, h] = st_ref[h].T


def _block_diag_01(n, blk, lower):
    r = np.arange(n)
    same = (r[:, None] // blk) == (r[None, :] // blk)
    if lower:
        same = same & (r[None, :] <= r[:, None])
    return jnp.asarray(same.astype(np.float32), BF16)


def _hgrn(hq, hk, lfh, hi, shg, s0, norm, *, nseq):
    rows = hq.shape[0]
    t_len = rows // nseq
    tc = min(HG_TC, t_len)
    nt = t_len // tc
    blk = min(HG_BLOCK, tc)
    tri = _block_diag_01(tc, blk, True)
    ones = _block_diag_01(tc, blk, False)
    row = pl.BlockSpec((tc, HG_WIDTH), lambda b, t: (b * nt + t, 0))
    state = pl.BlockSpec((1, HG_HEADS, HG_DIM, HG_DIM), lambda b, t: (b, 0, 0, 0))
    return pl.pallas_call(
        _hgrn_kernel,
        grid=(nseq, nt),
        in_specs=[row, row, row, row, row, state,
                  pl.BlockSpec((1, HG_DIM), lambda b, t: (0, 0)),
                  pl.BlockSpec((tc, tc), lambda b, t: (0, 0)),
                  pl.BlockSpec((tc, tc), lambda b, t: (0, 0))],
        out_specs=[row, state],
        out_shape=[jax.ShapeDtypeStruct((rows, HG_WIDTH), BF16),
                   jax.ShapeDtypeStruct((nseq, HG_HEADS, HG_DIM, HG_DIM), F32)],
        scratch_shapes=[
            pltpu.VMEM((HG_HEADS, HG_DIM, HG_DIM), F32),
            pltpu.VMEM((tc, HG_WIDTH), BF16),
            pltpu.VMEM((tc, HG_WIDTH), BF16),
            pltpu.VMEM((tc, HG_WIDTH), F32),
            pltpu.VMEM((tc, HG_WIDTH), F32),
            pltpu.VMEM((tc, HG_WIDTH), F32),
        ],
        compiler_params=pltpu.CompilerParams(dimension_semantics=("arbitrary", "arbitrary"),
                                             vmem_limit_bytes=VMEM_LIMIT),
        name="hgrn",
    )(hq, hk, lfh, hi, shg, s0, norm, tri, ones)


def _mixffn_kernel(x_ref, of_ref, oh_ref, sga_ref, sgb_ref, hist_ref,
                   wbf_ref, wbh_ref, wout_ref, wup_ref, wdn_ref,
                   npost_ref, npre2_ref, npost2_ref, cw_ref, cb_ref,
                   y_ref, conv_ref, tail_ref, *, seg_len):
    tm = x_ref.shape[0]
    br_f = jnp.dot(of_ref[...], wbf_ref[...], preferred_element_type=F32)
    br_h = jnp.dot(oh_ref[...], wbh_ref[...], preferred_element_type=F32)
    merged = sga_ref[...].astype(F32) * br_f + sgb_ref[...].astype(F32) * br_h
    mix = jnp.dot(merged.astype(BF16), wout_ref[...], preferred_element_type=F32)
    x1 = x_ref[...] + _rms_scale(mix) * npost_ref[...]

    h2 = (_rms_scale(x1) * npre2_ref[...]).astype(BF16)
    up = jnp.dot(h2, wup_ref[...], preferred_element_type=F32)
    a = up[:, :D_FF]
    g = up[:, D_FF:]

    prev1 = pltpu.roll(a, 1, axis=0)
    prev2 = pltpu.roll(a, 2, axis=0)
    rowi = lax.broadcasted_iota(jnp.int32, a.shape, 0)
    if seg_len >= tm:
        @pl.when(pl.program_id(0) == 0)
        def _():
            tail_ref[...] = hist_ref[0]
        t0 = tail_ref[0:1, :]
        t1 = tail_ref[1:2, :]
        prev1 = jnp.where(rowi == 0, t1, prev1)
        prev2 = jnp.where(rowi == 0, t0, jnp.where(rowi == 1, t1, prev2))
        tail_ref[...] = a[tm - 2:, :]
        conv_ref[0] = a[tm - 2:, :]
    else:
        for s in range(tm // seg_len):
            h0 = hist_ref[s, 0:1, :]
            h1 = hist_ref[s, 1:2, :]
            prev1 = jnp.where(rowi == s * seg_len, h1, prev1)
            prev2 = jnp.where(rowi == s * seg_len, h0, jnp.where(rowi == s * seg_len + 1, h1, prev2))
            conv_ref[s] = a[(s + 1) * seg_len - 2:(s + 1) * seg_len, :]
    c = cb_ref[...] + cw_ref[0:1, :] * prev2 + cw_ref[1:2, :] * prev1 + cw_ref[2:3, :] * a
    act = (jax.nn.gelu(c, approximate=True) * g).astype(BF16)
    ff = jnp.dot(act, wdn_ref[...], preferred_element_type=F32)
    y_ref[...] = x1 + _rms_scale(ff) * npost2_ref[...]


def _mixffn(x, o_fox, o_hg, sga, sgb, hist, w, *, seg_len):
    rows = x.shape[0]
    tm = min(FFN_TM, rows)
    n = rows // tm
    nseg = hist.shape[0]
    row = lambda width: pl.BlockSpec((tm, width), lambda i: (i, 0))
    weights = [w["bf"], w["bh"], w["out"], w["up"], w["down"]]
    smalls = [w["npost"], w["npre2"], w["npost2"], w["conv_w"], w["conv_b"]]
    hist_spec = pl.BlockSpec(hist.shape, lambda i: (0, 0, 0))
    scratch = [pltpu.VMEM((2, D_FF), F32)]
    return pl.pallas_call(
        functools.partial(_mixffn_kernel, seg_len=seg_len),
        grid=(n,),
        in_specs=[row(D_MODEL), row(FOX_WIDTH), row(HG_WIDTH), row(D_MODEL), row(D_MODEL), hist_spec]
                 + [_const_spec(a.shape) for a in weights] + [_const_spec(a.shape) for a in smalls],
        out_specs=[row(D_MODEL), pl.BlockSpec((nseg, 2, D_FF), lambda i: (0, 0, 0))],
        out_shape=[jax.ShapeDtypeStruct((rows, D_MODEL), F32), jax.ShapeDtypeStruct((nseg, 2, D_FF), F32)],
        scratch_shapes=scratch,
        compiler_params=pltpu.CompilerParams(dimension_semantics=("arbitrary",), vmem_limit_bytes=VMEM_LIMIT),
        name="mixffn",
    )(x, o_fox, o_hg, sga, sgb, hist, *weights, *smalls)


def _prep_w_in(w_in, fox_f_bias):
    offs = np.cumsum([0] + IN_SIZES)
    seg = [w_in[:, int(offs[i]):int(offs[i + 1])] for i in range(len(IN_SIZES))]
    pad = jnp.zeros((D_MODEL, LANES - 3 * FOX_HEADS), w_in.dtype)
    f3 = jnp.concatenate([seg[3], seg[3], seg[3], pad], axis=1)
    w_all = jnp.concatenate(seg[:3] + [f3] + seg[4:], axis=1).astype(BF16)
    fb = fox_f_bias.astype(F32)
    fb3 = jnp.concatenate([fb, fb, fb, jnp.zeros((LANES - 3 * FOX_HEADS,), F32)]).reshape(1, LANES)
    return w_all, fb3


def kernel(x_prompt, x_sample, cache_fox_k, cache_fox_v, cache_fox_logf, state_hgrn, state_ffn_conv, norm_mix_pre, norm_mix_post, w_in, fox_f_bias, hgrn_lb_logits, hgrn_norm, w_branch_fox, w_branch_hgrn, w_out, norm_ffn_pre, norm_ffn_post, w_up, ffn_conv_w, ffn_conv_b, w_down):
    depth = w_in.shape[0]
    assert depth == 1 and hgrn_lb_logits.shape[0] == 2
    bp, seq, _ = x_prompt.shape
    assert bp == 1
    nb, t_new, _ = x_sample.shape
    past = cache_fox_k.shape[2]

    w_all, fb3 = _prep_w_in(w_in[0], fox_f_bias[0])
    g_pre = norm_mix_pre[0].reshape(1, D_MODEL)
    lbl = hgrn_lb_logits.astype(F32)
    hnorm = hgrn_norm[0].astype(F32).reshape(1, HG_DIM)
    w = {
        "bf": w_branch_fox[0].astype(BF16), "bh": w_branch_hgrn[0].astype(BF16), "out": w_out[0].astype(BF16),
        "up": w_up[0].astype(BF16), "down": w_down[0].astype(BF16),
        "npost": norm_mix_post[0].reshape(1, D_MODEL), "npre2": norm_ffn_pre[0].reshape(1, D_MODEL),
        "npost2": norm_ffn_post[0].reshape(1, D_MODEL),
        "conv_w": ffn_conv_w[0], "conv_b": ffn_conv_b[0].reshape(1, D_FF),
    }

    xp = x_prompt.reshape(seq, D_MODEL)
    (qt, kh, vt, stats, pk, pv, plf, hq, hk, lfh, hi, shg, sga, sgb) = _proj(xp, g_pre, w_all, fb3, lbl, fold=True)
    o_fox = _fox_prompt(qt, kh, vt, stats)
    s0 = jnp.zeros((1, HG_HEADS, HG_DIM, HG_DIM), F32)
    o_hg, p_state = _hgrn(hq, hk, lfh, hi, shg, s0, hnorm, nseq=1)
    hist0 = jnp.zeros((1, 2, D_FF), F32)
    yp, pconv = _mixffn(xp, o_fox, o_hg, sga, sgb, hist0, w, seg_len=seq)

    xs = x_sample.reshape(nb * t_new, D_MODEL)
    (qs, sk, sv, slf, hq, hk, lfh, hi, shg, sga, sgb) = _proj(xs, g_pre, w_all, fb3, lbl, fold=False)
    lf_all_t = jnp.concatenate([
        jnp.swapaxes(cache_fox_logf[0].astype(F32), 1, 2),
        jnp.swapaxes(slf.reshape(nb, t_new, FOX_HEADS), 1, 2),
        jnp.zeros((nb, FOX_HEADS, LANES - t_new), F32)], axis=2)
    o_fox_s = _fox_sample(qs, sk, sv, lf_all_t,
                          cache_fox_k[0].reshape(nb, past, FOX_WIDTH), cache_fox_v[0].reshape(nb, past, FOX_WIDTH),
                          t_new=t_new)
    o_hg_s, s_state = _hgrn(hq, hk, lfh, hi, shg, state_hgrn[0].astype(F32), hnorm, nseq=nb)
    ys, sconv = _mixffn(xs, o_fox_s, o_hg_s, sga, sgb, state_ffn_conv[0], w, seg_len=t_new)

    return (
        yp.reshape(bp, seq, D_MODEL),
        ys.reshape(nb, t_new, D_MODEL),
        pk.reshape(1, bp, seq, FOX_HEADS, FOX_HEAD_DIM),
        pv.reshape(1, bp, seq, FOX_HEADS, FOX_HEAD_DIM),
        plf.reshape(1, bp, seq, FOX_HEADS),
        p_state.reshape(1, bp, HG_HEADS, HG_DIM, HG_DIM),
        pconv.reshape(1, bp, 2, D_FF),
        sk.reshape(1, nb, t_new, FOX_HEADS, FOX_HEAD_DIM),
        sv.reshape(1, nb, t_new, FOX_HEADS, FOX_HEAD_DIM),
        slf.reshape(1, nb, t_new, FOX_HEADS),
        s_state.reshape(1, nb, HG_HEADS, HG_DIM, HG_DIM),
        sconv.reshape(1, nb, 2, D_FF),
    )
```

```python
import functools

import numpy as np
import jax
import jax.numpy as jnp
from jax import lax
from jax.experimental import pallas as pl
from jax.experimental.pallas import tpu as pltpu

F32 = jnp.float32
BF16 = jnp.bfloat16

D_MODEL = 1024
FOX_HEADS = 8
FOX_HEAD_DIM = 64
FOX_WIDTH = FOX_HEADS * FOX_HEAD_DIM
HG_HEADS = 4
HG_DIM = 128
HG_WIDTH = HG_HEADS * HG_DIM
D_FF = 2816
RMS_EPS = 1e-6
NEG_INF = -1e30
LOG2E = 1.4426950408889634
FOX_SKIP_NATS = 110.0
IN_SIZES = [FOX_WIDTH, FOX_WIDTH, FOX_WIDTH, FOX_HEADS, HG_WIDTH, HG_WIDTH, HG_WIDTH, HG_WIDTH, D_MODEL, D_MODEL]

LANES = 128
FOX_PAD = 2 * FOX_HEAD_DIM
HG_BLOCK = 16
VMEM_LIMIT = 56 * 1024 * 1024

PROJ_TM = 256
FOX_TQ = 512
FOX_TK = 512
HG_TC = 256
FFN_TM = 256
SAMPLE_TK = 1024

_C_Q, _C_K, _C_V, _C_F = 0, 512, 1024, 1536
_C_HQ, _C_HF, _C_HI, _C_HG = 1664, 2176, 2688, 3200
_C_GA, _C_GB, _C_END = 3712, 4736, 5760


def _split3(x):
    hi = x.astype(BF16)
    r = x - hi.astype(F32)
    mid = r.astype(BF16)
    lo = (r - mid.astype(F32)).astype(BF16)
    return hi, mid, lo


def _sum_by_01_matrix(mat01, x):
    cat = jnp.concatenate(_split3(x), axis=1)
    y = jnp.dot(mat01, cat, preferred_element_type=F32)
    return y[:, :LANES] + y[:, LANES:2 * LANES] + y[:, 2 * LANES:]


def _rms_scale(x):
    return x * lax.rsqrt(jnp.mean(x * x, axis=-1, keepdims=True) + RMS_EPS)


def _log_sigmoid(x):
    return jnp.minimum(x, 0.0) - jnp.log1p(jnp.exp(-jnp.abs(x)))


def _sigmoid(x):
    return 1.0 / (1.0 + jnp.exp(-x))


def _proj_kernel(*refs, fold):
    if fold:
        (x_ref, g_ref, w_ref, fb_ref, lbl_ref, tri_ref, pq_ref, pk_ref, cq_ref, ck_ref, cv_ref, seg_ref,
         qh_ref, kh_ref, vh_ref, stat_ref, kout_ref, vout_ref, lf_ref, hq_ref, hk_ref, lfh_ref, hi_ref, shg_ref,
         sga_ref, sgb_ref, carry_ref) = refs
    else:
        (x_ref, g_ref, w_ref, fb_ref, lbl_ref,
         qs_ref, kout_ref, vout_ref, lf_ref, hq_ref, hk_ref, lfh_ref, hi_ref, shg_ref,
         sga_ref, sgb_ref) = refs

    h = (_rms_scale(x_ref[...]) * g_ref[...]).astype(BF16)
    z = jnp.dot(h, w_ref[...], preferred_element_type=F32)

    zq = z[:, _C_Q:_C_K] * (FOX_HEAD_DIM ** -0.5)
    zk = z[:, _C_K:_C_V]
    zv = z[:, _C_V:_C_F]
    kout_ref[...] = zk
    vout_ref[...] = zv
    logf = _log_sigmoid(z[:, _C_F:_C_HQ] + fb_ref[...])
    lf_ref[...] = logf[:, :FOX_HEADS]

    l0 = lbl_ref[0:1, :]
    l1 = lbl_ref[1:2, :]
    lmax = jnp.maximum(l0, l1)
    e0 = jnp.exp(l0 - lmax)
    lb = e0 / (e0 + jnp.exp(l1 - lmax))
    f = lb + (1.0 - lb) * _sigmoid(z[:, _C_HF:_C_HI])
    hq_ref[...] = z[:, _C_HQ:_C_HF].astype(BF16)
    hk_ref[...] = (1.0 - f).astype(BF16)
    lfh_ref[...] = jnp.log(f)
    hi_ref[...] = z[:, _C_HI:_C_HG].astype(BF16)
    shg_ref[...] = _sigmoid(z[:, _C_HG:_C_GA]).astype(BF16)
    sga_ref[...] = _sigmoid(z[:, _C_GA:_C_GB]).astype(BF16)
    sgb_ref[...] = _sigmoid(z[:, _C_GB:_C_END]).astype(BF16)

    if not fold:
        qs_ref[...] = zq.astype(BF16)
        return

    @pl.when(pl.program_id(0) == 0)
    def _():
        carry_ref[...] = jnp.zeros_like(carry_ref)

    cum = carry_ref[...] + _sum_by_01_matrix(tri_ref[...], logf)
    carry_ref[...] = cum[-1:, :]

    seg = seg_ref[...]
    qn2 = jnp.dot((zq * zq).astype(BF16), seg, preferred_element_type=F32)
    kn2 = jnp.dot((zk * zk).astype(BF16), seg, preferred_element_type=F32)
    dg = jnp.dot((zq * zk).astype(BF16), seg, preferred_element_type=F32)
    stat_ref[0, 0:1, :] = jnp.max(qn2, axis=0, keepdims=True)
    stat_ref[0, 1:2, :] = jnp.max(kn2, axis=0, keepdims=True)
    stat_ref[0, 2:3, :] = jnp.min(dg, axis=0, keepdims=True)
    stat_ref[0, 3:4, :] = cum[0:1, :]
    stat_ref[0, 4:5, :] = cum[-1:, :]
    stat_ref[0, 5:8, :] = jnp.zeros((3, LANES), F32)

    zq = zq * LOG2E
    c_hi, c_mid, c_lo = _split3(cum * LOG2E)
    lane = lax.broadcasted_iota(jnp.int32, cum.shape, 1)
    pieces = jnp.where(lane < 8, c_hi, jnp.where(lane < 16, c_mid, c_lo))
    pieces = jnp.where(lane < 24, pieces, jnp.zeros_like(pieces))
    ex_q = jnp.dot(pieces, pq_ref[...], preferred_element_type=F32) + cq_ref[...]
    ex_k = jnp.dot(pieces, pk_ref[...], preferred_element_type=F32) + ck_ref[...]
    ex_v = cv_ref[...]

    low = lax.broadcasted_iota(jnp.int32, (zq.shape[0], LANES), 1) < FOX_HEAD_DIM
    for src, ex, dst, transposed in ((zq, ex_q, qh_ref, True), (zk, ex_k, kh_ref, False), (zv, ex_v, vh_ref, True)):
        for c in range(FOX_WIDTH // LANES):
            pair = src[:, c * LANES:(c + 1) * LANES]
            swapped = pltpu.roll(pair, FOX_HEAD_DIM, axis=1)
            for j, data in enumerate((pair, swapped)):
                hd = 2 * c + j
                blk = jnp.where(low, data, ex[:, hd * LANES:(hd + 1) * LANES])
                dst[hd] = (blk.T if transposed else blk).astype(BF16)


def _bias_fold_constants():
    pq = np.zeros((LANES, FOX_HEADS * LANES), np.float32)
    pk = np.zeros((LANES, FOX_HEADS * LANES), np.float32)
    cq = np.zeros((1, FOX_HEADS * LANES), np.float32)
    ck = np.zeros((1, FOX_HEADS * LANES), np.float32)
    cv = np.zeros((1, FOX_HEADS * LANES), np.float32)
    for h in range(FOX_HEADS):
        base = h * LANES + FOX_HEAD_DIM
        for p in range(3):
            pq[p * 8 + h, base + p] = 1.0
            ck[0, base + p] = 1.0
            pk[p * 8 + h, base + 3 + p] = -1.0
            cq[0, base + 3 + p] = 1.0
        cv[0, base] = 1.0
    return (jnp.asarray(pq, BF16), jnp.asarray(pk, BF16), jnp.asarray(cq), jnp.asarray(ck), jnp.asarray(cv))


def _const_spec(shape, single=True):
    nd = len(shape)
    if single:
        return pl.BlockSpec(shape, lambda *_: (0,) * nd, pipeline_mode=pl.Buffered(1))
    return pl.BlockSpec(shape, lambda *_: (0,) * nd)


def _proj(x, gain, w_all, fb3, lb_logits, *, fold):
    rows = x.shape[0]
    tm = min(PROJ_TM, rows)
    n = rows // tm
    row = lambda width: pl.BlockSpec((tm, width), lambda i: (i, 0))
    in_specs = [row(D_MODEL), _const_spec((1, D_MODEL)), _const_spec(w_all.shape), _const_spec((1, LANES)),
                _const_spec(lb_logits.shape)]
    args = [x, gain, w_all, fb3, lb_logits]
    common_out = [
        (jax.ShapeDtypeStruct((rows, FOX_WIDTH), F32), row(FOX_WIDTH)),
        (jax.ShapeDtypeStruct((rows, FOX_WIDTH), F32), row(FOX_WIDTH)),
        (jax.ShapeDtypeStruct((rows, FOX_HEADS), F32), row(FOX_HEADS)),
        (jax.ShapeDtypeStruct((rows, HG_WIDTH), BF16), row(HG_WIDTH)),
        (jax.ShapeDtypeStruct((rows, HG_WIDTH), BF16), row(HG_WIDTH)),
        (jax.ShapeDtypeStruct((rows, HG_WIDTH), F32), row(HG_WIDTH)),
        (jax.ShapeDtypeStruct((rows, HG_WIDTH), BF16), row(HG_WIDTH)),
        (jax.ShapeDtypeStruct((rows, HG_WIDTH), BF16), row(HG_WIDTH)),
        (jax.ShapeDtypeStruct((rows, D_MODEL), BF16), row(D_MODEL)),
        (jax.ShapeDtypeStruct((rows, D_MODEL), BF16), row(D_MODEL)),
    ]
    scratch = []
    if fold:
        tri = jnp.asarray(np.tril(np.ones((tm, tm), np.float32)), BF16)
        seg = np.zeros((FOX_WIDTH, LANES), np.float32)
        seg[np.arange(FOX_WIDTH), np.arange(FOX_WIDTH) // FOX_HEAD_DIM] = 1.0
        consts = _bias_fold_constants() + (jnp.asarray(seg, BF16),)
        in_specs += [_const_spec(tri.shape)] + [_const_spec(c.shape) for c in consts]
        args += [tri, *consts]
        head_major = (jax.ShapeDtypeStruct((FOX_HEADS, rows, FOX_PAD), BF16),
                      pl.BlockSpec((FOX_HEADS, tm, FOX_PAD), lambda i: (0, i, 0)))
        head_major_t = (jax.ShapeDtypeStruct((FOX_HEADS, FOX_PAD, rows), BF16),
                        pl.BlockSpec((FOX_HEADS, FOX_PAD, tm), lambda i: (0, 0, i)))
        stats = (jax.ShapeDtypeStruct((n, 8, LANES), F32), pl.BlockSpec((1, 8, LANES), lambda i: (i, 0, 0)))
        outs = [head_major_t, head_major, head_major_t, stats] + common_out
        scratch = [pltpu.VMEM((1, LANES), F32)]
    else:
        outs = [(jax.ShapeDtypeStruct((rows, FOX_WIDTH), BF16), row(FOX_WIDTH))] + common_out
    return pl.pallas_call(
        functools.partial(_proj_kernel, fold=fold),
        grid=(n,),
        in_specs=in_specs,
        out_specs=[o[1] for o in outs],
        out_shape=[o[0] for o in outs],
        scratch_shapes=scratch,
        compiler_params=pltpu.CompilerParams(dimension_semantics=("arbitrary",), vmem_limit_bytes=VMEM_LIMIT),
        name="proj_fold" if fold else "proj",
    )(*args)


def _fox_kernel(qi_ref, ki_ref, mode_ref, qt_ref, k_ref, vt_ref, o_ref, m_ref, acc_ref):
    mode = mode_ref[pl.program_id(0)]
    tq = qt_ref.shape[2]
    tk = k_ref.shape[1]

    @pl.when((mode & _MODE_FIRST) != 0)
    def _():
        m_ref[...] = jnp.full_like(m_ref, NEG_INF)
        acc_ref[...] = jnp.zeros_like(acc_ref)

    def head_step(h, masked):
        s = jnp.dot(k_ref[h], qt_ref[h], preferred_element_type=F32)
        if masked:
            key = lax.broadcasted_iota(jnp.int32, (tk, tq), 0)
            qry = lax.broadcasted_iota(jnp.int32, (tk, tq), 1)
            s = jnp.where(key <= qry, s, NEG_INF)
        m_prev = m_ref[h]
        m_new = jnp.maximum(m_prev, jnp.max(s, axis=0, keepdims=True))
        alpha = jnp.exp2(m_prev - m_new)
        p = jnp.exp2(s - m_new).astype(BF16)
        acc_ref[h] = alpha * acc_ref[h] + jnp.dot(vt_ref[h], p, preferred_element_type=F32)
        m_ref[h] = m_new

    @pl.when((mode & _MODE_PAIR) != 0)
    def _():
        for h in range(FOX_HEADS):
            head_step(h, False)

    @pl.when((mode & _MODE_DIAG) != 0)
    def _():
        for h in range(FOX_HEADS):
            head_step(h, True)
        for c in range(FOX_WIDTH // LANES):
            halves = []
            for hd in (2 * c, 2 * c + 1):
                acc = acc_ref[hd]
                halves.append(acc[:FOX_HEAD_DIM, :] / acc[FOX_HEAD_DIM:FOX_HEAD_DIM + 1, :])
            o_ref[:, c * LANES:(c + 1) * LANES] = jnp.concatenate(halves, axis=0).T.astype(o_ref.dtype)


_MODE_FIRST, _MODE_PAIR, _MODE_DIAG = 1, 2, 4


def _fox_schedule(stats, nq):
    st = stats[:, :5, :FOX_HEADS].reshape(nq, -1, 5, FOX_HEADS)
    qn = jnp.sqrt(jnp.max(st[:, :, 0], axis=1)) * 1.01
    kn = jnp.sqrt(jnp.max(st[:, :, 1], axis=1)) * 1.01
    dmin = jnp.min(st[:, :, 2], axis=1)
    c_first = st[:, 0, 3]
    c_last = st[:, -1, 4]
    bound = qn[:, None] * kn[None, :] + (c_first - dmin)[:, None] - c_last[None, :]
    blk = jnp.arange(nq, dtype=jnp.int32)
    drop = jnp.all(bound < -FOX_SKIP_NATS, axis=-1) & (blk[None, :] < blk[:, None])
    prefix = jnp.cumsum(jnp.logical_not(drop).astype(jnp.int32), axis=1) == 0
    kstart = jnp.sum(prefix.astype(jnp.int32), axis=1)
    count = blk - kstart + 1
    ends = jnp.cumsum(count)
    nsteps = nq * (nq + 1) // 2
    step = jnp.arange(nsteps, dtype=jnp.int32)
    valid = step < ends[-1]
    done = ends[None, :] <= step[:, None]
    q_of = jnp.minimum(jnp.sum(done.astype(jnp.int32), axis=1), nq - 1)
    begin = jnp.sum(jnp.where(done, count[None, :], 0), axis=1)
    k_first = jnp.sum(jnp.where(blk[None, :] == q_of[:, None], kstart[None, :], 0), axis=1)
    k_of = k_first + (step - begin)
    mode = jnp.where(k_of == q_of, _MODE_DIAG, _MODE_PAIR) + jnp.where(k_of == k_first, _MODE_FIRST, 0)
    mode = jnp.where(valid, mode, 0)
    q_of = jnp.where(valid, q_of, nq - 1)
    k_of = jnp.where(valid, k_of, nq - 1)
    return q_of.astype(jnp.int32), k_of.astype(jnp.int32), mode.astype(jnp.int32)


def _fox_prompt(qt, kh, vt, stats):
    seq = kh.shape[1]
    assert FOX_TQ == FOX_TK
    nq = seq // FOX_TQ
    qi, ki, mode = _fox_schedule(stats, nq)
    grid_spec = pltpu.PrefetchScalarGridSpec(
        num_scalar_prefetch=3,
        grid=(nq * (nq + 1) // 2,),
        in_specs=[
            pl.BlockSpec((FOX_HEADS, FOX_PAD, FOX_TQ), lambda s, qi, ki, mode: (0, 0, qi[s])),
            pl.BlockSpec((FOX_HEADS, FOX_TK, FOX_PAD), lambda s, qi, ki, mode: (0, ki[s], 0)),
            pl.BlockSpec((FOX_HEADS, FOX_PAD, FOX_TK), lambda s, qi, ki, mode: (0, 0, ki[s])),
        ],
        out_specs=pl.BlockSpec((FOX_TQ, FOX_WIDTH), lambda s, qi, ki, mode: (qi[s], 0)),
        scratch_shapes=[pltpu.VMEM((FOX_HEADS, 1, FOX_TQ), F32),
                        pltpu.VMEM((FOX_HEADS, FOX_PAD, FOX_TQ), F32)],
    )
    return pl.pallas_call(
        _fox_kernel,
        grid_spec=grid_spec,
        out_shape=jax.ShapeDtypeStruct((seq, FOX_WIDTH), BF16),
        compiler_params=pltpu.CompilerParams(dimension_semantics=("arbitrary",), vmem_limit_bytes=VMEM_LIMIT),
        name="fox_prompt",
    )(qi, ki, mode, qt, kh, vt)


def _lane_cumsum(x):
    n = x.shape[1]
    lane = lax.broadcasted_iota(jnp.int32, x.shape, 1)
    shift = 1
    while shift < n:
        x = x + jnp.where(lane >= shift, pltpu.roll(x, shift, axis=1), 0.0)
        shift *= 2
    return x


def _fox_sample_kernel(q_ref, kn_ref, vn_ref, lft_ref, ck_ref, cv_ref, o_ref,
                       qbd_ref, cq_ref, cum_ref, m_ref, l_ref, acc_ref, *, n_past_chunks, tk, t_new):
    c = pl.program_id(1)
    rows = FOX_HEADS * t_new
    lane_head = lax.broadcasted_iota(jnp.int32, (t_new, FOX_WIDTH), 1) // FOX_HEAD_DIM

    @pl.when(c == 0)
    def _():
        cum = _lane_cumsum(lft_ref[0])
        for j in range(n_past_chunks):
            cum_ref[j] = cum[:, j * tk:(j + 1) * tk]
        new_cum = cum[:, n_past_chunks * tk:n_past_chunks * tk + LANES]
        cum_ref[n_past_chunks, :, :LANES] = new_cum
        new_cum_t = jnp.concatenate([new_cum] * (LANES // FOX_HEADS), axis=0).T
        q = q_ref[...]
        for h in range(FOX_HEADS):
            qbd_ref[h * t_new:(h + 1) * t_new, :] = jnp.where(lane_head == h, q, jnp.zeros_like(q))
            cq_ref[h * t_new:(h + 1) * t_new, :] = jnp.broadcast_to(new_cum_t[:t_new, h:h + 1], (t_new, LANES))
        m_ref[...] = jnp.full_like(m_ref, NEG_INF)
        l_ref[...] = jnp.zeros_like(l_ref)
        acc_ref[...] = jnp.zeros_like(acc_ref)

    def update(k, v, ck_rows, mask):
        s = lax.dot_general(qbd_ref[...], k, (((1,), (1,)), ((), ())), preferred_element_type=F32)
        s = s + (cq_ref[:, :1] - ck_rows)
        if mask is not None:
            s = jnp.where(mask, s, NEG_INF)
        m_prev = m_ref[...]
        m_new = jnp.maximum(m_prev, jnp.max(s, axis=1, keepdims=True))
        alpha = jnp.exp(m_prev - m_new)
        p = jnp.exp(s - m_new[:, :1])
        l_ref[...] = alpha * l_ref[...] + jnp.sum(p, axis=1, keepdims=True)
        acc_ref[...] = alpha[:, :1] * acc_ref[...] + jnp.dot(p.astype(BF16), v, preferred_element_type=F32)
        m_ref[...] = m_new

    def expand_rows(x, width):
        return jnp.concatenate([jnp.broadcast_to(x[h:h + 1, :], (t_new, width)) for h in range(FOX_HEADS)], axis=0)

    @pl.when(c < n_past_chunks)
    def _():
        update(ck_ref[0].astype(BF16), cv_ref[0].astype(BF16), expand_rows(cum_ref[c], tk), None)

    @pl.when(c == n_past_chunks)
    def _():
        ck_rows = expand_rows(cum_ref[n_past_chunks, :, :LANES], LANES)[:, :t_new]
        rowt = lax.broadcasted_iota(jnp.int32, (rows, t_new), 0) % t_new
        coli = lax.broadcasted_iota(jnp.int32, (rows, t_new), 1)
        update(kn_ref[...].astype(BF16), vn_ref[...].astype(BF16), ck_rows, coli <= rowt)
        out = acc_ref[...] / l_ref[:, :1]
        o = jnp.zeros((t_new, FOX_WIDTH), F32)
        for h in range(FOX_HEADS):
            o = o + jnp.where(lane_head == h, out[h * t_new:(h + 1) * t_new, :], 0.0)
        o_ref[...] = o.astype(o_ref.dtype)


def _fox_sample(qs, k_new, v_new, lf_all_t, cache_k, cache_v, *, t_new):
    nb, past = cache_k.shape[0], cache_k.shape[1]
    tk = SAMPLE_TK
    npc = past // tk
    rows = FOX_HEADS * t_new
    last = npc - 1
    kern = functools.partial(_fox_sample_kernel, n_past_chunks=npc, tk=tk, t_new=t_new)
    return pl.pallas_call(
        kern,
        grid=(nb, npc + 1),
        in_specs=[
            pl.BlockSpec((t_new, FOX_WIDTH), lambda b, c: (b, 0)),
            pl.BlockSpec((t_new, FOX_WIDTH), lambda b, c: (b, 0)),
            pl.BlockSpec((t_new, FOX_WIDTH), lambda b, c: (b, 0)),
            pl.BlockSpec((1, FOX_HEADS, past + LANES), lambda b, c: (b, 0, 0)),
            pl.BlockSpec((1, tk, FOX_WIDTH), lambda b, c: (b, jnp.minimum(c, last), 0)),
            pl.BlockSpec((1, tk, FOX_WIDTH), lambda b, c: (b, jnp.minimum(c, last), 0)),
        ],
        out_specs=pl.BlockSpec((t_new, FOX_WIDTH), lambda b, c: (b, 0)),
        out_shape=jax.ShapeDtypeStruct((nb * t_new, FOX_WIDTH), BF16),
        scratch_shapes=[
            pltpu.VMEM((rows, FOX_WIDTH), BF16),
            pltpu.VMEM((rows, LANES), F32),
            pltpu.VMEM((npc + 1, FOX_HEADS, tk), F32),
            pltpu.VMEM((rows, LANES), F32),
            pltpu.VMEM((rows, LANES), F32),
            pltpu.VMEM((rows, FOX_WIDTH), F32),
        ],
        compiler_params=pltpu.CompilerParams(dimension_semantics=("arbitrary", "arbitrary"),
                                             vmem_limit_bytes=VMEM_LIMIT),
        name="fox_sample",
    )(qs, k_new, v_new, lf_all_t, cache_k, cache_v)


def _hgrn_kernel(hq_ref, hk_ref, lfh_ref, hi_ref, shg_ref, s0_ref, norm_ref, tri_ref, ones_ref,
                 o_ref, sout_ref, st_ref, qq_ref, kk_ref, eb_ref, oin_ref, od_ref):
    t = pl.program_id(1)
    tc = hq_ref.shape[0]
    blk = min(HG_BLOCK, tc)

    @pl.when(t == 0)
    def _():
        for h in range(HG_HEADS):
            st_ref[h] = s0_ref[0, h].T

    row_in_blk = lax.broadcasted_iota(jnp.int32, (tc, HG_DIM), 0) % blk
    for h in range(HG_HEADS):
        sl = slice(h * HG_DIM, (h + 1) * HG_DIM)
        lf = lfh_ref[:, sl]
        b = _sum_by_01_matrix(tri_ref[...], lf)
        b_end = _sum_by_01_matrix(ones_ref[...], lf)
        eb = jnp.exp(b)
        q = hq_ref[:, sl].astype(F32)
        k = hk_ref[:, sl].astype(F32)
        v = hi_ref[:, sl].astype(F32)
        qq_ref[:, sl] = (q * eb).astype(BF16)
        kk_ref[:, sl] = (k * jnp.exp(b_end - b)).astype(BF16)
        eb_ref[:, sl] = eb
        od = jnp.sum(q * k, axis=1, keepdims=True) * v
        for lag in range(1, blk):
            k_l = pltpu.roll(k, lag, axis=0)
            b_l = pltpu.roll(b, lag, axis=0)
            v_l = pltpu.roll(v, lag, axis=0)
            w = q * k_l * jnp.exp(jnp.minimum(b - b_l, 0.0))
            w = jnp.where(row_in_blk >= lag, w, 0.0)
            od = od + jnp.sum(w, axis=1, keepdims=True) * v_l
        od_ref[:, sl] = od

    def block_step(j, carry):
        r0 = pl.multiple_of(j * blk, blk)
        for h in range(HG_HEADS):
            sl = slice(h * HG_DIM, (h + 1) * HG_DIM)
            st = st_ref[h]
            oin_ref[pl.ds(r0, blk), sl] = lax.dot_general(
                qq_ref[pl.ds(r0, blk), sl], st.astype(BF16), (((1,), (1,)), ((), ())), preferred_element_type=F32)
            upd = lax.dot_general(hi_ref[pl.ds(r0, blk), sl], kk_ref[pl.ds(r0, blk), sl],
                                  (((0,), (0,)), ((), ())), preferred_element_type=F32)
            tail = eb_ref[pl.ds(pl.multiple_of(r0 + blk - 8, 8), 8), sl]
            st_ref[h] = st * tail[7:8, :] + upd
        return carry

    lax.fori_loop(0, tc // blk, block_step, 0)

    for h in range(HG_HEADS):
        sl = slice(h * HG_DIM, (h + 1) * HG_DIM)
        o = oin_ref[:, sl] + od_ref[:, sl]
        y = (_rms_scale(o) * norm_ref[...]) * shg_ref[:, sl].astype(F32)
        o_ref[:, sl] = y.astype(o_ref.dtype)

    @pl.when(t == pl.num_programs(1) - 1)
    def _():
        for h in range(HG_HEADS):
            sout_ref[0, h] = st_ref[h].T


def _block_diag_01(n, blk, lower):
    r = np.arange(n)
    same = (r[:, None] // blk) == (r[None, :] // blk)
    if lower:
        same = same & (r[None, :] <= r[:, None])
    return jnp.asarray(same.astype(np.float32), BF16)


def _hgrn(hq, hk, lfh, hi, shg, s0, norm, *, nseq):
    rows = hq.shape[0]
    t_len = rows // nseq
    tc = min(HG_TC, t_len)
    nt = t_len // tc
    blk = min(HG_BLOCK, tc)
    tri = _block_diag_01(tc, blk, True)
    ones = _block_diag_01(tc, blk, False)
    row = pl.BlockSpec((tc, HG_WIDTH), lambda b, t: (b * nt + t, 0))
    state = pl.BlockSpec((1, HG_HEADS, HG_DIM, HG_DIM), lambda b, t: (b, 0, 0, 0))
    return pl.pallas_call(
        _hgrn_kernel,
        grid=(nseq, nt),
        in_specs=[row, row, row, row, row, state,
                  pl.BlockSpec((1, HG_DIM), lambda b, t: (0, 0)),
                  pl.BlockSpec((tc, tc), lambda b, t: (0, 0)),
                  pl.BlockSpec((tc, tc), lambda b, t: (0, 0))],
        out_specs=[row, state],
        out_shape=[jax.ShapeDtypeStruct((rows, HG_WIDTH), BF16),
                   jax.ShapeDtypeStruct((nseq, HG_HEADS, HG_DIM, HG_DIM), F32)],
        scratch_shapes=[
            pltpu.VMEM((HG_HEADS, HG_DIM, HG_DIM), F32),
            pltpu.VMEM((tc, HG_WIDTH), BF16),
            pltpu.VMEM((tc, HG_WIDTH), BF16),
            pltpu.VMEM((tc, HG_WIDTH), F32),
            pltpu.VMEM((tc, HG_WIDTH), F32),
            pltpu.VMEM((tc, HG_WIDTH), F32),
        ],
        compiler_params=pltpu.CompilerParams(dimension_semantics=("arbitrary", "arbitrary"),
                                             vmem_limit_bytes=VMEM_LIMIT),
        name="hgrn",
    )(hq, hk, lfh, hi, shg, s0, norm, tri, ones)


def _mixffn_kernel(x_ref, of_ref, oh_ref, sga_ref, sgb_ref, hist_ref,
                   wbf_ref, wbh_ref, wout_ref, wup_ref, wdn_ref,
                   npost_ref, npre2_ref, npost2_ref, cw_ref, cb_ref,
                   y_ref, conv_ref, tail_ref, *, seg_len):
    tm = x_ref.shape[0]
    br_f = jnp.dot(of_ref[...], wbf_ref[...], preferred_element_type=F32)
    br_h = jnp.dot(oh_ref[...], wbh_ref[...], preferred_element_type=F32)
    merged = sga_ref[...].astype(F32) * br_f + sgb_ref[...].astype(F32) * br_h
    mix = jnp.dot(merged.astype(BF16), wout_ref[...], preferred_element_type=F32)
    x1 = x_ref[...] + _rms_scale(mix) * npost_ref[...]

    h2 = (_rms_scale(x1) * npre2_ref[...]).astype(BF16)
    up = jnp.dot(h2, wup_ref[...], preferred_element_type=F32)
    a = up[:, :D_FF]
    g = up[:, D_FF:]

    prev1 = pltpu.roll(a, 1, axis=0)
    prev2 = pltpu.roll(a, 2, axis=0)
    rowi = lax.broadcasted_iota(jnp.int32, a.shape, 0)
    if seg_len >= tm:
        @pl.when(pl.program_id(0) == 0)
        def _():
            tail_ref[...] = hist_ref[0]
        t0 = tail_ref[0:1, :]
        t1 = tail_ref[1:2, :]
        prev1 = jnp.where(rowi == 0, t1, prev1)
        prev2 = jnp.where(rowi == 0, t0, jnp.where(rowi == 1, t1, prev2))
        tail_ref[...] = a[tm - 2:, :]
        conv_ref[0] = a[tm - 2:, :]
    else:
        for s in range(tm // seg_len):
            h0 = hist_ref[s, 0:1, :]
            h1 = hist_ref[s, 1:2, :]
            prev1 = jnp.where(rowi == s * seg_len, h1, prev1)
            prev2 = jnp.where(rowi == s * seg_len, h0, jnp.where(rowi == s * seg_len + 1, h1, prev2))
            conv_ref[s] = a[(s + 1) * seg_len - 2:(s + 1) * seg_len, :]
    c = cb_ref[...] + cw_ref[0:1, :] * prev2 + cw_ref[1:2, :] * prev1 + cw_ref[2:3, :] * a
    act = (jax.nn.gelu(c, approximate=True) * g).astype(BF16)
    ff = jnp.dot(act, wdn_ref[...], preferred_element_type=F32)
    y_ref[...] = x1 + _rms_scale(ff) * npost2_ref[...]


def _mixffn(x, o_fox, o_hg, sga, sgb, hist, w, *, seg_len):
    rows = x.shape[0]
    tm = min(FFN_TM, rows)
    n = rows // tm
    nseg = hist.shape[0]
    row = lambda width: pl.BlockSpec((tm, width), lambda i: (i, 0))
    weights = [w["bf"], w["bh"], w["out"], w["up"], w["down"]]
    smalls = [w["npost"], w["npre2"], w["npost2"], w["conv_w"], w["conv_b"]]
    hist_spec = pl.BlockSpec(hist.shape, lambda i: (0, 0, 0))
    scratch = [pltpu.VMEM((2, D_FF), F32)]
    return pl.pallas_call(
        functools.partial(_mixffn_kernel, seg_len=seg_len),
        grid=(n,),
        in_specs=[row(D_MODEL), row(FOX_WIDTH), row(HG_WIDTH), row(D_MODEL), row(D_MODEL), hist_spec]
                 + [_const_spec(a.shape) for a in weights] + [_const_spec(a.shape) for a in smalls],
        out_specs=[row(D_MODEL), pl.BlockSpec((nseg, 2, D_FF), lambda i: (0, 0, 0))],
        out_shape=[jax.ShapeDtypeStruct((rows, D_MODEL), F32), jax.ShapeDtypeStruct((nseg, 2, D_FF), F32)],
        scratch_shapes=scratch,
        compiler_params=pltpu.CompilerParams(dimension_semantics=("arbitrary",), vmem_limit_bytes=VMEM_LIMIT),
        name="mixffn",
    )(x, o_fox, o_hg, sga, sgb, hist, *weights, *smalls)


def _prep_w_in(w_in, fox_f_bias):
    offs = np.cumsum([0] + IN_SIZES)
    seg = [w_in[:, int(offs[i]):int(offs[i + 1])] for i in range(len(IN_SIZES))]
    pad = jnp.zeros((D_MODEL, LANES - 3 * FOX_HEADS), w_in.dtype)
    f3 = jnp.concatenate([seg[3], seg[3], seg[3], pad], axis=1)
    w_all = jnp.concatenate(seg[:3] + [f3] + seg[4:], axis=1).astype(BF16)
    fb = fox_f_bias.astype(F32)
    fb3 = jnp.concatenate([fb, fb, fb, jnp.zeros((LANES - 3 * FOX_HEADS,), F32)]).reshape(1, LANES)
    return w_all, fb3


def kernel(x_prompt, x_sample, cache_fox_k, cache_fox_v, cache_fox_logf, state_hgrn, state_ffn_conv, norm_mix_pre, norm_mix_post, w_in, fox_f_bias, hgrn_lb_logits, hgrn_norm, w_branch_fox, w_branch_hgrn, w_out, norm_ffn_pre, norm_ffn_post, w_up, ffn_conv_w, ffn_conv_b, w_down):
    depth = w_in.shape[0]
    assert depth == 1 and hgrn_lb_logits.shape[0] == 2
    bp, seq, _ = x_prompt.shape
    assert bp == 1
    nb, t_new, _ = x_sample.shape
    past = cache_fox_k.shape[2]

    w_all, fb3 = _prep_w_in(w_in[0], fox_f_bias[0])
    g_pre = norm_mix_pre[0].reshape(1, D_MODEL)
    lbl = hgrn_lb_logits.astype(F32)
    hnorm = hgrn_norm[0].astype(F32).reshape(1, HG_DIM)
    w = {
        "bf": w_branch_fox[0].astype(BF16), "bh": w_branch_hgrn[0].astype(BF16), "out": w_out[0].astype(BF16),
        "up": w_up[0].astype(BF16), "down": w_down[0].astype(BF16),
        "npost": norm_mix_post[0].reshape(1, D_MODEL), "npre2": norm_ffn_pre[0].reshape(1, D_MODEL),
        "npost2": norm_ffn_post[0].reshape(1, D_MODEL),
        "conv_w": ffn_conv_w[0], "conv_b": ffn_conv_b[0].reshape(1, D_FF),
    }

    xp = x_prompt.reshape(seq, D_MODEL)
    (qt, kh, vt, stats, pk, pv, plf, hq, hk, lfh, hi, shg, sga, sgb) = _proj(xp, g_pre, w_all, fb3, lbl, fold=True)
    o_fox = _fox_prompt(qt, kh, vt, stats)
    s0 = jnp.zeros((1, HG_HEADS, HG_DIM, HG_DIM), F32)
    o_hg, p_state = _hgrn(hq, hk, lfh, hi, shg, s0, hnorm, nseq=1)
    hist0 = jnp.zeros((1, 2, D_FF), F32)
    yp, pconv = _mixffn(xp, o_fox, o_hg, sga, sgb, hist0, w, seg_len=seq)

    xs = x_sample.reshape(nb * t_new, D_MODEL)
    (qs, sk, sv, slf, hq, hk, lfh, hi, shg, sga, sgb) = _proj(xs, g_pre, w_all, fb3, lbl, fold=False)
    lf_all_t = jnp.concatenate([
        jnp.swapaxes(cache_fox_logf[0].astype(F32), 1, 2),
        jnp.swapaxes(slf.reshape(nb, t_new, FOX_HEADS), 1, 2),
        jnp.zeros((nb, FOX_HEADS, LANES - t_new), F32)], axis=2)
    o_fox_s = _fox_sample(qs, sk, sv, lf_all_t,
                          cache_fox_k[0].reshape(nb, past, FOX_WIDTH), cache_fox_v[0].reshape(nb, past, FOX_WIDTH),
                          t_new=t_new)
    o_hg_s, s_state = _hgrn(hq, hk, lfh, hi, shg, state_hgrn[0].astype(F32), hnorm, nseq=nb)
    ys, sconv = _mixffn(xs, o_fox_s, o_hg_s, sga, sgb, state_ffn_conv[0], w, seg_len=t_new)

    return (
        yp.reshape(bp, seq, D_MODEL),
        ys.reshape(nb, t_new, D_MODEL),
        pk.reshape(1, bp, seq, FOX_HEADS, FOX_HEAD_DIM),
        pv.reshape(1, bp, seq, FOX_HEADS, FOX_HEAD_DIM),
        plf.reshape(1, bp, seq, FOX_HEADS),
        p_state.reshape(1, bp, HG_HEADS, HG_DIM, HG_DIM),
        pconv.reshape(1, bp, 2, D_FF),
        sk.reshape(1, nb, t_new, FOX_HEADS, FOX_HEAD_DIM),
        sv.reshape(1, nb, t_new, FOX_HEADS, FOX_HEAD_DIM),
        slf.reshape(1, nb, t_new, FOX_HEADS),
        s_state.reshape(1, nb, HG_HEADS, HG_DIM, HG_DIM),
        sconv.reshape(1, nb, 2, D_FF),
    )
```

```python
import functools

import numpy as np
import jax
import jax.numpy as jnp
from jax import lax
from jax.experimental import pallas as pl
from jax.experimental.pallas import tpu as pltpu

F32 = jnp.float32
BF16 = jnp.bfloat16

D_MODEL = 1024
FOX_HEADS = 8
FOX_HEAD_DIM = 64
FOX_WIDTH = FOX_HEADS * FOX_HEAD_DIM
HG_HEADS = 4
HG_DIM = 128
HG_WIDTH = HG_HEADS * HG_DIM
D_FF = 2816
RMS_EPS = 1e-6
NEG_INF = -1e30
LOG2E = 1.4426950408889634
FOX_SKIP_NATS = 110.0
IN_SIZES = [FOX_WIDTH, FOX_WIDTH, FOX_WIDTH, FOX_HEADS, HG_WIDTH, HG_WIDTH, HG_WIDTH, HG_WIDTH, D_MODEL, D_MODEL]

LANES = 128
FOX_PAD = 2 * FOX_HEAD_DIM
HG_BLOCK = 64
HG_SPLIT_MAX = 60.0
VMEM_LIMIT = 56 * 1024 * 1024

PROJ_TM = 256
FOX_TQ = 512
FOX_TK = 512
HG_TC = 256
FFN_TM = 256
SAMPLE_TK = 1024

_C_Q, _C_K, _C_V, _C_F = 0, 512, 1024, 1536
_C_HQ, _C_HF, _C_HI, _C_HG = 1664, 2176, 2688, 3200
_C_GA, _C_GB, _C_END = 3712, 4736, 5760


def _split3(x):
    hi = x.astype(BF16)
    r = x - hi.astype(F32)
    mid = r.astype(BF16)
    lo = (r - mid.astype(F32)).astype(BF16)
    return hi, mid, lo


def _sum_by_01_matrix(mat01, x):
    cat = jnp.concatenate(_split3(x), axis=1)
    y = jnp.dot(mat01, cat, preferred_element_type=F32)
    return y[:, :LANES] + y[:, LANES:2 * LANES] + y[:, 2 * LANES:]


def _rms_scale(x):
    return x * lax.rsqrt(jnp.mean(x * x, axis=-1, keepdims=True) + RMS_EPS)


def _log_sigmoid(x):
    return jnp.minimum(x, 0.0) - jnp.log1p(jnp.exp(-jnp.abs(x)))


def _sigmoid(x):
    return 1.0 / (1.0 + jnp.exp(-x))


def _proj_kernel(*refs, fold):
    if fold:
        (x_ref, g_ref, w_ref, fb_ref, lbl_ref, tri_ref, pq_ref, pk_ref, cq_ref, ck_ref, cv_ref, seg_ref,
         qh_ref, kh_ref, vh_ref, stat_ref, kout_ref, vout_ref, lf_ref, hq_ref, hk_ref, lfh_ref, hi_ref, shg_ref,
         sga_ref, sgb_ref, carry_ref) = refs
    else:
        (x_ref, g_ref, w_ref, fb_ref, lbl_ref,
         qs_ref, kout_ref, vout_ref, lf_ref, hq_ref, hk_ref, lfh_ref, hi_ref, shg_ref,
         sga_ref, sgb_ref) = refs

    h = (_rms_scale(x_ref[...]) * g_ref[...]).astype(BF16)
    z = jnp.dot(h, w_ref[...], preferred_element_type=F32)

    zq = z[:, _C_Q:_C_K] * (FOX_HEAD_DIM ** -0.5)
    zk = z[:, _C_K:_C_V]
    zv = z[:, _C_V:_C_F]
    kout_ref[...] = zk
    vout_ref[...] = zv
    logf = _log_sigmoid(z[:, _C_F:_C_HQ] + fb_ref[...])
    lf_ref[...] = logf[:, :FOX_HEADS]

    l0 = lbl_ref[0:1, :]
    l1 = lbl_ref[1:2, :]
    lmax = jnp.maximum(l0, l1)
    e0 = jnp.exp(l0 - lmax)
    lb = e0 / (e0 + jnp.exp(l1 - lmax))
    f = lb + (1.0 - lb) * _sigmoid(z[:, _C_HF:_C_HI])
    hq_ref[...] = z[:, _C_HQ:_C_HF].astype(BF16)
    hk_ref[...] = (1.0 - f).astype(BF16)
    lfh_ref[...] = jnp.log(f)
    hi_ref[...] = z[:, _C_HI:_C_HG].astype(BF16)
    shg_ref[...] = _sigmoid(z[:, _C_HG:_C_GA]).astype(BF16)
    sga_ref[...] = _sigmoid(z[:, _C_GA:_C_GB]).astype(BF16)
    sgb_ref[...] = _sigmoid(z[:, _C_GB:_C_END]).astype(BF16)

    if not fold:
        qs_ref[...] = zq.astype(BF16)
        return

    @pl.when(pl.program_id(0) == 0)
    def _():
        carry_ref[...] = jnp.zeros_like(carry_ref)

    cum = carry_ref[...] + _sum_by_01_matrix(tri_ref[...], logf)
    carry_ref[...] = cum[-1:, :]

    seg = seg_ref[...]
    qn2 = jnp.dot((zq * zq).astype(BF16), seg, preferred_element_type=F32)
    kn2 = jnp.dot((zk * zk).astype(BF16), seg, preferred_element_type=F32)
    dg = jnp.dot((zq * zk).astype(BF16), seg, preferred_element_type=F32)
    stat_ref[0, 0:1, :] = jnp.max(qn2, axis=0, keepdims=True)
    stat_ref[0, 1:2, :] = jnp.max(kn2, axis=0, keepdims=True)
    stat_ref[0, 2:3, :] = jnp.min(dg, axis=0, keepdims=True)
    stat_ref[0, 3:4, :] = cum[0:1, :]
    stat_ref[0, 4:5, :] = cum[-1:, :]
    stat_ref[0, 5:8, :] = jnp.zeros((3, LANES), F32)

    zq = zq * LOG2E
    c_hi, c_mid, c_lo = _split3(cum * LOG2E)
    lane = lax.broadcasted_iota(jnp.int32, cum.shape, 1)
    pieces = jnp.where(lane < 8, c_hi, jnp.where(lane < 16, c_mid, c_lo))
    pieces = jnp.where(lane < 24, pieces, jnp.zeros_like(pieces))
    ex_q = jnp.dot(pieces, pq_ref[...], preferred_element_type=F32) + cq_ref[...]
    ex_k = jnp.dot(pieces, pk_ref[...], preferred_element_type=F32) + ck_ref[...]
    ex_v = cv_ref[...]

    low = lax.broadcasted_iota(jnp.int32, (zq.shape[0], LANES), 1) < FOX_HEAD_DIM
    for src, ex, dst, transposed in ((zq, ex_q, qh_ref, True), (zk, ex_k, kh_ref, False), (zv, ex_v, vh_ref, True)):
        for c in range(FOX_WIDTH // LANES):
            pair = src[:, c * LANES:(c + 1) * LANES]
            swapped = pltpu.roll(pair, FOX_HEAD_DIM, axis=1)
            for j, data in enumerate((pair, swapped)):
                hd = 2 * c + j
                blk = jnp.where(low, data, ex[:, hd * LANES:(hd + 1) * LANES])
                dst[hd] = (blk.T if transposed else blk).astype(BF16)


def _bias_fold_constants():
    pq = np.zeros((LANES, FOX_HEADS * LANES), np.float32)
    pk = np.zeros((LANES, FOX_HEADS * LANES), np.float32)
    cq = np.zeros((1, FOX_HEADS * LANES), np.float32)
    ck = np.zeros((1, FOX_HEADS * LANES), np.float32)
    cv = np.zeros((1, FOX_HEADS * LANES), np.float32)
    for h in range(FOX_HEADS):
        base = h * LANES + FOX_HEAD_DIM
        for p in range(3):
            pq[p * 8 + h, base + p] = 1.0
            ck[0, base + p] = 1.0
            pk[p * 8 + h, base + 3 + p] = -1.0
            cq[0, base + 3 + p] = 1.0
        cv[0, base] = 1.0
    return (jnp.asarray(pq, BF16), jnp.asarray(pk, BF16), jnp.asarray(cq), jnp.asarray(ck), jnp.asarray(cv))


def _const_spec(shape, single=True):
    nd = len(shape)
    if single:
        return pl.BlockSpec(shape, lambda *_: (0,) * nd, pipeline_mode=pl.Buffered(1))
    return pl.BlockSpec(shape, lambda *_: (0,) * nd)


def _proj(x, gain, w_all, fb3, lb_logits, *, fold):
    rows = x.shape[0]
    tm = min(PROJ_TM, rows)
    n = rows // tm
    row = lambda width: pl.BlockSpec((tm, width), lambda i: (i, 0))
    in_specs = [row(D_MODEL), _const_spec((1, D_MODEL)), _const_spec(w_all.shape), _const_spec((1, LANES)),
                _const_spec(lb_logits.shape)]
    args = [x, gain, w_all, fb3, lb_logits]
    common_out = [
        (jax.ShapeDtypeStruct((rows, FOX_WIDTH), F32), row(FOX_WIDTH)),
        (jax.ShapeDtypeStruct((rows, FOX_WIDTH), F32), row(FOX_WIDTH)),
        (jax.ShapeDtypeStruct((rows, FOX_HEADS), F32), row(FOX_HEADS)),
        (jax.ShapeDtypeStruct((rows, HG_WIDTH), BF16), row(HG_WIDTH)),
        (jax.ShapeDtypeStruct((rows, HG_WIDTH), BF16), row(HG_WIDTH)),
        (jax.ShapeDtypeStruct((rows, HG_WIDTH), F32), row(HG_WIDTH)),
        (jax.ShapeDtypeStruct((rows, HG_WIDTH), BF16), row(HG_WIDTH)),
        (jax.ShapeDtypeStruct((rows, HG_WIDTH), BF16), row(HG_WIDTH)),
        (jax.ShapeDtypeStruct((rows, D_MODEL), BF16), row(D_MODEL)),
        (jax.ShapeDtypeStruct((rows, D_MODEL), BF16), row(D_MODEL)),
    ]
    scratch = []
    if fold:
        tri = jnp.asarray(np.tril(np.ones((tm, tm), np.float32)), BF16)
        seg = np.zeros((FOX_WIDTH, LANES), np.float32)
        seg[np.arange(FOX_WIDTH), np.arange(FOX_WIDTH) // FOX_HEAD_DIM] = 1.0
        consts = _bias_fold_constants() + (jnp.asarray(seg, BF16),)
        in_specs += [_const_spec(tri.shape)] + [_const_spec(c.shape) for c in consts]
        args += [tri, *consts]
        head_major = (jax.ShapeDtypeStruct((FOX_HEADS, rows, FOX_PAD), BF16),
                      pl.BlockSpec((FOX_HEADS, tm, FOX_PAD), lambda i: (0, i, 0)))
        head_major_t = (jax.ShapeDtypeStruct((FOX_HEADS, FOX_PAD, rows), BF16),
                        pl.BlockSpec((FOX_HEADS, FOX_PAD, tm), lambda i: (0, 0, i)))
        stats = (jax.ShapeDtypeStruct((n, 8, LANES), F32), pl.BlockSpec((1, 8, LANES), lambda i: (i, 0, 0)))
        outs = [head_major_t, head_major, head_major_t, stats] + common_out
        scratch = [pltpu.VMEM((1, LANES), F32)]
    else:
        outs = [(jax.ShapeDtypeStruct((rows, FOX_WIDTH), BF16), row(FOX_WIDTH))] + common_out
    return pl.pallas_call(
        functools.partial(_proj_kernel, fold=fold),
        grid=(n,),
        in_specs=in_specs,
        out_specs=[o[1] for o in outs],
        out_shape=[o[0] for o in outs],
        scratch_shapes=scratch,
        compiler_params=pltpu.CompilerParams(dimension_semantics=("arbitrary",), vmem_limit_bytes=VMEM_LIMIT),
        name="proj_fold" if fold else "proj",
    )(*args)


def _fox_kernel(qi_ref, ki_ref, mode_ref, qt_ref, k_ref, vt_ref, o_ref, m_ref, acc_ref):
    mode = mode_ref[pl.program_id(0)]
    tq = qt_ref.shape[2]
    tk = k_ref.shape[1]

    @pl.when((mode & _MODE_FIRST) != 0)
    def _():
        m_ref[...] = jnp.full_like(m_ref, NEG_INF)
        acc_ref[...] = jnp.zeros_like(acc_ref)

    def head_step(h, masked):
        s = jnp.dot(k_ref[h], qt_ref[h], preferred_element_type=F32)
        if masked:
            key = lax.broadcasted_iota(jnp.int32, (tk, tq), 0)
            qry = lax.broadcasted_iota(jnp.int32, (tk, tq), 1)
            s = jnp.where(key <= qry, s, NEG_INF)
        m_prev = m_ref[h]
        m_new = jnp.maximum(m_prev, jnp.max(s, axis=0, keepdims=True))
        alpha = jnp.exp2(m_prev - m_new)
        p = jnp.exp2(s - m_new).astype(BF16)
        acc_ref[h] = alpha * acc_ref[h] + jnp.dot(vt_ref[h], p, preferred_element_type=F32)
        m_ref[h] = m_new

    @pl.when((mode & _MODE_PAIR) != 0)
    def _():
        for h in range(FOX_HEADS):
            head_step(h, False)

    @pl.when((mode & _MODE_DIAG) != 0)
    def _():
        for h in range(FOX_HEADS):
            head_step(h, True)
        for c in range(FOX_WIDTH // LANES):
            halves = []
            for hd in (2 * c, 2 * c + 1):
                acc = acc_ref[hd]
                halves.append(acc[:FOX_HEAD_DIM, :] / acc[FOX_HEAD_DIM:FOX_HEAD_DIM + 1, :])
            o_ref[:, c * LANES:(c + 1) * LANES] = jnp.concatenate(halves, axis=0).T.astype(o_ref.dtype)


_MODE_FIRST, _MODE_PAIR, _MODE_DIAG = 1, 2, 4


def _fox_schedule(stats, nq):
    st = stats[:, :5, :FOX_HEADS].reshape(nq, -1, 5, FOX_HEADS)
    qn = jnp.sqrt(jnp.max(st[:, :, 0], axis=1)) * 1.01
    kn = jnp.sqrt(jnp.max(st[:, :, 1], axis=1)) * 1.01
    dmin = jnp.min(st[:, :, 2], axis=1)
    c_first = st[:, 0, 3]
    c_last = st[:, -1, 4]
    bound = qn[:, None] * kn[None, :] + (c_first - dmin)[:, None] - c_last[None, :]
    blk = jnp.arange(nq, dtype=jnp.int32)
    drop = jnp.all(bound < -FOX_SKIP_NATS, axis=-1) & (blk[None, :] < blk[:, None])
    prefix = jnp.cumsum(jnp.logical_not(drop).astype(jnp.int32), axis=1) == 0
    kstart = jnp.sum(prefix.astype(jnp.int32), axis=1)
    count = blk - kstart + 1
    ends = jnp.cumsum(count)
    nsteps = nq * (nq + 1) // 2
    step = jnp.arange(nsteps, dtype=jnp.int32)
    valid = step < ends[-1]
    done = ends[None, :] <= step[:, None]
    q_of = jnp.minimum(jnp.sum(done.astype(jnp.int32), axis=1), nq - 1)
    begin = jnp.sum(jnp.where(done, count[None, :], 0), axis=1)
    k_first = jnp.sum(jnp.where(blk[None, :] == q_of[:, None], kstart[None, :], 0), axis=1)
    k_of = k_first + (step - begin)
    mode = jnp.where(k_of == q_of, _MODE_DIAG, _MODE_PAIR) + jnp.where(k_of == k_first, _MODE_FIRST, 0)
    mode = jnp.where(valid, mode, 0)
    q_of = jnp.where(valid, q_of, nq - 1)
    k_of = jnp.where(valid, k_of, nq - 1)
    return q_of.astype(jnp.int32), k_of.astype(jnp.int32), mode.astype(jnp.int32)


def _fox_prompt(qt, kh, vt, stats):
    seq = kh.shape[1]
    assert FOX_TQ == FOX_TK
    nq = seq // FOX_TQ
    qi, ki, mode = _fox_schedule(stats, nq)
    grid_spec = pltpu.PrefetchScalarGridSpec(
        num_scalar_prefetch=3,
        grid=(nq * (nq + 1) // 2,),
        in_specs=[
            pl.BlockSpec((FOX_HEADS, FOX_PAD, FOX_TQ), lambda s, qi, ki, mode: (0, 0, qi[s])),
            pl.BlockSpec((FOX_HEADS, FOX_TK, FOX_PAD), lambda s, qi, ki, mode: (0, ki[s], 0)),
            pl.BlockSpec((FOX_HEADS, FOX_PAD, FOX_TK), lambda s, qi, ki, mode: (0, 0, ki[s])),
        ],
        out_specs=pl.BlockSpec((FOX_TQ, FOX_WIDTH), lambda s, qi, ki, mode: (qi[s], 0)),
        scratch_shapes=[pltpu.VMEM((FOX_HEADS, 1, FOX_TQ), F32),
                        pltpu.VMEM((FOX_HEADS, FOX_PAD, FOX_TQ), F32)],
    )
    return pl.pallas_call(
        _fox_kernel,
        grid_spec=grid_spec,
        out_shape=jax.ShapeDtypeStruct((seq, FOX_WIDTH), BF16),
        compiler_params=pltpu.CompilerParams(dimension_semantics=("arbitrary",), vmem_limit_bytes=VMEM_LIMIT),
        name="fox_prompt",
    )(qi, ki, mode, qt, kh, vt)


def _lane_cumsum(x):
    n = x.shape[1]
    lane = lax.broadcasted_iota(jnp.int32, x.shape, 1)
    shift = 1
    while shift < n:
        x = x + jnp.where(lane >= shift, pltpu.roll(x, shift, axis=1), 0.0)
        shift *= 2
    return x


def _fox_sample_kernel(q_ref, kn_ref, vn_ref, lft_ref, ck_ref, cv_ref, o_ref,
                       qbd_ref, cq_ref, cum_ref, m_ref, l_ref, acc_ref, *, n_past_chunks, tk, t_new):
    c = pl.program_id(1)
    rows = FOX_HEADS * t_new
    lane_head = lax.broadcasted_iota(jnp.int32, (t_new, FOX_WIDTH), 1) // FOX_HEAD_DIM

    @pl.when(c == 0)
    def _():
        cum = _lane_cumsum(lft_ref[0])
        for j in range(n_past_chunks):
            cum_ref[j] = cum[:, j * tk:(j + 1) * tk]
        new_cum = cum[:, n_past_chunks * tk:n_past_chunks * tk + LANES]
        cum_ref[n_past_chunks, :, :LANES] = new_cum
        new_cum_t = jnp.concatenate([new_cum] * (LANES // FOX_HEADS), axis=0).T
        q = q_ref[...]
        for h in range(FOX_HEADS):
            qbd_ref[h * t_new:(h + 1) * t_new, :] = jnp.where(lane_head == h, q, jnp.zeros_like(q))
            cq_ref[h * t_new:(h + 1) * t_new, :] = jnp.broadcast_to(new_cum_t[:t_new, h:h + 1], (t_new, LANES))
        m_ref[...] = jnp.full_like(m_ref, NEG_INF)
        l_ref[...] = jnp.zeros_like(l_ref)
        acc_ref[...] = jnp.zeros_like(acc_ref)

    def update(k, v, ck_rows, mask):
        s = lax.dot_general(qbd_ref[...], k, (((1,), (1,)), ((), ())), preferred_element_type=F32)
        s = s + (cq_ref[:, :1] - ck_rows)
        if mask is not None:
            s = jnp.where(mask, s, NEG_INF)
        m_prev = m_ref[...]
        m_new = jnp.maximum(m_prev, jnp.max(s, axis=1, keepdims=True))
        alpha = jnp.exp(m_prev - m_new)
        p = jnp.exp(s - m_new[:, :1])
        l_ref[...] = alpha * l_ref[...] + jnp.sum(p, axis=1, keepdims=True)
        acc_ref[...] = alpha[:, :1] * acc_ref[...] + jnp.dot(p.astype(BF16), v, preferred_element_type=F32)
        m_ref[...] = m_new

    def expand_rows(x, width):
        return jnp.concatenate([jnp.broadcast_to(x[h:h + 1, :], (t_new, width)) for h in range(FOX_HEADS)], axis=0)

    @pl.when(c < n_past_chunks)
    def _():
        update(ck_ref[0].astype(BF16), cv_ref[0].astype(BF16), expand_rows(cum_ref[c], tk), None)

    @pl.when(c == n_past_chunks)
    def _():
        ck_rows = expand_rows(cum_ref[n_past_chunks, :, :LANES], LANES)[:, :t_new]
        rowt = lax.broadcasted_iota(jnp.int32, (rows, t_new), 0) % t_new
        coli = lax.broadcasted_iota(jnp.int32, (rows, t_new), 1)
        update(kn_ref[...].astype(BF16), vn_ref[...].astype(BF16), ck_rows, coli <= rowt)
        out = acc_ref[...] / l_ref[:, :1]
        o = jnp.zeros((t_new, FOX_WIDTH), F32)
        for h in range(FOX_HEADS):
            o = o + jnp.where(lane_head == h, out[h * t_new:(h + 1) * t_new, :], 0.0)
        o_ref[...] = o.astype(o_ref.dtype)


def _fox_sample(qs, k_new, v_new, lf_all_t, cache_k, cache_v, *, t_new):
    nb, past = cache_k.shape[0], cache_k.shape[1]
    tk = SAMPLE_TK
    npc = past // tk
    rows = FOX_HEADS * t_new
    last = npc - 1
    kern = functools.partial(_fox_sample_kernel, n_past_chunks=npc, tk=tk, t_new=t_new)
    return pl.pallas_call(
        kern,
        grid=(nb, npc + 1),
        in_specs=[
            pl.BlockSpec((t_new, FOX_WIDTH), lambda b, c: (b, 0)),
            pl.BlockSpec((t_new, FOX_WIDTH), lambda b, c: (b, 0)),
            pl.BlockSpec((t_new, FOX_WIDTH), lambda b, c: (b, 0)),
            pl.BlockSpec((1, FOX_HEADS, past + LANES), lambda b, c: (b, 0, 0)),
            pl.BlockSpec((1, tk, FOX_WIDTH), lambda b, c: (b, jnp.minimum(c, last), 0)),
            pl.BlockSpec((1, tk, FOX_WIDTH), lambda b, c: (b, jnp.minimum(c, last), 0)),
        ],
        out_specs=pl.BlockSpec((t_new, FOX_WIDTH), lambda b, c: (b, 0)),
        out_shape=jax.ShapeDtypeStruct((nb * t_new, FOX_WIDTH), BF16),
        scratch_shapes=[
            pltpu.VMEM((rows, FOX_WIDTH), BF16),
            pltpu.VMEM((rows, LANES), F32),
            pltpu.VMEM((npc + 1, FOX_HEADS, tk), F32),
            pltpu.VMEM((rows, LANES), F32),
            pltpu.VMEM((rows, LANES), F32),
            pltpu.VMEM((rows, FOX_WIDTH), F32),
        ],
        compiler_params=pltpu.CompilerParams(dimension_semantics=("arbitrary", "arbitrary"),
                                             vmem_limit_bytes=VMEM_LIMIT),
        name="fox_sample",
    )(qs, k_new, v_new, lf_all_t, cache_k, cache_v)


def _hgrn_kernel(hq_ref, hk_ref, lfh_ref, hi_ref, shg_ref, s0_ref, norm_ref, tri_ref,
                 o_ref, sout_ref, st_ref, qq_ref, kk_ref, eb_ref, b_ref, oin_ref, od_ref):
    t = pl.program_id(1)
    tc = hq_ref.shape[0]
    blk = min(HG_BLOCK, tc)
    nblk = tc // blk
    heads = [slice(h * HG_DIM, (h + 1) * HG_DIM) for h in range(HG_HEADS)]

    @pl.when(t == 0)
    def _():
        for h in range(HG_HEADS):
            st_ref[h] = s0_ref[0, h].T

    def at_block_row(x, r):
        x3 = x.reshape(nblk, blk, HG_DIM)
        return jnp.broadcast_to(x3[:, r:r + 1, :], x3.shape).reshape(tc, HG_DIM)

    def rel_to_middle(b):
        return b - at_block_row(b, blk // 2 - 1)

    worst = jnp.zeros((), F32)
    for sl in heads:
        b = _sum_by_01_matrix(tri_ref[...], lfh_ref[:, sl])
        eb = jnp.exp(b)
        qq_ref[:, sl] = (hq_ref[:, sl].astype(F32) * eb).astype(BF16)
        kk_ref[:, sl] = (hk_ref[:, sl].astype(F32) * jnp.exp(at_block_row(b, blk - 1) - b)).astype(BF16)
        eb_ref[:, sl] = eb
        b_ref[:, sl] = b
        worst = jnp.maximum(worst, jnp.max(jnp.abs(rel_to_middle(b))))

    def carried_state_and_output():
        for j in range(nblk):
            rows = slice(j * blk, (j + 1) * blk)
            for h, sl in enumerate(heads):
                st = st_ref[h]
                oin_ref[rows, sl] = lax.dot_general(qq_ref[rows, sl], st.astype(BF16), (((1,), (1,)), ((), ())),
                                                    preferred_element_type=F32)
                upd = lax.dot_general(hi_ref[rows, sl], kk_ref[rows, sl], (((0,), (0,)), ((), ())),
                                      preferred_element_type=F32)
                st_ref[h] = st * eb_ref[(j + 1) * blk - 1:(j + 1) * blk, sl] + upd
        for sl in heads:
            o = oin_ref[:, sl] + od_ref[:, sl]
            y = (_rms_scale(o) * norm_ref[...]) * shg_ref[:, sl].astype(F32)
            o_ref[:, sl] = y.astype(o_ref.dtype)

    splittable = worst <= HG_SPLIT_MAX

    @pl.when(splittable)
    def _():
        row = lax.broadcasted_iota(jnp.int32, (tc, tc), 0)
        col = lax.broadcasted_iota(jnp.int32, (tc, tc), 1)
        pair_in_block = (row // blk == col // blk) & (col <= row)
        for sl in heads:
            b_rel = rel_to_middle(b_ref[:, sl])
            qs = (hq_ref[:, sl].astype(F32) * jnp.exp(b_rel)).astype(BF16)
            ks = (hk_ref[:, sl].astype(F32) * jnp.exp(-b_rel)).astype(BF16)
            a = lax.dot_general(qs, ks, (((1,), (1,)), ((), ())), preferred_element_type=F32)
            a = jnp.where(pair_in_block, a, 0.0).astype(BF16)
            od_ref[:, sl] = jnp.dot(a, hi_ref[:, sl], preferred_element_type=F32)
        carried_state_and_output()

    @pl.when(jnp.logical_not(splittable))
    def _():
        row_in_blk = lax.broadcasted_iota(jnp.int32, (tc, HG_DIM), 0) % blk
        for sl in heads:
            q = hq_ref[:, sl].astype(F32)
            k = hk_ref[:, sl].astype(F32)
            v = hi_ref[:, sl].astype(F32)
            b = b_ref[:, sl]

            def lag_step(lag, od):
                k_l = pltpu.roll(k, lag, axis=0)
                b_l = pltpu.roll(b, lag, axis=0)
                v_l = pltpu.roll(v, lag, axis=0)
                w = q * k_l * jnp.exp(jnp.minimum(b - b_l, 0.0))
                w = jnp.where(row_in_blk >= lag, w, 0.0)
                return od + jnp.sum(w, axis=1, keepdims=True) * v_l

            od_ref[:, sl] = lax.fori_loop(1, blk, lag_step, jnp.sum(q * k, axis=1, keepdims=True) * v)
        carried_state_and_output()

    @pl.when(t == pl.num_programs(1) - 1)
    def _():
        for h in range(HG_HEADS):
            sout_ref[0, h] = st_ref[h].T


def _hgrn_cumsum_matrix(n, blk):
    t = np.arange(n)[:, None]
    s = np.arange(n)[None, :]
    return jnp.asarray((((t // blk) == (s // blk)) & (s <= t)).astype(np.float32), BF16)


def _hgrn(hq, hk, lfh, hi, shg, s0, norm, *, nseq):
    rows = hq.shape[0]
    t_len = rows // nseq
    tc = min(HG_TC, t_len)
    nt = t_len // tc
    blk = min(HG_BLOCK, tc)
    tri = _hgrn_cumsum_matrix(tc, blk)
    row = pl.BlockSpec((tc, HG_WIDTH), lambda b, t: (b * nt + t, 0))
    state = pl.BlockSpec((1, HG_HEADS, HG_DIM, HG_DIM), lambda b, t: (b, 0, 0, 0))
    return pl.pallas_call(
        _hgrn_kernel,
        grid=(nseq, nt),
        in_specs=[row, row, row, row, row, state,
                  pl.BlockSpec((1, HG_DIM), lambda b, t: (0, 0)),
                  pl.BlockSpec((tc, tc), lambda b, t: (0, 0))],
        out_specs=[row, state],
        out_shape=[jax.ShapeDtypeStruct((rows, HG_WIDTH), BF16),
                   jax.ShapeDtypeStruct((nseq, HG_HEADS, HG_DIM, HG_DIM), F32)],
        scratch_shapes=[
            pltpu.VMEM((HG_HEADS, HG_DIM, HG_DIM), F32),
            pltpu.VMEM((tc, HG_WIDTH), BF16),
            pltpu.VMEM((tc, HG_WIDTH), BF16),
            pltpu.VMEM((tc, HG_WIDTH), F32),
            pltpu.VMEM((tc, HG_WIDTH), F32),
            pltpu.VMEM((tc, HG_WIDTH), F32),
            pltpu.VMEM((tc, HG_WIDTH), F32),
        ],
        compiler_params=pltpu.CompilerParams(dimension_semantics=("arbitrary", "arbitrary"),
                                             vmem_limit_bytes=VMEM_LIMIT),
        name="hgrn",
    )(hq, hk, lfh, hi, shg, s0, norm, tri)


def _mixffn_kernel(x_ref, of_ref, oh_ref, sga_ref, sgb_ref, hist_ref,
                   wbf_ref, wbh_ref, wout_ref, wup_ref, wdn_ref,
                   npost_ref, npre2_ref, npost2_ref, cw_ref, cb_ref,
                   y_ref, conv_ref, tail_ref, *, seg_len):
    tm = x_ref.shape[0]
    br_f = jnp.dot(of_ref[...], wbf_ref[...], preferred_element_type=F32)
    br_h = jnp.dot(oh_ref[...], wbh_ref[...], preferred_element_type=F32)
    merged = sga_ref[...].astype(F32) * br_f + sgb_ref[...].astype(F32) * br_h
    mix = jnp.dot(merged.astype(BF16), wout_ref[...], preferred_element_type=F32)
    x1 = x_ref[...] + _rms_scale(mix) * npost_ref[...]

    h2 = (_rms_scale(x1) * npre2_ref[...]).astype(BF16)
    up = jnp.dot(h2, wup_ref[...], preferred_element_type=F32)
    a = up[:, :D_FF]
    g = up[:, D_FF:]

    prev1 = pltpu.roll(a, 1, axis=0)
    prev2 = pltpu.roll(a, 2, axis=0)
    rowi = lax.broadcasted_iota(jnp.int32, a.shape, 0)
    if seg_len >= tm:
        @pl.when(pl.program_id(0) == 0)
        def _():
            tail_ref[...] = hist_ref[0]
        t0 = tail_ref[0:1, :]
        t1 = tail_ref[1:2, :]
        prev1 = jnp.where(rowi == 0, t1, prev1)
        prev2 = jnp.where(rowi == 0, t0, jnp.where(rowi == 1, t1, prev2))
        tail_ref[...] = a[tm - 2:, :]
        conv_ref[0] = a[tm - 2:, :]
    else:
        for s in range(tm // seg_len):
            h0 = hist_ref[s, 0:1, :]
            h1 = hist_ref[s, 1:2, :]
            prev1 = jnp.where(rowi == s * seg_len, h1, prev1)
            prev2 = jnp.where(rowi == s * seg_len, h0, jnp.where(rowi == s * seg_len + 1, h1, prev2))
            conv_ref[s] = a[(s + 1) * seg_len - 2:(s + 1) * seg_len, :]
    c = cb_ref[...] + cw_ref[0:1, :] * prev2 + cw_ref[1:2, :] * prev1 + cw_ref[2:3, :] * a
    act = (jax.nn.gelu(c, approximate=True) * g).astype(BF16)
    ff = jnp.dot(act, wdn_ref[...], preferred_element_type=F32)
    y_ref[...] = x1 + _rms_scale(ff) * npost2_ref[...]


def _mixffn(x, o_fox, o_hg, sga, sgb, hist, w, *, seg_len):
    rows = x.shape[0]
    tm = min(FFN_TM, rows)
    n = rows // tm
    nseg = hist.shape[0]
    row = lambda width: pl.BlockSpec((tm, width), lambda i: (i, 0))
    weights = [w["bf"], w["bh"], w["out"], w["up"], w["down"]]
    smalls = [w["npost"], w["npre2"], w["npost2"], w["conv_w"], w["conv_b"]]
    hist_spec = pl.BlockSpec(hist.shape, lambda i: (0, 0, 0))
    scratch = [pltpu.VMEM((2, D_FF), F32)]
    return pl.pallas_call(
        functools.partial(_mixffn_kernel, seg_len=seg_len),
        grid=(n,),
        in_specs=[row(D_MODEL), row(FOX_WIDTH), row(HG_WIDTH), row(D_MODEL), row(D_MODEL), hist_spec]
                 + [_const_spec(a.shape) for a in weights] + [_const_spec(a.shape) for a in smalls],
        out_specs=[row(D_MODEL), pl.BlockSpec((nseg, 2, D_FF), lambda i: (0, 0, 0))],
        out_shape=[jax.ShapeDtypeStruct((rows, D_MODEL), F32), jax.ShapeDtypeStruct((nseg, 2, D_FF), F32)],
        scratch_shapes=scratch,
        compiler_params=pltpu.CompilerParams(dimension_semantics=("arbitrary",), vmem_limit_bytes=VMEM_LIMIT),
        name="mixffn",
    )(x, o_fox, o_hg, sga, sgb, hist, *weights, *smalls)


def _prep_w_in(w_in, fox_f_bias):
    offs = np.cumsum([0] + IN_SIZES)
    seg = [w_in[:, int(offs[i]):int(offs[i + 1])] for i in range(len(IN_SIZES))]
    pad = jnp.zeros((D_MODEL, LANES - 3 * FOX_HEADS), w_in.dtype)
    f3 = jnp.concatenate([seg[3], seg[3], seg[3], pad], axis=1)
    w_all = jnp.concatenate(seg[:3] + [f3] + seg[4:], axis=1).astype(BF16)
    fb = fox_f_bias.astype(F32)
    fb3 = jnp.concatenate([fb, fb, fb, jnp.zeros((LANES - 3 * FOX_HEADS,), F32)]).reshape(1, LANES)
    return w_all, fb3


def kernel(x_prompt, x_sample, cache_fox_k, cache_fox_v, cache_fox_logf, state_hgrn, state_ffn_conv, norm_mix_pre, norm_mix_post, w_in, fox_f_bias, hgrn_lb_logits, hgrn_norm, w_branch_fox, w_branch_hgrn, w_out, norm_ffn_pre, norm_ffn_post, w_up, ffn_conv_w, ffn_conv_b, w_down):
    depth = w_in.shape[0]
    assert depth == 1 and hgrn_lb_logits.shape[0] == 2
    bp, seq, _ = x_prompt.shape
    assert bp == 1
    nb, t_new, _ = x_sample.shape
    past = cache_fox_k.shape[2]

    w_all, fb3 = _prep_w_in(w_in[0], fox_f_bias[0])
    g_pre = norm_mix_pre[0].reshape(1, D_MODEL)
    lbl = hgrn_lb_logits.astype(F32)
    hnorm = hgrn_norm[0].astype(F32).reshape(1, HG_DIM)
    w = {
        "bf": w_branch_fox[0].astype(BF16), "bh": w_branch_hgrn[0].astype(BF16), "out": w_out[0].astype(BF16),
        "up": w_up[0].astype(BF16), "down": w_down[0].astype(BF16),
        "npost": norm_mix_post[0].reshape(1, D_MODEL), "npre2": norm_ffn_pre[0].reshape(1, D_MODEL),
        "npost2": norm_ffn_post[0].reshape(1, D_MODEL),
        "conv_w": ffn_conv_w[0], "conv_b": ffn_conv_b[0].reshape(1, D_FF),
    }

    xp = x_prompt.reshape(seq, D_MODEL)
    (qt, kh, vt, stats, pk, pv, plf, hq, hk, lfh, hi, shg, sga, sgb) = _proj(xp, g_pre, w_all, fb3, lbl, fold=True)
    o_fox = _fox_prompt(qt, kh, vt, stats)
    s0 = jnp.zeros((1, HG_HEADS, HG_DIM, HG_DIM), F32)
    o_hg, p_state = _hgrn(hq, hk, lfh, hi, shg, s0, hnorm, nseq=1)
    hist0 = jnp.zeros((1, 2, D_FF), F32)
    yp, pconv = _mixffn(xp, o_fox, o_hg, sga, sgb, hist0, w, seg_len=seq)

    xs = x_sample.reshape(nb * t_new, D_MODEL)
    (qs, sk, sv, slf, hq, hk, lfh, hi, shg, sga, sgb) = _proj(xs, g_pre, w_all, fb3, lbl, fold=False)
    lf_all_t = jnp.concatenate([
        jnp.swapaxes(cache_fox_logf[0].astype(F32), 1, 2),
        jnp.swapaxes(slf.reshape(nb, t_new, FOX_HEADS), 1, 2),
        jnp.zeros((nb, FOX_HEADS, LANES - t_new), F32)], axis=2)
    o_fox_s = _fox_sample(qs, sk, sv, lf_all_t,
                          cache_fox_k[0].reshape(nb, past, FOX_WIDTH), cache_fox_v[0].reshape(nb, past, FOX_WIDTH),
                          t_new=t_new)
    o_hg_s, s_state = _hgrn(hq, hk, lfh, hi, shg, state_hgrn[0].astype(F32), hnorm, nseq=nb)
    ys, sconv = _mixffn(xs, o_fox_s, o_hg_s, sga, sgb, state_ffn_conv[0], w, seg_len=t_new)

    return (
        yp.reshape(bp, seq, D_MODEL),
        ys.reshape(nb, t_new, D_MODEL),
        pk.reshape(1, bp, seq, FOX_HEADS, FOX_HEAD_DIM),
        pv.reshape(1, bp, seq, FOX_HEADS, FOX_HEAD_DIM),
        plf.reshape(1, bp, seq, FOX_HEADS),
        p_state.reshape(1, bp, HG_HEADS, HG_DIM, HG_DIM),
        pconv.reshape(1, bp, 2, D_FF),
        sk.reshape(1, nb, t_new, FOX_HEADS, FOX_HEAD_DIM),
        sv.reshape(1, nb, t_new, FOX_HEADS, FOX_HEAD_DIM),
        slf.reshape(1, nb, t_new, FOX_HEADS),
        s_state.reshape(1, nb, HG_HEADS, HG_DIM, HG_DIM),
        sconv.reshape(1, nb, 2, D_FF),
    )
```

```python
import functools

import numpy as np
import jax
import jax.numpy as jnp
from jax import lax
from jax.experimental import pallas as pl
from jax.experimental.pallas import tpu as pltpu

F32 = jnp.float32
BF16 = jnp.bfloat16

D_MODEL = 1024
FOX_HEADS = 8
FOX_HEAD_DIM = 64
FOX_WIDTH = FOX_HEADS * FOX_HEAD_DIM
HG_HEADS = 4
HG_DIM = 128
HG_WIDTH = HG_HEADS * HG_DIM
D_FF = 2816
RMS_EPS = 1e-6
NEG_INF = -1e30
LOG2E = 1.4426950408889634
FOX_SKIP_NATS = 110.0
IN_SIZES = [FOX_WIDTH, FOX_WIDTH, FOX_WIDTH, FOX_HEADS, HG_WIDTH, HG_WIDTH, HG_WIDTH, HG_WIDTH, D_MODEL, D_MODEL]

LANES = 128
FOX_PAD = 2 * FOX_HEAD_DIM
FOX_V_ROWS = FOX_HEAD_DIM + 16
HG_BLOCK = 64
HG_SPLIT_MAX = 60.0
VMEM_LIMIT = 56 * 1024 * 1024

PROJ_TM = 256
FOX_TQ = 512
FOX_TK = 512
HG_TC = 256
FFN_TM = 256
SAMPLE_TK = 1024

_C_Q, _C_K, _C_V, _C_F = 0, 512, 1024, 1536
_C_HQ, _C_HF, _C_HI, _C_HG = 1664, 2176, 2688, 3200
_C_GA, _C_GB, _C_END = 3712, 4736, 5760


def _split3(x):
    hi = x.astype(BF16)
    r = x - hi.astype(F32)
    mid = r.astype(BF16)
    lo = (r - mid.astype(F32)).astype(BF16)
    return hi, mid, lo


def _sum_by_01_matrix(mat01, x):
    cat = jnp.concatenate(_split3(x), axis=1)
    y = jnp.dot(mat01, cat, preferred_element_type=F32)
    return y[:, :LANES] + y[:, LANES:2 * LANES] + y[:, 2 * LANES:]


def _rms_scale(x):
    return x * lax.rsqrt(jnp.mean(x * x, axis=-1, keepdims=True) + RMS_EPS)


def _log_sigmoid(x):
    return jnp.minimum(x, 0.0) - jnp.log1p(jnp.exp(-jnp.abs(x)))


def _sigmoid(x):
    return 1.0 / (1.0 + jnp.exp(-x))


def _proj_kernel(*refs, fold):
    if fold:
        (x_ref, g_ref, w_ref, fb_ref, lbl_ref, tri_ref, pq_ref, pk_ref, cq_ref, ck_ref, cv_ref, seg_ref,
         qh_ref, kh_ref, vh_ref, stat_ref, kout_ref, vout_ref, lf_ref, hq_ref, hk_ref, lfh_ref, hi_ref, shg_ref,
         sga_ref, sgb_ref, carry_ref) = refs
    else:
        (x_ref, g_ref, w_ref, fb_ref, lbl_ref,
         qs_ref, kout_ref, vout_ref, lf_ref, hq_ref, hk_ref, lfh_ref, hi_ref, shg_ref,
         sga_ref, sgb_ref) = refs

    h = (_rms_scale(x_ref[...]) * g_ref[...]).astype(BF16)
    z = jnp.dot(h, w_ref[...], preferred_element_type=F32)

    zq = z[:, _C_Q:_C_K] * (FOX_HEAD_DIM ** -0.5)
    zk = z[:, _C_K:_C_V]
    zv = z[:, _C_V:_C_F]
    kout_ref[...] = zk
    vout_ref[...] = zv
    logf = _log_sigmoid(z[:, _C_F:_C_HQ] + fb_ref[...])
    lf_ref[...] = logf[:, :FOX_HEADS]

    l0 = lbl_ref[0:1, :]
    l1 = lbl_ref[1:2, :]
    lmax = jnp.maximum(l0, l1)
    e0 = jnp.exp(l0 - lmax)
    lb = e0 / (e0 + jnp.exp(l1 - lmax))
    f = lb + (1.0 - lb) * _sigmoid(z[:, _C_HF:_C_HI])
    hq_ref[...] = z[:, _C_HQ:_C_HF].astype(BF16)
    hk_ref[...] = (1.0 - f).astype(BF16)
    lfh_ref[...] = jnp.log(f)
    hi_ref[...] = z[:, _C_HI:_C_HG].astype(BF16)
    shg_ref[...] = _sigmoid(z[:, _C_HG:_C_GA]).astype(BF16)
    sga_ref[...] = _sigmoid(z[:, _C_GA:_C_GB]).astype(BF16)
    sgb_ref[...] = _sigmoid(z[:, _C_GB:_C_END]).astype(BF16)

    if not fold:
        qs_ref[...] = zq.astype(BF16)
        return

    @pl.when(pl.program_id(0) == 0)
    def _():
        carry_ref[...] = jnp.zeros_like(carry_ref)

    cum = carry_ref[...] + _sum_by_01_matrix(tri_ref[...], logf)
    carry_ref[...] = cum[-1:, :]

    seg = seg_ref[...]
    qn2 = jnp.dot((zq * zq).astype(BF16), seg, preferred_element_type=F32)
    kn2 = jnp.dot((zk * zk).astype(BF16), seg, preferred_element_type=F32)
    dg = jnp.dot((zq * zk).astype(BF16), seg, preferred_element_type=F32)
    stat_ref[0, 0:1, :] = jnp.max(qn2, axis=0, keepdims=True)
    stat_ref[0, 1:2, :] = jnp.max(kn2, axis=0, keepdims=True)
    stat_ref[0, 2:3, :] = jnp.min(dg, axis=0, keepdims=True)
    stat_ref[0, 3:4, :] = cum[0:1, :]
    stat_ref[0, 4:5, :] = cum[-1:, :]
    stat_ref[0, 5:8, :] = jnp.zeros((3, LANES), F32)

    zq = zq * LOG2E
    c_hi, c_mid, c_lo = _split3(cum * LOG2E)
    lane = lax.broadcasted_iota(jnp.int32, cum.shape, 1)
    pieces = jnp.where(lane < 8, c_hi, jnp.where(lane < 16, c_mid, c_lo))
    pieces = jnp.where(lane < 24, pieces, jnp.zeros_like(pieces))
    ex_q = jnp.dot(pieces, pq_ref[...], preferred_element_type=F32) + cq_ref[...]
    ex_k = jnp.dot(pieces, pk_ref[...], preferred_element_type=F32) + ck_ref[...]
    ex_v = cv_ref[...]

    low = lax.broadcasted_iota(jnp.int32, (zq.shape[0], LANES), 1) < FOX_HEAD_DIM
    for src, ex, dst, transposed in ((zq, ex_q, qh_ref, True), (zk, ex_k, kh_ref, False), (zv, ex_v, vh_ref, True)):
        for c in range(FOX_WIDTH // LANES):
            pair = src[:, c * LANES:(c + 1) * LANES]
            swapped = pltpu.roll(pair, FOX_HEAD_DIM, axis=1)
            for j, data in enumerate((pair, swapped)):
                hd = 2 * c + j
                blk = jnp.where(low, data, ex[:, hd * LANES:(hd + 1) * LANES])
                dst[hd] = (blk.T[:dst.shape[1]] if transposed else blk).astype(BF16)


def _bias_fold_constants():
    pq = np.zeros((LANES, FOX_HEADS * LANES), np.float32)
    pk = np.zeros((LANES, FOX_HEADS * LANES), np.float32)
    cq = np.zeros((1, FOX_HEADS * LANES), np.float32)
    ck = np.zeros((1, FOX_HEADS * LANES), np.float32)
    cv = np.zeros((1, FOX_HEADS * LANES), np.float32)
    for h in range(FOX_HEADS):
        base = h * LANES + FOX_HEAD_DIM
        for p in range(3):
            pq[p * 8 + h, base + p] = 1.0
            ck[0, base + p] = 1.0
            pk[p * 8 + h, base + 3 + p] = -1.0
            cq[0, base + 3 + p] = 1.0
        cv[0, base] = 1.0
    return (jnp.asarray(pq, BF16), jnp.asarray(pk, BF16), jnp.asarray(cq), jnp.asarray(ck), jnp.asarray(cv))


def _const_spec(shape, single=True):
    nd = len(shape)
    if single:
        return pl.BlockSpec(shape, lambda *_: (0,) * nd, pipeline_mode=pl.Buffered(1))
    return pl.BlockSpec(shape, lambda *_: (0,) * nd)


def _proj(x, gain, w_all, fb3, lb_logits, *, fold):
    rows = x.shape[0]
    tm = min(PROJ_TM, rows)
    n = rows // tm
    row = lambda width: pl.BlockSpec((tm, width), lambda i: (i, 0))
    in_specs = [row(D_MODEL), _const_spec((1, D_MODEL)), _const_spec(w_all.shape), _const_spec((1, LANES)),
                _const_spec(lb_logits.shape)]
    args = [x, gain, w_all, fb3, lb_logits]
    common_out = [
        (jax.ShapeDtypeStruct((rows, FOX_WIDTH), F32), row(FOX_WIDTH)),
        (jax.ShapeDtypeStruct((rows, FOX_WIDTH), F32), row(FOX_WIDTH)),
        (jax.ShapeDtypeStruct((rows, FOX_HEADS), F32), row(FOX_HEADS)),
        (jax.ShapeDtypeStruct((rows, HG_WIDTH), BF16), row(HG_WIDTH)),
        (jax.ShapeDtypeStruct((rows, HG_WIDTH), BF16), row(HG_WIDTH)),
        (jax.ShapeDtypeStruct((rows, HG_WIDTH), F32), row(HG_WIDTH)),
        (jax.ShapeDtypeStruct((rows, HG_WIDTH), BF16), row(HG_WIDTH)),
        (jax.ShapeDtypeStruct((rows, HG_WIDTH), BF16), row(HG_WIDTH)),
        (jax.ShapeDtypeStruct((rows, D_MODEL), BF16), row(D_MODEL)),
        (jax.ShapeDtypeStruct((rows, D_MODEL), BF16), row(D_MODEL)),
    ]
    scratch = []
    if fold:
        tri = jnp.asarray(np.tril(np.ones((tm, tm), np.float32)), BF16)
        seg = np.zeros((FOX_WIDTH, LANES), np.float32)
        seg[np.arange(FOX_WIDTH), np.arange(FOX_WIDTH) // FOX_HEAD_DIM] = 1.0
        consts = _bias_fold_constants() + (jnp.asarray(seg, BF16),)
        in_specs += [_const_spec(tri.shape)] + [_const_spec(c.shape) for c in consts]
        args += [tri, *consts]
        head_major = (jax.ShapeDtypeStruct((FOX_HEADS, rows, FOX_PAD), BF16),
                      pl.BlockSpec((FOX_HEADS, tm, FOX_PAD), lambda i: (0, i, 0)))
        head_major_t = lambda depth: (jax.ShapeDtypeStruct((FOX_HEADS, depth, rows), BF16),
                                      pl.BlockSpec((FOX_HEADS, depth, tm), lambda i: (0, 0, i)))
        stats = (jax.ShapeDtypeStruct((n, 8, LANES), F32), pl.BlockSpec((1, 8, LANES), lambda i: (i, 0, 0)))
        outs = [head_major_t(FOX_PAD), head_major, head_major_t(FOX_V_ROWS), stats] + common_out
        scratch = [pltpu.VMEM((1, LANES), F32)]
    else:
        outs = [(jax.ShapeDtypeStruct((rows, FOX_WIDTH), BF16), row(FOX_WIDTH))] + common_out
    return pl.pallas_call(
        functools.partial(_proj_kernel, fold=fold),
        grid=(n,),
        in_specs=in_specs,
        out_specs=[o[1] for o in outs],
        out_shape=[o[0] for o in outs],
        scratch_shapes=scratch,
        compiler_params=pltpu.CompilerParams(dimension_semantics=("arbitrary",), vmem_limit_bytes=VMEM_LIMIT),
        name="proj_fold" if fold else "proj",
    )(*args)


def _fox_kernel(qi_ref, ki_ref, mode_ref, qt_ref, k_ref, vt_ref, o_ref, m_ref, acc_ref):
    mode = mode_ref[pl.program_id(0)]
    tq = qt_ref.shape[2]
    tk = k_ref.shape[1]

    @pl.when((mode & _MODE_FIRST) != 0)
    def _():
        m_ref[...] = jnp.full_like(m_ref, NEG_INF)
        acc_ref[...] = jnp.zeros_like(acc_ref)

    def logits(h):
        return jnp.dot(k_ref[h], qt_ref[h], preferred_element_type=F32)

    def softmax_update(h, s, masked):
        if masked:
            key = lax.broadcasted_iota(jnp.int32, (tk, tq), 0)
            qry = lax.broadcasted_iota(jnp.int32, (tk, tq), 1)
            s = jnp.where(key <= qry, s, NEG_INF)
        m_prev = m_ref[h]
        m_new = jnp.maximum(m_prev, jnp.max(s, axis=0, keepdims=True))
        alpha = jnp.exp2(m_prev - m_new)
        p = jnp.exp2(s - m_new).astype(BF16)
        acc_ref[h] = alpha * acc_ref[h] + jnp.dot(vt_ref[h], p, preferred_element_type=F32)
        m_ref[h] = m_new

    def all_heads(masked):
        s_next = logits(0)
        for h in range(FOX_HEADS):
            s = s_next
            if h + 1 < FOX_HEADS:
                s_next = logits(h + 1)
            softmax_update(h, s, masked)

    @pl.when((mode & _MODE_PAIR) != 0)
    def _():
        all_heads(False)

    @pl.when((mode & _MODE_DIAG) != 0)
    def _():
        all_heads(True)
        for c in range(FOX_WIDTH // LANES):
            halves = []
            for hd in (2 * c, 2 * c + 1):
                acc = acc_ref[hd]
                halves.append(acc[:FOX_HEAD_DIM, :] / acc[FOX_HEAD_DIM:FOX_HEAD_DIM + 1, :])
            o_ref[:, c * LANES:(c + 1) * LANES] = jnp.concatenate(halves, axis=0).T.astype(o_ref.dtype)


_MODE_FIRST, _MODE_PAIR, _MODE_DIAG = 1, 2, 4


def _fox_schedule(stats, nq):
    st = stats[:, :5, :FOX_HEADS].reshape(nq, -1, 5, FOX_HEADS)
    qn = jnp.sqrt(jnp.max(st[:, :, 0], axis=1)) * 1.01
    kn = jnp.sqrt(jnp.max(st[:, :, 1], axis=1)) * 1.01
    dmin = jnp.min(st[:, :, 2], axis=1)
    c_first = st[:, 0, 3]
    c_last = st[:, -1, 4]
    bound = qn[:, None] * kn[None, :] + (c_first - dmin)[:, None] - c_last[None, :]
    blk = jnp.arange(nq, dtype=jnp.int32)
    drop = jnp.all(bound < -FOX_SKIP_NATS, axis=-1) & (blk[None, :] < blk[:, None])
    prefix = jnp.cumsum(jnp.logical_not(drop).astype(jnp.int32), axis=1) == 0
    kstart = jnp.sum(prefix.astype(jnp.int32), axis=1)
    count = blk - kstart + 1
    ends = jnp.cumsum(count)
    nsteps = nq * (nq + 1) // 2
    step = jnp.arange(nsteps, dtype=jnp.int32)
    valid = step < ends[-1]
    done = ends[None, :] <= step[:, None]
    q_of = jnp.minimum(jnp.sum(done.astype(jnp.int32), axis=1), nq - 1)
    begin = jnp.sum(jnp.where(done, count[None, :], 0), axis=1)
    k_first = jnp.sum(jnp.where(blk[None, :] == q_of[:, None], kstart[None, :], 0), axis=1)
    k_of = k_first + (step - begin)
    mode = jnp.where(k_of == q_of, _MODE_DIAG, _MODE_PAIR) + jnp.where(k_of == k_first, _MODE_FIRST, 0)
    mode = jnp.where(valid, mode, 0)
    q_of = jnp.where(valid, q_of, nq - 1)
    k_of = jnp.where(valid, k_of, nq - 1)
    return q_of.astype(jnp.int32), k_of.astype(jnp.int32), mode.astype(jnp.int32)


def _fox_prompt(qt, kh, vt, stats):
    seq = kh.shape[1]
    assert FOX_TQ == FOX_TK
    nq = seq // FOX_TQ
    qi, ki, mode = _fox_schedule(stats, nq)
    grid_spec = pltpu.PrefetchScalarGridSpec(
        num_scalar_prefetch=3,
        grid=(nq * (nq + 1) // 2,),
        in_specs=[
            pl.BlockSpec((FOX_HEADS, FOX_PAD, FOX_TQ), lambda s, qi, ki, mode: (0, 0, qi[s])),
            pl.BlockSpec((FOX_HEADS, FOX_TK, FOX_PAD), lambda s, qi, ki, mode: (0, ki[s], 0)),
            pl.BlockSpec((FOX_HEADS, FOX_V_ROWS, FOX_TK), lambda s, qi, ki, mode: (0, 0, ki[s])),
        ],
        out_specs=pl.BlockSpec((FOX_TQ, FOX_WIDTH), lambda s, qi, ki, mode: (qi[s], 0)),
        scratch_shapes=[pltpu.VMEM((FOX_HEADS, 1, FOX_TQ), F32),
                        pltpu.VMEM((FOX_HEADS, FOX_V_ROWS, FOX_TQ), F32)],
    )
    return pl.pallas_call(
        _fox_kernel,
        grid_spec=grid_spec,
        out_shape=jax.ShapeDtypeStruct((seq, FOX_WIDTH), BF16),
        compiler_params=pltpu.CompilerParams(dimension_semantics=("arbitrary",), vmem_limit_bytes=VMEM_LIMIT),
        name="fox_prompt",
    )(qi, ki, mode, qt, kh, vt)


def _lane_cumsum(x):
    n = x.shape[1]
    lane = lax.broadcasted_iota(jnp.int32, x.shape, 1)
    shift = 1
    while shift < n:
        x = x + jnp.where(lane >= shift, pltpu.roll(x, shift, axis=1), 0.0)
        shift *= 2
    return x


def _fox_sample_kernel(q_ref, kn_ref, vn_ref, lft_ref, ck_ref, cv_ref, o_ref,
                       qbd_ref, cq_ref, cum_ref, m_ref, l_ref, acc_ref, *, n_past_chunks, tk, t_new):
    c = pl.program_id(1)
    rows = FOX_HEADS * t_new
    lane_head = lax.broadcasted_iota(jnp.int32, (t_new, FOX_WIDTH), 1) // FOX_HEAD_DIM

    @pl.when(c == 0)
    def _():
        cum = _lane_cumsum(lft_ref[0])
        for j in range(n_past_chunks):
            cum_ref[j] = cum[:, j * tk:(j + 1) * tk]
        new_cum = cum[:, n_past_chunks * tk:n_past_chunks * tk + LANES]
        cum_ref[n_past_chunks, :, :LANES] = new_cum
        new_cum_t = jnp.concatenate([new_cum] * (LANES // FOX_HEADS), axis=0).T
        q = q_ref[...]
        for h in range(FOX_HEADS):
            qbd_ref[h * t_new:(h + 1) * t_new, :] = jnp.where(lane_head == h, q, jnp.zeros_like(q))
            cq_ref[h * t_new:(h + 1) * t_new, :] = jnp.broadcast_to(new_cum_t[:t_new, h:h + 1], (t_new, LANES))
        m_ref[...] = jnp.full_like(m_ref, NEG_INF)
        l_ref[...] = jnp.zeros_like(l_ref)
        acc_ref[...] = jnp.zeros_like(acc_ref)

    def update(k, v, ck_rows, mask):
        s = lax.dot_general(qbd_ref[...], k, (((1,), (1,)), ((), ())), preferred_element_type=F32)
        s = s + (cq_ref[:, :1] - ck_rows)
        if mask is not None:
            s = jnp.where(mask, s, NEG_INF)
        m_prev = m_ref[...]
        m_new = jnp.maximum(m_prev, jnp.max(s, axis=1, keepdims=True))
        alpha = jnp.exp(m_prev - m_new)
        p = jnp.exp(s - m_new[:, :1])
        l_ref[...] = alpha * l_ref[...] + jnp.sum(p, axis=1, keepdims=True)
        acc_ref[...] = alpha[:, :1] * acc_ref[...] + jnp.dot(p.astype(BF16), v, preferred_element_type=F32)
        m_ref[...] = m_new

    def expand_rows(x, width):
        return jnp.concatenate([jnp.broadcast_to(x[h:h + 1, :], (t_new, width)) for h in range(FOX_HEADS)], axis=0)

    @pl.when(c < n_past_chunks)
    def _():
        update(ck_ref[0].astype(BF16), cv_ref[0].astype(BF16), expand_rows(cum_ref[c], tk), None)

    @pl.when(c == n_past_chunks)
    def _():
        ck_rows = expand_rows(cum_ref[n_past_chunks, :, :LANES], LANES)[:, :t_new]
        rowt = lax.broadcasted_iota(jnp.int32, (rows, t_new), 0) % t_new
        coli = lax.broadcasted_iota(jnp.int32, (rows, t_new), 1)
        update(kn_ref[...].astype(BF16), vn_ref[...].astype(BF16), ck_rows, coli <= rowt)
        out = acc_ref[...] / l_ref[:, :1]
        o = jnp.zeros((t_new, FOX_WIDTH), F32)
        for h in range(FOX_HEADS):
            o = o + jnp.where(lane_head == h, out[h * t_new:(h + 1) * t_new, :], 0.0)
        o_ref[...] = o.astype(o_ref.dtype)


def _fox_sample(qs, k_new, v_new, lf_all_t, cache_k, cache_v, *, t_new):
    nb, past = cache_k.shape[0], cache_k.shape[1]
    tk = SAMPLE_TK
    npc = past // tk
    rows = FOX_HEADS * t_new
    last = npc - 1
    kern = functools.partial(_fox_sample_kernel, n_past_chunks=npc, tk=tk, t_new=t_new)
    return pl.pallas_call(
        kern,
        grid=(nb, npc + 1),
        in_specs=[
            pl.BlockSpec((t_new, FOX_WIDTH), lambda b, c: (b, 0)),
            pl.BlockSpec((t_new, FOX_WIDTH), lambda b, c: (b, 0)),
            pl.BlockSpec((t_new, FOX_WIDTH), lambda b, c: (b, 0)),
            pl.BlockSpec((1, FOX_HEADS, past + LANES), lambda b, c: (b, 0, 0)),
            pl.BlockSpec((1, tk, FOX_WIDTH), lambda b, c: (b, jnp.minimum(c, last), 0)),
            pl.BlockSpec((1, tk, FOX_WIDTH), lambda b, c: (b, jnp.minimum(c, last), 0)),
        ],
        out_specs=pl.BlockSpec((t_new, FOX_WIDTH), lambda b, c: (b, 0)),
        out_shape=jax.ShapeDtypeStruct((nb * t_new, FOX_WIDTH), BF16),
        scratch_shapes=[
            pltpu.VMEM((rows, FOX_WIDTH), BF16),
            pltpu.VMEM((rows, LANES), F32),
            pltpu.VMEM((npc + 1, FOX_HEADS, tk), F32),
            pltpu.VMEM((rows, LANES), F32),
            pltpu.VMEM((rows, LANES), F32),
            pltpu.VMEM((rows, FOX_WIDTH), F32),
        ],
        compiler_params=pltpu.CompilerParams(dimension_semantics=("arbitrary", "arbitrary"),
                                             vmem_limit_bytes=VMEM_LIMIT),
        name="fox_sample",
    )(qs, k_new, v_new, lf_all_t, cache_k, cache_v)


def _hgrn_kernel(hq_ref, hk_ref, lfh_ref, hi_ref, shg_ref, s0_ref, norm_ref, tri_ref,
                 o_ref, sout_ref, st_ref, qq_ref, kk_ref, eb_ref, b_ref, oin_ref, od_ref):
    t = pl.program_id(1)
    tc = hq_ref.shape[0]
    blk = min(HG_BLOCK, tc)
    nblk = tc // blk
    heads = [slice(h * HG_DIM, (h + 1) * HG_DIM) for h in range(HG_HEADS)]

    @pl.when(t == 0)
    def _():
        for h in range(HG_HEADS):
            st_ref[h] = s0_ref[0, h].T

    def at_block_row(x, r):
        x3 = x.reshape(nblk, blk, HG_DIM)
        return jnp.broadcast_to(x3[:, r:r + 1, :], x3.shape).reshape(tc, HG_DIM)

    def rel_to_middle(b):
        return b - at_block_row(b, blk // 2 - 1)

    worst = jnp.zeros((), F32)
    for sl in heads:
        b = _sum_by_01_matrix(tri_ref[...], lfh_ref[:, sl])
        eb = jnp.exp(b)
        qq_ref[:, sl] = (hq_ref[:, sl].astype(F32) * eb).astype(BF16)
        kk_ref[:, sl] = (hk_ref[:, sl].astype(F32) * jnp.exp(at_block_row(b, blk - 1) - b)).astype(BF16)
        eb_ref[:, sl] = eb
        b_ref[:, sl] = b
        worst = jnp.maximum(worst, jnp.max(jnp.abs(rel_to_middle(b))))

    def carried_state_and_output():
        for j in range(nblk):
            rows = slice(j * blk, (j + 1) * blk)
            for h, sl in enumerate(heads):
                st = st_ref[h]
                oin_ref[rows, sl] = lax.dot_general(qq_ref[rows, sl], st.astype(BF16), (((1,), (1,)), ((), ())),
                                                    preferred_element_type=F32)
                upd = lax.dot_general(hi_ref[rows, sl], kk_ref[rows, sl], (((0,), (0,)), ((), ())),
                                      preferred_element_type=F32)
                st_ref[h] = st * eb_ref[(j + 1) * blk - 1:(j + 1) * blk, sl] + upd
        for sl in heads:
            o = oin_ref[:, sl] + od_ref[:, sl]
            y = (_rms_scale(o) * norm_ref[...]) * shg_ref[:, sl].astype(F32)
            o_ref[:, sl] = y.astype(o_ref.dtype)

    splittable = worst <= HG_SPLIT_MAX

    @pl.when(splittable)
    def _():
        row = lax.broadcasted_iota(jnp.int32, (tc, tc), 0)
        col = lax.broadcasted_iota(jnp.int32, (tc, tc), 1)
        pair_in_block = (row // blk == col // blk) & (col <= row)
        for sl in heads:
            b_rel = rel_to_middle(b_ref[:, sl])
            qs = (hq_ref[:, sl].astype(F32) * jnp.exp(b_rel)).astype(BF16)
            ks = (hk_ref[:, sl].astype(F32) * jnp.exp(-b_rel)).astype(BF16)
            a = lax.dot_general(qs, ks, (((1,), (1,)), ((), ())), preferred_element_type=F32)
            a = jnp.where(pair_in_block, a, 0.0).astype(BF16)
            od_ref[:, sl] = jnp.dot(a, hi_ref[:, sl], preferred_element_type=F32)
        carried_state_and_output()

    @pl.when(jnp.logical_not(splittable))
    def _():
        row_in_blk = lax.broadcasted_iota(jnp.int32, (tc, HG_DIM), 0) % blk
        for sl in heads:
            q = hq_ref[:, sl].astype(F32)
            k = hk_ref[:, sl].astype(F32)
            v = hi_ref[:, sl].astype(F32)
            b = b_ref[:, sl]

            def lag_step(lag, od):
                k_l = pltpu.roll(k, lag, axis=0)
                b_l = pltpu.roll(b, lag, axis=0)
                v_l = pltpu.roll(v, lag, axis=0)
                w = q * k_l * jnp.exp(jnp.minimum(b - b_l, 0.0))
                w = jnp.where(row_in_blk >= lag, w, 0.0)
                return od + jnp.sum(w, axis=1, keepdims=True) * v_l

            od_ref[:, sl] = lax.fori_loop(1, blk, lag_step, jnp.sum(q * k, axis=1, keepdims=True) * v)
        carried_state_and_output()

    @pl.when(t == pl.num_programs(1) - 1)
    def _():
        for h in range(HG_HEADS):
            sout_ref[0, h] = st_ref[h].T


def _hgrn_cumsum_matrix(n, blk):
    t = np.arange(n)[:, None]
    s = np.arange(n)[None, :]
    return jnp.asarray((((t // blk) == (s // blk)) & (s <= t)).astype(np.float32), BF16)


def _hgrn(hq, hk, lfh, hi, shg, s0, norm, *, nseq):
    rows = hq.shape[0]
    t_len = rows // nseq
    tc = min(HG_TC, t_len)
    nt = t_len // tc
    blk = min(HG_BLOCK, tc)
    tri = _hgrn_cumsum_matrix(tc, blk)
    row = pl.BlockSpec((tc, HG_WIDTH), lambda b, t: (b * nt + t, 0))
    state = pl.BlockSpec((1, HG_HEADS, HG_DIM, HG_DIM), lambda b, t: (b, 0, 0, 0))
    return pl.pallas_call(
        _hgrn_kernel,
        grid=(nseq, nt),
        in_specs=[row, row, row, row, row, state,
                  pl.BlockSpec((1, HG_DIM), lambda b, t: (0, 0)),
                  pl.BlockSpec((tc, tc), lambda b, t: (0, 0))],
        out_specs=[row, state],
        out_shape=[jax.ShapeDtypeStruct((rows, HG_WIDTH), BF16),
                   jax.ShapeDtypeStruct((nseq, HG_HEADS, HG_DIM, HG_DIM), F32)],
        scratch_shapes=[
            pltpu.VMEM((HG_HEADS, HG_DIM, HG_DIM), F32),
            pltpu.VMEM((tc, HG_WIDTH), BF16),
            pltpu.VMEM((tc, HG_WIDTH), BF16),
            pltpu.VMEM((tc, HG_WIDTH), F32),
            pltpu.VMEM((tc, HG_WIDTH), F32),
            pltpu.VMEM((tc, HG_WIDTH), F32),
            pltpu.VMEM((tc, HG_WIDTH), F32),
        ],
        compiler_params=pltpu.CompilerParams(dimension_semantics=("arbitrary", "arbitrary"),
                                             vmem_limit_bytes=VMEM_LIMIT),
        name="hgrn",
    )(hq, hk, lfh, hi, shg, s0, norm, tri)


def _mixffn_kernel(x_ref, of_ref, oh_ref, sga_ref, sgb_ref, hist_ref,
                   wbf_ref, wbh_ref, wout_ref, wup_ref, wdn_ref,
                   npost_ref, npre2_ref, npost2_ref, cw_ref, cb_ref,
                   y_ref, conv_ref, tail_ref, *, seg_len):
    tm = x_ref.shape[0]
    br_f = jnp.dot(of_ref[...], wbf_ref[...], preferred_element_type=F32)
    br_h = jnp.dot(oh_ref[...], wbh_ref[...], preferred_element_type=F32)
    merged = sga_ref[...].astype(F32) * br_f + sgb_ref[...].astype(F32) * br_h
    mix = jnp.dot(merged.astype(BF16), wout_ref[...], preferred_element_type=F32)
    x1 = x_ref[...] + _rms_scale(mix) * npost_ref[...]

    h2 = (_rms_scale(x1) * npre2_ref[...]).astype(BF16)
    up = jnp.dot(h2, wup_ref[...], preferred_element_type=F32)
    a = up[:, :D_FF]
    g = up[:, D_FF:]

    prev1 = pltpu.roll(a, 1, axis=0)
    prev2 = pltpu.roll(a, 2, axis=0)
    rowi = lax.broadcasted_iota(jnp.int32, a.shape, 0)
    if seg_len >= tm:
        @pl.when(pl.program_id(0) == 0)
        def _():
            tail_ref[...] = hist_ref[0]
        t0 = tail_ref[0:1, :]
        t1 = tail_ref[1:2, :]
        prev1 = jnp.where(rowi == 0, t1, prev1)
        prev2 = jnp.where(rowi == 0, t0, jnp.where(rowi == 1, t1, prev2))
        tail_ref[...] = a[tm - 2:, :]
        conv_ref[0] = a[tm - 2:, :]
    else:
        for s in range(tm // seg_len):
            h0 = hist_ref[s, 0:1, :]
            h1 = hist_ref[s, 1:2, :]
            prev1 = jnp.where(rowi == s * seg_len, h1, prev1)
            prev2 = jnp.where(rowi == s * seg_len, h0, jnp.where(rowi == s * seg_len + 1, h1, prev2))
            conv_ref[s] = a[(s + 1) * seg_len - 2:(s + 1) * seg_len, :]
    c = cb_ref[...] + cw_ref[0:1, :] * prev2 + cw_ref[1:2, :] * prev1 + cw_ref[2:3, :] * a
    act = (jax.nn.gelu(c, approximate=True) * g).astype(BF16)
    ff = jnp.dot(act, wdn_ref[...], preferred_element_type=F32)
    y_ref[...] = x1 + _rms_scale(ff) * npost2_ref[...]


def _mixffn(x, o_fox, o_hg, sga, sgb, hist, w, *, seg_len):
    rows = x.shape[0]
    tm = min(FFN_TM, rows)
    n = rows // tm
    nseg = hist.shape[0]
    row = lambda width: pl.BlockSpec((tm, width), lambda i: (i, 0))
    weights = [w["bf"], w["bh"], w["out"], w["up"], w["down"]]
    smalls = [w["npost"], w["npre2"], w["npost2"], w["conv_w"], w["conv_b"]]
    hist_spec = pl.BlockSpec(hist.shape, lambda i: (0, 0, 0))
    scratch = [pltpu.VMEM((2, D_FF), F32)]
    return pl.pallas_call(
        functools.partial(_mixffn_kernel, seg_len=seg_len),
        grid=(n,),
        in_specs=[row(D_MODEL), row(FOX_WIDTH), row(HG_WIDTH), row(D_MODEL), row(D_MODEL), hist_spec]
                 + [_const_spec(a.shape) for a in weights] + [_const_spec(a.shape) for a in smalls],
        out_specs=[row(D_MODEL), pl.BlockSpec((nseg, 2, D_FF), lambda i: (0, 0, 0))],
        out_shape=[jax.ShapeDtypeStruct((rows, D_MODEL), F32), jax.ShapeDtypeStruct((nseg, 2, D_FF), F32)],
        scratch_shapes=scratch,
        compiler_params=pltpu.CompilerParams(dimension_semantics=("arbitrary",), vmem_limit_bytes=VMEM_LIMIT),
        name="mixffn",
    )(x, o_fox, o_hg, sga, sgb, hist, *weights, *smalls)


def _prep_w_in(w_in, fox_f_bias):
    offs = np.cumsum([0] + IN_SIZES)
    seg = [w_in[:, int(offs[i]):int(offs[i + 1])] for i in range(len(IN_SIZES))]
    pad = jnp.zeros((D_MODEL, LANES - 3 * FOX_HEADS), w_in.dtype)
    f3 = jnp.concatenate([seg[3], seg[3], seg[3], pad], axis=1)
    w_all = jnp.concatenate(seg[:3] + [f3] + seg[4:], axis=1).astype(BF16)
    fb = fox_f_bias.astype(F32)
    fb3 = jnp.concatenate([fb, fb, fb, jnp.zeros((LANES - 3 * FOX_HEADS,), F32)]).reshape(1, LANES)
    return w_all, fb3


def kernel(x_prompt, x_sample, cache_fox_k, cache_fox_v, cache_fox_logf, state_hgrn, state_ffn_conv, norm_mix_pre, norm_mix_post, w_in, fox_f_bias, hgrn_lb_logits, hgrn_norm, w_branch_fox, w_branch_hgrn, w_out, norm_ffn_pre, norm_ffn_post, w_up, ffn_conv_w, ffn_conv_b, w_down):
    depth = w_in.shape[0]
    assert depth == 1 and hgrn_lb_logits.shape[0] == 2
    bp, seq, _ = x_prompt.shape
    assert bp == 1
    nb, t_new, _ = x_sample.shape
    past = cache_fox_k.shape[2]

    w_all, fb3 = _prep_w_in(w_in[0], fox_f_bias[0])
    g_pre = norm_mix_pre[0].reshape(1, D_MODEL)
    lbl = hgrn_lb_logits.astype(F32)
    hnorm = hgrn_norm[0].astype(F32).reshape(1, HG_DIM)
    w = {
        "bf": w_branch_fox[0].astype(BF16), "bh": w_branch_hgrn[0].astype(BF16), "out": w_out[0].astype(BF16),
        "up": w_up[0].astype(BF16), "down": w_down[0].astype(BF16),
        "npost": norm_mix_post[0].reshape(1, D_MODEL), "npre2": norm_ffn_pre[0].reshape(1, D_MODEL),
        "npost2": norm_ffn_post[0].reshape(1, D_MODEL),
        "conv_w": ffn_conv_w[0], "conv_b": ffn_conv_b[0].reshape(1, D_FF),
    }

    xp = x_prompt.reshape(seq, D_MODEL)
    (qt, kh, vt, stats, pk, pv, plf, hq, hk, lfh, hi, shg, sga, sgb) = _proj(xp, g_pre, w_all, fb3, lbl, fold=True)
    o_fox = _fox_prompt(qt, kh, vt, stats)
    s0 = jnp.zeros((1, HG_HEADS, HG_DIM, HG_DIM), F32)
    o_hg, p_state = _hgrn(hq, hk, lfh, hi, shg, s0, hnorm, nseq=1)
    hist0 = jnp.zeros((1, 2, D_FF), F32)
    yp, pconv = _mixffn(xp, o_fox, o_hg, sga, sgb, hist0, w, seg_len=seq)

    xs = x_sample.reshape(nb * t_new, D_MODEL)
    (qs, sk, sv, slf, hq, hk, lfh, hi, shg, sga, sgb) = _proj(xs, g_pre, w_all, fb3, lbl, fold=False)
    lf_all_t = jnp.concatenate([
        jnp.swapaxes(cache_fox_logf[0].astype(F32), 1, 2),
        jnp.swapaxes(slf.reshape(nb, t_new, FOX_HEADS), 1, 2),
        jnp.zeros((nb, FOX_HEADS, LANES - t_new), F32)], axis=2)
    o_fox_s = _fox_sample(qs, sk, sv, lf_all_t,
                          cache_fox_k[0].reshape(nb, past, FOX_WIDTH), cache_fox_v[0].reshape(nb, past, FOX_WIDTH),
                          t_new=t_new)
    o_hg_s, s_state = _hgrn(hq, hk, lfh, hi, shg, state_hgrn[0].astype(F32), hnorm, nseq=nb)
    ys, sconv = _mixffn(xs, o_fox_s, o_hg_s, sga, sgb, state_ffn_conv[0], w, seg_len=t_new)

    return (
        yp.reshape(bp, seq, D_MODEL),
        ys.reshape(nb, t_new, D_MODEL),
        pk.reshape(1, bp, seq, FOX_HEADS, FOX_HEAD_DIM),
        pv.reshape(1, bp, seq, FOX_HEADS, FOX_HEAD_DIM),
        plf.reshape(1, bp, seq, FOX_HEADS),
        p_state.reshape(1, bp, HG_HEADS, HG_DIM, HG_DIM),
        pconv.reshape(1, bp, 2, D_FF),
        sk.reshape(1, nb, t_new, FOX_HEADS, FOX_HEAD_DIM),
        sv.reshape(1, nb, t_new, FOX_HEADS, FOX_HEAD_DIM),
        slf.reshape(1, nb, t_new, FOX_HEADS),
        s_state.reshape(1, nb, HG_HEADS, HG_DIM, HG_DIM),
        sconv.reshape(1, nb, 2, D_FF),
    )
```

```python
import functools

import numpy as np
import jax
import jax.numpy as jnp
from jax import lax
from jax.experimental import pallas as pl
from jax.experimental.pallas import tpu as pltpu

F32 = jnp.float32
BF16 = jnp.bfloat16

D_MODEL = 1024
FOX_HEADS = 8
FOX_HEAD_DIM = 64
FOX_WIDTH = FOX_HEADS * FOX_HEAD_DIM
HG_HEADS = 4
HG_DIM = 128
HG_WIDTH = HG_HEADS * HG_DIM
D_FF = 2816
RMS_EPS = 1e-6
NEG_INF = -1e30
LOG2E = 1.4426950408889634
FOX_SKIP_NATS = 110.0
IN_SIZES = [FOX_WIDTH, FOX_WIDTH, FOX_WIDTH, FOX_HEADS, HG_WIDTH, HG_WIDTH, HG_WIDTH, HG_WIDTH, D_MODEL, D_MODEL]

LANES = 128
FOX_PAD = 2 * FOX_HEAD_DIM
FOX_V_ROWS = FOX_HEAD_DIM + 16
HG_BLOCK = 64
HG_SPLIT_MAX = 60.0
VMEM_LIMIT = 56 * 1024 * 1024

PROJ_TM = 256
FOX_TQ = 512
FOX_TK = 512
HG_TC = 256
FFN_TM = 256
SAMPLE_TK = 1024

_C_Q, _C_K, _C_V, _C_F = 0, 512, 1024, 1536
_C_HQ, _C_HF, _C_HI, _C_HG = 1664, 2176, 2688, 3200
_C_GA, _C_GB, _C_END = 3712, 4736, 5760


def _split3(x):
    hi = x.astype(BF16)
    r = x - hi.astype(F32)
    mid = r.astype(BF16)
    lo = (r - mid.astype(F32)).astype(BF16)
    return hi, mid, lo


def _sum_by_01_matrix(mat01, x):
    cat = jnp.concatenate(_split3(x), axis=1)
    y = jnp.dot(mat01, cat, preferred_element_type=F32)
    return y[:, :LANES] + y[:, LANES:2 * LANES] + y[:, 2 * LANES:]


def _rms_scale(x):
    return x * lax.rsqrt(jnp.mean(x * x, axis=-1, keepdims=True) + RMS_EPS)


def _log_sigmoid(x):
    return jnp.minimum(x, 0.0) - jnp.log1p(jnp.exp(-jnp.abs(x)))


def _sigmoid(x):
    return 1.0 / (1.0 + jnp.exp(-x))


def _proj_kernel(*refs, fold):
    if fold:
        (x_ref, g_ref, w_ref, fb_ref, lbl_ref, tri_ref, pq_ref, pk_ref, cq_ref, ck_ref, cv_ref, seg_ref,
         qh_ref, kh_ref, vh_ref, stat_ref, kout_ref, vout_ref, lf_ref, hq_ref, hk_ref, lfh_ref, hi_ref, shg_ref,
         sga_ref, sgb_ref, carry_ref) = refs
    else:
        (x_ref, g_ref, w_ref, fb_ref, lbl_ref,
         qs_ref, kout_ref, vout_ref, lf_ref, hq_ref, hk_ref, lfh_ref, hi_ref, shg_ref,
         sga_ref, sgb_ref) = refs

    h = (_rms_scale(x_ref[...]) * g_ref[...]).astype(BF16)
    z = jnp.dot(h, w_ref[...], preferred_element_type=F32)

    zq = z[:, _C_Q:_C_K] * (FOX_HEAD_DIM ** -0.5)
    zk = z[:, _C_K:_C_V]
    zv = z[:, _C_V:_C_F]
    kout_ref[...] = zk
    vout_ref[...] = zv
    logf = _log_sigmoid(z[:, _C_F:_C_HQ] + fb_ref[...])
    lf_ref[...] = logf[:, :FOX_HEADS]

    l0 = lbl_ref[0:1, :]
    l1 = lbl_ref[1:2, :]
    lmax = jnp.maximum(l0, l1)
    e0 = jnp.exp(l0 - lmax)
    lb = e0 / (e0 + jnp.exp(l1 - lmax))
    f = lb + (1.0 - lb) * _sigmoid(z[:, _C_HF:_C_HI])
    hq_ref[...] = z[:, _C_HQ:_C_HF].astype(BF16)
    hk_ref[...] = (1.0 - f).astype(BF16)
    lfh_ref[...] = jnp.log(f)
    hi_ref[...] = z[:, _C_HI:_C_HG].astype(BF16)
    shg_ref[...] = _sigmoid(z[:, _C_HG:_C_GA]).astype(BF16)
    sga_ref[...] = _sigmoid(z[:, _C_GA:_C_GB]).astype(BF16)
    sgb_ref[...] = _sigmoid(z[:, _C_GB:_C_END]).astype(BF16)

    if not fold:
        qs_ref[...] = zq.astype(BF16)
        return

    @pl.when(pl.program_id(0) == 0)
    def _():
        carry_ref[...] = jnp.zeros_like(carry_ref)

    cum = carry_ref[...] + _sum_by_01_matrix(tri_ref[...], logf)
    carry_ref[...] = cum[-1:, :]

    seg = seg_ref[...]
    qn2 = jnp.dot((zq * zq).astype(BF16), seg, preferred_element_type=F32)
    kn2 = jnp.dot((zk * zk).astype(BF16), seg, preferred_element_type=F32)
    dg = jnp.dot((zq * zk).astype(BF16), seg, preferred_element_type=F32)
    stat_ref[0, 0:1, :] = jnp.max(qn2, axis=0, keepdims=True)
    stat_ref[0, 1:2, :] = jnp.max(kn2, axis=0, keepdims=True)
    stat_ref[0, 2:3, :] = jnp.min(dg, axis=0, keepdims=True)
    stat_ref[0, 3:4, :] = cum[0:1, :]
    stat_ref[0, 4:5, :] = cum[-1:, :]
    stat_ref[0, 5:8, :] = jnp.zeros((3, LANES), F32)

    zq = zq * LOG2E
    c_hi, c_mid, c_lo = _split3(cum * LOG2E)
    lane = lax.broadcasted_iota(jnp.int32, cum.shape, 1)
    pieces = jnp.where(lane < 8, c_hi, jnp.where(lane < 16, c_mid, c_lo))
    pieces = jnp.where(lane < 24, pieces, jnp.zeros_like(pieces))
    ex_q = jnp.dot(pieces, pq_ref[...], preferred_element_type=F32) + cq_ref[...]
    ex_k = jnp.dot(pieces, pk_ref[...], preferred_element_type=F32) + ck_ref[...]
    ex_v = cv_ref[...]

    low = lax.broadcasted_iota(jnp.int32, (zq.shape[0], LANES), 1) < FOX_HEAD_DIM
    for src, ex, dst, transposed in ((zq, ex_q, qh_ref, True), (zk, ex_k, kh_ref, False), (zv, ex_v, vh_ref, True)):
        for c in range(FOX_WIDTH // LANES):
            pair = src[:, c * LANES:(c + 1) * LANES]
            swapped = pltpu.roll(pair, FOX_HEAD_DIM, axis=1)
            for j, data in enumerate((pair, swapped)):
                hd = 2 * c + j
                blk = jnp.where(low, data, ex[:, hd * LANES:(hd + 1) * LANES])
                if transposed:
                    dst[hd, 0] = blk.T[:dst.shape[2]].astype(BF16)
                else:
                    dst[hd] = blk.astype(BF16)


def _bias_fold_constants():
    pq = np.zeros((LANES, FOX_HEADS * LANES), np.float32)
    pk = np.zeros((LANES, FOX_HEADS * LANES), np.float32)
    cq = np.zeros((1, FOX_HEADS * LANES), np.float32)
    ck = np.zeros((1, FOX_HEADS * LANES), np.float32)
    cv = np.zeros((1, FOX_HEADS * LANES), np.float32)
    for h in range(FOX_HEADS):
        base = h * LANES + FOX_HEAD_DIM
        for p in range(3):
            pq[p * 8 + h, base + p] = 1.0
            ck[0, base + p] = 1.0
            pk[p * 8 + h, base + 3 + p] = -1.0
            cq[0, base + 3 + p] = 1.0
        cv[0, base] = 1.0
    return (jnp.asarray(pq, BF16), jnp.asarray(pk, BF16), jnp.asarray(cq), jnp.asarray(ck), jnp.asarray(cv))


def _const_spec(shape, single=True):
    nd = len(shape)
    if single:
        return pl.BlockSpec(shape, lambda *_: (0,) * nd, pipeline_mode=pl.Buffered(1))
    return pl.BlockSpec(shape, lambda *_: (0,) * nd)


def _proj(x, gain, w_all, fb3, lb_logits, *, fold):
    rows = x.shape[0]
    tm = min(PROJ_TM, rows)
    n = rows // tm
    row = lambda width: pl.BlockSpec((tm, width), lambda i: (i, 0))
    in_specs = [row(D_MODEL), _const_spec((1, D_MODEL)), _const_spec(w_all.shape), _const_spec((1, LANES)),
                _const_spec(lb_logits.shape)]
    args = [x, gain, w_all, fb3, lb_logits]
    common_out = [
        (jax.ShapeDtypeStruct((rows, FOX_WIDTH), F32), row(FOX_WIDTH)),
        (jax.ShapeDtypeStruct((rows, FOX_WIDTH), F32), row(FOX_WIDTH)),
        (jax.ShapeDtypeStruct((rows, FOX_HEADS), F32), row(FOX_HEADS)),
        (jax.ShapeDtypeStruct((rows, HG_WIDTH), BF16), row(HG_WIDTH)),
        (jax.ShapeDtypeStruct((rows, HG_WIDTH), BF16), row(HG_WIDTH)),
        (jax.ShapeDtypeStruct((rows, HG_WIDTH), F32), row(HG_WIDTH)),
        (jax.ShapeDtypeStruct((rows, HG_WIDTH), BF16), row(HG_WIDTH)),
        (jax.ShapeDtypeStruct((rows, HG_WIDTH), BF16), row(HG_WIDTH)),
        (jax.ShapeDtypeStruct((rows, D_MODEL), BF16), row(D_MODEL)),
        (jax.ShapeDtypeStruct((rows, D_MODEL), BF16), row(D_MODEL)),
    ]
    scratch = []
    if fold:
        tri = jnp.asarray(np.tril(np.ones((tm, tm), np.float32)), BF16)
        seg = np.zeros((FOX_WIDTH, LANES), np.float32)
        seg[np.arange(FOX_WIDTH), np.arange(FOX_WIDTH) // FOX_HEAD_DIM] = 1.0
        consts = _bias_fold_constants() + (jnp.asarray(seg, BF16),)
        in_specs += [_const_spec(tri.shape)] + [_const_spec(c.shape) for c in consts]
        args += [tri, *consts]
        head_major = (jax.ShapeDtypeStruct((FOX_HEADS, rows, FOX_PAD), BF16),
                      pl.BlockSpec((FOX_HEADS, tm, FOX_PAD), lambda i: (0, i, 0)))
        head_major_t = lambda depth: (jax.ShapeDtypeStruct((FOX_HEADS, n, depth, tm), BF16),
                                      pl.BlockSpec((FOX_HEADS, 1, depth, tm), lambda i: (0, i, 0, 0)))
        stats = (jax.ShapeDtypeStruct((n, 8, LANES), F32), pl.BlockSpec((1, 8, LANES), lambda i: (i, 0, 0)))
        outs = [head_major_t(FOX_PAD), head_major, head_major_t(FOX_V_ROWS), stats] + common_out
        scratch = [pltpu.VMEM((1, LANES), F32)]
    else:
        outs = [(jax.ShapeDtypeStruct((rows, FOX_WIDTH), BF16), row(FOX_WIDTH))] + common_out
    return pl.pallas_call(
        functools.partial(_proj_kernel, fold=fold),
        grid=(n,),
        in_specs=in_specs,
        out_specs=[o[1] for o in outs],
        out_shape=[o[0] for o in outs],
        scratch_shapes=scratch,
        compiler_params=pltpu.CompilerParams(dimension_semantics=("arbitrary",), vmem_limit_bytes=VMEM_LIMIT),
        name="proj_fold" if fold else "proj",
    )(*args)


def _fox_kernel(qi_ref, ki_ref, mode_ref, qt_ref, k_ref, vt_ref, o_ref, m_ref, acc_ref):
    mode = mode_ref[pl.program_id(0)]
    tiles, tile = qt_ref.shape[1], qt_ref.shape[3]
    tq = tiles * tile
    tk = k_ref.shape[1]

    @pl.when((mode & _MODE_FIRST) != 0)
    def _():
        m_ref[...] = jnp.full_like(m_ref, NEG_INF)
        acc_ref[...] = jnp.zeros_like(acc_ref)

    def logits(h):
        qt = jnp.concatenate([qt_ref[h, j] for j in range(tiles)], axis=1)
        return jnp.dot(k_ref[h], qt, preferred_element_type=F32)

    def softmax_update(h, s, masked):
        if masked:
            key = lax.broadcasted_iota(jnp.int32, (tk, tq), 0)
            qry = lax.broadcasted_iota(jnp.int32, (tk, tq), 1)
            s = jnp.where(key <= qry, s, NEG_INF)
        m_prev = m_ref[h]
        m_new = jnp.maximum(m_prev, jnp.max(s, axis=0, keepdims=True))
        alpha = jnp.exp2(m_prev - m_new)
        p = jnp.exp2(s - m_new).astype(BF16)
        pv = sum(jnp.dot(vt_ref[h, j], p[j * tile:(j + 1) * tile], preferred_element_type=F32)
                 for j in range(tiles))
        acc_ref[h] = alpha * acc_ref[h] + pv
        m_ref[h] = m_new

    def all_heads(masked):
        s_next = logits(0)
        for h in range(FOX_HEADS):
            s = s_next
            if h + 1 < FOX_HEADS:
                s_next = logits(h + 1)
            softmax_update(h, s, masked)

    @pl.when((mode & _MODE_PAIR) != 0)
    def _():
        all_heads(False)

    @pl.when((mode & _MODE_DIAG) != 0)
    def _():
        all_heads(True)
        for c in range(FOX_WIDTH // LANES):
            halves = []
            for hd in (2 * c, 2 * c + 1):
                acc = acc_ref[hd]
                halves.append(acc[:FOX_HEAD_DIM, :] / acc[FOX_HEAD_DIM:FOX_HEAD_DIM + 1, :])
            o_ref[:, c * LANES:(c + 1) * LANES] = jnp.concatenate(halves, axis=0).T.astype(o_ref.dtype)


_MODE_FIRST, _MODE_PAIR, _MODE_DIAG = 1, 2, 4


def _fox_schedule(stats, nq):
    st = stats[:, :5, :FOX_HEADS].reshape(nq, -1, 5, FOX_HEADS)
    qn = jnp.sqrt(jnp.max(st[:, :, 0], axis=1)) * 1.01
    kn = jnp.sqrt(jnp.max(st[:, :, 1], axis=1)) * 1.01
    dmin = jnp.min(st[:, :, 2], axis=1)
    c_first = st[:, 0, 3]
    c_last = st[:, -1, 4]
    bound = qn[:, None] * kn[None, :] + (c_first - dmin)[:, None] - c_last[None, :]
    blk = jnp.arange(nq, dtype=jnp.int32)
    drop = jnp.all(bound < -FOX_SKIP_NATS, axis=-1) & (blk[None, :] < blk[:, None])
    prefix = jnp.cumsum(jnp.logical_not(drop).astype(jnp.int32), axis=1) == 0
    kstart = jnp.sum(prefix.astype(jnp.int32), axis=1)
    count = blk - kstart + 1
    ends = jnp.cumsum(count)
    nsteps = nq * (nq + 1) // 2
    step = jnp.arange(nsteps, dtype=jnp.int32)
    valid = step < ends[-1]
    done = ends[None, :] <= step[:, None]
    q_of = jnp.minimum(jnp.sum(done.astype(jnp.int32), axis=1), nq - 1)
    begin = jnp.sum(jnp.where(done, count[None, :], 0), axis=1)
    k_first = jnp.sum(jnp.where(blk[None, :] == q_of[:, None], kstart[None, :], 0), axis=1)
    k_of = k_first + (step - begin)
    mode = jnp.where(k_of == q_of, _MODE_DIAG, _MODE_PAIR) + jnp.where(k_of == k_first, _MODE_FIRST, 0)
    mode = jnp.where(valid, mode, 0)
    q_of = jnp.where(valid, q_of, nq - 1)
    k_of = jnp.where(valid, k_of, nq - 1)
    return q_of.astype(jnp.int32), k_of.astype(jnp.int32), mode.astype(jnp.int32)


def _fox_prompt(qt, kh, vt, stats):
    seq = kh.shape[1]
    assert FOX_TQ == FOX_TK
    nq = seq // FOX_TQ
    tile = qt.shape[3]
    tiles = FOX_TQ // tile
    qi, ki, mode = _fox_schedule(stats, nq)
    grid_spec = pltpu.PrefetchScalarGridSpec(
        num_scalar_prefetch=3,
        grid=(nq * (nq + 1) // 2,),
        in_specs=[
            pl.BlockSpec((FOX_HEADS, tiles, FOX_PAD, tile), lambda s, qi, ki, mode: (0, qi[s], 0, 0)),
            pl.BlockSpec((FOX_HEADS, FOX_TK, FOX_PAD), lambda s, qi, ki, mode: (0, ki[s], 0)),
            pl.BlockSpec((FOX_HEADS, tiles, FOX_V_ROWS, tile), lambda s, qi, ki, mode: (0, ki[s], 0, 0)),
        ],
        out_specs=pl.BlockSpec((FOX_TQ, FOX_WIDTH), lambda s, qi, ki, mode: (qi[s], 0)),
        scratch_shapes=[pltpu.VMEM((FOX_HEADS, 1, FOX_TQ), F32),
                        pltpu.VMEM((FOX_HEADS, FOX_V_ROWS, FOX_TQ), F32)],
    )
    return pl.pallas_call(
        _fox_kernel,
        grid_spec=grid_spec,
        out_shape=jax.ShapeDtypeStruct((seq, FOX_WIDTH), BF16),
        compiler_params=pltpu.CompilerParams(dimension_semantics=("arbitrary",), vmem_limit_bytes=VMEM_LIMIT),
        name="fox_prompt",
    )(qi, ki, mode, qt, kh, vt)


def _lane_cumsum(x):
    n = x.shape[1]
    lane = lax.broadcasted_iota(jnp.int32, x.shape, 1)
    shift = 1
    while shift < n:
        x = x + jnp.where(lane >= shift, pltpu.roll(x, shift, axis=1), 0.0)
        shift *= 2
    return x


def _fox_sample_kernel(q_ref, kn_ref, vn_ref, lft_ref, ck_ref, cv_ref, o_ref,
                       qbd_ref, cq_ref, cum_ref, m_ref, l_ref, acc_ref, *, n_past_chunks, tk, t_new):
    c = pl.program_id(1)
    rows = FOX_HEADS * t_new
    lane_head = lax.broadcasted_iota(jnp.int32, (t_new, FOX_WIDTH), 1) // FOX_HEAD_DIM

    @pl.when(c == 0)
    def _():
        cum = _lane_cumsum(lft_ref[0])
        for j in range(n_past_chunks):
            cum_ref[j] = cum[:, j * tk:(j + 1) * tk]
        new_cum = cum[:, n_past_chunks * tk:n_past_chunks * tk + LANES]
        cum_ref[n_past_chunks, :, :LANES] = new_cum
        new_cum_t = jnp.concatenate([new_cum] * (LANES // FOX_HEADS), axis=0).T
        q = q_ref[...]
        for h in range(FOX_HEADS):
            qbd_ref[h * t_new:(h + 1) * t_new, :] = jnp.where(lane_head == h, q, jnp.zeros_like(q))
            cq_ref[h * t_new:(h + 1) * t_new, :] = jnp.broadcast_to(new_cum_t[:t_new, h:h + 1], (t_new, LANES))
        m_ref[...] = jnp.full_like(m_ref, NEG_INF)
        l_ref[...] = jnp.zeros_like(l_ref)
        acc_ref[...] = jnp.zeros_like(acc_ref)

    def update(k, v, ck_rows, mask):
        s = lax.dot_general(qbd_ref[...], k, (((1,), (1,)), ((), ())), preferred_element_type=F32)
        s = s + (cq_ref[:, :1] - ck_rows)
        if mask is not None:
            s = jnp.where(mask, s, NEG_INF)
        m_prev = m_ref[...]
        m_new = jnp.maximum(m_prev, jnp.max(s, axis=1, keepdims=True))
        alpha = jnp.exp(m_prev - m_new)
        p = jnp.exp(s - m_new[:, :1])
        l_ref[...] = alpha * l_ref[...] + jnp.sum(p, axis=1, keepdims=True)
        acc_ref[...] = alpha[:, :1] * acc_ref[...] + jnp.dot(p.astype(BF16), v, preferred_element_type=F32)
        m_ref[...] = m_new

    def expand_rows(x, width):
        return jnp.concatenate([jnp.broadcast_to(x[h:h + 1, :], (t_new, width)) for h in range(FOX_HEADS)], axis=0)

    @pl.when(c < n_past_chunks)
    def _():
        update(ck_ref[0].astype(BF16), cv_ref[0].astype(BF16), expand_rows(cum_ref[c], tk), None)

    @pl.when(c == n_past_chunks)
    def _():
        ck_rows = expand_rows(cum_ref[n_past_chunks, :, :LANES], LANES)[:, :t_new]
        rowt = lax.broadcasted_iota(jnp.int32, (rows, t_new), 0) % t_new
        coli = lax.broadcasted_iota(jnp.int32, (rows, t_new), 1)
        update(kn_ref[...].astype(BF16), vn_ref[...].astype(BF16), ck_rows, coli <= rowt)
        out = acc_ref[...] / l_ref[:, :1]
        o = jnp.zeros((t_new, FOX_WIDTH), F32)
        for h in range(FOX_HEADS):
            o = o + jnp.where(lane_head == h, out[h * t_new:(h + 1) * t_new, :], 0.0)
        o_ref[...] = o.astype(o_ref.dtype)


def _fox_sample(qs, k_new, v_new, lf_all_t, cache_k, cache_v, *, t_new):
    nb, past = cache_k.shape[0], cache_k.shape[1]
    tk = SAMPLE_TK
    npc = past // tk
    rows = FOX_HEADS * t_new
    last = npc - 1
    kern = functools.partial(_fox_sample_kernel, n_past_chunks=npc, tk=tk, t_new=t_new)
    return pl.pallas_call(
        kern,
        grid=(nb, npc + 1),
        in_specs=[
            pl.BlockSpec((t_new, FOX_WIDTH), lambda b, c: (b, 0)),
            pl.BlockSpec((t_new, FOX_WIDTH), lambda b, c: (b, 0)),
            pl.BlockSpec((t_new, FOX_WIDTH), lambda b, c: (b, 0)),
            pl.BlockSpec((1, FOX_HEADS, past + LANES), lambda b, c: (b, 0, 0)),
            pl.BlockSpec((1, tk, FOX_WIDTH), lambda b, c: (b, jnp.minimum(c, last), 0)),
            pl.BlockSpec((1, tk, FOX_WIDTH), lambda b, c: (b, jnp.minimum(c, last), 0)),
        ],
        out_specs=pl.BlockSpec((t_new, FOX_WIDTH), lambda b, c: (b, 0)),
        out_shape=jax.ShapeDtypeStruct((nb * t_new, FOX_WIDTH), BF16),
        scratch_shapes=[
            pltpu.VMEM((rows, FOX_WIDTH), BF16),
            pltpu.VMEM((rows, LANES), F32),
            pltpu.VMEM((npc + 1, FOX_HEADS, tk), F32),
            pltpu.VMEM((rows, LANES), F32),
            pltpu.VMEM((rows, LANES), F32),
            pltpu.VMEM((rows, FOX_WIDTH), F32),
        ],
        compiler_params=pltpu.CompilerParams(dimension_semantics=("arbitrary", "arbitrary"),
                                             vmem_limit_bytes=VMEM_LIMIT),
        name="fox_sample",
    )(qs, k_new, v_new, lf_all_t, cache_k, cache_v)


def _hgrn_kernel(hq_ref, hk_ref, lfh_ref, hi_ref, shg_ref, s0_ref, norm_ref, tri_ref,
                 o_ref, sout_ref, st_ref, qq_ref, kk_ref, eb_ref, b_ref, oin_ref, od_ref):
    t = pl.program_id(1)
    tc = hq_ref.shape[0]
    blk = min(HG_BLOCK, tc)
    nblk = tc // blk
    heads = [slice(h * HG_DIM, (h + 1) * HG_DIM) for h in range(HG_HEADS)]

    @pl.when(t == 0)
    def _():
        for h in range(HG_HEADS):
            st_ref[h] = s0_ref[0, h].T

    def at_block_row(x, r):
        x3 = x.reshape(nblk, blk, HG_DIM)
        return jnp.broadcast_to(x3[:, r:r + 1, :], x3.shape).reshape(tc, HG_DIM)

    def rel_to_middle(b):
        return b - at_block_row(b, blk // 2 - 1)

    worst = jnp.zeros((), F32)
    for sl in heads:
        b = _sum_by_01_matrix(tri_ref[...], lfh_ref[:, sl])
        eb = jnp.exp(b)
        qq_ref[:, sl] = (hq_ref[:, sl].astype(F32) * eb).astype(BF16)
        kk_ref[:, sl] = (hk_ref[:, sl].astype(F32) * jnp.exp(at_block_row(b, blk - 1) - b)).astype(BF16)
        eb_ref[:, sl] = eb
        b_ref[:, sl] = b
        worst = jnp.maximum(worst, jnp.max(jnp.abs(rel_to_middle(b))))

    def carried_state_and_output():
        for j in range(nblk):
            rows = slice(j * blk, (j + 1) * blk)
            for h, sl in enumerate(heads):
                st = st_ref[h]
                oin_ref[rows, sl] = lax.dot_general(qq_ref[rows, sl], st.astype(BF16), (((1,), (1,)), ((), ())),
                                                    preferred_element_type=F32)
                upd = lax.dot_general(hi_ref[rows, sl], kk_ref[rows, sl], (((0,), (0,)), ((), ())),
                                      preferred_element_type=F32)
                st_ref[h] = st * eb_ref[(j + 1) * blk - 1:(j + 1) * blk, sl] + upd
        for sl in heads:
            o = oin_ref[:, sl] + od_ref[:, sl]
            y = (_rms_scale(o) * norm_ref[...]) * shg_ref[:, sl].astype(F32)
            o_ref[:, sl] = y.astype(o_ref.dtype)

    splittable = worst <= HG_SPLIT_MAX

    @pl.when(splittable)
    def _():
        row = lax.broadcasted_iota(jnp.int32, (tc, tc), 0)
        col = lax.broadcasted_iota(jnp.int32, (tc, tc), 1)
        pair_in_block = (row // blk == col // blk) & (col <= row)
        for sl in heads:
            b_rel = rel_to_middle(b_ref[:, sl])
            qs = (hq_ref[:, sl].astype(F32) * jnp.exp(b_rel)).astype(BF16)
            ks = (hk_ref[:, sl].astype(F32) * jnp.exp(-b_rel)).astype(BF16)
            a = lax.dot_general(qs, ks, (((1,), (1,)), ((), ())), preferred_element_type=F32)
            a = jnp.where(pair_in_block, a, 0.0).astype(BF16)
            od_ref[:, sl] = jnp.dot(a, hi_ref[:, sl], preferred_element_type=F32)
        carried_state_and_output()

    @pl.when(jnp.logical_not(splittable))
    def _():
        row_in_blk = lax.broadcasted_iota(jnp.int32, (tc, HG_DIM), 0) % blk
        for sl in heads:
            q = hq_ref[:, sl].astype(F32)
            k = hk_ref[:, sl].astype(F32)
            v = hi_ref[:, sl].astype(F32)
            b = b_ref[:, sl]

            def lag_step(lag, od):
                k_l = pltpu.roll(k, lag, axis=0)
                b_l = pltpu.roll(b, lag, axis=0)
                v_l = pltpu.roll(v, lag, axis=0)
                w = q * k_l * jnp.exp(jnp.minimum(b - b_l, 0.0))
                w = jnp.where(row_in_blk >= lag, w, 0.0)
                return od + jnp.sum(w, axis=1, keepdims=True) * v_l

            od_ref[:, sl] = lax.fori_loop(1, blk, lag_step, jnp.sum(q * k, axis=1, keepdims=True) * v)
        carried_state_and_output()

    @pl.when(t == pl.num_programs(1) - 1)
    def _():
        for h in range(HG_HEADS):
            sout_ref[0, h] = st_ref[h].T


def _hgrn_cumsum_matrix(n, blk):
    t = np.arange(n)[:, None]
    s = np.arange(n)[None, :]
    return jnp.asarray((((t // blk) == (s // blk)) & (s <= t)).astype(np.float32), BF16)


def _hgrn(hq, hk, lfh, hi, shg, s0, norm, *, nseq):
    rows = hq.shape[0]
    t_len = rows // nseq
    tc = min(HG_TC, t_len)
    nt = t_len // tc
    blk = min(HG_BLOCK, tc)
    tri = _hgrn_cumsum_matrix(tc, blk)
    row = pl.BlockSpec((tc, HG_WIDTH), lambda b, t: (b * nt + t, 0))
    state = pl.BlockSpec((1, HG_HEADS, HG_DIM, HG_DIM), lambda b, t: (b, 0, 0, 0))
    return pl.pallas_call(
        _hgrn_kernel,
        grid=(nseq, nt),
        in_specs=[row, row, row, row, row, state,
                  pl.BlockSpec((1, HG_DIM), lambda b, t: (0, 0)),
                  pl.BlockSpec((tc, tc), lambda b, t: (0, 0))],
        out_specs=[row, state],
        out_shape=[jax.ShapeDtypeStruct((rows, HG_WIDTH), BF16),
                   jax.ShapeDtypeStruct((nseq, HG_HEADS, HG_DIM, HG_DIM), F32)],
        scratch_shapes=[
            pltpu.VMEM((HG_HEADS, HG_DIM, HG_DIM), F32),
            pltpu.VMEM((tc, HG_WIDTH), BF16),
            pltpu.VMEM((tc, HG_WIDTH), BF16),
            pltpu.VMEM((tc, HG_WIDTH), F32),
            pltpu.VMEM((tc, HG_WIDTH), F32),
            pltpu.VMEM((tc, HG_WIDTH), F32),
            pltpu.VMEM((tc, HG_WIDTH), F32),
        ],
        compiler_params=pltpu.CompilerParams(dimension_semantics=("arbitrary", "arbitrary"),
                                             vmem_limit_bytes=VMEM_LIMIT),
        name="hgrn",
    )(hq, hk, lfh, hi, shg, s0, norm, tri)


def _mixffn_kernel(x_ref, of_ref, oh_ref, sga_ref, sgb_ref, hist_ref,
                   wbf_ref, wbh_ref, wout_ref, wup_ref, wdn_ref,
                   npost_ref, npre2_ref, npost2_ref, cw_ref, cb_ref,
                   y_ref, conv_ref, tail_ref, *, seg_len):
    tm = x_ref.shape[0]
    br_f = jnp.dot(of_ref[...], wbf_ref[...], preferred_element_type=F32)
    br_h = jnp.dot(oh_ref[...], wbh_ref[...], preferred_element_type=F32)
    merged = sga_ref[...].astype(F32) * br_f + sgb_ref[...].astype(F32) * br_h
    mix = jnp.dot(merged.astype(BF16), wout_ref[...], preferred_element_type=F32)
    x1 = x_ref[...] + _rms_scale(mix) * npost_ref[...]

    h2 = (_rms_scale(x1) * npre2_ref[...]).astype(BF16)
    up = jnp.dot(h2, wup_ref[...], preferred_element_type=F32)
    a = up[:, :D_FF]
    g = up[:, D_FF:]

    prev1 = pltpu.roll(a, 1, axis=0)
    prev2 = pltpu.roll(a, 2, axis=0)
    rowi = lax.broadcasted_iota(jnp.int32, a.shape, 0)
    if seg_len >= tm:
        @pl.when(pl.program_id(0) == 0)
        def _():
            tail_ref[...] = hist_ref[0]
        t0 = tail_ref[0:1, :]
        t1 = tail_ref[1:2, :]
        prev1 = jnp.where(rowi == 0, t1, prev1)
        prev2 = jnp.where(rowi == 0, t0, jnp.where(rowi == 1, t1, prev2))
        tail_ref[...] = a[tm - 2:, :]
        conv_ref[0] = a[tm - 2:, :]
    else:
        for s in range(tm // seg_len):
            h0 = hist_ref[s, 0:1, :]
            h1 = hist_ref[s, 1:2, :]
            prev1 = jnp.where(rowi == s * seg_len, h1, prev1)
            prev2 = jnp.where(rowi == s * seg_len, h0, jnp.where(rowi == s * seg_len + 1, h1, prev2))
            conv_ref[s] = a[(s + 1) * seg_len - 2:(s + 1) * seg_len, :]
    c = cb_ref[...] + cw_ref[0:1, :] * prev2 + cw_ref[1:2, :] * prev1 + cw_ref[2:3, :] * a
    act = (jax.nn.gelu(c, approximate=True) * g).astype(BF16)
    ff = jnp.dot(act, wdn_ref[...], preferred_element_type=F32)
    y_ref[...] = x1 + _rms_scale(ff) * npost2_ref[...]


def _mixffn(x, o_fox, o_hg, sga, sgb, hist, w, *, seg_len):
    rows = x.shape[0]
    tm = min(FFN_TM, rows)
    n = rows // tm
    nseg = hist.shape[0]
    row = lambda width: pl.BlockSpec((tm, width), lambda i: (i, 0))
    weights = [w["bf"], w["bh"], w["out"], w["up"], w["down"]]
    smalls = [w["npost"], w["npre2"], w["npost2"], w["conv_w"], w["conv_b"]]
    hist_spec = pl.BlockSpec(hist.shape, lambda i: (0, 0, 0))
    scratch = [pltpu.VMEM((2, D_FF), F32)]
    return pl.pallas_call(
        functools.partial(_mixffn_kernel, seg_len=seg_len),
        grid=(n,),
        in_specs=[row(D_MODEL), row(FOX_WIDTH), row(HG_WIDTH), row(D_MODEL), row(D_MODEL), hist_spec]
                 + [_const_spec(a.shape) for a in weights] + [_const_spec(a.shape) for a in smalls],
        out_specs=[row(D_MODEL), pl.BlockSpec((nseg, 2, D_FF), lambda i: (0, 0, 0))],
        out_shape=[jax.ShapeDtypeStruct((rows, D_MODEL), F32), jax.ShapeDtypeStruct((nseg, 2, D_FF), F32)],
        scratch_shapes=scratch,
        compiler_params=pltpu.CompilerParams(dimension_semantics=("arbitrary",), vmem_limit_bytes=VMEM_LIMIT),
        name="mixffn",
    )(x, o_fox, o_hg, sga, sgb, hist, *weights, *smalls)


def _prep_w_in(w_in, fox_f_bias):
    offs = np.cumsum([0] + IN_SIZES)
    seg = [w_in[:, int(offs[i]):int(offs[i + 1])] for i in range(len(IN_SIZES))]
    pad = jnp.zeros((D_MODEL, LANES - 3 * FOX_HEADS), w_in.dtype)
    f3 = jnp.concatenate([seg[3], seg[3], seg[3], pad], axis=1)
    w_all = jnp.concatenate(seg[:3] + [f3] + seg[4:], axis=1).astype(BF16)
    fb = fox_f_bias.astype(F32)
    fb3 = jnp.concatenate([fb, fb, fb, jnp.zeros((LANES - 3 * FOX_HEADS,), F32)]).reshape(1, LANES)
    return w_all, fb3


def kernel(x_prompt, x_sample, cache_fox_k, cache_fox_v, cache_fox_logf, state_hgrn, state_ffn_conv, norm_mix_pre, norm_mix_post, w_in, fox_f_bias, hgrn_lb_logits, hgrn_norm, w_branch_fox, w_branch_hgrn, w_out, norm_ffn_pre, norm_ffn_post, w_up, ffn_conv_w, ffn_conv_b, w_down):
    depth = w_in.shape[0]
    assert depth == 1 and hgrn_lb_logits.shape[0] == 2
    bp, seq, _ = x_prompt.shape
    assert bp == 1
    nb, t_new, _ = x_sample.shape
    past = cache_fox_k.shape[2]

    w_all, fb3 = _prep_w_in(w_in[0], fox_f_bias[0])
    g_pre = norm_mix_pre[0].reshape(1, D_MODEL)
    lbl = hgrn_lb_logits.astype(F32)
    hnorm = hgrn_norm[0].astype(F32).reshape(1, HG_DIM)
    w = {
        "bf": w_branch_fox[0].astype(BF16), "bh": w_branch_hgrn[0].astype(BF16), "out": w_out[0].astype(BF16),
        "up": w_up[0].astype(BF16), "down": w_down[0].astype(BF16),
        "npost": norm_mix_post[0].reshape(1, D_MODEL), "npre2": norm_ffn_pre[0].reshape(1, D_MODEL),
        "npost2": norm_ffn_post[0].reshape(1, D_MODEL),
        "conv_w": ffn_conv_w[0], "conv_b": ffn_conv_b[0].reshape(1, D_FF),
    }

    xp = x_prompt.reshape(seq, D_MODEL)
    (qt, kh, vt, stats, pk, pv, plf, hq, hk, lfh, hi, shg, sga, sgb) = _proj(xp, g_pre, w_all, fb3, lbl, fold=True)
    o_fox = _fox_prompt(qt, kh, vt, stats)
    s0 = jnp.zeros((1, HG_HEADS, HG_DIM, HG_DIM), F32)
    o_hg, p_state = _hgrn(hq, hk, lfh, hi, shg, s0, hnorm, nseq=1)
    hist0 = jnp.zeros((1, 2, D_FF), F32)
    yp, pconv = _mixffn(xp, o_fox, o_hg, sga, sgb, hist0, w, seg_len=seq)

    xs = x_sample.reshape(nb * t_new, D_MODEL)
    (qs, sk, sv, slf, hq, hk, lfh, hi, shg, sga, sgb) = _proj(xs, g_pre, w_all, fb3, lbl, fold=False)
    lf_all_t = jnp.concatenate([
        jnp.swapaxes(cache_fox_logf[0].astype(F32), 1, 2),
        jnp.swapaxes(slf.reshape(nb, t_new, FOX_HEADS), 1, 2),
        jnp.zeros((nb, FOX_HEADS, LANES - t_new), F32)], axis=2)
    o_fox_s = _fox_sample(qs, sk, sv, lf_all_t,
                          cache_fox_k[0].reshape(nb, past, FOX_WIDTH), cache_fox_v[0].reshape(nb, past, FOX_WIDTH),
                          t_new=t_new)
    o_hg_s, s_state = _hgrn(hq, hk, lfh, hi, shg, state_hgrn[0].astype(F32), hnorm, nseq=nb)
    ys, sconv = _mixffn(xs, o_fox_s, o_hg_s, sga, sgb, state_ffn_conv[0], w, seg_len=t_new)

    return (
        yp.reshape(bp, seq, D_MODEL),
        ys.reshape(nb, t_new, D_MODEL),
        pk.reshape(1, bp, seq, FOX_HEADS, FOX_HEAD_DIM),
        pv.reshape(1, bp, seq, FOX_HEADS, FOX_HEAD_DIM),
        plf.reshape(1, bp, seq, FOX_HEADS),
        p_state.reshape(1, bp, HG_HEADS, HG_DIM, HG_DIM),
        pconv.reshape(1, bp, 2, D_FF),
        sk.reshape(1, nb, t_new, FOX_HEADS, FOX_HEAD_DIM),
        sv.reshape(1, nb, t_new, FOX_HEADS, FOX_HEAD_DIM),
        slf.reshape(1, nb, t_new, FOX_HEADS),
        s_state.reshape(1, nb, HG_HEADS, HG_DIM, HG_DIM),
        sconv.reshape(1, nb, 2, D_FF),
    )
```

```python
import functools

import numpy as np
import jax
import jax.numpy as jnp
from jax import lax
from jax.experimental import pallas as pl
from jax.experimental.pallas import tpu as pltpu

F32 = jnp.float32
BF16 = jnp.bfloat16

D_MODEL = 1024
FOX_HEADS = 8
FOX_HEAD_DIM = 64
FOX_WIDTH = FOX_HEADS * FOX_HEAD_DIM
HG_HEADS = 4
HG_DIM = 128
HG_WIDTH = HG_HEADS * HG_DIM
D_FF = 2816
RMS_EPS = 1e-6
NEG_INF = -1e30
LOG2E = 1.4426950408889634
FOX_SKIP_NATS = 110.0
IN_SIZES = [FOX_WIDTH, FOX_WIDTH, FOX_WIDTH, FOX_HEADS, HG_WIDTH, HG_WIDTH, HG_WIDTH, HG_WIDTH, D_MODEL, D_MODEL]

LANES = 128
FOX_PAD = 2 * FOX_HEAD_DIM
FOX_V_ROWS = FOX_HEAD_DIM + 16
HG_BLOCK = 64
HG_SPLIT_MAX = 60.0
VMEM_LIMIT = 56 * 1024 * 1024

PROJ_TM = 256
FOX_TQ = 512
FOX_TK = 512
HG_TC = 256
FFN_TM = 256
SAMPLE_TK = 1024

_C_Q, _C_K, _C_V, _C_F = 0, 512, 1024, 1536
_C_HQ, _C_HF, _C_HI, _C_HG = 1664, 2176, 2688, 3200
_C_GA, _C_GB, _C_END = 3712, 4736, 5760


def _split3(x):
    hi = x.astype(BF16)
    r = x - hi.astype(F32)
    mid = r.astype(BF16)
    lo = (r - mid.astype(F32)).astype(BF16)
    return hi, mid, lo


def _sum_by_01_matrix(mat01, x):
    cat = jnp.concatenate(_split3(x), axis=1)
    y = jnp.dot(mat01, cat, preferred_element_type=F32)
    return y[:, :LANES] + y[:, LANES:2 * LANES] + y[:, 2 * LANES:]


def _rms_scale(x):
    return x * lax.rsqrt(jnp.mean(x * x, axis=-1, keepdims=True) + RMS_EPS)


def _log_sigmoid(x):
    return jnp.minimum(x, 0.0) - jnp.log1p(jnp.exp(-jnp.abs(x)))


def _sigmoid(x):
    return 1.0 / (1.0 + jnp.exp(-x))


def _proj_kernel(*refs, fold):
    if fold:
        (x_ref, g_ref, w_ref, fb_ref, lbl_ref, tri_ref, pq_ref, pk_ref, cq_ref, ck_ref, cv_ref, seg_ref,
         qh_ref, kh_ref, vh_ref, stat_ref, kout_ref, vout_ref, lf_ref, hq_ref, hk_ref, lfh_ref, hi_ref, shg_ref,
         sga_ref, sgb_ref, carry_ref) = refs
    else:
        (x_ref, g_ref, w_ref, fb_ref, lbl_ref,
         qs_ref, kout_ref, vout_ref, lf_ref, hq_ref, hk_ref, lfh_ref, hi_ref, shg_ref,
         sga_ref, sgb_ref) = refs

    h = (_rms_scale(x_ref[...]) * g_ref[...]).astype(BF16)
    z = jnp.dot(h, w_ref[...], preferred_element_type=F32)

    zq = z[:, _C_Q:_C_K] * (FOX_HEAD_DIM ** -0.5)
    zk = z[:, _C_K:_C_V]
    zv = z[:, _C_V:_C_F]
    kout_ref[...] = zk
    vout_ref[...] = zv
    logf = _log_sigmoid(z[:, _C_F:_C_HQ] + fb_ref[...])
    lf_ref[...] = logf[:, :FOX_HEADS]

    l0 = lbl_ref[0:1, :]
    l1 = lbl_ref[1:2, :]
    lmax = jnp.maximum(l0, l1)
    e0 = jnp.exp(l0 - lmax)
    lb = e0 / (e0 + jnp.exp(l1 - lmax))
    f = lb + (1.0 - lb) * _sigmoid(z[:, _C_HF:_C_HI])
    hq_ref[...] = z[:, _C_HQ:_C_HF].astype(BF16)
    hk_ref[...] = (1.0 - f).astype(BF16)
    lfh_ref[...] = jnp.log(f)
    hi_ref[...] = z[:, _C_HI:_C_HG].astype(BF16)
    shg_ref[...] = _sigmoid(z[:, _C_HG:_C_GA]).astype(BF16)
    sga_ref[...] = _sigmoid(z[:, _C_GA:_C_GB]).astype(BF16)
    sgb_ref[...] = _sigmoid(z[:, _C_GB:_C_END]).astype(BF16)

    if not fold:
        qs_ref[...] = zq.astype(BF16)
        return

    @pl.when(pl.program_id(0) == 0)
    def _():
        carry_ref[...] = jnp.zeros_like(carry_ref)

    cum = carry_ref[...] + _sum_by_01_matrix(tri_ref[...], logf)
    carry_ref[...] = cum[-1:, :]

    seg = seg_ref[...]
    qn2 = jnp.dot((zq * zq).astype(BF16), seg, preferred_element_type=F32)
    kn2 = jnp.dot((zk * zk).astype(BF16), seg, preferred_element_type=F32)
    dg = jnp.dot((zq * zk).astype(BF16), seg, preferred_element_type=F32)
    stat_ref[0, 0:1, :] = jnp.max(qn2, axis=0, keepdims=True)
    stat_ref[0, 1:2, :] = jnp.max(kn2, axis=0, keepdims=True)
    stat_ref[0, 2:3, :] = jnp.min(dg, axis=0, keepdims=True)
    stat_ref[0, 3:4, :] = cum[0:1, :]
    stat_ref[0, 4:5, :] = cum[-1:, :]
    stat_ref[0, 5:8, :] = jnp.zeros((3, LANES), F32)

    zq = zq * LOG2E
    c_hi, c_mid, c_lo = _split3(cum * LOG2E)
    lane = lax.broadcasted_iota(jnp.int32, cum.shape, 1)
    pieces = jnp.where(lane < 8, c_hi, jnp.where(lane < 16, c_mid, c_lo))
    pieces = jnp.where(lane < 24, pieces, jnp.zeros_like(pieces))
    ex_q = jnp.dot(pieces, pq_ref[...], preferred_element_type=F32) + cq_ref[...]
    ex_k = jnp.dot(pieces, pk_ref[...], preferred_element_type=F32) + ck_ref[...]
    ex_v = cv_ref[...]

    low = lax.broadcasted_iota(jnp.int32, (zq.shape[0], LANES), 1) < FOX_HEAD_DIM
    for src, ex, dst, transposed in ((zq, ex_q, qh_ref, True), (zk, ex_k, kh_ref, False), (zv, ex_v, vh_ref, True)):
        for c in range(FOX_WIDTH // LANES):
            pair = src[:, c * LANES:(c + 1) * LANES]
            swapped = pltpu.roll(pair, FOX_HEAD_DIM, axis=1)
            for j, data in enumerate((pair, swapped)):
                hd = 2 * c + j
                blk = jnp.where(low, data, ex[:, hd * LANES:(hd + 1) * LANES])
                dst[hd] = (blk.T[:dst.shape[1]] if transposed else blk).astype(BF16)


def _bias_fold_constants():
    pq = np.zeros((LANES, FOX_HEADS * LANES), np.float32)
    pk = np.zeros((LANES, FOX_HEADS * LANES), np.float32)
    cq = np.zeros((1, FOX_HEADS * LANES), np.float32)
    ck = np.zeros((1, FOX_HEADS * LANES), np.float32)
    cv = np.zeros((1, FOX_HEADS * LANES), np.float32)
    for h in range(FOX_HEADS):
        base = h * LANES + FOX_HEAD_DIM
        for p in range(3):
            pq[p * 8 + h, base + p] = 1.0
            ck[0, base + p] = 1.0
            pk[p * 8 + h, base + 3 + p] = -1.0
            cq[0, base + 3 + p] = 1.0
        cv[0, base] = 1.0
    return (jnp.asarray(pq, BF16), jnp.asarray(pk, BF16), jnp.asarray(cq), jnp.asarray(ck), jnp.asarray(cv))


def _const_spec(shape, single=True):
    nd = len(shape)
    if single:
        return pl.BlockSpec(shape, lambda *_: (0,) * nd, pipeline_mode=pl.Buffered(1))
    return pl.BlockSpec(shape, lambda *_: (0,) * nd)


def _proj(x, gain, w_all, fb3, lb_logits, *, fold):
    rows = x.shape[0]
    tm = min(PROJ_TM, rows)
    n = rows // tm
    row = lambda width: pl.BlockSpec((tm, width), lambda i: (i, 0))
    in_specs = [row(D_MODEL), _const_spec((1, D_MODEL)), _const_spec(w_all.shape), _const_spec((1, LANES)),
                _const_spec(lb_logits.shape)]
    args = [x, gain, w_all, fb3, lb_logits]
    common_out = [
        (jax.ShapeDtypeStruct((rows, FOX_WIDTH), F32), row(FOX_WIDTH)),
        (jax.ShapeDtypeStruct((rows, FOX_WIDTH), F32), row(FOX_WIDTH)),
        (jax.ShapeDtypeStruct((rows, FOX_HEADS), F32), row(FOX_HEADS)),
        (jax.ShapeDtypeStruct((rows, HG_WIDTH), BF16), row(HG_WIDTH)),
        (jax.ShapeDtypeStruct((rows, HG_WIDTH), BF16), row(HG_WIDTH)),
        (jax.ShapeDtypeStruct((rows, HG_WIDTH), F32), row(HG_WIDTH)),
        (jax.ShapeDtypeStruct((rows, HG_WIDTH), BF16), row(HG_WIDTH)),
        (jax.ShapeDtypeStruct((rows, HG_WIDTH), BF16), row(HG_WIDTH)),
        (jax.ShapeDtypeStruct((rows, D_MODEL), BF16), row(D_MODEL)),
        (jax.ShapeDtypeStruct((rows, D_MODEL), BF16), row(D_MODEL)),
    ]
    scratch = []
    if fold:
        tri = jnp.asarray(np.tril(np.ones((tm, tm), np.float32)), BF16)
        seg = np.zeros((FOX_WIDTH, LANES), np.float32)
        seg[np.arange(FOX_WIDTH), np.arange(FOX_WIDTH) // FOX_HEAD_DIM] = 1.0
        consts = _bias_fold_constants() + (jnp.asarray(seg, BF16),)
        in_specs += [_const_spec(tri.shape)] + [_const_spec(c.shape) for c in consts]
        args += [tri, *consts]
        head_major = (jax.ShapeDtypeStruct((FOX_HEADS, rows, FOX_PAD), BF16),
                      pl.BlockSpec((FOX_HEADS, tm, FOX_PAD), lambda i: (0, i, 0)))
        head_major_t = lambda depth: (jax.ShapeDtypeStruct((FOX_HEADS, depth, rows), BF16),
                                      pl.BlockSpec((FOX_HEADS, depth, tm), lambda i: (0, 0, i)))
        stats = (jax.ShapeDtypeStruct((n, 8, LANES), F32), pl.BlockSpec((1, 8, LANES), lambda i: (i, 0, 0)))
        outs = [head_major_t(FOX_PAD), head_major, head_major_t(FOX_V_ROWS), stats] + common_out
        scratch = [pltpu.VMEM((1, LANES), F32)]
    else:
        outs = [(jax.ShapeDtypeStruct((rows, FOX_WIDTH), BF16), row(FOX_WIDTH))] + common_out
    return pl.pallas_call(
        functools.partial(_proj_kernel, fold=fold),
        grid=(n,),
        in_specs=in_specs,
        out_specs=[o[1] for o in outs],
        out_shape=[o[0] for o in outs],
        scratch_shapes=scratch,
        compiler_params=pltpu.CompilerParams(dimension_semantics=("arbitrary",), vmem_limit_bytes=VMEM_LIMIT),
        name="proj_fold" if fold else "proj",
    )(*args)


def _fox_kernel(qi_ref, ki_ref, mode_ref, qt_ref, k_ref, vt_ref, o_ref, m_ref, acc_ref):
    mode = mode_ref[pl.program_id(0)]
    tq = qt_ref.shape[2]
    tk = k_ref.shape[1]

    @pl.when((mode & _MODE_FIRST) != 0)
    def _():
        m_ref[...] = jnp.full_like(m_ref, NEG_INF)
        acc_ref[...] = jnp.zeros_like(acc_ref)

    def logits(h):
        return jnp.dot(k_ref[h], qt_ref[h], preferred_element_type=F32)

    def softmax_update(h, s, masked):
        if masked:
            key = lax.broadcasted_iota(jnp.int32, (tk, tq), 0)
            qry = lax.broadcasted_iota(jnp.int32, (tk, tq), 1)
            s = jnp.where(key <= qry, s, NEG_INF)
        m_prev = m_ref[h]
        m_new = jnp.maximum(m_prev, jnp.max(s, axis=0, keepdims=True))
        alpha = jnp.exp2(m_prev - m_new)
        p = jnp.exp2(s - m_new).astype(BF16)
        acc_ref[h] = alpha * acc_ref[h] + jnp.dot(vt_ref[h], p, preferred_element_type=F32)
        m_ref[h] = m_new

    def all_heads(masked):
        s_next = logits(0)
        for h in range(FOX_HEADS):
            s = s_next
            if h + 1 < FOX_HEADS:
                s_next = logits(h + 1)
            softmax_update(h, s, masked)

    @pl.when((mode & _MODE_PAIR) != 0)
    def _():
        all_heads(False)

    @pl.when((mode & _MODE_DIAG) != 0)
    def _():
        all_heads(True)
        for c in range(FOX_WIDTH // LANES):
            halves = []
            for hd in (2 * c, 2 * c + 1):
                acc = acc_ref[hd]
                halves.append(acc[:FOX_HEAD_DIM, :] / acc[FOX_HEAD_DIM:FOX_HEAD_DIM + 1, :])
            o_ref[:, c * LANES:(c + 1) * LANES] = jnp.concatenate(halves, axis=0).T.astype(o_ref.dtype)


_MODE_FIRST, _MODE_PAIR, _MODE_DIAG = 1, 2, 4


def _fox_schedule(stats, nq):
    st = stats[:, :5, :FOX_HEADS].reshape(nq, -1, 5, FOX_HEADS)
    qn = jnp.sqrt(jnp.max(st[:, :, 0], axis=1)) * 1.01
    kn = jnp.sqrt(jnp.max(st[:, :, 1], axis=1)) * 1.01
    dmin = jnp.min(st[:, :, 2], axis=1)
    c_first = st[:, 0, 3]
    c_last = st[:, -1, 4]
    bound = qn[:, None] * kn[None, :] + (c_first - dmin)[:, None] - c_last[None, :]
    blk = jnp.arange(nq, dtype=jnp.int32)
    drop = jnp.all(bound < -FOX_SKIP_NATS, axis=-1) & (blk[None, :] < blk[:, None])
    prefix = jnp.cumsum(jnp.logical_not(drop).astype(jnp.int32), axis=1) == 0
    kstart = jnp.sum(prefix.astype(jnp.int32), axis=1)
    count = blk - kstart + 1
    ends = jnp.cumsum(count)
    nsteps = nq * (nq + 1) // 2
    step = jnp.arange(nsteps, dtype=jnp.int32)
    valid = step < ends[-1]
    done = ends[None, :] <= step[:, None]
    q_of = jnp.minimum(jnp.sum(done.astype(jnp.int32), axis=1), nq - 1)
    begin = jnp.sum(jnp.where(done, count[None, :], 0), axis=1)
    k_first = jnp.sum(jnp.where(blk[None, :] == q_of[:, None], kstart[None, :], 0), axis=1)
    k_of = k_first + (step - begin)
    mode = jnp.where(k_of == q_of, _MODE_DIAG, _MODE_PAIR) + jnp.where(k_of == k_first, _MODE_FIRST, 0)
    mode = jnp.where(valid, mode, 0)
    q_of = jnp.where(valid, q_of, nq - 1)
    k_of = jnp.where(valid, k_of, nq - 1)
    return q_of.astype(jnp.int32), k_of.astype(jnp.int32), mode.astype(jnp.int32)


def _fox_prompt(qt, kh, vt, stats):
    seq = kh.shape[1]
    assert FOX_TQ == FOX_TK
    nq = seq // FOX_TQ
    qi, ki, mode = _fox_schedule(stats, nq)
    grid_spec = pltpu.PrefetchScalarGridSpec(
        num_scalar_prefetch=3,
        grid=(nq * (nq + 1) // 2,),
        in_specs=[
            pl.BlockSpec((FOX_HEADS, FOX_PAD, FOX_TQ), lambda s, qi, ki, mode: (0, 0, qi[s])),
            pl.BlockSpec((FOX_HEADS, FOX_TK, FOX_PAD), lambda s, qi, ki, mode: (0, ki[s], 0)),
            pl.BlockSpec((FOX_HEADS, FOX_V_ROWS, FOX_TK), lambda s, qi, ki, mode: (0, 0, ki[s])),
        ],
        out_specs=pl.BlockSpec((FOX_TQ, FOX_WIDTH), lambda s, qi, ki, mode: (qi[s], 0)),
        scratch_shapes=[pltpu.VMEM((FOX_HEADS, 1, FOX_TQ), F32),
                        pltpu.VMEM((FOX_HEADS, FOX_V_ROWS, FOX_TQ), F32)],
    )
    return pl.pallas_call(
        _fox_kernel,
        grid_spec=grid_spec,
        out_shape=jax.ShapeDtypeStruct((seq, FOX_WIDTH), BF16),
        compiler_params=pltpu.CompilerParams(dimension_semantics=("arbitrary",), vmem_limit_bytes=VMEM_LIMIT),
        name="fox_prompt",
    )(qi, ki, mode, qt, kh, vt)


def _lane_cumsum(x):
    n = x.shape[1]
    lane = lax.broadcasted_iota(jnp.int32, x.shape, 1)
    shift = 1
    while shift < n:
        x = x + jnp.where(lane >= shift, pltpu.roll(x, shift, axis=1), 0.0)
        shift *= 2
    return x


def _fox_sample_kernel(q_ref, kn_ref, vn_ref, lft_ref, ck_ref, cv_ref, o_ref,
                       cq_ref, cum_ref, m_ref, l_ref, acc_ref, *, n_past_chunks, tk, t_new):
    c = pl.program_id(1)
    rows = FOX_HEADS * t_new
    heads = [slice(h * FOX_HEAD_DIM, (h + 1) * FOX_HEAD_DIM) for h in range(FOX_HEADS)]
    head_rows = [slice(h * t_new, (h + 1) * t_new) for h in range(FOX_HEADS)]

    @pl.when(c == 0)
    def _():
        cum = _lane_cumsum(lft_ref[0])
        for j in range(n_past_chunks):
            cum_ref[j] = cum[:, j * tk:(j + 1) * tk]
        new_cum = cum[:, n_past_chunks * tk:n_past_chunks * tk + LANES]
        cum_ref[n_past_chunks, :, :LANES] = new_cum
        new_cum_t = jnp.concatenate([new_cum] * (LANES // FOX_HEADS), axis=0).T
        for h in range(FOX_HEADS):
            cq_ref[head_rows[h], :] = jnp.broadcast_to(new_cum_t[:t_new, h:h + 1], (t_new, LANES))
        m_ref[...] = jnp.full_like(m_ref, NEG_INF)
        l_ref[...] = jnp.zeros_like(l_ref)
        acc_ref[...] = jnp.zeros_like(acc_ref)

    def update(k_of, v_of, ck_rows, mask):
        q = q_ref[...]
        s = jnp.concatenate([lax.dot_general(q[:, heads[h]], k_of(h), (((1,), (1,)), ((), ())),
                                             preferred_element_type=F32) for h in range(FOX_HEADS)], axis=0)
        s = s + (cq_ref[:, :1] - ck_rows)
        if mask is not None:
            s = jnp.where(mask, s, NEG_INF)
        m_prev = m_ref[...]
        m_new = jnp.maximum(m_prev, jnp.max(s, axis=1, keepdims=True))
        alpha = jnp.exp(m_prev - m_new)
        p = jnp.exp(s - m_new[:, :1])
        l_ref[...] = alpha * l_ref[...] + jnp.sum(p, axis=1, keepdims=True)
        p = p.astype(BF16)
        pv = jnp.concatenate([jnp.dot(p[head_rows[h]], v_of(h), preferred_element_type=F32)
                              for h in range(FOX_HEADS)], axis=0)
        acc_ref[...] = alpha[:, :FOX_HEAD_DIM] * acc_ref[...] + pv
        m_ref[...] = m_new

    def expand_rows(x, width):
        return jnp.concatenate([jnp.broadcast_to(x[h:h + 1, :], (t_new, width)) for h in range(FOX_HEADS)], axis=0)

    @pl.when(c < n_past_chunks)
    def _():
        update(lambda h: ck_ref[0, :, h, :].astype(BF16), lambda h: cv_ref[0, :, h, :].astype(BF16),
               expand_rows(cum_ref[c], tk), None)

    @pl.when(c == n_past_chunks)
    def _():
        ck_rows = expand_rows(cum_ref[n_past_chunks, :, :LANES], LANES)[:, :t_new]
        rowt = lax.broadcasted_iota(jnp.int32, (rows, t_new), 0) % t_new
        coli = lax.broadcasted_iota(jnp.int32, (rows, t_new), 1)
        update(lambda h: kn_ref[:, heads[h]].astype(BF16), lambda h: vn_ref[:, heads[h]].astype(BF16),
               ck_rows, coli <= rowt)
        out = acc_ref[...] / l_ref[:, :FOX_HEAD_DIM]
        o_ref[...] = jnp.concatenate([out[head_rows[h]] for h in range(FOX_HEADS)], axis=1).astype(o_ref.dtype)


def _fox_sample(qs, k_new, v_new, lf_all_t, cache_k, cache_v, *, t_new):
    nb, past = cache_k.shape[0], cache_k.shape[1]
    tk = SAMPLE_TK
    npc = past // tk
    rows = FOX_HEADS * t_new
    last = npc - 1
    kern = functools.partial(_fox_sample_kernel, n_past_chunks=npc, tk=tk, t_new=t_new)
    cache_spec = pl.BlockSpec((1, tk, FOX_HEADS, FOX_HEAD_DIM), lambda b, c: (b, jnp.minimum(c, last), 0, 0))
    return pl.pallas_call(
        kern,
        grid=(nb, npc + 1),
        in_specs=[
            pl.BlockSpec((t_new, FOX_WIDTH), lambda b, c: (b, 0)),
            pl.BlockSpec((t_new, FOX_WIDTH), lambda b, c: (b, 0)),
            pl.BlockSpec((t_new, FOX_WIDTH), lambda b, c: (b, 0)),
            pl.BlockSpec((1, FOX_HEADS, past + LANES), lambda b, c: (b, 0, 0)),
            cache_spec, cache_spec,
        ],
        out_specs=pl.BlockSpec((t_new, FOX_WIDTH), lambda b, c: (b, 0)),
        out_shape=jax.ShapeDtypeStruct((nb * t_new, FOX_WIDTH), BF16),
        scratch_shapes=[
            pltpu.VMEM((rows, LANES), F32),
            pltpu.VMEM((npc + 1, FOX_HEADS, tk), F32),
            pltpu.VMEM((rows, LANES), F32),
            pltpu.VMEM((rows, LANES), F32),
            pltpu.VMEM((rows, FOX_HEAD_DIM), F32),
        ],
        compiler_params=pltpu.CompilerParams(dimension_semantics=("arbitrary", "arbitrary"),
                                             vmem_limit_bytes=VMEM_LIMIT),
        name="fox_sample",
    )(qs, k_new, v_new, lf_all_t, cache_k, cache_v)


def _hgrn_kernel(hq_ref, hk_ref, lfh_ref, hi_ref, shg_ref, s0_ref, norm_ref, tri_ref,
                 o_ref, sout_ref, st_ref, qq_ref, kk_ref, eb_ref, b_ref, oin_ref, od_ref):
    t = pl.program_id(1)
    tc = hq_ref.shape[0]
    blk = min(HG_BLOCK, tc)
    nblk = tc // blk
    heads = [slice(h * HG_DIM, (h + 1) * HG_DIM) for h in range(HG_HEADS)]

    @pl.when(t == 0)
    def _():
        for h in range(HG_HEADS):
            st_ref[h] = s0_ref[0, h].T

    def at_block_row(x, r):
        x3 = x.reshape(nblk, blk, HG_DIM)
        return jnp.broadcast_to(x3[:, r:r + 1, :], x3.shape).reshape(tc, HG_DIM)

    def rel_to_middle(b):
        return b - at_block_row(b, blk // 2 - 1)

    worst = jnp.zeros((), F32)
    for sl in heads:
        b = _sum_by_01_matrix(tri_ref[...], lfh_ref[:, sl])
        eb = jnp.exp(b)
        qq_ref[:, sl] = (hq_ref[:, sl].astype(F32) * eb).astype(BF16)
        kk_ref[:, sl] = (hk_ref[:, sl].astype(F32) * jnp.exp(at_block_row(b, blk - 1) - b)).astype(BF16)
        eb_ref[:, sl] = eb
        b_ref[:, sl] = b
        worst = jnp.maximum(worst, jnp.max(jnp.abs(rel_to_middle(b))))

    def carried_state_and_output():
        for j in range(nblk):
            rows = slice(j * blk, (j + 1) * blk)
            for h, sl in enumerate(heads):
                st = st_ref[h]
                oin_ref[rows, sl] = lax.dot_general(qq_ref[rows, sl], st.astype(BF16), (((1,), (1,)), ((), ())),
                                                    preferred_element_type=F32)
                upd = lax.dot_general(hi_ref[rows, sl], kk_ref[rows, sl], (((0,), (0,)), ((), ())),
                                      preferred_element_type=F32)
                st_ref[h] = st * eb_ref[(j + 1) * blk - 1:(j + 1) * blk, sl] + upd
        for sl in heads:
            o = oin_ref[:, sl] + od_ref[:, sl]
            y = (_rms_scale(o) * norm_ref[...]) * shg_ref[:, sl].astype(F32)
            o_ref[:, sl] = y.astype(o_ref.dtype)

    splittable = worst <= HG_SPLIT_MAX

    @pl.when(splittable)
    def _():
        row = lax.broadcasted_iota(jnp.int32, (tc, tc), 0)
        col = lax.broadcasted_iota(jnp.int32, (tc, tc), 1)
        pair_in_block = (row // blk == col // blk) & (col <= row)
        for sl in heads:
            b_rel = rel_to_middle(b_ref[:, sl])
            qs = (hq_ref[:, sl].astype(F32) * jnp.exp(b_rel)).astype(BF16)
            ks = (hk_ref[:, sl].astype(F32) * jnp.exp(-b_rel)).astype(BF16)
            a = lax.dot_general(qs, ks, (((1,), (1,)), ((), ())), preferred_element_type=F32)
            a = jnp.where(pair_in_block, a, 0.0).astype(BF16)
            od_ref[:, sl] = jnp.dot(a, hi_ref[:, sl], preferred_element_type=F32)
        carried_state_and_output()

    @pl.when(jnp.logical_not(splittable))
    def _():
        row_in_blk = lax.broadcasted_iota(jnp.int32, (tc, HG_DIM), 0) % blk
        for sl in heads:
            q = hq_ref[:, sl].astype(F32)
            k = hk_ref[:, sl].astype(F32)
            v = hi_ref[:, sl].astype(F32)
            b = b_ref[:, sl]

            def lag_step(lag, od):
                k_l = pltpu.roll(k, lag, axis=0)
                b_l = pltpu.roll(b, lag, axis=0)
                v_l = pltpu.roll(v, lag, axis=0)
                w = q * k_l * jnp.exp(jnp.minimum(b - b_l, 0.0))
                w = jnp.where(row_in_blk >= lag, w, 0.0)
                return od + jnp.sum(w, axis=1, keepdims=True) * v_l

            od_ref[:, sl] = lax.fori_loop(1, blk, lag_step, jnp.sum(q * k, axis=1, keepdims=True) * v)
        carried_state_and_output()

    @pl.when(t == pl.num_programs(1) - 1)
    def _():
        for h in range(HG_HEADS):
            sout_ref[0, h] = st_ref[h].T


def _hgrn_cumsum_matrix(n, blk):
    t = np.arange(n)[:, None]
    s = np.arange(n)[None, :]
    return jnp.asarray((((t // blk) == (s // blk)) & (s <= t)).astype(np.float32), BF16)


def _hgrn(hq, hk, lfh, hi, shg, s0, norm, *, nseq):
    rows = hq.shape[0]
    t_len = rows // nseq
    tc = min(HG_TC, t_len)
    nt = t_len // tc
    blk = min(HG_BLOCK, tc)
    tri = _hgrn_cumsum_matrix(tc, blk)
    row = pl.BlockSpec((tc, HG_WIDTH), lambda b, t: (b * nt + t, 0))
    state = pl.BlockSpec((1, HG_HEADS, HG_DIM, HG_DIM), lambda b, t: (b, 0, 0, 0))
    return pl.pallas_call(
        _hgrn_kernel,
        grid=(nseq, nt),
        in_specs=[row, row, row, row, row, state,
                  pl.BlockSpec((1, HG_DIM), lambda b, t: (0, 0)),
                  pl.BlockSpec((tc, tc), lambda b, t: (0, 0))],
        out_specs=[row, state],
        out_shape=[jax.ShapeDtypeStruct((rows, HG_WIDTH), BF16),
                   jax.ShapeDtypeStruct((nseq, HG_HEADS, HG_DIM, HG_DIM), F32)],
        scratch_shapes=[
            pltpu.VMEM((HG_HEADS, HG_DIM, HG_DIM), F32),
            pltpu.VMEM((tc, HG_WIDTH), BF16),
            pltpu.VMEM((tc, HG_WIDTH), BF16),
            pltpu.VMEM((tc, HG_WIDTH), F32),
            pltpu.VMEM((tc, HG_WIDTH), F32),
            pltpu.VMEM((tc, HG_WIDTH), F32),
            pltpu.VMEM((tc, HG_WIDTH), F32),
        ],
        compiler_params=pltpu.CompilerParams(dimension_semantics=("arbitrary", "arbitrary"),
                                             vmem_limit_bytes=VMEM_LIMIT),
        name="hgrn",
    )(hq, hk, lfh, hi, shg, s0, norm, tri)


def _mixffn_kernel(x_ref, of_ref, oh_ref, sga_ref, sgb_ref, hist_ref,
                   wbf_ref, wbh_ref, wout_ref, wup_ref, wdn_ref,
                   npost_ref, npre2_ref, npost2_ref, cw_ref, cb_ref,
                   y_ref, conv_ref, tail_ref, *, seg_len):
    tm = x_ref.shape[0]
    br_f = jnp.dot(of_ref[...], wbf_ref[...], preferred_element_type=F32)
    br_h = jnp.dot(oh_ref[...], wbh_ref[...], preferred_element_type=F32)
    merged = sga_ref[...].astype(F32) * br_f + sgb_ref[...].astype(F32) * br_h
    mix = jnp.dot(merged.astype(BF16), wout_ref[...], preferred_element_type=F32)
    x1 = x_ref[...] + _rms_scale(mix) * npost_ref[...]

    h2 = (_rms_scale(x1) * npre2_ref[...]).astype(BF16)
    up = jnp.dot(h2, wup_ref[...], preferred_element_type=F32)
    a = up[:, :D_FF]
    g = up[:, D_FF:]

    prev1 = pltpu.roll(a, 1, axis=0)
    prev2 = pltpu.roll(a, 2, axis=0)
    rowi = lax.broadcasted_iota(jnp.int32, a.shape, 0)
    if seg_len >= tm:
        @pl.when(pl.program_id(0) == 0)
        def _():
            tail_ref[...] = hist_ref[0]
        t0 = tail_ref[0:1, :]
        t1 = tail_ref[1:2, :]
        prev1 = jnp.where(rowi == 0, t1, prev1)
        prev2 = jnp.where(rowi == 0, t0, jnp.where(rowi == 1, t1, prev2))
        tail_ref[...] = a[tm - 2:, :]
        conv_ref[0] = a[tm - 2:, :]
    else:
        for s in range(tm // seg_len):
            h0 = hist_ref[s, 0:1, :]
            h1 = hist_ref[s, 1:2, :]
            prev1 = jnp.where(rowi == s * seg_len, h1, prev1)
            prev2 = jnp.where(rowi == s * seg_len, h0, jnp.where(rowi == s * seg_len + 1, h1, prev2))
            conv_ref[s] = a[(s + 1) * seg_len - 2:(s + 1) * seg_len, :]
    c = cb_ref[...] + cw_ref[0:1, :] * prev2 + cw_ref[1:2, :] * prev1 + cw_ref[2:3, :] * a
    act = (jax.nn.gelu(c, approximate=True) * g).astype(BF16)
    ff = jnp.dot(act, wdn_ref[...], preferred_element_type=F32)
    y_ref[...] = x1 + _rms_scale(ff) * npost2_ref[...]


def _mixffn(x, o_fox, o_hg, sga, sgb, hist, w, *, seg_len):
    rows = x.shape[0]
    tm = min(FFN_TM, rows)
    n = rows // tm
    nseg = hist.shape[0]
    row = lambda width: pl.BlockSpec((tm, width), lambda i: (i, 0))
    weights = [w["bf"], w["bh"], w["out"], w["up"], w["down"]]
    smalls = [w["npost"], w["npre2"], w["npost2"], w["conv_w"], w["conv_b"]]
    hist_spec = pl.BlockSpec(hist.shape, lambda i: (0, 0, 0))
    scratch = [pltpu.VMEM((2, D_FF), F32)]
    return pl.pallas_call(
        functools.partial(_mixffn_kernel, seg_len=seg_len),
        grid=(n,),
        in_specs=[row(D_MODEL), row(FOX_WIDTH), row(HG_WIDTH), row(D_MODEL), row(D_MODEL), hist_spec]
                 + [_const_spec(a.shape) for a in weights] + [_const_spec(a.shape) for a in smalls],
        out_specs=[row(D_MODEL), pl.BlockSpec((nseg, 2, D_FF), lambda i: (0, 0, 0))],
        out_shape=[jax.ShapeDtypeStruct((rows, D_MODEL), F32), jax.ShapeDtypeStruct((nseg, 2, D_FF), F32)],
        scratch_shapes=scratch,
        compiler_params=pltpu.CompilerParams(dimension_semantics=("arbitrary",), vmem_limit_bytes=VMEM_LIMIT),
        name="mixffn",
    )(x, o_fox, o_hg, sga, sgb, hist, *weights, *smalls)


def _prep_w_in(w_in, fox_f_bias):
    offs = np.cumsum([0] + IN_SIZES)
    seg = [w_in[:, int(offs[i]):int(offs[i + 1])] for i in range(len(IN_SIZES))]
    pad = jnp.zeros((D_MODEL, LANES - 3 * FOX_HEADS), w_in.dtype)
    f3 = jnp.concatenate([seg[3], seg[3], seg[3], pad], axis=1)
    w_all = jnp.concatenate(seg[:3] + [f3] + seg[4:], axis=1).astype(BF16)
    fb = fox_f_bias.astype(F32)
    fb3 = jnp.concatenate([fb, fb, fb, jnp.zeros((LANES - 3 * FOX_HEADS,), F32)]).reshape(1, LANES)
    return w_all, fb3


def kernel(x_prompt, x_sample, cache_fox_k, cache_fox_v, cache_fox_logf, state_hgrn, state_ffn_conv, norm_mix_pre, norm_mix_post, w_in, fox_f_bias, hgrn_lb_logits, hgrn_norm, w_branch_fox, w_branch_hgrn, w_out, norm_ffn_pre, norm_ffn_post, w_up, ffn_conv_w, ffn_conv_b, w_down):
    depth = w_in.shape[0]
    assert depth == 1 and hgrn_lb_logits.shape[0] == 2
    bp, seq, _ = x_prompt.shape
    assert bp == 1
    nb, t_new, _ = x_sample.shape
    past = cache_fox_k.shape[2]

    w_all, fb3 = _prep_w_in(w_in[0], fox_f_bias[0])
    g_pre = norm_mix_pre[0].reshape(1, D_MODEL)
    lbl = hgrn_lb_logits.astype(F32)
    hnorm = hgrn_norm[0].astype(F32).reshape(1, HG_DIM)
    w = {
        "bf": w_branch_fox[0].astype(BF16), "bh": w_branch_hgrn[0].astype(BF16), "out": w_out[0].astype(BF16),
        "up": w_up[0].astype(BF16), "down": w_down[0].astype(BF16),
        "npost": norm_mix_post[0].reshape(1, D_MODEL), "npre2": norm_ffn_pre[0].reshape(1, D_MODEL),
        "npost2": norm_ffn_post[0].reshape(1, D_MODEL),
        "conv_w": ffn_conv_w[0], "conv_b": ffn_conv_b[0].reshape(1, D_FF),
    }

    xp = x_prompt.reshape(seq, D_MODEL)
    (qt, kh, vt, stats, pk, pv, plf, hq, hk, lfh, hi, shg, sga, sgb) = _proj(xp, g_pre, w_all, fb3, lbl, fold=True)
    o_fox = _fox_prompt(qt, kh, vt, stats)
    s0 = jnp.zeros((1, HG_HEADS, HG_DIM, HG_DIM), F32)
    o_hg, p_state = _hgrn(hq, hk, lfh, hi, shg, s0, hnorm, nseq=1)
    hist0 = jnp.zeros((1, 2, D_FF), F32)
    yp, pconv = _mixffn(xp, o_fox, o_hg, sga, sgb, hist0, w, seg_len=seq)

    xs = x_sample.reshape(nb * t_new, D_MODEL)
    (qs, sk, sv, slf, hq, hk, lfh, hi, shg, sga, sgb) = _proj(xs, g_pre, w_all, fb3, lbl, fold=False)
    lf_all_t = jnp.concatenate([
        jnp.swapaxes(cache_fox_logf[0].astype(F32), 1, 2),
        jnp.swapaxes(slf.reshape(nb, t_new, FOX_HEADS), 1, 2),
        jnp.zeros((nb, FOX_HEADS, LANES - t_new), F32)], axis=2)
    o_fox_s = _fox_sample(qs, sk, sv, lf_all_t, cache_fox_k[0], cache_fox_v[0], t_new=t_new)
    o_hg_s, s_state = _hgrn(hq, hk, lfh, hi, shg, state_hgrn[0].astype(F32), hnorm, nseq=nb)
    ys, sconv = _mixffn(xs, o_fox_s, o_hg_s, sga, sgb, state_ffn_conv[0], w, seg_len=t_new)

    return (
        yp.reshape(bp, seq, D_MODEL),
        ys.reshape(nb, t_new, D_MODEL),
        pk.reshape(1, bp, seq, FOX_HEADS, FOX_HEAD_DIM),
        pv.reshape(1, bp, seq, FOX_HEADS, FOX_HEAD_DIM),
        plf.reshape(1, bp, seq, FOX_HEADS),
        p_state.reshape(1, bp, HG_HEADS, HG_DIM, HG_DIM),
        pconv.reshape(1, bp, 2, D_FF),
        sk.reshape(1, nb, t_new, FOX_HEADS, FOX_HEAD_DIM),
        sv.reshape(1, nb, t_new, FOX_HEADS, FOX_HEAD_DIM),
        slf.reshape(1, nb, t_new, FOX_HEADS),
        s_state.reshape(1, nb, HG_HEADS, HG_DIM, HG_DIM),
        sconv.reshape(1, nb, 2, D_FF),
    )
```

```python
import functools

import numpy as np
import jax
import jax.numpy as jnp
from jax import lax
from jax.experimental import pallas as pl
from jax.experimental.pallas import tpu as pltpu

F32 = jnp.float32
BF16 = jnp.bfloat16

D_MODEL = 1024
FOX_HEADS = 8
FOX_HEAD_DIM = 64
FOX_WIDTH = FOX_HEADS * FOX_HEAD_DIM
HG_HEADS = 4
HG_DIM = 128
HG_WIDTH = HG_HEADS * HG_DIM
D_FF = 2816
RMS_EPS = 1e-6
NEG_INF = -1e30
LOG2E = 1.4426950408889634
FOX_SKIP_NATS = 110.0
IN_SIZES = [FOX_WIDTH, FOX_WIDTH, FOX_WIDTH, FOX_HEADS, HG_WIDTH, HG_WIDTH, HG_WIDTH, HG_WIDTH, D_MODEL, D_MODEL]

LANES = 128
FOX_PAD = 2 * FOX_HEAD_DIM
FOX_V_ROWS = FOX_HEAD_DIM + 16
HG_BLOCK = 64
HG_SPLIT_MAX = 60.0
VMEM_LIMIT = 56 * 1024 * 1024

PROJ_TM = 256
FOX_TQ = 512
FOX_TK = 512
HG_TC = 256
FFN_TM = 256
SAMPLE_TK = 1024

_C_Q, _C_K, _C_V, _C_F = 0, 512, 1024, 1536
_C_HQ, _C_HF, _C_HI, _C_HG = 1664, 2176, 2688, 3200
_C_GA, _C_GB, _C_END = 3712, 4736, 5760


def _split3(x):
    hi = x.astype(BF16)
    r = x - hi.astype(F32)
    mid = r.astype(BF16)
    lo = (r - mid.astype(F32)).astype(BF16)
    return hi, mid, lo


def _sum_by_01_matrix(mat01, x):
    cat = jnp.concatenate(_split3(x), axis=1)
    y = jnp.dot(mat01, cat, preferred_element_type=F32)
    return y[:, :LANES] + y[:, LANES:2 * LANES] + y[:, 2 * LANES:]


def _rms_scale(x):
    return x * lax.rsqrt(jnp.mean(x * x, axis=-1, keepdims=True) + RMS_EPS)


def _log_sigmoid(x):
    return jnp.minimum(x, 0.0) - jnp.log1p(jnp.exp(-jnp.abs(x)))


def _sigmoid(x):
    return 1.0 / (1.0 + jnp.exp(-x))


def _proj_kernel(*refs, fold):
    if fold:
        (x_ref, g_ref, w_ref, fb_ref, lbl_ref, tri_ref, pq_ref, pk_ref, cq_ref, ck_ref, cv_ref, seg_ref,
         qh_ref, kh_ref, vh_ref, stat_ref, kout_ref, vout_ref, lf_ref, hq_ref, hk_ref, lfh_ref, hi_ref, shg_ref,
         sga_ref, sgb_ref, carry_ref) = refs
    else:
        (x_ref, g_ref, w_ref, fb_ref, lbl_ref,
         qs_ref, kout_ref, vout_ref, lf_ref, hq_ref, hk_ref, lfh_ref, hi_ref, shg_ref,
         sga_ref, sgb_ref) = refs

    h = (_rms_scale(x_ref[...]) * g_ref[...]).astype(BF16)
    z = jnp.dot(h, w_ref[...], preferred_element_type=F32)

    zq = z[:, _C_Q:_C_K] * (FOX_HEAD_DIM ** -0.5)
    zk = z[:, _C_K:_C_V]
    zv = z[:, _C_V:_C_F]
    kout_ref[...] = zk
    vout_ref[...] = zv
    logf = _log_sigmoid(z[:, _C_F:_C_HQ] + fb_ref[...])
    lf_ref[...] = logf[:, :FOX_HEADS]

    l0 = lbl_ref[0:1, :]
    l1 = lbl_ref[1:2, :]
    lmax = jnp.maximum(l0, l1)
    e0 = jnp.exp(l0 - lmax)
    lb = e0 / (e0 + jnp.exp(l1 - lmax))
    f = lb + (1.0 - lb) * _sigmoid(z[:, _C_HF:_C_HI])
    hq_ref[...] = z[:, _C_HQ:_C_HF].astype(BF16)
    hk_ref[...] = (1.0 - f).astype(BF16)
    lfh_ref[...] = jnp.log(f)
    hi_ref[...] = z[:, _C_HI:_C_HG].astype(BF16)
    shg_ref[...] = _sigmoid(z[:, _C_HG:_C_GA]).astype(BF16)
    sga_ref[...] = _sigmoid(z[:, _C_GA:_C_GB]).astype(BF16)
    sgb_ref[...] = _sigmoid(z[:, _C_GB:_C_END]).astype(BF16)

    if not fold:
        qs_ref[...] = zq.astype(BF16)
        return

    @pl.when(pl.program_id(0) == 0)
    def _():
        carry_ref[...] = jnp.zeros_like(carry_ref)

    cum = carry_ref[...] + _sum_by_01_matrix(tri_ref[...], logf)
    carry_ref[...] = cum[-1:, :]

    seg = seg_ref[...]
    qn2 = jnp.dot((zq * zq).astype(BF16), seg, preferred_element_type=F32)
    kn2 = jnp.dot((zk * zk).astype(BF16), seg, preferred_element_type=F32)
    dg = jnp.dot((zq * zk).astype(BF16), seg, preferred_element_type=F32)
    stat_ref[0, 0:1, :] = jnp.max(qn2, axis=0, keepdims=True)
    stat_ref[0, 1:2, :] = jnp.max(kn2, axis=0, keepdims=True)
    stat_ref[0, 2:3, :] = jnp.min(dg, axis=0, keepdims=True)
    stat_ref[0, 3:4, :] = cum[0:1, :]
    stat_ref[0, 4:5, :] = cum[-1:, :]
    stat_ref[0, 5:8, :] = jnp.zeros((3, LANES), F32)

    zq = zq * LOG2E
    c_hi, c_mid, c_lo = _split3(cum * LOG2E)
    lane = lax.broadcasted_iota(jnp.int32, cum.shape, 1)
    pieces = jnp.where(lane < 8, c_hi, jnp.where(lane < 16, c_mid, c_lo))
    pieces = jnp.where(lane < 24, pieces, jnp.zeros_like(pieces))
    ex_q = jnp.dot(pieces, pq_ref[...], preferred_element_type=F32) + cq_ref[...]
    ex_k = jnp.dot(pieces, pk_ref[...], preferred_element_type=F32) + ck_ref[...]
    ex_v = cv_ref[...]

    low = lax.broadcasted_iota(jnp.int32, (zq.shape[0], LANES), 1) < FOX_HEAD_DIM
    for src, ex, dst, transposed in ((zq, ex_q, qh_ref, True), (zk, ex_k, kh_ref, False), (zv, ex_v, vh_ref, True)):
        for c in range(FOX_WIDTH // LANES):
            pair = src[:, c * LANES:(c + 1) * LANES]
            swapped = pltpu.roll(pair, FOX_HEAD_DIM, axis=1)
            for j, data in enumerate((pair, swapped)):
                hd = 2 * c + j
                blk = jnp.where(low, data, ex[:, hd * LANES:(hd + 1) * LANES])
                dst[hd] = (blk.T[:dst.shape[1]] if transposed else blk).astype(BF16)


def _bias_fold_constants():
    pq = np.zeros((LANES, FOX_HEADS * LANES), np.float32)
    pk = np.zeros((LANES, FOX_HEADS * LANES), np.float32)
    cq = np.zeros((1, FOX_HEADS * LANES), np.float32)
    ck = np.zeros((1, FOX_HEADS * LANES), np.float32)
    cv = np.zeros((1, FOX_HEADS * LANES), np.float32)
    for h in range(FOX_HEADS):
        base = h * LANES + FOX_HEAD_DIM
        for p in range(3):
            pq[p * 8 + h, base + p] = 1.0
            ck[0, base + p] = 1.0
            pk[p * 8 + h, base + 3 + p] = -1.0
            cq[0, base + 3 + p] = 1.0
        cv[0, base] = 1.0
    return (jnp.asarray(pq, BF16), jnp.asarray(pk, BF16), jnp.asarray(cq), jnp.asarray(ck), jnp.asarray(cv))


def _const_spec(shape, single=True):
    nd = len(shape)
    if single:
        return pl.BlockSpec(shape, lambda *_: (0,) * nd, pipeline_mode=pl.Buffered(1))
    return pl.BlockSpec(shape, lambda *_: (0,) * nd)


def _proj(x, gain, w_all, fb3, lb_logits, *, fold):
    rows = x.shape[0]
    tm = min(PROJ_TM, rows)
    n = rows // tm
    row = lambda width: pl.BlockSpec((tm, width), lambda i: (i, 0))
    in_specs = [row(D_MODEL), _const_spec((1, D_MODEL)), _const_spec(w_all.shape), _const_spec((1, LANES)),
                _const_spec(lb_logits.shape)]
    args = [x, gain, w_all, fb3, lb_logits]
    common_out = [
        (jax.ShapeDtypeStruct((rows, FOX_WIDTH), F32), row(FOX_WIDTH)),
        (jax.ShapeDtypeStruct((rows, FOX_WIDTH), F32), row(FOX_WIDTH)),
        (jax.ShapeDtypeStruct((rows, FOX_HEADS), F32), row(FOX_HEADS)),
        (jax.ShapeDtypeStruct((rows, HG_WIDTH), BF16), row(HG_WIDTH)),
        (jax.ShapeDtypeStruct((rows, HG_WIDTH), BF16), row(HG_WIDTH)),
        (jax.ShapeDtypeStruct((rows, HG_WIDTH), F32), row(HG_WIDTH)),
        (jax.ShapeDtypeStruct((rows, HG_WIDTH), BF16), row(HG_WIDTH)),
        (jax.ShapeDtypeStruct((rows, HG_WIDTH), BF16), row(HG_WIDTH)),
        (jax.ShapeDtypeStruct((rows, D_MODEL), BF16), row(D_MODEL)),
        (jax.ShapeDtypeStruct((rows, D_MODEL), BF16), row(D_MODEL)),
    ]
    scratch = []
    if fold:
        tri = jnp.asarray(np.tril(np.ones((tm, tm), np.float32)), BF16)
        seg = np.zeros((FOX_WIDTH, LANES), np.float32)
        seg[np.arange(FOX_WIDTH), np.arange(FOX_WIDTH) // FOX_HEAD_DIM] = 1.0
        consts = _bias_fold_constants() + (jnp.asarray(seg, BF16),)
        in_specs += [_const_spec(tri.shape)] + [_const_spec(c.shape) for c in consts]
        args += [tri, *consts]
        head_major = (jax.ShapeDtypeStruct((FOX_HEADS, rows, FOX_PAD), BF16),
                      pl.BlockSpec((FOX_HEADS, tm, FOX_PAD), lambda i: (0, i, 0)))
        head_major_t = lambda depth: (jax.ShapeDtypeStruct((FOX_HEADS, depth, rows), BF16),
                                      pl.BlockSpec((FOX_HEADS, depth, tm), lambda i: (0, 0, i)))
        stats = (jax.ShapeDtypeStruct((n, 8, LANES), F32), pl.BlockSpec((1, 8, LANES), lambda i: (i, 0, 0)))
        outs = [head_major_t(FOX_PAD), head_major, head_major_t(FOX_V_ROWS), stats] + common_out
        scratch = [pltpu.VMEM((1, LANES), F32)]
    else:
        outs = [(jax.ShapeDtypeStruct((rows, FOX_WIDTH), BF16), row(FOX_WIDTH))] + common_out
    return pl.pallas_call(
        functools.partial(_proj_kernel, fold=fold),
        grid=(n,),
        in_specs=in_specs,
        out_specs=[o[1] for o in outs],
        out_shape=[o[0] for o in outs],
        scratch_shapes=scratch,
        compiler_params=pltpu.CompilerParams(dimension_semantics=("arbitrary",), vmem_limit_bytes=VMEM_LIMIT),
        name="proj_fold" if fold else "proj",
    )(*args)


def _fox_kernel(qi_ref, ki_ref, mode_ref, qt_ref, k_ref, vt_ref, o_ref, m_ref, acc_ref):
    mode = mode_ref[pl.program_id(0)]
    tq = qt_ref.shape[2]
    tk = k_ref.shape[1]

    @pl.when((mode & _MODE_FIRST) != 0)
    def _():
        m_ref[...] = jnp.full_like(m_ref, NEG_INF)
        acc_ref[...] = jnp.zeros_like(acc_ref)

    def logits(h):
        return jnp.dot(k_ref[h], qt_ref[h], preferred_element_type=F32)

    def softmax_update(h, s, masked):
        if masked:
            key = lax.broadcasted_iota(jnp.int32, (tk, tq), 0)
            qry = lax.broadcasted_iota(jnp.int32, (tk, tq), 1)
            s = jnp.where(key <= qry, s, NEG_INF)
        m_prev = m_ref[h]
        m_new = jnp.maximum(m_prev, jnp.max(s, axis=0, keepdims=True))
        alpha = jnp.exp2(m_prev - m_new)
        p = jnp.exp2(s - m_new).astype(BF16)
        acc_ref[h] = alpha * acc_ref[h] + jnp.dot(vt_ref[h], p, preferred_element_type=F32)
        m_ref[h] = m_new

    def all_heads(masked):
        s_next = logits(0)
        for h in range(FOX_HEADS):
            s = s_next
            if h + 1 < FOX_HEADS:
                s_next = logits(h + 1)
            softmax_update(h, s, masked)

    @pl.when((mode & _MODE_PAIR) != 0)
    def _():
        all_heads(False)

    @pl.when((mode & _MODE_DIAG) != 0)
    def _():
        all_heads(True)
        for c in range(FOX_WIDTH // LANES):
            halves = []
            for hd in (2 * c, 2 * c + 1):
                acc = acc_ref[hd]
                halves.append(acc[:FOX_HEAD_DIM, :] / acc[FOX_HEAD_DIM:FOX_HEAD_DIM + 1, :])
            o_ref[:, c * LANES:(c + 1) * LANES] = jnp.concatenate(halves, axis=0).T.astype(o_ref.dtype)


_MODE_FIRST, _MODE_PAIR, _MODE_DIAG = 1, 2, 4


def _fox_schedule(stats, nq):
    st = stats[:, :5, :FOX_HEADS].reshape(nq, -1, 5, FOX_HEADS)
    qn = jnp.sqrt(jnp.max(st[:, :, 0], axis=1)) * 1.01
    kn = jnp.sqrt(jnp.max(st[:, :, 1], axis=1)) * 1.01
    dmin = jnp.min(st[:, :, 2], axis=1)
    c_first = st[:, 0, 3]
    c_last = st[:, -1, 4]
    bound = qn[:, None] * kn[None, :] + (c_first - dmin)[:, None] - c_last[None, :]
    blk = jnp.arange(nq, dtype=jnp.int32)
    drop = jnp.all(bound < -FOX_SKIP_NATS, axis=-1) & (blk[None, :] < blk[:, None])
    prefix = jnp.cumsum(jnp.logical_not(drop).astype(jnp.int32), axis=1) == 0
    kstart = jnp.sum(prefix.astype(jnp.int32), axis=1)
    count = blk - kstart + 1
    ends = jnp.cumsum(count)
    nsteps = nq * (nq + 1) // 2
    step = jnp.arange(nsteps, dtype=jnp.int32)
    valid = step < ends[-1]
    done = ends[None, :] <= step[:, None]
    q_of = jnp.minimum(jnp.sum(done.astype(jnp.int32), axis=1), nq - 1)
    begin = jnp.sum(jnp.where(done, count[None, :], 0), axis=1)
    k_first = jnp.sum(jnp.where(blk[None, :] == q_of[:, None], kstart[None, :], 0), axis=1)
    k_of = k_first + (step - begin)
    mode = jnp.where(k_of == q_of, _MODE_DIAG, _MODE_PAIR) + jnp.where(k_of == k_first, _MODE_FIRST, 0)
    mode = jnp.where(valid, mode, 0)
    q_of = jnp.where(valid, q_of, nq - 1)
    k_of = jnp.where(valid, k_of, nq - 1)
    return q_of.astype(jnp.int32), k_of.astype(jnp.int32), mode.astype(jnp.int32)


def _fox_prompt(qt, kh, vt, stats):
    seq = kh.shape[1]
    assert FOX_TQ == FOX_TK
    nq = seq // FOX_TQ
    qi, ki, mode = _fox_schedule(stats, nq)
    grid_spec = pltpu.PrefetchScalarGridSpec(
        num_scalar_prefetch=3,
        grid=(nq * (nq + 1) // 2,),
        in_specs=[
            pl.BlockSpec((FOX_HEADS, FOX_PAD, FOX_TQ), lambda s, qi, ki, mode: (0, 0, qi[s])),
            pl.BlockSpec((FOX_HEADS, FOX_TK, FOX_PAD), lambda s, qi, ki, mode: (0, ki[s], 0)),
            pl.BlockSpec((FOX_HEADS, FOX_V_ROWS, FOX_TK), lambda s, qi, ki, mode: (0, 0, ki[s])),
        ],
        out_specs=pl.BlockSpec((FOX_TQ, FOX_WIDTH), lambda s, qi, ki, mode: (qi[s], 0)),
        scratch_shapes=[pltpu.VMEM((FOX_HEADS, 1, FOX_TQ), F32),
                        pltpu.VMEM((FOX_HEADS, FOX_V_ROWS, FOX_TQ), F32)],
    )
    return pl.pallas_call(
        _fox_kernel,
        grid_spec=grid_spec,
        out_shape=jax.ShapeDtypeStruct((seq, FOX_WIDTH), BF16),
        compiler_params=pltpu.CompilerParams(dimension_semantics=("arbitrary",), vmem_limit_bytes=VMEM_LIMIT),
        name="fox_prompt",
    )(qi, ki, mode, qt, kh, vt)


def _lane_cumsum(x):
    n = x.shape[1]
    lane = lax.broadcasted_iota(jnp.int32, x.shape, 1)
    shift = 1
    while shift < n:
        x = x + jnp.where(lane >= shift, pltpu.roll(x, shift, axis=1), 0.0)
        shift *= 2
    return x


def _fox_sample_kernel(q_ref, kn_ref, vn_ref, lft_ref, ck_ref, cv_ref, o_ref,
                       cq_ref, cum_ref, m_ref, l_ref, acc_ref, *, n_past_chunks, tk, t_new):
    c = pl.program_id(1)
    rows = FOX_HEADS * t_new
    heads = [slice(h * FOX_HEAD_DIM, (h + 1) * FOX_HEAD_DIM) for h in range(FOX_HEADS)]
    head_rows = [slice(h * t_new, (h + 1) * t_new) for h in range(FOX_HEADS)]

    @pl.when(c == 0)
    def _():
        cum = _lane_cumsum(lft_ref[0])
        for j in range(n_past_chunks):
            cum_ref[j] = cum[:, j * tk:(j + 1) * tk]
        new_cum = cum[:, n_past_chunks * tk:n_past_chunks * tk + LANES]
        cum_ref[n_past_chunks, :, :LANES] = new_cum
        new_cum_t = jnp.concatenate([new_cum] * (LANES // FOX_HEADS), axis=0).T
        for h in range(FOX_HEADS):
            cq_ref[head_rows[h], :] = jnp.broadcast_to(new_cum_t[:t_new, h:h + 1], (t_new, LANES))
        m_ref[...] = jnp.full_like(m_ref, NEG_INF)
        l_ref[...] = jnp.zeros_like(l_ref)
        acc_ref[...] = jnp.zeros_like(acc_ref)

    nt = (((1,), (1,)), ((), ()))

    def update(qk, pv_of, ck_rows, mask):
        q = q_ref[...]
        s = jnp.concatenate([qk(q[:, heads[h]], h) for h in range(FOX_HEADS)], axis=0)
        s = s + (cq_ref[:, :1] - ck_rows)
        if mask is not None:
            s = jnp.where(mask, s, NEG_INF)
        m_prev = m_ref[...]
        m_new = jnp.maximum(m_prev, jnp.max(s, axis=1, keepdims=True))
        alpha = jnp.exp(m_prev - m_new)
        p = jnp.exp(s - m_new[:, :1])
        l_ref[...] = alpha * l_ref[...] + jnp.sum(p, axis=1, keepdims=True)
        p = p.astype(BF16)
        pv = jnp.concatenate([pv_of(p[head_rows[h]], h) for h in range(FOX_HEADS)], axis=0)
        acc_ref[...] = alpha[:, :FOX_HEAD_DIM] * acc_ref[...] + pv
        m_ref[...] = m_new

    def expand_rows(x, width):
        return jnp.concatenate([jnp.broadcast_to(x[h:h + 1, :], (t_new, width)) for h in range(FOX_HEADS)], axis=0)

    @pl.when(c < n_past_chunks)
    def _():
        update(lambda qh, h: jnp.dot(qh, ck_ref[0, h].astype(BF16), preferred_element_type=F32),
               lambda ph, h: lax.dot_general(ph, cv_ref[0, h].astype(BF16), nt, preferred_element_type=F32),
               expand_rows(cum_ref[c], tk), None)

    @pl.when(c == n_past_chunks)
    def _():
        ck_rows = expand_rows(cum_ref[n_past_chunks, :, :LANES], LANES)[:, :t_new]
        rowt = lax.broadcasted_iota(jnp.int32, (rows, t_new), 0) % t_new
        coli = lax.broadcasted_iota(jnp.int32, (rows, t_new), 1)
        update(lambda qh, h: lax.dot_general(qh, kn_ref[:, heads[h]].astype(BF16), nt, preferred_element_type=F32),
               lambda ph, h: jnp.dot(ph, vn_ref[:, heads[h]].astype(BF16), preferred_element_type=F32),
               ck_rows, coli <= rowt)
        out = acc_ref[...] / l_ref[:, :FOX_HEAD_DIM]
        o_ref[...] = jnp.concatenate([out[head_rows[h]] for h in range(FOX_HEADS)], axis=1).astype(o_ref.dtype)


def _fox_sample(qs, k_new, v_new, lf_all_t, cache_k, cache_v, *, t_new):
    nb, past = cache_k.shape[0], cache_k.shape[3]
    tk = SAMPLE_TK
    npc = past // tk
    rows = FOX_HEADS * t_new
    last = npc - 1
    kern = functools.partial(_fox_sample_kernel, n_past_chunks=npc, tk=tk, t_new=t_new)
    cache_spec = pl.BlockSpec((1, FOX_HEADS, FOX_HEAD_DIM, tk), lambda b, c: (b, 0, 0, jnp.minimum(c, last)))
    return pl.pallas_call(
        kern,
        grid=(nb, npc + 1),
        in_specs=[
            pl.BlockSpec((t_new, FOX_WIDTH), lambda b, c: (b, 0)),
            pl.BlockSpec((t_new, FOX_WIDTH), lambda b, c: (b, 0)),
            pl.BlockSpec((t_new, FOX_WIDTH), lambda b, c: (b, 0)),
            pl.BlockSpec((1, FOX_HEADS, past + LANES), lambda b, c: (b, 0, 0)),
            cache_spec, cache_spec,
        ],
        out_specs=pl.BlockSpec((t_new, FOX_WIDTH), lambda b, c: (b, 0)),
        out_shape=jax.ShapeDtypeStruct((nb * t_new, FOX_WIDTH), BF16),
        scratch_shapes=[
            pltpu.VMEM((rows, LANES), F32),
            pltpu.VMEM((npc + 1, FOX_HEADS, tk), F32),
            pltpu.VMEM((rows, LANES), F32),
            pltpu.VMEM((rows, LANES), F32),
            pltpu.VMEM((rows, FOX_HEAD_DIM), F32),
        ],
        compiler_params=pltpu.CompilerParams(dimension_semantics=("arbitrary", "arbitrary"),
                                             vmem_limit_bytes=VMEM_LIMIT),
        name="fox_sample",
    )(qs, k_new, v_new, lf_all_t, cache_k, cache_v)


def _hgrn_kernel(hq_ref, hk_ref, lfh_ref, hi_ref, shg_ref, s0_ref, norm_ref, tri_ref,
                 o_ref, sout_ref, st_ref, qq_ref, kk_ref, eb_ref, b_ref, oin_ref, od_ref):
    t = pl.program_id(1)
    tc = hq_ref.shape[0]
    blk = min(HG_BLOCK, tc)
    nblk = tc // blk
    heads = [slice(h * HG_DIM, (h + 1) * HG_DIM) for h in range(HG_HEADS)]

    @pl.when(t == 0)
    def _():
        for h in range(HG_HEADS):
            st_ref[h] = s0_ref[0, h].T

    def at_block_row(x, r):
        x3 = x.reshape(nblk, blk, HG_DIM)
        return jnp.broadcast_to(x3[:, r:r + 1, :], x3.shape).reshape(tc, HG_DIM)

    def rel_to_middle(b):
        return b - at_block_row(b, blk // 2 - 1)

    worst = jnp.zeros((), F32)
    for sl in heads:
        b = _sum_by_01_matrix(tri_ref[...], lfh_ref[:, sl])
        eb = jnp.exp(b)
        qq_ref[:, sl] = (hq_ref[:, sl].astype(F32) * eb).astype(BF16)
        kk_ref[:, sl] = (hk_ref[:, sl].astype(F32) * jnp.exp(at_block_row(b, blk - 1) - b)).astype(BF16)
        eb_ref[:, sl] = eb
        b_ref[:, sl] = b
        worst = jnp.maximum(worst, jnp.max(jnp.abs(rel_to_middle(b))))

    def carried_state_and_output():
        for j in range(nblk):
            rows = slice(j * blk, (j + 1) * blk)
            for h, sl in enumerate(heads):
                st = st_ref[h]
                oin_ref[rows, sl] = lax.dot_general(qq_ref[rows, sl], st.astype(BF16), (((1,), (1,)), ((), ())),
                                                    preferred_element_type=F32)
                upd = lax.dot_general(hi_ref[rows, sl], kk_ref[rows, sl], (((0,), (0,)), ((), ())),
                                      preferred_element_type=F32)
                st_ref[h] = st * eb_ref[(j + 1) * blk - 1:(j + 1) * blk, sl] + upd
        for sl in heads:
            o = oin_ref[:, sl] + od_ref[:, sl]
            y = (_rms_scale(o) * norm_ref[...]) * shg_ref[:, sl].astype(F32)
            o_ref[:, sl] = y.astype(o_ref.dtype)

    splittable = worst <= HG_SPLIT_MAX

    @pl.when(splittable)
    def _():
        row = lax.broadcasted_iota(jnp.int32, (tc, tc), 0)
        col = lax.broadcasted_iota(jnp.int32, (tc, tc), 1)
        pair_in_block = (row // blk == col // blk) & (col <= row)
        for sl in heads:
            b_rel = rel_to_middle(b_ref[:, sl])
            qs = (hq_ref[:, sl].astype(F32) * jnp.exp(b_rel)).astype(BF16)
            ks = (hk_ref[:, sl].astype(F32) * jnp.exp(-b_rel)).astype(BF16)
            a = lax.dot_general(qs, ks, (((1,), (1,)), ((), ())), preferred_element_type=F32)
            a = jnp.where(pair_in_block, a, 0.0).astype(BF16)
            od_ref[:, sl] = jnp.dot(a, hi_ref[:, sl], preferred_element_type=F32)
        carried_state_and_output()

    @pl.when(jnp.logical_not(splittable))
    def _():
        row_in_blk = lax.broadcasted_iota(jnp.int32, (tc, HG_DIM), 0) % blk
        for sl in heads:
            q = hq_ref[:, sl].astype(F32)
            k = hk_ref[:, sl].astype(F32)
            v = hi_ref[:, sl].astype(F32)
            b = b_ref[:, sl]

            def lag_step(lag, od):
                k_l = pltpu.roll(k, lag, axis=0)
                b_l = pltpu.roll(b, lag, axis=0)
                v_l = pltpu.roll(v, lag, axis=0)
                w = q * k_l * jnp.exp(jnp.minimum(b - b_l, 0.0))
                w = jnp.where(row_in_blk >= lag, w, 0.0)
                return od + jnp.sum(w, axis=1, keepdims=True) * v_l

            od_ref[:, sl] = lax.fori_loop(1, blk, lag_step, jnp.sum(q * k, axis=1, keepdims=True) * v)
        carried_state_and_output()

    @pl.when(t == pl.num_programs(1) - 1)
    def _():
        for h in range(HG_HEADS):
            sout_ref[0, h] = st_ref[h].T


def _hgrn_cumsum_matrix(n, blk):
    t = np.arange(n)[:, None]
    s = np.arange(n)[None, :]
    return jnp.asarray((((t // blk) == (s // blk)) & (s <= t)).astype(np.float32), BF16)


def _hgrn(hq, hk, lfh, hi, shg, s0, norm, *, nseq):
    rows = hq.shape[0]
    t_len = rows // nseq
    tc = min(HG_TC, t_len)
    nt = t_len // tc
    blk = min(HG_BLOCK, tc)
    tri = _hgrn_cumsum_matrix(tc, blk)
    row = pl.BlockSpec((tc, HG_WIDTH), lambda b, t: (b * nt + t, 0))
    state = pl.BlockSpec((1, HG_HEADS, HG_DIM, HG_DIM), lambda b, t: (b, 0, 0, 0))
    return pl.pallas_call(
        _hgrn_kernel,
        grid=(nseq, nt),
        in_specs=[row, row, row, row, row, state,
                  pl.BlockSpec((1, HG_DIM), lambda b, t: (0, 0)),
                  pl.BlockSpec((tc, tc), lambda b, t: (0, 0))],
        out_specs=[row, state],
        out_shape=[jax.ShapeDtypeStruct((rows, HG_WIDTH), BF16),
                   jax.ShapeDtypeStruct((nseq, HG_HEADS, HG_DIM, HG_DIM), F32)],
        scratch_shapes=[
            pltpu.VMEM((HG_HEADS, HG_DIM, HG_DIM), F32),
            pltpu.VMEM((tc, HG_WIDTH), BF16),
            pltpu.VMEM((tc, HG_WIDTH), BF16),
            pltpu.VMEM((tc, HG_WIDTH), F32),
            pltpu.VMEM((tc, HG_WIDTH), F32),
            pltpu.VMEM((tc, HG_WIDTH), F32),
            pltpu.VMEM((tc, HG_WIDTH), F32),
        ],
        compiler_params=pltpu.CompilerParams(dimension_semantics=("arbitrary", "arbitrary"),
                                             vmem_limit_bytes=VMEM_LIMIT),
        name="hgrn",
    )(hq, hk, lfh, hi, shg, s0, norm, tri)


def _mixffn_kernel(x_ref, of_ref, oh_ref, sga_ref, sgb_ref, hist_ref,
                   wbf_ref, wbh_ref, wout_ref, wup_ref, wdn_ref,
                   npost_ref, npre2_ref, npost2_ref, cw_ref, cb_ref,
                   y_ref, conv_ref, tail_ref, *, seg_len):
    tm = x_ref.shape[0]
    br_f = jnp.dot(of_ref[...], wbf_ref[...], preferred_element_type=F32)
    br_h = jnp.dot(oh_ref[...], wbh_ref[...], preferred_element_type=F32)
    merged = sga_ref[...].astype(F32) * br_f + sgb_ref[...].astype(F32) * br_h
    mix = jnp.dot(merged.astype(BF16), wout_ref[...], preferred_element_type=F32)
    x1 = x_ref[...] + _rms_scale(mix) * npost_ref[...]

    h2 = (_rms_scale(x1) * npre2_ref[...]).astype(BF16)
    up = jnp.dot(h2, wup_ref[...], preferred_element_type=F32)
    a = up[:, :D_FF]
    g = up[:, D_FF:]

    prev1 = pltpu.roll(a, 1, axis=0)
    prev2 = pltpu.roll(a, 2, axis=0)
    rowi = lax.broadcasted_iota(jnp.int32, a.shape, 0)
    if seg_len >= tm:
        @pl.when(pl.program_id(0) == 0)
        def _():
            tail_ref[...] = hist_ref[0]
        t0 = tail_ref[0:1, :]
        t1 = tail_ref[1:2, :]
        prev1 = jnp.where(rowi == 0, t1, prev1)
        prev2 = jnp.where(rowi == 0, t0, jnp.where(rowi == 1, t1, prev2))
        tail_ref[...] = a[tm - 2:, :]
        conv_ref[0] = a[tm - 2:, :]
    else:
        for s in range(tm // seg_len):
            h0 = hist_ref[s, 0:1, :]
            h1 = hist_ref[s, 1:2, :]
            prev1 = jnp.where(rowi == s * seg_len, h1, prev1)
            prev2 = jnp.where(rowi == s * seg_len, h0, jnp.where(rowi == s * seg_len + 1, h1, prev2))
            conv_ref[s] = a[(s + 1) * seg_len - 2:(s + 1) * seg_len, :]
    c = cb_ref[...] + cw_ref[0:1, :] * prev2 + cw_ref[1:2, :] * prev1 + cw_ref[2:3, :] * a
    act = (jax.nn.gelu(c, approximate=True) * g).astype(BF16)
    ff = jnp.dot(act, wdn_ref[...], preferred_element_type=F32)
    y_ref[...] = x1 + _rms_scale(ff) * npost2_ref[...]


def _mixffn(x, o_fox, o_hg, sga, sgb, hist, w, *, seg_len):
    rows = x.shape[0]
    tm = min(FFN_TM, rows)
    n = rows // tm
    nseg = hist.shape[0]
    row = lambda width: pl.BlockSpec((tm, width), lambda i: (i, 0))
    weights = [w["bf"], w["bh"], w["out"], w["up"], w["down"]]
    smalls = [w["npost"], w["npre2"], w["npost2"], w["conv_w"], w["conv_b"]]
    hist_spec = pl.BlockSpec(hist.shape, lambda i: (0, 0, 0))
    scratch = [pltpu.VMEM((2, D_FF), F32)]
    return pl.pallas_call(
        functools.partial(_mixffn_kernel, seg_len=seg_len),
        grid=(n,),
        in_specs=[row(D_MODEL), row(FOX_WIDTH), row(HG_WIDTH), row(D_MODEL), row(D_MODEL), hist_spec]
                 + [_const_spec(a.shape) for a in weights] + [_const_spec(a.shape) for a in smalls],
        out_specs=[row(D_MODEL), pl.BlockSpec((nseg, 2, D_FF), lambda i: (0, 0, 0))],
        out_shape=[jax.ShapeDtypeStruct((rows, D_MODEL), F32), jax.ShapeDtypeStruct((nseg, 2, D_FF), F32)],
        scratch_shapes=scratch,
        compiler_params=pltpu.CompilerParams(dimension_semantics=("arbitrary",), vmem_limit_bytes=VMEM_LIMIT),
        name="mixffn",
    )(x, o_fox, o_hg, sga, sgb, hist, *weights, *smalls)


def _prep_w_in(w_in, fox_f_bias):
    offs = np.cumsum([0] + IN_SIZES)
    seg = [w_in[:, int(offs[i]):int(offs[i + 1])] for i in range(len(IN_SIZES))]
    pad = jnp.zeros((D_MODEL, LANES - 3 * FOX_HEADS), w_in.dtype)
    f3 = jnp.concatenate([seg[3], seg[3], seg[3], pad], axis=1)
    w_all = jnp.concatenate(seg[:3] + [f3] + seg[4:], axis=1).astype(BF16)
    fb = fox_f_bias.astype(F32)
    fb3 = jnp.concatenate([fb, fb, fb, jnp.zeros((LANES - 3 * FOX_HEADS,), F32)]).reshape(1, LANES)
    return w_all, fb3


def kernel(x_prompt, x_sample, cache_fox_k, cache_fox_v, cache_fox_logf, state_hgrn, state_ffn_conv, norm_mix_pre, norm_mix_post, w_in, fox_f_bias, hgrn_lb_logits, hgrn_norm, w_branch_fox, w_branch_hgrn, w_out, norm_ffn_pre, norm_ffn_post, w_up, ffn_conv_w, ffn_conv_b, w_down):
    depth = w_in.shape[0]
    assert depth == 1 and hgrn_lb_logits.shape[0] == 2
    bp, seq, _ = x_prompt.shape
    assert bp == 1
    nb, t_new, _ = x_sample.shape
    past = cache_fox_k.shape[2]

    w_all, fb3 = _prep_w_in(w_in[0], fox_f_bias[0])
    g_pre = norm_mix_pre[0].reshape(1, D_MODEL)
    lbl = hgrn_lb_logits.astype(F32)
    hnorm = hgrn_norm[0].astype(F32).reshape(1, HG_DIM)
    w = {
        "bf": w_branch_fox[0].astype(BF16), "bh": w_branch_hgrn[0].astype(BF16), "out": w_out[0].astype(BF16),
        "up": w_up[0].astype(BF16), "down": w_down[0].astype(BF16),
        "npost": norm_mix_post[0].reshape(1, D_MODEL), "npre2": norm_ffn_pre[0].reshape(1, D_MODEL),
        "npost2": norm_ffn_post[0].reshape(1, D_MODEL),
        "conv_w": ffn_conv_w[0], "conv_b": ffn_conv_b[0].reshape(1, D_FF),
    }

    xp = x_prompt.reshape(seq, D_MODEL)
    (qt, kh, vt, stats, pk, pv, plf, hq, hk, lfh, hi, shg, sga, sgb) = _proj(xp, g_pre, w_all, fb3, lbl, fold=True)
    o_fox = _fox_prompt(qt, kh, vt, stats)
    s0 = jnp.zeros((1, HG_HEADS, HG_DIM, HG_DIM), F32)
    o_hg, p_state = _hgrn(hq, hk, lfh, hi, shg, s0, hnorm, nseq=1)
    hist0 = jnp.zeros((1, 2, D_FF), F32)
    yp, pconv = _mixffn(xp, o_fox, o_hg, sga, sgb, hist0, w, seg_len=seq)

    xs = x_sample.reshape(nb * t_new, D_MODEL)
    (qs, sk, sv, slf, hq, hk, lfh, hi, shg, sga, sgb) = _proj(xs, g_pre, w_all, fb3, lbl, fold=False)
    lf_all_t = jnp.concatenate([
        jnp.swapaxes(cache_fox_logf[0].astype(F32), 1, 2),
        jnp.swapaxes(slf.reshape(nb, t_new, FOX_HEADS), 1, 2),
        jnp.zeros((nb, FOX_HEADS, LANES - t_new), F32)], axis=2)
    cache_kt = jnp.transpose(cache_fox_k[0], (0, 2, 3, 1))
    cache_vt = jnp.transpose(cache_fox_v[0], (0, 2, 3, 1))
    o_fox_s = _fox_sample(qs, sk, sv, lf_all_t, cache_kt, cache_vt, t_new=t_new)
    o_hg_s, s_state = _hgrn(hq, hk, lfh, hi, shg, state_hgrn[0].astype(F32), hnorm, nseq=nb)
    ys, sconv = _mixffn(xs, o_fox_s, o_hg_s, sga, sgb, state_ffn_conv[0], w, seg_len=t_new)

    return (
        yp.reshape(bp, seq, D_MODEL),
        ys.reshape(nb, t_new, D_MODEL),
        pk.reshape(1, bp, seq, FOX_HEADS, FOX_HEAD_DIM),
        pv.reshape(1, bp, seq, FOX_HEADS, FOX_HEAD_DIM),
        plf.reshape(1, bp, seq, FOX_HEADS),
        p_state.reshape(1, bp, HG_HEADS, HG_DIM, HG_DIM),
        pconv.reshape(1, bp, 2, D_FF),
        sk.reshape(1, nb, t_new, FOX_HEADS, FOX_HEAD_DIM),
        sv.reshape(1, nb, t_new, FOX_HEADS, FOX_HEAD_DIM),
        slf.reshape(1, nb, t_new, FOX_HEADS),
        s_state.reshape(1, nb, HG_HEADS, HG_DIM, HG_DIM),
        sconv.reshape(1, nb, 2, D_FF),
    )
```

```python
import functools

import numpy as np
import jax
import jax.numpy as jnp
from jax import lax
from jax.experimental import pallas as pl
from jax.experimental.pallas import tpu as pltpu

F32 = jnp.float32
BF16 = jnp.bfloat16

D_MODEL = 1024
FOX_HEADS = 8
FOX_HEAD_DIM = 64
FOX_WIDTH = FOX_HEADS * FOX_HEAD_DIM
HG_HEADS = 4
HG_DIM = 128
HG_WIDTH = HG_HEADS * HG_DIM
D_FF = 2816
RMS_EPS = 1e-6
NEG_INF = -1e30
LOG2E = 1.4426950408889634
FOX_SKIP_NATS = 110.0
IN_SIZES = [FOX_WIDTH, FOX_WIDTH, FOX_WIDTH, FOX_HEADS, HG_WIDTH, HG_WIDTH, HG_WIDTH, HG_WIDTH, D_MODEL, D_MODEL]

LANES = 128
FOX_PAD = 2 * FOX_HEAD_DIM
FOX_V_ROWS = FOX_HEAD_DIM + 16
HG_BLOCK = 64
HG_SPLIT_MAX = 60.0
VMEM_LIMIT = 56 * 1024 * 1024

PROJ_TM = 256
FOX_TQ = 512
FOX_TK = 512
HG_TC = 256
FFN_TM = 256
FFN_CHUNK = 256
SAMPLE_TK = 1024

_C_Q, _C_K, _C_V, _C_F = 0, 512, 1024, 1536
_C_HQ, _C_HF, _C_HI, _C_HG = 1664, 2176, 2688, 3200
_C_GA, _C_GB, _C_END = 3712, 4736, 5760


def _split3(x):
    hi = x.astype(BF16)
    r = x - hi.astype(F32)
    mid = r.astype(BF16)
    lo = (r - mid.astype(F32)).astype(BF16)
    return hi, mid, lo


def _sum_by_01_matrix(mat01, x):
    cat = jnp.concatenate(_split3(x), axis=1)
    y = jnp.dot(mat01, cat, preferred_element_type=F32)
    return y[:, :LANES] + y[:, LANES:2 * LANES] + y[:, 2 * LANES:]


def _rms_scale(x):
    return x * lax.rsqrt(jnp.mean(x * x, axis=-1, keepdims=True) + RMS_EPS)


def _log_sigmoid(x):
    return jnp.minimum(x, 0.0) - jnp.log1p(jnp.exp(-jnp.abs(x)))


def _sigmoid(x):
    return 1.0 / (1.0 + jnp.exp(-x))


def _proj_kernel(*refs, fold):
    if fold:
        (x_ref, g_ref, w_ref, fb_ref, lbl_ref, tri_ref, pq_ref, pk_ref, cq_ref, ck_ref, cv_ref, seg_ref,
         qh_ref, kh_ref, vh_ref, stat_ref, kout_ref, vout_ref, lf_ref, hq_ref, hk_ref, lfh_ref, hi_ref, shg_ref,
         sga_ref, sgb_ref, carry_ref) = refs
    else:
        (x_ref, g_ref, w_ref, fb_ref, lbl_ref,
         qs_ref, kout_ref, vout_ref, lf_ref, hq_ref, hk_ref, lfh_ref, hi_ref, shg_ref,
         sga_ref, sgb_ref) = refs

    h = (_rms_scale(x_ref[...]) * g_ref[...]).astype(BF16)
    z = jnp.dot(h, w_ref[...], preferred_element_type=F32)

    zq = z[:, _C_Q:_C_K] * (FOX_HEAD_DIM ** -0.5)
    zk = z[:, _C_K:_C_V]
    zv = z[:, _C_V:_C_F]
    kout_ref[...] = zk
    vout_ref[...] = zv
    logf = _log_sigmoid(z[:, _C_F:_C_HQ] + fb_ref[...])
    lf_ref[...] = logf[:, :FOX_HEADS]

    l0 = lbl_ref[0:1, :]
    l1 = lbl_ref[1:2, :]
    lmax = jnp.maximum(l0, l1)
    e0 = jnp.exp(l0 - lmax)
    lb = e0 / (e0 + jnp.exp(l1 - lmax))
    f = lb + (1.0 - lb) * _sigmoid(z[:, _C_HF:_C_HI])
    hq_ref[...] = z[:, _C_HQ:_C_HF].astype(BF16)
    hk_ref[...] = (1.0 - f).astype(BF16)
    lfh_ref[...] = jnp.log(f)
    hi_ref[...] = z[:, _C_HI:_C_HG].astype(BF16)
    shg_ref[...] = _sigmoid(z[:, _C_HG:_C_GA]).astype(BF16)
    sga_ref[...] = _sigmoid(z[:, _C_GA:_C_GB]).astype(BF16)
    sgb_ref[...] = _sigmoid(z[:, _C_GB:_C_END]).astype(BF16)

    if not fold:
        qs_ref[...] = zq.astype(BF16)
        return

    @pl.when(pl.program_id(0) == 0)
    def _():
        carry_ref[...] = jnp.zeros_like(carry_ref)

    cum = carry_ref[...] + _sum_by_01_matrix(tri_ref[...], logf)
    carry_ref[...] = cum[-1:, :]

    seg = seg_ref[...]
    qn2 = jnp.dot((zq * zq).astype(BF16), seg, preferred_element_type=F32)
    kn2 = jnp.dot((zk * zk).astype(BF16), seg, preferred_element_type=F32)
    dg = jnp.dot((zq * zk).astype(BF16), seg, preferred_element_type=F32)
    stat_ref[0, 0:1, :] = jnp.max(qn2, axis=0, keepdims=True)
    stat_ref[0, 1:2, :] = jnp.max(kn2, axis=0, keepdims=True)
    stat_ref[0, 2:3, :] = jnp.min(dg, axis=0, keepdims=True)
    stat_ref[0, 3:4, :] = cum[0:1, :]
    stat_ref[0, 4:5, :] = cum[-1:, :]
    stat_ref[0, 5:8, :] = jnp.zeros((3, LANES), F32)

    zq = zq * LOG2E
    c_hi, c_mid, c_lo = _split3(cum * LOG2E)
    lane = lax.broadcasted_iota(jnp.int32, cum.shape, 1)
    pieces = jnp.where(lane < 8, c_hi, jnp.where(lane < 16, c_mid, c_lo))
    pieces = jnp.where(lane < 24, pieces, jnp.zeros_like(pieces))
    ex_q = jnp.dot(pieces, pq_ref[...], preferred_element_type=F32) + cq_ref[...]
    ex_k = jnp.dot(pieces, pk_ref[...], preferred_element_type=F32) + ck_ref[...]
    ex_v = cv_ref[...]

    low = lax.broadcasted_iota(jnp.int32, (zq.shape[0], LANES), 1) < FOX_HEAD_DIM
    for src, ex, dst, transposed in ((zq, ex_q, qh_ref, True), (zk, ex_k, kh_ref, False), (zv, ex_v, vh_ref, True)):
        for c in range(FOX_WIDTH // LANES):
            pair = src[:, c * LANES:(c + 1) * LANES]
            swapped = pltpu.roll(pair, FOX_HEAD_DIM, axis=1)
            for j, data in enumerate((pair, swapped)):
                hd = 2 * c + j
                blk = jnp.where(low, data, ex[:, hd * LANES:(hd + 1) * LANES])
                dst[hd] = (blk.T[:dst.shape[1]] if transposed else blk).astype(BF16)


def _bias_fold_constants():
    pq = np.zeros((LANES, FOX_HEADS * LANES), np.float32)
    pk = np.zeros((LANES, FOX_HEADS * LANES), np.float32)
    cq = np.zeros((1, FOX_HEADS * LANES), np.float32)
    ck = np.zeros((1, FOX_HEADS * LANES), np.float32)
    cv = np.zeros((1, FOX_HEADS * LANES), np.float32)
    for h in range(FOX_HEADS):
        base = h * LANES + FOX_HEAD_DIM
        for p in range(3):
            pq[p * 8 + h, base + p] = 1.0
            ck[0, base + p] = 1.0
            pk[p * 8 + h, base + 3 + p] = -1.0
            cq[0, base + 3 + p] = 1.0
        cv[0, base] = 1.0
    return (jnp.asarray(pq, BF16), jnp.asarray(pk, BF16), jnp.asarray(cq), jnp.asarray(ck), jnp.asarray(cv))


def _const_spec(shape, single=True):
    nd = len(shape)
    if single:
        return pl.BlockSpec(shape, lambda *_: (0,) * nd, pipeline_mode=pl.Buffered(1))
    return pl.BlockSpec(shape, lambda *_: (0,) * nd)


def _proj(x, gain, w_all, fb3, lb_logits, *, fold):
    rows = x.shape[0]
    tm = min(PROJ_TM, rows)
    n = rows // tm
    row = lambda width: pl.BlockSpec((tm, width), lambda i: (i, 0))
    in_specs = [row(D_MODEL), _const_spec((1, D_MODEL)), _const_spec(w_all.shape), _const_spec((1, LANES)),
                _const_spec(lb_logits.shape)]
    args = [x, gain, w_all, fb3, lb_logits]
    common_out = [
        (jax.ShapeDtypeStruct((rows, FOX_WIDTH), F32), row(FOX_WIDTH)),
        (jax.ShapeDtypeStruct((rows, FOX_WIDTH), F32), row(FOX_WIDTH)),
        (jax.ShapeDtypeStruct((rows, FOX_HEADS), F32), row(FOX_HEADS)),
        (jax.ShapeDtypeStruct((rows, HG_WIDTH), BF16), row(HG_WIDTH)),
        (jax.ShapeDtypeStruct((rows, HG_WIDTH), BF16), row(HG_WIDTH)),
        (jax.ShapeDtypeStruct((rows, HG_WIDTH), F32), row(HG_WIDTH)),
        (jax.ShapeDtypeStruct((rows, HG_WIDTH), BF16), row(HG_WIDTH)),
        (jax.ShapeDtypeStruct((rows, HG_WIDTH), BF16), row(HG_WIDTH)),
        (jax.ShapeDtypeStruct((rows, D_MODEL), BF16), row(D_MODEL)),
        (jax.ShapeDtypeStruct((rows, D_MODEL), BF16), row(D_MODEL)),
    ]
    scratch = []
    if fold:
        tri = jnp.asarray(np.tril(np.ones((tm, tm), np.float32)), BF16)
        seg = np.zeros((FOX_WIDTH, LANES), np.float32)
        seg[np.arange(FOX_WIDTH), np.arange(FOX_WIDTH) // FOX_HEAD_DIM] = 1.0
        consts = _bias_fold_constants() + (jnp.asarray(seg, BF16),)
        in_specs += [_const_spec(tri.shape)] + [_const_spec(c.shape) for c in consts]
        args += [tri, *consts]
        head_major = (jax.ShapeDtypeStruct((FOX_HEADS, rows, FOX_PAD), BF16),
                      pl.BlockSpec((FOX_HEADS, tm, FOX_PAD), lambda i: (0, i, 0)))
        head_major_t = lambda depth: (jax.ShapeDtypeStruct((FOX_HEADS, depth, rows), BF16),
                                      pl.BlockSpec((FOX_HEADS, depth, tm), lambda i: (0, 0, i)))
        stats = (jax.ShapeDtypeStruct((n, 8, LANES), F32), pl.BlockSpec((1, 8, LANES), lambda i: (i, 0, 0)))
        outs = [head_major_t(FOX_PAD), head_major, head_major_t(FOX_V_ROWS), stats] + common_out
        scratch = [pltpu.VMEM((1, LANES), F32)]
    else:
        outs = [(jax.ShapeDtypeStruct((rows, FOX_WIDTH), BF16), row(FOX_WIDTH))] + common_out
    return pl.pallas_call(
        functools.partial(_proj_kernel, fold=fold),
        grid=(n,),
        in_specs=in_specs,
        out_specs=[o[1] for o in outs],
        out_shape=[o[0] for o in outs],
        scratch_shapes=scratch,
        compiler_params=pltpu.CompilerParams(dimension_semantics=("arbitrary",), vmem_limit_bytes=VMEM_LIMIT),
        name="proj_fold" if fold else "proj",
    )(*args)


def _fox_kernel(qi_ref, ki_ref, mode_ref, qt_ref, k_ref, vt_ref, o_ref, m_ref, acc_ref):
    mode = mode_ref[pl.program_id(0)]
    tq = qt_ref.shape[2]
    tk = k_ref.shape[1]

    @pl.when((mode & _MODE_FIRST) != 0)
    def _():
        m_ref[...] = jnp.full_like(m_ref, NEG_INF)
        acc_ref[...] = jnp.zeros_like(acc_ref)

    def logits(h):
        return jnp.dot(k_ref[h], qt_ref[h], preferred_element_type=F32)

    def softmax_update(h, s, masked):
        if masked:
            key = lax.broadcasted_iota(jnp.int32, (tk, tq), 0)
            qry = lax.broadcasted_iota(jnp.int32, (tk, tq), 1)
            s = jnp.where(key <= qry, s, NEG_INF)
        m_prev = m_ref[h]
        m_new = jnp.maximum(m_prev, jnp.max(s, axis=0, keepdims=True))
        alpha = jnp.exp2(m_prev - m_new)
        p = jnp.exp2(s - m_new).astype(BF16)
        acc_ref[h] = alpha * acc_ref[h] + jnp.dot(vt_ref[h], p, preferred_element_type=F32)
        m_ref[h] = m_new

    def all_heads(masked):
        s_next = logits(0)
        for h in range(FOX_HEADS):
            s = s_next
            if h + 1 < FOX_HEADS:
                s_next = logits(h + 1)
            softmax_update(h, s, masked)

    @pl.when((mode & _MODE_PAIR) != 0)
    def _():
        all_heads(False)

    @pl.when((mode & _MODE_DIAG) != 0)
    def _():
        all_heads(True)
        for c in range(FOX_WIDTH // LANES):
            halves = []
            for hd in (2 * c, 2 * c + 1):
                acc = acc_ref[hd]
                halves.append(acc[:FOX_HEAD_DIM, :] / acc[FOX_HEAD_DIM:FOX_HEAD_DIM + 1, :])
            o_ref[:, c * LANES:(c + 1) * LANES] = jnp.concatenate(halves, axis=0).T.astype(o_ref.dtype)


_MODE_FIRST, _MODE_PAIR, _MODE_DIAG = 1, 2, 4


def _fox_schedule(stats, nq):
    st = stats[:, :5, :FOX_HEADS].reshape(nq, -1, 5, FOX_HEADS)
    qn = jnp.sqrt(jnp.max(st[:, :, 0], axis=1)) * 1.01
    kn = jnp.sqrt(jnp.max(st[:, :, 1], axis=1)) * 1.01
    dmin = jnp.min(st[:, :, 2], axis=1)
    c_first = st[:, 0, 3]
    c_last = st[:, -1, 4]
    bound = qn[:, None] * kn[None, :] + (c_first - dmin)[:, None] - c_last[None, :]
    blk = jnp.arange(nq, dtype=jnp.int32)
    drop = jnp.all(bound < -FOX_SKIP_NATS, axis=-1) & (blk[None, :] < blk[:, None])
    prefix = jnp.cumsum(jnp.logical_not(drop).astype(jnp.int32), axis=1) == 0
    kstart = jnp.sum(prefix.astype(jnp.int32), axis=1)
    count = blk - kstart + 1
    ends = jnp.cumsum(count)
    nsteps = nq * (nq + 1) // 2
    step = jnp.arange(nsteps, dtype=jnp.int32)
    valid = step < ends[-1]
    done = ends[None, :] <= step[:, None]
    q_of = jnp.minimum(jnp.sum(done.astype(jnp.int32), axis=1), nq - 1)
    begin = jnp.sum(jnp.where(done, count[None, :], 0), axis=1)
    k_first = jnp.sum(jnp.where(blk[None, :] == q_of[:, None], kstart[None, :], 0), axis=1)
    k_of = k_first + (step - begin)
    mode = jnp.where(k_of == q_of, _MODE_DIAG, _MODE_PAIR) + jnp.where(k_of == k_first, _MODE_FIRST, 0)
    mode = jnp.where(valid, mode, 0)
    q_of = jnp.where(valid, q_of, nq - 1)
    k_of = jnp.where(valid, k_of, nq - 1)
    return q_of.astype(jnp.int32), k_of.astype(jnp.int32), mode.astype(jnp.int32)


def _fox_prompt(qt, kh, vt, stats):
    seq = kh.shape[1]
    assert FOX_TQ == FOX_TK
    nq = seq // FOX_TQ
    qi, ki, mode = _fox_schedule(stats, nq)
    grid_spec = pltpu.PrefetchScalarGridSpec(
        num_scalar_prefetch=3,
        grid=(nq * (nq + 1) // 2,),
        in_specs=[
            pl.BlockSpec((FOX_HEADS, FOX_PAD, FOX_TQ), lambda s, qi, ki, mode: (0, 0, qi[s])),
            pl.BlockSpec((FOX_HEADS, FOX_TK, FOX_PAD), lambda s, qi, ki, mode: (0, ki[s], 0)),
            pl.BlockSpec((FOX_HEADS, FOX_V_ROWS, FOX_TK), lambda s, qi, ki, mode: (0, 0, ki[s])),
        ],
        out_specs=pl.BlockSpec((FOX_TQ, FOX_WIDTH), lambda s, qi, ki, mode: (qi[s], 0)),
        scratch_shapes=[pltpu.VMEM((FOX_HEADS, 1, FOX_TQ), F32),
                        pltpu.VMEM((FOX_HEADS, FOX_V_ROWS, FOX_TQ), F32)],
    )
    return pl.pallas_call(
        _fox_kernel,
        grid_spec=grid_spec,
        out_shape=jax.ShapeDtypeStruct((seq, FOX_WIDTH), BF16),
        compiler_params=pltpu.CompilerParams(dimension_semantics=("arbitrary",), vmem_limit_bytes=VMEM_LIMIT),
        name="fox_prompt",
    )(qi, ki, mode, qt, kh, vt)


def _lane_cumsum(x):
    n = x.shape[1]
    lane = lax.broadcasted_iota(jnp.int32, x.shape, 1)
    shift = 1
    while shift < n:
        x = x + jnp.where(lane >= shift, pltpu.roll(x, shift, axis=1), 0.0)
        shift *= 2
    return x


def _fox_sample_kernel(q_ref, kn_ref, vn_ref, lft_ref, ck_ref, cv_ref, o_ref,
                       cq_ref, cum_ref, m_ref, l_ref, acc_ref, *, n_past_chunks, tk, t_new):
    c = pl.program_id(1)
    rows = FOX_HEADS * t_new
    heads = [slice(h * FOX_HEAD_DIM, (h + 1) * FOX_HEAD_DIM) for h in range(FOX_HEADS)]
    head_rows = [slice(h * t_new, (h + 1) * t_new) for h in range(FOX_HEADS)]

    @pl.when(c == 0)
    def _():
        cum = _lane_cumsum(lft_ref[0])
        for j in range(n_past_chunks):
            cum_ref[j] = cum[:, j * tk:(j + 1) * tk]
        new_cum = cum[:, n_past_chunks * tk:n_past_chunks * tk + LANES]
        cum_ref[n_past_chunks, :, :LANES] = new_cum
        new_cum_t = jnp.concatenate([new_cum] * (LANES // FOX_HEADS), axis=0).T
        for h in range(FOX_HEADS):
            cq_ref[head_rows[h], :] = jnp.broadcast_to(new_cum_t[:t_new, h:h + 1], (t_new, LANES))
        m_ref[...] = jnp.full_like(m_ref, NEG_INF)
        l_ref[...] = jnp.zeros_like(l_ref)
        acc_ref[...] = jnp.zeros_like(acc_ref)

    nt = (((1,), (1,)), ((), ()))

    def update(qk, pv_of, ck_rows, mask):
        q = q_ref[...]
        s = jnp.concatenate([qk(q[:, heads[h]], h) for h in range(FOX_HEADS)], axis=0)
        s = s + (cq_ref[:, :1] - ck_rows)
        if mask is not None:
            s = jnp.where(mask, s, NEG_INF)
        m_prev = m_ref[...]
        m_new = jnp.maximum(m_prev, jnp.max(s, axis=1, keepdims=True))
        alpha = jnp.exp(m_prev - m_new)
        p = jnp.exp(s - m_new[:, :1])
        l_ref[...] = alpha * l_ref[...] + jnp.sum(p, axis=1, keepdims=True)
        p = p.astype(BF16)
        pv = jnp.concatenate([pv_of(p[head_rows[h]], h) for h in range(FOX_HEADS)], axis=0)
        acc_ref[...] = alpha[:, :FOX_HEAD_DIM] * acc_ref[...] + pv
        m_ref[...] = m_new

    def expand_rows(x, width):
        return jnp.concatenate([jnp.broadcast_to(x[h:h + 1, :], (t_new, width)) for h in range(FOX_HEADS)], axis=0)

    @pl.when(c < n_past_chunks)
    def _():
        update(lambda qh, h: jnp.dot(qh, ck_ref[0, h].astype(BF16), preferred_element_type=F32),
               lambda ph, h: lax.dot_general(ph, cv_ref[0, h].astype(BF16), nt, preferred_element_type=F32),
               expand_rows(cum_ref[c], tk), None)

    @pl.when(c == n_past_chunks)
    def _():
        ck_rows = expand_rows(cum_ref[n_past_chunks, :, :LANES], LANES)[:, :t_new]
        rowt = lax.broadcasted_iota(jnp.int32, (rows, t_new), 0) % t_new
        coli = lax.broadcasted_iota(jnp.int32, (rows, t_new), 1)
        update(lambda qh, h: lax.dot_general(qh, kn_ref[:, heads[h]].astype(BF16), nt, preferred_element_type=F32),
               lambda ph, h: jnp.dot(ph, vn_ref[:, heads[h]].astype(BF16), preferred_element_type=F32),
               ck_rows, coli <= rowt)
        out = acc_ref[...] / l_ref[:, :FOX_HEAD_DIM]
        o_ref[...] = jnp.concatenate([out[head_rows[h]] for h in range(FOX_HEADS)], axis=1).astype(o_ref.dtype)


def _fox_sample(qs, k_new, v_new, lf_all_t, cache_k, cache_v, *, t_new):
    nb, past = cache_k.shape[0], cache_k.shape[3]
    tk = SAMPLE_TK
    npc = past // tk
    rows = FOX_HEADS * t_new
    last = npc - 1
    kern = functools.partial(_fox_sample_kernel, n_past_chunks=npc, tk=tk, t_new=t_new)
    cache_spec = pl.BlockSpec((1, FOX_HEADS, FOX_HEAD_DIM, tk), lambda b, c: (b, 0, 0, jnp.minimum(c, last)))
    return pl.pallas_call(
        kern,
        grid=(nb, npc + 1),
        in_specs=[
            pl.BlockSpec((t_new, FOX_WIDTH), lambda b, c: (b, 0)),
            pl.BlockSpec((t_new, FOX_WIDTH), lambda b, c: (b, 0)),
            pl.BlockSpec((t_new, FOX_WIDTH), lambda b, c: (b, 0)),
            pl.BlockSpec((1, FOX_HEADS, past + LANES), lambda b, c: (b, 0, 0)),
            cache_spec, cache_spec,
        ],
        out_specs=pl.BlockSpec((t_new, FOX_WIDTH), lambda b, c: (b, 0)),
        out_shape=jax.ShapeDtypeStruct((nb * t_new, FOX_WIDTH), BF16),
        scratch_shapes=[
            pltpu.VMEM((rows, LANES), F32),
            pltpu.VMEM((npc + 1, FOX_HEADS, tk), F32),
            pltpu.VMEM((rows, LANES), F32),
            pltpu.VMEM((rows, LANES), F32),
            pltpu.VMEM((rows, FOX_HEAD_DIM), F32),
        ],
        compiler_params=pltpu.CompilerParams(dimension_semantics=("arbitrary", "arbitrary"),
                                             vmem_limit_bytes=VMEM_LIMIT),
        name="fox_sample",
    )(qs, k_new, v_new, lf_all_t, cache_k, cache_v)


def _hgrn_kernel(hq_ref, hk_ref, lfh_ref, hi_ref, shg_ref, s0_ref, norm_ref, tri_ref,
                 o_ref, sout_ref, st_ref, qq_ref, kk_ref, eb_ref, b_ref, oin_ref, od_ref):
    t = pl.program_id(1)
    tc = hq_ref.shape[0]
    blk = min(HG_BLOCK, tc)
    nblk = tc // blk
    heads = [slice(h * HG_DIM, (h + 1) * HG_DIM) for h in range(HG_HEADS)]

    @pl.when(t == 0)
    def _():
        for h in range(HG_HEADS):
            st_ref[h] = s0_ref[0, h].T

    def at_block_row(x, r):
        x3 = x.reshape(nblk, blk, HG_DIM)
        return jnp.broadcast_to(x3[:, r:r + 1, :], x3.shape).reshape(tc, HG_DIM)

    def rel_to_middle(b):
        return b - at_block_row(b, blk // 2 - 1)

    worst = jnp.zeros((), F32)
    for sl in heads:
        b = _sum_by_01_matrix(tri_ref[...], lfh_ref[:, sl])
        eb = jnp.exp(b)
        qq_ref[:, sl] = (hq_ref[:, sl].astype(F32) * eb).astype(BF16)
        kk_ref[:, sl] = (hk_ref[:, sl].astype(F32) * jnp.exp(at_block_row(b, blk - 1) - b)).astype(BF16)
        eb_ref[:, sl] = eb
        b_ref[:, sl] = b
        worst = jnp.maximum(worst, jnp.max(jnp.abs(rel_to_middle(b))))

    def carried_state_and_output():
        for j in range(nblk):
            rows = slice(j * blk, (j + 1) * blk)
            for h, sl in enumerate(heads):
                st = st_ref[h]
                oin_ref[rows, sl] = lax.dot_general(qq_ref[rows, sl], st.astype(BF16), (((1,), (1,)), ((), ())),
                                                    preferred_element_type=F32)
                upd = lax.dot_general(hi_ref[rows, sl], kk_ref[rows, sl], (((0,), (0,)), ((), ())),
                                      preferred_element_type=F32)
                st_ref[h] = st * eb_ref[(j + 1) * blk - 1:(j + 1) * blk, sl] + upd
        for sl in heads:
            o = oin_ref[:, sl] + od_ref[:, sl]
            y = (_rms_scale(o) * norm_ref[...]) * shg_ref[:, sl].astype(F32)
            o_ref[:, sl] = y.astype(o_ref.dtype)

    splittable = worst <= HG_SPLIT_MAX

    @pl.when(splittable)
    def _():
        row = lax.broadcasted_iota(jnp.int32, (tc, tc), 0)
        col = lax.broadcasted_iota(jnp.int32, (tc, tc), 1)
        pair_in_block = (row // blk == col // blk) & (col <= row)
        for sl in heads:
            b_rel = rel_to_middle(b_ref[:, sl])
            qs = (hq_ref[:, sl].astype(F32) * jnp.exp(b_rel)).astype(BF16)
            ks = (hk_ref[:, sl].astype(F32) * jnp.exp(-b_rel)).astype(BF16)
            a = lax.dot_general(qs, ks, (((1,), (1,)), ((), ())), preferred_element_type=F32)
            a = jnp.where(pair_in_block, a, 0.0).astype(BF16)
            od_ref[:, sl] = jnp.dot(a, hi_ref[:, sl], preferred_element_type=F32)
        carried_state_and_output()

    @pl.when(jnp.logical_not(splittable))
    def _():
        row_in_blk = lax.broadcasted_iota(jnp.int32, (tc, HG_DIM), 0) % blk
        for sl in heads:
            q = hq_ref[:, sl].astype(F32)
            k = hk_ref[:, sl].astype(F32)
            v = hi_ref[:, sl].astype(F32)
            b = b_ref[:, sl]

            def lag_step(lag, od):
                k_l = pltpu.roll(k, lag, axis=0)
                b_l = pltpu.roll(b, lag, axis=0)
                v_l = pltpu.roll(v, lag, axis=0)
                w = q * k_l * jnp.exp(jnp.minimum(b - b_l, 0.0))
                w = jnp.where(row_in_blk >= lag, w, 0.0)
                return od + jnp.sum(w, axis=1, keepdims=True) * v_l

            od_ref[:, sl] = lax.fori_loop(1, blk, lag_step, jnp.sum(q * k, axis=1, keepdims=True) * v)
        carried_state_and_output()

    @pl.when(t == pl.num_programs(1) - 1)
    def _():
        for h in range(HG_HEADS):
            sout_ref[0, h] = st_ref[h].T


def _hgrn_cumsum_matrix(n, blk):
    t = np.arange(n)[:, None]
    s = np.arange(n)[None, :]
    return jnp.asarray((((t // blk) == (s // blk)) & (s <= t)).astype(np.float32), BF16)


def _hgrn(hq, hk, lfh, hi, shg, s0, norm, *, nseq):
    rows = hq.shape[0]
    t_len = rows // nseq
    tc = min(HG_TC, t_len)
    nt = t_len // tc
    blk = min(HG_BLOCK, tc)
    tri = _hgrn_cumsum_matrix(tc, blk)
    row = pl.BlockSpec((tc, HG_WIDTH), lambda b, t: (b * nt + t, 0))
    state = pl.BlockSpec((1, HG_HEADS, HG_DIM, HG_DIM), lambda b, t: (b, 0, 0, 0))
    return pl.pallas_call(
        _hgrn_kernel,
        grid=(nseq, nt),
        in_specs=[row, row, row, row, row, state,
                  pl.BlockSpec((1, HG_DIM), lambda b, t: (0, 0)),
                  pl.BlockSpec((tc, tc), lambda b, t: (0, 0))],
        out_specs=[row, state],
        out_shape=[jax.ShapeDtypeStruct((rows, HG_WIDTH), BF16),
                   jax.ShapeDtypeStruct((nseq, HG_HEADS, HG_DIM, HG_DIM), F32)],
        scratch_shapes=[
            pltpu.VMEM((HG_HEADS, HG_DIM, HG_DIM), F32),
            pltpu.VMEM((tc, HG_WIDTH), BF16),
            pltpu.VMEM((tc, HG_WIDTH), BF16),
            pltpu.VMEM((tc, HG_WIDTH), F32),
            pltpu.VMEM((tc, HG_WIDTH), F32),
            pltpu.VMEM((tc, HG_WIDTH), F32),
            pltpu.VMEM((tc, HG_WIDTH), F32),
        ],
        compiler_params=pltpu.CompilerParams(dimension_semantics=("arbitrary", "arbitrary"),
                                             vmem_limit_bytes=VMEM_LIMIT),
        name="hgrn",
    )(hq, hk, lfh, hi, shg, s0, norm, tri)


def _mixffn_kernel(x_ref, of_ref, oh_ref, sga_ref, sgb_ref, hist_ref,
                   wbf_ref, wbh_ref, wout_ref, wup_ref, wdn_ref,
                   npost_ref, npre2_ref, npost2_ref, cw_ref, cb_ref,
                   y_ref, conv_ref, tail_ref, *, seg_len):
    tm = x_ref.shape[0]
    br_f = jnp.dot(of_ref[...], wbf_ref[...], preferred_element_type=F32)
    br_h = jnp.dot(oh_ref[...], wbh_ref[...], preferred_element_type=F32)
    merged = sga_ref[...].astype(F32) * br_f + sgb_ref[...].astype(F32) * br_h
    mix = jnp.dot(merged.astype(BF16), wout_ref[...], preferred_element_type=F32)
    x1 = x_ref[...] + _rms_scale(mix) * npost_ref[...]

    h2 = (_rms_scale(x1) * npre2_ref[...]).astype(BF16)
    def up_chunk(j):
        return jnp.dot(h2, wup_ref[:, 2 * j * FFN_CHUNK:(2 * j + 2) * FFN_CHUNK], preferred_element_type=F32)

    n_chunks = D_FF // FFN_CHUNK
    carried = seg_len >= tm
    if carried:
        @pl.when(pl.program_id(0) == 0)
        def _():
            tail_ref[...] = hist_ref[0]
    rowi = lax.broadcasted_iota(jnp.int32, (tm, FFN_CHUNK), 0)
    ff = None
    up_next = up_chunk(0)
    for j in range(n_chunks):
        cols = slice(j * FFN_CHUNK, (j + 1) * FFN_CHUNK)
        up = up_next
        if j + 1 < n_chunks:
            up_next = up_chunk(j + 1)
        a = up[:, :FFN_CHUNK]
        g = up[:, FFN_CHUNK:]
        prev1 = pltpu.roll(a, 1, axis=0)
        prev2 = pltpu.roll(a, 2, axis=0)
        if carried:
            t0 = tail_ref[0:1, cols]
            t1 = tail_ref[1:2, cols]
            prev1 = jnp.where(rowi == 0, t1, prev1)
            prev2 = jnp.where(rowi == 0, t0, jnp.where(rowi == 1, t1, prev2))
            tail_ref[:, cols] = a[tm - 2:, :]
            conv_ref[0, :, cols] = a[tm - 2:, :]
        else:
            for s in range(tm // seg_len):
                h0 = hist_ref[s, 0:1, cols]
                h1 = hist_ref[s, 1:2, cols]
                prev1 = jnp.where(rowi == s * seg_len, h1, prev1)
                prev2 = jnp.where(rowi == s * seg_len, h0, jnp.where(rowi == s * seg_len + 1, h1, prev2))
                conv_ref[s, :, cols] = a[(s + 1) * seg_len - 2:(s + 1) * seg_len, :]
        c = cb_ref[:, cols] + cw_ref[0:1, cols] * prev2 + cw_ref[1:2, cols] * prev1 + cw_ref[2:3, cols] * a
        act = (jax.nn.gelu(c, approximate=True) * g).astype(BF16)
        part = jnp.dot(act, wdn_ref[cols, :], preferred_element_type=F32)
        ff = part if ff is None else ff + part
    y_ref[...] = x1 + _rms_scale(ff) * npost2_ref[...]


def _mixffn(x, o_fox, o_hg, sga, sgb, hist, w, *, seg_len):
    rows = x.shape[0]
    tm = min(FFN_TM, rows)
    n = rows // tm
    nseg = hist.shape[0]
    row = lambda width: pl.BlockSpec((tm, width), lambda i: (i, 0))
    weights = [w["bf"], w["bh"], w["out"], w["up"], w["down"]]
    smalls = [w["npost"], w["npre2"], w["npost2"], w["conv_w"], w["conv_b"]]
    hist_spec = pl.BlockSpec(hist.shape, lambda i: (0, 0, 0))
    scratch = [pltpu.VMEM((2, D_FF), F32)]
    return pl.pallas_call(
        functools.partial(_mixffn_kernel, seg_len=seg_len),
        grid=(n,),
        in_specs=[row(D_MODEL), row(FOX_WIDTH), row(HG_WIDTH), row(D_MODEL), row(D_MODEL), hist_spec]
                 + [_const_spec(a.shape) for a in weights] + [_const_spec(a.shape) for a in smalls],
        out_specs=[row(D_MODEL), pl.BlockSpec((nseg, 2, D_FF), lambda i: (0, 0, 0))],
        out_shape=[jax.ShapeDtypeStruct((rows, D_MODEL), F32), jax.ShapeDtypeStruct((nseg, 2, D_FF), F32)],
        scratch_shapes=scratch,
        compiler_params=pltpu.CompilerParams(dimension_semantics=("arbitrary",), vmem_limit_bytes=VMEM_LIMIT),
        name="mixffn",
    )(x, o_fox, o_hg, sga, sgb, hist, *weights, *smalls)


def _prep_w_in(w_in, fox_f_bias):
    offs = np.cumsum([0] + IN_SIZES)
    seg = [w_in[:, int(offs[i]):int(offs[i + 1])] for i in range(len(IN_SIZES))]
    pad = jnp.zeros((D_MODEL, LANES - 3 * FOX_HEADS), w_in.dtype)
    f3 = jnp.concatenate([seg[3], seg[3], seg[3], pad], axis=1)
    w_all = jnp.concatenate(seg[:3] + [f3] + seg[4:], axis=1).astype(BF16)
    fb = fox_f_bias.astype(F32)
    fb3 = jnp.concatenate([fb, fb, fb, jnp.zeros((LANES - 3 * FOX_HEADS,), F32)]).reshape(1, LANES)
    return w_all, fb3


def _interleave_up(w_up):
    d = w_up.shape[0]
    halves = w_up.reshape(d, 2, D_FF // FFN_CHUNK, FFN_CHUNK)
    return jnp.swapaxes(halves, 1, 2).reshape(d, 2 * D_FF)


def kernel(x_prompt, x_sample, cache_fox_k, cache_fox_v, cache_fox_logf, state_hgrn, state_ffn_conv, norm_mix_pre, norm_mix_post, w_in, fox_f_bias, hgrn_lb_logits, hgrn_norm, w_branch_fox, w_branch_hgrn, w_out, norm_ffn_pre, norm_ffn_post, w_up, ffn_conv_w, ffn_conv_b, w_down):
    depth = w_in.shape[0]
    assert depth == 1 and hgrn_lb_logits.shape[0] == 2
    bp, seq, _ = x_prompt.shape
    assert bp == 1
    nb, t_new, _ = x_sample.shape
    past = cache_fox_k.shape[2]

    w_all, fb3 = _prep_w_in(w_in[0], fox_f_bias[0])
    g_pre = norm_mix_pre[0].reshape(1, D_MODEL)
    lbl = hgrn_lb_logits.astype(F32)
    hnorm = hgrn_norm[0].astype(F32).reshape(1, HG_DIM)
    w = {
        "bf": w_branch_fox[0].astype(BF16), "bh": w_branch_hgrn[0].astype(BF16), "out": w_out[0].astype(BF16),
        "up": _interleave_up(w_up[0]).astype(BF16), "down": w_down[0].astype(BF16),
        "npost": norm_mix_post[0].reshape(1, D_MODEL), "npre2": norm_ffn_pre[0].reshape(1, D_MODEL),
        "npost2": norm_ffn_post[0].reshape(1, D_MODEL),
        "conv_w": ffn_conv_w[0], "conv_b": ffn_conv_b[0].reshape(1, D_FF),
    }

    xp = x_prompt.reshape(seq, D_MODEL)
    (qt, kh, vt, stats, pk, pv, plf, hq, hk, lfh, hi, shg, sga, sgb) = _proj(xp, g_pre, w_all, fb3, lbl, fold=True)
    o_fox = _fox_prompt(qt, kh, vt, stats)
    s0 = jnp.zeros((1, HG_HEADS, HG_DIM, HG_DIM), F32)
    o_hg, p_state = _hgrn(hq, hk, lfh, hi, shg, s0, hnorm, nseq=1)
    hist0 = jnp.zeros((1, 2, D_FF), F32)
    yp, pconv = _mixffn(xp, o_fox, o_hg, sga, sgb, hist0, w, seg_len=seq)

    xs = x_sample.reshape(nb * t_new, D_MODEL)
    (qs, sk, sv, slf, hq, hk, lfh, hi, shg, sga, sgb) = _proj(xs, g_pre, w_all, fb3, lbl, fold=False)
    lf_all_t = jnp.concatenate([
        jnp.swapaxes(cache_fox_logf[0].astype(F32), 1, 2),
        jnp.swapaxes(slf.reshape(nb, t_new, FOX_HEADS), 1, 2),
        jnp.zeros((nb, FOX_HEADS, LANES - t_new), F32)], axis=2)
    cache_kt = jnp.transpose(cache_fox_k[0], (0, 2, 3, 1))
    cache_vt = jnp.transpose(cache_fox_v[0], (0, 2, 3, 1))
    o_fox_s = _fox_sample(qs, sk, sv, lf_all_t, cache_kt, cache_vt, t_new=t_new)
    o_hg_s, s_state = _hgrn(hq, hk, lfh, hi, shg, state_hgrn[0].astype(F32), hnorm, nseq=nb)
    ys, sconv = _mixffn(xs, o_fox_s, o_hg_s, sga, sgb, state_ffn_conv[0], w, seg_len=t_new)

    return (
        yp.reshape(bp, seq, D_MODEL),
        ys.reshape(nb, t_new, D_MODEL),
        pk.reshape(1, bp, seq, FOX_HEADS, FOX_HEAD_DIM),
        pv.reshape(1, bp, seq, FOX_HEADS, FOX_HEAD_DIM),
        plf.reshape(1, bp, seq, FOX_HEADS),
        p_state.reshape(1, bp, HG_HEADS, HG_DIM, HG_DIM),
        pconv.reshape(1, bp, 2, D_FF),
        sk.reshape(1, nb, t_new, FOX_HEADS, FOX_HEAD_DIM),
        sv.reshape(1, nb, t_new, FOX_HEADS, FOX_HEAD_DIM),
        slf.reshape(1, nb, t_new, FOX_HEADS),
        s_state.reshape(1, nb, HG_HEADS, HG_DIM, HG_DIM),
        sconv.reshape(1, nb, 2, D_FF),
    )
```

```python
import functools

import numpy as np
import jax
import jax.numpy as jnp
from jax import lax
from jax.experimental import pallas as pl
from jax.experimental.pallas import tpu as pltpu

F32 = jnp.float32
BF16 = jnp.bfloat16

D_MODEL = 1024
FOX_HEADS = 8
FOX_HEAD_DIM = 64
FOX_WIDTH = FOX_HEADS * FOX_HEAD_DIM
HG_HEADS = 4
HG_DIM = 128
HG_WIDTH = HG_HEADS * HG_DIM
D_FF = 2816
RMS_EPS = 1e-6
NEG_INF = -1e30
LOG2E = 1.4426950408889634
FOX_SKIP_NATS = 110.0
IN_SIZES = [FOX_WIDTH, FOX_WIDTH, FOX_WIDTH, FOX_HEADS, HG_WIDTH, HG_WIDTH, HG_WIDTH, HG_WIDTH, D_MODEL, D_MODEL]

LANES = 128
FOX_PAD = 2 * FOX_HEAD_DIM
FOX_V_ROWS = FOX_HEAD_DIM + 16
HG_BLOCK = 64
HG_SPLIT_MAX = 60.0
VMEM_LIMIT = 56 * 1024 * 1024

PROJ_TM = 256
FOX_TQ = 512
FOX_TK = 512
HG_TC = 256
FFN_TM = 256
FFN_CHUNK = 256
SAMPLE_TK = 1024

_C_Q, _C_K, _C_V, _C_F = 0, 512, 1024, 1536
_C_HQ, _C_HF, _C_HI, _C_HG = 1664, 2176, 2688, 3200
_C_GA, _C_GB, _C_END = 3712, 4736, 5760


def _split3(x):
    hi = x.astype(BF16)
    r = x - hi.astype(F32)
    mid = r.astype(BF16)
    lo = (r - mid.astype(F32)).astype(BF16)
    return hi, mid, lo


def _sum_by_01_matrix(mat01, x):
    cat = jnp.concatenate(_split3(x), axis=1)
    y = jnp.dot(mat01, cat, preferred_element_type=F32)
    return y[:, :LANES] + y[:, LANES:2 * LANES] + y[:, 2 * LANES:]


def _rms_scale(x):
    return x * lax.rsqrt(jnp.mean(x * x, axis=-1, keepdims=True) + RMS_EPS)


def _log_sigmoid(x):
    return jnp.minimum(x, 0.0) - jnp.log1p(jnp.exp(-jnp.abs(x)))


def _sigmoid(x):
    return 1.0 / (1.0 + jnp.exp(-x))


def _proj_kernel(*refs, fold):
    if fold:
        (x_ref, g_ref, w_ref, fb_ref, lbl_ref, tri_ref, pq_ref, pk_ref, cq_ref, ck_ref, cv_ref, seg_ref,
         qh_ref, kh_ref, vh_ref, stat_ref, kout_ref, vout_ref, lf_ref, hq_ref, hk_ref, lfh_ref, hi_ref, shg_ref,
         sga_ref, sgb_ref, carry_ref) = refs
    else:
        (x_ref, g_ref, w_ref, fb_ref, lbl_ref,
         qs_ref, kout_ref, vout_ref, lf_ref, hq_ref, hk_ref, lfh_ref, hi_ref, shg_ref,
         sga_ref, sgb_ref) = refs

    h = (_rms_scale(x_ref[...]) * g_ref[...]).astype(BF16)
    z = jnp.dot(h, w_ref[...], preferred_element_type=F32)

    zq = z[:, _C_Q:_C_K] * (FOX_HEAD_DIM ** -0.5)
    zk = z[:, _C_K:_C_V]
    zv = z[:, _C_V:_C_F]
    kout_ref[...] = zk
    vout_ref[...] = zv
    logf = _log_sigmoid(z[:, _C_F:_C_HQ] + fb_ref[...])
    lf_ref[...] = logf[:, :FOX_HEADS]

    l0 = lbl_ref[0:1, :]
    l1 = lbl_ref[1:2, :]
    lmax = jnp.maximum(l0, l1)
    e0 = jnp.exp(l0 - lmax)
    lb = e0 / (e0 + jnp.exp(l1 - lmax))
    f = lb + (1.0 - lb) * _sigmoid(z[:, _C_HF:_C_HI])
    hq_ref[...] = z[:, _C_HQ:_C_HF].astype(BF16)
    hk_ref[...] = (1.0 - f).astype(BF16)
    lfh_ref[...] = jnp.log(f)
    hi_ref[...] = z[:, _C_HI:_C_HG].astype(BF16)
    shg_ref[...] = _sigmoid(z[:, _C_HG:_C_GA]).astype(BF16)
    sga_ref[...] = _sigmoid(z[:, _C_GA:_C_GB]).astype(BF16)
    sgb_ref[...] = _sigmoid(z[:, _C_GB:_C_END]).astype(BF16)

    if not fold:
        qs_ref[...] = zq.astype(BF16)
        return

    @pl.when(pl.program_id(0) == 0)
    def _():
        carry_ref[...] = jnp.zeros_like(carry_ref)

    cum = carry_ref[...] + _sum_by_01_matrix(tri_ref[...], logf)
    carry_ref[...] = cum[-1:, :]

    seg = seg_ref[...]
    qn2 = jnp.dot((zq * zq).astype(BF16), seg, preferred_element_type=F32)
    kn2 = jnp.dot((zk * zk).astype(BF16), seg, preferred_element_type=F32)
    dg = jnp.dot((zq * zk).astype(BF16), seg, preferred_element_type=F32)
    stat_ref[0, 0:1, :] = jnp.max(qn2, axis=0, keepdims=True)
    stat_ref[0, 1:2, :] = jnp.max(kn2, axis=0, keepdims=True)
    stat_ref[0, 2:3, :] = jnp.min(dg, axis=0, keepdims=True)
    stat_ref[0, 3:4, :] = cum[0:1, :]
    stat_ref[0, 4:5, :] = cum[-1:, :]
    stat_ref[0, 5:8, :] = jnp.zeros((3, LANES), F32)

    zq = zq * LOG2E
    c_hi, c_mid, c_lo = _split3(cum * LOG2E)
    lane = lax.broadcasted_iota(jnp.int32, cum.shape, 1)
    pieces = jnp.where(lane < 8, c_hi, jnp.where(lane < 16, c_mid, c_lo))
    pieces = jnp.where(lane < 24, pieces, jnp.zeros_like(pieces))
    ex_q = jnp.dot(pieces, pq_ref[...], preferred_element_type=F32) + cq_ref[...]
    ex_k = jnp.dot(pieces, pk_ref[...], preferred_element_type=F32) + ck_ref[...]
    ex_v = cv_ref[...]

    low = lax.broadcasted_iota(jnp.int32, (zq.shape[0], LANES), 1) < FOX_HEAD_DIM
    for src, ex, dst, transposed in ((zq, ex_q, qh_ref, True), (zk, ex_k, kh_ref, False), (zv, ex_v, vh_ref, True)):
        for c in range(FOX_WIDTH // LANES):
            pair = src[:, c * LANES:(c + 1) * LANES]
            swapped = pltpu.roll(pair, FOX_HEAD_DIM, axis=1)
            for j, data in enumerate((pair, swapped)):
                hd = 2 * c + j
                blk = jnp.where(low, data, ex[:, hd * LANES:(hd + 1) * LANES])
                dst[hd] = (blk.T[:dst.shape[1]] if transposed else blk).astype(BF16)


def _bias_fold_constants():
    pq = np.zeros((LANES, FOX_HEADS * LANES), np.float32)
    pk = np.zeros((LANES, FOX_HEADS * LANES), np.float32)
    cq = np.zeros((1, FOX_HEADS * LANES), np.float32)
    ck = np.zeros((1, FOX_HEADS * LANES), np.float32)
    cv = np.zeros((1, FOX_HEADS * LANES), np.float32)
    for h in range(FOX_HEADS):
        base = h * LANES + FOX_HEAD_DIM
        for p in range(3):
            pq[p * 8 + h, base + p] = 1.0
            ck[0, base + p] = 1.0
            pk[p * 8 + h, base + 3 + p] = -1.0
            cq[0, base + 3 + p] = 1.0
        cv[0, base] = 1.0
    return (jnp.asarray(pq, BF16), jnp.asarray(pk, BF16), jnp.asarray(cq), jnp.asarray(ck), jnp.asarray(cv))


def _const_spec(shape, single=True):
    nd = len(shape)
    if single:
        return pl.BlockSpec(shape, lambda *_: (0,) * nd, pipeline_mode=pl.Buffered(1))
    return pl.BlockSpec(shape, lambda *_: (0,) * nd)


def _proj(x, gain, w_all, fb3, lb_logits, *, fold):
    rows = x.shape[0]
    tm = min(PROJ_TM, rows)
    n = rows // tm
    row = lambda width: pl.BlockSpec((tm, width), lambda i: (i, 0))
    in_specs = [row(D_MODEL), _const_spec((1, D_MODEL)), _const_spec(w_all.shape), _const_spec((1, LANES)),
                _const_spec(lb_logits.shape)]
    args = [x, gain, w_all, fb3, lb_logits]
    common_out = [
        (jax.ShapeDtypeStruct((rows, FOX_WIDTH), F32), row(FOX_WIDTH)),
        (jax.ShapeDtypeStruct((rows, FOX_WIDTH), F32), row(FOX_WIDTH)),
        (jax.ShapeDtypeStruct((rows, FOX_HEADS), F32), row(FOX_HEADS)),
        (jax.ShapeDtypeStruct((rows, HG_WIDTH), BF16), row(HG_WIDTH)),
        (jax.ShapeDtypeStruct((rows, HG_WIDTH), BF16), row(HG_WIDTH)),
        (jax.ShapeDtypeStruct((rows, HG_WIDTH), F32), row(HG_WIDTH)),
        (jax.ShapeDtypeStruct((rows, HG_WIDTH), BF16), row(HG_WIDTH)),
        (jax.ShapeDtypeStruct((rows, HG_WIDTH), BF16), row(HG_WIDTH)),
        (jax.ShapeDtypeStruct((rows, D_MODEL), BF16), row(D_MODEL)),
        (jax.ShapeDtypeStruct((rows, D_MODEL), BF16), row(D_MODEL)),
    ]
    scratch = []
    if fold:
        tri = jnp.asarray(np.tril(np.ones((tm, tm), np.float32)), BF16)
        seg = np.zeros((FOX_WIDTH, LANES), np.float32)
        seg[np.arange(FOX_WIDTH), np.arange(FOX_WIDTH) // FOX_HEAD_DIM] = 1.0
        consts = _bias_fold_constants() + (jnp.asarray(seg, BF16),)
        in_specs += [_const_spec(tri.shape)] + [_const_spec(c.shape) for c in consts]
        args += [tri, *consts]
        head_major = (jax.ShapeDtypeStruct((FOX_HEADS, rows, FOX_PAD), BF16),
                      pl.BlockSpec((FOX_HEADS, tm, FOX_PAD), lambda i: (0, i, 0)))
        head_major_t = lambda depth: (jax.ShapeDtypeStruct((FOX_HEADS, depth, rows), BF16),
                                      pl.BlockSpec((FOX_HEADS, depth, tm), lambda i: (0, 0, i)))
        stats = (jax.ShapeDtypeStruct((n, 8, LANES), F32), pl.BlockSpec((1, 8, LANES), lambda i: (i, 0, 0)))
        outs = [head_major_t(FOX_PAD), head_major, head_major_t(FOX_V_ROWS), stats] + common_out
        scratch = [pltpu.VMEM((1, LANES), F32)]
    else:
        outs = [(jax.ShapeDtypeStruct((rows, FOX_WIDTH), BF16), row(FOX_WIDTH))] + common_out
    return pl.pallas_call(
        functools.partial(_proj_kernel, fold=fold),
        grid=(n,),
        in_specs=in_specs,
        out_specs=[o[1] for o in outs],
        out_shape=[o[0] for o in outs],
        scratch_shapes=scratch,
        compiler_params=pltpu.CompilerParams(dimension_semantics=("arbitrary",), vmem_limit_bytes=VMEM_LIMIT),
        name="proj_fold" if fold else "proj",
    )(*args)


def _fox_kernel(qi_ref, ki_ref, mode_ref, qt_ref, k_ref, vt_ref, o_ref, m_ref, acc_ref):
    mode = mode_ref[pl.program_id(0)]
    tq = qt_ref.shape[2]
    tk = k_ref.shape[1]

    @pl.when((mode & _MODE_FIRST) != 0)
    def _():
        m_ref[...] = jnp.full_like(m_ref, NEG_INF)
        acc_ref[...] = jnp.zeros_like(acc_ref)

    def logits(h):
        return jnp.dot(k_ref[h], qt_ref[h], preferred_element_type=F32)

    def softmax_update(h, s, masked):
        if masked:
            key = lax.broadcasted_iota(jnp.int32, (tk, tq), 0)
            qry = lax.broadcasted_iota(jnp.int32, (tk, tq), 1)
            s = jnp.where(key <= qry, s, NEG_INF)
        m_prev = m_ref[h]
        m_new = jnp.maximum(m_prev, jnp.max(s, axis=0, keepdims=True))
        alpha = jnp.exp2(m_prev - m_new)
        p = jnp.exp2(s - m_new).astype(BF16)
        acc_ref[h] = alpha * acc_ref[h] + jnp.dot(vt_ref[h], p, preferred_element_type=F32)
        m_ref[h] = m_new

    def all_heads(masked):
        s_next = logits(0)
        for h in range(FOX_HEADS):
            s = s_next
            if h + 1 < FOX_HEADS:
                s_next = logits(h + 1)
            softmax_update(h, s, masked)

    @pl.when((mode & _MODE_PAIR) != 0)
    def _():
        all_heads(False)

    @pl.when((mode & _MODE_DIAG) != 0)
    def _():
        all_heads(True)
        for c in range(FOX_WIDTH // LANES):
            halves = []
            for hd in (2 * c, 2 * c + 1):
                acc = acc_ref[hd]
                halves.append(acc[:FOX_HEAD_DIM, :] / acc[FOX_HEAD_DIM:FOX_HEAD_DIM + 1, :])
            o_ref[:, c * LANES:(c + 1) * LANES] = jnp.concatenate(halves, axis=0).T.astype(o_ref.dtype)


_MODE_FIRST, _MODE_PAIR, _MODE_DIAG = 1, 2, 4


def _fox_schedule(stats, nq):
    st = stats[:, :5, :FOX_HEADS].reshape(nq, -1, 5, FOX_HEADS)
    qn = jnp.sqrt(jnp.max(st[:, :, 0], axis=1)) * 1.01
    kn = jnp.sqrt(jnp.max(st[:, :, 1], axis=1)) * 1.01
    dmin = jnp.min(st[:, :, 2], axis=1)
    c_first = st[:, 0, 3]
    c_last = st[:, -1, 4]
    bound = qn[:, None] * kn[None, :] + (c_first - dmin)[:, None] - c_last[None, :]
    blk = jnp.arange(nq, dtype=jnp.int32)
    drop = jnp.all(bound < -FOX_SKIP_NATS, axis=-1) & (blk[None, :] < blk[:, None])
    prefix = jnp.cumsum(jnp.logical_not(drop).astype(jnp.int32), axis=1) == 0
    kstart = jnp.sum(prefix.astype(jnp.int32), axis=1)
    count = blk - kstart + 1
    ends = jnp.cumsum(count)
    nsteps = nq * (nq + 1) // 2
    step = jnp.arange(nsteps, dtype=jnp.int32)
    valid = step < ends[-1]
    done = ends[None, :] <= step[:, None]
    q_of = jnp.minimum(jnp.sum(done.astype(jnp.int32), axis=1), nq - 1)
    begin = jnp.sum(jnp.where(done, count[None, :], 0), axis=1)
    k_first = jnp.sum(jnp.where(blk[None, :] == q_of[:, None], kstart[None, :], 0), axis=1)
    k_of = k_first + (step - begin)
    mode = jnp.where(k_of == q_of, _MODE_DIAG, _MODE_PAIR) + jnp.where(k_of == k_first, _MODE_FIRST, 0)
    mode = jnp.where(valid, mode, 0)
    q_of = jnp.where(valid, q_of, nq - 1)
    k_of = jnp.where(valid, k_of, nq - 1)
    return q_of.astype(jnp.int32), k_of.astype(jnp.int32), mode.astype(jnp.int32)


def _fox_prompt(qt, kh, vt, stats):
    seq = kh.shape[1]
    assert FOX_TQ == FOX_TK
    nq = seq // FOX_TQ
    qi, ki, mode = _fox_schedule(stats, nq)
    grid_spec = pltpu.PrefetchScalarGridSpec(
        num_scalar_prefetch=3,
        grid=(nq * (nq + 1) // 2,),
        in_specs=[
            pl.BlockSpec((FOX_HEADS, FOX_PAD, FOX_TQ), lambda s, qi, ki, mode: (0, 0, qi[s])),
            pl.BlockSpec((FOX_HEADS, FOX_TK, FOX_PAD), lambda s, qi, ki, mode: (0, ki[s], 0)),
            pl.BlockSpec((FOX_HEADS, FOX_V_ROWS, FOX_TK), lambda s, qi, ki, mode: (0, 0, ki[s])),
        ],
        out_specs=pl.BlockSpec((FOX_TQ, FOX_WIDTH), lambda s, qi, ki, mode: (qi[s], 0)),
        scratch_shapes=[pltpu.VMEM((FOX_HEADS, 1, FOX_TQ), F32),
                        pltpu.VMEM((FOX_HEADS, FOX_V_ROWS, FOX_TQ), F32)],
    )
    return pl.pallas_call(
        _fox_kernel,
        grid_spec=grid_spec,
        out_shape=jax.ShapeDtypeStruct((seq, FOX_WIDTH), BF16),
        compiler_params=pltpu.CompilerParams(dimension_semantics=("arbitrary",), vmem_limit_bytes=VMEM_LIMIT),
        name="fox_prompt",
    )(qi, ki, mode, qt, kh, vt)


def _lane_cumsum(x):
    n = x.shape[1]
    lane = lax.broadcasted_iota(jnp.int32, x.shape, 1)
    shift = 1
    while shift < n:
        x = x + jnp.where(lane >= shift, pltpu.roll(x, shift, axis=1), 0.0)
        shift *= 2
    return x


def _fox_sample_kernel(q_ref, kn_ref, vn_ref, lft_ref, ck_ref, cv_ref, o_ref,
                       cq_ref, cum_ref, m_ref, l_ref, acc_ref, *, n_past_chunks, tk, t_new):
    c = pl.program_id(1)
    rows = FOX_HEADS * t_new
    heads = [slice(h * FOX_HEAD_DIM, (h + 1) * FOX_HEAD_DIM) for h in range(FOX_HEADS)]
    head_rows = [slice(h * t_new, (h + 1) * t_new) for h in range(FOX_HEADS)]

    @pl.when(c == 0)
    def _():
        cum = _lane_cumsum(lft_ref[0])
        for j in range(n_past_chunks):
            cum_ref[j] = cum[:, j * tk:(j + 1) * tk]
        new_cum = cum[:, n_past_chunks * tk:n_past_chunks * tk + LANES]
        cum_ref[n_past_chunks, :, :LANES] = new_cum
        new_cum_t = jnp.concatenate([new_cum] * (LANES // FOX_HEADS), axis=0).T
        for h in range(FOX_HEADS):
            cq_ref[head_rows[h], :] = jnp.broadcast_to(new_cum_t[:t_new, h:h + 1], (t_new, LANES))
        m_ref[...] = jnp.full_like(m_ref, NEG_INF)
        l_ref[...] = jnp.zeros_like(l_ref)
        acc_ref[...] = jnp.zeros_like(acc_ref)

    nt = (((1,), (1,)), ((), ()))

    def update(qk, pv_of, ck_rows, mask):
        q = q_ref[...]
        s = jnp.concatenate([qk(q[:, heads[h]], h) for h in range(FOX_HEADS)], axis=0)
        s = s + (cq_ref[:, :1] - ck_rows)
        if mask is not None:
            s = jnp.where(mask, s, NEG_INF)
        m_prev = m_ref[...]
        m_new = jnp.maximum(m_prev, jnp.max(s, axis=1, keepdims=True))
        alpha = jnp.exp(m_prev - m_new)
        p = jnp.exp(s - m_new[:, :1])
        l_ref[...] = alpha * l_ref[...] + jnp.sum(p, axis=1, keepdims=True)
        p = p.astype(BF16)
        pv = jnp.concatenate([pv_of(p[head_rows[h]], h) for h in range(FOX_HEADS)], axis=0)
        acc_ref[...] = alpha[:, :FOX_HEAD_DIM] * acc_ref[...] + pv
        m_ref[...] = m_new

    def expand_rows(x, width):
        return jnp.concatenate([jnp.broadcast_to(x[h:h + 1, :], (t_new, width)) for h in range(FOX_HEADS)], axis=0)

    @pl.when(c < n_past_chunks)
    def _():
        update(lambda qh, h: jnp.dot(qh, ck_ref[0, h].astype(BF16), preferred_element_type=F32),
               lambda ph, h: lax.dot_general(ph, cv_ref[0, h].astype(BF16), nt, preferred_element_type=F32),
               expand_rows(cum_ref[c], tk), None)

    @pl.when(c == n_past_chunks)
    def _():
        ck_rows = expand_rows(cum_ref[n_past_chunks, :, :LANES], LANES)[:, :t_new]
        rowt = lax.broadcasted_iota(jnp.int32, (rows, t_new), 0) % t_new
        coli = lax.broadcasted_iota(jnp.int32, (rows, t_new), 1)
        update(lambda qh, h: lax.dot_general(qh, kn_ref[:, heads[h]].astype(BF16), nt, preferred_element_type=F32),
               lambda ph, h: jnp.dot(ph, vn_ref[:, heads[h]].astype(BF16), preferred_element_type=F32),
               ck_rows, coli <= rowt)
        out = acc_ref[...] / l_ref[:, :FOX_HEAD_DIM]
        o_ref[...] = jnp.concatenate([out[head_rows[h]] for h in range(FOX_HEADS)], axis=1).astype(o_ref.dtype)


def _fox_sample(qs, k_new, v_new, lf_all_t, cache_k, cache_v, *, t_new):
    nb, past = cache_k.shape[0], cache_k.shape[3]
    tk = SAMPLE_TK
    npc = past // tk
    rows = FOX_HEADS * t_new
    last = npc - 1
    kern = functools.partial(_fox_sample_kernel, n_past_chunks=npc, tk=tk, t_new=t_new)
    cache_spec = pl.BlockSpec((1, FOX_HEADS, FOX_HEAD_DIM, tk), lambda b, c: (b, 0, 0, jnp.minimum(c, last)))
    return pl.pallas_call(
        kern,
        grid=(nb, npc + 1),
        in_specs=[
            pl.BlockSpec((t_new, FOX_WIDTH), lambda b, c: (b, 0)),
            pl.BlockSpec((t_new, FOX_WIDTH), lambda b, c: (b, 0)),
            pl.BlockSpec((t_new, FOX_WIDTH), lambda b, c: (b, 0)),
            pl.BlockSpec((1, FOX_HEADS, past + LANES), lambda b, c: (b, 0, 0)),
            cache_spec, cache_spec,
        ],
        out_specs=pl.BlockSpec((t_new, FOX_WIDTH), lambda b, c: (b, 0)),
        out_shape=jax.ShapeDtypeStruct((nb * t_new, FOX_WIDTH), BF16),
        scratch_shapes=[
            pltpu.VMEM((rows, LANES), F32),
            pltpu.VMEM((npc + 1, FOX_HEADS, tk), F32),
            pltpu.VMEM((rows, LANES), F32),
            pltpu.VMEM((rows, LANES), F32),
            pltpu.VMEM((rows, FOX_HEAD_DIM), F32),
        ],
        compiler_params=pltpu.CompilerParams(dimension_semantics=("arbitrary", "arbitrary"),
                                             vmem_limit_bytes=VMEM_LIMIT),
        name="fox_sample",
    )(qs, k_new, v_new, lf_all_t, cache_k, cache_v)


def _hgrn_kernel(hq_ref, hk_ref, lfh_ref, hi_ref, shg_ref, s0_ref, norm_ref, tri_ref,
                 o_ref, sout_ref, st_ref, qq_ref, kk_ref, eb_ref, b_ref, oin_ref, od_ref):
    t = pl.program_id(1)
    tc = hq_ref.shape[0]
    blk = min(HG_BLOCK, tc)
    nblk = tc // blk
    heads = [slice(h * HG_DIM, (h + 1) * HG_DIM) for h in range(HG_HEADS)]

    @pl.when(t == 0)
    def _():
        for h in range(HG_HEADS):
            st_ref[h] = s0_ref[0, h].T

    def at_block_row(x, r):
        x3 = x.reshape(nblk, blk, HG_DIM)
        return jnp.broadcast_to(x3[:, r:r + 1, :], x3.shape).reshape(tc, HG_DIM)

    def rel_to_middle(b):
        return b - at_block_row(b, blk // 2 - 1)

    worst = jnp.zeros((), F32)
    for sl in heads:
        b = _sum_by_01_matrix(tri_ref[...], lfh_ref[:, sl])
        eb = jnp.exp(b)
        qq_ref[:, sl] = (hq_ref[:, sl].astype(F32) * eb).astype(BF16)
        kk_ref[:, sl] = (hk_ref[:, sl].astype(F32) * jnp.exp(at_block_row(b, blk - 1) - b)).astype(BF16)
        eb_ref[:, sl] = eb
        b_ref[:, sl] = b
        worst = jnp.maximum(worst, jnp.max(jnp.abs(rel_to_middle(b))))

    def carried_state_and_output():
        for j in range(nblk):
            rows = slice(j * blk, (j + 1) * blk)
            for h, sl in enumerate(heads):
                st = st_ref[h]
                oin_ref[rows, sl] = lax.dot_general(qq_ref[rows, sl], st.astype(BF16), (((1,), (1,)), ((), ())),
                                                    preferred_element_type=F32)
                upd = lax.dot_general(hi_ref[rows, sl], kk_ref[rows, sl], (((0,), (0,)), ((), ())),
                                      preferred_element_type=F32)
                st_ref[h] = st * eb_ref[(j + 1) * blk - 1:(j + 1) * blk, sl] + upd
        for sl in heads:
            o = oin_ref[:, sl] + od_ref[:, sl]
            y = (_rms_scale(o) * norm_ref[...]) * shg_ref[:, sl].astype(F32)
            o_ref[:, sl] = y.astype(o_ref.dtype)

    splittable = worst <= HG_SPLIT_MAX

    @pl.when(splittable)
    def _():
        row = lax.broadcasted_iota(jnp.int32, (tc, tc), 0)
        col = lax.broadcasted_iota(jnp.int32, (tc, tc), 1)
        pair_in_block = (row // blk == col // blk) & (col <= row)
        for sl in heads:
            b_rel = rel_to_middle(b_ref[:, sl])
            qs = (hq_ref[:, sl].astype(F32) * jnp.exp(b_rel)).astype(BF16)
            ks = (hk_ref[:, sl].astype(F32) * jnp.exp(-b_rel)).astype(BF16)
            a = lax.dot_general(qs, ks, (((1,), (1,)), ((), ())), preferred_element_type=F32)
            a = jnp.where(pair_in_block, a, 0.0).astype(BF16)
            od_ref[:, sl] = jnp.dot(a, hi_ref[:, sl], preferred_element_type=F32)
        carried_state_and_output()

    @pl.when(jnp.logical_not(splittable))
    def _():
        row_in_blk = lax.broadcasted_iota(jnp.int32, (tc, HG_DIM), 0) % blk
        for sl in heads:
            q = hq_ref[:, sl].astype(F32)
            k = hk_ref[:, sl].astype(F32)
            v = hi_ref[:, sl].astype(F32)
            b = b_ref[:, sl]

            def lag_step(lag, od):
                k_l = pltpu.roll(k, lag, axis=0)
                b_l = pltpu.roll(b, lag, axis=0)
                v_l = pltpu.roll(v, lag, axis=0)
                w = q * k_l * jnp.exp(jnp.minimum(b - b_l, 0.0))
                w = jnp.where(row_in_blk >= lag, w, 0.0)
                return od + jnp.sum(w, axis=1, keepdims=True) * v_l

            od_ref[:, sl] = lax.fori_loop(1, blk, lag_step, jnp.sum(q * k, axis=1, keepdims=True) * v)
        carried_state_and_output()

    @pl.when(t == pl.num_programs(1) - 1)
    def _():
        for h in range(HG_HEADS):
            sout_ref[0, h] = st_ref[h].T


def _hgrn_cumsum_matrix(n, blk):
    t = np.arange(n)[:, None]
    s = np.arange(n)[None, :]
    return jnp.asarray((((t // blk) == (s // blk)) & (s <= t)).astype(np.float32), BF16)


def _hgrn(hq, hk, lfh, hi, shg, s0, norm, *, nseq):
    rows = hq.shape[0]
    t_len = rows // nseq
    tc = min(HG_TC, t_len)
    nt = t_len // tc
    blk = min(HG_BLOCK, tc)
    tri = _hgrn_cumsum_matrix(tc, blk)
    row = pl.BlockSpec((tc, HG_WIDTH), lambda b, t: (b * nt + t, 0))
    state = pl.BlockSpec((1, HG_HEADS, HG_DIM, HG_DIM), lambda b, t: (b, 0, 0, 0))
    return pl.pallas_call(
        _hgrn_kernel,
        grid=(nseq, nt),
        in_specs=[row, row, row, row, row, state,
                  pl.BlockSpec((1, HG_DIM), lambda b, t: (0, 0)),
                  pl.BlockSpec((tc, tc), lambda b, t: (0, 0))],
        out_specs=[row, state],
        out_shape=[jax.ShapeDtypeStruct((rows, HG_WIDTH), BF16),
                   jax.ShapeDtypeStruct((nseq, HG_HEADS, HG_DIM, HG_DIM), F32)],
        scratch_shapes=[
            pltpu.VMEM((HG_HEADS, HG_DIM, HG_DIM), F32),
            pltpu.VMEM((tc, HG_WIDTH), BF16),
            pltpu.VMEM((tc, HG_WIDTH), BF16),
            pltpu.VMEM((tc, HG_WIDTH), F32),
            pltpu.VMEM((tc, HG_WIDTH), F32),
            pltpu.VMEM((tc, HG_WIDTH), F32),
            pltpu.VMEM((tc, HG_WIDTH), F32),
        ],
        compiler_params=pltpu.CompilerParams(dimension_semantics=("arbitrary", "arbitrary"),
                                             vmem_limit_bytes=VMEM_LIMIT),
        name="hgrn",
    )(hq, hk, lfh, hi, shg, s0, norm, tri)


def _mixffn_kernel(x_ref, of_ref, oh_ref, sga_ref, sgb_ref, hist_ref,
                   wbf_ref, wbh_ref, wout_ref, wup_ref, wdn_ref,
                   npost_ref, npre2_ref, npost2_ref, cw_ref, cb_ref,
                   y_ref, conv_ref, tail_ref, *, seg_len):
    tm = x_ref.shape[0]
    br_f = jnp.dot(of_ref[...], wbf_ref[...], preferred_element_type=F32)
    br_h = jnp.dot(oh_ref[...], wbh_ref[...], preferred_element_type=F32)
    merged = sga_ref[...].astype(F32) * br_f + sgb_ref[...].astype(F32) * br_h
    mix = jnp.dot(merged.astype(BF16), wout_ref[...], preferred_element_type=F32)
    x1 = x_ref[...] + _rms_scale(mix) * npost_ref[...]

    h2 = (_rms_scale(x1) * npre2_ref[...]).astype(BF16)
    def up_chunk(j):
        w = jnp.concatenate([wup_ref[:, j * FFN_CHUNK:(j + 1) * FFN_CHUNK],
                             wup_ref[:, D_FF + j * FFN_CHUNK:D_FF + (j + 1) * FFN_CHUNK]], axis=1)
        return jnp.dot(h2, w, preferred_element_type=F32)

    n_chunks = D_FF // FFN_CHUNK
    carried = seg_len >= tm
    if carried:
        @pl.when(pl.program_id(0) == 0)
        def _():
            tail_ref[...] = hist_ref[0]
    rowi = lax.broadcasted_iota(jnp.int32, (tm, FFN_CHUNK), 0)
    ff = None
    up_next = up_chunk(0)
    for j in range(n_chunks):
        cols = slice(j * FFN_CHUNK, (j + 1) * FFN_CHUNK)
        up = up_next
        if j + 1 < n_chunks:
            up_next = up_chunk(j + 1)
        a = up[:, :FFN_CHUNK]
        g = up[:, FFN_CHUNK:]
        prev1 = pltpu.roll(a, 1, axis=0)
        prev2 = pltpu.roll(a, 2, axis=0)
        if carried:
            t0 = tail_ref[0:1, cols]
            t1 = tail_ref[1:2, cols]
            prev1 = jnp.where(rowi == 0, t1, prev1)
            prev2 = jnp.where(rowi == 0, t0, jnp.where(rowi == 1, t1, prev2))
            tail_ref[:, cols] = a[tm - 2:, :]
            conv_ref[0, :, cols] = a[tm - 2:, :]
        else:
            for s in range(tm // seg_len):
                h0 = hist_ref[s, 0:1, cols]
                h1 = hist_ref[s, 1:2, cols]
                prev1 = jnp.where(rowi == s * seg_len, h1, prev1)
                prev2 = jnp.where(rowi == s * seg_len, h0, jnp.where(rowi == s * seg_len + 1, h1, prev2))
                conv_ref[s, :, cols] = a[(s + 1) * seg_len - 2:(s + 1) * seg_len, :]
        c = cb_ref[:, cols] + cw_ref[0:1, cols] * prev2 + cw_ref[1:2, cols] * prev1 + cw_ref[2:3, cols] * a
        act = (jax.nn.gelu(c, approximate=True) * g).astype(BF16)
        part = jnp.dot(act, wdn_ref[cols, :], preferred_element_type=F32)
        ff = part if ff is None else ff + part
    y_ref[...] = x1 + _rms_scale(ff) * npost2_ref[...]


def _mixffn(x, o_fox, o_hg, sga, sgb, hist, w, *, seg_len):
    rows = x.shape[0]
    tm = min(FFN_TM, rows)
    n = rows // tm
    nseg = hist.shape[0]
    row = lambda width: pl.BlockSpec((tm, width), lambda i: (i, 0))
    weights = [w["bf"], w["bh"], w["out"], w["up"], w["down"]]
    smalls = [w["npost"], w["npre2"], w["npost2"], w["conv_w"], w["conv_b"]]
    hist_spec = pl.BlockSpec(hist.shape, lambda i: (0, 0, 0))
    scratch = [pltpu.VMEM((2, D_FF), F32)]
    return pl.pallas_call(
        functools.partial(_mixffn_kernel, seg_len=seg_len),
        grid=(n,),
        in_specs=[row(D_MODEL), row(FOX_WIDTH), row(HG_WIDTH), row(D_MODEL), row(D_MODEL), hist_spec]
                 + [_const_spec(a.shape) for a in weights] + [_const_spec(a.shape) for a in smalls],
        out_specs=[row(D_MODEL), pl.BlockSpec((nseg, 2, D_FF), lambda i: (0, 0, 0))],
        out_shape=[jax.ShapeDtypeStruct((rows, D_MODEL), F32), jax.ShapeDtypeStruct((nseg, 2, D_FF), F32)],
        scratch_shapes=scratch,
        compiler_params=pltpu.CompilerParams(dimension_semantics=("arbitrary",), vmem_limit_bytes=VMEM_LIMIT),
        name="mixffn",
    )(x, o_fox, o_hg, sga, sgb, hist, *weights, *smalls)


def _prep_w_in(w_in, fox_f_bias):
    offs = np.cumsum([0] + IN_SIZES)
    seg = [w_in[:, int(offs[i]):int(offs[i + 1])] for i in range(len(IN_SIZES))]
    pad = jnp.zeros((D_MODEL, LANES - 3 * FOX_HEADS), w_in.dtype)
    f3 = jnp.concatenate([seg[3], seg[3], seg[3], pad], axis=1)
    w_all = jnp.concatenate(seg[:3] + [f3] + seg[4:], axis=1).astype(BF16)
    fb = fox_f_bias.astype(F32)
    fb3 = jnp.concatenate([fb, fb, fb, jnp.zeros((LANES - 3 * FOX_HEADS,), F32)]).reshape(1, LANES)
    return w_all, fb3


def kernel(x_prompt, x_sample, cache_fox_k, cache_fox_v, cache_fox_logf, state_hgrn, state_ffn_conv, norm_mix_pre, norm_mix_post, w_in, fox_f_bias, hgrn_lb_logits, hgrn_norm, w_branch_fox, w_branch_hgrn, w_out, norm_ffn_pre, norm_ffn_post, w_up, ffn_conv_w, ffn_conv_b, w_down):
    depth = w_in.shape[0]
    assert depth == 1 and hgrn_lb_logits.shape[0] == 2
    bp, seq, _ = x_prompt.shape
    assert bp == 1
    nb, t_new, _ = x_sample.shape
    past = cache_fox_k.shape[2]

    w_all, fb3 = _prep_w_in(w_in[0], fox_f_bias[0])
    g_pre = norm_mix_pre[0].reshape(1, D_MODEL)
    lbl = hgrn_lb_logits.astype(F32)
    hnorm = hgrn_norm[0].astype(F32).reshape(1, HG_DIM)
    w = {
        "bf": w_branch_fox[0].astype(BF16), "bh": w_branch_hgrn[0].astype(BF16), "out": w_out[0].astype(BF16),
        "up": w_up[0].astype(BF16), "down": w_down[0].astype(BF16),
        "npost": norm_mix_post[0].reshape(1, D_MODEL), "npre2": norm_ffn_pre[0].reshape(1, D_MODEL),
        "npost2": norm_ffn_post[0].reshape(1, D_MODEL),
        "conv_w": ffn_conv_w[0], "conv_b": ffn_conv_b[0].reshape(1, D_FF),
    }

    xp = x_prompt.reshape(seq, D_MODEL)
    (qt, kh, vt, stats, pk, pv, plf, hq, hk, lfh, hi, shg, sga, sgb) = _proj(xp, g_pre, w_all, fb3, lbl, fold=True)
    o_fox = _fox_prompt(qt, kh, vt, stats)
    s0 = jnp.zeros((1, HG_HEADS, HG_DIM, HG_DIM), F32)
    o_hg, p_state = _hgrn(hq, hk, lfh, hi, shg, s0, hnorm, nseq=1)
    hist0 = jnp.zeros((1, 2, D_FF), F32)
    yp, pconv = _mixffn(xp, o_fox, o_hg, sga, sgb, hist0, w, seg_len=seq)

    xs = x_sample.reshape(nb * t_new, D_MODEL)
    (qs, sk, sv, slf, hq, hk, lfh, hi, shg, sga, sgb) = _proj(xs, g_pre, w_all, fb3, lbl, fold=False)
    lf_all_t = jnp.concatenate([
        jnp.swapaxes(cache_fox_logf[0].astype(F32), 1, 2),
        jnp.swapaxes(slf.reshape(nb, t_new, FOX_HEADS), 1, 2),
        jnp.zeros((nb, FOX_HEADS, LANES - t_new), F32)], axis=2)
    cache_kt = jnp.transpose(cache_fox_k[0], (0, 2, 3, 1))
    cache_vt = jnp.transpose(cache_fox_v[0], (0, 2, 3, 1))
    o_fox_s = _fox_sample(qs, sk, sv, lf_all_t, cache_kt, cache_vt, t_new=t_new)
    o_hg_s, s_state = _hgrn(hq, hk, lfh, hi, shg, state_hgrn[0].astype(F32), hnorm, nseq=nb)
    ys, sconv = _mixffn(xs, o_fox_s, o_hg_s, sga, sgb, state_ffn_conv[0], w, seg_len=t_new)

    return (
        yp.reshape(bp, seq, D_MODEL),
        ys.reshape(nb, t_new, D_MODEL),
        pk.reshape(1, bp, seq, FOX_HEADS, FOX_HEAD_DIM),
        pv.reshape(1, bp, seq, FOX_HEADS, FOX_HEAD_DIM),
        plf.reshape(1, bp, seq, FOX_HEADS),
        p_state.reshape(1, bp, HG_HEADS, HG_DIM, HG_DIM),
        pconv.reshape(1, bp, 2, D_FF),
        sk.reshape(1, nb, t_new, FOX_HEADS, FOX_HEAD_DIM),
        sv.reshape(1, nb, t_new, FOX_HEADS, FOX_HEAD_DIM),
        slf.reshape(1, nb, t_new, FOX_HEADS),
        s_state.reshape(1, nb, HG_HEADS, HG_DIM, HG_DIM),
        sconv.reshape(1, nb, 2, D_FF),
    )
```

```python
import functools

import numpy as np
import jax
import jax.numpy as jnp
from jax import lax
from jax.experimental import pallas as pl
from jax.experimental.pallas import tpu as pltpu

F32 = jnp.float32
BF16 = jnp.bfloat16

D_MODEL = 1024
FOX_HEADS = 8
FOX_HEAD_DIM = 64
FOX_WIDTH = FOX_HEADS * FOX_HEAD_DIM
HG_HEADS = 4
HG_DIM = 128
HG_WIDTH = HG_HEADS * HG_DIM
D_FF = 2816
RMS_EPS = 1e-6
NEG_INF = -1e30
LOG2E = 1.4426950408889634
FOX_SKIP_NATS = 110.0
IN_SIZES = [FOX_WIDTH, FOX_WIDTH, FOX_WIDTH, FOX_HEADS, HG_WIDTH, HG_WIDTH, HG_WIDTH, HG_WIDTH, D_MODEL, D_MODEL]

LANES = 128
FOX_PAD = 2 * FOX_HEAD_DIM
FOX_V_ROWS = FOX_HEAD_DIM + 16
HG_BLOCK = 64
HG_SPLIT_MAX = 60.0
VMEM_LIMIT = 56 * 1024 * 1024

PROJ_TM = 256
FOX_TQ = 512
FOX_TK = 512
HG_TC = 256
FFN_TM = 256
FFN_CHUNK = 256
SAMPLE_TK = 1024

_C_Q, _C_K, _C_V, _C_F = 0, 512, 1024, 1536
_C_HQ, _C_HF, _C_HI, _C_HG = 1664, 2176, 2688, 3200
_C_GA, _C_GB, _C_END = 3712, 4736, 5760


def _split3(x):
    hi = x.astype(BF16)
    r = x - hi.astype(F32)
    mid = r.astype(BF16)
    lo = (r - mid.astype(F32)).astype(BF16)
    return hi, mid, lo


def _sum_by_01_matrix(mat01, x):
    cat = jnp.concatenate(_split3(x), axis=1)
    y = jnp.dot(mat01, cat, preferred_element_type=F32)
    return y[:, :LANES] + y[:, LANES:2 * LANES] + y[:, 2 * LANES:]


def _rms_scale(x):
    return x * lax.rsqrt(jnp.mean(x * x, axis=-1, keepdims=True) + RMS_EPS)


def _log_sigmoid(x):
    return jnp.minimum(x, 0.0) - jnp.log1p(jnp.exp(-jnp.abs(x)))


def _sigmoid(x):
    return 1.0 / (1.0 + jnp.exp(-x))


def _proj_kernel(*refs, fold):
    if fold:
        (x_ref, g_ref, w_ref, fb_ref, lbl_ref, tri_ref, pq_ref, pk_ref, cq_ref, ck_ref, cv_ref, seg_ref,
         qh_ref, kh_ref, vh_ref, stat_ref, kout_ref, vout_ref, lf_ref, hq_ref, hk_ref, lfh_ref, hi_ref, hg_ref,
         ga_ref, gb_ref, carry_ref) = refs
    else:
        (x_ref, g_ref, w_ref, fb_ref, lbl_ref,
         qs_ref, kout_ref, vout_ref, lf_ref, hq_ref, hk_ref, lfh_ref, hi_ref, hg_ref,
         ga_ref, gb_ref) = refs

    h = (_rms_scale(x_ref[...]) * g_ref[...]).astype(BF16)

    z = jnp.dot(h, w_ref[...], preferred_element_type=F32)

    zq = z[:, _C_Q:_C_K] * (FOX_HEAD_DIM ** -0.5)
    zk = z[:, _C_K:_C_V]
    zv = z[:, _C_V:_C_F]
    kout_ref[...] = zk
    vout_ref[...] = zv
    logf = _log_sigmoid(z[:, _C_F:_C_HQ] + fb_ref[...])
    lf_ref[...] = logf[:, :FOX_HEADS]

    l0 = lbl_ref[0:1, :]
    l1 = lbl_ref[1:2, :]
    lmax = jnp.maximum(l0, l1)
    e0 = jnp.exp(l0 - lmax)
    lb = e0 / (e0 + jnp.exp(l1 - lmax))
    f = lb + (1.0 - lb) * _sigmoid(z[:, _C_HF:_C_HI])
    hq_ref[...] = z[:, _C_HQ:_C_HF].astype(BF16)
    hk_ref[...] = (1.0 - f).astype(BF16)
    lfh_ref[...] = jnp.log(f)
    hi_ref[...] = z[:, _C_HI:_C_HG].astype(BF16)
    hg_ref[...] = _sigmoid(z[:, _C_HG:_C_GA]).astype(BF16)
    ga_ref[...] = _sigmoid(z[:, _C_GA:_C_GB]).astype(BF16)
    gb_ref[...] = _sigmoid(z[:, _C_GB:_C_END]).astype(BF16)

    if not fold:
        qs_ref[...] = zq.astype(BF16)
        return

    @pl.when(pl.program_id(0) == 0)
    def _():
        carry_ref[...] = jnp.zeros_like(carry_ref)

    cum = carry_ref[...] + _sum_by_01_matrix(tri_ref[...], logf)
    carry_ref[...] = cum[-1:, :]

    seg = seg_ref[...]
    qn2 = jnp.dot((zq * zq).astype(BF16), seg, preferred_element_type=F32)
    kn2 = jnp.dot((zk * zk).astype(BF16), seg, preferred_element_type=F32)
    dg = jnp.dot((zq * zk).astype(BF16), seg, preferred_element_type=F32)
    stat_ref[0, 0:1, :] = jnp.max(qn2, axis=0, keepdims=True)
    stat_ref[0, 1:2, :] = jnp.max(kn2, axis=0, keepdims=True)
    stat_ref[0, 2:3, :] = jnp.min(dg, axis=0, keepdims=True)
    stat_ref[0, 3:4, :] = cum[0:1, :]
    stat_ref[0, 4:5, :] = cum[-1:, :]
    stat_ref[0, 5:8, :] = jnp.zeros((3, LANES), F32)

    zq = zq * LOG2E
    c_hi, c_mid, c_lo = _split3(cum * LOG2E)
    lane = lax.broadcasted_iota(jnp.int32, cum.shape, 1)
    pieces = jnp.where(lane < 8, c_hi, jnp.where(lane < 16, c_mid, c_lo))
    pieces = jnp.where(lane < 24, pieces, jnp.zeros_like(pieces))
    ex_q = jnp.dot(pieces, pq_ref[...], preferred_element_type=F32) + cq_ref[...]
    ex_k = jnp.dot(pieces, pk_ref[...], preferred_element_type=F32) + ck_ref[...]
    ex_v = cv_ref[...]

    low = lax.broadcasted_iota(jnp.int32, (zq.shape[0], LANES), 1) < FOX_HEAD_DIM
    for src, ex, dst, transposed in ((zq, ex_q, qh_ref, True), (zk, ex_k, kh_ref, False), (zv, ex_v, vh_ref, True)):
        for c in range(FOX_WIDTH // LANES):
            pair = src[:, c * LANES:(c + 1) * LANES]
            swapped = pltpu.roll(pair, FOX_HEAD_DIM, axis=1)
            for j, data in enumerate((pair, swapped)):
                hd = 2 * c + j
                blk = jnp.where(low, data, ex[:, hd * LANES:(hd + 1) * LANES])
                dst[hd] = (blk.T[:dst.shape[1]] if transposed else blk).astype(BF16)


def _bias_fold_constants():
    pq = np.zeros((LANES, FOX_HEADS * LANES), np.float32)
    pk = np.zeros((LANES, FOX_HEADS * LANES), np.float32)
    cq = np.zeros((1, FOX_HEADS * LANES), np.float32)
    ck = np.zeros((1, FOX_HEADS * LANES), np.float32)
    cv = np.zeros((1, FOX_HEADS * LANES), np.float32)
    for h in range(FOX_HEADS):
        base = h * LANES + FOX_HEAD_DIM
        for p in range(3):
            pq[p * 8 + h, base + p] = 1.0
            ck[0, base + p] = 1.0
            pk[p * 8 + h, base + 3 + p] = -1.0
            cq[0, base + 3 + p] = 1.0
        cv[0, base] = 1.0
    return (jnp.asarray(pq, BF16), jnp.asarray(pk, BF16), jnp.asarray(cq), jnp.asarray(ck), jnp.asarray(cv))


def _const_spec(shape, single=True):
    nd = len(shape)
    if single:
        return pl.BlockSpec(shape, lambda *_: (0,) * nd, pipeline_mode=pl.Buffered(1))
    return pl.BlockSpec(shape, lambda *_: (0,) * nd)


def _proj(x, gain, w_all, fb3, lb_logits, *, fold):
    rows = x.shape[0]
    tm = min(PROJ_TM, rows)
    n = rows // tm
    row = lambda width: pl.BlockSpec((tm, width), lambda i: (i, 0))
    in_specs = [row(D_MODEL), _const_spec((1, D_MODEL)), _const_spec(w_all.shape), _const_spec((1, LANES)),
                _const_spec(lb_logits.shape)]
    args = [x, gain, w_all, fb3, lb_logits]
    common_out = [
        (jax.ShapeDtypeStruct((rows, FOX_WIDTH), F32), row(FOX_WIDTH)),
        (jax.ShapeDtypeStruct((rows, FOX_WIDTH), F32), row(FOX_WIDTH)),
        (jax.ShapeDtypeStruct((rows, FOX_HEADS), F32), row(FOX_HEADS)),
        (jax.ShapeDtypeStruct((rows, HG_WIDTH), BF16), row(HG_WIDTH)),
        (jax.ShapeDtypeStruct((rows, HG_WIDTH), BF16), row(HG_WIDTH)),
        (jax.ShapeDtypeStruct((rows, HG_WIDTH), F32), row(HG_WIDTH)),
        (jax.ShapeDtypeStruct((rows, HG_WIDTH), BF16), row(HG_WIDTH)),
        (jax.ShapeDtypeStruct((rows, HG_WIDTH), BF16), row(HG_WIDTH)),
        (jax.ShapeDtypeStruct((rows, D_MODEL), BF16), row(D_MODEL)),
        (jax.ShapeDtypeStruct((rows, D_MODEL), BF16), row(D_MODEL)),
    ]
    scratch = []
    if fold:
        tri = jnp.asarray(np.tril(np.ones((tm, tm), np.float32)), BF16)
        seg = np.zeros((FOX_WIDTH, LANES), np.float32)
        seg[np.arange(FOX_WIDTH), np.arange(FOX_WIDTH) // FOX_HEAD_DIM] = 1.0
        consts = _bias_fold_constants() + (jnp.asarray(seg, BF16),)
        in_specs += [_const_spec(tri.shape)] + [_const_spec(c.shape) for c in consts]
        args += [tri, *consts]
        head_major = (jax.ShapeDtypeStruct((FOX_HEADS, rows, FOX_PAD), BF16),
                      pl.BlockSpec((FOX_HEADS, tm, FOX_PAD), lambda i: (0, i, 0)))
        head_major_t = lambda depth: (jax.ShapeDtypeStruct((FOX_HEADS, depth, rows), BF16),
                                      pl.BlockSpec((FOX_HEADS, depth, tm), lambda i: (0, 0, i)))
        stats = (jax.ShapeDtypeStruct((n, 8, LANES), F32), pl.BlockSpec((1, 8, LANES), lambda i: (i, 0, 0)))
        outs = [head_major_t(FOX_PAD), head_major, head_major_t(FOX_V_ROWS), stats] + common_out
        scratch = [pltpu.VMEM((1, LANES), F32)]
    else:
        outs = [(jax.ShapeDtypeStruct((rows, FOX_WIDTH), BF16), row(FOX_WIDTH))] + common_out
    return pl.pallas_call(
        functools.partial(_proj_kernel, fold=fold),
        grid=(n,),
        in_specs=in_specs,
        out_specs=[o[1] for o in outs],
        out_shape=[o[0] for o in outs],
        scratch_shapes=scratch,
        compiler_params=pltpu.CompilerParams(dimension_semantics=("arbitrary",), vmem_limit_bytes=VMEM_LIMIT),
        name="proj_fold" if fold else "proj",
    )(*args)


def _fox_kernel(qi_ref, ki_ref, mode_ref, nact_ref, order_ref, qt_ref, k_ref, vt_ref, o_ref, m_ref, acc_ref):
    step = pl.program_id(0)
    mode = mode_ref[step]
    tq = qt_ref.shape[2]
    tk = k_ref.shape[1]

    @pl.when((mode & _MODE_FIRST) != 0)
    def _():
        m_ref[...] = jnp.full_like(m_ref, NEG_INF)
        acc_ref[...] = jnp.zeros_like(acc_ref)

    def logits(h):
        return jnp.dot(k_ref[h], qt_ref[h], preferred_element_type=F32)

    def softmax_update(h, s, masked):
        if masked:
            key = lax.broadcasted_iota(jnp.int32, (tk, tq), 0)
            qry = lax.broadcasted_iota(jnp.int32, (tk, tq), 1)
            s = jnp.where(key <= qry, s, NEG_INF)
        m_prev = m_ref[h]
        m_new = jnp.maximum(m_prev, jnp.max(s, axis=0, keepdims=True))
        alpha = jnp.exp2(m_prev - m_new)
        p = jnp.exp2(s - m_new).astype(BF16)
        acc_ref[h] = alpha * acc_ref[h] + jnp.dot(vt_ref[h], p, preferred_element_type=F32)
        m_ref[h] = m_new

    def sweep(heads, masked):
        s_next = logits(heads[0])
        for j, h in enumerate(heads):
            s = s_next
            if j + 1 < len(heads):
                s_next = logits(heads[j + 1])
            softmax_update(h, s, masked)

    @pl.when((mode & _MODE_PAIR) != 0)
    def _():
        base = qi_ref[step] * FOX_HEADS
        n_active = nact_ref[step]
        for count in range(2, FOX_HEADS + 1, 2):
            @pl.when(n_active == count)
            def _():
                sweep([order_ref[base + j] for j in range(count)], False)

    @pl.when((mode & _MODE_DIAG) != 0)
    def _():
        sweep(list(range(FOX_HEADS)), True)
        for c in range(FOX_WIDTH // LANES):
            halves = []
            for hd in (2 * c, 2 * c + 1):
                acc = acc_ref[hd]
                halves.append(acc[:FOX_HEAD_DIM, :] / acc[FOX_HEAD_DIM:FOX_HEAD_DIM + 1, :])
            o_ref[:, c * LANES:(c + 1) * LANES] = jnp.concatenate(halves, axis=0).T.astype(o_ref.dtype)


_MODE_FIRST, _MODE_PAIR, _MODE_DIAG = 1, 2, 4


def _fox_schedule(stats, nq):
    st = stats[:, :5, :FOX_HEADS].reshape(nq, -1, 5, FOX_HEADS)
    qn = jnp.sqrt(jnp.max(st[:, :, 0], axis=1)) * 1.01
    kn = jnp.sqrt(jnp.max(st[:, :, 1], axis=1)) * 1.01
    dmin = jnp.min(st[:, :, 2], axis=1)
    c_first = st[:, 0, 3]
    c_last = st[:, -1, 4]
    bound = qn[:, None] * kn[None, :] + (c_first - dmin)[:, None] - c_last[None, :]
    blk = jnp.arange(nq, dtype=jnp.int32)
    drop = (bound < -FOX_SKIP_NATS) & (blk[None, :] < blk[:, None])[:, :, None]
    prefix = jnp.cumsum(jnp.logical_not(drop).astype(jnp.int32), axis=1) == 0
    kstart_h = jnp.sum(prefix.astype(jnp.int32), axis=1)
    kstart = jnp.min(kstart_h, axis=1)
    hd = jnp.arange(FOX_HEADS, dtype=jnp.int32)
    before = (kstart_h[:, None, :] < kstart_h[:, :, None]) | (
        (kstart_h[:, None, :] == kstart_h[:, :, None]) & (hd[None, None, :] < hd[None, :, None]))
    rank = jnp.sum(before.astype(jnp.int32), axis=2)
    order = jnp.sum(jnp.where(rank[:, None, :] == hd[None, :, None], hd[None, None, :], 0), axis=2)
    count = blk - kstart + 1
    ends = jnp.cumsum(count)
    nsteps = nq * (nq + 1) // 2
    step = jnp.arange(nsteps, dtype=jnp.int32)
    valid = step < ends[-1]
    done = ends[None, :] <= step[:, None]
    q_of = jnp.minimum(jnp.sum(done.astype(jnp.int32), axis=1), nq - 1)
    begin = jnp.sum(jnp.where(done, count[None, :], 0), axis=1)
    k_first = jnp.sum(jnp.where(blk[None, :] == q_of[:, None], kstart[None, :], 0), axis=1)
    k_of = k_first + (step - begin)
    mode = jnp.where(k_of == q_of, _MODE_DIAG, _MODE_PAIR) + jnp.where(k_of == k_first, _MODE_FIRST, 0)
    mode = jnp.where(valid, mode, 0)
    kstart_h_of = jnp.sum(jnp.where((blk[None, :] == q_of[:, None])[:, :, None], kstart_h[None, :, :], 0), axis=1)
    n_active = jnp.sum((kstart_h_of <= k_of[:, None]).astype(jnp.int32), axis=1)
    n_active = jnp.minimum((n_active + 1) // 2 * 2, FOX_HEADS)
    q_of = jnp.where(valid, q_of, nq - 1)
    k_of = jnp.where(valid, k_of, nq - 1)
    as_i32 = lambda a: a.astype(jnp.int32)
    return as_i32(q_of), as_i32(k_of), as_i32(mode), as_i32(n_active), as_i32(order.reshape(-1))


def _fox_prompt(qt, kh, vt, stats):
    seq = kh.shape[1]
    assert FOX_TQ == FOX_TK
    nq = seq // FOX_TQ
    tables = _fox_schedule(stats, nq)
    grid_spec = pltpu.PrefetchScalarGridSpec(
        num_scalar_prefetch=len(tables),
        grid=(nq * (nq + 1) // 2,),
        in_specs=[
            pl.BlockSpec((FOX_HEADS, FOX_PAD, FOX_TQ), lambda s, qi, ki, *_: (0, 0, qi[s])),
            pl.BlockSpec((FOX_HEADS, FOX_TK, FOX_PAD), lambda s, qi, ki, *_: (0, ki[s], 0)),
            pl.BlockSpec((FOX_HEADS, FOX_V_ROWS, FOX_TK), lambda s, qi, ki, *_: (0, 0, ki[s])),
        ],
        out_specs=pl.BlockSpec((FOX_TQ, FOX_WIDTH), lambda s, qi, ki, *_: (qi[s], 0)),
        scratch_shapes=[pltpu.VMEM((FOX_HEADS, 1, FOX_TQ), F32),
                        pltpu.VMEM((FOX_HEADS, FOX_V_ROWS, FOX_TQ), F32)],
    )
    return pl.pallas_call(
        _fox_kernel,
        grid_spec=grid_spec,
        out_shape=jax.ShapeDtypeStruct((seq, FOX_WIDTH), BF16),
        compiler_params=pltpu.CompilerParams(dimension_semantics=("arbitrary",), vmem_limit_bytes=VMEM_LIMIT),
        name="fox_prompt",
    )(*tables, qt, kh, vt)


def _lane_cumsum(x):
    n = x.shape[1]
    lane = lax.broadcasted_iota(jnp.int32, x.shape, 1)
    shift = 1
    while shift < n:
        x = x + jnp.where(lane >= shift, pltpu.roll(x, shift, axis=1), 0.0)
        shift *= 2
    return x


def _fox_sample_kernel(q_ref, kn_ref, vn_ref, lft_ref, ck_ref, cv_ref, o_ref,
                       cq_ref, cum_ref, m_ref, l_ref, acc_ref, *, n_past_chunks, tk, t_new):
    c = pl.program_id(1)
    rows = FOX_HEADS * t_new
    heads = [slice(h * FOX_HEAD_DIM, (h + 1) * FOX_HEAD_DIM) for h in range(FOX_HEADS)]
    head_rows = [slice(h * t_new, (h + 1) * t_new) for h in range(FOX_HEADS)]

    @pl.when(c == 0)
    def _():
        cum = _lane_cumsum(lft_ref[0])
        for j in range(n_past_chunks):
            cum_ref[j] = cum[:, j * tk:(j + 1) * tk]
        new_cum = cum[:, n_past_chunks * tk:n_past_chunks * tk + LANES]
        cum_ref[n_past_chunks, :, :LANES] = new_cum
        new_cum_t = jnp.concatenate([new_cum] * (LANES // FOX_HEADS), axis=0).T
        for h in range(FOX_HEADS):
            cq_ref[head_rows[h], :] = jnp.broadcast_to(new_cum_t[:t_new, h:h + 1], (t_new, LANES))
        m_ref[...] = jnp.full_like(m_ref, NEG_INF)
        l_ref[...] = jnp.zeros_like(l_ref)
        acc_ref[...] = jnp.zeros_like(acc_ref)

    nt = (((1,), (1,)), ((), ()))

    def update(qk, pv_of, ck_rows, mask):
        q = q_ref[...]
        s = jnp.concatenate([qk(q[:, heads[h]], h) for h in range(FOX_HEADS)], axis=0)
        s = s + (cq_ref[:, :1] - ck_rows)
        if mask is not None:
            s = jnp.where(mask, s, NEG_INF)
        m_prev = m_ref[...]
        m_new = jnp.maximum(m_prev, jnp.max(s, axis=1, keepdims=True))
        alpha = jnp.exp(m_prev - m_new)
        p = jnp.exp(s - m_new[:, :1])
        l_ref[...] = alpha * l_ref[...] + jnp.sum(p, axis=1, keepdims=True)
        p = p.astype(BF16)
        pv = jnp.concatenate([pv_of(p[head_rows[h]], h) for h in range(FOX_HEADS)], axis=0)
        acc_ref[...] = alpha[:, :FOX_HEAD_DIM] * acc_ref[...] + pv
        m_ref[...] = m_new

    def expand_rows(x, width):
        return jnp.concatenate([jnp.broadcast_to(x[h:h + 1, :], (t_new, width)) for h in range(FOX_HEADS)], axis=0)

    @pl.when(c < n_past_chunks)
    def _():
        update(lambda qh, h: jnp.dot(qh, ck_ref[0, h].astype(BF16), preferred_element_type=F32),
               lambda ph, h: lax.dot_general(ph, cv_ref[0, h].astype(BF16), nt, preferred_element_type=F32),
               expand_rows(cum_ref[c], tk), None)

    @pl.when(c == n_past_chunks)
    def _():
        ck_rows = expand_rows(cum_ref[n_past_chunks, :, :LANES], LANES)[:, :t_new]
        rowt = lax.broadcasted_iota(jnp.int32, (rows, t_new), 0) % t_new
        coli = lax.broadcasted_iota(jnp.int32, (rows, t_new), 1)
        update(lambda qh, h: lax.dot_general(qh, kn_ref[:, heads[h]].astype(BF16), nt, preferred_element_type=F32),
               lambda ph, h: jnp.dot(ph, vn_ref[:, heads[h]].astype(BF16), preferred_element_type=F32),
               ck_rows, coli <= rowt)
        out = acc_ref[...] / l_ref[:, :FOX_HEAD_DIM]
        o_ref[...] = jnp.concatenate([out[head_rows[h]] for h in range(FOX_HEADS)], axis=1).astype(o_ref.dtype)


def _fox_sample(qs, k_new, v_new, lf_all_t, cache_k, cache_v, *, t_new):
    nb, past = cache_k.shape[0], cache_k.shape[3]
    tk = SAMPLE_TK
    npc = past // tk
    rows = FOX_HEADS * t_new
    last = npc - 1
    kern = functools.partial(_fox_sample_kernel, n_past_chunks=npc, tk=tk, t_new=t_new)
    cache_spec = pl.BlockSpec((1, FOX_HEADS, FOX_HEAD_DIM, tk), lambda b, c: (b, 0, 0, jnp.minimum(c, last)))
    return pl.pallas_call(
        kern,
        grid=(nb, npc + 1),
        in_specs=[
            pl.BlockSpec((t_new, FOX_WIDTH), lambda b, c: (b, 0)),
            pl.BlockSpec((t_new, FOX_WIDTH), lambda b, c: (b, 0)),
            pl.BlockSpec((t_new, FOX_WIDTH), lambda b, c: (b, 0)),
            pl.BlockSpec((1, FOX_HEADS, past + LANES), lambda b, c: (b, 0, 0)),
            cache_spec, cache_spec,
        ],
        out_specs=pl.BlockSpec((t_new, FOX_WIDTH), lambda b, c: (b, 0)),
        out_shape=jax.ShapeDtypeStruct((nb * t_new, FOX_WIDTH), BF16),
        scratch_shapes=[
            pltpu.VMEM((rows, LANES), F32),
            pltpu.VMEM((npc + 1, FOX_HEADS, tk), F32),
            pltpu.VMEM((rows, LANES), F32),
            pltpu.VMEM((rows, LANES), F32),
            pltpu.VMEM((rows, FOX_HEAD_DIM), F32),
        ],
        compiler_params=pltpu.CompilerParams(dimension_semantics=("arbitrary", "arbitrary"),
                                             vmem_limit_bytes=VMEM_LIMIT),
        name="fox_sample",
    )(qs, k_new, v_new, lf_all_t, cache_k, cache_v)


def _hgrn_kernel(hq_ref, hk_ref, lfh_ref, hi_ref, hg_ref, s0_ref, norm_ref, tri_ref,
                 o_ref, sout_ref, st_ref, qq_ref, kk_ref, eb_ref, b_ref, oin_ref, od_ref):
    t = pl.program_id(1)
    tc = hq_ref.shape[0]
    blk = min(HG_BLOCK, tc)
    nblk = tc // blk
    heads = [slice(h * HG_DIM, (h + 1) * HG_DIM) for h in range(HG_HEADS)]

    @pl.when(t == 0)
    def _():
        for h in range(HG_HEADS):
            st_ref[h] = s0_ref[0, h].T

    def at_block_row(x, r):
        x3 = x.reshape(nblk, blk, HG_DIM)
        return jnp.broadcast_to(x3[:, r:r + 1, :], x3.shape).reshape(tc, HG_DIM)

    def rel_to_middle(b):
        return b - at_block_row(b, blk // 2 - 1)

    worst = jnp.zeros((), F32)
    for sl in heads:
        b = _sum_by_01_matrix(tri_ref[...], lfh_ref[:, sl])
        eb = jnp.exp(b)
        qq_ref[:, sl] = (hq_ref[:, sl].astype(F32) * eb).astype(BF16)
        kk_ref[:, sl] = (hk_ref[:, sl].astype(F32) * jnp.exp(at_block_row(b, blk - 1) - b)).astype(BF16)
        eb_ref[:, sl] = eb
        b_ref[:, sl] = b
        worst = jnp.maximum(worst, jnp.max(jnp.abs(rel_to_middle(b))))

    def carried_state_and_output():
        for j in range(nblk):
            rows = slice(j * blk, (j + 1) * blk)
            for h, sl in enumerate(heads):
                st = st_ref[h]
                oin_ref[rows, sl] = lax.dot_general(qq_ref[rows, sl], st.astype(BF16), (((1,), (1,)), ((), ())),
                                                    preferred_element_type=F32)
                upd = lax.dot_general(hi_ref[rows, sl], kk_ref[rows, sl], (((0,), (0,)), ((), ())),
                                      preferred_element_type=F32)
                st_ref[h] = st * eb_ref[(j + 1) * blk - 1:(j + 1) * blk, sl] + upd
        for sl in heads:
            o = oin_ref[:, sl] + od_ref[:, sl]
            y = (_rms_scale(o) * norm_ref[...]) * hg_ref[:, sl].astype(F32)
            o_ref[:, sl] = y.astype(o_ref.dtype)

    splittable = worst <= HG_SPLIT_MAX

    @pl.when(splittable)
    def _():
        row = lax.broadcasted_iota(jnp.int32, (tc, tc), 0)
        col = lax.broadcasted_iota(jnp.int32, (tc, tc), 1)
        pair_in_block = (row // blk == col // blk) & (col <= row)
        for sl in heads:
            b_rel = rel_to_middle(b_ref[:, sl])
            qs = (hq_ref[:, sl].astype(F32) * jnp.exp(b_rel)).astype(BF16)
            ks = (hk_ref[:, sl].astype(F32) * jnp.exp(-b_rel)).astype(BF16)
            a = lax.dot_general(qs, ks, (((1,), (1,)), ((), ())), preferred_element_type=F32)
            a = jnp.where(pair_in_block, a, 0.0).astype(BF16)
            od_ref[:, sl] = jnp.dot(a, hi_ref[:, sl], preferred_element_type=F32)
        carried_state_and_output()

    @pl.when(jnp.logical_not(splittable))
    def _():
        row_in_blk = lax.broadcasted_iota(jnp.int32, (tc, HG_DIM), 0) % blk
        for sl in heads:
            q = hq_ref[:, sl].astype(F32)
            k = hk_ref[:, sl].astype(F32)
            v = hi_ref[:, sl].astype(F32)
            b = b_ref[:, sl]

            def lag_step(lag, od):
                k_l = pltpu.roll(k, lag, axis=0)
                b_l = pltpu.roll(b, lag, axis=0)
                v_l = pltpu.roll(v, lag, axis=0)
                w = q * k_l * jnp.exp(jnp.minimum(b - b_l, 0.0))
                w = jnp.where(row_in_blk >= lag, w, 0.0)
                return od + jnp.sum(w, axis=1, keepdims=True) * v_l

            od_ref[:, sl] = lax.fori_loop(1, blk, lag_step, jnp.sum(q * k, axis=1, keepdims=True) * v)
        carried_state_and_output()

    @pl.when(t == pl.num_programs(1) - 1)
    def _():
        for h in range(HG_HEADS):
            sout_ref[0, h] = st_ref[h].T


def _hgrn_cumsum_matrix(n, blk):
    t = np.arange(n)[:, None]
    s = np.arange(n)[None, :]
    return jnp.asarray((((t // blk) == (s // blk)) & (s <= t)).astype(np.float32), BF16)


def _hgrn(hq, hk, lfh, hi, hg, s0, norm, *, nseq):
    rows = hq.shape[0]
    t_len = rows // nseq
    tc = min(HG_TC, t_len)
    nt = t_len // tc
    blk = min(HG_BLOCK, tc)
    tri = _hgrn_cumsum_matrix(tc, blk)
    row = pl.BlockSpec((tc, HG_WIDTH), lambda b, t: (b * nt + t, 0))
    state = pl.BlockSpec((1, HG_HEADS, HG_DIM, HG_DIM), lambda b, t: (b, 0, 0, 0))
    return pl.pallas_call(
        _hgrn_kernel,
        grid=(nseq, nt),
        in_specs=[row, row, row, row, row, state,
                  pl.BlockSpec((1, HG_DIM), lambda b, t: (0, 0)),
                  pl.BlockSpec((tc, tc), lambda b, t: (0, 0))],
        out_specs=[row, state],
        out_shape=[jax.ShapeDtypeStruct((rows, HG_WIDTH), BF16),
                   jax.ShapeDtypeStruct((nseq, HG_HEADS, HG_DIM, HG_DIM), F32)],
        scratch_shapes=[
            pltpu.VMEM((HG_HEADS, HG_DIM, HG_DIM), F32),
            pltpu.VMEM((tc, HG_WIDTH), BF16),
            pltpu.VMEM((tc, HG_WIDTH), BF16),
            pltpu.VMEM((tc, HG_WIDTH), F32),
            pltpu.VMEM((tc, HG_WIDTH), F32),
            pltpu.VMEM((tc, HG_WIDTH), F32),
            pltpu.VMEM((tc, HG_WIDTH), F32),
        ],
        compiler_params=pltpu.CompilerParams(dimension_semantics=("arbitrary", "arbitrary"),
                                             vmem_limit_bytes=VMEM_LIMIT),
        name="hgrn",
    )(hq, hk, lfh, hi, hg, s0, norm, tri)


def _mixffn_kernel(x_ref, of_ref, oh_ref, ga_ref, gb_ref, hist_ref,
                   wbf_ref, wbh_ref, wout_ref, wup_ref, wdn_ref,
                   npost_ref, npre2_ref, npost2_ref, cw_ref, cb_ref,
                   y_ref, conv_ref, tail_ref, *, seg_len):
    tm = x_ref.shape[0]
    br_f = jnp.dot(of_ref[...], wbf_ref[...], preferred_element_type=F32)
    br_h = jnp.dot(oh_ref[...], wbh_ref[...], preferred_element_type=F32)
    merged = ga_ref[...].astype(F32) * br_f + gb_ref[...].astype(F32) * br_h
    mix = jnp.dot(merged.astype(BF16), wout_ref[...], preferred_element_type=F32)
    x1 = x_ref[...] + _rms_scale(mix) * npost_ref[...]

    h2 = (_rms_scale(x1) * npre2_ref[...]).astype(BF16)
    def up_chunk(j):
        w = jnp.concatenate([wup_ref[:, j * FFN_CHUNK:(j + 1) * FFN_CHUNK],
                             wup_ref[:, D_FF + j * FFN_CHUNK:D_FF + (j + 1) * FFN_CHUNK]], axis=1)
        return jnp.dot(h2, w, preferred_element_type=F32)

    n_chunks = D_FF // FFN_CHUNK
    carried = seg_len >= tm
    if carried:
        @pl.when(pl.program_id(0) == 0)
        def _():
            tail_ref[...] = hist_ref[0]
    rowi = lax.broadcasted_iota(jnp.int32, (tm, FFN_CHUNK), 0)
    ff = None
    up_next = up_chunk(0)
    for j in range(n_chunks):
        cols = slice(j * FFN_CHUNK, (j + 1) * FFN_CHUNK)
        up = up_next
        if j + 1 < n_chunks:
            up_next = up_chunk(j + 1)
        a = up[:, :FFN_CHUNK]
        g = up[:, FFN_CHUNK:]
        prev1 = pltpu.roll(a, 1, axis=0)
        prev2 = pltpu.roll(a, 2, axis=0)
        if carried:
            t0 = tail_ref[0:1, cols]
            t1 = tail_ref[1:2, cols]
            prev1 = jnp.where(rowi == 0, t1, prev1)
            prev2 = jnp.where(rowi == 0, t0, jnp.where(rowi == 1, t1, prev2))
            tail_ref[:, cols] = a[tm - 2:, :]
            conv_ref[0, :, cols] = a[tm - 2:, :]
        else:
            for s in range(tm // seg_len):
                h0 = hist_ref[s, 0:1, cols]
                h1 = hist_ref[s, 1:2, cols]
                prev1 = jnp.where(rowi == s * seg_len, h1, prev1)
                prev2 = jnp.where(rowi == s * seg_len, h0, jnp.where(rowi == s * seg_len + 1, h1, prev2))
                conv_ref[s, :, cols] = a[(s + 1) * seg_len - 2:(s + 1) * seg_len, :]
        c = cb_ref[:, cols] + cw_ref[0:1, cols] * prev2 + cw_ref[1:2, cols] * prev1 + cw_ref[2:3, cols] * a
        act = (jax.nn.gelu(c, approximate=True) * g).astype(BF16)
        part = jnp.dot(act, wdn_ref[cols, :], preferred_element_type=F32)
        ff = part if ff is None else ff + part
    y_ref[...] = x1 + _rms_scale(ff) * npost2_ref[...]


def _mixffn(x, o_fox, o_hg, ga, gb, hist, w, *, seg_len):
    rows = x.shape[0]
    tm = min(FFN_TM, rows)
    n = rows // tm
    nseg = hist.shape[0]
    row = lambda width: pl.BlockSpec((tm, width), lambda i: (i, 0))
    weights = [w["bf"], w["bh"], w["out"], w["up"], w["down"]]
    smalls = [w["npost"], w["npre2"], w["npost2"], w["conv_w"], w["conv_b"]]
    hist_spec = pl.BlockSpec(hist.shape, lambda i: (0, 0, 0))
    scratch = [pltpu.VMEM((2, D_FF), F32)]
    return pl.pallas_call(
        functools.partial(_mixffn_kernel, seg_len=seg_len),
        grid=(n,),
        in_specs=[row(D_MODEL), row(FOX_WIDTH), row(HG_WIDTH), row(D_MODEL), row(D_MODEL), hist_spec]
                 + [_const_spec(a.shape) for a in weights] + [_const_spec(a.shape) for a in smalls],
        out_specs=[row(D_MODEL), pl.BlockSpec((nseg, 2, D_FF), lambda i: (0, 0, 0))],
        out_shape=[jax.ShapeDtypeStruct((rows, D_MODEL), F32), jax.ShapeDtypeStruct((nseg, 2, D_FF), F32)],
        scratch_shapes=scratch,
        compiler_params=pltpu.CompilerParams(dimension_semantics=("arbitrary",), vmem_limit_bytes=VMEM_LIMIT),
        name="mixffn",
    )(x, o_fox, o_hg, ga, gb, hist, *weights, *smalls)


def _prep_w_in(w_in, fox_f_bias):
    offs = np.cumsum([0] + IN_SIZES)
    seg = [w_in[:, int(offs[i]):int(offs[i + 1])] for i in range(len(IN_SIZES))]
    pad = jnp.zeros((D_MODEL, LANES - 3 * FOX_HEADS), w_in.dtype)
    f3 = jnp.concatenate([seg[3], seg[3], seg[3], pad], axis=1)
    w_all = jnp.concatenate(seg[:3] + [f3] + seg[4:], axis=1).astype(BF16)
    fb = fox_f_bias.astype(F32)
    fb3 = jnp.concatenate([fb, fb, fb, jnp.zeros((LANES - 3 * FOX_HEADS,), F32)]).reshape(1, LANES)
    return w_all, fb3


def kernel(x_prompt, x_sample, cache_fox_k, cache_fox_v, cache_fox_logf, state_hgrn, state_ffn_conv, norm_mix_pre, norm_mix_post, w_in, fox_f_bias, hgrn_lb_logits, hgrn_norm, w_branch_fox, w_branch_hgrn, w_out, norm_ffn_pre, norm_ffn_post, w_up, ffn_conv_w, ffn_conv_b, w_down):
    depth = w_in.shape[0]
    assert depth == 1 and hgrn_lb_logits.shape[0] == 2
    bp, seq, _ = x_prompt.shape
    assert bp == 1
    nb, t_new, _ = x_sample.shape
    past = cache_fox_k.shape[2]

    w_all, fb3 = _prep_w_in(w_in[0], fox_f_bias[0])
    g_pre = norm_mix_pre[0].reshape(1, D_MODEL)
    lbl = hgrn_lb_logits.astype(F32)
    hnorm = hgrn_norm[0].astype(F32).reshape(1, HG_DIM)
    w = {
        "bf": w_branch_fox[0].astype(BF16), "bh": w_branch_hgrn[0].astype(BF16), "out": w_out[0].astype(BF16),
        "up": w_up[0].astype(BF16), "down": w_down[0].astype(BF16),
        "npost": norm_mix_post[0].reshape(1, D_MODEL), "npre2": norm_ffn_pre[0].reshape(1, D_MODEL),
        "npost2": norm_ffn_post[0].reshape(1, D_MODEL),
        "conv_w": ffn_conv_w[0], "conv_b": ffn_conv_b[0].reshape(1, D_FF),
    }

    xp = x_prompt.reshape(seq, D_MODEL)
    (qt, kh, vt, stats, pk, pv, plf, hq, hk, lfh, hi, hg, ga, gb) = _proj(xp, g_pre, w_all, fb3, lbl, fold=True)
    o_fox = _fox_prompt(qt, kh, vt, stats)
    s0 = jnp.zeros((1, HG_HEADS, HG_DIM, HG_DIM), F32)
    o_hg, p_state = _hgrn(hq, hk, lfh, hi, hg, s0, hnorm, nseq=1)
    hist0 = jnp.zeros((1, 2, D_FF), F32)
    yp, pconv = _mixffn(xp, o_fox, o_hg, ga, gb, hist0, w, seg_len=seq)

    xs = x_sample.reshape(nb * t_new, D_MODEL)
    (qs, sk, sv, slf, hq, hk, lfh, hi, hg, ga, gb) = _proj(xs, g_pre, w_all, fb3, lbl, fold=False)
    lf_all_t = jnp.concatenate([
        jnp.swapaxes(cache_fox_logf[0].astype(F32), 1, 2),
        jnp.swapaxes(slf.reshape(nb, t_new, FOX_HEADS), 1, 2),
        jnp.zeros((nb, FOX_HEADS, LANES - t_new), F32)], axis=2)
    cache_kt = jnp.transpose(cache_fox_k[0], (0, 2, 3, 1))
    cache_vt = jnp.transpose(cache_fox_v[0], (0, 2, 3, 1))
    o_fox_s = _fox_sample(qs, sk, sv, lf_all_t, cache_kt, cache_vt, t_new=t_new)
    o_hg_s, s_state = _hgrn(hq, hk, lfh, hi, hg, state_hgrn[0].astype(F32), hnorm, nseq=nb)
    ys, sconv = _mixffn(xs, o_fox_s, o_hg_s, ga, gb, state_ffn_conv[0], w, seg_len=t_new)

    return (
        yp.reshape(bp, seq, D_MODEL),
        ys.reshape(nb, t_new, D_MODEL),
        pk.reshape(1, bp, seq, FOX_HEADS, FOX_HEAD_DIM),
        pv.reshape(1, bp, seq, FOX_HEADS, FOX_HEAD_DIM),
        plf.reshape(1, bp, seq, FOX_HEADS),
        p_state.reshape(1, bp, HG_HEADS, HG_DIM, HG_DIM),
        pconv.reshape(1, bp, 2, D_FF),
        sk.reshape(1, nb, t_new, FOX_HEADS, FOX_HEAD_DIM),
        sv.reshape(1, nb, t_new, FOX_HEADS, FOX_HEAD_DIM),
        slf.reshape(1, nb, t_new, FOX_HEADS),
        s_state.reshape(1, nb, HG_HEADS, HG_DIM, HG_DIM),
        sconv.reshape(1, nb, 2, D_FF),
    )
```

```python
import functools

import numpy as np
import jax
import jax.numpy as jnp
from jax import lax
from jax.experimental import pallas as pl
from jax.experimental.pallas import tpu as pltpu

F32 = jnp.float32
BF16 = jnp.bfloat16

D_MODEL = 1024
FOX_HEADS = 8
FOX_HEAD_DIM = 64
FOX_WIDTH = FOX_HEADS * FOX_HEAD_DIM
HG_HEADS = 4
HG_DIM = 128
HG_WIDTH = HG_HEADS * HG_DIM
D_FF = 2816
RMS_EPS = 1e-6
NEG_INF = -1e30
LOG2E = 1.4426950408889634
FOX_SKIP_NATS = 110.0
IN_SIZES = [FOX_WIDTH, FOX_WIDTH, FOX_WIDTH, FOX_HEADS, HG_WIDTH, HG_WIDTH, HG_WIDTH, HG_WIDTH, D_MODEL, D_MODEL]

LANES = 128
FOX_PAD = 2 * FOX_HEAD_DIM
FOX_V_ROWS = FOX_HEAD_DIM + 16
HG_BLOCK = 64
HG_SPLIT_MAX = 60.0
VMEM_LIMIT = 56 * 1024 * 1024

PROJ_TM = 256
FOX_TQ = 512
FOX_TK = 512
HG_TC = 256
FFN_TM = 256
FFN_CHUNK = 256
SAMPLE_TK = 1024

_C_Q, _C_K, _C_V, _C_F = 0, 512, 1024, 1536
_C_HQ, _C_HF, _C_HI, _C_HG = 1664, 2176, 2688, 3200
_C_GA, _C_GB, _C_END = 3712, 4736, 5760


def _split3(x):
    hi = x.astype(BF16)
    r = x - hi.astype(F32)
    mid = r.astype(BF16)
    lo = (r - mid.astype(F32)).astype(BF16)
    return hi, mid, lo


def _sum_by_01_matrix(mat01, x):
    cat = jnp.concatenate(_split3(x), axis=1)
    y = jnp.dot(mat01, cat, preferred_element_type=F32)
    return y[:, :LANES] + y[:, LANES:2 * LANES] + y[:, 2 * LANES:]


def _rms_scale(x):
    return x * lax.rsqrt(jnp.mean(x * x, axis=-1, keepdims=True) + RMS_EPS)


def _log_sigmoid(x):
    return jnp.minimum(x, 0.0) - jnp.log1p(jnp.exp(-jnp.abs(x)))


def _sigmoid(x):
    return 1.0 / (1.0 + jnp.exp(-x))


def _proj_kernel(*refs, fold):
    if fold:
        (x_ref, g_ref, w_ref, fb_ref, lbl_ref, tri_ref, pq_ref, pk_ref, cq_ref, ck_ref, cv_ref, seg_ref,
         qh_ref, kh_ref, vh_ref, stat_ref, kout_ref, vout_ref, lf_ref, hq_ref, hk_ref, lfh_ref, hi_ref, hg_ref,
         ga_ref, gb_ref, carry_ref) = refs
    else:
        (x_ref, g_ref, w_ref, fb_ref, lbl_ref,
         qs_ref, kout_ref, vout_ref, lf_ref, hq_ref, hk_ref, lfh_ref, hi_ref, hg_ref,
         ga_ref, gb_ref) = refs

    h = (_rms_scale(x_ref[...]) * g_ref[...]).astype(BF16)

    z = jnp.dot(h, w_ref[...], preferred_element_type=F32)

    zq = z[:, _C_Q:_C_K] * (FOX_HEAD_DIM ** -0.5)
    zk = z[:, _C_K:_C_V]
    zv = z[:, _C_V:_C_F]
    kout_ref[...] = zk
    vout_ref[...] = zv
    logf = _log_sigmoid(z[:, _C_F:_C_HQ] + fb_ref[...])
    lf_ref[...] = logf[:, :FOX_HEADS]

    l0 = lbl_ref[0:1, :]
    l1 = lbl_ref[1:2, :]
    lmax = jnp.maximum(l0, l1)
    e0 = jnp.exp(l0 - lmax)
    lb = e0 / (e0 + jnp.exp(l1 - lmax))
    f = lb + (1.0 - lb) * _sigmoid(z[:, _C_HF:_C_HI])
    hq_ref[...] = z[:, _C_HQ:_C_HF].astype(BF16)
    hk_ref[...] = (1.0 - f).astype(BF16)
    lfh_ref[...] = jnp.log(f)
    hi_ref[...] = z[:, _C_HI:_C_HG].astype(BF16)
    hg_ref[...] = _sigmoid(z[:, _C_HG:_C_GA]).astype(BF16)
    ga_ref[...] = _sigmoid(z[:, _C_GA:_C_GB]).astype(BF16)
    gb_ref[...] = _sigmoid(z[:, _C_GB:_C_END]).astype(BF16)

    if not fold:
        qs_ref[...] = zq.astype(BF16)
        return

    @pl.when(pl.program_id(0) == 0)
    def _():
        carry_ref[...] = jnp.zeros_like(carry_ref)

    cum = carry_ref[...] + _sum_by_01_matrix(tri_ref[...], logf)
    carry_ref[...] = cum[-1:, :]

    seg = seg_ref[...]
    qn2 = jnp.dot((zq * zq).astype(BF16), seg, preferred_element_type=F32)
    kn2 = jnp.dot((zk * zk).astype(BF16), seg, preferred_element_type=F32)
    dg = jnp.dot((zq * zk).astype(BF16), seg, preferred_element_type=F32)
    stat_ref[0, 0:1, :] = jnp.max(qn2, axis=0, keepdims=True)
    stat_ref[0, 1:2, :] = jnp.max(kn2, axis=0, keepdims=True)
    stat_ref[0, 2:3, :] = jnp.min(dg, axis=0, keepdims=True)
    stat_ref[0, 3:4, :] = cum[0:1, :]
    stat_ref[0, 4:5, :] = cum[-1:, :]
    stat_ref[0, 5:8, :] = jnp.zeros((3, LANES), F32)

    zq = zq * LOG2E
    c_hi, c_mid, c_lo = _split3(cum * LOG2E)
    lane = lax.broadcasted_iota(jnp.int32, cum.shape, 1)
    pieces = jnp.where(lane < 8, c_hi, jnp.where(lane < 16, c_mid, c_lo))
    pieces = jnp.where(lane < 24, pieces, jnp.zeros_like(pieces))
    ex_q = jnp.dot(pieces, pq_ref[...], preferred_element_type=F32) + cq_ref[...]
    ex_k = jnp.dot(pieces, pk_ref[...], preferred_element_type=F32) + ck_ref[...]
    ex_v = cv_ref[...]

    low = lax.broadcasted_iota(jnp.int32, (zq.shape[0], LANES), 1) < FOX_HEAD_DIM
    for src, ex, dst, transposed in ((zq, ex_q, qh_ref, True), (zk, ex_k, kh_ref, False), (zv, ex_v, vh_ref, True)):
        for c in range(FOX_WIDTH // LANES):
            pair = src[:, c * LANES:(c + 1) * LANES]
            swapped = pltpu.roll(pair, FOX_HEAD_DIM, axis=1)
            for j, data in enumerate((pair, swapped)):
                hd = 2 * c + j
                blk = jnp.where(low, data, ex[:, hd * LANES:(hd + 1) * LANES])
                dst[hd] = (blk.T[:dst.shape[1]] if transposed else blk).astype(BF16)


def _bias_fold_constants():
    pq = np.zeros((LANES, FOX_HEADS * LANES), np.float32)
    pk = np.zeros((LANES, FOX_HEADS * LANES), np.float32)
    cq = np.zeros((1, FOX_HEADS * LANES), np.float32)
    ck = np.zeros((1, FOX_HEADS * LANES), np.float32)
    cv = np.zeros((1, FOX_HEADS * LANES), np.float32)
    for h in range(FOX_HEADS):
        base = h * LANES + FOX_HEAD_DIM
        for p in range(3):
            pq[p * 8 + h, base + p] = 1.0
            ck[0, base + p] = 1.0
            pk[p * 8 + h, base + 3 + p] = -1.0
            cq[0, base + 3 + p] = 1.0
        cv[0, base] = 1.0
    return (jnp.asarray(pq, BF16), jnp.asarray(pk, BF16), jnp.asarray(cq), jnp.asarray(ck), jnp.asarray(cv))


def _const_spec(shape, single=True):
    nd = len(shape)
    if single:
        return pl.BlockSpec(shape, lambda *_: (0,) * nd, pipeline_mode=pl.Buffered(1))
    return pl.BlockSpec(shape, lambda *_: (0,) * nd)


def _proj(x, gain, w_all, fb3, lb_logits, *, fold):
    rows = x.shape[0]
    tm = min(PROJ_TM, rows)
    n = rows // tm
    row = lambda width: pl.BlockSpec((tm, width), lambda i: (i, 0))
    in_specs = [row(D_MODEL), _const_spec((1, D_MODEL)), _const_spec(w_all.shape), _const_spec((1, LANES)),
                _const_spec(lb_logits.shape)]
    args = [x, gain, w_all, fb3, lb_logits]
    common_out = [
        (jax.ShapeDtypeStruct((rows, FOX_WIDTH), F32), row(FOX_WIDTH)),
        (jax.ShapeDtypeStruct((rows, FOX_WIDTH), F32), row(FOX_WIDTH)),
        (jax.ShapeDtypeStruct((rows, FOX_HEADS), F32), row(FOX_HEADS)),
        (jax.ShapeDtypeStruct((rows, HG_WIDTH), BF16), row(HG_WIDTH)),
        (jax.ShapeDtypeStruct((rows, HG_WIDTH), BF16), row(HG_WIDTH)),
        (jax.ShapeDtypeStruct((rows, HG_WIDTH), F32), row(HG_WIDTH)),
        (jax.ShapeDtypeStruct((rows, HG_WIDTH), BF16), row(HG_WIDTH)),
        (jax.ShapeDtypeStruct((rows, HG_WIDTH), BF16), row(HG_WIDTH)),
        (jax.ShapeDtypeStruct((rows, D_MODEL), BF16), row(D_MODEL)),
        (jax.ShapeDtypeStruct((rows, D_MODEL), BF16), row(D_MODEL)),
    ]
    scratch = []
    if fold:
        tri = jnp.asarray(np.tril(np.ones((tm, tm), np.float32)), BF16)
        seg = np.zeros((FOX_WIDTH, LANES), np.float32)
        seg[np.arange(FOX_WIDTH), np.arange(FOX_WIDTH) // FOX_HEAD_DIM] = 1.0
        consts = _bias_fold_constants() + (jnp.asarray(seg, BF16),)
        in_specs += [_const_spec(tri.shape)] + [_const_spec(c.shape) for c in consts]
        args += [tri, *consts]
        head_major = (jax.ShapeDtypeStruct((FOX_HEADS, rows, FOX_PAD), BF16),
                      pl.BlockSpec((FOX_HEADS, tm, FOX_PAD), lambda i: (0, i, 0)))
        head_major_t = lambda depth: (jax.ShapeDtypeStruct((FOX_HEADS, depth, rows), BF16),
                                      pl.BlockSpec((FOX_HEADS, depth, tm), lambda i: (0, 0, i)))
        stats = (jax.ShapeDtypeStruct((n, 8, LANES), F32), pl.BlockSpec((1, 8, LANES), lambda i: (i, 0, 0)))
        outs = [head_major_t(FOX_PAD), head_major, head_major_t(FOX_V_ROWS), stats] + common_out
        scratch = [pltpu.VMEM((1, LANES), F32)]
    else:
        outs = [(jax.ShapeDtypeStruct((rows, FOX_WIDTH), BF16), row(FOX_WIDTH))] + common_out
    return pl.pallas_call(
        functools.partial(_proj_kernel, fold=fold),
        grid=(n,),
        in_specs=in_specs,
        out_specs=[o[1] for o in outs],
        out_shape=[o[0] for o in outs],
        scratch_shapes=scratch,
        compiler_params=pltpu.CompilerParams(dimension_semantics=("arbitrary",), vmem_limit_bytes=VMEM_LIMIT),
        name="proj_fold" if fold else "proj",
    )(*args)


def _fox_kernel(qi_ref, ki_ref, mode_ref, nact_ref, order_ref, qn_ref, kn_ref, cl_ref,
                qt_ref, k_ref, vt_ref, o_ref, m_ref, acc_ref):
    step = pl.program_id(0)
    mode = mode_ref[step]
    q_blk = qi_ref[step]
    k_blk = ki_ref[step]
    tq = qt_ref.shape[2]
    tk = k_ref.shape[1]
    table = lambda ref, blk, h: lax.bitcast_convert_type(ref[blk * FOX_HEADS + h], F32)

    @pl.when((mode & _MODE_FIRST) != 0)
    def _():
        m_ref[...] = jnp.full_like(m_ref, NEG_INF)
        acc_ref[...] = jnp.zeros_like(acc_ref)

    def logits(h):
        return jnp.dot(k_ref[h], qt_ref[h], preferred_element_type=F32)

    def softmax_update(h, s, masked):
        if masked:
            key = lax.broadcasted_iota(jnp.int32, (tk, tq), 0)
            qry = lax.broadcasted_iota(jnp.int32, (tk, tq), 1)
            s = jnp.where(key <= qry, s, NEG_INF)
        bound = table(qn_ref, q_blk, h) * table(kn_ref, k_blk, h)
        if not masked:
            cum_q = jnp.sum(qt_ref[h, FOX_HEAD_DIM:FOX_HEAD_DIM + 3, :].astype(F32), axis=0, keepdims=True)
            bound = cum_q + (bound - table(cl_ref, k_blk, h))
        m_prev = m_ref[h]
        m_new = jnp.maximum(m_prev, bound)
        alpha = jnp.exp2(m_prev - m_new)
        p = jnp.exp2(s - m_new).astype(BF16)
        acc_ref[h] = alpha * acc_ref[h] + jnp.dot(vt_ref[h], p, preferred_element_type=F32)
        m_ref[h] = m_new

    def sweep(heads, masked):
        s_next = logits(heads[0])
        for j, h in enumerate(heads):
            s = s_next
            if j + 1 < len(heads):
                s_next = logits(heads[j + 1])
            softmax_update(h, s, masked)

    @pl.when((mode & _MODE_PAIR) != 0)
    def _():
        base = qi_ref[step] * FOX_HEADS
        n_active = nact_ref[step]
        for count in range(2, FOX_HEADS + 1, 2):
            @pl.when(n_active == count)
            def _():
                sweep([order_ref[base + j] for j in range(count)], False)

    @pl.when((mode & _MODE_DIAG) != 0)
    def _():
        sweep(list(range(FOX_HEADS)), True)
        for c in range(FOX_WIDTH // LANES):
            halves = []
            for hd in (2 * c, 2 * c + 1):
                acc = acc_ref[hd]
                halves.append(acc[:FOX_HEAD_DIM, :] / acc[FOX_HEAD_DIM:FOX_HEAD_DIM + 1, :])
            o_ref[:, c * LANES:(c + 1) * LANES] = jnp.concatenate(halves, axis=0).T.astype(o_ref.dtype)


_MODE_FIRST, _MODE_PAIR, _MODE_DIAG = 1, 2, 4


def _fox_schedule(stats, nq):
    st = stats[:, :5, :FOX_HEADS].reshape(nq, -1, 5, FOX_HEADS)
    qn = jnp.sqrt(jnp.max(st[:, :, 0], axis=1)) * 1.01
    kn = jnp.sqrt(jnp.max(st[:, :, 1], axis=1)) * 1.01
    dmin = jnp.min(st[:, :, 2], axis=1)
    c_first = st[:, 0, 3]
    c_last = st[:, -1, 4]
    bound = qn[:, None] * kn[None, :] + (c_first - dmin)[:, None] - c_last[None, :]
    blk = jnp.arange(nq, dtype=jnp.int32)
    drop = (bound < -FOX_SKIP_NATS) & (blk[None, :] < blk[:, None])[:, :, None]
    prefix = jnp.cumsum(jnp.logical_not(drop).astype(jnp.int32), axis=1) == 0
    kstart_h = jnp.sum(prefix.astype(jnp.int32), axis=1)
    kstart = jnp.min(kstart_h, axis=1)
    hd = jnp.arange(FOX_HEADS, dtype=jnp.int32)
    before = (kstart_h[:, None, :] < kstart_h[:, :, None]) | (
        (kstart_h[:, None, :] == kstart_h[:, :, None]) & (hd[None, None, :] < hd[None, :, None]))
    rank = jnp.sum(before.astype(jnp.int32), axis=2)
    order = jnp.sum(jnp.where(rank[:, None, :] == hd[None, :, None], hd[None, None, :], 0), axis=2)
    count = blk - kstart + 1
    ends = jnp.cumsum(count)
    nsteps = nq * (nq + 1) // 2
    step = jnp.arange(nsteps, dtype=jnp.int32)
    valid = step < ends[-1]
    done = ends[None, :] <= step[:, None]
    q_of = jnp.minimum(jnp.sum(done.astype(jnp.int32), axis=1), nq - 1)
    begin = jnp.sum(jnp.where(done, count[None, :], 0), axis=1)
    k_first = jnp.sum(jnp.where(blk[None, :] == q_of[:, None], kstart[None, :], 0), axis=1)
    k_of = k_first + (step - begin)
    mode = jnp.where(k_of == q_of, _MODE_DIAG, _MODE_PAIR) + jnp.where(k_of == k_first, _MODE_FIRST, 0)
    mode = jnp.where(valid, mode, 0)
    kstart_h_of = jnp.sum(jnp.where((blk[None, :] == q_of[:, None])[:, :, None], kstart_h[None, :, :], 0), axis=1)
    n_active = jnp.sum((kstart_h_of <= k_of[:, None]).astype(jnp.int32), axis=1)
    n_active = jnp.minimum((n_active + 1) // 2 * 2, FOX_HEADS)
    q_of = jnp.where(valid, q_of, nq - 1)
    k_of = jnp.where(valid, k_of, nq - 1)
    as_i32 = lambda a: a.astype(jnp.int32)
    bits = lambda a: lax.bitcast_convert_type(a.astype(F32), jnp.int32).reshape(-1)
    return (as_i32(q_of), as_i32(k_of), as_i32(mode), as_i32(n_active), as_i32(order.reshape(-1)),
            bits(qn * LOG2E), bits(kn), bits(c_last * LOG2E))


def _fox_prompt(qt, kh, vt, stats):
    seq = kh.shape[1]
    assert FOX_TQ == FOX_TK
    nq = seq // FOX_TQ
    tables = _fox_schedule(stats, nq)
    grid_spec = pltpu.PrefetchScalarGridSpec(
        num_scalar_prefetch=len(tables),
        grid=(nq * (nq + 1) // 2,),
        in_specs=[
            pl.BlockSpec((FOX_HEADS, FOX_PAD, FOX_TQ), lambda s, qi, ki, *_: (0, 0, qi[s])),
            pl.BlockSpec((FOX_HEADS, FOX_TK, FOX_PAD), lambda s, qi, ki, *_: (0, ki[s], 0)),
            pl.BlockSpec((FOX_HEADS, FOX_V_ROWS, FOX_TK), lambda s, qi, ki, *_: (0, 0, ki[s])),
        ],
        out_specs=pl.BlockSpec((FOX_TQ, FOX_WIDTH), lambda s, qi, ki, *_: (qi[s], 0)),
        scratch_shapes=[pltpu.VMEM((FOX_HEADS, 1, FOX_TQ), F32),
                        pltpu.VMEM((FOX_HEADS, FOX_V_ROWS, FOX_TQ), F32)],
    )
    return pl.pallas_call(
        _fox_kernel,
        grid_spec=grid_spec,
        out_shape=jax.ShapeDtypeStruct((seq, FOX_WIDTH), BF16),
        compiler_params=pltpu.CompilerParams(dimension_semantics=("arbitrary",), vmem_limit_bytes=VMEM_LIMIT),
        name="fox_prompt",
    )(*tables, qt, kh, vt)


def _lane_cumsum(x):
    n = x.shape[1]
    lane = lax.broadcasted_iota(jnp.int32, x.shape, 1)
    shift = 1
    while shift < n:
        x = x + jnp.where(lane >= shift, pltpu.roll(x, shift, axis=1), 0.0)
        shift *= 2
    return x


def _fox_sample_kernel(q_ref, kn_ref, vn_ref, lft_ref, ck_ref, cv_ref, o_ref,
                       cq_ref, cum_ref, m_ref, l_ref, acc_ref, *, n_past_chunks, tk, t_new):
    c = pl.program_id(1)
    rows = FOX_HEADS * t_new
    heads = [slice(h * FOX_HEAD_DIM, (h + 1) * FOX_HEAD_DIM) for h in range(FOX_HEADS)]
    head_rows = [slice(h * t_new, (h + 1) * t_new) for h in range(FOX_HEADS)]

    @pl.when(c == 0)
    def _():
        cum = _lane_cumsum(lft_ref[0])
        for j in range(n_past_chunks):
            cum_ref[j] = cum[:, j * tk:(j + 1) * tk]
        new_cum = cum[:, n_past_chunks * tk:n_past_chunks * tk + LANES]
        cum_ref[n_past_chunks, :, :LANES] = new_cum
        new_cum_t = jnp.concatenate([new_cum] * (LANES // FOX_HEADS), axis=0).T
        for h in range(FOX_HEADS):
            cq_ref[head_rows[h], :] = jnp.broadcast_to(new_cum_t[:t_new, h:h + 1], (t_new, LANES))
        m_ref[...] = jnp.full_like(m_ref, NEG_INF)
        l_ref[...] = jnp.zeros_like(l_ref)
        acc_ref[...] = jnp.zeros_like(acc_ref)

    nt = (((1,), (1,)), ((), ()))

    def update(qk, pv_of, ck_rows, mask):
        q = q_ref[...]
        s = jnp.concatenate([qk(q[:, heads[h]], h) for h in range(FOX_HEADS)], axis=0)
        s = s + (cq_ref[:, :1] - ck_rows)
        if mask is not None:
            s = jnp.where(mask, s, NEG_INF)
        m_prev = m_ref[...]
        m_new = jnp.maximum(m_prev, jnp.max(s, axis=1, keepdims=True))
        alpha = jnp.exp(m_prev - m_new)
        p = jnp.exp(s - m_new[:, :1])
        l_ref[...] = alpha * l_ref[...] + jnp.sum(p, axis=1, keepdims=True)
        p = p.astype(BF16)
        pv = jnp.concatenate([pv_of(p[head_rows[h]], h) for h in range(FOX_HEADS)], axis=0)
        acc_ref[...] = alpha[:, :FOX_HEAD_DIM] * acc_ref[...] + pv
        m_ref[...] = m_new

    def expand_rows(x, width):
        return jnp.concatenate([jnp.broadcast_to(x[h:h + 1, :], (t_new, width)) for h in range(FOX_HEADS)], axis=0)

    @pl.when(c < n_past_chunks)
    def _():
        update(lambda qh, h: jnp.dot(qh, ck_ref[0, h].astype(BF16), preferred_element_type=F32),
               lambda ph, h: lax.dot_general(ph, cv_ref[0, h].astype(BF16), nt, preferred_element_type=F32),
               expand_rows(cum_ref[c], tk), None)

    @pl.when(c == n_past_chunks)
    def _():
        ck_rows = expand_rows(cum_ref[n_past_chunks, :, :LANES], LANES)[:, :t_new]
        rowt = lax.broadcasted_iota(jnp.int32, (rows, t_new), 0) % t_new
        coli = lax.broadcasted_iota(jnp.int32, (rows, t_new), 1)
        update(lambda qh, h: lax.dot_general(qh, kn_ref[:, heads[h]].astype(BF16), nt, preferred_element_type=F32),
               lambda ph, h: jnp.dot(ph, vn_ref[:, heads[h]].astype(BF16), preferred_element_type=F32),
               ck_rows, coli <= rowt)
        out = acc_ref[...] / l_ref[:, :FOX_HEAD_DIM]
        o_ref[...] = jnp.concatenate([out[head_rows[h]] for h in range(FOX_HEADS)], axis=1).astype(o_ref.dtype)


def _fox_sample(qs, k_new, v_new, lf_all_t, cache_k, cache_v, *, t_new):
    nb, past = cache_k.shape[0], cache_k.shape[3]
    tk = SAMPLE_TK
    npc = past // tk
    rows = FOX_HEADS * t_new
    last = npc - 1
    kern = functools.partial(_fox_sample_kernel, n_past_chunks=npc, tk=tk, t_new=t_new)
    cache_spec = pl.BlockSpec((1, FOX_HEADS, FOX_HEAD_DIM, tk), lambda b, c: (b, 0, 0, jnp.minimum(c, last)))
    return pl.pallas_call(
        kern,
        grid=(nb, npc + 1),
        in_specs=[
            pl.BlockSpec((t_new, FOX_WIDTH), lambda b, c: (b, 0)),
            pl.BlockSpec((t_new, FOX_WIDTH), lambda b, c: (b, 0)),
            pl.BlockSpec((t_new, FOX_WIDTH), lambda b, c: (b, 0)),
            pl.BlockSpec((1, FOX_HEADS, past + LANES), lambda b, c: (b, 0, 0)),
            cache_spec, cache_spec,
        ],
        out_specs=pl.BlockSpec((t_new, FOX_WIDTH), lambda b, c: (b, 0)),
        out_shape=jax.ShapeDtypeStruct((nb * t_new, FOX_WIDTH), BF16),
        scratch_shapes=[
            pltpu.VMEM((rows, LANES), F32),
            pltpu.VMEM((npc + 1, FOX_HEADS, tk), F32),
            pltpu.VMEM((rows, LANES), F32),
            pltpu.VMEM((rows, LANES), F32),
            pltpu.VMEM((rows, FOX_HEAD_DIM), F32),
        ],
        compiler_params=pltpu.CompilerParams(dimension_semantics=("arbitrary", "arbitrary"),
                                             vmem_limit_bytes=VMEM_LIMIT),
        name="fox_sample",
    )(qs, k_new, v_new, lf_all_t, cache_k, cache_v)


def _hgrn_kernel(hq_ref, hk_ref, lfh_ref, hi_ref, hg_ref, s0_ref, norm_ref, tri_ref,
                 o_ref, sout_ref, st_ref, qq_ref, kk_ref, eb_ref, b_ref, oin_ref, od_ref):
    t = pl.program_id(1)
    tc = hq_ref.shape[0]
    blk = min(HG_BLOCK, tc)
    nblk = tc // blk
    heads = [slice(h * HG_DIM, (h + 1) * HG_DIM) for h in range(HG_HEADS)]

    @pl.when(t == 0)
    def _():
        for h in range(HG_HEADS):
            st_ref[h] = s0_ref[0, h].T

    def at_block_row(x, r):
        x3 = x.reshape(nblk, blk, HG_DIM)
        return jnp.broadcast_to(x3[:, r:r + 1, :], x3.shape).reshape(tc, HG_DIM)

    def rel_to_middle(b):
        return b - at_block_row(b, blk // 2 - 1)

    worst = jnp.zeros((), F32)
    for sl in heads:
        b = _sum_by_01_matrix(tri_ref[...], lfh_ref[:, sl])
        eb = jnp.exp(b)
        qq_ref[:, sl] = (hq_ref[:, sl].astype(F32) * eb).astype(BF16)
        kk_ref[:, sl] = (hk_ref[:, sl].astype(F32) * jnp.exp(at_block_row(b, blk - 1) - b)).astype(BF16)
        eb_ref[:, sl] = eb
        b_ref[:, sl] = b
        worst = jnp.maximum(worst, jnp.max(jnp.abs(rel_to_middle(b))))

    def carried_state_and_output():
        for j in range(nblk):
            rows = slice(j * blk, (j + 1) * blk)
            for h, sl in enumerate(heads):
                st = st_ref[h]
                oin_ref[rows, sl] = lax.dot_general(qq_ref[rows, sl], st.astype(BF16), (((1,), (1,)), ((), ())),
                                                    preferred_element_type=F32)
                upd = lax.dot_general(hi_ref[rows, sl], kk_ref[rows, sl], (((0,), (0,)), ((), ())),
                                      preferred_element_type=F32)
                st_ref[h] = st * eb_ref[(j + 1) * blk - 1:(j + 1) * blk, sl] + upd
        for sl in heads:
            o = oin_ref[:, sl] + od_ref[:, sl]
            y = (_rms_scale(o) * norm_ref[...]) * hg_ref[:, sl].astype(F32)
            o_ref[:, sl] = y.astype(o_ref.dtype)

    splittable = worst <= HG_SPLIT_MAX

    @pl.when(splittable)
    def _():
        row = lax.broadcasted_iota(jnp.int32, (tc, tc), 0)
        col = lax.broadcasted_iota(jnp.int32, (tc, tc), 1)
        pair_in_block = (row // blk == col // blk) & (col <= row)
        for sl in heads:
            b_rel = rel_to_middle(b_ref[:, sl])
            qs = (hq_ref[:, sl].astype(F32) * jnp.exp(b_rel)).astype(BF16)
            ks = (hk_ref[:, sl].astype(F32) * jnp.exp(-b_rel)).astype(BF16)
            a = lax.dot_general(qs, ks, (((1,), (1,)), ((), ())), preferred_element_type=F32)
            a = jnp.where(pair_in_block, a, 0.0).astype(BF16)
            od_ref[:, sl] = jnp.dot(a, hi_ref[:, sl], preferred_element_type=F32)
        carried_state_and_output()

    @pl.when(jnp.logical_not(splittable))
    def _():
        row_in_blk = lax.broadcasted_iota(jnp.int32, (tc, HG_DIM), 0) % blk
        for sl in heads:
            q = hq_ref[:, sl].astype(F32)
            k = hk_ref[:, sl].astype(F32)
            v = hi_ref[:, sl].astype(F32)
            b = b_ref[:, sl]

            def lag_step(lag, od):
                k_l = pltpu.roll(k, lag, axis=0)
                b_l = pltpu.roll(b, lag, axis=0)
                v_l = pltpu.roll(v, lag, axis=0)
                w = q * k_l * jnp.exp(jnp.minimum(b - b_l, 0.0))
                w = jnp.where(row_in_blk >= lag, w, 0.0)
                return od + jnp.sum(w, axis=1, keepdims=True) * v_l

            od_ref[:, sl] = lax.fori_loop(1, blk, lag_step, jnp.sum(q * k, axis=1, keepdims=True) * v)
        carried_state_and_output()

    @pl.when(t == pl.num_programs(1) - 1)
    def _():
        for h in range(HG_HEADS):
            sout_ref[0, h] = st_ref[h].T


def _hgrn_cumsum_matrix(n, blk):
    t = np.arange(n)[:, None]
    s = np.arange(n)[None, :]
    return jnp.asarray((((t // blk) == (s // blk)) & (s <= t)).astype(np.float32), BF16)


def _hgrn(hq, hk, lfh, hi, hg, s0, norm, *, nseq):
    rows = hq.shape[0]
    t_len = rows // nseq
    tc = min(HG_TC, t_len)
    nt = t_len // tc
    blk = min(HG_BLOCK, tc)
    tri = _hgrn_cumsum_matrix(tc, blk)
    row = pl.BlockSpec((tc, HG_WIDTH), lambda b, t: (b * nt + t, 0))
    state = pl.BlockSpec((1, HG_HEADS, HG_DIM, HG_DIM), lambda b, t: (b, 0, 0, 0))
    return pl.pallas_call(
        _hgrn_kernel,
        grid=(nseq, nt),
        in_specs=[row, row, row, row, row, state,
                  pl.BlockSpec((1, HG_DIM), lambda b, t: (0, 0)),
                  pl.BlockSpec((tc, tc), lambda b, t: (0, 0))],
        out_specs=[row, state],
        out_shape=[jax.ShapeDtypeStruct((rows, HG_WIDTH), BF16),
                   jax.ShapeDtypeStruct((nseq, HG_HEADS, HG_DIM, HG_DIM), F32)],
        scratch_shapes=[
            pltpu.VMEM((HG_HEADS, HG_DIM, HG_DIM), F32),
            pltpu.VMEM((tc, HG_WIDTH), BF16),
            pltpu.VMEM((tc, HG_WIDTH), BF16),
            pltpu.VMEM((tc, HG_WIDTH), F32),
            pltpu.VMEM((tc, HG_WIDTH), F32),
            pltpu.VMEM((tc, HG_WIDTH), F32),
            pltpu.VMEM((tc, HG_WIDTH), F32),
        ],
        compiler_params=pltpu.CompilerParams(dimension_semantics=("arbitrary", "arbitrary"),
                                             vmem_limit_bytes=VMEM_LIMIT),
        name="hgrn",
    )(hq, hk, lfh, hi, hg, s0, norm, tri)


def _mixffn_kernel(x_ref, of_ref, oh_ref, ga_ref, gb_ref, hist_ref,
                   wbf_ref, wbh_ref, wout_ref, wup_ref, wdn_ref,
                   npost_ref, npre2_ref, npost2_ref, cw_ref, cb_ref,
                   y_ref, conv_ref, tail_ref, *, seg_len):
    tm = x_ref.shape[0]
    br_f = jnp.dot(of_ref[...], wbf_ref[...], preferred_element_type=F32)
    br_h = jnp.dot(oh_ref[...], wbh_ref[...], preferred_element_type=F32)
    merged = ga_ref[...].astype(F32) * br_f + gb_ref[...].astype(F32) * br_h
    mix = jnp.dot(merged.astype(BF16), wout_ref[...], preferred_element_type=F32)
    x1 = x_ref[...] + _rms_scale(mix) * npost_ref[...]

    h2 = (_rms_scale(x1) * npre2_ref[...]).astype(BF16)
    def up_chunk(j):
        w = jnp.concatenate([wup_ref[:, j * FFN_CHUNK:(j + 1) * FFN_CHUNK],
                             wup_ref[:, D_FF + j * FFN_CHUNK:D_FF + (j + 1) * FFN_CHUNK]], axis=1)
        return jnp.dot(h2, w, preferred_element_type=F32)

    n_chunks = D_FF // FFN_CHUNK
    carried = seg_len >= tm
    if carried:
        @pl.when(pl.program_id(0) == 0)
        def _():
            tail_ref[...] = hist_ref[0]
    rowi = lax.broadcasted_iota(jnp.int32, (tm, FFN_CHUNK), 0)
    ff = None
    up_next = up_chunk(0)
    for j in range(n_chunks):
        cols = slice(j * FFN_CHUNK, (j + 1) * FFN_CHUNK)
        up = up_next
        if j + 1 < n_chunks:
            up_next = up_chunk(j + 1)
        a = up[:, :FFN_CHUNK]
        g = up[:, FFN_CHUNK:]
        prev1 = pltpu.roll(a, 1, axis=0)
        prev2 = pltpu.roll(a, 2, axis=0)
        if carried:
            t0 = tail_ref[0:1, cols]
            t1 = tail_ref[1:2, cols]
            prev1 = jnp.where(rowi == 0, t1, prev1)
            prev2 = jnp.where(rowi == 0, t0, jnp.where(rowi == 1, t1, prev2))
            tail_ref[:, cols] = a[tm - 2:, :]
            conv_ref[0, :, cols] = a[tm - 2:, :]
        else:
            for s in range(tm // seg_len):
                h0 = hist_ref[s, 0:1, cols]
                h1 = hist_ref[s, 1:2, cols]
                prev1 = jnp.where(rowi == s * seg_len, h1, prev1)
                prev2 = jnp.where(rowi == s * seg_len, h0, jnp.where(rowi == s * seg_len + 1, h1, prev2))
                conv_ref[s, :, cols] = a[(s + 1) * seg_len - 2:(s + 1) * seg_len, :]
        c = cb_ref[:, cols] + cw_ref[0:1, cols] * prev2 + cw_ref[1:2, cols] * prev1 + cw_ref[2:3, cols] * a
        act = (jax.nn.gelu(c, approximate=True) * g).astype(BF16)
        part = jnp.dot(act, wdn_ref[cols, :], preferred_element_type=F32)
        ff = part if ff is None else ff + part
    y_ref[...] = x1 + _rms_scale(ff) * npost2_ref[...]


def _mixffn(x, o_fox, o_hg, ga, gb, hist, w, *, seg_len):
    rows = x.shape[0]
    tm = min(FFN_TM, rows)
    n = rows // tm
    nseg = hist.shape[0]
    row = lambda width: pl.BlockSpec((tm, width), lambda i: (i, 0))
    weights = [w["bf"], w["bh"], w["out"], w["up"], w["down"]]
    smalls = [w["npost"], w["npre2"], w["npost2"], w["conv_w"], w["conv_b"]]
    hist_spec = pl.BlockSpec(hist.shape, lambda i: (0, 0, 0))
    scratch = [pltpu.VMEM((2, D_FF), F32)]
    return pl.pallas_call(
        functools.partial(_mixffn_kernel, seg_len=seg_len),
        grid=(n,),
        in_specs=[row(D_MODEL), row(FOX_WIDTH), row(HG_WIDTH), row(D_MODEL), row(D_MODEL), hist_spec]
                 + [_const_spec(a.shape) for a in weights] + [_const_spec(a.shape) for a in smalls],
        out_specs=[row(D_MODEL), pl.BlockSpec((nseg, 2, D_FF), lambda i: (0, 0, 0))],
        out_shape=[jax.ShapeDtypeStruct((rows, D_MODEL), F32), jax.ShapeDtypeStruct((nseg, 2, D_FF), F32)],
        scratch_shapes=scratch,
        compiler_params=pltpu.CompilerParams(dimension_semantics=("arbitrary",), vmem_limit_bytes=VMEM_LIMIT),
        name="mixffn",
    )(x, o_fox, o_hg, ga, gb, hist, *weights, *smalls)


def _prep_w_in(w_in, fox_f_bias):
    offs = np.cumsum([0] + IN_SIZES)
    seg = [w_in[:, int(offs[i]):int(offs[i + 1])] for i in range(len(IN_SIZES))]
    pad = jnp.zeros((D_MODEL, LANES - 3 * FOX_HEADS), w_in.dtype)
    f3 = jnp.concatenate([seg[3], seg[3], seg[3], pad], axis=1)
    w_all = jnp.concatenate(seg[:3] + [f3] + seg[4:], axis=1).astype(BF16)
    fb = fox_f_bias.astype(F32)
    fb3 = jnp.concatenate([fb, fb, fb, jnp.zeros((LANES - 3 * FOX_HEADS,), F32)]).reshape(1, LANES)
    return w_all, fb3


def kernel(x_prompt, x_sample, cache_fox_k, cache_fox_v, cache_fox_logf, state_hgrn, state_ffn_conv, norm_mix_pre, norm_mix_post, w_in, fox_f_bias, hgrn_lb_logits, hgrn_norm, w_branch_fox, w_branch_hgrn, w_out, norm_ffn_pre, norm_ffn_post, w_up, ffn_conv_w, ffn_conv_b, w_down):
    depth = w_in.shape[0]
    assert depth == 1 and hgrn_lb_logits.shape[0] == 2
    bp, seq, _ = x_prompt.shape
    assert bp == 1
    nb, t_new, _ = x_sample.shape
    past = cache_fox_k.shape[2]

    w_all, fb3 = _prep_w_in(w_in[0], fox_f_bias[0])
    g_pre = norm_mix_pre[0].reshape(1, D_MODEL)
    lbl = hgrn_lb_logits.astype(F32)
    hnorm = hgrn_norm[0].astype(F32).reshape(1, HG_DIM)
    w = {
        "bf": w_branch_fox[0].astype(BF16), "bh": w_branch_hgrn[0].astype(BF16), "out": w_out[0].astype(BF16),
        "up": w_up[0].astype(BF16), "down": w_down[0].astype(BF16),
        "npost": norm_mix_post[0].reshape(1, D_MODEL), "npre2": norm_ffn_pre[0].reshape(1, D_MODEL),
        "npost2": norm_ffn_post[0].reshape(1, D_MODEL),
        "conv_w": ffn_conv_w[0], "conv_b": ffn_conv_b[0].reshape(1, D_FF),
    }

    xp = x_prompt.reshape(seq, D_MODEL)
    (qt, kh, vt, stats, pk, pv, plf, hq, hk, lfh, hi, hg, ga, gb) = _proj(xp, g_pre, w_all, fb3, lbl, fold=True)
    o_fox = _fox_prompt(qt, kh, vt, stats)
    s0 = jnp.zeros((1, HG_HEADS, HG_DIM, HG_DIM), F32)
    o_hg, p_state = _hgrn(hq, hk, lfh, hi, hg, s0, hnorm, nseq=1)
    hist0 = jnp.zeros((1, 2, D_FF), F32)
    yp, pconv = _mixffn(xp, o_fox, o_hg, ga, gb, hist0, w, seg_len=seq)

    xs = x_sample.reshape(nb * t_new, D_MODEL)
    (qs, sk, sv, slf, hq, hk, lfh, hi, hg, ga, gb) = _proj(xs, g_pre, w_all, fb3, lbl, fold=False)
    lf_all_t = jnp.concatenate([
        jnp.swapaxes(cache_fox_logf[0].astype(F32), 1, 2),
        jnp.swapaxes(slf.reshape(nb, t_new, FOX_HEADS), 1, 2),
        jnp.zeros((nb, FOX_HEADS, LANES - t_new), F32)], axis=2)
    cache_kt = jnp.transpose(cache_fox_k[0], (0, 2, 3, 1))
    cache_vt = jnp.transpose(cache_fox_v[0], (0, 2, 3, 1))
    o_fox_s = _fox_sample(qs, sk, sv, lf_all_t, cache_kt, cache_vt, t_new=t_new)
    o_hg_s, s_state = _hgrn(hq, hk, lfh, hi, hg, state_hgrn[0].astype(F32), hnorm, nseq=nb)
    ys, sconv = _mixffn(xs, o_fox_s, o_hg_s, ga, gb, state_ffn_conv[0], w, seg_len=t_new)

    return (
        yp.reshape(bp, seq, D_MODEL),
        ys.reshape(nb, t_new, D_MODEL),
        pk.reshape(1, bp, seq, FOX_HEADS, FOX_HEAD_DIM),
        pv.reshape(1, bp, seq, FOX_HEADS, FOX_HEAD_DIM),
        plf.reshape(1, bp, seq, FOX_HEADS),
        p_state.reshape(1, bp, HG_HEADS, HG_DIM, HG_DIM),
        pconv.reshape(1, bp, 2, D_FF),
        sk.reshape(1, nb, t_new, FOX_HEADS, FOX_HEAD_DIM),
        sv.reshape(1, nb, t_new, FOX_HEADS, FOX_HEAD_DIM),
        slf.reshape(1, nb, t_new, FOX_HEADS),
        s_state.reshape(1, nb, HG_HEADS, HG_DIM, HG_DIM),
        sconv.reshape(1, nb, 2, D_FF),
    )
```

```python
import functools

import numpy as np
import jax
import jax.numpy as jnp
from jax import lax
from jax.experimental import pallas as pl
from jax.experimental.pallas import tpu as pltpu

F32 = jnp.float32
BF16 = jnp.bfloat16

D_MODEL = 1024
FOX_HEADS = 8
FOX_HEAD_DIM = 64
FOX_WIDTH = FOX_HEADS * FOX_HEAD_DIM
HG_HEADS = 4
HG_DIM = 128
HG_WIDTH = HG_HEADS * HG_DIM
D_FF = 2816
RMS_EPS = 1e-6
NEG_INF = -1e30
LOG2E = 1.4426950408889634
FOX_SKIP_NATS = 110.0
IN_SIZES = [FOX_WIDTH, FOX_WIDTH, FOX_WIDTH, FOX_HEADS, HG_WIDTH, HG_WIDTH, HG_WIDTH, HG_WIDTH, D_MODEL, D_MODEL]

LANES = 128
FOX_PAD = 2 * FOX_HEAD_DIM
FOX_V_ROWS = FOX_HEAD_DIM + 16
HG_BLOCK = 64
HG_SPLIT_MAX = 60.0
VMEM_LIMIT = 56 * 1024 * 1024

PROJ_TM = 256
FOX_TQ = 512
FOX_TK = 512
HG_TC = 256
FFN_TM = 256
FFN_CHUNK = 256
SAMPLE_TK = 1024

_C_Q, _C_K, _C_V, _C_F = 0, 512, 1024, 1536
_C_HQ, _C_HF, _C_HI, _C_HG = 1664, 2176, 2688, 3200
_C_GA, _C_GB, _C_END = 3712, 4736, 5760


def _split3(x):
    hi = x.astype(BF16)
    r = x - hi.astype(F32)
    mid = r.astype(BF16)
    lo = (r - mid.astype(F32)).astype(BF16)
    return hi, mid, lo


def _sum_by_01_matrix(mat01, x):
    cat = jnp.concatenate(_split3(x), axis=1)
    y = jnp.dot(mat01, cat, preferred_element_type=F32)
    return y[:, :LANES] + y[:, LANES:2 * LANES] + y[:, 2 * LANES:]


def _rms_scale(x):
    return x * lax.rsqrt(jnp.mean(x * x, axis=-1, keepdims=True) + RMS_EPS)


def _log_sigmoid(x):
    return jnp.minimum(x, 0.0) - jnp.log1p(jnp.exp(-jnp.abs(x)))


def _sigmoid(x):
    return 1.0 / (1.0 + jnp.exp(-x))


def _proj_kernel(*refs, fold):
    if fold:
        (x_ref, g_ref, w_ref, fb_ref, lbl_ref, tri_ref, pq_ref, pk_ref, cq_ref, ck_ref, cv_ref, seg_ref,
         qh_ref, kh_ref, vh_ref, stat_ref, kout_ref, vout_ref, lf_ref, hq_ref, hk_ref, lfh_ref, hi_ref, hg_ref,
         ga_ref, gb_ref, carry_ref) = refs
    else:
        (x_ref, g_ref, w_ref, fb_ref, lbl_ref,
         qs_ref, kout_ref, vout_ref, lf_ref, hq_ref, hk_ref, lfh_ref, hi_ref, hg_ref,
         ga_ref, gb_ref) = refs

    h = (_rms_scale(x_ref[...]) * g_ref[...]).astype(BF16)

    z = jnp.dot(h, w_ref[...], preferred_element_type=F32)

    zq = z[:, _C_Q:_C_K] * (FOX_HEAD_DIM ** -0.5)
    zk = z[:, _C_K:_C_V]
    zv = z[:, _C_V:_C_F]
    kout_ref[...] = zk
    vout_ref[...] = zv
    logf = _log_sigmoid(z[:, _C_F:_C_HQ] + fb_ref[...])
    lf_ref[...] = logf[:, :FOX_HEADS]

    l0 = lbl_ref[0:1, :]
    l1 = lbl_ref[1:2, :]
    lmax = jnp.maximum(l0, l1)
    e0 = jnp.exp(l0 - lmax)
    lb = e0 / (e0 + jnp.exp(l1 - lmax))
    f = lb + (1.0 - lb) * _sigmoid(z[:, _C_HF:_C_HI])
    hq_ref[...] = z[:, _C_HQ:_C_HF].astype(BF16)
    hk_ref[...] = (1.0 - f).astype(BF16)
    lfh_ref[...] = jnp.log(f)
    hi_ref[...] = z[:, _C_HI:_C_HG].astype(BF16)
    hg_ref[...] = _sigmoid(z[:, _C_HG:_C_GA]).astype(BF16)
    ga_ref[...] = _sigmoid(z[:, _C_GA:_C_GB]).astype(BF16)
    gb_ref[...] = _sigmoid(z[:, _C_GB:_C_END]).astype(BF16)

    if not fold:
        qs_ref[...] = zq.astype(BF16)
        return

    @pl.when(pl.program_id(0) == 0)
    def _():
        carry_ref[...] = jnp.zeros_like(carry_ref)

    cum = carry_ref[...] + _sum_by_01_matrix(tri_ref[...], logf)
    carry_ref[...] = cum[-1:, :]

    seg = seg_ref[...]
    qn2 = jnp.dot((zq * zq).astype(BF16), seg, preferred_element_type=F32)
    kn2 = jnp.dot((zk * zk).astype(BF16), seg, preferred_element_type=F32)
    dg = jnp.dot((zq * zk).astype(BF16), seg, preferred_element_type=F32)
    stat_ref[0, 0:1, :] = jnp.max(qn2, axis=0, keepdims=True)
    stat_ref[0, 1:2, :] = jnp.max(kn2, axis=0, keepdims=True)
    stat_ref[0, 2:3, :] = jnp.min(dg, axis=0, keepdims=True)
    stat_ref[0, 3:4, :] = cum[0:1, :]
    stat_ref[0, 4:5, :] = cum[-1:, :]
    stat_ref[0, 5:8, :] = jnp.zeros((3, LANES), F32)

    zq = zq * LOG2E
    c_hi, c_mid, c_lo = _split3(cum * LOG2E)
    lane = lax.broadcasted_iota(jnp.int32, cum.shape, 1)
    pieces = jnp.where(lane < 8, c_hi, jnp.where(lane < 16, c_mid, c_lo))
    pieces = jnp.where(lane < 24, pieces, jnp.zeros_like(pieces))
    ex_q = jnp.dot(pieces, pq_ref[...], preferred_element_type=F32) + cq_ref[...]
    ex_k = jnp.dot(pieces, pk_ref[...], preferred_element_type=F32) + ck_ref[...]
    ex_v = cv_ref[...]

    low = lax.broadcasted_iota(jnp.int32, (zq.shape[0], LANES), 1) < FOX_HEAD_DIM
    for src, ex, dst, transposed in ((zq, ex_q, qh_ref, True), (zk, ex_k, kh_ref, False), (zv, ex_v, vh_ref, True)):
        for c in range(FOX_WIDTH // LANES):
            pair = src[:, c * LANES:(c + 1) * LANES]
            swapped = pltpu.roll(pair, FOX_HEAD_DIM, axis=1)
            for j, data in enumerate((pair, swapped)):
                hd = 2 * c + j
                blk = jnp.where(low, data, ex[:, hd * LANES:(hd + 1) * LANES])
                dst[hd] = (blk.T[:dst.shape[1]] if transposed else blk).astype(BF16)


def _bias_fold_constants():
    pq = np.zeros((LANES, FOX_HEADS * LANES), np.float32)
    pk = np.zeros((LANES, FOX_HEADS * LANES), np.float32)
    cq = np.zeros((1, FOX_HEADS * LANES), np.float32)
    ck = np.zeros((1, FOX_HEADS * LANES), np.float32)
    cv = np.zeros((1, FOX_HEADS * LANES), np.float32)
    for h in range(FOX_HEADS):
        base = h * LANES + FOX_HEAD_DIM
        for p in range(3):
            pq[p * 8 + h, base + p] = 1.0
            ck[0, base + p] = 1.0
            pk[p * 8 + h, base + 3 + p] = -1.0
            cq[0, base + 3 + p] = 1.0
        cv[0, base] = 1.0
    return (jnp.asarray(pq, BF16), jnp.asarray(pk, BF16), jnp.asarray(cq), jnp.asarray(ck), jnp.asarray(cv))


def _const_spec(shape, single=True):
    nd = len(shape)
    if single:
        return pl.BlockSpec(shape, lambda *_: (0,) * nd, pipeline_mode=pl.Buffered(1))
    return pl.BlockSpec(shape, lambda *_: (0,) * nd)


def _proj(x, gain, w_all, fb3, lb_logits, *, fold):
    rows = x.shape[0]
    tm = min(PROJ_TM, rows)
    n = rows // tm
    row = lambda width: pl.BlockSpec((tm, width), lambda i: (i, 0))
    in_specs = [row(D_MODEL), _const_spec((1, D_MODEL)), _const_spec(w_all.shape), _const_spec((1, LANES)),
                _const_spec(lb_logits.shape)]
    args = [x, gain, w_all, fb3, lb_logits]
    common_out = [
        (jax.ShapeDtypeStruct((rows, FOX_WIDTH), F32), row(FOX_WIDTH)),
        (jax.ShapeDtypeStruct((rows, FOX_WIDTH), F32), row(FOX_WIDTH)),
        (jax.ShapeDtypeStruct((rows, FOX_HEADS), F32), row(FOX_HEADS)),
        (jax.ShapeDtypeStruct((rows, HG_WIDTH), BF16), row(HG_WIDTH)),
        (jax.ShapeDtypeStruct((rows, HG_WIDTH), BF16), row(HG_WIDTH)),
        (jax.ShapeDtypeStruct((rows, HG_WIDTH), F32), row(HG_WIDTH)),
        (jax.ShapeDtypeStruct((rows, HG_WIDTH), BF16), row(HG_WIDTH)),
        (jax.ShapeDtypeStruct((rows, HG_WIDTH), BF16), row(HG_WIDTH)),
        (jax.ShapeDtypeStruct((rows, D_MODEL), BF16), row(D_MODEL)),
        (jax.ShapeDtypeStruct((rows, D_MODEL), BF16), row(D_MODEL)),
    ]
    scratch = []
    if fold:
        tri = jnp.asarray(np.tril(np.ones((tm, tm), np.float32)), BF16)
        seg = np.zeros((FOX_WIDTH, LANES), np.float32)
        seg[np.arange(FOX_WIDTH), np.arange(FOX_WIDTH) // FOX_HEAD_DIM] = 1.0
        consts = _bias_fold_constants() + (jnp.asarray(seg, BF16),)
        in_specs += [_const_spec(tri.shape)] + [_const_spec(c.shape) for c in consts]
        args += [tri, *consts]
        head_major = (jax.ShapeDtypeStruct((FOX_HEADS, rows, FOX_PAD), BF16),
                      pl.BlockSpec((FOX_HEADS, tm, FOX_PAD), lambda i: (0, i, 0)))
        head_major_t = lambda depth: (jax.ShapeDtypeStruct((FOX_HEADS, depth, rows), BF16),
                                      pl.BlockSpec((FOX_HEADS, depth, tm), lambda i: (0, 0, i)))
        stats = (jax.ShapeDtypeStruct((n, 8, LANES), F32), pl.BlockSpec((1, 8, LANES), lambda i: (i, 0, 0)))
        outs = [head_major_t(FOX_PAD), head_major, head_major_t(FOX_V_ROWS), stats] + common_out
        scratch = [pltpu.VMEM((1, LANES), F32)]
    else:
        outs = [(jax.ShapeDtypeStruct((rows, FOX_WIDTH), BF16), row(FOX_WIDTH))] + common_out
    return pl.pallas_call(
        functools.partial(_proj_kernel, fold=fold),
        grid=(n,),
        in_specs=in_specs,
        out_specs=[o[1] for o in outs],
        out_shape=[o[0] for o in outs],
        scratch_shapes=scratch,
        compiler_params=pltpu.CompilerParams(dimension_semantics=("arbitrary",), vmem_limit_bytes=VMEM_LIMIT),
        name="proj_fold" if fold else "proj",
    )(*args)


def _fox_kernel(ks_ref, nact_ref, order_ref, qn_ref, kn_ref, cl_ref, qt_ref, k_hbm, vt_hbm, o_ref,
                kbuf, vbuf, sem, slot0_ref, m_ref, acc_ref, *, nq):
    q_blk = pl.program_id(0)
    k_first = ks_ref[q_blk]
    tq = qt_ref.shape[2]
    tk = kbuf.shape[2]
    table = lambda ref, blk, h: lax.bitcast_convert_type(ref[blk * FOX_HEADS + h], F32)

    def copies(k_blk, slot):
        rows = pl.ds(pl.multiple_of(k_blk * tk, tk), tk)
        return (pltpu.make_async_copy(k_hbm.at[:, rows, :], kbuf.at[slot], sem.at[0, slot]),
                pltpu.make_async_copy(vt_hbm.at[:, :, rows], vbuf.at[slot], sem.at[1, slot]))

    def start(k_blk, slot):
        for c in copies(k_blk, slot):
            c.start()

    def wait(k_blk, slot):
        for c in copies(k_blk, slot):
            c.wait()

    @pl.when(q_blk == 0)
    def _():
        slot0_ref[0] = 0
        start(k_first, 0)

    slot0 = slot0_ref[0]
    slot_of = lambda k_blk: (slot0 + (k_blk - k_first)) & 1
    m_ref[...] = jnp.full_like(m_ref, NEG_INF)
    acc_ref[...] = jnp.zeros_like(acc_ref)

    def logits(h, slot):
        return jnp.dot(kbuf[slot, h], qt_ref[h], preferred_element_type=F32)

    def softmax_update(h, s, k_blk, slot, masked):
        if masked:
            key = lax.broadcasted_iota(jnp.int32, (tk, tq), 0)
            qry = lax.broadcasted_iota(jnp.int32, (tk, tq), 1)
            s = jnp.where(key <= qry, s, NEG_INF)
        bound = table(qn_ref, q_blk, h) * table(kn_ref, k_blk, h)
        if not masked:
            cum_q = jnp.sum(qt_ref[h, FOX_HEAD_DIM:FOX_HEAD_DIM + 3, :].astype(F32), axis=0, keepdims=True)
            bound = cum_q + (bound - table(cl_ref, k_blk, h))
        m_prev = m_ref[h]
        m_new = jnp.maximum(m_prev, bound)
        alpha = jnp.exp2(m_prev - m_new)
        p = jnp.exp2(s - m_new).astype(BF16)
        acc_ref[h] = alpha * acc_ref[h] + jnp.dot(vbuf[slot, h], p, preferred_element_type=F32)
        m_ref[h] = m_new

    def sweep(heads, k_blk, slot, masked):
        s_next = logits(heads[0], slot)
        for j, h in enumerate(heads):
            s = s_next
            if j + 1 < len(heads):
                s_next = logits(heads[j + 1], slot)
            softmax_update(h, s, k_blk, slot, masked)

    def past_block(k_blk, carry):
        slot = slot_of(k_blk)
        wait(k_blk, slot)
        start(k_blk + 1, 1 - slot)
        n_active = nact_ref[q_blk * nq + k_blk]
        for count in range(2, FOX_HEADS + 1, 2):
            @pl.when(n_active == count)
            def _():
                sweep([order_ref[q_blk * FOX_HEADS + j] for j in range(count)], k_blk, slot, False)
        return carry

    lax.fori_loop(k_first, q_blk, past_block, 0)

    slot = slot_of(q_blk)
    wait(q_blk, slot)

    @pl.when(q_blk + 1 < nq)
    def _():
        start(ks_ref[jnp.minimum(q_blk + 1, nq - 1)], 1 - slot)
    slot0_ref[0] = 1 - slot

    sweep(list(range(FOX_HEADS)), q_blk, slot, True)
    for c in range(FOX_WIDTH // LANES):
        halves = []
        for hd in (2 * c, 2 * c + 1):
            acc = acc_ref[hd]
            halves.append(acc[:FOX_HEAD_DIM, :] / acc[FOX_HEAD_DIM:FOX_HEAD_DIM + 1, :])
        o_ref[:, c * LANES:(c + 1) * LANES] = jnp.concatenate(halves, axis=0).T.astype(o_ref.dtype)


def _fox_schedule(stats, nq):
    st = stats[:, :5, :FOX_HEADS].reshape(nq, -1, 5, FOX_HEADS)
    qn = jnp.sqrt(jnp.max(st[:, :, 0], axis=1)) * 1.01
    kn = jnp.sqrt(jnp.max(st[:, :, 1], axis=1)) * 1.01
    dmin = jnp.min(st[:, :, 2], axis=1)
    c_first = st[:, 0, 3]
    c_last = st[:, -1, 4]
    bound = qn[:, None] * kn[None, :] + (c_first - dmin)[:, None] - c_last[None, :]
    blk = jnp.arange(nq, dtype=jnp.int32)
    drop = (bound < -FOX_SKIP_NATS) & (blk[None, :] < blk[:, None])[:, :, None]
    prefix = jnp.cumsum(jnp.logical_not(drop).astype(jnp.int32), axis=1) == 0
    kstart_h = jnp.sum(prefix.astype(jnp.int32), axis=1)
    kstart = jnp.min(kstart_h, axis=1)
    hd = jnp.arange(FOX_HEADS, dtype=jnp.int32)
    before = (kstart_h[:, None, :] < kstart_h[:, :, None]) | (
        (kstart_h[:, None, :] == kstart_h[:, :, None]) & (hd[None, None, :] < hd[None, :, None]))
    rank = jnp.sum(before.astype(jnp.int32), axis=2)
    order = jnp.sum(jnp.where(rank[:, None, :] == hd[None, :, None], hd[None, None, :], 0), axis=2)
    n_active = jnp.sum((kstart_h[:, None, :] <= blk[None, :, None]).astype(jnp.int32), axis=2)
    n_active = jnp.minimum((n_active + 1) // 2 * 2, FOX_HEADS)
    as_i32 = lambda a: a.astype(jnp.int32).reshape(-1)
    bits = lambda a: lax.bitcast_convert_type(a.astype(F32), jnp.int32).reshape(-1)
    return (as_i32(kstart), as_i32(n_active), as_i32(order), bits(qn * LOG2E), bits(kn), bits(c_last * LOG2E))


def _fox_prompt(qt, kh, vt, stats):
    seq = kh.shape[1]
    assert FOX_TQ == FOX_TK
    nq = seq // FOX_TQ
    tables = _fox_schedule(stats, nq)
    grid_spec = pltpu.PrefetchScalarGridSpec(
        num_scalar_prefetch=len(tables),
        grid=(nq,),
        in_specs=[
            pl.BlockSpec((FOX_HEADS, FOX_PAD, FOX_TQ), lambda q, *_: (0, 0, q)),
            pl.BlockSpec(memory_space=pl.ANY),
            pl.BlockSpec(memory_space=pl.ANY),
        ],
        out_specs=pl.BlockSpec((FOX_TQ, FOX_WIDTH), lambda q, *_: (q, 0)),
        scratch_shapes=[pltpu.VMEM((2, FOX_HEADS, FOX_TK, FOX_PAD), BF16),
                        pltpu.VMEM((2, FOX_HEADS, FOX_V_ROWS, FOX_TK), BF16),
                        pltpu.SemaphoreType.DMA((2, 2)),
                        pltpu.SMEM((1,), jnp.int32),
                        pltpu.VMEM((FOX_HEADS, 1, FOX_TQ), F32),
                        pltpu.VMEM((FOX_HEADS, FOX_V_ROWS, FOX_TQ), F32)],
    )
    return pl.pallas_call(
        functools.partial(_fox_kernel, nq=nq),
        grid_spec=grid_spec,
        out_shape=jax.ShapeDtypeStruct((seq, FOX_WIDTH), BF16),
        compiler_params=pltpu.CompilerParams(dimension_semantics=("arbitrary",), vmem_limit_bytes=VMEM_LIMIT),
        name="fox_prompt",
    )(*tables, qt, kh, vt)


def _lane_cumsum(x):
    n = x.shape[1]
    lane = lax.broadcasted_iota(jnp.int32, x.shape, 1)
    shift = 1
    while shift < n:
        x = x + jnp.where(lane >= shift, pltpu.roll(x, shift, axis=1), 0.0)
        shift *= 2
    return x


def _fox_sample_kernel(q_ref, kn_ref, vn_ref, lft_ref, ck_ref, cv_ref, o_ref,
                       cq_ref, cum_ref, m_ref, l_ref, acc_ref, *, n_past_chunks, tk, t_new):
    c = pl.program_id(1)
    rows = FOX_HEADS * t_new
    heads = [slice(h * FOX_HEAD_DIM, (h + 1) * FOX_HEAD_DIM) for h in range(FOX_HEADS)]
    head_rows = [slice(h * t_new, (h + 1) * t_new) for h in range(FOX_HEADS)]

    @pl.when(c == 0)
    def _():
        cum = _lane_cumsum(lft_ref[0])
        for j in range(n_past_chunks):
            cum_ref[j] = cum[:, j * tk:(j + 1) * tk]
        new_cum = cum[:, n_past_chunks * tk:n_past_chunks * tk + LANES]
        cum_ref[n_past_chunks, :, :LANES] = new_cum
        new_cum_t = jnp.concatenate([new_cum] * (LANES // FOX_HEADS), axis=0).T
        for h in range(FOX_HEADS):
            cq_ref[head_rows[h], :] = jnp.broadcast_to(new_cum_t[:t_new, h:h + 1], (t_new, LANES))
        m_ref[...] = jnp.full_like(m_ref, NEG_INF)
        l_ref[...] = jnp.zeros_like(l_ref)
        acc_ref[...] = jnp.zeros_like(acc_ref)

    nt = (((1,), (1,)), ((), ()))

    def update(qk, pv_of, ck_rows, mask):
        q = q_ref[...]
        s = jnp.concatenate([qk(q[:, heads[h]], h) for h in range(FOX_HEADS)], axis=0)
        s = s + (cq_ref[:, :1] - ck_rows)
        if mask is not None:
            s = jnp.where(mask, s, NEG_INF)
        m_prev = m_ref[...]
        m_new = jnp.maximum(m_prev, jnp.max(s, axis=1, keepdims=True))
        alpha = jnp.exp(m_prev - m_new)
        p = jnp.exp(s - m_new[:, :1])
        l_ref[...] = alpha * l_ref[...] + jnp.sum(p, axis=1, keepdims=True)
        p = p.astype(BF16)
        pv = jnp.concatenate([pv_of(p[head_rows[h]], h) for h in range(FOX_HEADS)], axis=0)
        acc_ref[...] = alpha[:, :FOX_HEAD_DIM] * acc_ref[...] + pv
        m_ref[...] = m_new

    def expand_rows(x, width):
        return jnp.concatenate([jnp.broadcast_to(x[h:h + 1, :], (t_new, width)) for h in range(FOX_HEADS)], axis=0)

    @pl.when(c < n_past_chunks)
    def _():
        update(lambda qh, h: jnp.dot(qh, ck_ref[0, h].astype(BF16), preferred_element_type=F32),
               lambda ph, h: lax.dot_general(ph, cv_ref[0, h].astype(BF16), nt, preferred_element_type=F32),
               expand_rows(cum_ref[c], tk), None)

    @pl.when(c == n_past_chunks)
    def _():
        ck_rows = expand_rows(cum_ref[n_past_chunks, :, :LANES], LANES)[:, :t_new]
        rowt = lax.broadcasted_iota(jnp.int32, (rows, t_new), 0) % t_new
        coli = lax.broadcasted_iota(jnp.int32, (rows, t_new), 1)
        update(lambda qh, h: lax.dot_general(qh, kn_ref[:, heads[h]].astype(BF16), nt, preferred_element_type=F32),
               lambda ph, h: jnp.dot(ph, vn_ref[:, heads[h]].astype(BF16), preferred_element_type=F32),
               ck_rows, coli <= rowt)
        out = acc_ref[...] / l_ref[:, :FOX_HEAD_DIM]
        o_ref[...] = jnp.concatenate([out[head_rows[h]] for h in range(FOX_HEADS)], axis=1).astype(o_ref.dtype)


def _fox_sample(qs, k_new, v_new, lf_all_t, cache_k, cache_v, *, t_new):
    nb, past = cache_k.shape[0], cache_k.shape[3]
    tk = SAMPLE_TK
    npc = past // tk
    rows = FOX_HEADS * t_new
    last = npc - 1
    kern = functools.partial(_fox_sample_kernel, n_past_chunks=npc, tk=tk, t_new=t_new)
    cache_spec = pl.BlockSpec((1, FOX_HEADS, FOX_HEAD_DIM, tk), lambda b, c: (b, 0, 0, jnp.minimum(c, last)))
    return pl.pallas_call(
        kern,
        grid=(nb, npc + 1),
        in_specs=[
            pl.BlockSpec((t_new, FOX_WIDTH), lambda b, c: (b, 0)),
            pl.BlockSpec((t_new, FOX_WIDTH), lambda b, c: (b, 0)),
            pl.BlockSpec((t_new, FOX_WIDTH), lambda b, c: (b, 0)),
            pl.BlockSpec((1, FOX_HEADS, past + LANES), lambda b, c: (b, 0, 0)),
            cache_spec, cache_spec,
        ],
        out_specs=pl.BlockSpec((t_new, FOX_WIDTH), lambda b, c: (b, 0)),
        out_shape=jax.ShapeDtypeStruct((nb * t_new, FOX_WIDTH), BF16),
        scratch_shapes=[
            pltpu.VMEM((rows, LANES), F32),
            pltpu.VMEM((npc + 1, FOX_HEADS, tk), F32),
            pltpu.VMEM((rows, LANES), F32),
            pltpu.VMEM((rows, LANES), F32),
            pltpu.VMEM((rows, FOX_HEAD_DIM), F32),
        ],
        compiler_params=pltpu.CompilerParams(dimension_semantics=("arbitrary", "arbitrary"),
                                             vmem_limit_bytes=VMEM_LIMIT),
        name="fox_sample",
    )(qs, k_new, v_new, lf_all_t, cache_k, cache_v)


def _hgrn_kernel(hq_ref, hk_ref, lfh_ref, hi_ref, hg_ref, s0_ref, norm_ref, tri_ref,
                 o_ref, sout_ref, st_ref, qq_ref, kk_ref, eb_ref, b_ref, oin_ref, od_ref):
    t = pl.program_id(1)
    tc = hq_ref.shape[0]
    blk = min(HG_BLOCK, tc)
    nblk = tc // blk
    heads = [slice(h * HG_DIM, (h + 1) * HG_DIM) for h in range(HG_HEADS)]

    @pl.when(t == 0)
    def _():
        for h in range(HG_HEADS):
            st_ref[h] = s0_ref[0, h].T

    def at_block_row(x, r):
        x3 = x.reshape(nblk, blk, HG_DIM)
        return jnp.broadcast_to(x3[:, r:r + 1, :], x3.shape).reshape(tc, HG_DIM)

    def rel_to_middle(b):
        return b - at_block_row(b, blk // 2 - 1)

    worst = jnp.zeros((), F32)
    for sl in heads:
        b = _sum_by_01_matrix(tri_ref[...], lfh_ref[:, sl])
        eb = jnp.exp(b)
        qq_ref[:, sl] = (hq_ref[:, sl].astype(F32) * eb).astype(BF16)
        kk_ref[:, sl] = (hk_ref[:, sl].astype(F32) * jnp.exp(at_block_row(b, blk - 1) - b)).astype(BF16)
        eb_ref[:, sl] = eb
        b_ref[:, sl] = b
        worst = jnp.maximum(worst, jnp.max(jnp.abs(rel_to_middle(b))))

    def carried_state_and_output():
        for j in range(nblk):
            rows = slice(j * blk, (j + 1) * blk)
            for h, sl in enumerate(heads):
                st = st_ref[h]
                oin_ref[rows, sl] = lax.dot_general(qq_ref[rows, sl], st.astype(BF16), (((1,), (1,)), ((), ())),
                                                    preferred_element_type=F32)
                upd = lax.dot_general(hi_ref[rows, sl], kk_ref[rows, sl], (((0,), (0,)), ((), ())),
                                      preferred_element_type=F32)
                st_ref[h] = st * eb_ref[(j + 1) * blk - 1:(j + 1) * blk, sl] + upd
        for sl in heads:
            o = oin_ref[:, sl] + od_ref[:, sl]
            y = (_rms_scale(o) * norm_ref[...]) * hg_ref[:, sl].astype(F32)
            o_ref[:, sl] = y.astype(o_ref.dtype)

    splittable = worst <= HG_SPLIT_MAX

    @pl.when(splittable)
    def _():
        row = lax.broadcasted_iota(jnp.int32, (tc, tc), 0)
        col = lax.broadcasted_iota(jnp.int32, (tc, tc), 1)
        pair_in_block = (row // blk == col // blk) & (col <= row)
        for sl in heads:
            b_rel = rel_to_middle(b_ref[:, sl])
            qs = (hq_ref[:, sl].astype(F32) * jnp.exp(b_rel)).astype(BF16)
            ks = (hk_ref[:, sl].astype(F32) * jnp.exp(-b_rel)).astype(BF16)
            a = lax.dot_general(qs, ks, (((1,), (1,)), ((), ())), preferred_element_type=F32)
            a = jnp.where(pair_in_block, a, 0.0).astype(BF16)
            od_ref[:, sl] = jnp.dot(a, hi_ref[:, sl], preferred_element_type=F32)
        carried_state_and_output()

    @pl.when(jnp.logical_not(splittable))
    def _():
        row_in_blk = lax.broadcasted_iota(jnp.int32, (tc, HG_DIM), 0) % blk
        for sl in heads:
            q = hq_ref[:, sl].astype(F32)
            k = hk_ref[:, sl].astype(F32)
            v = hi_ref[:, sl].astype(F32)
            b = b_ref[:, sl]

            def lag_step(lag, od):
                k_l = pltpu.roll(k, lag, axis=0)
                b_l = pltpu.roll(b, lag, axis=0)
                v_l = pltpu.roll(v, lag, axis=0)
                w = q * k_l * jnp.exp(jnp.minimum(b - b_l, 0.0))
                w = jnp.where(row_in_blk >= lag, w, 0.0)
                return od + jnp.sum(w, axis=1, keepdims=True) * v_l

            od_ref[:, sl] = lax.fori_loop(1, blk, lag_step, jnp.sum(q * k, axis=1, keepdims=True) * v)
        carried_state_and_output()

    @pl.when(t == pl.num_programs(1) - 1)
    def _():
        for h in range(HG_HEADS):
            sout_ref[0, h] = st_ref[h].T


def _hgrn_cumsum_matrix(n, blk):
    t = np.arange(n)[:, None]
    s = np.arange(n)[None, :]
    return jnp.asarray((((t // blk) == (s // blk)) & (s <= t)).astype(np.float32), BF16)


def _hgrn(hq, hk, lfh, hi, hg, s0, norm, *, nseq):
    rows = hq.shape[0]
    t_len = rows // nseq
    tc = min(HG_TC, t_len)
    nt = t_len // tc
    blk = min(HG_BLOCK, tc)
    tri = _hgrn_cumsum_matrix(tc, blk)
    row = pl.BlockSpec((tc, HG_WIDTH), lambda b, t: (b * nt + t, 0))
    state = pl.BlockSpec((1, HG_HEADS, HG_DIM, HG_DIM), lambda b, t: (b, 0, 0, 0))
    return pl.pallas_call(
        _hgrn_kernel,
        grid=(nseq, nt),
        in_specs=[row, row, row, row, row, state,
                  pl.BlockSpec((1, HG_DIM), lambda b, t: (0, 0)),
                  pl.BlockSpec((tc, tc), lambda b, t: (0, 0))],
        out_specs=[row, state],
        out_shape=[jax.ShapeDtypeStruct((rows, HG_WIDTH), BF16),
                   jax.ShapeDtypeStruct((nseq, HG_HEADS, HG_DIM, HG_DIM), F32)],
        scratch_shapes=[
            pltpu.VMEM((HG_HEADS, HG_DIM, HG_DIM), F32),
            pltpu.VMEM((tc, HG_WIDTH), BF16),
            pltpu.VMEM((tc, HG_WIDTH), BF16),
            pltpu.VMEM((tc, HG_WIDTH), F32),
            pltpu.VMEM((tc, HG_WIDTH), F32),
            pltpu.VMEM((tc, HG_WIDTH), F32),
            pltpu.VMEM((tc, HG_WIDTH), F32),
        ],
        compiler_params=pltpu.CompilerParams(dimension_semantics=("arbitrary", "arbitrary"),
                                             vmem_limit_bytes=VMEM_LIMIT),
        name="hgrn",
    )(hq, hk, lfh, hi, hg, s0, norm, tri)


def _mixffn_kernel(x_ref, of_ref, oh_ref, ga_ref, gb_ref, hist_ref,
                   wbf_ref, wbh_ref, wout_ref, wup_ref, wdn_ref,
                   npost_ref, npre2_ref, npost2_ref, cw_ref, cb_ref,
                   y_ref, conv_ref, tail_ref, *, seg_len):
    tm = x_ref.shape[0]
    br_f = jnp.dot(of_ref[...], wbf_ref[...], preferred_element_type=F32)
    br_h = jnp.dot(oh_ref[...], wbh_ref[...], preferred_element_type=F32)
    merged = ga_ref[...].astype(F32) * br_f + gb_ref[...].astype(F32) * br_h
    mix = jnp.dot(merged.astype(BF16), wout_ref[...], preferred_element_type=F32)
    x1 = x_ref[...] + _rms_scale(mix) * npost_ref[...]

    h2 = (_rms_scale(x1) * npre2_ref[...]).astype(BF16)
    def up_chunk(j):
        w = jnp.concatenate([wup_ref[:, j * FFN_CHUNK:(j + 1) * FFN_CHUNK],
                             wup_ref[:, D_FF + j * FFN_CHUNK:D_FF + (j + 1) * FFN_CHUNK]], axis=1)
        return jnp.dot(h2, w, preferred_element_type=F32)

    n_chunks = D_FF // FFN_CHUNK
    carried = seg_len >= tm
    if carried:
        @pl.when(pl.program_id(0) == 0)
        def _():
            tail_ref[...] = hist_ref[0]
    rowi = lax.broadcasted_iota(jnp.int32, (tm, FFN_CHUNK), 0)
    ff = None
    up_next = up_chunk(0)
    for j in range(n_chunks):
        cols = slice(j * FFN_CHUNK, (j + 1) * FFN_CHUNK)
        up = up_next
        if j + 1 < n_chunks:
            up_next = up_chunk(j + 1)
        a = up[:, :FFN_CHUNK]
        g = up[:, FFN_CHUNK:]
        prev1 = pltpu.roll(a, 1, axis=0)
        prev2 = pltpu.roll(a, 2, axis=0)
        if carried:
            t0 = tail_ref[0:1, cols]
            t1 = tail_ref[1:2, cols]
            prev1 = jnp.where(rowi == 0, t1, prev1)
            prev2 = jnp.where(rowi == 0, t0, jnp.where(rowi == 1, t1, prev2))
            tail_ref[:, cols] = a[tm - 2:, :]
            conv_ref[0, :, cols] = a[tm - 2:, :]
        else:
            for s in range(tm // seg_len):
                h0 = hist_ref[s, 0:1, cols]
                h1 = hist_ref[s, 1:2, cols]
                prev1 = jnp.where(rowi == s * seg_len, h1, prev1)
                prev2 = jnp.where(rowi == s * seg_len, h0, jnp.where(rowi == s * seg_len + 1, h1, prev2))
                conv_ref[s, :, cols] = a[(s + 1) * seg_len - 2:(s + 1) * seg_len, :]
        c = cb_ref[:, cols] + cw_ref[0:1, cols] * prev2 + cw_ref[1:2, cols] * prev1 + cw_ref[2:3, cols] * a
        act = (jax.nn.gelu(c, approximate=True) * g).astype(BF16)
        part = jnp.dot(act, wdn_ref[cols, :], preferred_element_type=F32)
        ff = part if ff is None else ff + part
    y_ref[...] = x1 + _rms_scale(ff) * npost2_ref[...]


def _mixffn(x, o_fox, o_hg, ga, gb, hist, w, *, seg_len):
    rows = x.shape[0]
    tm = min(FFN_TM, rows)
    n = rows // tm
    nseg = hist.shape[0]
    row = lambda width: pl.BlockSpec((tm, width), lambda i: (i, 0))
    weights = [w["bf"], w["bh"], w["out"], w["up"], w["down"]]
    smalls = [w["npost"], w["npre2"], w["npost2"], w["conv_w"], w["conv_b"]]
    hist_spec = pl.BlockSpec(hist.shape, lambda i: (0, 0, 0))
    scratch = [pltpu.VMEM((2, D_FF), F32)]
    return pl.pallas_call(
        functools.partial(_mixffn_kernel, seg_len=seg_len),
        grid=(n,),
        in_specs=[row(D_MODEL), row(FOX_WIDTH), row(HG_WIDTH), row(D_MODEL), row(D_MODEL), hist_spec]
                 + [_const_spec(a.shape) for a in weights] + [_const_spec(a.shape) for a in smalls],
        out_specs=[row(D_MODEL), pl.BlockSpec((nseg, 2, D_FF), lambda i: (0, 0, 0))],
        out_shape=[jax.ShapeDtypeStruct((rows, D_MODEL), F32), jax.ShapeDtypeStruct((nseg, 2, D_FF), F32)],
        scratch_shapes=scratch,
        compiler_params=pltpu.CompilerParams(dimension_semantics=("arbitrary",), vmem_limit_bytes=VMEM_LIMIT),
        name="mixffn",
    )(x, o_fox, o_hg, ga, gb, hist, *weights, *smalls)


def _prep_w_in(w_in, fox_f_bias):
    offs = np.cumsum([0] + IN_SIZES)
    seg = [w_in[:, int(offs[i]):int(offs[i + 1])] for i in range(len(IN_SIZES))]
    pad = jnp.zeros((D_MODEL, LANES - 3 * FOX_HEADS), w_in.dtype)
    f3 = jnp.concatenate([seg[3], seg[3], seg[3], pad], axis=1)
    w_all = jnp.concatenate(seg[:3] + [f3] + seg[4:], axis=1).astype(BF16)
    fb = fox_f_bias.astype(F32)
    fb3 = jnp.concatenate([fb, fb, fb, jnp.zeros((LANES - 3 * FOX_HEADS,), F32)]).reshape(1, LANES)
    return w_all, fb3


def kernel(x_prompt, x_sample, cache_fox_k, cache_fox_v, cache_fox_logf, state_hgrn, state_ffn_conv, norm_mix_pre, norm_mix_post, w_in, fox_f_bias, hgrn_lb_logits, hgrn_norm, w_branch_fox, w_branch_hgrn, w_out, norm_ffn_pre, norm_ffn_post, w_up, ffn_conv_w, ffn_conv_b, w_down):
    depth = w_in.shape[0]
    assert depth == 1 and hgrn_lb_logits.shape[0] == 2
    bp, seq, _ = x_prompt.shape
    assert bp == 1
    nb, t_new, _ = x_sample.shape
    past = cache_fox_k.shape[2]

    w_all, fb3 = _prep_w_in(w_in[0], fox_f_bias[0])
    g_pre = norm_mix_pre[0].reshape(1, D_MODEL)
    lbl = hgrn_lb_logits.astype(F32)
    hnorm = hgrn_norm[0].astype(F32).reshape(1, HG_DIM)
    w = {
        "bf": w_branch_fox[0].astype(BF16), "bh": w_branch_hgrn[0].astype(BF16), "out": w_out[0].astype(BF16),
        "up": w_up[0].astype(BF16), "down": w_down[0].astype(BF16),
        "npost": norm_mix_post[0].reshape(1, D_MODEL), "npre2": norm_ffn_pre[0].reshape(1, D_MODEL),
        "npost2": norm_ffn_post[0].reshape(1, D_MODEL),
        "conv_w": ffn_conv_w[0], "conv_b": ffn_conv_b[0].reshape(1, D_FF),
    }

    xp = x_prompt.reshape(seq, D_MODEL)
    (qt, kh, vt, stats, pk, pv, plf, hq, hk, lfh, hi, hg, ga, gb) = _proj(xp, g_pre, w_all, fb3, lbl, fold=True)
    o_fox = _fox_prompt(qt, kh, vt, stats)
    s0 = jnp.zeros((1, HG_HEADS, HG_DIM, HG_DIM), F32)
    o_hg, p_state = _hgrn(hq, hk, lfh, hi, hg, s0, hnorm, nseq=1)
    hist0 = jnp.zeros((1, 2, D_FF), F32)
    yp, pconv = _mixffn(xp, o_fox, o_hg, ga, gb, hist0, w, seg_len=seq)

    xs = x_sample.reshape(nb * t_new, D_MODEL)
    (qs, sk, sv, slf, hq, hk, lfh, hi, hg, ga, gb) = _proj(xs, g_pre, w_all, fb3, lbl, fold=False)
    lf_all_t = jnp.concatenate([
        jnp.swapaxes(cache_fox_logf[0].astype(F32), 1, 2),
        jnp.swapaxes(slf.reshape(nb, t_new, FOX_HEADS), 1, 2),
        jnp.zeros((nb, FOX_HEADS, LANES - t_new), F32)], axis=2)
    cache_kt = jnp.transpose(cache_fox_k[0], (0, 2, 3, 1))
    cache_vt = jnp.transpose(cache_fox_v[0], (0, 2, 3, 1))
    o_fox_s = _fox_sample(qs, sk, sv, lf_all_t, cache_kt, cache_vt, t_new=t_new)
    o_hg_s, s_state = _hgrn(hq, hk, lfh, hi, hg, state_hgrn[0].astype(F32), hnorm, nseq=nb)
    ys, sconv = _mixffn(xs, o_fox_s, o_hg_s, ga, gb, state_ffn_conv[0], w, seg_len=t_new)

    return (
        yp.reshape(bp, seq, D_MODEL),
        ys.reshape(nb, t_new, D_MODEL),
        pk.reshape(1, bp, seq, FOX_HEADS, FOX_HEAD_DIM),
        pv.reshape(1, bp, seq, FOX_HEADS, FOX_HEAD_DIM),
        plf.reshape(1, bp, seq, FOX_HEADS),
        p_state.reshape(1, bp, HG_HEADS, HG_DIM, HG_DIM),
        pconv.reshape(1, bp, 2, D_FF),
        sk.reshape(1, nb, t_new, FOX_HEADS, FOX_HEAD_DIM),
        sv.reshape(1, nb, t_new, FOX_HEADS, FOX_HEAD_DIM),
        slf.reshape(1, nb, t_new, FOX_HEADS),
        s_state.reshape(1, nb, HG_HEADS, HG_DIM, HG_DIM),
        sconv.reshape(1, nb, 2, D_FF),
    )
```

```python
import functools

import numpy as np
import jax
import jax.numpy as jnp
from jax import lax
from jax.experimental import pallas as pl
from jax.experimental.pallas import tpu as pltpu

F32 = jnp.float32
BF16 = jnp.bfloat16

D_MODEL = 1024
FOX_HEADS = 8
FOX_HEAD_DIM = 64
FOX_WIDTH = FOX_HEADS * FOX_HEAD_DIM
HG_HEADS = 4
HG_DIM = 128
HG_WIDTH = HG_HEADS * HG_DIM
D_FF = 2816
RMS_EPS = 1e-6
NEG_INF = -1e30
LOG2E = 1.4426950408889634
FOX_SKIP_NATS = 110.0
IN_SIZES = [FOX_WIDTH, FOX_WIDTH, FOX_WIDTH, FOX_HEADS, HG_WIDTH, HG_WIDTH, HG_WIDTH, HG_WIDTH, D_MODEL, D_MODEL]

LANES = 128
FOX_PAD = 2 * FOX_HEAD_DIM
FOX_V_ROWS = FOX_HEAD_DIM + 16
HG_BLOCK = 64
HG_SPLIT_MAX = 60.0
VMEM_LIMIT = 56 * 1024 * 1024

PROJ_TM = 256
FOX_TQ = 512
FOX_TK = 512
HG_TC = 256
FFN_TM = 256
FFN_CHUNK = 256
SAMPLE_TK = 1024

_C_Q, _C_K, _C_V, _C_F = 0, 512, 1024, 1536
_C_HQ, _C_HF, _C_HI, _C_HG = 1664, 2176, 2688, 3200
_C_GA, _C_GB, _C_END = 3712, 4736, 5760


def _split3(x):
    hi = x.astype(BF16)
    r = x - hi.astype(F32)
    mid = r.astype(BF16)
    lo = (r - mid.astype(F32)).astype(BF16)
    return hi, mid, lo


def _sum_by_01_matrix(mat01, x):
    cat = jnp.concatenate(_split3(x), axis=1)
    y = jnp.dot(mat01, cat, preferred_element_type=F32)
    return y[:, :LANES] + y[:, LANES:2 * LANES] + y[:, 2 * LANES:]


def _rms_scale(x):
    return x * lax.rsqrt(jnp.mean(x * x, axis=-1, keepdims=True) + RMS_EPS)


def _log_sigmoid(x):
    return jnp.minimum(x, 0.0) - jnp.log1p(jnp.exp(-jnp.abs(x)))


def _sigmoid(x):
    return 1.0 / (1.0 + jnp.exp(-x))


def _proj_kernel(*refs, fold):
    if fold:
        (x_ref, g_ref, w_ref, fb_ref, lbl_ref, tri_ref, pq_ref, pk_ref, cq_ref, ck_ref, cv_ref, seg_ref,
         qh_ref, kh_ref, vh_ref, stat_ref, kout_ref, vout_ref, lf_ref, hq_ref, hk_ref, lfh_ref, hi_ref, hg_ref,
         ga_ref, gb_ref, carry_ref) = refs
    else:
        (x_ref, g_ref, w_ref, fb_ref, lbl_ref,
         qs_ref, kout_ref, vout_ref, lf_ref, hq_ref, hk_ref, lfh_ref, hi_ref, hg_ref,
         ga_ref, gb_ref) = refs

    h = (_rms_scale(x_ref[...]) * g_ref[...]).astype(BF16)

    z = jnp.dot(h, w_ref[...], preferred_element_type=F32)

    zq = z[:, _C_Q:_C_K] * (FOX_HEAD_DIM ** -0.5)
    zk = z[:, _C_K:_C_V]
    zv = z[:, _C_V:_C_F]
    kout_ref[...] = zk
    vout_ref[...] = zv
    logf = _log_sigmoid(z[:, _C_F:_C_HQ] + fb_ref[...])
    lf_ref[...] = logf[:, :FOX_HEADS]

    l0 = lbl_ref[0:1, :]
    l1 = lbl_ref[1:2, :]
    lmax = jnp.maximum(l0, l1)
    e0 = jnp.exp(l0 - lmax)
    lb = e0 / (e0 + jnp.exp(l1 - lmax))
    f = lb + (1.0 - lb) * _sigmoid(z[:, _C_HF:_C_HI])
    hq_ref[...] = z[:, _C_HQ:_C_HF].astype(BF16)
    hk_ref[...] = (1.0 - f).astype(BF16)
    lfh_ref[...] = jnp.log(f)
    hi_ref[...] = z[:, _C_HI:_C_HG].astype(BF16)
    hg_ref[...] = _sigmoid(z[:, _C_HG:_C_GA]).astype(BF16)
    ga_ref[...] = _sigmoid(z[:, _C_GA:_C_GB]).astype(BF16)
    gb_ref[...] = _sigmoid(z[:, _C_GB:_C_END]).astype(BF16)

    if not fold:
        qs_ref[...] = zq.astype(BF16)
        return

    @pl.when(pl.program_id(0) == 0)
    def _():
        carry_ref[...] = jnp.zeros_like(carry_ref)

    cum = carry_ref[...] + _sum_by_01_matrix(tri_ref[...], logf)
    carry_ref[...] = cum[-1:, :]

    seg = seg_ref[...]
    qn2 = jnp.dot((zq * zq).astype(BF16), seg, preferred_element_type=F32)
    kn2 = jnp.dot((zk * zk).astype(BF16), seg, preferred_element_type=F32)
    dg = jnp.dot((zq * zk).astype(BF16), seg, preferred_element_type=F32)
    stat_ref[0, 0:1, :] = jnp.max(qn2, axis=0, keepdims=True)
    stat_ref[0, 1:2, :] = jnp.max(kn2, axis=0, keepdims=True)
    stat_ref[0, 2:3, :] = jnp.min(dg, axis=0, keepdims=True)
    stat_ref[0, 3:4, :] = cum[0:1, :]
    stat_ref[0, 4:5, :] = cum[-1:, :]
    stat_ref[0, 5:8, :] = jnp.zeros((3, LANES), F32)

    zq = zq * LOG2E
    c_hi, c_mid, c_lo = _split3(cum * LOG2E)
    lane = lax.broadcasted_iota(jnp.int32, cum.shape, 1)
    pieces = jnp.where(lane < 8, c_hi, jnp.where(lane < 16, c_mid, c_lo))
    pieces = jnp.where(lane < 24, pieces, jnp.zeros_like(pieces))
    ex_q = jnp.dot(pieces, pq_ref[...], preferred_element_type=F32) + cq_ref[...]
    ex_k = jnp.dot(pieces, pk_ref[...], preferred_element_type=F32) + ck_ref[...]
    ex_v = cv_ref[...]

    low = lax.broadcasted_iota(jnp.int32, (zq.shape[0], LANES), 1) < FOX_HEAD_DIM
    for src, ex, dst, transposed in ((zq, ex_q, qh_ref, True), (zk, ex_k, kh_ref, False), (zv, ex_v, vh_ref, True)):
        for c in range(FOX_WIDTH // LANES):
            pair = src[:, c * LANES:(c + 1) * LANES]
            swapped = pltpu.roll(pair, FOX_HEAD_DIM, axis=1)
            for j, data in enumerate((pair, swapped)):
                hd = 2 * c + j
                blk = jnp.where(low, data, ex[:, hd * LANES:(hd + 1) * LANES])
                if not transposed:
                    dst[hd] = blk.astype(BF16)
                elif len(dst.shape) == 3:
                    dst[hd] = blk.T.astype(BF16)
                else:
                    dst[hd, 0] = blk.T[:dst.shape[2]].astype(BF16)


def _bias_fold_constants():
    pq = np.zeros((LANES, FOX_HEADS * LANES), np.float32)
    pk = np.zeros((LANES, FOX_HEADS * LANES), np.float32)
    cq = np.zeros((1, FOX_HEADS * LANES), np.float32)
    ck = np.zeros((1, FOX_HEADS * LANES), np.float32)
    cv = np.zeros((1, FOX_HEADS * LANES), np.float32)
    for h in range(FOX_HEADS):
        base = h * LANES + FOX_HEAD_DIM
        for p in range(3):
            pq[p * 8 + h, base + p] = 1.0
            ck[0, base + p] = 1.0
            pk[p * 8 + h, base + 3 + p] = -1.0
            cq[0, base + 3 + p] = 1.0
        cv[0, base] = 1.0
    return (jnp.asarray(pq, BF16), jnp.asarray(pk, BF16), jnp.asarray(cq), jnp.asarray(ck), jnp.asarray(cv))


def _const_spec(shape, single=True):
    nd = len(shape)
    if single:
        return pl.BlockSpec(shape, lambda *_: (0,) * nd, pipeline_mode=pl.Buffered(1))
    return pl.BlockSpec(shape, lambda *_: (0,) * nd)


def _proj(x, gain, w_all, fb3, lb_logits, *, fold):
    rows = x.shape[0]
    tm = min(PROJ_TM, rows)
    n = rows // tm
    row = lambda width: pl.BlockSpec((tm, width), lambda i: (i, 0))
    in_specs = [row(D_MODEL), _const_spec((1, D_MODEL)), _const_spec(w_all.shape), _const_spec((1, LANES)),
                _const_spec(lb_logits.shape)]
    args = [x, gain, w_all, fb3, lb_logits]
    common_out = [
        (jax.ShapeDtypeStruct((rows, FOX_WIDTH), F32), row(FOX_WIDTH)),
        (jax.ShapeDtypeStruct((rows, FOX_WIDTH), F32), row(FOX_WIDTH)),
        (jax.ShapeDtypeStruct((rows, FOX_HEADS), F32), row(FOX_HEADS)),
        (jax.ShapeDtypeStruct((rows, HG_WIDTH), BF16), row(HG_WIDTH)),
        (jax.ShapeDtypeStruct((rows, HG_WIDTH), BF16), row(HG_WIDTH)),
        (jax.ShapeDtypeStruct((rows, HG_WIDTH), F32), row(HG_WIDTH)),
        (jax.ShapeDtypeStruct((rows, HG_WIDTH), BF16), row(HG_WIDTH)),
        (jax.ShapeDtypeStruct((rows, HG_WIDTH), BF16), row(HG_WIDTH)),
        (jax.ShapeDtypeStruct((rows, D_MODEL), BF16), row(D_MODEL)),
        (jax.ShapeDtypeStruct((rows, D_MODEL), BF16), row(D_MODEL)),
    ]
    scratch = []
    if fold:
        tri = jnp.asarray(np.tril(np.ones((tm, tm), np.float32)), BF16)
        seg = np.zeros((FOX_WIDTH, LANES), np.float32)
        seg[np.arange(FOX_WIDTH), np.arange(FOX_WIDTH) // FOX_HEAD_DIM] = 1.0
        consts = _bias_fold_constants() + (jnp.asarray(seg, BF16),)
        in_specs += [_const_spec(tri.shape)] + [_const_spec(c.shape) for c in consts]
        args += [tri, *consts]
        head_major = (jax.ShapeDtypeStruct((FOX_HEADS, rows, FOX_PAD), BF16),
                      pl.BlockSpec((FOX_HEADS, tm, FOX_PAD), lambda i: (0, i, 0)))
        head_major_t = lambda depth: (jax.ShapeDtypeStruct((FOX_HEADS, depth, rows), BF16),
                                      pl.BlockSpec((FOX_HEADS, depth, tm), lambda i: (0, 0, i)))
        stats = (jax.ShapeDtypeStruct((n, 8, LANES), F32), pl.BlockSpec((1, 8, LANES), lambda i: (i, 0, 0)))
        v_tiles = (jax.ShapeDtypeStruct((FOX_HEADS, n, FOX_V_ROWS, tm), BF16),
                   pl.BlockSpec((FOX_HEADS, 1, FOX_V_ROWS, tm), lambda i: (0, i, 0, 0)))
        outs = [head_major_t(FOX_PAD), head_major, v_tiles, stats] + common_out
        scratch = [pltpu.VMEM((1, LANES), F32)]
    else:
        outs = [(jax.ShapeDtypeStruct((rows, FOX_WIDTH), BF16), row(FOX_WIDTH))] + common_out
    return pl.pallas_call(
        functools.partial(_proj_kernel, fold=fold),
        grid=(n,),
        in_specs=in_specs,
        out_specs=[o[1] for o in outs],
        out_shape=[o[0] for o in outs],
        scratch_shapes=scratch,
        compiler_params=pltpu.CompilerParams(dimension_semantics=("arbitrary",), vmem_limit_bytes=VMEM_LIMIT),
        name="proj_fold" if fold else "proj",
    )(*args)


def _fox_kernel(ks_ref, nact_ref, order_ref, qn_ref, kn_ref, cl_ref, qt_ref, k_hbm, vt_hbm, o_ref,
                kbuf, vbuf, sem, slot0_ref, m_ref, acc_ref, *, nq):
    q_blk = pl.program_id(0)
    k_first = ks_ref[q_blk]
    tq = qt_ref.shape[2]
    tk = kbuf.shape[2]
    table = lambda ref, blk, h: lax.bitcast_convert_type(ref[blk * FOX_HEADS + h], F32)

    tiles, tile = vbuf.shape[2], vbuf.shape[4]

    def copies(k_blk, slot):
        rows = pl.ds(pl.multiple_of(k_blk * tk, tk), tk)
        return (pltpu.make_async_copy(k_hbm.at[:, rows, :], kbuf.at[slot], sem.at[0, slot]),
                pltpu.make_async_copy(vt_hbm.at[:, pl.ds(k_blk * tiles, tiles)], vbuf.at[slot], sem.at[1, slot]))

    def start(k_blk, slot):
        for c in copies(k_blk, slot):
            c.start()

    def wait(k_blk, slot):
        for c in copies(k_blk, slot):
            c.wait()

    @pl.when(q_blk == 0)
    def _():
        slot0_ref[0] = 0
        start(k_first, 0)

    slot0 = slot0_ref[0]
    slot_of = lambda k_blk: (slot0 + (k_blk - k_first)) & 1
    m_ref[...] = jnp.full_like(m_ref, NEG_INF)
    acc_ref[...] = jnp.zeros_like(acc_ref)

    def logits(h, slot):
        return jnp.dot(kbuf[slot, h], qt_ref[h], preferred_element_type=F32)

    def softmax_update(h, s, k_blk, slot, masked):
        if masked:
            key = lax.broadcasted_iota(jnp.int32, (tk, tq), 0)
            qry = lax.broadcasted_iota(jnp.int32, (tk, tq), 1)
            s = jnp.where(key <= qry, s, NEG_INF)
        bound = table(qn_ref, q_blk, h) * table(kn_ref, k_blk, h)
        if not masked:
            cum_q = jnp.sum(qt_ref[h, FOX_HEAD_DIM:FOX_HEAD_DIM + 3, :].astype(F32), axis=0, keepdims=True)
            bound = cum_q + (bound - table(cl_ref, k_blk, h))
        m_prev = m_ref[h]
        m_new = jnp.maximum(m_prev, bound)
        alpha = jnp.exp2(m_prev - m_new)
        p = jnp.exp2(s - m_new).astype(BF16)
        pv = sum(jnp.dot(vbuf[slot, h, j], p[j * tile:(j + 1) * tile], preferred_element_type=F32)
                 for j in range(tiles))
        acc_ref[h] = alpha * acc_ref[h] + pv
        m_ref[h] = m_new

    def sweep(heads, k_blk, slot, masked):
        s_next = logits(heads[0], slot)
        for j, h in enumerate(heads):
            s = s_next
            if j + 1 < len(heads):
                s_next = logits(heads[j + 1], slot)
            softmax_update(h, s, k_blk, slot, masked)

    def past_block(k_blk, carry):
        slot = slot_of(k_blk)
        wait(k_blk, slot)
        start(k_blk + 1, 1 - slot)
        n_active = nact_ref[q_blk * nq + k_blk]
        for count in range(2, FOX_HEADS + 1, 2):
            @pl.when(n_active == count)
            def _():
                sweep([order_ref[q_blk * FOX_HEADS + j] for j in range(count)], k_blk, slot, False)
        return carry

    lax.fori_loop(k_first, q_blk, past_block, 0)

    slot = slot_of(q_blk)
    wait(q_blk, slot)

    @pl.when(q_blk + 1 < nq)
    def _():
        start(ks_ref[jnp.minimum(q_blk + 1, nq - 1)], 1 - slot)
    slot0_ref[0] = 1 - slot

    sweep(list(range(FOX_HEADS)), q_blk, slot, True)
    for c in range(FOX_WIDTH // LANES):
        halves = []
        for hd in (2 * c, 2 * c + 1):
            acc = acc_ref[hd]
            halves.append(acc[:FOX_HEAD_DIM, :] / acc[FOX_HEAD_DIM:FOX_HEAD_DIM + 1, :])
        o_ref[:, c * LANES:(c + 1) * LANES] = jnp.concatenate(halves, axis=0).T.astype(o_ref.dtype)


def _fox_schedule(stats, nq):
    st = stats[:, :5, :FOX_HEADS].reshape(nq, -1, 5, FOX_HEADS)
    qn = jnp.sqrt(jnp.max(st[:, :, 0], axis=1)) * 1.01
    kn = jnp.sqrt(jnp.max(st[:, :, 1], axis=1)) * 1.01
    dmin = jnp.min(st[:, :, 2], axis=1)
    c_first = st[:, 0, 3]
    c_last = st[:, -1, 4]
    bound = qn[:, None] * kn[None, :] + (c_first - dmin)[:, None] - c_last[None, :]
    blk = jnp.arange(nq, dtype=jnp.int32)
    drop = (bound < -FOX_SKIP_NATS) & (blk[None, :] < blk[:, None])[:, :, None]
    prefix = jnp.cumsum(jnp.logical_not(drop).astype(jnp.int32), axis=1) == 0
    kstart_h = jnp.sum(prefix.astype(jnp.int32), axis=1)
    kstart = jnp.min(kstart_h, axis=1)
    hd = jnp.arange(FOX_HEADS, dtype=jnp.int32)
    before = (kstart_h[:, None, :] < kstart_h[:, :, None]) | (
        (kstart_h[:, None, :] == kstart_h[:, :, None]) & (hd[None, None, :] < hd[None, :, None]))
    rank = jnp.sum(before.astype(jnp.int32), axis=2)
    order = jnp.sum(jnp.where(rank[:, None, :] == hd[None, :, None], hd[None, None, :], 0), axis=2)
    n_active = jnp.sum((kstart_h[:, None, :] <= blk[None, :, None]).astype(jnp.int32), axis=2)
    n_active = jnp.minimum((n_active + 1) // 2 * 2, FOX_HEADS)
    as_i32 = lambda a: a.astype(jnp.int32).reshape(-1)
    bits = lambda a: lax.bitcast_convert_type(a.astype(F32), jnp.int32).reshape(-1)
    return (as_i32(kstart), as_i32(n_active), as_i32(order), bits(qn * LOG2E), bits(kn), bits(c_last * LOG2E))


def _fox_prompt(qt, kh, vt, stats):
    seq = kh.shape[1]
    assert FOX_TQ == FOX_TK
    nq = seq // FOX_TQ
    tile = vt.shape[3]
    tables = _fox_schedule(stats, nq)
    grid_spec = pltpu.PrefetchScalarGridSpec(
        num_scalar_prefetch=len(tables),
        grid=(nq,),
        in_specs=[
            pl.BlockSpec((FOX_HEADS, FOX_PAD, FOX_TQ), lambda q, *_: (0, 0, q)),
            pl.BlockSpec(memory_space=pl.ANY),
            pl.BlockSpec(memory_space=pl.ANY),
        ],
        out_specs=pl.BlockSpec((FOX_TQ, FOX_WIDTH), lambda q, *_: (q, 0)),
        scratch_shapes=[pltpu.VMEM((2, FOX_HEADS, FOX_TK, FOX_PAD), BF16),
                        pltpu.VMEM((2, FOX_HEADS, FOX_TK // tile, FOX_V_ROWS, tile), BF16),
                        pltpu.SemaphoreType.DMA((2, 2)),
                        pltpu.SMEM((1,), jnp.int32),
                        pltpu.VMEM((FOX_HEADS, 1, FOX_TQ), F32),
                        pltpu.VMEM((FOX_HEADS, FOX_V_ROWS, FOX_TQ), F32)],
    )
    return pl.pallas_call(
        functools.partial(_fox_kernel, nq=nq),
        grid_spec=grid_spec,
        out_shape=jax.ShapeDtypeStruct((seq, FOX_WIDTH), BF16),
        compiler_params=pltpu.CompilerParams(dimension_semantics=("arbitrary",), vmem_limit_bytes=VMEM_LIMIT),
        name="fox_prompt",
    )(*tables, qt, kh, vt)


def _lane_cumsum(x):
    n = x.shape[1]
    lane = lax.broadcasted_iota(jnp.int32, x.shape, 1)
    shift = 1
    while shift < n:
        x = x + jnp.where(lane >= shift, pltpu.roll(x, shift, axis=1), 0.0)
        shift *= 2
    return x


def _fox_sample_kernel(q_ref, kn_ref, vn_ref, lft_ref, ck_ref, cv_ref, o_ref,
                       cq_ref, cum_ref, m_ref, l_ref, acc_ref, *, n_past_chunks, tk, t_new):
    c = pl.program_id(1)
    rows = FOX_HEADS * t_new
    heads = [slice(h * FOX_HEAD_DIM, (h + 1) * FOX_HEAD_DIM) for h in range(FOX_HEADS)]
    head_rows = [slice(h * t_new, (h + 1) * t_new) for h in range(FOX_HEADS)]

    @pl.when(c == 0)
    def _():
        cum = _lane_cumsum(lft_ref[0])
        for j in range(n_past_chunks):
            cum_ref[j] = cum[:, j * tk:(j + 1) * tk]
        new_cum = cum[:, n_past_chunks * tk:n_past_chunks * tk + LANES]
        cum_ref[n_past_chunks, :, :LANES] = new_cum
        new_cum_t = jnp.concatenate([new_cum] * (LANES // FOX_HEADS), axis=0).T
        for h in range(FOX_HEADS):
            cq_ref[head_rows[h], :] = jnp.broadcast_to(new_cum_t[:t_new, h:h + 1], (t_new, LANES))
        m_ref[...] = jnp.full_like(m_ref, NEG_INF)
        l_ref[...] = jnp.zeros_like(l_ref)
        acc_ref[...] = jnp.zeros_like(acc_ref)

    nt = (((1,), (1,)), ((), ()))

    def update(qk, pv_of, ck_rows, mask):
        q = q_ref[...]
        s = jnp.concatenate([qk(q[:, heads[h]], h) for h in range(FOX_HEADS)], axis=0)
        s = s + (cq_ref[:, :1] - ck_rows)
        if mask is not None:
            s = jnp.where(mask, s, NEG_INF)
        m_prev = m_ref[...]
        m_new = jnp.maximum(m_prev, jnp.max(s, axis=1, keepdims=True))
        alpha = jnp.exp(m_prev - m_new)
        p = jnp.exp(s - m_new[:, :1])
        l_ref[...] = alpha * l_ref[...] + jnp.sum(p, axis=1, keepdims=True)
        p = p.astype(BF16)
        pv = jnp.concatenate([pv_of(p[head_rows[h]], h) for h in range(FOX_HEADS)], axis=0)
        acc_ref[...] = alpha[:, :FOX_HEAD_DIM] * acc_ref[...] + pv
        m_ref[...] = m_new

    def expand_rows(x, width):
        return jnp.concatenate([jnp.broadcast_to(x[h:h + 1, :], (t_new, width)) for h in range(FOX_HEADS)], axis=0)

    @pl.when(c < n_past_chunks)
    def _():
        update(lambda qh, h: jnp.dot(qh, ck_ref[0, h].astype(BF16), preferred_element_type=F32),
               lambda ph, h: lax.dot_general(ph, cv_ref[0, h].astype(BF16), nt, preferred_element_type=F32),
               expand_rows(cum_ref[c], tk), None)

    @pl.when(c == n_past_chunks)
    def _():
        ck_rows = expand_rows(cum_ref[n_past_chunks, :, :LANES], LANES)[:, :t_new]
        rowt = lax.broadcasted_iota(jnp.int32, (rows, t_new), 0) % t_new
        coli = lax.broadcasted_iota(jnp.int32, (rows, t_new), 1)
        update(lambda qh, h: lax.dot_general(qh, kn_ref[:, heads[h]].astype(BF16), nt, preferred_element_type=F32),
               lambda ph, h: jnp.dot(ph, vn_ref[:, heads[h]].astype(BF16), preferred_element_type=F32),
               ck_rows, coli <= rowt)
        out = acc_ref[...] / l_ref[:, :FOX_HEAD_DIM]
        o_ref[...] = jnp.concatenate([out[head_rows[h]] for h in range(FOX_HEADS)], axis=1).astype(o_ref.dtype)


def _fox_sample(qs, k_new, v_new, lf_all_t, cache_k, cache_v, *, t_new):
    nb, past = cache_k.shape[0], cache_k.shape[3]
    tk = SAMPLE_TK
    npc = past // tk
    rows = FOX_HEADS * t_new
    last = npc - 1
    kern = functools.partial(_fox_sample_kernel, n_past_chunks=npc, tk=tk, t_new=t_new)
    cache_spec = pl.BlockSpec((1, FOX_HEADS, FOX_HEAD_DIM, tk), lambda b, c: (b, 0, 0, jnp.minimum(c, last)))
    return pl.pallas_call(
        kern,
        grid=(nb, npc + 1),
        in_specs=[
            pl.BlockSpec((t_new, FOX_WIDTH), lambda b, c: (b, 0)),
            pl.BlockSpec((t_new, FOX_WIDTH), lambda b, c: (b, 0)),
            pl.BlockSpec((t_new, FOX_WIDTH), lambda b, c: (b, 0)),
            pl.BlockSpec((1, FOX_HEADS, past + LANES), lambda b, c: (b, 0, 0)),
            cache_spec, cache_spec,
        ],
        out_specs=pl.BlockSpec((t_new, FOX_WIDTH), lambda b, c: (b, 0)),
        out_shape=jax.ShapeDtypeStruct((nb * t_new, FOX_WIDTH), BF16),
        scratch_shapes=[
            pltpu.VMEM((rows, LANES), F32),
            pltpu.VMEM((npc + 1, FOX_HEADS, tk), F32),
            pltpu.VMEM((rows, LANES), F32),
            pltpu.VMEM((rows, LANES), F32),
            pltpu.VMEM((rows, FOX_HEAD_DIM), F32),
        ],
        compiler_params=pltpu.CompilerParams(dimension_semantics=("arbitrary", "arbitrary"),
                                             vmem_limit_bytes=VMEM_LIMIT),
        name="fox_sample",
    )(qs, k_new, v_new, lf_all_t, cache_k, cache_v)


def _hgrn_kernel(hq_ref, hk_ref, lfh_ref, hi_ref, hg_ref, s0_ref, norm_ref, tri_ref,
                 o_ref, sout_ref, st_ref, qq_ref, kk_ref, eb_ref, b_ref, oin_ref, od_ref):
    t = pl.program_id(1)
    tc = hq_ref.shape[0]
    blk = min(HG_BLOCK, tc)
    nblk = tc // blk
    heads = [slice(h * HG_DIM, (h + 1) * HG_DIM) for h in range(HG_HEADS)]

    @pl.when(t == 0)
    def _():
        for h in range(HG_HEADS):
            st_ref[h] = s0_ref[0, h].T

    def at_block_row(x, r):
        x3 = x.reshape(nblk, blk, HG_DIM)
        return jnp.broadcast_to(x3[:, r:r + 1, :], x3.shape).reshape(tc, HG_DIM)

    def rel_to_middle(b):
        return b - at_block_row(b, blk // 2 - 1)

    worst = jnp.zeros((), F32)
    for sl in heads:
        b = _sum_by_01_matrix(tri_ref[...], lfh_ref[:, sl])
        eb = jnp.exp(b)
        qq_ref[:, sl] = (hq_ref[:, sl].astype(F32) * eb).astype(BF16)
        kk_ref[:, sl] = (hk_ref[:, sl].astype(F32) * jnp.exp(at_block_row(b, blk - 1) - b)).astype(BF16)
        eb_ref[:, sl] = eb
        b_ref[:, sl] = b
        worst = jnp.maximum(worst, jnp.max(jnp.abs(rel_to_middle(b))))

    def carried_state_and_output():
        for j in range(nblk):
            rows = slice(j * blk, (j + 1) * blk)
            for h, sl in enumerate(heads):
                st = st_ref[h]
                oin_ref[rows, sl] = lax.dot_general(qq_ref[rows, sl], st.astype(BF16), (((1,), (1,)), ((), ())),
                                                    preferred_element_type=F32)
                upd = lax.dot_general(hi_ref[rows, sl], kk_ref[rows, sl], (((0,), (0,)), ((), ())),
                                      preferred_element_type=F32)
                st_ref[h] = st * eb_ref[(j + 1) * blk - 1:(j + 1) * blk, sl] + upd
        for sl in heads:
            o = oin_ref[:, sl] + od_ref[:, sl]
            y = (_rms_scale(o) * norm_ref[...]) * hg_ref[:, sl].astype(F32)
            o_ref[:, sl] = y.astype(o_ref.dtype)

    splittable = worst <= HG_SPLIT_MAX

    @pl.when(splittable)
    def _():
        row = lax.broadcasted_iota(jnp.int32, (tc, tc), 0)
        col = lax.broadcasted_iota(jnp.int32, (tc, tc), 1)
        pair_in_block = (row // blk == col // blk) & (col <= row)
        for sl in heads:
            b_rel = rel_to_middle(b_ref[:, sl])
            qs = (hq_ref[:, sl].astype(F32) * jnp.exp(b_rel)).astype(BF16)
            ks = (hk_ref[:, sl].astype(F32) * jnp.exp(-b_rel)).astype(BF16)
            a = lax.dot_general(qs, ks, (((1,), (1,)), ((), ())), preferred_element_type=F32)
            a = jnp.where(pair_in_block, a, 0.0).astype(BF16)
            od_ref[:, sl] = jnp.dot(a, hi_ref[:, sl], preferred_element_type=F32)
        carried_state_and_output()

    @pl.when(jnp.logical_not(splittable))
    def _():
        row_in_blk = lax.broadcasted_iota(jnp.int32, (tc, HG_DIM), 0) % blk
        for sl in heads:
            q = hq_ref[:, sl].astype(F32)
            k = hk_ref[:, sl].astype(F32)
            v = hi_ref[:, sl].astype(F32)
            b = b_ref[:, sl]

            def lag_step(lag, od):
                k_l = pltpu.roll(k, lag, axis=0)
                b_l = pltpu.roll(b, lag, axis=0)
                v_l = pltpu.roll(v, lag, axis=0)
                w = q * k_l * jnp.exp(jnp.minimum(b - b_l, 0.0))
                w = jnp.where(row_in_blk >= lag, w, 0.0)
                return od + jnp.sum(w, axis=1, keepdims=True) * v_l

            od_ref[:, sl] = lax.fori_loop(1, blk, lag_step, jnp.sum(q * k, axis=1, keepdims=True) * v)
        carried_state_and_output()

    @pl.when(t == pl.num_programs(1) - 1)
    def _():
        for h in range(HG_HEADS):
            sout_ref[0, h] = st_ref[h].T


def _hgrn_cumsum_matrix(n, blk):
    t = np.arange(n)[:, None]
    s = np.arange(n)[None, :]
    return jnp.asarray((((t // blk) == (s // blk)) & (s <= t)).astype(np.float32), BF16)


def _hgrn(hq, hk, lfh, hi, hg, s0, norm, *, nseq):
    rows = hq.shape[0]
    t_len = rows // nseq
    tc = min(HG_TC, t_len)
    nt = t_len // tc
    blk = min(HG_BLOCK, tc)
    tri = _hgrn_cumsum_matrix(tc, blk)
    row = pl.BlockSpec((tc, HG_WIDTH), lambda b, t: (b * nt + t, 0))
    state = pl.BlockSpec((1, HG_HEADS, HG_DIM, HG_DIM), lambda b, t: (b, 0, 0, 0))
    return pl.pallas_call(
        _hgrn_kernel,
        grid=(nseq, nt),
        in_specs=[row, row, row, row, row, state,
                  pl.BlockSpec((1, HG_DIM), lambda b, t: (0, 0)),
                  pl.BlockSpec((tc, tc), lambda b, t: (0, 0))],
        out_specs=[row, state],
        out_shape=[jax.ShapeDtypeStruct((rows, HG_WIDTH), BF16),
                   jax.ShapeDtypeStruct((nseq, HG_HEADS, HG_DIM, HG_DIM), F32)],
        scratch_shapes=[
            pltpu.VMEM((HG_HEADS, HG_DIM, HG_DIM), F32),
            pltpu.VMEM((tc, HG_WIDTH), BF16),
            pltpu.VMEM((tc, HG_WIDTH), BF16),
            pltpu.VMEM((tc, HG_WIDTH), F32),
            pltpu.VMEM((tc, HG_WIDTH), F32),
            pltpu.VMEM((tc, HG_WIDTH), F32),
            pltpu.VMEM((tc, HG_WIDTH), F32),
        ],
        compiler_params=pltpu.CompilerParams(dimension_semantics=("arbitrary", "arbitrary"),
                                             vmem_limit_bytes=VMEM_LIMIT),
        name="hgrn",
    )(hq, hk, lfh, hi, hg, s0, norm, tri)


def _mixffn_kernel(x_ref, of_ref, oh_ref, ga_ref, gb_ref, hist_ref,
                   wbf_ref, wbh_ref, wout_ref, wup_ref, wdn_ref,
                   npost_ref, npre2_ref, npost2_ref, cw_ref, cb_ref,
                   y_ref, conv_ref, tail_ref, *, seg_len):
    tm = x_ref.shape[0]
    br_f = jnp.dot(of_ref[...], wbf_ref[...], preferred_element_type=F32)
    br_h = jnp.dot(oh_ref[...], wbh_ref[...], preferred_element_type=F32)
    merged = ga_ref[...].astype(F32) * br_f + gb_ref[...].astype(F32) * br_h
    mix = jnp.dot(merged.astype(BF16), wout_ref[...], preferred_element_type=F32)
    x1 = x_ref[...] + _rms_scale(mix) * npost_ref[...]

    h2 = (_rms_scale(x1) * npre2_ref[...]).astype(BF16)
    def up_chunk(j):
        w = jnp.concatenate([wup_ref[:, j * FFN_CHUNK:(j + 1) * FFN_CHUNK],
                             wup_ref[:, D_FF + j * FFN_CHUNK:D_FF + (j + 1) * FFN_CHUNK]], axis=1)
        return jnp.dot(h2, w, preferred_element_type=F32)

    n_chunks = D_FF // FFN_CHUNK
    carried = seg_len >= tm
    if carried:
        @pl.when(pl.program_id(0) == 0)
        def _():
            tail_ref[...] = hist_ref[0]
    rowi = lax.broadcasted_iota(jnp.int32, (tm, FFN_CHUNK), 0)
    ff = None
    up_next = up_chunk(0)
    for j in range(n_chunks):
        cols = slice(j * FFN_CHUNK, (j + 1) * FFN_CHUNK)
        up = up_next
        if j + 1 < n_chunks:
            up_next = up_chunk(j + 1)
        a = up[:, :FFN_CHUNK]
        g = up[:, FFN_CHUNK:]
        prev1 = pltpu.roll(a, 1, axis=0)
        prev2 = pltpu.roll(a, 2, axis=0)
        if carried:
            t0 = tail_ref[0:1, cols]
            t1 = tail_ref[1:2, cols]
            prev1 = jnp.where(rowi == 0, t1, prev1)
            prev2 = jnp.where(rowi == 0, t0, jnp.where(rowi == 1, t1, prev2))
            tail_ref[:, cols] = a[tm - 2:, :]
            conv_ref[0, :, cols] = a[tm - 2:, :]
        else:
            for s in range(tm // seg_len):
                h0 = hist_ref[s, 0:1, cols]
                h1 = hist_ref[s, 1:2, cols]
                prev1 = jnp.where(rowi == s * seg_len, h1, prev1)
                prev2 = jnp.where(rowi == s * seg_len, h0, jnp.where(rowi == s * seg_len + 1, h1, prev2))
                conv_ref[s, :, cols] = a[(s + 1) * seg_len - 2:(s + 1) * seg_len, :]
        c = cb_ref[:, cols] + cw_ref[0:1, cols] * prev2 + cw_ref[1:2, cols] * prev1 + cw_ref[2:3, cols] * a
        act = (jax.nn.gelu(c, approximate=True) * g).astype(BF16)
        part = jnp.dot(act, wdn_ref[cols, :], preferred_element_type=F32)
        ff = part if ff is None else ff + part
    y_ref[...] = x1 + _rms_scale(ff) * npost2_ref[...]


def _mixffn(x, o_fox, o_hg, ga, gb, hist, w, *, seg_len):
    rows = x.shape[0]
    tm = min(FFN_TM, rows)
    n = rows // tm
    nseg = hist.shape[0]
    row = lambda width: pl.BlockSpec((tm, width), lambda i: (i, 0))
    weights = [w["bf"], w["bh"], w["out"], w["up"], w["down"]]
    smalls = [w["npost"], w["npre2"], w["npost2"], w["conv_w"], w["conv_b"]]
    hist_spec = pl.BlockSpec(hist.shape, lambda i: (0, 0, 0))
    scratch = [pltpu.VMEM((2, D_FF), F32)]
    return pl.pallas_call(
        functools.partial(_mixffn_kernel, seg_len=seg_len),
        grid=(n,),
        in_specs=[row(D_MODEL), row(FOX_WIDTH), row(HG_WIDTH), row(D_MODEL), row(D_MODEL), hist_spec]
                 + [_const_spec(a.shape) for a in weights] + [_const_spec(a.shape) for a in smalls],
        out_specs=[row(D_MODEL), pl.BlockSpec((nseg, 2, D_FF), lambda i: (0, 0, 0))],
        out_shape=[jax.ShapeDtypeStruct((rows, D_MODEL), F32), jax.ShapeDtypeStruct((nseg, 2, D_FF), F32)],
        scratch_shapes=scratch,
        compiler_params=pltpu.CompilerParams(dimension_semantics=("arbitrary",), vmem_limit_bytes=VMEM_LIMIT),
        name="mixffn",
    )(x, o_fox, o_hg, ga, gb, hist, *weights, *smalls)


def _prep_w_in(w_in, fox_f_bias):
    offs = np.cumsum([0] + IN_SIZES)
    seg = [w_in[:, int(offs[i]):int(offs[i + 1])] for i in range(len(IN_SIZES))]
    pad = jnp.zeros((D_MODEL, LANES - 3 * FOX_HEADS), w_in.dtype)
    f3 = jnp.concatenate([seg[3], seg[3], seg[3], pad], axis=1)
    w_all = jnp.concatenate(seg[:3] + [f3] + seg[4:], axis=1).astype(BF16)
    fb = fox_f_bias.astype(F32)
    fb3 = jnp.concatenate([fb, fb, fb, jnp.zeros((LANES - 3 * FOX_HEADS,), F32)]).reshape(1, LANES)
    return w_all, fb3


def kernel(x_prompt, x_sample, cache_fox_k, cache_fox_v, cache_fox_logf, state_hgrn, state_ffn_conv, norm_mix_pre, norm_mix_post, w_in, fox_f_bias, hgrn_lb_logits, hgrn_norm, w_branch_fox, w_branch_hgrn, w_out, norm_ffn_pre, norm_ffn_post, w_up, ffn_conv_w, ffn_conv_b, w_down):
    depth = w_in.shape[0]
    assert depth == 1 and hgrn_lb_logits.shape[0] == 2
    bp, seq, _ = x_prompt.shape
    assert bp == 1
    nb, t_new, _ = x_sample.shape
    past = cache_fox_k.shape[2]

    w_all, fb3 = _prep_w_in(w_in[0], fox_f_bias[0])
    g_pre = norm_mix_pre[0].reshape(1, D_MODEL)
    lbl = hgrn_lb_logits.astype(F32)
    hnorm = hgrn_norm[0].astype(F32).reshape(1, HG_DIM)
    w = {
        "bf": w_branch_fox[0].astype(BF16), "bh": w_branch_hgrn[0].astype(BF16), "out": w_out[0].astype(BF16),
        "up": w_up[0].astype(BF16), "down": w_down[0].astype(BF16),
        "npost": norm_mix_post[0].reshape(1, D_MODEL), "npre2": norm_ffn_pre[0].reshape(1, D_MODEL),
        "npost2": norm_ffn_post[0].reshape(1, D_MODEL),
        "conv_w": ffn_conv_w[0], "conv_b": ffn_conv_b[0].reshape(1, D_FF),
    }

    xp = x_prompt.reshape(seq, D_MODEL)
    (qt, kh, vt, stats, pk, pv, plf, hq, hk, lfh, hi, hg, ga, gb) = _proj(xp, g_pre, w_all, fb3, lbl, fold=True)
    o_fox = _fox_prompt(qt, kh, vt, stats)
    s0 = jnp.zeros((1, HG_HEADS, HG_DIM, HG_DIM), F32)
    o_hg, p_state = _hgrn(hq, hk, lfh, hi, hg, s0, hnorm, nseq=1)
    hist0 = jnp.zeros((1, 2, D_FF), F32)
    yp, pconv = _mixffn(xp, o_fox, o_hg, ga, gb, hist0, w, seg_len=seq)

    xs = x_sample.reshape(nb * t_new, D_MODEL)
    (qs, sk, sv, slf, hq, hk, lfh, hi, hg, ga, gb) = _proj(xs, g_pre, w_all, fb3, lbl, fold=False)
    lf_all_t = jnp.concatenate([
        jnp.swapaxes(cache_fox_logf[0].astype(F32), 1, 2),
        jnp.swapaxes(slf.reshape(nb, t_new, FOX_HEADS), 1, 2),
        jnp.zeros((nb, FOX_HEADS, LANES - t_new), F32)], axis=2)
    cache_kt = jnp.transpose(cache_fox_k[0], (0, 2, 3, 1))
    cache_vt = jnp.transpose(cache_fox_v[0], (0, 2, 3, 1))
    o_fox_s = _fox_sample(qs, sk, sv, lf_all_t, cache_kt, cache_vt, t_new=t_new)
    o_hg_s, s_state = _hgrn(hq, hk, lfh, hi, hg, state_hgrn[0].astype(F32), hnorm, nseq=nb)
    ys, sconv = _mixffn(xs, o_fox_s, o_hg_s, ga, gb, state_ffn_conv[0], w, seg_len=t_new)

    return (
        yp.reshape(bp, seq, D_MODEL),
        ys.reshape(nb, t_new, D_MODEL),
        pk.reshape(1, bp, seq, FOX_HEADS, FOX_HEAD_DIM),
        pv.reshape(1, bp, seq, FOX_HEADS, FOX_HEAD_DIM),
        plf.reshape(1, bp, seq, FOX_HEADS),
        p_state.reshape(1, bp, HG_HEADS, HG_DIM, HG_DIM),
        pconv.reshape(1, bp, 2, D_FF),
        sk.reshape(1, nb, t_new, FOX_HEADS, FOX_HEAD_DIM),
        sv.reshape(1, nb, t_new, FOX_HEADS, FOX_HEAD_DIM),
        slf.reshape(1, nb, t_new, FOX_HEADS),
        s_state.reshape(1, nb, HG_HEADS, HG_DIM, HG_DIM),
        sconv.reshape(1, nb, 2, D_FF),
    )
```

```python
import functools

import numpy as np
import jax
import jax.numpy as jnp
from jax import lax
from jax.experimental import pallas as pl
from jax.experimental.pallas import tpu as pltpu

F32 = jnp.float32
BF16 = jnp.bfloat16

D_MODEL = 1024
FOX_HEADS = 8
FOX_HEAD_DIM = 64
FOX_WIDTH = FOX_HEADS * FOX_HEAD_DIM
HG_HEADS = 4
HG_DIM = 128
HG_WIDTH = HG_HEADS * HG_DIM
D_FF = 2816
RMS_EPS = 1e-6
NEG_INF = -1e30
LOG2E = 1.4426950408889634
FOX_SKIP_NATS = 110.0
IN_SIZES = [FOX_WIDTH, FOX_WIDTH, FOX_WIDTH, FOX_HEADS, HG_WIDTH, HG_WIDTH, HG_WIDTH, HG_WIDTH, D_MODEL, D_MODEL]

LANES = 128
FOX_PAD = 2 * FOX_HEAD_DIM
FOX_V_ROWS = FOX_HEAD_DIM + 16
HG_BLOCK = 64
HG_SPLIT_MAX = 60.0
VMEM_LIMIT = 56 * 1024 * 1024

PROJ_TM = 256
FOX_TQ = 512
FOX_TK = 512
FOX_SLOTS = 3
HG_TC = 256
FFN_TM = 256
FFN_CHUNK = 256
SAMPLE_TK = 1024

_C_Q, _C_K, _C_V, _C_F = 0, 512, 1024, 1536
_C_HQ, _C_HF, _C_HI, _C_HG = 1664, 2176, 2688, 3200
_C_GA, _C_GB, _C_END = 3712, 4736, 5760


def _split3(x):
    hi = x.astype(BF16)
    r = x - hi.astype(F32)
    mid = r.astype(BF16)
    lo = (r - mid.astype(F32)).astype(BF16)
    return hi, mid, lo


def _sum_by_01_matrix(mat01, x):
    cat = jnp.concatenate(_split3(x), axis=1)
    y = jnp.dot(mat01, cat, preferred_element_type=F32)
    return y[:, :LANES] + y[:, LANES:2 * LANES] + y[:, 2 * LANES:]


def _rms_scale(x):
    return x * lax.rsqrt(jnp.mean(x * x, axis=-1, keepdims=True) + RMS_EPS)


def _log_sigmoid(x):
    return jnp.minimum(x, 0.0) - jnp.log1p(jnp.exp(-jnp.abs(x)))


def _sigmoid(x):
    return 1.0 / (1.0 + jnp.exp(-x))


def _proj_kernel(*refs, fold):
    if fold:
        (x_ref, g_ref, w_ref, fb_ref, lbl_ref, tri_ref, pq_ref, pk_ref, cq_ref, ck_ref, cv_ref, seg_ref,
         qh_ref, kh_ref, vh_ref, stat_ref, kout_ref, vout_ref, lf_ref, hq_ref, hk_ref, lfh_ref, hi_ref, hg_ref,
         ga_ref, gb_ref, carry_ref) = refs
    else:
        (x_ref, g_ref, w_ref, fb_ref, lbl_ref,
         qs_ref, kout_ref, vout_ref, lf_ref, hq_ref, hk_ref, lfh_ref, hi_ref, hg_ref,
         ga_ref, gb_ref) = refs

    h = (_rms_scale(x_ref[...]) * g_ref[...]).astype(BF16)

    z = jnp.dot(h, w_ref[...], preferred_element_type=F32)

    zq = z[:, _C_Q:_C_K] * (FOX_HEAD_DIM ** -0.5)
    zk = z[:, _C_K:_C_V]
    zv = z[:, _C_V:_C_F]
    kout_ref[...] = zk
    vout_ref[...] = zv
    logf = _log_sigmoid(z[:, _C_F:_C_HQ] + fb_ref[...])
    lf_ref[...] = logf[:, :FOX_HEADS]

    l0 = lbl_ref[0:1, :]
    l1 = lbl_ref[1:2, :]
    lmax = jnp.maximum(l0, l1)
    e0 = jnp.exp(l0 - lmax)
    lb = e0 / (e0 + jnp.exp(l1 - lmax))
    f = lb + (1.0 - lb) * _sigmoid(z[:, _C_HF:_C_HI])
    hq_ref[...] = z[:, _C_HQ:_C_HF].astype(BF16)
    hk_ref[...] = (1.0 - f).astype(BF16)
    lfh_ref[...] = jnp.log(f)
    hi_ref[...] = z[:, _C_HI:_C_HG].astype(BF16)
    hg_ref[...] = _sigmoid(z[:, _C_HG:_C_GA]).astype(BF16)
    ga_ref[...] = _sigmoid(z[:, _C_GA:_C_GB]).astype(BF16)
    gb_ref[...] = _sigmoid(z[:, _C_GB:_C_END]).astype(BF16)

    if not fold:
        qs_ref[...] = zq.astype(BF16)
        return

    @pl.when(pl.program_id(0) == 0)
    def _():
        carry_ref[...] = jnp.zeros_like(carry_ref)

    cum = carry_ref[...] + _sum_by_01_matrix(tri_ref[...], logf)
    carry_ref[...] = cum[-1:, :]

    seg = seg_ref[...]
    qn2 = jnp.dot((zq * zq).astype(BF16), seg, preferred_element_type=F32)
    kn2 = jnp.dot((zk * zk).astype(BF16), seg, preferred_element_type=F32)
    dg = jnp.dot((zq * zk).astype(BF16), seg, preferred_element_type=F32)
    stat_ref[0, 0:1, :] = jnp.max(qn2, axis=0, keepdims=True)
    stat_ref[0, 1:2, :] = jnp.max(kn2, axis=0, keepdims=True)
    stat_ref[0, 2:3, :] = jnp.min(dg, axis=0, keepdims=True)
    stat_ref[0, 3:4, :] = cum[0:1, :]
    stat_ref[0, 4:5, :] = cum[-1:, :]
    stat_ref[0, 5:8, :] = jnp.zeros((3, LANES), F32)

    zq = zq * LOG2E
    c_hi, c_mid, c_lo = _split3(cum * LOG2E)
    lane = lax.broadcasted_iota(jnp.int32, cum.shape, 1)
    pieces = jnp.where(lane < 8, c_hi, jnp.where(lane < 16, c_mid, c_lo))
    pieces = jnp.where(lane < 24, pieces, jnp.zeros_like(pieces))
    ex_q = jnp.dot(pieces, pq_ref[...], preferred_element_type=F32) + cq_ref[...]
    ex_k = jnp.dot(pieces, pk_ref[...], preferred_element_type=F32) + ck_ref[...]
    ex_v = cv_ref[...]

    low = lax.broadcasted_iota(jnp.int32, (zq.shape[0], LANES), 1) < FOX_HEAD_DIM
    for src, ex, dst, transposed in ((zq, ex_q, qh_ref, True), (zk, ex_k, kh_ref, False), (zv, ex_v, vh_ref, True)):
        for c in range(FOX_WIDTH // LANES):
            pair = src[:, c * LANES:(c + 1) * LANES]
            swapped = pltpu.roll(pair, FOX_HEAD_DIM, axis=1)
            for j, data in enumerate((pair, swapped)):
                hd = 2 * c + j
                blk = jnp.where(low, data, ex[:, hd * LANES:(hd + 1) * LANES])
                if not transposed:
                    dst[hd] = blk.astype(BF16)
                elif len(dst.shape) == 3:
                    dst[hd] = blk.T.astype(BF16)
                else:
                    dst[hd, 0] = blk.T[:dst.shape[2]].astype(BF16)


def _bias_fold_constants():
    pq = np.zeros((LANES, FOX_HEADS * LANES), np.float32)
    pk = np.zeros((LANES, FOX_HEADS * LANES), np.float32)
    cq = np.zeros((1, FOX_HEADS * LANES), np.float32)
    ck = np.zeros((1, FOX_HEADS * LANES), np.float32)
    cv = np.zeros((1, FOX_HEADS * LANES), np.float32)
    for h in range(FOX_HEADS):
        base = h * LANES + FOX_HEAD_DIM
        for p in range(3):
            pq[p * 8 + h, base + p] = 1.0
            ck[0, base + p] = 1.0
            pk[p * 8 + h, base + 3 + p] = -1.0
            cq[0, base + 3 + p] = 1.0
        cv[0, base] = 1.0
    return (jnp.asarray(pq, BF16), jnp.asarray(pk, BF16), jnp.asarray(cq), jnp.asarray(ck), jnp.asarray(cv))


def _const_spec(shape, single=True):
    nd = len(shape)
    if single:
        return pl.BlockSpec(shape, lambda *_: (0,) * nd, pipeline_mode=pl.Buffered(1))
    return pl.BlockSpec(shape, lambda *_: (0,) * nd)


def _proj(x, gain, w_all, fb3, lb_logits, *, fold):
    rows = x.shape[0]
    tm = min(PROJ_TM, rows)
    n = rows // tm
    row = lambda width: pl.BlockSpec((tm, width), lambda i: (i, 0))
    in_specs = [row(D_MODEL), _const_spec((1, D_MODEL)), _const_spec(w_all.shape), _const_spec((1, LANES)),
                _const_spec(lb_logits.shape)]
    args = [x, gain, w_all, fb3, lb_logits]
    common_out = [
        (jax.ShapeDtypeStruct((rows, FOX_WIDTH), F32), row(FOX_WIDTH)),
        (jax.ShapeDtypeStruct((rows, FOX_WIDTH), F32), row(FOX_WIDTH)),
        (jax.ShapeDtypeStruct((rows, FOX_HEADS), F32), row(FOX_HEADS)),
        (jax.ShapeDtypeStruct((rows, HG_WIDTH), BF16), row(HG_WIDTH)),
        (jax.ShapeDtypeStruct((rows, HG_WIDTH), BF16), row(HG_WIDTH)),
        (jax.ShapeDtypeStruct((rows, HG_WIDTH), F32), row(HG_WIDTH)),
        (jax.ShapeDtypeStruct((rows, HG_WIDTH), BF16), row(HG_WIDTH)),
        (jax.ShapeDtypeStruct((rows, HG_WIDTH), BF16), row(HG_WIDTH)),
        (jax.ShapeDtypeStruct((rows, D_MODEL), BF16), row(D_MODEL)),
        (jax.ShapeDtypeStruct((rows, D_MODEL), BF16), row(D_MODEL)),
    ]
    scratch = []
    if fold:
        tri = jnp.asarray(np.tril(np.ones((tm, tm), np.float32)), BF16)
        seg = np.zeros((FOX_WIDTH, LANES), np.float32)
        seg[np.arange(FOX_WIDTH), np.arange(FOX_WIDTH) // FOX_HEAD_DIM] = 1.0
        consts = _bias_fold_constants() + (jnp.asarray(seg, BF16),)
        in_specs += [_const_spec(tri.shape)] + [_const_spec(c.shape) for c in consts]
        args += [tri, *consts]
        head_major = (jax.ShapeDtypeStruct((FOX_HEADS, rows, FOX_PAD), BF16),
                      pl.BlockSpec((FOX_HEADS, tm, FOX_PAD), lambda i: (0, i, 0)))
        head_major_t = lambda depth: (jax.ShapeDtypeStruct((FOX_HEADS, depth, rows), BF16),
                                      pl.BlockSpec((FOX_HEADS, depth, tm), lambda i: (0, 0, i)))
        stats = (jax.ShapeDtypeStruct((n, 8, LANES), F32), pl.BlockSpec((1, 8, LANES), lambda i: (i, 0, 0)))
        v_tiles = (jax.ShapeDtypeStruct((FOX_HEADS, n, FOX_V_ROWS, tm), BF16),
                   pl.BlockSpec((FOX_HEADS, 1, FOX_V_ROWS, tm), lambda i: (0, i, 0, 0)))
        outs = [head_major_t(FOX_PAD), head_major, v_tiles, stats] + common_out
        scratch = [pltpu.VMEM((1, LANES), F32)]
    else:
        outs = [(jax.ShapeDtypeStruct((rows, FOX_WIDTH), BF16), row(FOX_WIDTH))] + common_out
    return pl.pallas_call(
        functools.partial(_proj_kernel, fold=fold),
        grid=(n,),
        in_specs=in_specs,
        out_specs=[o[1] for o in outs],
        out_shape=[o[0] for o in outs],
        scratch_shapes=scratch,
        compiler_params=pltpu.CompilerParams(dimension_semantics=("arbitrary",), vmem_limit_bytes=VMEM_LIMIT),
        name="proj_fold" if fold else "proj",
    )(*args)


def _fox_kernel(ks_ref, nact_ref, order_ref, qn_ref, kn_ref, cl_ref, qt_ref, k_hbm, vt_hbm, o_ref,
                kbuf, vbuf, sem, slot0_ref, m_ref, acc_ref, *, nq):
    q_blk = pl.program_id(0)
    k_first = ks_ref[q_blk]
    tq = qt_ref.shape[2]
    n_slots, tk = kbuf.shape[0], kbuf.shape[2]
    tiles, tile = vbuf.shape[2], vbuf.shape[4]
    table = lambda ref, blk, h: lax.bitcast_convert_type(ref[blk * FOX_HEADS + h], F32)

    def successor(q, k):
        at_diagonal = k == q
        return (jnp.where(at_diagonal, q + 1, q),
                jnp.where(at_diagonal, ks_ref[jnp.minimum(q + 1, nq - 1)], k + 1))

    def for_swept_heads(q, k, slot, action):
        n_swept = nact_ref[q * nq + k]
        for j in range(FOX_HEADS):
            @pl.when(j < n_swept)
            def _():
                h = order_ref[q * FOX_HEADS + j]
                rows = pl.ds(pl.multiple_of(k * tk, tk), tk)
                action(pltpu.make_async_copy(k_hbm.at[h, rows, :], kbuf.at[slot, h], sem.at[0, slot]))
                action(pltpu.make_async_copy(vt_hbm.at[h, pl.ds(k * tiles, tiles)], vbuf.at[slot, h],
                                             sem.at[1, slot]))

    def start(q, k, slot):
        @pl.when(q < nq)
        def _():
            for_swept_heads(q, k, slot, lambda c: c.start())

    def wait(q, k, slot):
        for_swept_heads(q, k, slot, lambda c: c.wait())

    def request_second_next(k_blk, slot):
        q1, k1 = successor(q_blk, k_blk)
        q2, k2 = successor(jnp.minimum(q1, nq - 1), k1)
        start(jnp.where(q1 < nq, q2, nq), k2, lax.rem(slot + 2, n_slots))

    @pl.when(q_blk == 0)
    def _():
        slot0_ref[0] = 0
        start(q_blk, k_first, 0)
        q1, k1 = successor(q_blk, k_first)
        start(q1, k1, 1)

    slot0 = slot0_ref[0]
    slot_of = lambda k_blk: lax.rem(slot0 + (k_blk - k_first), n_slots)
    m_ref[...] = jnp.full_like(m_ref, NEG_INF)
    acc_ref[...] = jnp.zeros_like(acc_ref)

    def logits(h, slot):
        return jnp.dot(kbuf[slot, h], qt_ref[h], preferred_element_type=F32)

    def softmax_update(h, s, k_blk, slot, masked):
        if masked:
            key = lax.broadcasted_iota(jnp.int32, (tk, tq), 0)
            qry = lax.broadcasted_iota(jnp.int32, (tk, tq), 1)
            s = jnp.where(key <= qry, s, NEG_INF)
        bound = table(qn_ref, q_blk, h) * table(kn_ref, k_blk, h)
        if not masked:
            cum_q = jnp.sum(qt_ref[h, FOX_HEAD_DIM:FOX_HEAD_DIM + 3, :].astype(F32), axis=0, keepdims=True)
            bound = cum_q + (bound - table(cl_ref, k_blk, h))
        m_prev = m_ref[h]
        m_new = jnp.maximum(m_prev, bound)
        alpha = jnp.exp2(m_prev - m_new)
        p = jnp.exp2(s - m_new).astype(BF16)
        pv = sum(jnp.dot(vbuf[slot, h, j], p[j * tile:(j + 1) * tile], preferred_element_type=F32)
                 for j in range(tiles))
        acc_ref[h] = alpha * acc_ref[h] + pv
        m_ref[h] = m_new

    def sweep(heads, k_blk, slot, masked):
        s_next = logits(heads[0], slot)
        for j, h in enumerate(heads):
            s = s_next
            if j + 1 < len(heads):
                s_next = logits(heads[j + 1], slot)
            softmax_update(h, s, k_blk, slot, masked)

    def past_block(k_blk, carry):
        slot = slot_of(k_blk)
        wait(q_blk, k_blk, slot)
        request_second_next(k_blk, slot)
        n_active = nact_ref[q_blk * nq + k_blk]
        for count in range(2, FOX_HEADS + 1, 2):
            @pl.when(n_active == count)
            def _():
                sweep([order_ref[q_blk * FOX_HEADS + j] for j in range(count)], k_blk, slot, False)
        return carry

    lax.fori_loop(k_first, q_blk, past_block, 0)

    slot = slot_of(q_blk)
    wait(q_blk, q_blk, slot)
    request_second_next(q_blk, slot)
    slot0_ref[0] = lax.rem(slot + 1, n_slots)

    sweep(list(range(FOX_HEADS)), q_blk, slot, True)
    for c in range(FOX_WIDTH // LANES):
        halves = []
        for hd in (2 * c, 2 * c + 1):
            acc = acc_ref[hd]
            halves.append(acc[:FOX_HEAD_DIM, :] / acc[FOX_HEAD_DIM:FOX_HEAD_DIM + 1, :])
        o_ref[:, c * LANES:(c + 1) * LANES] = jnp.concatenate(halves, axis=0).T.astype(o_ref.dtype)


def _fox_schedule(stats, nq):
    st = stats[:, :5, :FOX_HEADS].reshape(nq, -1, 5, FOX_HEADS)
    qn = jnp.sqrt(jnp.max(st[:, :, 0], axis=1)) * 1.01
    kn = jnp.sqrt(jnp.max(st[:, :, 1], axis=1)) * 1.01
    dmin = jnp.min(st[:, :, 2], axis=1)
    c_first = st[:, 0, 3]
    c_last = st[:, -1, 4]
    bound = qn[:, None] * kn[None, :] + (c_first - dmin)[:, None] - c_last[None, :]
    blk = jnp.arange(nq, dtype=jnp.int32)
    drop = (bound < -FOX_SKIP_NATS) & (blk[None, :] < blk[:, None])[:, :, None]
    prefix = jnp.cumsum(jnp.logical_not(drop).astype(jnp.int32), axis=1) == 0
    kstart_h = jnp.sum(prefix.astype(jnp.int32), axis=1)
    kstart = jnp.min(kstart_h, axis=1)
    hd = jnp.arange(FOX_HEADS, dtype=jnp.int32)
    before = (kstart_h[:, None, :] < kstart_h[:, :, None]) | (
        (kstart_h[:, None, :] == kstart_h[:, :, None]) & (hd[None, None, :] < hd[None, :, None]))
    rank = jnp.sum(before.astype(jnp.int32), axis=2)
    order = jnp.sum(jnp.where(rank[:, None, :] == hd[None, :, None], hd[None, None, :], 0), axis=2)
    n_active = jnp.sum((kstart_h[:, None, :] <= blk[None, :, None]).astype(jnp.int32), axis=2)
    n_active = jnp.minimum((n_active + 1) // 2 * 2, FOX_HEADS)
    as_i32 = lambda a: a.astype(jnp.int32).reshape(-1)
    bits = lambda a: lax.bitcast_convert_type(a.astype(F32), jnp.int32).reshape(-1)
    return (as_i32(kstart), as_i32(n_active), as_i32(order), bits(qn * LOG2E), bits(kn), bits(c_last * LOG2E))


def _fox_prompt(qt, kh, vt, stats):
    seq = kh.shape[1]
    assert FOX_TQ == FOX_TK
    nq = seq // FOX_TQ
    tile = vt.shape[3]
    tables = _fox_schedule(stats, nq)
    grid_spec = pltpu.PrefetchScalarGridSpec(
        num_scalar_prefetch=len(tables),
        grid=(nq,),
        in_specs=[
            pl.BlockSpec((FOX_HEADS, FOX_PAD, FOX_TQ), lambda q, *_: (0, 0, q)),
            pl.BlockSpec(memory_space=pl.ANY),
            pl.BlockSpec(memory_space=pl.ANY),
        ],
        out_specs=pl.BlockSpec((FOX_TQ, FOX_WIDTH), lambda q, *_: (q, 0)),
        scratch_shapes=[pltpu.VMEM((FOX_SLOTS, FOX_HEADS, FOX_TK, FOX_PAD), BF16),
                        pltpu.VMEM((FOX_SLOTS, FOX_HEADS, FOX_TK // tile, FOX_V_ROWS, tile), BF16),
                        pltpu.SemaphoreType.DMA((2, FOX_SLOTS)),
                        pltpu.SMEM((1,), jnp.int32),
                        pltpu.VMEM((FOX_HEADS, 1, FOX_TQ), F32),
                        pltpu.VMEM((FOX_HEADS, FOX_V_ROWS, FOX_TQ), F32)],
    )
    return pl.pallas_call(
        functools.partial(_fox_kernel, nq=nq),
        grid_spec=grid_spec,
        out_shape=jax.ShapeDtypeStruct((seq, FOX_WIDTH), BF16),
        compiler_params=pltpu.CompilerParams(dimension_semantics=("arbitrary",), vmem_limit_bytes=VMEM_LIMIT),
        name="fox_prompt",
    )(*tables, qt, kh, vt)


def _lane_cumsum(x):
    n = x.shape[1]
    lane = lax.broadcasted_iota(jnp.int32, x.shape, 1)
    shift = 1
    while shift < n:
        x = x + jnp.where(lane >= shift, pltpu.roll(x, shift, axis=1), 0.0)
        shift *= 2
    return x


def _fox_sample_kernel(q_ref, kn_ref, vn_ref, lft_ref, ck_ref, cv_ref, o_ref,
                       cq_ref, cum_ref, m_ref, l_ref, acc_ref, *, n_past_chunks, tk, t_new):
    c = pl.program_id(1)
    rows = FOX_HEADS * t_new
    heads = [slice(h * FOX_HEAD_DIM, (h + 1) * FOX_HEAD_DIM) for h in range(FOX_HEADS)]
    head_rows = [slice(h * t_new, (h + 1) * t_new) for h in range(FOX_HEADS)]

    @pl.when(c == 0)
    def _():
        cum = _lane_cumsum(lft_ref[0])
        for j in range(n_past_chunks):
            cum_ref[j] = cum[:, j * tk:(j + 1) * tk]
        new_cum = cum[:, n_past_chunks * tk:n_past_chunks * tk + LANES]
        cum_ref[n_past_chunks, :, :LANES] = new_cum
        new_cum_t = jnp.concatenate([new_cum] * (LANES // FOX_HEADS), axis=0).T
        for h in range(FOX_HEADS):
            cq_ref[head_rows[h], :] = jnp.broadcast_to(new_cum_t[:t_new, h:h + 1], (t_new, LANES))
        m_ref[...] = jnp.full_like(m_ref, NEG_INF)
        l_ref[...] = jnp.zeros_like(l_ref)
        acc_ref[...] = jnp.zeros_like(acc_ref)

    nt = (((1,), (1,)), ((), ()))

    def update(qk, pv_of, ck_rows, mask):
        q = q_ref[...]
        s = jnp.concatenate([qk(q[:, heads[h]], h) for h in range(FOX_HEADS)], axis=0)
        s = s + (cq_ref[:, :1] - ck_rows)
        if mask is not None:
            s = jnp.where(mask, s, NEG_INF)
        m_prev = m_ref[...]
        m_new = jnp.maximum(m_prev, jnp.max(s, axis=1, keepdims=True))
        alpha = jnp.exp(m_prev - m_new)
        p = jnp.exp(s - m_new[:, :1])
        l_ref[...] = alpha * l_ref[...] + jnp.sum(p, axis=1, keepdims=True)
        p = p.astype(BF16)
        pv = jnp.concatenate([pv_of(p[head_rows[h]], h) for h in range(FOX_HEADS)], axis=0)
        acc_ref[...] = alpha[:, :FOX_HEAD_DIM] * acc_ref[...] + pv
        m_ref[...] = m_new

    def expand_rows(x, width):
        return jnp.concatenate([jnp.broadcast_to(x[h:h + 1, :], (t_new, width)) for h in range(FOX_HEADS)], axis=0)

    @pl.when(c < n_past_chunks)
    def _():
        update(lambda qh, h: jnp.dot(qh, ck_ref[0, h].astype(BF16), preferred_element_type=F32),
               lambda ph, h: lax.dot_general(ph, cv_ref[0, h].astype(BF16), nt, preferred_element_type=F32),
               expand_rows(cum_ref[c], tk), None)

    @pl.when(c == n_past_chunks)
    def _():
        ck_rows = expand_rows(cum_ref[n_past_chunks, :, :LANES], LANES)[:, :t_new]
        rowt = lax.broadcasted_iota(jnp.int32, (rows, t_new), 0) % t_new
        coli = lax.broadcasted_iota(jnp.int32, (rows, t_new), 1)
        update(lambda qh, h: lax.dot_general(qh, kn_ref[:, heads[h]].astype(BF16), nt, preferred_element_type=F32),
               lambda ph, h: jnp.dot(ph, vn_ref[:, heads[h]].astype(BF16), preferred_element_type=F32),
               ck_rows, coli <= rowt)
        out = acc_ref[...] / l_ref[:, :FOX_HEAD_DIM]
        o_ref[...] = jnp.concatenate([out[head_rows[h]] for h in range(FOX_HEADS)], axis=1).astype(o_ref.dtype)


def _fox_sample(qs, k_new, v_new, lf_all_t, cache_k, cache_v, *, t_new):
    nb, past = cache_k.shape[0], cache_k.shape[3]
    tk = SAMPLE_TK
    npc = past // tk
    rows = FOX_HEADS * t_new
    last = npc - 1
    kern = functools.partial(_fox_sample_kernel, n_past_chunks=npc, tk=tk, t_new=t_new)
    cache_spec = pl.BlockSpec((1, FOX_HEADS, FOX_HEAD_DIM, tk), lambda b, c: (b, 0, 0, jnp.minimum(c, last)))
    return pl.pallas_call(
        kern,
        grid=(nb, npc + 1),
        in_specs=[
            pl.BlockSpec((t_new, FOX_WIDTH), lambda b, c: (b, 0)),
            pl.BlockSpec((t_new, FOX_WIDTH), lambda b, c: (b, 0)),
            pl.BlockSpec((t_new, FOX_WIDTH), lambda b, c: (b, 0)),
            pl.BlockSpec((1, FOX_HEADS, past + LANES), lambda b, c: (b, 0, 0)),
            cache_spec, cache_spec,
        ],
        out_specs=pl.BlockSpec((t_new, FOX_WIDTH), lambda b, c: (b, 0)),
        out_shape=jax.ShapeDtypeStruct((nb * t_new, FOX_WIDTH), BF16),
        scratch_shapes=[
            pltpu.VMEM((rows, LANES), F32),
            pltpu.VMEM((npc + 1, FOX_HEADS, tk), F32),
            pltpu.VMEM((rows, LANES), F32),
            pltpu.VMEM((rows, LANES), F32),
            pltpu.VMEM((rows, FOX_HEAD_DIM), F32),
        ],
        compiler_params=pltpu.CompilerParams(dimension_semantics=("arbitrary", "arbitrary"),
                                             vmem_limit_bytes=VMEM_LIMIT),
        name="fox_sample",
    )(qs, k_new, v_new, lf_all_t, cache_k, cache_v)


def _hgrn_kernel(hq_ref, hk_ref, lfh_ref, hi_ref, hg_ref, s0_ref, norm_ref, tri_ref,
                 o_ref, sout_ref, st_ref, qq_ref, kk_ref, eb_ref, b_ref, oin_ref, od_ref):
    t = pl.program_id(1)
    tc = hq_ref.shape[0]
    blk = min(HG_BLOCK, tc)
    nblk = tc // blk
    heads = [slice(h * HG_DIM, (h + 1) * HG_DIM) for h in range(HG_HEADS)]

    @pl.when(t == 0)
    def _():
        for h in range(HG_HEADS):
            st_ref[h] = s0_ref[0, h].T

    def at_block_row(x, r):
        x3 = x.reshape(nblk, blk, HG_DIM)
        return jnp.broadcast_to(x3[:, r:r + 1, :], x3.shape).reshape(tc, HG_DIM)

    def rel_to_middle(b):
        return b - at_block_row(b, blk // 2 - 1)

    worst = jnp.zeros((), F32)
    for sl in heads:
        b = _sum_by_01_matrix(tri_ref[...], lfh_ref[:, sl])
        eb = jnp.exp(b)
        qq_ref[:, sl] = (hq_ref[:, sl].astype(F32) * eb).astype(BF16)
        kk_ref[:, sl] = (hk_ref[:, sl].astype(F32) * jnp.exp(at_block_row(b, blk - 1) - b)).astype(BF16)
        eb_ref[:, sl] = eb
        b_ref[:, sl] = b
        worst = jnp.maximum(worst, jnp.max(jnp.abs(rel_to_middle(b))))

    def carried_state_and_output():
        for j in range(nblk):
            rows = slice(j * blk, (j + 1) * blk)
            for h, sl in enumerate(heads):
                st = st_ref[h]
                oin_ref[rows, sl] = lax.dot_general(qq_ref[rows, sl], st.astype(BF16), (((1,), (1,)), ((), ())),
                                                    preferred_element_type=F32)
                upd = lax.dot_general(hi_ref[rows, sl], kk_ref[rows, sl], (((0,), (0,)), ((), ())),
                                      preferred_element_type=F32)
                st_ref[h] = st * eb_ref[(j + 1) * blk - 1:(j + 1) * blk, sl] + upd
        for sl in heads:
            o = oin_ref[:, sl] + od_ref[:, sl]
            y = (_rms_scale(o) * norm_ref[...]) * hg_ref[:, sl].astype(F32)
            o_ref[:, sl] = y.astype(o_ref.dtype)

    splittable = worst <= HG_SPLIT_MAX

    @pl.when(splittable)
    def _():
        row = lax.broadcasted_iota(jnp.int32, (tc, tc), 0)
        col = lax.broadcasted_iota(jnp.int32, (tc, tc), 1)
        pair_in_block = (row // blk == col // blk) & (col <= row)
        for sl in heads:
            b_rel = rel_to_middle(b_ref[:, sl])
            qs = (hq_ref[:, sl].astype(F32) * jnp.exp(b_rel)).astype(BF16)
            ks = (hk_ref[:, sl].astype(F32) * jnp.exp(-b_rel)).astype(BF16)
            a = lax.dot_general(qs, ks, (((1,), (1,)), ((), ())), preferred_element_type=F32)
            a = jnp.where(pair_in_block, a, 0.0).astype(BF16)
            od_ref[:, sl] = jnp.dot(a, hi_ref[:, sl], preferred_element_type=F32)
        carried_state_and_output()

    @pl.when(jnp.logical_not(splittable))
    def _():
        row_in_blk = lax.broadcasted_iota(jnp.int32, (tc, HG_DIM), 0) % blk
        for sl in heads:
            q = hq_ref[:, sl].astype(F32)
            k = hk_ref[:, sl].astype(F32)
            v = hi_ref[:, sl].astype(F32)
            b = b_ref[:, sl]

            def lag_step(lag, od):
                k_l = pltpu.roll(k, lag, axis=0)
                b_l = pltpu.roll(b, lag, axis=0)
                v_l = pltpu.roll(v, lag, axis=0)
                w = q * k_l * jnp.exp(jnp.minimum(b - b_l, 0.0))
                w = jnp.where(row_in_blk >= lag, w, 0.0)
                return od + jnp.sum(w, axis=1, keepdims=True) * v_l

            od_ref[:, sl] = lax.fori_loop(1, blk, lag_step, jnp.sum(q * k, axis=1, keepdims=True) * v)
        carried_state_and_output()

    @pl.when(t == pl.num_programs(1) - 1)
    def _():
        for h in range(HG_HEADS):
            sout_ref[0, h] = st_ref[h].T


def _hgrn_cumsum_matrix(n, blk):
    t = np.arange(n)[:, None]
    s = np.arange(n)[None, :]
    return jnp.asarray((((t // blk) == (s // blk)) & (s <= t)).astype(np.float32), BF16)


def _hgrn(hq, hk, lfh, hi, hg, s0, norm, *, nseq):
    rows = hq.shape[0]
    t_len = rows // nseq
    tc = min(HG_TC, t_len)
    nt = t_len // tc
    blk = min(HG_BLOCK, tc)
    tri = _hgrn_cumsum_matrix(tc, blk)
    row = pl.BlockSpec((tc, HG_WIDTH), lambda b, t: (b * nt + t, 0))
    state = pl.BlockSpec((1, HG_HEADS, HG_DIM, HG_DIM), lambda b, t: (b, 0, 0, 0))
    return pl.pallas_call(
        _hgrn_kernel,
        grid=(nseq, nt),
        in_specs=[row, row, row, row, row, state,
                  pl.BlockSpec((1, HG_DIM), lambda b, t: (0, 0)),
                  pl.BlockSpec((tc, tc), lambda b, t: (0, 0))],
        out_specs=[row, state],
        out_shape=[jax.ShapeDtypeStruct((rows, HG_WIDTH), BF16),
                   jax.ShapeDtypeStruct((nseq, HG_HEADS, HG_DIM, HG_DIM), F32)],
        scratch_shapes=[
            pltpu.VMEM((HG_HEADS, HG_DIM, HG_DIM), F32),
            pltpu.VMEM((tc, HG_WIDTH), BF16),
            pltpu.VMEM((tc, HG_WIDTH), BF16),
            pltpu.VMEM((tc, HG_WIDTH), F32),
            pltpu.VMEM((tc, HG_WIDTH), F32),
            pltpu.VMEM((tc, HG_WIDTH), F32),
            pltpu.VMEM((tc, HG_WIDTH), F32),
        ],
        compiler_params=pltpu.CompilerParams(dimension_semantics=("arbitrary", "arbitrary"),
                                             vmem_limit_bytes=VMEM_LIMIT),
        name="hgrn",
    )(hq, hk, lfh, hi, hg, s0, norm, tri)


def _mixffn_kernel(x_ref, of_ref, oh_ref, ga_ref, gb_ref, hist_ref,
                   wbf_ref, wbh_ref, wout_ref, wup_ref, wdn_ref,
                   npost_ref, npre2_ref, npost2_ref, cw_ref, cb_ref,
                   y_ref, conv_ref, tail_ref, *, seg_len):
    tm = x_ref.shape[0]
    br_f = jnp.dot(of_ref[...], wbf_ref[...], preferred_element_type=F32)
    br_h = jnp.dot(oh_ref[...], wbh_ref[...], preferred_element_type=F32)
    merged = ga_ref[...].astype(F32) * br_f + gb_ref[...].astype(F32) * br_h
    mix = jnp.dot(merged.astype(BF16), wout_ref[...], preferred_element_type=F32)
    x1 = x_ref[...] + _rms_scale(mix) * npost_ref[...]

    h2 = (_rms_scale(x1) * npre2_ref[...]).astype(BF16)
    def up_chunk(j):
        w = jnp.concatenate([wup_ref[:, j * FFN_CHUNK:(j + 1) * FFN_CHUNK],
                             wup_ref[:, D_FF + j * FFN_CHUNK:D_FF + (j + 1) * FFN_CHUNK]], axis=1)
        return jnp.dot(h2, w, preferred_element_type=F32)

    n_chunks = D_FF // FFN_CHUNK
    carried = seg_len >= tm
    if carried:
        @pl.when(pl.program_id(0) == 0)
        def _():
            tail_ref[...] = hist_ref[0]
    rowi = lax.broadcasted_iota(jnp.int32, (tm, FFN_CHUNK), 0)
    ff = None
    up_next = up_chunk(0)
    for j in range(n_chunks):
        cols = slice(j * FFN_CHUNK, (j + 1) * FFN_CHUNK)
        up = up_next
        if j + 1 < n_chunks:
            up_next = up_chunk(j + 1)
        a = up[:, :FFN_CHUNK]
        g = up[:, FFN_CHUNK:]
        prev1 = pltpu.roll(a, 1, axis=0)
        prev2 = pltpu.roll(a, 2, axis=0)
        if carried:
            t0 = tail_ref[0:1, cols]
            t1 = tail_ref[1:2, cols]
            prev1 = jnp.where(rowi == 0, t1, prev1)
            prev2 = jnp.where(rowi == 0, t0, jnp.where(rowi == 1, t1, prev2))
            tail_ref[:, cols] = a[tm - 2:, :]
            conv_ref[0, :, cols] = a[tm - 2:, :]
        else:
            for s in range(tm // seg_len):
                h0 = hist_ref[s, 0:1, cols]
                h1 = hist_ref[s, 1:2, cols]
                prev1 = jnp.where(rowi == s * seg_len, h1, prev1)
                prev2 = jnp.where(rowi == s * seg_len, h0, jnp.where(rowi == s * seg_len + 1, h1, prev2))
                conv_ref[s, :, cols] = a[(s + 1) * seg_len - 2:(s + 1) * seg_len, :]
        c = cb_ref[:, cols] + cw_ref[0:1, cols] * prev2 + cw_ref[1:2, cols] * prev1 + cw_ref[2:3, cols] * a
        act = (jax.nn.gelu(c, approximate=True) * g).astype(BF16)
        part = jnp.dot(act, wdn_ref[cols, :], preferred_element_type=F32)
        ff = part if ff is None else ff + part
    y_ref[...] = x1 + _rms_scale(ff) * npost2_ref[...]


def _mixffn(x, o_fox, o_hg, ga, gb, hist, w, *, seg_len):
    rows = x.shape[0]
    tm = min(FFN_TM, rows)
    n = rows // tm
    nseg = hist.shape[0]
    row = lambda width: pl.BlockSpec((tm, width), lambda i: (i, 0))
    weights = [w["bf"], w["bh"], w["out"], w["up"], w["down"]]
    smalls = [w["npost"], w["npre2"], w["npost2"], w["conv_w"], w["conv_b"]]
    hist_spec = pl.BlockSpec(hist.shape, lambda i: (0, 0, 0))
    scratch = [pltpu.VMEM((2, D_FF), F32)]
    return pl.pallas_call(
        functools.partial(_mixffn_kernel, seg_len=seg_len),
        grid=(n,),
        in_specs=[row(D_MODEL), row(FOX_WIDTH), row(HG_WIDTH), row(D_MODEL), row(D_MODEL), hist_spec]
                 + [_const_spec(a.shape) for a in weights] + [_const_spec(a.shape) for a in smalls],
        out_specs=[row(D_MODEL), pl.BlockSpec((nseg, 2, D_FF), lambda i: (0, 0, 0))],
        out_shape=[jax.ShapeDtypeStruct((rows, D_MODEL), F32), jax.ShapeDtypeStruct((nseg, 2, D_FF), F32)],
        scratch_shapes=scratch,
        compiler_params=pltpu.CompilerParams(dimension_semantics=("arbitrary",), vmem_limit_bytes=VMEM_LIMIT),
        name="mixffn",
    )(x, o_fox, o_hg, ga, gb, hist, *weights, *smalls)


def _prep_w_in(w_in, fox_f_bias):
    offs = np.cumsum([0] + IN_SIZES)
    seg = [w_in[:, int(offs[i]):int(offs[i + 1])] for i in range(len(IN_SIZES))]
    pad = jnp.zeros((D_MODEL, LANES - 3 * FOX_HEADS), w_in.dtype)
    f3 = jnp.concatenate([seg[3], seg[3], seg[3], pad], axis=1)
    w_all = jnp.concatenate(seg[:3] + [f3] + seg[4:], axis=1).astype(BF16)
    fb = fox_f_bias.astype(F32)
    fb3 = jnp.concatenate([fb, fb, fb, jnp.zeros((LANES - 3 * FOX_HEADS,), F32)]).reshape(1, LANES)
    return w_all, fb3


def kernel(x_prompt, x_sample, cache_fox_k, cache_fox_v, cache_fox_logf, state_hgrn, state_ffn_conv, norm_mix_pre, norm_mix_post, w_in, fox_f_bias, hgrn_lb_logits, hgrn_norm, w_branch_fox, w_branch_hgrn, w_out, norm_ffn_pre, norm_ffn_post, w_up, ffn_conv_w, ffn_conv_b, w_down):
    depth = w_in.shape[0]
    assert depth == 1 and hgrn_lb_logits.shape[0] == 2
    bp, seq, _ = x_prompt.shape
    assert bp == 1
    nb, t_new, _ = x_sample.shape
    past = cache_fox_k.shape[2]

    w_all, fb3 = _prep_w_in(w_in[0], fox_f_bias[0])
    g_pre = norm_mix_pre[0].reshape(1, D_MODEL)
    lbl = hgrn_lb_logits.astype(F32)
    hnorm = hgrn_norm[0].astype(F32).reshape(1, HG_DIM)
    w = {
        "bf": w_branch_fox[0].astype(BF16), "bh": w_branch_hgrn[0].astype(BF16), "out": w_out[0].astype(BF16),
        "up": w_up[0].astype(BF16), "down": w_down[0].astype(BF16),
        "npost": norm_mix_post[0].reshape(1, D_MODEL), "npre2": norm_ffn_pre[0].reshape(1, D_MODEL),
        "npost2": norm_ffn_post[0].reshape(1, D_MODEL),
        "conv_w": ffn_conv_w[0], "conv_b": ffn_conv_b[0].reshape(1, D_FF),
    }

    xp = x_prompt.reshape(seq, D_MODEL)
    (qt, kh, vt, stats, pk, pv, plf, hq, hk, lfh, hi, hg, ga, gb) = _proj(xp, g_pre, w_all, fb3, lbl, fold=True)
    o_fox = _fox_prompt(qt, kh, vt, stats)
    s0 = jnp.zeros((1, HG_HEADS, HG_DIM, HG_DIM), F32)
    o_hg, p_state = _hgrn(hq, hk, lfh, hi, hg, s0, hnorm, nseq=1)
    hist0 = jnp.zeros((1, 2, D_FF), F32)
    yp, pconv = _mixffn(xp, o_fox, o_hg, ga, gb, hist0, w, seg_len=seq)

    xs = x_sample.reshape(nb * t_new, D_MODEL)
    (qs, sk, sv, slf, hq, hk, lfh, hi, hg, ga, gb) = _proj(xs, g_pre, w_all, fb3, lbl, fold=False)
    lf_all_t = jnp.concatenate([
        jnp.swapaxes(cache_fox_logf[0].astype(F32), 1, 2),
        jnp.swapaxes(slf.reshape(nb, t_new, FOX_HEADS), 1, 2),
        jnp.zeros((nb, FOX_HEADS, LANES - t_new), F32)], axis=2)
    cache_kt = jnp.transpose(cache_fox_k[0], (0, 2, 3, 1))
    cache_vt = jnp.transpose(cache_fox_v[0], (0, 2, 3, 1))
    o_fox_s = _fox_sample(qs, sk, sv, lf_all_t, cache_kt, cache_vt, t_new=t_new)
    o_hg_s, s_state = _hgrn(hq, hk, lfh, hi, hg, state_hgrn[0].astype(F32), hnorm, nseq=nb)
    ys, sconv = _mixffn(xs, o_fox_s, o_hg_s, ga, gb, state_ffn_conv[0], w, seg_len=t_new)

    return (
        yp.reshape(bp, seq, D_MODEL),
        ys.reshape(nb, t_new, D_MODEL),
        pk.reshape(1, bp, seq, FOX_HEADS, FOX_HEAD_DIM),
        pv.reshape(1, bp, seq, FOX_HEADS, FOX_HEAD_DIM),
        plf.reshape(1, bp, seq, FOX_HEADS),
        p_state.reshape(1, bp, HG_HEADS, HG_DIM, HG_DIM),
        pconv.reshape(1, bp, 2, D_FF),
        sk.reshape(1, nb, t_new, FOX_HEADS, FOX_HEAD_DIM),
        sv.reshape(1, nb, t_new, FOX_HEADS, FOX_HEAD_DIM),
        slf.reshape(1, nb, t_new, FOX_HEADS),
        s_state.reshape(1, nb, HG_HEADS, HG_DIM, HG_DIM),
        sconv.reshape(1, nb, 2, D_FF),
    )
```

```python
import functools

import numpy as np
import jax
import jax.numpy as jnp
from jax import lax
from jax.experimental import pallas as pl
from jax.experimental.pallas import tpu as pltpu

F32 = jnp.float32
BF16 = jnp.bfloat16

D_MODEL = 1024
FOX_HEADS = 8
FOX_HEAD_DIM = 64
FOX_WIDTH = FOX_HEADS * FOX_HEAD_DIM
HG_HEADS = 4
HG_DIM = 128
HG_WIDTH = HG_HEADS * HG_DIM
D_FF = 2816
RMS_EPS = 1e-6
NEG_INF = -1e30
LOG2E = 1.4426950408889634
FOX_SKIP_NATS = 110.0
IN_SIZES = [FOX_WIDTH, FOX_WIDTH, FOX_WIDTH, FOX_HEADS, HG_WIDTH, HG_WIDTH, HG_WIDTH, HG_WIDTH, D_MODEL, D_MODEL]

LANES = 128
FOX_PAD = 2 * FOX_HEAD_DIM
FOX_V_ROWS = FOX_HEAD_DIM + 16
HG_BLOCK = 64
HG_SPLIT_MAX = 60.0
VMEM_LIMIT = 56 * 1024 * 1024

PROJ_TM = 256
FOX_TQ = 512
FOX_TK = 512
FOX_SLOTS = 3
HG_TC = 256
FFN_TM = 512
FFN_ROWS = 256
FFN_CHUNK = 256
SAMPLE_TK = 1024

_C_Q, _C_K, _C_V, _C_F = 0, 512, 1024, 1536
_C_HQ, _C_HF, _C_HI, _C_HG = 1664, 2176, 2688, 3200
_C_GA, _C_GB, _C_END = 3712, 4736, 5760


def _split3(x):
    hi = x.astype(BF16)
    r = x - hi.astype(F32)
    mid = r.astype(BF16)
    lo = (r - mid.astype(F32)).astype(BF16)
    return hi, mid, lo


def _sum_by_01_matrix(mat01, x):
    cat = jnp.concatenate(_split3(x), axis=1)
    y = jnp.dot(mat01, cat, preferred_element_type=F32)
    return y[:, :LANES] + y[:, LANES:2 * LANES] + y[:, 2 * LANES:]


def _rms_scale(x):
    return x * lax.rsqrt(jnp.mean(x * x, axis=-1, keepdims=True) + RMS_EPS)


def _log_sigmoid(x):
    return jnp.minimum(x, 0.0) - jnp.log1p(jnp.exp(-jnp.abs(x)))


def _sigmoid(x):
    return 1.0 / (1.0 + jnp.exp(-x))


def _proj_kernel(*refs, fold):
    if fold:
        (x_ref, g_ref, w_ref, fb_ref, lbl_ref, tri_ref, pq_ref, pk_ref, cq_ref, ck_ref, cv_ref, seg_ref,
         qh_ref, kh_ref, vh_ref, stat_ref, kout_ref, vout_ref, lf_ref, hq_ref, hk_ref, lfh_ref, hi_ref, hg_ref,
         ga_ref, gb_ref, carry_ref) = refs
    else:
        (x_ref, g_ref, w_ref, fb_ref, lbl_ref,
         qs_ref, kout_ref, vout_ref, lf_ref, hq_ref, hk_ref, lfh_ref, hi_ref, hg_ref,
         ga_ref, gb_ref) = refs

    h = (_rms_scale(x_ref[...]) * g_ref[...]).astype(BF16)

    z = jnp.dot(h, w_ref[...], preferred_element_type=F32)

    zq = z[:, _C_Q:_C_K] * (FOX_HEAD_DIM ** -0.5)
    zk = z[:, _C_K:_C_V]
    zv = z[:, _C_V:_C_F]
    kout_ref[...] = zk
    vout_ref[...] = zv
    logf = _log_sigmoid(z[:, _C_F:_C_HQ] + fb_ref[...])
    lf_ref[...] = logf[:, :FOX_HEADS]

    l0 = lbl_ref[0:1, :]
    l1 = lbl_ref[1:2, :]
    lmax = jnp.maximum(l0, l1)
    e0 = jnp.exp(l0 - lmax)
    lb = e0 / (e0 + jnp.exp(l1 - lmax))
    f = lb + (1.0 - lb) * _sigmoid(z[:, _C_HF:_C_HI])
    hq_ref[...] = z[:, _C_HQ:_C_HF].astype(BF16)
    hk_ref[...] = (1.0 - f).astype(BF16)
    lfh_ref[...] = jnp.log(f)
    hi_ref[...] = z[:, _C_HI:_C_HG].astype(BF16)
    hg_ref[...] = _sigmoid(z[:, _C_HG:_C_GA]).astype(BF16)
    ga_ref[...] = _sigmoid(z[:, _C_GA:_C_GB]).astype(BF16)
    gb_ref[...] = _sigmoid(z[:, _C_GB:_C_END]).astype(BF16)

    if not fold:
        qs_ref[...] = zq.astype(BF16)
        return

    @pl.when(pl.program_id(0) == 0)
    def _():
        carry_ref[...] = jnp.zeros_like(carry_ref)

    cum = carry_ref[...] + _sum_by_01_matrix(tri_ref[...], logf)
    carry_ref[...] = cum[-1:, :]

    seg = seg_ref[...]
    qn2 = jnp.dot((zq * zq).astype(BF16), seg, preferred_element_type=F32)
    kn2 = jnp.dot((zk * zk).astype(BF16), seg, preferred_element_type=F32)
    dg = jnp.dot((zq * zk).astype(BF16), seg, preferred_element_type=F32)
    stat_ref[0, 0:1, :] = jnp.max(qn2, axis=0, keepdims=True)
    stat_ref[0, 1:2, :] = jnp.max(kn2, axis=0, keepdims=True)
    stat_ref[0, 2:3, :] = jnp.min(dg, axis=0, keepdims=True)
    stat_ref[0, 3:4, :] = cum[0:1, :]
    stat_ref[0, 4:5, :] = cum[-1:, :]
    stat_ref[0, 5:8, :] = jnp.zeros((3, LANES), F32)

    zq = zq * LOG2E
    c_hi, c_mid, c_lo = _split3(cum * LOG2E)
    lane = lax.broadcasted_iota(jnp.int32, cum.shape, 1)
    pieces = jnp.where(lane < 8, c_hi, jnp.where(lane < 16, c_mid, c_lo))
    pieces = jnp.where(lane < 24, pieces, jnp.zeros_like(pieces))
    ex_q = jnp.dot(pieces, pq_ref[...], preferred_element_type=F32) + cq_ref[...]
    ex_k = jnp.dot(pieces, pk_ref[...], preferred_element_type=F32) + ck_ref[...]
    ex_v = cv_ref[...]

    low = lax.broadcasted_iota(jnp.int32, (zq.shape[0], LANES), 1) < FOX_HEAD_DIM
    for src, ex, dst, transposed in ((zq, ex_q, qh_ref, True), (zk, ex_k, kh_ref, False), (zv, ex_v, vh_ref, True)):
        for c in range(FOX_WIDTH // LANES):
            pair = src[:, c * LANES:(c + 1) * LANES]
            swapped = pltpu.roll(pair, FOX_HEAD_DIM, axis=1)
            for j, data in enumerate((pair, swapped)):
                hd = 2 * c + j
                blk = jnp.where(low, data, ex[:, hd * LANES:(hd + 1) * LANES])
                if not transposed:
                    dst[hd] = blk.astype(BF16)
                elif len(dst.shape) == 3:
                    dst[hd] = blk.T.astype(BF16)
                else:
                    dst[hd, 0] = blk.T[:dst.shape[2]].astype(BF16)


def _bias_fold_constants():
    pq = np.zeros((LANES, FOX_HEADS * LANES), np.float32)
    pk = np.zeros((LANES, FOX_HEADS * LANES), np.float32)
    cq = np.zeros((1, FOX_HEADS * LANES), np.float32)
    ck = np.zeros((1, FOX_HEADS * LANES), np.float32)
    cv = np.zeros((1, FOX_HEADS * LANES), np.float32)
    for h in range(FOX_HEADS):
        base = h * LANES + FOX_HEAD_DIM
        for p in range(3):
            pq[p * 8 + h, base + p] = 1.0
            ck[0, base + p] = 1.0
            pk[p * 8 + h, base + 3 + p] = -1.0
            cq[0, base + 3 + p] = 1.0
        cv[0, base] = 1.0
    return (jnp.asarray(pq, BF16), jnp.asarray(pk, BF16), jnp.asarray(cq), jnp.asarray(ck), jnp.asarray(cv))


def _const_spec(shape, single=True):
    nd = len(shape)
    if single:
        return pl.BlockSpec(shape, lambda *_: (0,) * nd, pipeline_mode=pl.Buffered(1))
    return pl.BlockSpec(shape, lambda *_: (0,) * nd)


def _proj(x, gain, w_all, fb3, lb_logits, *, fold):
    rows = x.shape[0]
    tm = min(PROJ_TM, rows)
    n = rows // tm
    row = lambda width: pl.BlockSpec((tm, width), lambda i: (i, 0))
    in_specs = [row(D_MODEL), _const_spec((1, D_MODEL)), _const_spec(w_all.shape), _const_spec((1, LANES)),
                _const_spec(lb_logits.shape)]
    args = [x, gain, w_all, fb3, lb_logits]
    common_out = [
        (jax.ShapeDtypeStruct((rows, FOX_WIDTH), F32), row(FOX_WIDTH)),
        (jax.ShapeDtypeStruct((rows, FOX_WIDTH), F32), row(FOX_WIDTH)),
        (jax.ShapeDtypeStruct((rows, FOX_HEADS), F32), row(FOX_HEADS)),
        (jax.ShapeDtypeStruct((rows, HG_WIDTH), BF16), row(HG_WIDTH)),
        (jax.ShapeDtypeStruct((rows, HG_WIDTH), BF16), row(HG_WIDTH)),
        (jax.ShapeDtypeStruct((rows, HG_WIDTH), F32), row(HG_WIDTH)),
        (jax.ShapeDtypeStruct((rows, HG_WIDTH), BF16), row(HG_WIDTH)),
        (jax.ShapeDtypeStruct((rows, HG_WIDTH), BF16), row(HG_WIDTH)),
        (jax.ShapeDtypeStruct((rows, D_MODEL), BF16), row(D_MODEL)),
        (jax.ShapeDtypeStruct((rows, D_MODEL), BF16), row(D_MODEL)),
    ]
    scratch = []
    if fold:
        tri = jnp.asarray(np.tril(np.ones((tm, tm), np.float32)), BF16)
        seg = np.zeros((FOX_WIDTH, LANES), np.float32)
        seg[np.arange(FOX_WIDTH), np.arange(FOX_WIDTH) // FOX_HEAD_DIM] = 1.0
        consts = _bias_fold_constants() + (jnp.asarray(seg, BF16),)
        in_specs += [_const_spec(tri.shape)] + [_const_spec(c.shape) for c in consts]
        args += [tri, *consts]
        head_major = (jax.ShapeDtypeStruct((FOX_HEADS, rows, FOX_PAD), BF16),
                      pl.BlockSpec((FOX_HEADS, tm, FOX_PAD), lambda i: (0, i, 0)))
        head_major_t = lambda depth: (jax.ShapeDtypeStruct((FOX_HEADS, depth, rows), BF16),
                                      pl.BlockSpec((FOX_HEADS, depth, tm), lambda i: (0, 0, i)))
        stats = (jax.ShapeDtypeStruct((n, 8, LANES), F32), pl.BlockSpec((1, 8, LANES), lambda i: (i, 0, 0)))
        v_tiles = (jax.ShapeDtypeStruct((FOX_HEADS, n, FOX_V_ROWS, tm), BF16),
                   pl.BlockSpec((FOX_HEADS, 1, FOX_V_ROWS, tm), lambda i: (0, i, 0, 0)))
        outs = [head_major_t(FOX_PAD), head_major, v_tiles, stats] + common_out
        scratch = [pltpu.VMEM((1, LANES), F32)]
    else:
        outs = [(jax.ShapeDtypeStruct((rows, FOX_WIDTH), BF16), row(FOX_WIDTH))] + common_out
    return pl.pallas_call(
        functools.partial(_proj_kernel, fold=fold),
        grid=(n,),
        in_specs=in_specs,
        out_specs=[o[1] for o in outs],
        out_shape=[o[0] for o in outs],
        scratch_shapes=scratch,
        compiler_params=pltpu.CompilerParams(dimension_semantics=("arbitrary",), vmem_limit_bytes=VMEM_LIMIT),
        name="proj_fold" if fold else "proj",
    )(*args)


def _fox_kernel(ks_ref, nact_ref, order_ref, qn_ref, kn_ref, cl_ref, qt_ref, k_hbm, vt_hbm, o_ref,
                kbuf, vbuf, sem, slot0_ref, m_ref, acc_ref, *, nq):
    q_blk = pl.program_id(0)
    k_first = ks_ref[q_blk]
    tq = qt_ref.shape[2]
    n_slots, tk = kbuf.shape[0], kbuf.shape[2]
    tiles, tile = vbuf.shape[2], vbuf.shape[4]
    table = lambda ref, blk, h: lax.bitcast_convert_type(ref[blk * FOX_HEADS + h], F32)

    def successor(q, k):
        at_diagonal = k == q
        return (jnp.where(at_diagonal, q + 1, q),
                jnp.where(at_diagonal, ks_ref[jnp.minimum(q + 1, nq - 1)], k + 1))

    def for_swept_heads(q, k, slot, action):
        n_swept = nact_ref[q * nq + k]
        for j in range(FOX_HEADS):
            @pl.when(j < n_swept)
            def _():
                h = order_ref[q * FOX_HEADS + j]
                rows = pl.ds(pl.multiple_of(k * tk, tk), tk)
                action(pltpu.make_async_copy(k_hbm.at[h, rows, :], kbuf.at[slot, h], sem.at[0, slot]))
                action(pltpu.make_async_copy(vt_hbm.at[h, pl.ds(k * tiles, tiles)], vbuf.at[slot, h],
                                             sem.at[1, slot]))

    def start(q, k, slot):
        @pl.when(q < nq)
        def _():
            for_swept_heads(q, k, slot, lambda c: c.start())

    def wait(q, k, slot):
        for_swept_heads(q, k, slot, lambda c: c.wait())

    def request_second_next(k_blk, slot):
        q1, k1 = successor(q_blk, k_blk)
        q2, k2 = successor(jnp.minimum(q1, nq - 1), k1)
        start(jnp.where(q1 < nq, q2, nq), k2, lax.rem(slot + 2, n_slots))

    @pl.when(q_blk == 0)
    def _():
        slot0_ref[0] = 0
        start(q_blk, k_first, 0)
        q1, k1 = successor(q_blk, k_first)
        start(q1, k1, 1)

    slot0 = slot0_ref[0]
    slot_of = lambda k_blk: lax.rem(slot0 + (k_blk - k_first), n_slots)
    m_ref[...] = jnp.full_like(m_ref, NEG_INF)
    acc_ref[...] = jnp.zeros_like(acc_ref)

    def logits(h, slot):
        return jnp.dot(kbuf[slot, h], qt_ref[h], preferred_element_type=F32)

    def softmax_update(h, s, k_blk, slot, masked):
        if masked:
            key = lax.broadcasted_iota(jnp.int32, (tk, tq), 0)
            qry = lax.broadcasted_iota(jnp.int32, (tk, tq), 1)
            s = jnp.where(key <= qry, s, NEG_INF)
        bound = table(qn_ref, q_blk, h) * table(kn_ref, k_blk, h)
        if not masked:
            cum_q = jnp.sum(qt_ref[h, FOX_HEAD_DIM:FOX_HEAD_DIM + 3, :].astype(F32), axis=0, keepdims=True)
            bound = cum_q + (bound - table(cl_ref, k_blk, h))
        m_prev = m_ref[h]
        m_new = jnp.maximum(m_prev, bound)
        alpha = jnp.exp2(m_prev - m_new)
        p = jnp.exp2(s - m_new).astype(BF16)
        pv = sum(jnp.dot(vbuf[slot, h, j], p[j * tile:(j + 1) * tile], preferred_element_type=F32)
                 for j in range(tiles))
        acc_ref[h] = alpha * acc_ref[h] + pv
        m_ref[h] = m_new

    def sweep(heads, k_blk, slot, masked):
        s_next = logits(heads[0], slot)
        for j, h in enumerate(heads):
            s = s_next
            if j + 1 < len(heads):
                s_next = logits(heads[j + 1], slot)
            softmax_update(h, s, k_blk, slot, masked)

    def past_block(k_blk, carry):
        slot = slot_of(k_blk)
        wait(q_blk, k_blk, slot)
        request_second_next(k_blk, slot)
        n_active = nact_ref[q_blk * nq + k_blk]
        for count in range(2, FOX_HEADS + 1, 2):
            @pl.when(n_active == count)
            def _():
                sweep([order_ref[q_blk * FOX_HEADS + j] for j in range(count)], k_blk, slot, False)
        return carry

    lax.fori_loop(k_first, q_blk, past_block, 0)

    slot = slot_of(q_blk)
    wait(q_blk, q_blk, slot)
    request_second_next(q_blk, slot)
    slot0_ref[0] = lax.rem(slot + 1, n_slots)

    sweep(list(range(FOX_HEADS)), q_blk, slot, True)
    for c in range(FOX_WIDTH // LANES):
        halves = []
        for hd in (2 * c, 2 * c + 1):
            acc = acc_ref[hd]
            halves.append(acc[:FOX_HEAD_DIM, :] / acc[FOX_HEAD_DIM:FOX_HEAD_DIM + 1, :])
        o_ref[:, c * LANES:(c + 1) * LANES] = jnp.concatenate(halves, axis=0).T.astype(o_ref.dtype)


def _fox_schedule(stats, nq):
    st = stats[:, :5, :FOX_HEADS].reshape(nq, -1, 5, FOX_HEADS)
    qn = jnp.sqrt(jnp.max(st[:, :, 0], axis=1)) * 1.01
    kn = jnp.sqrt(jnp.max(st[:, :, 1], axis=1)) * 1.01
    dmin = jnp.min(st[:, :, 2], axis=1)
    c_first = st[:, 0, 3]
    c_last = st[:, -1, 4]
    bound = qn[:, None] * kn[None, :] + (c_first - dmin)[:, None] - c_last[None, :]
    blk = jnp.arange(nq, dtype=jnp.int32)
    drop = (bound < -FOX_SKIP_NATS) & (blk[None, :] < blk[:, None])[:, :, None]
    prefix = jnp.cumsum(jnp.logical_not(drop).astype(jnp.int32), axis=1) == 0
    kstart_h = jnp.sum(prefix.astype(jnp.int32), axis=1)
    kstart = jnp.min(kstart_h, axis=1)
    hd = jnp.arange(FOX_HEADS, dtype=jnp.int32)
    before = (kstart_h[:, None, :] < kstart_h[:, :, None]) | (
        (kstart_h[:, None, :] == kstart_h[:, :, None]) & (hd[None, None, :] < hd[None, :, None]))
    rank = jnp.sum(before.astype(jnp.int32), axis=2)
    order = jnp.sum(jnp.where(rank[:, None, :] == hd[None, :, None], hd[None, None, :], 0), axis=2)
    n_active = jnp.sum((kstart_h[:, None, :] <= blk[None, :, None]).astype(jnp.int32), axis=2)
    n_active = jnp.minimum((n_active + 1) // 2 * 2, FOX_HEADS)
    as_i32 = lambda a: a.astype(jnp.int32).reshape(-1)
    bits = lambda a: lax.bitcast_convert_type(a.astype(F32), jnp.int32).reshape(-1)
    return (as_i32(kstart), as_i32(n_active), as_i32(order), bits(qn * LOG2E), bits(kn), bits(c_last * LOG2E))


def _fox_prompt(qt, kh, vt, stats):
    seq = kh.shape[1]
    assert FOX_TQ == FOX_TK
    nq = seq // FOX_TQ
    tile = vt.shape[3]
    tables = _fox_schedule(stats, nq)
    grid_spec = pltpu.PrefetchScalarGridSpec(
        num_scalar_prefetch=len(tables),
        grid=(nq,),
        in_specs=[
            pl.BlockSpec((FOX_HEADS, FOX_PAD, FOX_TQ), lambda q, *_: (0, 0, q)),
            pl.BlockSpec(memory_space=pl.ANY),
            pl.BlockSpec(memory_space=pl.ANY),
        ],
        out_specs=pl.BlockSpec((FOX_TQ, FOX_WIDTH), lambda q, *_: (q, 0)),
        scratch_shapes=[pltpu.VMEM((FOX_SLOTS, FOX_HEADS, FOX_TK, FOX_PAD), BF16),
                        pltpu.VMEM((FOX_SLOTS, FOX_HEADS, FOX_TK // tile, FOX_V_ROWS, tile), BF16),
                        pltpu.SemaphoreType.DMA((2, FOX_SLOTS)),
                        pltpu.SMEM((1,), jnp.int32),
                        pltpu.VMEM((FOX_HEADS, 1, FOX_TQ), F32),
                        pltpu.VMEM((FOX_HEADS, FOX_V_ROWS, FOX_TQ), F32)],
    )
    return pl.pallas_call(
        functools.partial(_fox_kernel, nq=nq),
        grid_spec=grid_spec,
        out_shape=jax.ShapeDtypeStruct((seq, FOX_WIDTH), BF16),
        compiler_params=pltpu.CompilerParams(dimension_semantics=("arbitrary",), vmem_limit_bytes=VMEM_LIMIT),
        name="fox_prompt",
    )(*tables, qt, kh, vt)


def _lane_cumsum(x):
    n = x.shape[1]
    lane = lax.broadcasted_iota(jnp.int32, x.shape, 1)
    shift = 1
    while shift < n:
        x = x + jnp.where(lane >= shift, pltpu.roll(x, shift, axis=1), 0.0)
        shift *= 2
    return x


def _fox_sample_kernel(q_ref, kn_ref, vn_ref, lft_ref, ck_ref, cv_ref, o_ref,
                       cq_ref, cum_ref, m_ref, l_ref, acc_ref, *, n_past_chunks, tk, t_new):
    c = pl.program_id(1)
    rows = FOX_HEADS * t_new
    heads = [slice(h * FOX_HEAD_DIM, (h + 1) * FOX_HEAD_DIM) for h in range(FOX_HEADS)]
    head_rows = [slice(h * t_new, (h + 1) * t_new) for h in range(FOX_HEADS)]

    @pl.when(c == 0)
    def _():
        cum = _lane_cumsum(lft_ref[0])
        for j in range(n_past_chunks):
            cum_ref[j] = cum[:, j * tk:(j + 1) * tk]
        new_cum = cum[:, n_past_chunks * tk:n_past_chunks * tk + LANES]
        cum_ref[n_past_chunks, :, :LANES] = new_cum
        new_cum_t = jnp.concatenate([new_cum] * (LANES // FOX_HEADS), axis=0).T
        for h in range(FOX_HEADS):
            cq_ref[head_rows[h], :] = jnp.broadcast_to(new_cum_t[:t_new, h:h + 1], (t_new, LANES))
        m_ref[...] = jnp.full_like(m_ref, NEG_INF)
        l_ref[...] = jnp.zeros_like(l_ref)
        acc_ref[...] = jnp.zeros_like(acc_ref)

    nt = (((1,), (1,)), ((), ()))

    def update(qk, pv_of, ck_rows, mask):
        q = q_ref[...]
        s = jnp.concatenate([qk(q[:, heads[h]], h) for h in range(FOX_HEADS)], axis=0)
        s = s + (cq_ref[:, :1] - ck_rows)
        if mask is not None:
            s = jnp.where(mask, s, NEG_INF)
        m_prev = m_ref[...]
        m_new = jnp.maximum(m_prev, jnp.max(s, axis=1, keepdims=True))
        alpha = jnp.exp(m_prev - m_new)
        p = jnp.exp(s - m_new[:, :1])
        l_ref[...] = alpha * l_ref[...] + jnp.sum(p, axis=1, keepdims=True)
        p = p.astype(BF16)
        pv = jnp.concatenate([pv_of(p[head_rows[h]], h) for h in range(FOX_HEADS)], axis=0)
        acc_ref[...] = alpha[:, :FOX_HEAD_DIM] * acc_ref[...] + pv
        m_ref[...] = m_new

    def expand_rows(x, width):
        return jnp.concatenate([jnp.broadcast_to(x[h:h + 1, :], (t_new, width)) for h in range(FOX_HEADS)], axis=0)

    @pl.when(c < n_past_chunks)
    def _():
        update(lambda qh, h: jnp.dot(qh, ck_ref[0, h].astype(BF16), preferred_element_type=F32),
               lambda ph, h: lax.dot_general(ph, cv_ref[0, h].astype(BF16), nt, preferred_element_type=F32),
               expand_rows(cum_ref[c], tk), None)

    @pl.when(c == n_past_chunks)
    def _():
        ck_rows = expand_rows(cum_ref[n_past_chunks, :, :LANES], LANES)[:, :t_new]
        rowt = lax.broadcasted_iota(jnp.int32, (rows, t_new), 0) % t_new
        coli = lax.broadcasted_iota(jnp.int32, (rows, t_new), 1)
        update(lambda qh, h: lax.dot_general(qh, kn_ref[:, heads[h]].astype(BF16), nt, preferred_element_type=F32),
               lambda ph, h: jnp.dot(ph, vn_ref[:, heads[h]].astype(BF16), preferred_element_type=F32),
               ck_rows, coli <= rowt)
        out = acc_ref[...] / l_ref[:, :FOX_HEAD_DIM]
        o_ref[...] = jnp.concatenate([out[head_rows[h]] for h in range(FOX_HEADS)], axis=1).astype(o_ref.dtype)


def _fox_sample(qs, k_new, v_new, lf_all_t, cache_k, cache_v, *, t_new):
    nb, past = cache_k.shape[0], cache_k.shape[3]
    tk = SAMPLE_TK
    npc = past // tk
    rows = FOX_HEADS * t_new
    last = npc - 1
    kern = functools.partial(_fox_sample_kernel, n_past_chunks=npc, tk=tk, t_new=t_new)
    cache_spec = pl.BlockSpec((1, FOX_HEADS, FOX_HEAD_DIM, tk), lambda b, c: (b, 0, 0, jnp.minimum(c, last)))
    return pl.pallas_call(
        kern,
        grid=(nb, npc + 1),
        in_specs=[
            pl.BlockSpec((t_new, FOX_WIDTH), lambda b, c: (b, 0)),
            pl.BlockSpec((t_new, FOX_WIDTH), lambda b, c: (b, 0)),
            pl.BlockSpec((t_new, FOX_WIDTH), lambda b, c: (b, 0)),
            pl.BlockSpec((1, FOX_HEADS, past + LANES), lambda b, c: (b, 0, 0)),
            cache_spec, cache_spec,
        ],
        out_specs=pl.BlockSpec((t_new, FOX_WIDTH), lambda b, c: (b, 0)),
        out_shape=jax.ShapeDtypeStruct((nb * t_new, FOX_WIDTH), BF16),
        scratch_shapes=[
            pltpu.VMEM((rows, LANES), F32),
            pltpu.VMEM((npc + 1, FOX_HEADS, tk), F32),
            pltpu.VMEM((rows, LANES), F32),
            pltpu.VMEM((rows, LANES), F32),
            pltpu.VMEM((rows, FOX_HEAD_DIM), F32),
        ],
        compiler_params=pltpu.CompilerParams(dimension_semantics=("arbitrary", "arbitrary"),
                                             vmem_limit_bytes=VMEM_LIMIT),
        name="fox_sample",
    )(qs, k_new, v_new, lf_all_t, cache_k, cache_v)


def _hgrn_kernel(hq_ref, hk_ref, lfh_ref, hi_ref, hg_ref, s0_ref, norm_ref, tri_ref,
                 o_ref, sout_ref, st_ref, qq_ref, kk_ref, eb_ref, b_ref, oin_ref, od_ref):
    t = pl.program_id(1)
    tc = hq_ref.shape[0]
    blk = min(HG_BLOCK, tc)
    nblk = tc // blk
    heads = [slice(h * HG_DIM, (h + 1) * HG_DIM) for h in range(HG_HEADS)]

    @pl.when(t == 0)
    def _():
        for h in range(HG_HEADS):
            st_ref[h] = s0_ref[0, h].T

    def at_block_row(x, r):
        x3 = x.reshape(nblk, blk, HG_DIM)
        return jnp.broadcast_to(x3[:, r:r + 1, :], x3.shape).reshape(tc, HG_DIM)

    def rel_to_middle(b):
        return b - at_block_row(b, blk // 2 - 1)

    worst = jnp.zeros((), F32)
    for sl in heads:
        b = _sum_by_01_matrix(tri_ref[...], lfh_ref[:, sl])
        eb = jnp.exp(b)
        qq_ref[:, sl] = (hq_ref[:, sl].astype(F32) * eb).astype(BF16)
        kk_ref[:, sl] = (hk_ref[:, sl].astype(F32) * jnp.exp(at_block_row(b, blk - 1) - b)).astype(BF16)
        eb_ref[:, sl] = eb
        b_ref[:, sl] = b
        worst = jnp.maximum(worst, jnp.max(jnp.abs(rel_to_middle(b))))

    def carried_state_and_output():
        for j in range(nblk):
            rows = slice(j * blk, (j + 1) * blk)
            for h, sl in enumerate(heads):
                st = st_ref[h]
                oin_ref[rows, sl] = lax.dot_general(qq_ref[rows, sl], st.astype(BF16), (((1,), (1,)), ((), ())),
                                                    preferred_element_type=F32)
                upd = lax.dot_general(hi_ref[rows, sl], kk_ref[rows, sl], (((0,), (0,)), ((), ())),
                                      preferred_element_type=F32)
                st_ref[h] = st * eb_ref[(j + 1) * blk - 1:(j + 1) * blk, sl] + upd
        for sl in heads:
            o = oin_ref[:, sl] + od_ref[:, sl]
            y = (_rms_scale(o) * norm_ref[...]) * hg_ref[:, sl].astype(F32)
            o_ref[:, sl] = y.astype(o_ref.dtype)

    splittable = worst <= HG_SPLIT_MAX

    @pl.when(splittable)
    def _():
        row = lax.broadcasted_iota(jnp.int32, (tc, tc), 0)
        col = lax.broadcasted_iota(jnp.int32, (tc, tc), 1)
        pair_in_block = (row // blk == col // blk) & (col <= row)
        for sl in heads:
            b_rel = rel_to_middle(b_ref[:, sl])
            qs = (hq_ref[:, sl].astype(F32) * jnp.exp(b_rel)).astype(BF16)
            ks = (hk_ref[:, sl].astype(F32) * jnp.exp(-b_rel)).astype(BF16)
            a = lax.dot_general(qs, ks, (((1,), (1,)), ((), ())), preferred_element_type=F32)
            a = jnp.where(pair_in_block, a, 0.0).astype(BF16)
            od_ref[:, sl] = jnp.dot(a, hi_ref[:, sl], preferred_element_type=F32)
        carried_state_and_output()

    @pl.when(jnp.logical_not(splittable))
    def _():
        row_in_blk = lax.broadcasted_iota(jnp.int32, (tc, HG_DIM), 0) % blk
        for sl in heads:
            q = hq_ref[:, sl].astype(F32)
            k = hk_ref[:, sl].astype(F32)
            v = hi_ref[:, sl].astype(F32)
            b = b_ref[:, sl]

            def lag_step(lag, od):
                k_l = pltpu.roll(k, lag, axis=0)
                b_l = pltpu.roll(b, lag, axis=0)
                v_l = pltpu.roll(v, lag, axis=0)
                w = q * k_l * jnp.exp(jnp.minimum(b - b_l, 0.0))
                w = jnp.where(row_in_blk >= lag, w, 0.0)
                return od + jnp.sum(w, axis=1, keepdims=True) * v_l

            od_ref[:, sl] = lax.fori_loop(1, blk, lag_step, jnp.sum(q * k, axis=1, keepdims=True) * v)
        carried_state_and_output()

    @pl.when(t == pl.num_programs(1) - 1)
    def _():
        for h in range(HG_HEADS):
            sout_ref[0, h] = st_ref[h].T


def _hgrn_cumsum_matrix(n, blk):
    t = np.arange(n)[:, None]
    s = np.arange(n)[None, :]
    return jnp.asarray((((t // blk) == (s // blk)) & (s <= t)).astype(np.float32), BF16)


def _hgrn(hq, hk, lfh, hi, hg, s0, norm, *, nseq):
    rows = hq.shape[0]
    t_len = rows // nseq
    tc = min(HG_TC, t_len)
    nt = t_len // tc
    blk = min(HG_BLOCK, tc)
    tri = _hgrn_cumsum_matrix(tc, blk)
    row = pl.BlockSpec((tc, HG_WIDTH), lambda b, t: (b * nt + t, 0))
    state = pl.BlockSpec((1, HG_HEADS, HG_DIM, HG_DIM), lambda b, t: (b, 0, 0, 0))
    return pl.pallas_call(
        _hgrn_kernel,
        grid=(nseq, nt),
        in_specs=[row, row, row, row, row, state,
                  pl.BlockSpec((1, HG_DIM), lambda b, t: (0, 0)),
                  pl.BlockSpec((tc, tc), lambda b, t: (0, 0))],
        out_specs=[row, state],
        out_shape=[jax.ShapeDtypeStruct((rows, HG_WIDTH), BF16),
                   jax.ShapeDtypeStruct((nseq, HG_HEADS, HG_DIM, HG_DIM), F32)],
        scratch_shapes=[
            pltpu.VMEM((HG_HEADS, HG_DIM, HG_DIM), F32),
            pltpu.VMEM((tc, HG_WIDTH), BF16),
            pltpu.VMEM((tc, HG_WIDTH), BF16),
            pltpu.VMEM((tc, HG_WIDTH), F32),
            pltpu.VMEM((tc, HG_WIDTH), F32),
            pltpu.VMEM((tc, HG_WIDTH), F32),
            pltpu.VMEM((tc, HG_WIDTH), F32),
        ],
        compiler_params=pltpu.CompilerParams(dimension_semantics=("arbitrary", "arbitrary"),
                                             vmem_limit_bytes=VMEM_LIMIT),
        name="hgrn",
    )(hq, hk, lfh, hi, hg, s0, norm, tri)


def _mixffn_kernel(x_ref, of_ref, oh_ref, ga_ref, gb_ref, hist_ref,
                   wbf_ref, wbh_ref, wout_ref, wup_ref, wdn_ref,
                   npost_ref, npre2_ref, npost2_ref, cw_ref, cb_ref,
                   y_ref, conv_ref, tail_ref, *, seg_len):
    tm = x_ref.shape[0]
    n_chunks = D_FF // FFN_CHUNK
    carried = seg_len >= tm
    if carried:
        @pl.when(pl.program_id(0) == 0)
        def _():
            tail_ref[...] = hist_ref[0]

    groups = [slice(r, r + FFN_ROWS) for r in range(0, tm, FFN_ROWS)] if tm > FFN_ROWS else [slice(0, tm)]

    def mix_stage(rows):
        br_f = jnp.dot(of_ref[rows, :], wbf_ref[...], preferred_element_type=F32)
        br_h = jnp.dot(oh_ref[rows, :], wbh_ref[...], preferred_element_type=F32)
        merged = ga_ref[rows, :].astype(F32) * br_f + gb_ref[rows, :].astype(F32) * br_h
        mix = jnp.dot(merged.astype(BF16), wout_ref[...], preferred_element_type=F32)
        x1 = x_ref[rows, :] + _rms_scale(mix) * npost_ref[...]
        return x1, (_rms_scale(x1) * npre2_ref[...]).astype(BF16)

    def ffn_stage(rows, x1, h2):
        n_rows = rows.stop - rows.start
        def up_chunk(j):
            w = jnp.concatenate([wup_ref[:, j * FFN_CHUNK:(j + 1) * FFN_CHUNK],
                                 wup_ref[:, D_FF + j * FFN_CHUNK:D_FF + (j + 1) * FFN_CHUNK]], axis=1)
            return jnp.dot(h2, w, preferred_element_type=F32)

        rowi = lax.broadcasted_iota(jnp.int32, (n_rows, FFN_CHUNK), 0)
        ff = None
        up_next = up_chunk(0)
        for j in range(n_chunks):
            cols = slice(j * FFN_CHUNK, (j + 1) * FFN_CHUNK)
            up = up_next
            if j + 1 < n_chunks:
                up_next = up_chunk(j + 1)
            a = up[:, :FFN_CHUNK]
            g = up[:, FFN_CHUNK:]
            prev1 = pltpu.roll(a, 1, axis=0)
            prev2 = pltpu.roll(a, 2, axis=0)
            if carried:
                t0 = tail_ref[0:1, cols]
                t1 = tail_ref[1:2, cols]
                prev1 = jnp.where(rowi == 0, t1, prev1)
                prev2 = jnp.where(rowi == 0, t0, jnp.where(rowi == 1, t1, prev2))
                tail_ref[:, cols] = a[n_rows - 2:, :]
                conv_ref[0, :, cols] = a[n_rows - 2:, :]
            else:
                for s in range(n_rows // seg_len):
                    h0 = hist_ref[s, 0:1, cols]
                    h1 = hist_ref[s, 1:2, cols]
                    prev1 = jnp.where(rowi == s * seg_len, h1, prev1)
                    prev2 = jnp.where(rowi == s * seg_len, h0, jnp.where(rowi == s * seg_len + 1, h1, prev2))
                    conv_ref[s, :, cols] = a[(s + 1) * seg_len - 2:(s + 1) * seg_len, :]
            c = cb_ref[:, cols] + cw_ref[0:1, cols] * prev2 + cw_ref[1:2, cols] * prev1 + cw_ref[2:3, cols] * a
            act = (jax.nn.gelu(c, approximate=True) * g).astype(BF16)
            part = jnp.dot(act, wdn_ref[cols, :], preferred_element_type=F32)
            ff = part if ff is None else ff + part
        y_ref[rows, :] = x1 + _rms_scale(ff) * npost2_ref[...]

    mixed = [mix_stage(rows) for rows in groups]
    for rows, (x1, h2) in zip(groups, mixed):
        ffn_stage(rows, x1, h2)


def _mixffn(x, o_fox, o_hg, ga, gb, hist, w, *, seg_len):
    rows = x.shape[0]
    tm = min(FFN_TM, rows)
    n = rows // tm
    nseg = hist.shape[0]
    row = lambda width: pl.BlockSpec((tm, width), lambda i: (i, 0))
    weights = [w["bf"], w["bh"], w["out"], w["up"], w["down"]]
    smalls = [w["npost"], w["npre2"], w["npost2"], w["conv_w"], w["conv_b"]]
    hist_spec = pl.BlockSpec(hist.shape, lambda i: (0, 0, 0))
    scratch = [pltpu.VMEM((2, D_FF), F32)]
    return pl.pallas_call(
        functools.partial(_mixffn_kernel, seg_len=seg_len),
        grid=(n,),
        in_specs=[row(D_MODEL), row(FOX_WIDTH), row(HG_WIDTH), row(D_MODEL), row(D_MODEL), hist_spec]
                 + [_const_spec(a.shape) for a in weights] + [_const_spec(a.shape) for a in smalls],
        out_specs=[row(D_MODEL), pl.BlockSpec((nseg, 2, D_FF), lambda i: (0, 0, 0))],
        out_shape=[jax.ShapeDtypeStruct((rows, D_MODEL), F32), jax.ShapeDtypeStruct((nseg, 2, D_FF), F32)],
        scratch_shapes=scratch,
        compiler_params=pltpu.CompilerParams(dimension_semantics=("arbitrary",), vmem_limit_bytes=VMEM_LIMIT),
        name="mixffn",
    )(x, o_fox, o_hg, ga, gb, hist, *weights, *smalls)


def _prep_w_in(w_in, fox_f_bias):
    offs = np.cumsum([0] + IN_SIZES)
    seg = [w_in[:, int(offs[i]):int(offs[i + 1])] for i in range(len(IN_SIZES))]
    pad = jnp.zeros((D_MODEL, LANES - 3 * FOX_HEADS), w_in.dtype)
    f3 = jnp.concatenate([seg[3], seg[3], seg[3], pad], axis=1)
    w_all = jnp.concatenate(seg[:3] + [f3] + seg[4:], axis=1).astype(BF16)
    fb = fox_f_bias.astype(F32)
    fb3 = jnp.concatenate([fb, fb, fb, jnp.zeros((LANES - 3 * FOX_HEADS,), F32)]).reshape(1, LANES)
    return w_all, fb3


def kernel(x_prompt, x_sample, cache_fox_k, cache_fox_v, cache_fox_logf, state_hgrn, state_ffn_conv, norm_mix_pre, norm_mix_post, w_in, fox_f_bias, hgrn_lb_logits, hgrn_norm, w_branch_fox, w_branch_hgrn, w_out, norm_ffn_pre, norm_ffn_post, w_up, ffn_conv_w, ffn_conv_b, w_down):
    depth = w_in.shape[0]
    assert depth == 1 and hgrn_lb_logits.shape[0] == 2
    bp, seq, _ = x_prompt.shape
    assert bp == 1
    nb, t_new, _ = x_sample.shape
    past = cache_fox_k.shape[2]

    w_all, fb3 = _prep_w_in(w_in[0], fox_f_bias[0])
    g_pre = norm_mix_pre[0].reshape(1, D_MODEL)
    lbl = hgrn_lb_logits.astype(F32)
    hnorm = hgrn_norm[0].astype(F32).reshape(1, HG_DIM)
    w = {
        "bf": w_branch_fox[0].astype(BF16), "bh": w_branch_hgrn[0].astype(BF16), "out": w_out[0].astype(BF16),
        "up": w_up[0].astype(BF16), "down": w_down[0].astype(BF16),
        "npost": norm_mix_post[0].reshape(1, D_MODEL), "npre2": norm_ffn_pre[0].reshape(1, D_MODEL),
        "npost2": norm_ffn_post[0].reshape(1, D_MODEL),
        "conv_w": ffn_conv_w[0], "conv_b": ffn_conv_b[0].reshape(1, D_FF),
    }

    xp = x_prompt.reshape(seq, D_MODEL)
    (qt, kh, vt, stats, pk, pv, plf, hq, hk, lfh, hi, hg, ga, gb) = _proj(xp, g_pre, w_all, fb3, lbl, fold=True)
    o_fox = _fox_prompt(qt, kh, vt, stats)
    s0 = jnp.zeros((1, HG_HEADS, HG_DIM, HG_DIM), F32)
    o_hg, p_state = _hgrn(hq, hk, lfh, hi, hg, s0, hnorm, nseq=1)
    hist0 = jnp.zeros((1, 2, D_FF), F32)
    yp, pconv = _mixffn(xp, o_fox, o_hg, ga, gb, hist0, w, seg_len=seq)

    xs = x_sample.reshape(nb * t_new, D_MODEL)
    (qs, sk, sv, slf, hq, hk, lfh, hi, hg, ga, gb) = _proj(xs, g_pre, w_all, fb3, lbl, fold=False)
    lf_all_t = jnp.concatenate([
        jnp.swapaxes(cache_fox_logf[0].astype(F32), 1, 2),
        jnp.swapaxes(slf.reshape(nb, t_new, FOX_HEADS), 1, 2),
        jnp.zeros((nb, FOX_HEADS, LANES - t_new), F32)], axis=2)
    cache_kt = jnp.transpose(cache_fox_k[0], (0, 2, 3, 1))
    cache_vt = jnp.transpose(cache_fox_v[0], (0, 2, 3, 1))
    o_fox_s = _fox_sample(qs, sk, sv, lf_all_t, cache_kt, cache_vt, t_new=t_new)
    o_hg_s, s_state = _hgrn(hq, hk, lfh, hi, hg, state_hgrn[0].astype(F32), hnorm, nseq=nb)
    ys, sconv = _mixffn(xs, o_fox_s, o_hg_s, ga, gb, state_ffn_conv[0], w, seg_len=t_new)

    return (
        yp.reshape(bp, seq, D_MODEL),
        ys.reshape(nb, t_new, D_MODEL),
        pk.reshape(1, bp, seq, FOX_HEADS, FOX_HEAD_DIM),
        pv.reshape(1, bp, seq, FOX_HEADS, FOX_HEAD_DIM),
        plf.reshape(1, bp, seq, FOX_HEADS),
        p_state.reshape(1, bp, HG_HEADS, HG_DIM, HG_DIM),
        pconv.reshape(1, bp, 2, D_FF),
        sk.reshape(1, nb, t_new, FOX_HEADS, FOX_HEAD_DIM),
        sv.reshape(1, nb, t_new, FOX_HEADS, FOX_HEAD_DIM),
        slf.reshape(1, nb, t_new, FOX_HEADS),
        s_state.reshape(1, nb, HG_HEADS, HG_DIM, HG_DIM),
        sconv.reshape(1, nb, 2, D_FF),
    )
```

```python
import functools

import numpy as np
import jax
import jax.numpy as jnp
from jax import lax
from jax.experimental import pallas as pl
from jax.experimental.pallas import tpu as pltpu

F32 = jnp.float32
BF16 = jnp.bfloat16

D_MODEL = 1024
FOX_HEADS = 8
FOX_HEAD_DIM = 64
FOX_WIDTH = FOX_HEADS * FOX_HEAD_DIM
HG_HEADS = 4
HG_DIM = 128
HG_WIDTH = HG_HEADS * HG_DIM
D_FF = 2816
RMS_EPS = 1e-6
NEG_INF = -1e30
LOG2E = 1.4426950408889634
FOX_SKIP_NATS = 110.0
IN_SIZES = [FOX_WIDTH, FOX_WIDTH, FOX_WIDTH, FOX_HEADS, HG_WIDTH, HG_WIDTH, HG_WIDTH, HG_WIDTH, D_MODEL, D_MODEL]

LANES = 128
FOX_PAD = 2 * FOX_HEAD_DIM
FOX_V_ROWS = FOX_HEAD_DIM + 16
HG_BLOCK = 64
HG_SPLIT_MAX = 60.0
VMEM_LIMIT = 56 * 1024 * 1024

PROJ_TM = 256
FOX_TQ = 512
FOX_TK = 512
FOX_SLOTS = 3
HG_TC = 256
FFN_TM = 512
FFN_ROWS = 256
FFN_CHUNK = 256
SAMPLE_TK = 2048

_C_Q, _C_K, _C_V, _C_F = 0, 512, 1024, 1536
_C_HQ, _C_HF, _C_HI, _C_HG = 1664, 2176, 2688, 3200
_C_GA, _C_GB, _C_END = 3712, 4736, 5760


def _split3(x):
    hi = x.astype(BF16)
    r = x - hi.astype(F32)
    mid = r.astype(BF16)
    lo = (r - mid.astype(F32)).astype(BF16)
    return hi, mid, lo


def _sum_by_01_matrix(mat01, x):
    cat = jnp.concatenate(_split3(x), axis=1)
    y = jnp.dot(mat01, cat, preferred_element_type=F32)
    return y[:, :LANES] + y[:, LANES:2 * LANES] + y[:, 2 * LANES:]


def _rms_scale(x):
    return x * lax.rsqrt(jnp.mean(x * x, axis=-1, keepdims=True) + RMS_EPS)


def _log_sigmoid(x):
    return jnp.minimum(x, 0.0) - jnp.log1p(jnp.exp(-jnp.abs(x)))


def _sigmoid(x):
    return 1.0 / (1.0 + jnp.exp(-x))


def _proj_kernel(*refs, fold):
    if fold:
        (x_ref, g_ref, w_ref, fb_ref, lbl_ref, tri_ref, pq_ref, pk_ref, cq_ref, ck_ref, cv_ref, seg_ref,
         qh_ref, kh_ref, vh_ref, stat_ref, kout_ref, vout_ref, lf_ref, hq_ref, hk_ref, lfh_ref, hi_ref, hg_ref,
         ga_ref, gb_ref, carry_ref) = refs
    else:
        (x_ref, g_ref, w_ref, fb_ref, lbl_ref,
         qs_ref, kout_ref, vout_ref, lf_ref, hq_ref, hk_ref, lfh_ref, hi_ref, hg_ref,
         ga_ref, gb_ref) = refs

    h = (_rms_scale(x_ref[...]) * g_ref[...]).astype(BF16)

    z = jnp.dot(h, w_ref[...], preferred_element_type=F32)

    zq = z[:, _C_Q:_C_K] * (FOX_HEAD_DIM ** -0.5)
    zk = z[:, _C_K:_C_V]
    zv = z[:, _C_V:_C_F]
    kout_ref[...] = zk
    vout_ref[...] = zv
    logf = _log_sigmoid(z[:, _C_F:_C_HQ] + fb_ref[...])
    lf_ref[...] = logf[:, :FOX_HEADS]

    l0 = lbl_ref[0:1, :]
    l1 = lbl_ref[1:2, :]
    lmax = jnp.maximum(l0, l1)
    e0 = jnp.exp(l0 - lmax)
    lb = e0 / (e0 + jnp.exp(l1 - lmax))
    f = lb + (1.0 - lb) * _sigmoid(z[:, _C_HF:_C_HI])
    hq_ref[...] = z[:, _C_HQ:_C_HF].astype(BF16)
    hk_ref[...] = (1.0 - f).astype(BF16)
    lfh_ref[...] = jnp.log(f)
    hi_ref[...] = z[:, _C_HI:_C_HG].astype(BF16)
    hg_ref[...] = _sigmoid(z[:, _C_HG:_C_GA]).astype(BF16)
    ga_ref[...] = _sigmoid(z[:, _C_GA:_C_GB]).astype(BF16)
    gb_ref[...] = _sigmoid(z[:, _C_GB:_C_END]).astype(BF16)

    if not fold:
        qs_ref[...] = zq.astype(BF16)
        return

    @pl.when(pl.program_id(0) == 0)
    def _():
        carry_ref[...] = jnp.zeros_like(carry_ref)

    cum = carry_ref[...] + _sum_by_01_matrix(tri_ref[...], logf)
    carry_ref[...] = cum[-1:, :]

    seg = seg_ref[...]
    qn2 = jnp.dot((zq * zq).astype(BF16), seg, preferred_element_type=F32)
    kn2 = jnp.dot((zk * zk).astype(BF16), seg, preferred_element_type=F32)
    dg = jnp.dot((zq * zk).astype(BF16), seg, preferred_element_type=F32)
    stat_ref[0, 0:1, :] = jnp.max(qn2, axis=0, keepdims=True)
    stat_ref[0, 1:2, :] = jnp.max(kn2, axis=0, keepdims=True)
    stat_ref[0, 2:3, :] = jnp.min(dg, axis=0, keepdims=True)
    stat_ref[0, 3:4, :] = cum[0:1, :]
    stat_ref[0, 4:5, :] = cum[-1:, :]
    stat_ref[0, 5:8, :] = jnp.zeros((3, LANES), F32)

    zq = zq * LOG2E
    c_hi, c_mid, c_lo = _split3(cum * LOG2E)
    lane = lax.broadcasted_iota(jnp.int32, cum.shape, 1)
    pieces = jnp.where(lane < 8, c_hi, jnp.where(lane < 16, c_mid, c_lo))
    pieces = jnp.where(lane < 24, pieces, jnp.zeros_like(pieces))
    ex_q = jnp.dot(pieces, pq_ref[...], preferred_element_type=F32) + cq_ref[...]
    ex_k = jnp.dot(pieces, pk_ref[...], preferred_element_type=F32) + ck_ref[...]
    ex_v = cv_ref[...]

    low = lax.broadcasted_iota(jnp.int32, (zq.shape[0], LANES), 1) < FOX_HEAD_DIM
    for src, ex, dst, transposed in ((zq, ex_q, qh_ref, True), (zk, ex_k, kh_ref, False), (zv, ex_v, vh_ref, True)):
        for c in range(FOX_WIDTH // LANES):
            pair = src[:, c * LANES:(c + 1) * LANES]
            swapped = pltpu.roll(pair, FOX_HEAD_DIM, axis=1)
            for j, data in enumerate((pair, swapped)):
                hd = 2 * c + j
                blk = jnp.where(low, data, ex[:, hd * LANES:(hd + 1) * LANES])
                if not transposed:
                    dst[hd] = blk.astype(BF16)
                elif len(dst.shape) == 3:
                    dst[hd] = blk.T.astype(BF16)
                else:
                    dst[hd, 0] = blk.T[:dst.shape[2]].astype(BF16)


def _bias_fold_constants():
    pq = np.zeros((LANES, FOX_HEADS * LANES), np.float32)
    pk = np.zeros((LANES, FOX_HEADS * LANES), np.float32)
    cq = np.zeros((1, FOX_HEADS * LANES), np.float32)
    ck = np.zeros((1, FOX_HEADS * LANES), np.float32)
    cv = np.zeros((1, FOX_HEADS * LANES), np.float32)
    for h in range(FOX_HEADS):
        base = h * LANES + FOX_HEAD_DIM
        for p in range(3):
            pq[p * 8 + h, base + p] = 1.0
            ck[0, base + p] = 1.0
            pk[p * 8 + h, base + 3 + p] = -1.0
            cq[0, base + 3 + p] = 1.0
        cv[0, base] = 1.0
    return (jnp.asarray(pq, BF16), jnp.asarray(pk, BF16), jnp.asarray(cq), jnp.asarray(ck), jnp.asarray(cv))


def _const_spec(shape, single=True):
    nd = len(shape)
    if single:
        return pl.BlockSpec(shape, lambda *_: (0,) * nd, pipeline_mode=pl.Buffered(1))
    return pl.BlockSpec(shape, lambda *_: (0,) * nd)


def _proj(x, gain, w_all, fb3, lb_logits, *, fold):
    rows = x.shape[0]
    tm = min(PROJ_TM, rows)
    n = rows // tm
    row = lambda width: pl.BlockSpec((tm, width), lambda i: (i, 0))
    in_specs = [row(D_MODEL), _const_spec((1, D_MODEL)), _const_spec(w_all.shape), _const_spec((1, LANES)),
                _const_spec(lb_logits.shape)]
    args = [x, gain, w_all, fb3, lb_logits]
    common_out = [
        (jax.ShapeDtypeStruct((rows, FOX_WIDTH), F32), row(FOX_WIDTH)),
        (jax.ShapeDtypeStruct((rows, FOX_WIDTH), F32), row(FOX_WIDTH)),
        (jax.ShapeDtypeStruct((rows, FOX_HEADS), F32), row(FOX_HEADS)),
        (jax.ShapeDtypeStruct((rows, HG_WIDTH), BF16), row(HG_WIDTH)),
        (jax.ShapeDtypeStruct((rows, HG_WIDTH), BF16), row(HG_WIDTH)),
        (jax.ShapeDtypeStruct((rows, HG_WIDTH), F32), row(HG_WIDTH)),
        (jax.ShapeDtypeStruct((rows, HG_WIDTH), BF16), row(HG_WIDTH)),
        (jax.ShapeDtypeStruct((rows, HG_WIDTH), BF16), row(HG_WIDTH)),
        (jax.ShapeDtypeStruct((rows, D_MODEL), BF16), row(D_MODEL)),
        (jax.ShapeDtypeStruct((rows, D_MODEL), BF16), row(D_MODEL)),
    ]
    scratch = []
    if fold:
        tri = jnp.asarray(np.tril(np.ones((tm, tm), np.float32)), BF16)
        seg = np.zeros((FOX_WIDTH, LANES), np.float32)
        seg[np.arange(FOX_WIDTH), np.arange(FOX_WIDTH) // FOX_HEAD_DIM] = 1.0
        consts = _bias_fold_constants() + (jnp.asarray(seg, BF16),)
        in_specs += [_const_spec(tri.shape)] + [_const_spec(c.shape) for c in consts]
        args += [tri, *consts]
        head_major = (jax.ShapeDtypeStruct((FOX_HEADS, rows, FOX_PAD), BF16),
                      pl.BlockSpec((FOX_HEADS, tm, FOX_PAD), lambda i: (0, i, 0)))
        head_major_t = lambda depth: (jax.ShapeDtypeStruct((FOX_HEADS, depth, rows), BF16),
                                      pl.BlockSpec((FOX_HEADS, depth, tm), lambda i: (0, 0, i)))
        stats = (jax.ShapeDtypeStruct((n, 8, LANES), F32), pl.BlockSpec((1, 8, LANES), lambda i: (i, 0, 0)))
        v_tiles = (jax.ShapeDtypeStruct((FOX_HEADS, n, FOX_V_ROWS, tm), BF16),
                   pl.BlockSpec((FOX_HEADS, 1, FOX_V_ROWS, tm), lambda i: (0, i, 0, 0)))
        outs = [head_major_t(FOX_PAD), head_major, v_tiles, stats] + common_out
        scratch = [pltpu.VMEM((1, LANES), F32)]
    else:
        outs = [(jax.ShapeDtypeStruct((rows, FOX_WIDTH), BF16), row(FOX_WIDTH))] + common_out
    return pl.pallas_call(
        functools.partial(_proj_kernel, fold=fold),
        grid=(n,),
        in_specs=in_specs,
        out_specs=[o[1] for o in outs],
        out_shape=[o[0] for o in outs],
        scratch_shapes=scratch,
        compiler_params=pltpu.CompilerParams(dimension_semantics=("arbitrary",), vmem_limit_bytes=VMEM_LIMIT),
        name="proj_fold" if fold else "proj",
    )(*args)


def _fox_kernel(ks_ref, nact_ref, order_ref, qn_ref, kn_ref, cl_ref, qt_ref, k_hbm, vt_hbm, o_ref,
                kbuf, vbuf, sem, slot0_ref, m_ref, acc_ref, *, nq):
    q_blk = pl.program_id(0)
    k_first = ks_ref[q_blk]
    tq = qt_ref.shape[2]
    n_slots, tk = kbuf.shape[0], kbuf.shape[2]
    tiles, tile = vbuf.shape[2], vbuf.shape[4]
    table = lambda ref, blk, h: lax.bitcast_convert_type(ref[blk * FOX_HEADS + h], F32)

    def successor(q, k):
        at_diagonal = k == q
        return (jnp.where(at_diagonal, q + 1, q),
                jnp.where(at_diagonal, ks_ref[jnp.minimum(q + 1, nq - 1)], k + 1))

    def for_swept_heads(q, k, slot, action):
        n_swept = nact_ref[q * nq + k]
        for j in range(FOX_HEADS):
            @pl.when(j < n_swept)
            def _():
                h = order_ref[q * FOX_HEADS + j]
                rows = pl.ds(pl.multiple_of(k * tk, tk), tk)
                action(pltpu.make_async_copy(k_hbm.at[h, rows, :], kbuf.at[slot, h], sem.at[0, slot]))
                action(pltpu.make_async_copy(vt_hbm.at[h, pl.ds(k * tiles, tiles)], vbuf.at[slot, h],
                                             sem.at[1, slot]))

    def start(q, k, slot):
        @pl.when(q < nq)
        def _():
            for_swept_heads(q, k, slot, lambda c: c.start())

    def wait(q, k, slot):
        for_swept_heads(q, k, slot, lambda c: c.wait())

    def request_second_next(k_blk, slot):
        q1, k1 = successor(q_blk, k_blk)
        q2, k2 = successor(jnp.minimum(q1, nq - 1), k1)
        start(jnp.where(q1 < nq, q2, nq), k2, lax.rem(slot + 2, n_slots))

    @pl.when(q_blk == 0)
    def _():
        slot0_ref[0] = 0
        start(q_blk, k_first, 0)
        q1, k1 = successor(q_blk, k_first)
        start(q1, k1, 1)

    slot0 = slot0_ref[0]
    slot_of = lambda k_blk: lax.rem(slot0 + (k_blk - k_first), n_slots)
    m_ref[...] = jnp.full_like(m_ref, NEG_INF)
    acc_ref[...] = jnp.zeros_like(acc_ref)

    def logits(h, slot):
        return jnp.dot(kbuf[slot, h], qt_ref[h], preferred_element_type=F32)

    def softmax_update(h, s, k_blk, slot, masked):
        if masked:
            key = lax.broadcasted_iota(jnp.int32, (tk, tq), 0)
            qry = lax.broadcasted_iota(jnp.int32, (tk, tq), 1)
            s = jnp.where(key <= qry, s, NEG_INF)
        bound = table(qn_ref, q_blk, h) * table(kn_ref, k_blk, h)
        if not masked:
            cum_q = jnp.sum(qt_ref[h, FOX_HEAD_DIM:FOX_HEAD_DIM + 3, :].astype(F32), axis=0, keepdims=True)
            bound = cum_q + (bound - table(cl_ref, k_blk, h))
        m_prev = m_ref[h]
        m_new = jnp.maximum(m_prev, bound)
        alpha = jnp.exp2(m_prev - m_new)
        p = jnp.exp2(s - m_new).astype(BF16)
        pv = sum(jnp.dot(vbuf[slot, h, j], p[j * tile:(j + 1) * tile], preferred_element_type=F32)
                 for j in range(tiles))
        acc_ref[h] = alpha * acc_ref[h] + pv
        m_ref[h] = m_new

    def sweep(heads, k_blk, slot, masked):
        s_next = logits(heads[0], slot)
        for j, h in enumerate(heads):
            s = s_next
            if j + 1 < len(heads):
                s_next = logits(heads[j + 1], slot)
            softmax_update(h, s, k_blk, slot, masked)

    def past_block(k_blk, carry):
        slot = slot_of(k_blk)
        wait(q_blk, k_blk, slot)
        request_second_next(k_blk, slot)
        n_active = nact_ref[q_blk * nq + k_blk]
        for count in range(2, FOX_HEADS + 1, 2):
            @pl.when(n_active == count)
            def _():
                sweep([order_ref[q_blk * FOX_HEADS + j] for j in range(count)], k_blk, slot, False)
        return carry

    lax.fori_loop(k_first, q_blk, past_block, 0)

    slot = slot_of(q_blk)
    wait(q_blk, q_blk, slot)
    request_second_next(q_blk, slot)
    slot0_ref[0] = lax.rem(slot + 1, n_slots)

    sweep(list(range(FOX_HEADS)), q_blk, slot, True)
    for c in range(FOX_WIDTH // LANES):
        halves = []
        for hd in (2 * c, 2 * c + 1):
            acc = acc_ref[hd]
            halves.append(acc[:FOX_HEAD_DIM, :] / acc[FOX_HEAD_DIM:FOX_HEAD_DIM + 1, :])
        o_ref[:, c * LANES:(c + 1) * LANES] = jnp.concatenate(halves, axis=0).T.astype(o_ref.dtype)


def _fox_schedule(stats, nq):
    st = stats[:, :5, :FOX_HEADS].reshape(nq, -1, 5, FOX_HEADS)
    qn = jnp.sqrt(jnp.max(st[:, :, 0], axis=1)) * 1.01
    kn = jnp.sqrt(jnp.max(st[:, :, 1], axis=1)) * 1.01
    dmin = jnp.min(st[:, :, 2], axis=1)
    c_first = st[:, 0, 3]
    c_last = st[:, -1, 4]
    bound = qn[:, None] * kn[None, :] + (c_first - dmin)[:, None] - c_last[None, :]
    blk = jnp.arange(nq, dtype=jnp.int32)
    drop = (bound < -FOX_SKIP_NATS) & (blk[None, :] < blk[:, None])[:, :, None]
    prefix = jnp.cumsum(jnp.logical_not(drop).astype(jnp.int32), axis=1) == 0
    kstart_h = jnp.sum(prefix.astype(jnp.int32), axis=1)
    kstart = jnp.min(kstart_h, axis=1)
    hd = jnp.arange(FOX_HEADS, dtype=jnp.int32)
    before = (kstart_h[:, None, :] < kstart_h[:, :, None]) | (
        (kstart_h[:, None, :] == kstart_h[:, :, None]) & (hd[None, None, :] < hd[None, :, None]))
    rank = jnp.sum(before.astype(jnp.int32), axis=2)
    order = jnp.sum(jnp.where(rank[:, None, :] == hd[None, :, None], hd[None, None, :], 0), axis=2)
    n_active = jnp.sum((kstart_h[:, None, :] <= blk[None, :, None]).astype(jnp.int32), axis=2)
    n_active = jnp.minimum((n_active + 1) // 2 * 2, FOX_HEADS)
    as_i32 = lambda a: a.astype(jnp.int32).reshape(-1)
    bits = lambda a: lax.bitcast_convert_type(a.astype(F32), jnp.int32).reshape(-1)
    return (as_i32(kstart), as_i32(n_active), as_i32(order), bits(qn * LOG2E), bits(kn), bits(c_last * LOG2E))


def _fox_prompt(qt, kh, vt, stats):
    seq = kh.shape[1]
    assert FOX_TQ == FOX_TK
    nq = seq // FOX_TQ
    tile = vt.shape[3]
    tables = _fox_schedule(stats, nq)
    grid_spec = pltpu.PrefetchScalarGridSpec(
        num_scalar_prefetch=len(tables),
        grid=(nq,),
        in_specs=[
            pl.BlockSpec((FOX_HEADS, FOX_PAD, FOX_TQ), lambda q, *_: (0, 0, q)),
            pl.BlockSpec(memory_space=pl.ANY),
            pl.BlockSpec(memory_space=pl.ANY),
        ],
        out_specs=pl.BlockSpec((FOX_TQ, FOX_WIDTH), lambda q, *_: (q, 0)),
        scratch_shapes=[pltpu.VMEM((FOX_SLOTS, FOX_HEADS, FOX_TK, FOX_PAD), BF16),
                        pltpu.VMEM((FOX_SLOTS, FOX_HEADS, FOX_TK // tile, FOX_V_ROWS, tile), BF16),
                        pltpu.SemaphoreType.DMA((2, FOX_SLOTS)),
                        pltpu.SMEM((1,), jnp.int32),
                        pltpu.VMEM((FOX_HEADS, 1, FOX_TQ), F32),
                        pltpu.VMEM((FOX_HEADS, FOX_V_ROWS, FOX_TQ), F32)],
    )
    return pl.pallas_call(
        functools.partial(_fox_kernel, nq=nq),
        grid_spec=grid_spec,
        out_shape=jax.ShapeDtypeStruct((seq, FOX_WIDTH), BF16),
        compiler_params=pltpu.CompilerParams(dimension_semantics=("arbitrary",), vmem_limit_bytes=VMEM_LIMIT),
        name="fox_prompt",
    )(*tables, qt, kh, vt)


def _lane_cumsum(x):
    n = x.shape[1]
    lane = lax.broadcasted_iota(jnp.int32, x.shape, 1)
    shift = 1
    while shift < n:
        x = x + jnp.where(lane >= shift, pltpu.roll(x, shift, axis=1), 0.0)
        shift *= 2
    return x


def _fox_sample_kernel(q_ref, kn_ref, vn_ref, lft_ref, ck_ref, cv_ref, o_ref,
                       cq_ref, cum_ref, m_ref, l_ref, acc_ref, *, n_past_chunks, tk, t_new):
    c = pl.program_id(1)
    rows = FOX_HEADS * t_new
    heads = [slice(h * FOX_HEAD_DIM, (h + 1) * FOX_HEAD_DIM) for h in range(FOX_HEADS)]
    head_rows = [slice(h * t_new, (h + 1) * t_new) for h in range(FOX_HEADS)]

    @pl.when(c == 0)
    def _():
        cum = _lane_cumsum(lft_ref[0])
        for j in range(n_past_chunks):
            cum_ref[j] = cum[:, j * tk:(j + 1) * tk]
        new_cum = cum[:, n_past_chunks * tk:n_past_chunks * tk + LANES]
        cum_ref[n_past_chunks, :, :LANES] = new_cum
        new_cum_t = jnp.concatenate([new_cum] * (LANES // FOX_HEADS), axis=0).T
        for h in range(FOX_HEADS):
            cq_ref[head_rows[h], :] = jnp.broadcast_to(new_cum_t[:t_new, h:h + 1], (t_new, LANES))
        m_ref[...] = jnp.full_like(m_ref, NEG_INF)
        l_ref[...] = jnp.zeros_like(l_ref)
        acc_ref[...] = jnp.zeros_like(acc_ref)

    nt = (((1,), (1,)), ((), ()))

    def update(qk, pv_of, ck_rows, mask):
        q = q_ref[...]
        s = jnp.concatenate([qk(q[:, heads[h]], h) for h in range(FOX_HEADS)], axis=0)
        s = s + (cq_ref[:, :1] - ck_rows)
        if mask is not None:
            s = jnp.where(mask, s, NEG_INF)
        m_prev = m_ref[...]
        m_new = jnp.maximum(m_prev, jnp.max(s, axis=1, keepdims=True))
        alpha = jnp.exp(m_prev - m_new)
        p = jnp.exp(s - m_new[:, :1])
        l_ref[...] = alpha * l_ref[...] + jnp.sum(p, axis=1, keepdims=True)
        p = p.astype(BF16)
        pv = jnp.concatenate([pv_of(p[head_rows[h]], h) for h in range(FOX_HEADS)], axis=0)
        acc_ref[...] = alpha[:, :FOX_HEAD_DIM] * acc_ref[...] + pv
        m_ref[...] = m_new

    def expand_rows(x, width):
        return jnp.concatenate([jnp.broadcast_to(x[h:h + 1, :], (t_new, width)) for h in range(FOX_HEADS)], axis=0)

    @pl.when(c < n_past_chunks)
    def _():
        update(lambda qh, h: jnp.dot(qh, ck_ref[0, h].astype(BF16), preferred_element_type=F32),
               lambda ph, h: lax.dot_general(ph, cv_ref[0, h].astype(BF16), nt, preferred_element_type=F32),
               expand_rows(cum_ref[c], tk), None)

    @pl.when(c == n_past_chunks)
    def _():
        ck_rows = expand_rows(cum_ref[n_past_chunks, :, :LANES], LANES)[:, :t_new]
        rowt = lax.broadcasted_iota(jnp.int32, (rows, t_new), 0) % t_new
        coli = lax.broadcasted_iota(jnp.int32, (rows, t_new), 1)
        update(lambda qh, h: lax.dot_general(qh, kn_ref[:, heads[h]].astype(BF16), nt, preferred_element_type=F32),
               lambda ph, h: jnp.dot(ph, vn_ref[:, heads[h]].astype(BF16), preferred_element_type=F32),
               ck_rows, coli <= rowt)
        out = acc_ref[...] / l_ref[:, :FOX_HEAD_DIM]
        o_ref[...] = jnp.concatenate([out[head_rows[h]] for h in range(FOX_HEADS)], axis=1).astype(o_ref.dtype)


def _fox_sample(qs, k_new, v_new, lf_all_t, cache_k, cache_v, *, t_new):
    nb, past = cache_k.shape[0], cache_k.shape[3]
    tk = SAMPLE_TK
    npc = past // tk
    rows = FOX_HEADS * t_new
    last = npc - 1
    kern = functools.partial(_fox_sample_kernel, n_past_chunks=npc, tk=tk, t_new=t_new)
    cache_spec = pl.BlockSpec((1, FOX_HEADS, FOX_HEAD_DIM, tk), lambda b, c: (b, 0, 0, jnp.minimum(c, last)))
    return pl.pallas_call(
        kern,
        grid=(nb, npc + 1),
        in_specs=[
            pl.BlockSpec((t_new, FOX_WIDTH), lambda b, c: (b, 0)),
            pl.BlockSpec((t_new, FOX_WIDTH), lambda b, c: (b, 0)),
            pl.BlockSpec((t_new, FOX_WIDTH), lambda b, c: (b, 0)),
            pl.BlockSpec((1, FOX_HEADS, past + LANES), lambda b, c: (b, 0, 0)),
            cache_spec, cache_spec,
        ],
        out_specs=pl.BlockSpec((t_new, FOX_WIDTH), lambda b, c: (b, 0)),
        out_shape=jax.ShapeDtypeStruct((nb * t_new, FOX_WIDTH), BF16),
        scratch_shapes=[
            pltpu.VMEM((rows, LANES), F32),
            pltpu.VMEM((npc + 1, FOX_HEADS, tk), F32),
            pltpu.VMEM((rows, LANES), F32),
            pltpu.VMEM((rows, LANES), F32),
            pltpu.VMEM((rows, FOX_HEAD_DIM), F32),
        ],
        compiler_params=pltpu.CompilerParams(dimension_semantics=("arbitrary", "arbitrary"),
                                             vmem_limit_bytes=VMEM_LIMIT),
        name="fox_sample",
    )(qs, k_new, v_new, lf_all_t, cache_k, cache_v)


def _hgrn_kernel(hq_ref, hk_ref, lfh_ref, hi_ref, hg_ref, s0_ref, norm_ref, tri_ref,
                 o_ref, sout_ref, st_ref, qq_ref, kk_ref, eb_ref, b_ref, oin_ref, od_ref):
    t = pl.program_id(1)
    tc = hq_ref.shape[0]
    blk = min(HG_BLOCK, tc)
    nblk = tc // blk
    heads = [slice(h * HG_DIM, (h + 1) * HG_DIM) for h in range(HG_HEADS)]

    @pl.when(t == 0)
    def _():
        for h in range(HG_HEADS):
            st_ref[h] = s0_ref[0, h].T

    def at_block_row(x, r):
        x3 = x.reshape(nblk, blk, HG_DIM)
        return jnp.broadcast_to(x3[:, r:r + 1, :], x3.shape).reshape(tc, HG_DIM)

    def rel_to_middle(b):
        return b - at_block_row(b, blk // 2 - 1)

    worst = jnp.zeros((), F32)
    for sl in heads:
        b = _sum_by_01_matrix(tri_ref[...], lfh_ref[:, sl])
        eb = jnp.exp(b)
        qq_ref[:, sl] = (hq_ref[:, sl].astype(F32) * eb).astype(BF16)
        kk_ref[:, sl] = (hk_ref[:, sl].astype(F32) * jnp.exp(at_block_row(b, blk - 1) - b)).astype(BF16)
        eb_ref[:, sl] = eb
        b_ref[:, sl] = b
        worst = jnp.maximum(worst, jnp.max(jnp.abs(rel_to_middle(b))))

    def carried_state_and_output():
        for j in range(nblk):
            rows = slice(j * blk, (j + 1) * blk)
            for h, sl in enumerate(heads):
                st = st_ref[h]
                oin_ref[rows, sl] = lax.dot_general(qq_ref[rows, sl], st.astype(BF16), (((1,), (1,)), ((), ())),
                                                    preferred_element_type=F32)
                upd = lax.dot_general(hi_ref[rows, sl], kk_ref[rows, sl], (((0,), (0,)), ((), ())),
                                      preferred_element_type=F32)
                st_ref[h] = st * eb_ref[(j + 1) * blk - 1:(j + 1) * blk, sl] + upd
        for sl in heads:
            o = oin_ref[:, sl] + od_ref[:, sl]
            y = (_rms_scale(o) * norm_ref[...]) * hg_ref[:, sl].astype(F32)
            o_ref[:, sl] = y.astype(o_ref.dtype)

    splittable = worst <= HG_SPLIT_MAX

    @pl.when(splittable)
    def _():
        row = lax.broadcasted_iota(jnp.int32, (tc, tc), 0)
        col = lax.broadcasted_iota(jnp.int32, (tc, tc), 1)
        pair_in_block = (row // blk == col // blk) & (col <= row)
        for sl in heads:
            b_rel = rel_to_middle(b_ref[:, sl])
            qs = (hq_ref[:, sl].astype(F32) * jnp.exp(b_rel)).astype(BF16)
            ks = (hk_ref[:, sl].astype(F32) * jnp.exp(-b_rel)).astype(BF16)
            a = lax.dot_general(qs, ks, (((1,), (1,)), ((), ())), preferred_element_type=F32)
            a = jnp.where(pair_in_block, a, 0.0).astype(BF16)
            od_ref[:, sl] = jnp.dot(a, hi_ref[:, sl], preferred_element_type=F32)
        carried_state_and_output()

    @pl.when(jnp.logical_not(splittable))
    def _():
        row_in_blk = lax.broadcasted_iota(jnp.int32, (tc, HG_DIM), 0) % blk
        for sl in heads:
            q = hq_ref[:, sl].astype(F32)
            k = hk_ref[:, sl].astype(F32)
            v = hi_ref[:, sl].astype(F32)
            b = b_ref[:, sl]

            def lag_step(lag, od):
                k_l = pltpu.roll(k, lag, axis=0)
                b_l = pltpu.roll(b, lag, axis=0)
                v_l = pltpu.roll(v, lag, axis=0)
                w = q * k_l * jnp.exp(jnp.minimum(b - b_l, 0.0))
                w = jnp.where(row_in_blk >= lag, w, 0.0)
                return od + jnp.sum(w, axis=1, keepdims=True) * v_l

            od_ref[:, sl] = lax.fori_loop(1, blk, lag_step, jnp.sum(q * k, axis=1, keepdims=True) * v)
        carried_state_and_output()

    @pl.when(t == pl.num_programs(1) - 1)
    def _():
        for h in range(HG_HEADS):
            sout_ref[0, h] = st_ref[h].T


def _hgrn_cumsum_matrix(n, blk):
    t = np.arange(n)[:, None]
    s = np.arange(n)[None, :]
    return jnp.asarray((((t // blk) == (s // blk)) & (s <= t)).astype(np.float32), BF16)


def _hgrn(hq, hk, lfh, hi, hg, s0, norm, *, nseq):
    rows = hq.shape[0]
    t_len = rows // nseq
    tc = min(HG_TC, t_len)
    nt = t_len // tc
    blk = min(HG_BLOCK, tc)
    tri = _hgrn_cumsum_matrix(tc, blk)
    row = pl.BlockSpec((tc, HG_WIDTH), lambda b, t: (b * nt + t, 0))
    state = pl.BlockSpec((1, HG_HEADS, HG_DIM, HG_DIM), lambda b, t: (b, 0, 0, 0))
    return pl.pallas_call(
        _hgrn_kernel,
        grid=(nseq, nt),
        in_specs=[row, row, row, row, row, state,
                  pl.BlockSpec((1, HG_DIM), lambda b, t: (0, 0)),
                  pl.BlockSpec((tc, tc), lambda b, t: (0, 0))],
        out_specs=[row, state],
        out_shape=[jax.ShapeDtypeStruct((rows, HG_WIDTH), BF16),
                   jax.ShapeDtypeStruct((nseq, HG_HEADS, HG_DIM, HG_DIM), F32)],
        scratch_shapes=[
            pltpu.VMEM((HG_HEADS, HG_DIM, HG_DIM), F32),
            pltpu.VMEM((tc, HG_WIDTH), BF16),
            pltpu.VMEM((tc, HG_WIDTH), BF16),
            pltpu.VMEM((tc, HG_WIDTH), F32),
            pltpu.VMEM((tc, HG_WIDTH), F32),
            pltpu.VMEM((tc, HG_WIDTH), F32),
            pltpu.VMEM((tc, HG_WIDTH), F32),
        ],
        compiler_params=pltpu.CompilerParams(dimension_semantics=("arbitrary", "arbitrary"),
                                             vmem_limit_bytes=VMEM_LIMIT),
        name="hgrn",
    )(hq, hk, lfh, hi, hg, s0, norm, tri)


def _mixffn_kernel(x_ref, of_ref, oh_ref, ga_ref, gb_ref, hist_ref,
                   wbf_ref, wbh_ref, wout_ref, wup_ref, wdn_ref,
                   npost_ref, npre2_ref, npost2_ref, cw_ref, cb_ref,
                   y_ref, conv_ref, tail_ref, *, seg_len):
    tm = x_ref.shape[0]
    n_chunks = D_FF // FFN_CHUNK
    carried = seg_len >= tm
    if carried:
        @pl.when(pl.program_id(0) == 0)
        def _():
            tail_ref[...] = hist_ref[0]

    groups = [slice(r, r + FFN_ROWS) for r in range(0, tm, FFN_ROWS)] if tm > FFN_ROWS else [slice(0, tm)]

    def mix_stage(rows):
        br_f = jnp.dot(of_ref[rows, :], wbf_ref[...], preferred_element_type=F32)
        br_h = jnp.dot(oh_ref[rows, :], wbh_ref[...], preferred_element_type=F32)
        merged = ga_ref[rows, :].astype(F32) * br_f + gb_ref[rows, :].astype(F32) * br_h
        mix = jnp.dot(merged.astype(BF16), wout_ref[...], preferred_element_type=F32)
        x1 = x_ref[rows, :] + _rms_scale(mix) * npost_ref[...]
        return x1, (_rms_scale(x1) * npre2_ref[...]).astype(BF16)

    def ffn_stage(rows, x1, h2):
        n_rows = rows.stop - rows.start
        def up_chunk(j):
            w = jnp.concatenate([wup_ref[:, j * FFN_CHUNK:(j + 1) * FFN_CHUNK],
                                 wup_ref[:, D_FF + j * FFN_CHUNK:D_FF + (j + 1) * FFN_CHUNK]], axis=1)
            return jnp.dot(h2, w, preferred_element_type=F32)

        rowi = lax.broadcasted_iota(jnp.int32, (n_rows, FFN_CHUNK), 0)
        ff = None
        up_next = up_chunk(0)
        for j in range(n_chunks):
            cols = slice(j * FFN_CHUNK, (j + 1) * FFN_CHUNK)
            up = up_next
            if j + 1 < n_chunks:
                up_next = up_chunk(j + 1)
            a = up[:, :FFN_CHUNK]
            g = up[:, FFN_CHUNK:]
            prev1 = pltpu.roll(a, 1, axis=0)
            prev2 = pltpu.roll(a, 2, axis=0)
            if carried:
                t0 = tail_ref[0:1, cols]
                t1 = tail_ref[1:2, cols]
                prev1 = jnp.where(rowi == 0, t1, prev1)
                prev2 = jnp.where(rowi == 0, t0, jnp.where(rowi == 1, t1, prev2))
                tail_ref[:, cols] = a[n_rows - 2:, :]
                conv_ref[0, :, cols] = a[n_rows - 2:, :]
            else:
                for s in range(n_rows // seg_len):
                    h0 = hist_ref[s, 0:1, cols]
                    h1 = hist_ref[s, 1:2, cols]
                    prev1 = jnp.where(rowi == s * seg_len, h1, prev1)
                    prev2 = jnp.where(rowi == s * seg_len, h0, jnp.where(rowi == s * seg_len + 1, h1, prev2))
                    conv_ref[s, :, cols] = a[(s + 1) * seg_len - 2:(s + 1) * seg_len, :]
            c = cb_ref[:, cols] + cw_ref[0:1, cols] * prev2 + cw_ref[1:2, cols] * prev1 + cw_ref[2:3, cols] * a
            act = (jax.nn.gelu(c, approximate=True) * g).astype(BF16)
            part = jnp.dot(act, wdn_ref[cols, :], preferred_element_type=F32)
            ff = part if ff is None else ff + part
        y_ref[rows, :] = x1 + _rms_scale(ff) * npost2_ref[...]

    mixed = [mix_stage(rows) for rows in groups]
    for rows, (x1, h2) in zip(groups, mixed):
        ffn_stage(rows, x1, h2)


def _mixffn(x, o_fox, o_hg, ga, gb, hist, w, *, seg_len):
    rows = x.shape[0]
    tm = min(FFN_TM, rows)
    n = rows // tm
    nseg = hist.shape[0]
    row = lambda width: pl.BlockSpec((tm, width), lambda i: (i, 0))
    weights = [w["bf"], w["bh"], w["out"], w["up"], w["down"]]
    smalls = [w["npost"], w["npre2"], w["npost2"], w["conv_w"], w["conv_b"]]
    hist_spec = pl.BlockSpec(hist.shape, lambda i: (0, 0, 0))
    scratch = [pltpu.VMEM((2, D_FF), F32)]
    return pl.pallas_call(
        functools.partial(_mixffn_kernel, seg_len=seg_len),
        grid=(n,),
        in_specs=[row(D_MODEL), row(FOX_WIDTH), row(HG_WIDTH), row(D_MODEL), row(D_MODEL), hist_spec]
                 + [_const_spec(a.shape) for a in weights] + [_const_spec(a.shape) for a in smalls],
        out_specs=[row(D_MODEL), pl.BlockSpec((nseg, 2, D_FF), lambda i: (0, 0, 0))],
        out_shape=[jax.ShapeDtypeStruct((rows, D_MODEL), F32), jax.ShapeDtypeStruct((nseg, 2, D_FF), F32)],
        scratch_shapes=scratch,
        compiler_params=pltpu.CompilerParams(dimension_semantics=("arbitrary",), vmem_limit_bytes=VMEM_LIMIT),
        name="mixffn",
    )(x, o_fox, o_hg, ga, gb, hist, *weights, *smalls)


def _prep_w_in(w_in, fox_f_bias):
    offs = np.cumsum([0] + IN_SIZES)
    w_in = w_in.astype(BF16)
    seg = [w_in[:, int(offs[i]):int(offs[i + 1])] for i in range(len(IN_SIZES))]
    pad = jnp.zeros((D_MODEL, LANES - 3 * FOX_HEADS), BF16)
    f3 = jnp.concatenate([seg[3], seg[3], seg[3], pad], axis=1)
    w_all = jnp.concatenate(seg[:3] + [f3] + seg[4:], axis=1)
    fb = fox_f_bias.astype(F32)
    fb3 = jnp.concatenate([fb, fb, fb, jnp.zeros((LANES - 3 * FOX_HEADS,), F32)]).reshape(1, LANES)
    return w_all, fb3


def kernel(x_prompt, x_sample, cache_fox_k, cache_fox_v, cache_fox_logf, state_hgrn, state_ffn_conv, norm_mix_pre, norm_mix_post, w_in, fox_f_bias, hgrn_lb_logits, hgrn_norm, w_branch_fox, w_branch_hgrn, w_out, norm_ffn_pre, norm_ffn_post, w_up, ffn_conv_w, ffn_conv_b, w_down):
    depth = w_in.shape[0]
    assert depth == 1 and hgrn_lb_logits.shape[0] == 2
    bp, seq, _ = x_prompt.shape
    assert bp == 1
    nb, t_new, _ = x_sample.shape
    past = cache_fox_k.shape[2]

    w_all, fb3 = _prep_w_in(w_in[0], fox_f_bias[0])
    g_pre = norm_mix_pre[0].reshape(1, D_MODEL)
    lbl = hgrn_lb_logits.astype(F32)
    hnorm = hgrn_norm[0].astype(F32).reshape(1, HG_DIM)
    w = {
        "bf": w_branch_fox[0].astype(BF16), "bh": w_branch_hgrn[0].astype(BF16), "out": w_out[0].astype(BF16),
        "up": w_up[0].astype(BF16), "down": w_down[0].astype(BF16),
        "npost": norm_mix_post[0].reshape(1, D_MODEL), "npre2": norm_ffn_pre[0].reshape(1, D_MODEL),
        "npost2": norm_ffn_post[0].reshape(1, D_MODEL),
        "conv_w": ffn_conv_w[0], "conv_b": ffn_conv_b[0].reshape(1, D_FF),
    }

    xp = x_prompt.reshape(seq, D_MODEL)
    (qt, kh, vt, stats, pk, pv, plf, hq, hk, lfh, hi, hg, ga, gb) = _proj(xp, g_pre, w_all, fb3, lbl, fold=True)
    o_fox = _fox_prompt(qt, kh, vt, stats)
    s0 = jnp.zeros((1, HG_HEADS, HG_DIM, HG_DIM), F32)
    o_hg, p_state = _hgrn(hq, hk, lfh, hi, hg, s0, hnorm, nseq=1)
    hist0 = jnp.zeros((1, 2, D_FF), F32)
    yp, pconv = _mixffn(xp, o_fox, o_hg, ga, gb, hist0, w, seg_len=seq)

    xs = x_sample.reshape(nb * t_new, D_MODEL)
    (qs, sk, sv, slf, hq, hk, lfh, hi, hg, ga, gb) = _proj(xs, g_pre, w_all, fb3, lbl, fold=False)
    lf_all_t = jnp.concatenate([
        jnp.swapaxes(cache_fox_logf[0].astype(F32), 1, 2),
        jnp.swapaxes(slf.reshape(nb, t_new, FOX_HEADS), 1, 2),
        jnp.zeros((nb, FOX_HEADS, LANES - t_new), F32)], axis=2)
    cache_kt = jnp.transpose(cache_fox_k[0], (0, 2, 3, 1))
    cache_vt = jnp.transpose(cache_fox_v[0], (0, 2, 3, 1))
    o_fox_s = _fox_sample(qs, sk, sv, lf_all_t, cache_kt, cache_vt, t_new=t_new)
    o_hg_s, s_state = _hgrn(hq, hk, lfh, hi, hg, state_hgrn[0].astype(F32), hnorm, nseq=nb)
    ys, sconv = _mixffn(xs, o_fox_s, o_hg_s, ga, gb, state_ffn_conv[0], w, seg_len=t_new)

    return (
        yp.reshape(bp, seq, D_MODEL),
        ys.reshape(nb, t_new, D_MODEL),
        pk.reshape(1, bp, seq, FOX_HEADS, FOX_HEAD_DIM),
        pv.reshape(1, bp, seq, FOX_HEADS, FOX_HEAD_DIM),
        plf.reshape(1, bp, seq, FOX_HEADS),
        p_state.reshape(1, bp, HG_HEADS, HG_DIM, HG_DIM),
        pconv.reshape(1, bp, 2, D_FF),
        sk.reshape(1, nb, t_new, FOX_HEADS, FOX_HEAD_DIM),
        sv.reshape(1, nb, t_new, FOX_HEADS, FOX_HEAD_DIM),
        slf.reshape(1, nb, t_new, FOX_HEADS),
        s_state.reshape(1, nb, HG_HEADS, HG_DIM, HG_DIM),
        sconv.reshape(1, nb, 2, D_FF),
    )
```

```python
import functools

import numpy as np
import jax
import jax.numpy as jnp
from jax import lax
from jax.experimental import pallas as pl
from jax.experimental.pallas import tpu as pltpu

F32 = jnp.float32
BF16 = jnp.bfloat16

D_MODEL = 1024
FOX_HEADS = 8
FOX_HEAD_DIM = 64
FOX_WIDTH = FOX_HEADS * FOX_HEAD_DIM
HG_HEADS = 4
HG_DIM = 128
HG_WIDTH = HG_HEADS * HG_DIM
D_FF = 2816
RMS_EPS = 1e-6
NEG_INF = -1e30
LOG2E = 1.4426950408889634
FOX_SKIP_NATS = 110.0
IN_SIZES = [FOX_WIDTH, FOX_WIDTH, FOX_WIDTH, FOX_HEADS, HG_WIDTH, HG_WIDTH, HG_WIDTH, HG_WIDTH, D_MODEL, D_MODEL]

LANES = 128
FOX_PAD = 2 * FOX_HEAD_DIM
FOX_V_ROWS = FOX_HEAD_DIM + 16
HG_BLOCK = 64
HG_SPLIT_MAX = 60.0
VMEM_LIMIT = 56 * 1024 * 1024

PROJ_TM = 512
PROJ_ROWS = 256
FOX_TQ = 512
FOX_TK = 512
FOX_SLOTS = 3
HG_TC = 256
FFN_TM = 512
FFN_ROWS = 256
FFN_CHUNK = 256
SAMPLE_TK = 2048

_C_Q, _C_K, _C_V, _C_F = 0, 512, 1024, 1536
_C_HQ, _C_HF, _C_HI, _C_HG = 1664, 2176, 2688, 3200
_C_GA, _C_GB, _C_END = 3712, 4736, 5760


def _split3(x):
    hi = x.astype(BF16)
    r = x - hi.astype(F32)
    mid = r.astype(BF16)
    lo = (r - mid.astype(F32)).astype(BF16)
    return hi, mid, lo


def _sum_by_01_matrix(mat01, x):
    cat = jnp.concatenate(_split3(x), axis=1)
    y = jnp.dot(mat01, cat, preferred_element_type=F32)
    return y[:, :LANES] + y[:, LANES:2 * LANES] + y[:, 2 * LANES:]


def _rms_scale(x):
    return x * lax.rsqrt(jnp.mean(x * x, axis=-1, keepdims=True) + RMS_EPS)


def _log_sigmoid(x):
    return jnp.minimum(x, 0.0) - jnp.log1p(jnp.exp(-jnp.abs(x)))


def _sigmoid(x):
    return 1.0 / (1.0 + jnp.exp(-x))


def _proj_kernel(*refs, fold):
    x_ref, g_ref, w_ref = refs[:3]
    carry_ref = refs[-1]
    tm = x_ref.shape[0]
    groups = [slice(r, r + PROJ_ROWS) for r in range(0, tm, PROJ_ROWS)] if tm > PROJ_ROWS else [slice(0, tm)]
    if fold:
        @pl.when(pl.program_id(0) == 0)
        def _():
            carry_ref[...] = jnp.zeros_like(carry_ref)

    def project(rows):
        h = (_rms_scale(x_ref[rows, :]) * g_ref[...]).astype(BF16)
        return jnp.dot(h, w_ref[...], preferred_element_type=F32)

    projected = [project(rows) for rows in groups]
    for g, (rows, z) in enumerate(zip(groups, projected)):
        _proj_tail(refs, fold, g, rows, z)


def _proj_tail(refs, fold, g, rows, z):
    if fold:
        (x_ref, g_ref, w_ref, fb_ref, lbl_ref, tri_ref, pq_ref, pk_ref, cq_ref, ck_ref, cv_ref, seg_ref,
         qh_ref, kh_ref, vh_ref, stat_ref, kout_ref, vout_ref, lf_ref, hq_ref, hk_ref, lfh_ref, hi_ref, hg_ref,
         ga_ref, gb_ref, carry_ref) = refs
    else:
        (x_ref, g_ref, w_ref, fb_ref, lbl_ref,
         qs_ref, kout_ref, vout_ref, lf_ref, hq_ref, hk_ref, lfh_ref, hi_ref, hg_ref,
         ga_ref, gb_ref) = refs

    zq = z[:, _C_Q:_C_K] * (FOX_HEAD_DIM ** -0.5)
    zk = z[:, _C_K:_C_V]
    zv = z[:, _C_V:_C_F]
    kout_ref[rows, :] = zk
    vout_ref[rows, :] = zv
    logf = _log_sigmoid(z[:, _C_F:_C_HQ] + fb_ref[...])
    lf_ref[rows, :] = logf[:, :FOX_HEADS]

    l0 = lbl_ref[0:1, :]
    l1 = lbl_ref[1:2, :]
    lmax = jnp.maximum(l0, l1)
    e0 = jnp.exp(l0 - lmax)
    lb = e0 / (e0 + jnp.exp(l1 - lmax))
    f = lb + (1.0 - lb) * _sigmoid(z[:, _C_HF:_C_HI])
    hq_ref[rows, :] = z[:, _C_HQ:_C_HF].astype(BF16)
    hk_ref[rows, :] = (1.0 - f).astype(BF16)
    lfh_ref[rows, :] = jnp.log(f)
    hi_ref[rows, :] = z[:, _C_HI:_C_HG].astype(BF16)
    hg_ref[rows, :] = _sigmoid(z[:, _C_HG:_C_GA]).astype(BF16)
    ga_ref[rows, :] = _sigmoid(z[:, _C_GA:_C_GB]).astype(BF16)
    gb_ref[rows, :] = _sigmoid(z[:, _C_GB:_C_END]).astype(BF16)

    if not fold:
        qs_ref[rows, :] = zq.astype(BF16)
        return

    cum = carry_ref[...] + _sum_by_01_matrix(tri_ref[...], logf)
    carry_ref[...] = cum[-1:, :]

    seg = seg_ref[...]
    qn2 = jnp.dot((zq * zq).astype(BF16), seg, preferred_element_type=F32)
    kn2 = jnp.dot((zk * zk).astype(BF16), seg, preferred_element_type=F32)
    dg = jnp.dot((zq * zk).astype(BF16), seg, preferred_element_type=F32)
    stat_ref[g, 0:1, :] = jnp.max(qn2, axis=0, keepdims=True)
    stat_ref[g, 1:2, :] = jnp.max(kn2, axis=0, keepdims=True)
    stat_ref[g, 2:3, :] = jnp.min(dg, axis=0, keepdims=True)
    stat_ref[g, 3:4, :] = cum[0:1, :]
    stat_ref[g, 4:5, :] = cum[-1:, :]
    stat_ref[g, 5:8, :] = jnp.zeros((3, LANES), F32)

    zq = zq * LOG2E
    c_hi, c_mid, c_lo = _split3(cum * LOG2E)
    lane = lax.broadcasted_iota(jnp.int32, cum.shape, 1)
    pieces = jnp.where(lane < 8, c_hi, jnp.where(lane < 16, c_mid, c_lo))
    pieces = jnp.where(lane < 24, pieces, jnp.zeros_like(pieces))
    ex_q = jnp.dot(pieces, pq_ref[...], preferred_element_type=F32) + cq_ref[...]
    ex_k = jnp.dot(pieces, pk_ref[...], preferred_element_type=F32) + ck_ref[...]
    ex_v = cv_ref[...]

    low = lax.broadcasted_iota(jnp.int32, (zq.shape[0], LANES), 1) < FOX_HEAD_DIM
    for src, ex, dst, transposed in ((zq, ex_q, qh_ref, True), (zk, ex_k, kh_ref, False), (zv, ex_v, vh_ref, True)):
        for c in range(FOX_WIDTH // LANES):
            pair = src[:, c * LANES:(c + 1) * LANES]
            swapped = pltpu.roll(pair, FOX_HEAD_DIM, axis=1)
            for j, data in enumerate((pair, swapped)):
                hd = 2 * c + j
                blk = jnp.where(low, data, ex[:, hd * LANES:(hd + 1) * LANES])
                if not transposed:
                    dst[hd, rows, :] = blk.astype(BF16)
                elif len(dst.shape) == 3:
                    dst[hd, :, rows] = blk.T.astype(BF16)
                else:
                    dst[hd, g] = blk.T[:dst.shape[2]].astype(BF16)


def _bias_fold_constants():
    pq = np.zeros((LANES, FOX_HEADS * LANES), np.float32)
    pk = np.zeros((LANES, FOX_HEADS * LANES), np.float32)
    cq = np.zeros((1, FOX_HEADS * LANES), np.float32)
    ck = np.zeros((1, FOX_HEADS * LANES), np.float32)
    cv = np.zeros((1, FOX_HEADS * LANES), np.float32)
    for h in range(FOX_HEADS):
        base = h * LANES + FOX_HEAD_DIM
        for p in range(3):
            pq[p * 8 + h, base + p] = 1.0
            ck[0, base + p] = 1.0
            pk[p * 8 + h, base + 3 + p] = -1.0
            cq[0, base + 3 + p] = 1.0
        cv[0, base] = 1.0
    return (jnp.asarray(pq, BF16), jnp.asarray(pk, BF16), jnp.asarray(cq), jnp.asarray(ck), jnp.asarray(cv))


def _const_spec(shape, single=True):
    nd = len(shape)
    if single:
        return pl.BlockSpec(shape, lambda *_: (0,) * nd, pipeline_mode=pl.Buffered(1))
    return pl.BlockSpec(shape, lambda *_: (0,) * nd)


def _proj(x, gain, w_all, fb3, lb_logits, *, fold):
    rows = x.shape[0]
    tm = min(PROJ_TM, rows)
    n = rows // tm
    row = lambda width: pl.BlockSpec((tm, width), lambda i: (i, 0))
    in_specs = [row(D_MODEL), _const_spec((1, D_MODEL)), _const_spec(w_all.shape), _const_spec((1, LANES)),
                _const_spec(lb_logits.shape)]
    args = [x, gain, w_all, fb3, lb_logits]
    common_out = [
        (jax.ShapeDtypeStruct((rows, FOX_WIDTH), F32), row(FOX_WIDTH)),
        (jax.ShapeDtypeStruct((rows, FOX_WIDTH), F32), row(FOX_WIDTH)),
        (jax.ShapeDtypeStruct((rows, FOX_HEADS), F32), row(FOX_HEADS)),
        (jax.ShapeDtypeStruct((rows, HG_WIDTH), BF16), row(HG_WIDTH)),
        (jax.ShapeDtypeStruct((rows, HG_WIDTH), BF16), row(HG_WIDTH)),
        (jax.ShapeDtypeStruct((rows, HG_WIDTH), F32), row(HG_WIDTH)),
        (jax.ShapeDtypeStruct((rows, HG_WIDTH), BF16), row(HG_WIDTH)),
        (jax.ShapeDtypeStruct((rows, HG_WIDTH), BF16), row(HG_WIDTH)),
        (jax.ShapeDtypeStruct((rows, D_MODEL), BF16), row(D_MODEL)),
        (jax.ShapeDtypeStruct((rows, D_MODEL), BF16), row(D_MODEL)),
    ]
    scratch = []
    if fold:
        grp = min(PROJ_ROWS, tm)
        per_tile = tm // grp
        tri = jnp.asarray(np.tril(np.ones((grp, grp), np.float32)), BF16)
        seg = np.zeros((FOX_WIDTH, LANES), np.float32)
        seg[np.arange(FOX_WIDTH), np.arange(FOX_WIDTH) // FOX_HEAD_DIM] = 1.0
        consts = _bias_fold_constants() + (jnp.asarray(seg, BF16),)
        in_specs += [_const_spec(tri.shape)] + [_const_spec(c.shape) for c in consts]
        args += [tri, *consts]
        head_major = (jax.ShapeDtypeStruct((FOX_HEADS, rows, FOX_PAD), BF16),
                      pl.BlockSpec((FOX_HEADS, tm, FOX_PAD), lambda i: (0, i, 0)))
        head_major_t = lambda depth: (jax.ShapeDtypeStruct((FOX_HEADS, depth, rows), BF16),
                                      pl.BlockSpec((FOX_HEADS, depth, tm), lambda i: (0, 0, i)))
        stats = (jax.ShapeDtypeStruct((n * per_tile, 8, LANES), F32),
                 pl.BlockSpec((per_tile, 8, LANES), lambda i: (i, 0, 0)))
        v_tiles = (jax.ShapeDtypeStruct((FOX_HEADS, n * per_tile, FOX_V_ROWS, grp), BF16),
                   pl.BlockSpec((FOX_HEADS, per_tile, FOX_V_ROWS, grp), lambda i: (0, i, 0, 0)))
        outs = [head_major_t(FOX_PAD), head_major, v_tiles, stats] + common_out
        scratch = [pltpu.VMEM((1, LANES), F32)]
    else:
        outs = [(jax.ShapeDtypeStruct((rows, FOX_WIDTH), BF16), row(FOX_WIDTH))] + common_out
    return pl.pallas_call(
        functools.partial(_proj_kernel, fold=fold),
        grid=(n,),
        in_specs=in_specs,
        out_specs=[o[1] for o in outs],
        out_shape=[o[0] for o in outs],
        scratch_shapes=scratch,
        compiler_params=pltpu.CompilerParams(dimension_semantics=("arbitrary",), vmem_limit_bytes=VMEM_LIMIT),
        name="proj_fold" if fold else "proj",
    )(*args)


def _fox_kernel(ks_ref, nact_ref, order_ref, qn_ref, kn_ref, cl_ref, qt_ref, k_hbm, vt_hbm, o_ref,
                kbuf, vbuf, sem, slot0_ref, m_ref, acc_ref, *, nq):
    q_blk = pl.program_id(0)
    k_first = ks_ref[q_blk]
    tq = qt_ref.shape[2]
    n_slots, tk = kbuf.shape[0], kbuf.shape[2]
    tiles, tile = vbuf.shape[2], vbuf.shape[4]
    table = lambda ref, blk, h: lax.bitcast_convert_type(ref[blk * FOX_HEADS + h], F32)

    def successor(q, k):
        at_diagonal = k == q
        return (jnp.where(at_diagonal, q + 1, q),
                jnp.where(at_diagonal, ks_ref[jnp.minimum(q + 1, nq - 1)], k + 1))

    def for_swept_heads(q, k, slot, action):
        n_swept = nact_ref[q * nq + k]
        rows = pl.ds(pl.multiple_of(k * tk, tk), tk)
        for j in range(0, FOX_HEADS, 2):
            @pl.when(j < n_swept)
            def _():
                for h in (order_ref[q * FOX_HEADS + j], order_ref[q * FOX_HEADS + j + 1]):
                    action(pltpu.make_async_copy(k_hbm.at[h, rows, :], kbuf.at[slot, h], sem.at[0, slot]))
                    action(pltpu.make_async_copy(vt_hbm.at[h, pl.ds(k * tiles, tiles)], vbuf.at[slot, h],
                                                 sem.at[1, slot]))

    def start(q, k, slot):
        @pl.when(q < nq)
        def _():
            for_swept_heads(q, k, slot, lambda c: c.start())

    def wait(q, k, slot):
        for_swept_heads(q, k, slot, lambda c: c.wait())

    def request_second_next(k_blk, slot):
        q1, k1 = successor(q_blk, k_blk)
        q2, k2 = successor(jnp.minimum(q1, nq - 1), k1)
        start(jnp.where(q1 < nq, q2, nq), k2, lax.rem(slot + 2, n_slots))

    @pl.when(q_blk == 0)
    def _():
        slot0_ref[0] = 0
        start(q_blk, k_first, 0)
        q1, k1 = successor(q_blk, k_first)
        start(q1, k1, 1)

    slot0 = slot0_ref[0]
    slot_of = lambda k_blk: lax.rem(slot0 + (k_blk - k_first), n_slots)
    m_ref[...] = jnp.full_like(m_ref, NEG_INF)
    acc_ref[...] = jnp.zeros_like(acc_ref)

    def logits(h, slot):
        return jnp.dot(kbuf[slot, h], qt_ref[h], preferred_element_type=F32)

    def softmax_update(h, s, k_blk, slot, masked):
        if masked:
            key = lax.broadcasted_iota(jnp.int32, (tk, tq), 0)
            qry = lax.broadcasted_iota(jnp.int32, (tk, tq), 1)
            s = jnp.where(key <= qry, s, NEG_INF)
        bound = table(qn_ref, q_blk, h) * table(kn_ref, k_blk, h)
        if not masked:
            cum_q = jnp.sum(qt_ref[h, FOX_HEAD_DIM:FOX_HEAD_DIM + 3, :].astype(F32), axis=0, keepdims=True)
            bound = cum_q + (bound - table(cl_ref, k_blk, h))
        m_prev = m_ref[h]
        m_new = jnp.maximum(m_prev, bound)
        alpha = jnp.exp2(m_prev - m_new)
        p = jnp.exp2(s - m_new).astype(BF16)
        pv = sum(jnp.dot(vbuf[slot, h, j], p[j * tile:(j + 1) * tile], preferred_element_type=F32)
                 for j in range(tiles))
        acc_ref[h] = alpha * acc_ref[h] + pv
        m_ref[h] = m_new

    def sweep(heads, k_blk, slot, masked):
        s_next = logits(heads[0], slot)
        for j, h in enumerate(heads):
            s = s_next
            if j + 1 < len(heads):
                s_next = logits(heads[j + 1], slot)
            softmax_update(h, s, k_blk, slot, masked)

    def past_block(k_blk, carry):
        slot = slot_of(k_blk)
        wait(q_blk, k_blk, slot)
        request_second_next(k_blk, slot)
        n_active = nact_ref[q_blk * nq + k_blk]
        for count in range(2, FOX_HEADS + 1, 2):
            @pl.when(n_active == count)
            def _():
                sweep([order_ref[q_blk * FOX_HEADS + j] for j in range(count)], k_blk, slot, False)
        return carry

    lax.fori_loop(k_first, q_blk, past_block, 0)

    slot = slot_of(q_blk)
    wait(q_blk, q_blk, slot)
    request_second_next(q_blk, slot)
    slot0_ref[0] = lax.rem(slot + 1, n_slots)

    sweep(list(range(FOX_HEADS)), q_blk, slot, True)
    for c in range(FOX_WIDTH // LANES):
        halves = []
        for hd in (2 * c, 2 * c + 1):
            acc = acc_ref[hd]
            halves.append(acc[:FOX_HEAD_DIM, :] / acc[FOX_HEAD_DIM:FOX_HEAD_DIM + 1, :])
        o_ref[:, c * LANES:(c + 1) * LANES] = jnp.concatenate(halves, axis=0).T.astype(o_ref.dtype)


def _fox_schedule(stats, nq):
    st = stats[:, :5, :FOX_HEADS].reshape(nq, -1, 5, FOX_HEADS)
    qn = jnp.sqrt(jnp.max(st[:, :, 0], axis=1)) * 1.01
    kn = jnp.sqrt(jnp.max(st[:, :, 1], axis=1)) * 1.01
    dmin = jnp.min(st[:, :, 2], axis=1)
    c_first = st[:, 0, 3]
    c_last = st[:, -1, 4]
    bound = qn[:, None] * kn[None, :] + (c_first - dmin)[:, None] - c_last[None, :]
    blk = jnp.arange(nq, dtype=jnp.int32)
    drop = (bound < -FOX_SKIP_NATS) & (blk[None, :] < blk[:, None])[:, :, None]
    prefix = jnp.cumsum(jnp.logical_not(drop).astype(jnp.int32), axis=1) == 0
    kstart_h = jnp.sum(prefix.astype(jnp.int32), axis=1)
    kstart = jnp.min(kstart_h, axis=1)
    hd = jnp.arange(FOX_HEADS, dtype=jnp.int32)
    before = (kstart_h[:, None, :] < kstart_h[:, :, None]) | (
        (kstart_h[:, None, :] == kstart_h[:, :, None]) & (hd[None, None, :] < hd[None, :, None]))
    rank = jnp.sum(before.astype(jnp.int32), axis=2)
    order = jnp.sum(jnp.where(rank[:, None, :] == hd[None, :, None], hd[None, None, :], 0), axis=2)
    n_active = jnp.sum((kstart_h[:, None, :] <= blk[None, :, None]).astype(jnp.int32), axis=2)
    n_active = jnp.minimum((n_active + 1) // 2 * 2, FOX_HEADS)
    as_i32 = lambda a: a.astype(jnp.int32).reshape(-1)
    bits = lambda a: lax.bitcast_convert_type(a.astype(F32), jnp.int32).reshape(-1)
    return (as_i32(kstart), as_i32(n_active), as_i32(order), bits(qn * LOG2E), bits(kn), bits(c_last * LOG2E))


def _fox_prompt(qt, kh, vt, stats):
    seq = kh.shape[1]
    assert FOX_TQ == FOX_TK
    nq = seq // FOX_TQ
    tile = vt.shape[3]
    tables = _fox_schedule(stats, nq)
    grid_spec = pltpu.PrefetchScalarGridSpec(
        num_scalar_prefetch=len(tables),
        grid=(nq,),
        in_specs=[
            pl.BlockSpec((FOX_HEADS, FOX_PAD, FOX_TQ), lambda q, *_: (0, 0, q)),
            pl.BlockSpec(memory_space=pl.ANY),
            pl.BlockSpec(memory_space=pl.ANY),
        ],
        out_specs=pl.BlockSpec((FOX_TQ, FOX_WIDTH), lambda q, *_: (q, 0)),
        scratch_shapes=[pltpu.VMEM((FOX_SLOTS, FOX_HEADS, FOX_TK, FOX_PAD), BF16),
                        pltpu.VMEM((FOX_SLOTS, FOX_HEADS, FOX_TK // tile, FOX_V_ROWS, tile), BF16),
                        pltpu.SemaphoreType.DMA((2, FOX_SLOTS)),
                        pltpu.SMEM((1,), jnp.int32),
                        pltpu.VMEM((FOX_HEADS, 1, FOX_TQ), F32),
                        pltpu.VMEM((FOX_HEADS, FOX_V_ROWS, FOX_TQ), F32)],
    )
    return pl.pallas_call(
        functools.partial(_fox_kernel, nq=nq),
        grid_spec=grid_spec,
        out_shape=jax.ShapeDtypeStruct((seq, FOX_WIDTH), BF16),
        compiler_params=pltpu.CompilerParams(dimension_semantics=("arbitrary",), vmem_limit_bytes=VMEM_LIMIT),
        name="fox_prompt",
    )(*tables, qt, kh, vt)


def _lane_cumsum(x):
    n = x.shape[1]
    lane = lax.broadcasted_iota(jnp.int32, x.shape, 1)
    shift = 1
    while shift < n:
        x = x + jnp.where(lane >= shift, pltpu.roll(x, shift, axis=1), 0.0)
        shift *= 2
    return x


def _fox_sample_kernel(q_ref, kn_ref, vn_ref, lft_ref, ck_ref, cv_ref, o_ref,
                       cq_ref, cum_ref, m_ref, l_ref, acc_ref, *, n_past_chunks, tk, t_new):
    c = pl.program_id(1)
    rows = FOX_HEADS * t_new
    heads = [slice(h * FOX_HEAD_DIM, (h + 1) * FOX_HEAD_DIM) for h in range(FOX_HEADS)]
    head_rows = [slice(h * t_new, (h + 1) * t_new) for h in range(FOX_HEADS)]

    @pl.when(c == 0)
    def _():
        cum = _lane_cumsum(lft_ref[0])
        for j in range(n_past_chunks):
            cum_ref[j] = cum[:, j * tk:(j + 1) * tk]
        new_cum = cum[:, n_past_chunks * tk:n_past_chunks * tk + LANES]
        cum_ref[n_past_chunks, :, :LANES] = new_cum
        new_cum_t = jnp.concatenate([new_cum] * (LANES // FOX_HEADS), axis=0).T
        for h in range(FOX_HEADS):
            cq_ref[head_rows[h], :] = jnp.broadcast_to(new_cum_t[:t_new, h:h + 1], (t_new, LANES))
        m_ref[...] = jnp.full_like(m_ref, NEG_INF)
        l_ref[...] = jnp.zeros_like(l_ref)
        acc_ref[...] = jnp.zeros_like(acc_ref)

    nt = (((1,), (1,)), ((), ()))

    def update(qk, pv_of, ck_rows, mask):
        q = q_ref[...]
        s = jnp.concatenate([qk(q[:, heads[h]], h) for h in range(FOX_HEADS)], axis=0)
        s = s + (cq_ref[:, :1] - ck_rows)
        if mask is not None:
            s = jnp.where(mask, s, NEG_INF)
        m_prev = m_ref[...]
        m_new = jnp.maximum(m_prev, jnp.max(s, axis=1, keepdims=True))
        alpha = jnp.exp(m_prev - m_new)
        p = jnp.exp(s - m_new[:, :1])
        l_ref[...] = alpha * l_ref[...] + jnp.sum(p, axis=1, keepdims=True)
        p = p.astype(BF16)
        pv = jnp.concatenate([pv_of(p[head_rows[h]], h) for h in range(FOX_HEADS)], axis=0)
        acc_ref[...] = alpha[:, :FOX_HEAD_DIM] * acc_ref[...] + pv
        m_ref[...] = m_new

    def expand_rows(x, width):
        return jnp.concatenate([jnp.broadcast_to(x[h:h + 1, :], (t_new, width)) for h in range(FOX_HEADS)], axis=0)

    @pl.when(c < n_past_chunks)
    def _():
        update(lambda qh, h: jnp.dot(qh, ck_ref[0, h].astype(BF16), preferred_element_type=F32),
               lambda ph, h: lax.dot_general(ph, cv_ref[0, h].astype(BF16), nt, preferred_element_type=F32),
               expand_rows(cum_ref[c], tk), None)

    @pl.when(c == n_past_chunks)
    def _():
        ck_rows = expand_rows(cum_ref[n_past_chunks, :, :LANES], LANES)[:, :t_new]
        rowt = lax.broadcasted_iota(jnp.int32, (rows, t_new), 0) % t_new
        coli = lax.broadcasted_iota(jnp.int32, (rows, t_new), 1)
        update(lambda qh, h: lax.dot_general(qh, kn_ref[:, heads[h]].astype(BF16), nt, preferred_element_type=F32),
               lambda ph, h: jnp.dot(ph, vn_ref[:, heads[h]].astype(BF16), preferred_element_type=F32),
               ck_rows, coli <= rowt)
        out = acc_ref[...] / l_ref[:, :FOX_HEAD_DIM]
        o_ref[...] = jnp.concatenate([out[head_rows[h]] for h in range(FOX_HEADS)], axis=1).astype(o_ref.dtype)


def _fox_sample(qs, k_new, v_new, lf_all_t, cache_k, cache_v, *, t_new):
    nb, past = cache_k.shape[0], cache_k.shape[3]
    tk = SAMPLE_TK
    npc = past // tk
    rows = FOX_HEADS * t_new
    last = npc - 1
    kern = functools.partial(_fox_sample_kernel, n_past_chunks=npc, tk=tk, t_new=t_new)
    cache_spec = pl.BlockSpec((1, FOX_HEADS, FOX_HEAD_DIM, tk), lambda b, c: (b, 0, 0, jnp.minimum(c, last)))
    return pl.pallas_call(
        kern,
        grid=(nb, npc + 1),
        in_specs=[
            pl.BlockSpec((t_new, FOX_WIDTH), lambda b, c: (b, 0)),
            pl.BlockSpec((t_new, FOX_WIDTH), lambda b, c: (b, 0)),
            pl.BlockSpec((t_new, FOX_WIDTH), lambda b, c: (b, 0)),
            pl.BlockSpec((1, FOX_HEADS, past + LANES), lambda b, c: (b, 0, 0)),
            cache_spec, cache_spec,
        ],
        out_specs=pl.BlockSpec((t_new, FOX_WIDTH), lambda b, c: (b, 0)),
        out_shape=jax.ShapeDtypeStruct((nb * t_new, FOX_WIDTH), BF16),
        scratch_shapes=[
            pltpu.VMEM((rows, LANES), F32),
            pltpu.VMEM((npc + 1, FOX_HEADS, tk), F32),
            pltpu.VMEM((rows, LANES), F32),
            pltpu.VMEM((rows, LANES), F32),
            pltpu.VMEM((rows, FOX_HEAD_DIM), F32),
        ],
        compiler_params=pltpu.CompilerParams(dimension_semantics=("arbitrary", "arbitrary"),
                                             vmem_limit_bytes=VMEM_LIMIT),
        name="fox_sample",
    )(qs, k_new, v_new, lf_all_t, cache_k, cache_v)


def _hgrn_kernel(hq_ref, hk_ref, lfh_ref, hi_ref, hg_ref, s0_ref, norm_ref, tri_ref,
                 o_ref, sout_ref, st_ref, qq_ref, kk_ref, eb_ref, b_ref, oin_ref, od_ref):
    t = pl.program_id(1)
    tc = hq_ref.shape[0]
    blk = min(HG_BLOCK, tc)
    nblk = tc // blk
    heads = [slice(h * HG_DIM, (h + 1) * HG_DIM) for h in range(HG_HEADS)]

    @pl.when(t == 0)
    def _():
        for h in range(HG_HEADS):
            st_ref[h] = s0_ref[0, h].T

    def at_block_row(x, r):
        x3 = x.reshape(nblk, blk, HG_DIM)
        return jnp.broadcast_to(x3[:, r:r + 1, :], x3.shape).reshape(tc, HG_DIM)

    def rel_to_middle(b):
        return b - at_block_row(b, blk // 2 - 1)

    worst = jnp.zeros((), F32)
    for sl in heads:
        b = _sum_by_01_matrix(tri_ref[...], lfh_ref[:, sl])
        eb = jnp.exp(b)
        qq_ref[:, sl] = (hq_ref[:, sl].astype(F32) * eb).astype(BF16)
        kk_ref[:, sl] = (hk_ref[:, sl].astype(F32) * jnp.exp(at_block_row(b, blk - 1) - b)).astype(BF16)
        eb_ref[:, sl] = eb
        b_ref[:, sl] = b
        worst = jnp.maximum(worst, jnp.max(jnp.abs(rel_to_middle(b))))

    def carried_state_and_output():
        for j in range(nblk):
            rows = slice(j * blk, (j + 1) * blk)
            for h, sl in enumerate(heads):
                st = st_ref[h]
                oin_ref[rows, sl] = lax.dot_general(qq_ref[rows, sl], st.astype(BF16), (((1,), (1,)), ((), ())),
                                                    preferred_element_type=F32)
                upd = lax.dot_general(hi_ref[rows, sl], kk_ref[rows, sl], (((0,), (0,)), ((), ())),
                                      preferred_element_type=F32)
                st_ref[h] = st * eb_ref[(j + 1) * blk - 1:(j + 1) * blk, sl] + upd
        for sl in heads:
            o = oin_ref[:, sl] + od_ref[:, sl]
            y = (_rms_scale(o) * norm_ref[...]) * hg_ref[:, sl].astype(F32)
            o_ref[:, sl] = y.astype(o_ref.dtype)

    splittable = worst <= HG_SPLIT_MAX

    @pl.when(splittable)
    def _():
        row = lax.broadcasted_iota(jnp.int32, (tc, tc), 0)
        col = lax.broadcasted_iota(jnp.int32, (tc, tc), 1)
        pair_in_block = (row // blk == col // blk) & (col <= row)
        for sl in heads:
            b_rel = rel_to_middle(b_ref[:, sl])
            qs = (hq_ref[:, sl].astype(F32) * jnp.exp(b_rel)).astype(BF16)
            ks = (hk_ref[:, sl].astype(F32) * jnp.exp(-b_rel)).astype(BF16)
            a = lax.dot_general(qs, ks, (((1,), (1,)), ((), ())), preferred_element_type=F32)
            a = jnp.where(pair_in_block, a, 0.0).astype(BF16)
            od_ref[:, sl] = jnp.dot(a, hi_ref[:, sl], preferred_element_type=F32)
        carried_state_and_output()

    @pl.when(jnp.logical_not(splittable))
    def _():
        row_in_blk = lax.broadcasted_iota(jnp.int32, (tc, HG_DIM), 0) % blk
        for sl in heads:
            q = hq_ref[:, sl].astype(F32)
            k = hk_ref[:, sl].astype(F32)
            v = hi_ref[:, sl].astype(F32)
            b = b_ref[:, sl]

            def lag_step(lag, od):
                k_l = pltpu.roll(k, lag, axis=0)
                b_l = pltpu.roll(b, lag, axis=0)
                v_l = pltpu.roll(v, lag, axis=0)
                w = q * k_l * jnp.exp(jnp.minimum(b - b_l, 0.0))
                w = jnp.where(row_in_blk >= lag, w, 0.0)
                return od + jnp.sum(w, axis=1, keepdims=True) * v_l

            od_ref[:, sl] = lax.fori_loop(1, blk, lag_step, jnp.sum(q * k, axis=1, keepdims=True) * v)
        carried_state_and_output()

    @pl.when(t == pl.num_programs(1) - 1)
    def _():
        for h in range(HG_HEADS):
            sout_ref[0, h] = st_ref[h].T


def _hgrn_cumsum_matrix(n, blk):
    t = np.arange(n)[:, None]
    s = np.arange(n)[None, :]
    return jnp.asarray((((t // blk) == (s // blk)) & (s <= t)).astype(np.float32), BF16)


def _hgrn(hq, hk, lfh, hi, hg, s0, norm, *, nseq):
    rows = hq.shape[0]
    t_len = rows // nseq
    tc = min(HG_TC, t_len)
    nt = t_len // tc
    blk = min(HG_BLOCK, tc)
    tri = _hgrn_cumsum_matrix(tc, blk)
    row = pl.BlockSpec((tc, HG_WIDTH), lambda b, t: (b * nt + t, 0))
    state = pl.BlockSpec((1, HG_HEADS, HG_DIM, HG_DIM), lambda b, t: (b, 0, 0, 0))
    return pl.pallas_call(
        _hgrn_kernel,
        grid=(nseq, nt),
        in_specs=[row, row, row, row, row, state,
                  pl.BlockSpec((1, HG_DIM), lambda b, t: (0, 0)),
                  pl.BlockSpec((tc, tc), lambda b, t: (0, 0))],
        out_specs=[row, state],
        out_shape=[jax.ShapeDtypeStruct((rows, HG_WIDTH), BF16),
                   jax.ShapeDtypeStruct((nseq, HG_HEADS, HG_DIM, HG_DIM), F32)],
        scratch_shapes=[
            pltpu.VMEM((HG_HEADS, HG_DIM, HG_DIM), F32),
            pltpu.VMEM((tc, HG_WIDTH), BF16),
            pltpu.VMEM((tc, HG_WIDTH), BF16),
            pltpu.VMEM((tc, HG_WIDTH), F32),
            pltpu.VMEM((tc, HG_WIDTH), F32),
            pltpu.VMEM((tc, HG_WIDTH), F32),
            pltpu.VMEM((tc, HG_WIDTH), F32),
        ],
        compiler_params=pltpu.CompilerParams(dimension_semantics=("arbitrary", "arbitrary"),
                                             vmem_limit_bytes=VMEM_LIMIT),
        name="hgrn",
    )(hq, hk, lfh, hi, hg, s0, norm, tri)


def _mixffn_kernel(x_ref, of_ref, oh_ref, ga_ref, gb_ref, hist_ref,
                   wbf_ref, wbh_ref, wout_ref, wup_ref, wdn_ref,
                   npost_ref, npre2_ref, npost2_ref, cw_ref, cb_ref,
                   y_ref, conv_ref, tail_ref, *, seg_len):
    tm = x_ref.shape[0]
    n_chunks = D_FF // FFN_CHUNK
    carried = seg_len >= tm
    if carried:
        @pl.when(pl.program_id(0) == 0)
        def _():
            tail_ref[...] = hist_ref[0]

    groups = [slice(r, r + FFN_ROWS) for r in range(0, tm, FFN_ROWS)] if tm > FFN_ROWS else [slice(0, tm)]

    def mix_stage(rows):
        br_f = jnp.dot(of_ref[rows, :], wbf_ref[...], preferred_element_type=F32)
        br_h = jnp.dot(oh_ref[rows, :], wbh_ref[...], preferred_element_type=F32)
        merged = ga_ref[rows, :].astype(F32) * br_f + gb_ref[rows, :].astype(F32) * br_h
        mix = jnp.dot(merged.astype(BF16), wout_ref[...], preferred_element_type=F32)
        x1 = x_ref[rows, :] + _rms_scale(mix) * npost_ref[...]
        return x1, (_rms_scale(x1) * npre2_ref[...]).astype(BF16)

    def ffn_stage(rows, x1, h2):
        n_rows = rows.stop - rows.start
        def up_chunk(j):
            w = jnp.concatenate([wup_ref[:, j * FFN_CHUNK:(j + 1) * FFN_CHUNK],
                                 wup_ref[:, D_FF + j * FFN_CHUNK:D_FF + (j + 1) * FFN_CHUNK]], axis=1)
            return jnp.dot(h2, w, preferred_element_type=F32)

        rowi = lax.broadcasted_iota(jnp.int32, (n_rows, FFN_CHUNK), 0)
        ff = None
        up_next = up_chunk(0)
        for j in range(n_chunks):
            cols = slice(j * FFN_CHUNK, (j + 1) * FFN_CHUNK)
            up = up_next
            if j + 1 < n_chunks:
                up_next = up_chunk(j + 1)
            a = up[:, :FFN_CHUNK]
            g = up[:, FFN_CHUNK:]
            prev1 = pltpu.roll(a, 1, axis=0)
            prev2 = pltpu.roll(a, 2, axis=0)
            if carried:
                t0 = tail_ref[0:1, cols]
                t1 = tail_ref[1:2, cols]
                prev1 = jnp.where(rowi == 0, t1, prev1)
                prev2 = jnp.where(rowi == 0, t0, jnp.where(rowi == 1, t1, prev2))
                tail_ref[:, cols] = a[n_rows - 2:, :]
                conv_ref[0, :, cols] = a[n_rows - 2:, :]
            else:
                for s in range(n_rows // seg_len):
                    h0 = hist_ref[s, 0:1, cols]
                    h1 = hist_ref[s, 1:2, cols]
                    prev1 = jnp.where(rowi == s * seg_len, h1, prev1)
                    prev2 = jnp.where(rowi == s * seg_len, h0, jnp.where(rowi == s * seg_len + 1, h1, prev2))
                    conv_ref[s, :, cols] = a[(s + 1) * seg_len - 2:(s + 1) * seg_len, :]
            c = cb_ref[:, cols] + cw_ref[0:1, cols] * prev2 + cw_ref[1:2, cols] * prev1 + cw_ref[2:3, cols] * a
            act = (jax.nn.gelu(c, approximate=True) * g).astype(BF16)
            part = jnp.dot(act, wdn_ref[cols, :], preferred_element_type=F32)
            ff = part if ff is None else ff + part
        y_ref[rows, :] = x1 + _rms_scale(ff) * npost2_ref[...]

    mixed = [mix_stage(rows) for rows in groups]
    for rows, (x1, h2) in zip(groups, mixed):
        ffn_stage(rows, x1, h2)


def _mixffn(x, o_fox, o_hg, ga, gb, hist, w, *, seg_len):
    rows = x.shape[0]
    tm = min(FFN_TM, rows)
    n = rows // tm
    nseg = hist.shape[0]
    row = lambda width: pl.BlockSpec((tm, width), lambda i: (i, 0))
    weights = [w["bf"], w["bh"], w["out"], w["up"], w["down"]]
    smalls = [w["npost"], w["npre2"], w["npost2"], w["conv_w"], w["conv_b"]]
    hist_spec = pl.BlockSpec(hist.shape, lambda i: (0, 0, 0))
    scratch = [pltpu.VMEM((2, D_FF), F32)]
    return pl.pallas_call(
        functools.partial(_mixffn_kernel, seg_len=seg_len),
        grid=(n,),
        in_specs=[row(D_MODEL), row(FOX_WIDTH), row(HG_WIDTH), row(D_MODEL), row(D_MODEL), hist_spec]
                 + [_const_spec(a.shape) for a in weights] + [_const_spec(a.shape) for a in smalls],
        out_specs=[row(D_MODEL), pl.BlockSpec((nseg, 2, D_FF), lambda i: (0, 0, 0))],
        out_shape=[jax.ShapeDtypeStruct((rows, D_MODEL), F32), jax.ShapeDtypeStruct((nseg, 2, D_FF), F32)],
        scratch_shapes=scratch,
        compiler_params=pltpu.CompilerParams(dimension_semantics=("arbitrary",), vmem_limit_bytes=VMEM_LIMIT),
        name="mixffn",
    )(x, o_fox, o_hg, ga, gb, hist, *weights, *smalls)


def _prep_w_in(w_in, fox_f_bias):
    offs = np.cumsum([0] + IN_SIZES)
    w_in = w_in.astype(BF16)
    seg = [w_in[:, int(offs[i]):int(offs[i + 1])] for i in range(len(IN_SIZES))]
    pad = jnp.zeros((D_MODEL, LANES - 3 * FOX_HEADS), BF16)
    f3 = jnp.concatenate([seg[3], seg[3], seg[3], pad], axis=1)
    w_all = jnp.concatenate(seg[:3] + [f3] + seg[4:], axis=1)
    fb = fox_f_bias.astype(F32)
    fb3 = jnp.concatenate([fb, fb, fb, jnp.zeros((LANES - 3 * FOX_HEADS,), F32)]).reshape(1, LANES)
    return w_all, fb3


def kernel(x_prompt, x_sample, cache_fox_k, cache_fox_v, cache_fox_logf, state_hgrn, state_ffn_conv, norm_mix_pre, norm_mix_post, w_in, fox_f_bias, hgrn_lb_logits, hgrn_norm, w_branch_fox, w_branch_hgrn, w_out, norm_ffn_pre, norm_ffn_post, w_up, ffn_conv_w, ffn_conv_b, w_down):
    depth = w_in.shape[0]
    assert depth == 1 and hgrn_lb_logits.shape[0] == 2
    bp, seq, _ = x_prompt.shape
    assert bp == 1
    nb, t_new, _ = x_sample.shape
    past = cache_fox_k.shape[2]

    w_all, fb3 = _prep_w_in(w_in[0], fox_f_bias[0])
    g_pre = norm_mix_pre[0].reshape(1, D_MODEL)
    lbl = hgrn_lb_logits.astype(F32)
    hnorm = hgrn_norm[0].astype(F32).reshape(1, HG_DIM)
    w = {
        "bf": w_branch_fox[0].astype(BF16), "bh": w_branch_hgrn[0].astype(BF16), "out": w_out[0].astype(BF16),
        "up": w_up[0].astype(BF16), "down": w_down[0].astype(BF16),
        "npost": norm_mix_post[0].reshape(1, D_MODEL), "npre2": norm_ffn_pre[0].reshape(1, D_MODEL),
        "npost2": norm_ffn_post[0].reshape(1, D_MODEL),
        "conv_w": ffn_conv_w[0], "conv_b": ffn_conv_b[0].reshape(1, D_FF),
    }

    xp = x_prompt.reshape(seq, D_MODEL)
    (qt, kh, vt, stats, pk, pv, plf, hq, hk, lfh, hi, hg, ga, gb) = _proj(xp, g_pre, w_all, fb3, lbl, fold=True)
    o_fox = _fox_prompt(qt, kh, vt, stats)
    s0 = jnp.zeros((1, HG_HEADS, HG_DIM, HG_DIM), F32)
    o_hg, p_state = _hgrn(hq, hk, lfh, hi, hg, s0, hnorm, nseq=1)
    hist0 = jnp.zeros((1, 2, D_FF), F32)
    yp, pconv = _mixffn(xp, o_fox, o_hg, ga, gb, hist0, w, seg_len=seq)

    xs = x_sample.reshape(nb * t_new, D_MODEL)
    (qs, sk, sv, slf, hq, hk, lfh, hi, hg, ga, gb) = _proj(xs, g_pre, w_all, fb3, lbl, fold=False)
    lf_all_t = jnp.concatenate([
        jnp.swapaxes(cache_fox_logf[0].astype(F32), 1, 2),
        jnp.swapaxes(slf.reshape(nb, t_new, FOX_HEADS), 1, 2),
        jnp.zeros((nb, FOX_HEADS, LANES - t_new), F32)], axis=2)
    cache_kt = jnp.transpose(cache_fox_k[0], (0, 2, 3, 1))
    cache_vt = jnp.transpose(cache_fox_v[0], (0, 2, 3, 1))
    o_fox_s = _fox_sample(qs, sk, sv, lf_all_t, cache_kt, cache_vt, t_new=t_new)
    o_hg_s, s_state = _hgrn(hq, hk, lfh, hi, hg, state_hgrn[0].astype(F32), hnorm, nseq=nb)
    ys, sconv = _mixffn(xs, o_fox_s, o_hg_s, ga, gb, state_ffn_conv[0], w, seg_len=t_new)

    return (
        yp.reshape(bp, seq, D_MODEL),
        ys.reshape(nb, t_new, D_MODEL),
        pk.reshape(1, bp, seq, FOX_HEADS, FOX_HEAD_DIM),
        pv.reshape(1, bp, seq, FOX_HEADS, FOX_HEAD_DIM),
        plf.reshape(1, bp, seq, FOX_HEADS),
        p_state.reshape(1, bp, HG_HEADS, HG_DIM, HG_DIM),
        pconv.reshape(1, bp, 2, D_FF),
        sk.reshape(1, nb, t_new, FOX_HEADS, FOX_HEAD_DIM),
        sv.reshape(1, nb, t_new, FOX_HEADS, FOX_HEAD_DIM),
        slf.reshape(1, nb, t_new, FOX_HEADS),
        s_state.reshape(1, nb, HG_HEADS, HG_DIM, HG_DIM),
        sconv.reshape(1, nb, 2, D_FF),
    )
```

```python
import functools

import numpy as np
import jax
import jax.numpy as jnp
from jax import lax
from jax.experimental import pallas as pl
from jax.experimental.pallas import tpu as pltpu

F32 = jnp.float32
BF16 = jnp.bfloat16

D_MODEL = 1024
FOX_HEADS = 8
FOX_HEAD_DIM = 64
FOX_WIDTH = FOX_HEADS * FOX_HEAD_DIM
HG_HEADS = 4
HG_DIM = 128
HG_WIDTH = HG_HEADS * HG_DIM
D_FF = 2816
RMS_EPS = 1e-6
NEG_INF = -1e30
LOG2E = 1.4426950408889634
FOX_SKIP_NATS = 110.0
NORM_MARGIN = 1.01
IN_SIZES = [FOX_WIDTH, FOX_WIDTH, FOX_WIDTH, FOX_HEADS, HG_WIDTH, HG_WIDTH, HG_WIDTH, HG_WIDTH, D_MODEL, D_MODEL]

LANES = 128
FOX_PAD = 2 * FOX_HEAD_DIM
FOX_V_ROWS = FOX_HEAD_DIM + 16
HG_BLOCK = 64
HG_SPLIT_MAX = 60.0
VMEM_LIMIT = 56 * 1024 * 1024

PROJ_TM = 512
PROJ_ROWS = 256
FOX_TQ = 512
FOX_TK = 512
FOX_SLOTS = 4
HG_TC = 256
FFN_TM = 512
FFN_ROWS = 256
FFN_CHUNK = 256
SAMPLE_TK = 2048

_C_Q, _C_K, _C_V, _C_F = 0, 512, 1024, 1536
_C_HQ, _C_HF, _C_HI, _C_HG = 1664, 2176, 2688, 3200
_C_GA, _C_GB, _C_END = 3712, 4736, 5760


def _split3(x):
    hi = x.astype(BF16)
    r = x - hi.astype(F32)
    mid = r.astype(BF16)
    lo = (r - mid.astype(F32)).astype(BF16)
    return hi, mid, lo


def _sum_by_01_matrix(mat01, x):
    cat = jnp.concatenate(_split3(x), axis=1)
    y = jnp.dot(mat01, cat, preferred_element_type=F32)
    return y[:, :LANES] + y[:, LANES:2 * LANES] + y[:, 2 * LANES:]


def _rms_scale(x):
    return x * lax.rsqrt(jnp.mean(x * x, axis=-1, keepdims=True) + RMS_EPS)


def _log_sigmoid(x):
    return jnp.minimum(x, 0.0) - jnp.log1p(jnp.exp(-jnp.abs(x)))


def _sigmoid(x):
    return 1.0 / (1.0 + jnp.exp(-x))


def _proj_kernel(*refs, fold):
    x_ref, g_ref, w_ref = refs[:3]
    carry_ref = refs[-1]
    tm = x_ref.shape[0]
    groups = [slice(r, r + PROJ_ROWS) for r in range(0, tm, PROJ_ROWS)] if tm > PROJ_ROWS else [slice(0, tm)]
    if fold:
        @pl.when(pl.program_id(0) == 0)
        def _():
            carry_ref[...] = jnp.zeros_like(carry_ref)

    def project(rows):
        h = (_rms_scale(x_ref[rows, :]) * g_ref[...]).astype(BF16)
        return jnp.dot(h, w_ref[...], preferred_element_type=F32)

    projected = [project(rows) for rows in groups]
    for g, (rows, z) in enumerate(zip(groups, projected)):
        _proj_tail(refs, fold, g, rows, z)


def _proj_tail(refs, fold, g, rows, z):
    if fold:
        (x_ref, g_ref, w_ref, fb_ref, lbl_ref, tri_ref, pq_ref, pk_ref, cq_ref, ck_ref, cv_ref, seg_ref,
         qh_ref, kh_ref, vh_ref, stat_ref, kout_ref, vout_ref, lf_ref, hq_ref, hk_ref, lfh_ref, hi_ref, hg_ref,
         ga_ref, gb_ref, carry_ref) = refs
    else:
        (x_ref, g_ref, w_ref, fb_ref, lbl_ref,
         qs_ref, kout_ref, vout_ref, lf_ref, hq_ref, hk_ref, lfh_ref, hi_ref, hg_ref,
         ga_ref, gb_ref) = refs

    zq = z[:, _C_Q:_C_K] * (FOX_HEAD_DIM ** -0.5)
    zk = z[:, _C_K:_C_V]
    zv = z[:, _C_V:_C_F]
    kout_ref[rows, :] = zk
    vout_ref[rows, :] = zv
    logf = _log_sigmoid(z[:, _C_F:_C_HQ] + fb_ref[...])
    lf_ref[rows, :] = logf[:, :FOX_HEADS]

    l0 = lbl_ref[0:1, :]
    l1 = lbl_ref[1:2, :]
    lmax = jnp.maximum(l0, l1)
    e0 = jnp.exp(l0 - lmax)
    lb = e0 / (e0 + jnp.exp(l1 - lmax))
    f = lb + (1.0 - lb) * _sigmoid(z[:, _C_HF:_C_HI])
    hq_ref[rows, :] = z[:, _C_HQ:_C_HF].astype(BF16)
    hk_ref[rows, :] = (1.0 - f).astype(BF16)
    lfh_ref[rows, :] = jnp.log(f)
    hi_ref[rows, :] = z[:, _C_HI:_C_HG].astype(BF16)
    hg_ref[rows, :] = _sigmoid(z[:, _C_HG:_C_GA]).astype(BF16)
    ga_ref[rows, :] = _sigmoid(z[:, _C_GA:_C_GB]).astype(BF16)
    gb_ref[rows, :] = _sigmoid(z[:, _C_GB:_C_END]).astype(BF16)

    if not fold:
        qs_ref[rows, :] = zq.astype(BF16)
        return

    cum = carry_ref[...] + _sum_by_01_matrix(tri_ref[...], logf)
    carry_ref[...] = cum[-1:, :]

    seg = seg_ref[...]
    qn2 = jnp.dot((zq * zq).astype(BF16), seg, preferred_element_type=F32)
    kn2 = jnp.dot((zk * zk).astype(BF16), seg, preferred_element_type=F32)
    dg = jnp.dot((zq * zk).astype(BF16), seg, preferred_element_type=F32)
    stat_ref[g, 0:1, :] = jnp.max(qn2, axis=0, keepdims=True)
    stat_ref[g, 1:2, :] = jnp.max(kn2, axis=0, keepdims=True)
    stat_ref[g, 2:3, :] = jnp.min(dg, axis=0, keepdims=True)
    stat_ref[g, 3:4, :] = cum[0:1, :]
    stat_ref[g, 4:5, :] = cum[-1:, :]
    stat_ref[g, 5:8, :] = jnp.zeros((3, LANES), F32)

    zq = zq * LOG2E
    c_hi, c_mid, c_lo = _split3(cum * LOG2E)
    lane = lax.broadcasted_iota(jnp.int32, cum.shape, 1)
    pieces = jnp.where(lane < FOX_HEADS, c_hi, jnp.where(lane < 2 * FOX_HEADS, c_mid, c_lo))
    pieces = jnp.where(lane < 3 * FOX_HEADS, pieces, jnp.zeros_like(pieces))
    ex_q = jnp.dot(pieces, pq_ref[...], preferred_element_type=F32) + cq_ref[...]
    ex_k = jnp.dot(pieces, pk_ref[...], preferred_element_type=F32) + ck_ref[...]
    ex_v = cv_ref[...]

    low = lax.broadcasted_iota(jnp.int32, (zq.shape[0], LANES), 1) < FOX_HEAD_DIM
    for src, ex, dst, transposed in ((zq, ex_q, qh_ref, True), (zk, ex_k, kh_ref, False), (zv, ex_v, vh_ref, True)):
        for c in range(FOX_WIDTH // LANES):
            pair = src[:, c * LANES:(c + 1) * LANES]
            swapped = pltpu.roll(pair, FOX_HEAD_DIM, axis=1)
            for j, data in enumerate((pair, swapped)):
                hd = 2 * c + j
                blk = jnp.where(low, data, ex[:, hd * LANES:(hd + 1) * LANES])
                if not transposed:
                    dst[hd, rows, :] = blk.astype(BF16)
                elif len(dst.shape) == 3:
                    dst[hd, :, rows] = blk.T.astype(BF16)
                else:
                    dst[hd, g] = blk.T[:dst.shape[2]].astype(BF16)


def _bias_fold_constants():
    pq = np.zeros((LANES, FOX_HEADS * LANES), np.float32)
    pk = np.zeros((LANES, FOX_HEADS * LANES), np.float32)
    cq = np.zeros((1, FOX_HEADS * LANES), np.float32)
    ck = np.zeros((1, FOX_HEADS * LANES), np.float32)
    cv = np.zeros((1, FOX_HEADS * LANES), np.float32)
    for h in range(FOX_HEADS):
        base = h * LANES + FOX_HEAD_DIM
        for p in range(3):
            pq[p * FOX_HEADS + h, base + p] = 1.0
            ck[0, base + p] = 1.0
            pk[p * FOX_HEADS + h, base + 3 + p] = -1.0
            cq[0, base + 3 + p] = 1.0
        cv[0, base] = 1.0
    return (jnp.asarray(pq, BF16), jnp.asarray(pk, BF16), jnp.asarray(cq), jnp.asarray(ck), jnp.asarray(cv))


def _const_spec(shape, single=True):
    nd = len(shape)
    if single:
        return pl.BlockSpec(shape, lambda *_: (0,) * nd, pipeline_mode=pl.Buffered(1))
    return pl.BlockSpec(shape, lambda *_: (0,) * nd)


def _proj(x, gain, w_all, fb3, lb_logits, *, fold):
    rows = x.shape[0]
    tm = min(PROJ_TM, rows)
    n = rows // tm
    row = lambda width: pl.BlockSpec((tm, width), lambda i: (i, 0))
    in_specs = [row(D_MODEL), _const_spec((1, D_MODEL)), _const_spec(w_all.shape), _const_spec((1, LANES)),
                _const_spec(lb_logits.shape)]
    args = [x, gain, w_all, fb3, lb_logits]
    common_out = [
        (jax.ShapeDtypeStruct((rows, FOX_WIDTH), F32), row(FOX_WIDTH)),
        (jax.ShapeDtypeStruct((rows, FOX_WIDTH), F32), row(FOX_WIDTH)),
        (jax.ShapeDtypeStruct((rows, FOX_HEADS), F32), row(FOX_HEADS)),
        (jax.ShapeDtypeStruct((rows, HG_WIDTH), BF16), row(HG_WIDTH)),
        (jax.ShapeDtypeStruct((rows, HG_WIDTH), BF16), row(HG_WIDTH)),
        (jax.ShapeDtypeStruct((rows, HG_WIDTH), F32), row(HG_WIDTH)),
        (jax.ShapeDtypeStruct((rows, HG_WIDTH), BF16), row(HG_WIDTH)),
        (jax.ShapeDtypeStruct((rows, HG_WIDTH), BF16), row(HG_WIDTH)),
        (jax.ShapeDtypeStruct((rows, D_MODEL), BF16), row(D_MODEL)),
        (jax.ShapeDtypeStruct((rows, D_MODEL), BF16), row(D_MODEL)),
    ]
    scratch = []
    if fold:
        grp = min(PROJ_ROWS, tm)
        per_tile = tm // grp
        tri = jnp.asarray(np.tril(np.ones((grp, grp), np.float32)), BF16)
        seg = np.zeros((FOX_WIDTH, LANES), np.float32)
        seg[np.arange(FOX_WIDTH), np.arange(FOX_WIDTH) // FOX_HEAD_DIM] = 1.0
        consts = _bias_fold_constants() + (jnp.asarray(seg, BF16),)
        in_specs += [_const_spec(tri.shape)] + [_const_spec(c.shape) for c in consts]
        args += [tri, *consts]
        head_major = (jax.ShapeDtypeStruct((FOX_HEADS, rows, FOX_PAD), BF16),
                      pl.BlockSpec((FOX_HEADS, tm, FOX_PAD), lambda i: (0, i, 0)))
        head_major_t = lambda depth: (jax.ShapeDtypeStruct((FOX_HEADS, depth, rows), BF16),
                                      pl.BlockSpec((FOX_HEADS, depth, tm), lambda i: (0, 0, i)))
        stats = (jax.ShapeDtypeStruct((n * per_tile, 8, LANES), F32),
                 pl.BlockSpec((per_tile, 8, LANES), lambda i: (i, 0, 0)))
        v_tiles = (jax.ShapeDtypeStruct((FOX_HEADS, n * per_tile, FOX_V_ROWS, grp), BF16),
                   pl.BlockSpec((FOX_HEADS, per_tile, FOX_V_ROWS, grp), lambda i: (0, i, 0, 0)))
        outs = [head_major_t(FOX_PAD), head_major, v_tiles, stats] + common_out
        scratch = [pltpu.VMEM((1, LANES), F32)]
    else:
        outs = [(jax.ShapeDtypeStruct((rows, FOX_WIDTH), BF16), row(FOX_WIDTH))] + common_out
    return pl.pallas_call(
        functools.partial(_proj_kernel, fold=fold),
        grid=(n,),
        in_specs=in_specs,
        out_specs=[o[1] for o in outs],
        out_shape=[o[0] for o in outs],
        scratch_shapes=scratch,
        compiler_params=pltpu.CompilerParams(dimension_semantics=("arbitrary",), vmem_limit_bytes=VMEM_LIMIT),
        name="proj_fold" if fold else "proj",
    )(*args)


def _fox_kernel(ks_ref, nact_ref, order_ref, qn_ref, kn_ref, cl_ref, qt_ref, k_hbm, vt_hbm, o_ref,
                kbuf, vbuf, sem, slot0_ref, m_ref, acc_ref, *, nq):
    q_blk = pl.program_id(0)
    k_first = ks_ref[q_blk]
    tq = qt_ref.shape[2]
    n_slots, tk = kbuf.shape[0], kbuf.shape[2]
    tiles, tile = vbuf.shape[2], vbuf.shape[4]
    table = lambda ref, blk, h: lax.bitcast_convert_type(ref[blk * FOX_HEADS + h], F32)

    def successor(q, k):
        at_diagonal = k == q
        return (jnp.where(q >= nq, nq, jnp.where(at_diagonal, q + 1, q)),
                jnp.where(at_diagonal, ks_ref[jnp.minimum(q + 1, nq - 1)], k + 1))

    def for_swept_heads(q, k, slot, action):
        n_swept = nact_ref[q * nq + k]
        rows = pl.ds(pl.multiple_of(k * tk, tk), tk)
        for j in range(0, FOX_HEADS, 2):
            @pl.when(j < n_swept)
            def _():
                for h in (order_ref[q * FOX_HEADS + j], order_ref[q * FOX_HEADS + j + 1]):
                    action(pltpu.make_async_copy(k_hbm.at[h, rows, :], kbuf.at[slot, h], sem.at[0, slot]))
                    action(pltpu.make_async_copy(vt_hbm.at[h, pl.ds(k * tiles, tiles)], vbuf.at[slot, h],
                                                 sem.at[1, slot]))

    def start(q, k, slot):
        @pl.when(q < nq)
        def _():
            for_swept_heads(q, k, slot, lambda c: c.start())

    def wait(q, k, slot):
        for_swept_heads(q, k, slot, lambda c: c.wait())

    lead = n_slots - 1

    def request_ahead(k_blk, slot):
        q, k = q_blk, k_blk
        for _ in range(lead):
            q, k = successor(q, k)
        start(q, k, lax.rem(slot + lead, n_slots))

    @pl.when(q_blk == 0)
    def _():
        slot0_ref[0] = 0
        q, k = q_blk, k_first
        for ahead in range(lead):
            start(q, k, ahead)
            q, k = successor(q, k)

    slot0 = slot0_ref[0]
    slot_of = lambda k_blk: lax.rem(slot0 + (k_blk - k_first), n_slots)
    m_ref[...] = jnp.full_like(m_ref, NEG_INF)
    acc_ref[...] = jnp.zeros_like(acc_ref)

    def logits(h, slot):
        return jnp.dot(kbuf[slot, h], qt_ref[h], preferred_element_type=F32)

    def softmax_update(h, s, k_blk, slot, masked):
        if masked:
            key = lax.broadcasted_iota(jnp.int32, (tk, tq), 0)
            qry = lax.broadcasted_iota(jnp.int32, (tk, tq), 1)
            s = jnp.where(key <= qry, s, NEG_INF)
        bound = table(qn_ref, q_blk, h) * table(kn_ref, k_blk, h)
        if not masked:
            cum_q = jnp.sum(qt_ref[h, FOX_HEAD_DIM:FOX_HEAD_DIM + 3, :].astype(F32), axis=0, keepdims=True)
            bound = cum_q + (bound - table(cl_ref, k_blk, h))
        m_prev = m_ref[h]
        m_new = jnp.maximum(m_prev, bound)
        alpha = jnp.exp2(m_prev - m_new)
        p = jnp.exp2(s - m_new).astype(BF16)
        pv = sum(jnp.dot(vbuf[slot, h, j], p[j * tile:(j + 1) * tile], preferred_element_type=F32)
                 for j in range(tiles))
        acc_ref[h] = alpha * acc_ref[h] + pv
        m_ref[h] = m_new

    def sweep(heads, k_blk, slot, masked):
        s_next = logits(heads[0], slot)
        for j, h in enumerate(heads):
            s = s_next
            if j + 1 < len(heads):
                s_next = logits(heads[j + 1], slot)
            softmax_update(h, s, k_blk, slot, masked)

    def past_block(k_blk, carry):
        slot = slot_of(k_blk)
        wait(q_blk, k_blk, slot)
        request_ahead(k_blk, slot)
        n_active = nact_ref[q_blk * nq + k_blk]
        for count in range(2, FOX_HEADS + 1, 2):
            @pl.when(n_active == count)
            def _():
                sweep([order_ref[q_blk * FOX_HEADS + j] for j in range(count)], k_blk, slot, False)
        return carry

    lax.fori_loop(k_first, q_blk, past_block, 0)

    slot = slot_of(q_blk)
    wait(q_blk, q_blk, slot)
    request_ahead(q_blk, slot)
    slot0_ref[0] = lax.rem(slot + 1, n_slots)

    sweep(list(range(FOX_HEADS)), q_blk, slot, True)
    for c in range(FOX_WIDTH // LANES):
        halves = []
        for hd in (2 * c, 2 * c + 1):
            acc = acc_ref[hd]
            halves.append(acc[:FOX_HEAD_DIM, :] / acc[FOX_HEAD_DIM:FOX_HEAD_DIM + 1, :])
        o_ref[:, c * LANES:(c + 1) * LANES] = jnp.concatenate(halves, axis=0).T.astype(o_ref.dtype)


def _fox_schedule(stats, nq):
    st = stats[:, :5, :FOX_HEADS].reshape(nq, -1, 5, FOX_HEADS)
    qn = jnp.sqrt(jnp.max(st[:, :, 0], axis=1)) * NORM_MARGIN
    kn = jnp.sqrt(jnp.max(st[:, :, 1], axis=1)) * NORM_MARGIN
    dmin = jnp.min(st[:, :, 2], axis=1)
    c_first = st[:, 0, 3]
    c_last = st[:, -1, 4]
    bound = qn[:, None] * kn[None, :] + (c_first - dmin)[:, None] - c_last[None, :]
    blk = jnp.arange(nq, dtype=jnp.int32)
    drop = (bound < -FOX_SKIP_NATS) & (blk[None, :] < blk[:, None])[:, :, None]
    prefix = jnp.cumsum(jnp.logical_not(drop).astype(jnp.int32), axis=1) == 0
    kstart_h = jnp.sum(prefix.astype(jnp.int32), axis=1)
    kstart = jnp.min(kstart_h, axis=1)
    hd = jnp.arange(FOX_HEADS, dtype=jnp.int32)
    before = (kstart_h[:, None, :] < kstart_h[:, :, None]) | (
        (kstart_h[:, None, :] == kstart_h[:, :, None]) & (hd[None, None, :] < hd[None, :, None]))
    rank = jnp.sum(before.astype(jnp.int32), axis=2)
    order = jnp.sum(jnp.where(rank[:, None, :] == hd[None, :, None], hd[None, None, :], 0), axis=2)
    n_active = jnp.sum((kstart_h[:, None, :] <= blk[None, :, None]).astype(jnp.int32), axis=2)
    n_active = jnp.minimum((n_active + 1) // 2 * 2, FOX_HEADS)
    as_i32 = lambda a: a.astype(jnp.int32).reshape(-1)
    bits = lambda a: lax.bitcast_convert_type(a.astype(F32), jnp.int32).reshape(-1)
    return (as_i32(kstart), as_i32(n_active), as_i32(order), bits(qn * LOG2E), bits(kn), bits(c_last * LOG2E))


def _fox_prompt(qt, kh, vt, stats):
    seq = kh.shape[1]
    assert FOX_TQ == FOX_TK
    nq = seq // FOX_TQ
    tile = vt.shape[3]
    tables = _fox_schedule(stats, nq)
    grid_spec = pltpu.PrefetchScalarGridSpec(
        num_scalar_prefetch=len(tables),
        grid=(nq,),
        in_specs=[
            pl.BlockSpec((FOX_HEADS, FOX_PAD, FOX_TQ), lambda q, *_: (0, 0, q)),
            pl.BlockSpec(memory_space=pl.ANY),
            pl.BlockSpec(memory_space=pl.ANY),
        ],
        out_specs=pl.BlockSpec((FOX_TQ, FOX_WIDTH), lambda q, *_: (q, 0)),
        scratch_shapes=[pltpu.VMEM((FOX_SLOTS, FOX_HEADS, FOX_TK, FOX_PAD), BF16),
                        pltpu.VMEM((FOX_SLOTS, FOX_HEADS, FOX_TK // tile, FOX_V_ROWS, tile), BF16),
                        pltpu.SemaphoreType.DMA((2, FOX_SLOTS)),
                        pltpu.SMEM((1,), jnp.int32),
                        pltpu.VMEM((FOX_HEADS, 1, FOX_TQ), F32),
                        pltpu.VMEM((FOX_HEADS, FOX_V_ROWS, FOX_TQ), F32)],
    )
    return pl.pallas_call(
        functools.partial(_fox_kernel, nq=nq),
        grid_spec=grid_spec,
        out_shape=jax.ShapeDtypeStruct((seq, FOX_WIDTH), BF16),
        compiler_params=pltpu.CompilerParams(dimension_semantics=("arbitrary",), vmem_limit_bytes=VMEM_LIMIT),
        name="fox_prompt",
    )(*tables, qt, kh, vt)


def _lane_cumsum(x):
    n = x.shape[1]
    lane = lax.broadcasted_iota(jnp.int32, x.shape, 1)
    shift = 1
    while shift < n:
        x = x + jnp.where(lane >= shift, pltpu.roll(x, shift, axis=1), 0.0)
        shift *= 2
    return x


def _fox_sample_kernel(q_ref, kn_ref, vn_ref, lft_ref, ck_ref, cv_ref, o_ref,
                       cq_ref, cum_ref, m_ref, l_ref, acc_ref, *, n_past_chunks, tk, t_new):
    c = pl.program_id(1)
    rows = FOX_HEADS * t_new
    heads = [slice(h * FOX_HEAD_DIM, (h + 1) * FOX_HEAD_DIM) for h in range(FOX_HEADS)]
    head_rows = [slice(h * t_new, (h + 1) * t_new) for h in range(FOX_HEADS)]

    @pl.when(c == 0)
    def _():
        cum = _lane_cumsum(lft_ref[0])
        for j in range(n_past_chunks):
            cum_ref[j] = cum[:, j * tk:(j + 1) * tk]
        new_cum = cum[:, n_past_chunks * tk:n_past_chunks * tk + LANES]
        cum_ref[n_past_chunks, :, :LANES] = new_cum
        new_cum_t = jnp.concatenate([new_cum] * (LANES // FOX_HEADS), axis=0).T
        for h in range(FOX_HEADS):
            cq_ref[head_rows[h], :] = jnp.broadcast_to(new_cum_t[:t_new, h:h + 1], (t_new, LANES))
        m_ref[...] = jnp.full_like(m_ref, NEG_INF)
        l_ref[...] = jnp.zeros_like(l_ref)
        acc_ref[...] = jnp.zeros_like(acc_ref)

    nt = (((1,), (1,)), ((), ()))

    def update(qk, pv_of, ck_rows, mask):
        q = q_ref[...]
        s = jnp.concatenate([qk(q[:, heads[h]], h) for h in range(FOX_HEADS)], axis=0)
        s = s + (cq_ref[:, :1] - ck_rows)
        if mask is not None:
            s = jnp.where(mask, s, NEG_INF)
        m_prev = m_ref[...]
        m_new = jnp.maximum(m_prev, jnp.max(s, axis=1, keepdims=True))
        alpha = jnp.exp(m_prev - m_new)
        p = jnp.exp(s - m_new[:, :1])
        l_ref[...] = alpha * l_ref[...] + jnp.sum(p, axis=1, keepdims=True)
        p = p.astype(BF16)
        pv = jnp.concatenate([pv_of(p[head_rows[h]], h) for h in range(FOX_HEADS)], axis=0)
        acc_ref[...] = alpha[:, :FOX_HEAD_DIM] * acc_ref[...] + pv
        m_ref[...] = m_new

    def expand_rows(x, width):
        return jnp.concatenate([jnp.broadcast_to(x[h:h + 1, :], (t_new, width)) for h in range(FOX_HEADS)], axis=0)

    @pl.when(c < n_past_chunks)
    def _():
        update(lambda qh, h: jnp.dot(qh, ck_ref[0, h].astype(BF16), preferred_element_type=F32),
               lambda ph, h: lax.dot_general(ph, cv_ref[0, h].astype(BF16), nt, preferred_element_type=F32),
               expand_rows(cum_ref[c], tk), None)

    @pl.when(c == n_past_chunks)
    def _():
        ck_rows = expand_rows(cum_ref[n_past_chunks, :, :LANES], LANES)[:, :t_new]
        rowt = lax.broadcasted_iota(jnp.int32, (rows, t_new), 0) % t_new
        coli = lax.broadcasted_iota(jnp.int32, (rows, t_new), 1)
        update(lambda qh, h: lax.dot_general(qh, kn_ref[:, heads[h]].astype(BF16), nt, preferred_element_type=F32),
               lambda ph, h: jnp.dot(ph, vn_ref[:, heads[h]].astype(BF16), preferred_element_type=F32),
               ck_rows, coli <= rowt)
        out = acc_ref[...] / l_ref[:, :FOX_HEAD_DIM]
        o_ref[...] = jnp.concatenate([out[head_rows[h]] for h in range(FOX_HEADS)], axis=1).astype(o_ref.dtype)


def _fox_sample(qs, k_new, v_new, lf_all_t, cache_k, cache_v, *, t_new):
    nb, past = cache_k.shape[0], cache_k.shape[3]
    tk = SAMPLE_TK
    npc = past // tk
    rows = FOX_HEADS * t_new
    last = npc - 1
    kern = functools.partial(_fox_sample_kernel, n_past_chunks=npc, tk=tk, t_new=t_new)
    cache_spec = pl.BlockSpec((1, FOX_HEADS, FOX_HEAD_DIM, tk), lambda b, c: (b, 0, 0, jnp.minimum(c, last)))
    return pl.pallas_call(
        kern,
        grid=(nb, npc + 1),
        in_specs=[
            pl.BlockSpec((t_new, FOX_WIDTH), lambda b, c: (b, 0)),
            pl.BlockSpec((t_new, FOX_WIDTH), lambda b, c: (b, 0)),
            pl.BlockSpec((t_new, FOX_WIDTH), lambda b, c: (b, 0)),
            pl.BlockSpec((1, FOX_HEADS, past + LANES), lambda b, c: (b, 0, 0)),
            cache_spec, cache_spec,
        ],
        out_specs=pl.BlockSpec((t_new, FOX_WIDTH), lambda b, c: (b, 0)),
        out_shape=jax.ShapeDtypeStruct((nb * t_new, FOX_WIDTH), BF16),
        scratch_shapes=[
            pltpu.VMEM((rows, LANES), F32),
            pltpu.VMEM((npc + 1, FOX_HEADS, tk), F32),
            pltpu.VMEM((rows, LANES), F32),
            pltpu.VMEM((rows, LANES), F32),
            pltpu.VMEM((rows, FOX_HEAD_DIM), F32),
        ],
        compiler_params=pltpu.CompilerParams(dimension_semantics=("arbitrary", "arbitrary"),
                                             vmem_limit_bytes=VMEM_LIMIT),
        name="fox_sample",
    )(qs, k_new, v_new, lf_all_t, cache_k, cache_v)


def _hgrn_kernel(hq_ref, hk_ref, lfh_ref, hi_ref, hg_ref, s0_ref, norm_ref, tri_ref,
                 o_ref, sout_ref, st_ref, qq_ref, kk_ref, eb_ref, b_ref, oin_ref, od_ref):
    t = pl.program_id(1)
    tc = hq_ref.shape[0]
    blk = min(HG_BLOCK, tc)
    nblk = tc // blk
    heads = [slice(h * HG_DIM, (h + 1) * HG_DIM) for h in range(HG_HEADS)]

    @pl.when(t == 0)
    def _():
        for h in range(HG_HEADS):
            st_ref[h] = s0_ref[0, h].T

    def at_block_row(x, r):
        x3 = x.reshape(nblk, blk, HG_DIM)
        return jnp.broadcast_to(x3[:, r:r + 1, :], x3.shape).reshape(tc, HG_DIM)

    def rel_to_middle(b):
        return b - at_block_row(b, blk // 2 - 1)

    worst = jnp.zeros((), F32)
    for sl in heads:
        b = _sum_by_01_matrix(tri_ref[...], lfh_ref[:, sl])
        eb = jnp.exp(b)
        qq_ref[:, sl] = (hq_ref[:, sl].astype(F32) * eb).astype(BF16)
        kk_ref[:, sl] = (hk_ref[:, sl].astype(F32) * jnp.exp(at_block_row(b, blk - 1) - b)).astype(BF16)
        eb_ref[:, sl] = eb
        b_ref[:, sl] = b
        worst = jnp.maximum(worst, jnp.max(jnp.abs(rel_to_middle(b))))

    def carried_state_and_output():
        for j in range(nblk):
            rows = slice(j * blk, (j + 1) * blk)
            for h, sl in enumerate(heads):
                st = st_ref[h]
                oin_ref[rows, sl] = lax.dot_general(qq_ref[rows, sl], st.astype(BF16), (((1,), (1,)), ((), ())),
                                                    preferred_element_type=F32)
                upd = lax.dot_general(hi_ref[rows, sl], kk_ref[rows, sl], (((0,), (0,)), ((), ())),
                                      preferred_element_type=F32)
                st_ref[h] = st * eb_ref[(j + 1) * blk - 1:(j + 1) * blk, sl] + upd
        for sl in heads:
            o = oin_ref[:, sl] + od_ref[:, sl]
            y = (_rms_scale(o) * norm_ref[...]) * hg_ref[:, sl].astype(F32)
            o_ref[:, sl] = y.astype(o_ref.dtype)

    splittable = worst <= HG_SPLIT_MAX

    @pl.when(splittable)
    def _():
        row = lax.broadcasted_iota(jnp.int32, (tc, tc), 0)
        col = lax.broadcasted_iota(jnp.int32, (tc, tc), 1)
        pair_in_block = (row // blk == col // blk) & (col <= row)
        for sl in heads:
            b_rel = rel_to_middle(b_ref[:, sl])
            qs = (hq_ref[:, sl].astype(F32) * jnp.exp(b_rel)).astype(BF16)
            ks = (hk_ref[:, sl].astype(F32) * jnp.exp(-b_rel)).astype(BF16)
            a = lax.dot_general(qs, ks, (((1,), (1,)), ((), ())), preferred_element_type=F32)
            a = jnp.where(pair_in_block, a, 0.0).astype(BF16)
            od_ref[:, sl] = jnp.dot(a, hi_ref[:, sl], preferred_element_type=F32)
        carried_state_and_output()

    @pl.when(jnp.logical_not(splittable))
    def _():
        row_in_blk = lax.broadcasted_iota(jnp.int32, (tc, HG_DIM), 0) % blk
        for sl in heads:
            q = hq_ref[:, sl].astype(F32)
            k = hk_ref[:, sl].astype(F32)
            v = hi_ref[:, sl].astype(F32)
            b = b_ref[:, sl]

            def lag_step(lag, od):
                k_l = pltpu.roll(k, lag, axis=0)
                b_l = pltpu.roll(b, lag, axis=0)
                v_l = pltpu.roll(v, lag, axis=0)
                w = q * k_l * jnp.exp(jnp.minimum(b - b_l, 0.0))
                w = jnp.where(row_in_blk >= lag, w, 0.0)
                return od + jnp.sum(w, axis=1, keepdims=True) * v_l

            od_ref[:, sl] = lax.fori_loop(1, blk, lag_step, jnp.sum(q * k, axis=1, keepdims=True) * v)
        carried_state_and_output()

    @pl.when(t == pl.num_programs(1) - 1)
    def _():
        for h in range(HG_HEADS):
            sout_ref[0, h] = st_ref[h].T


def _hgrn_cumsum_matrix(n, blk):
    t = np.arange(n)[:, None]
    s = np.arange(n)[None, :]
    return jnp.asarray((((t // blk) == (s // blk)) & (s <= t)).astype(np.float32), BF16)


def _hgrn(hq, hk, lfh, hi, hg, s0, norm, *, nseq):
    rows = hq.shape[0]
    t_len = rows // nseq
    tc = min(HG_TC, t_len)
    nt = t_len // tc
    blk = min(HG_BLOCK, tc)
    tri = _hgrn_cumsum_matrix(tc, blk)
    row = pl.BlockSpec((tc, HG_WIDTH), lambda b, t: (b * nt + t, 0))
    state = pl.BlockSpec((1, HG_HEADS, HG_DIM, HG_DIM), lambda b, t: (b, 0, 0, 0))
    return pl.pallas_call(
        _hgrn_kernel,
        grid=(nseq, nt),
        in_specs=[row, row, row, row, row, state,
                  pl.BlockSpec((1, HG_DIM), lambda b, t: (0, 0)),
                  pl.BlockSpec((tc, tc), lambda b, t: (0, 0))],
        out_specs=[row, state],
        out_shape=[jax.ShapeDtypeStruct((rows, HG_WIDTH), BF16),
                   jax.ShapeDtypeStruct((nseq, HG_HEADS, HG_DIM, HG_DIM), F32)],
        scratch_shapes=[
            pltpu.VMEM((HG_HEADS, HG_DIM, HG_DIM), F32),
            pltpu.VMEM((tc, HG_WIDTH), BF16),
            pltpu.VMEM((tc, HG_WIDTH), BF16),
            pltpu.VMEM((tc, HG_WIDTH), F32),
            pltpu.VMEM((tc, HG_WIDTH), F32),
            pltpu.VMEM((tc, HG_WIDTH), F32),
            pltpu.VMEM((tc, HG_WIDTH), F32),
        ],
        compiler_params=pltpu.CompilerParams(dimension_semantics=("arbitrary", "arbitrary"),
                                             vmem_limit_bytes=VMEM_LIMIT),
        name="hgrn",
    )(hq, hk, lfh, hi, hg, s0, norm, tri)


def _mixffn_kernel(x_ref, of_ref, oh_ref, ga_ref, gb_ref, hist_ref,
                   wbf_ref, wbh_ref, wout_ref, wup_ref, wdn_ref,
                   npost_ref, npre2_ref, npost2_ref, cw_ref, cb_ref,
                   y_ref, conv_ref, tail_ref, *, seg_len):
    tm = x_ref.shape[0]
    n_chunks = D_FF // FFN_CHUNK
    carried = seg_len >= tm
    if carried:
        @pl.when(pl.program_id(0) == 0)
        def _():
            tail_ref[...] = hist_ref[0]

    groups = [slice(r, r + FFN_ROWS) for r in range(0, tm, FFN_ROWS)] if tm > FFN_ROWS else [slice(0, tm)]

    def mix_stage(rows):
        br_f = jnp.dot(of_ref[rows, :], wbf_ref[...], preferred_element_type=F32)
        br_h = jnp.dot(oh_ref[rows, :], wbh_ref[...], preferred_element_type=F32)
        merged = ga_ref[rows, :].astype(F32) * br_f + gb_ref[rows, :].astype(F32) * br_h
        mix = jnp.dot(merged.astype(BF16), wout_ref[...], preferred_element_type=F32)
        x1 = x_ref[rows, :] + _rms_scale(mix) * npost_ref[...]
        return x1, (_rms_scale(x1) * npre2_ref[...]).astype(BF16)

    def ffn_stage(rows, x1, h2):
        n_rows = rows.stop - rows.start
        def up_chunk(j):
            w = jnp.concatenate([wup_ref[:, j * FFN_CHUNK:(j + 1) * FFN_CHUNK],
                                 wup_ref[:, D_FF + j * FFN_CHUNK:D_FF + (j + 1) * FFN_CHUNK]], axis=1)
            return jnp.dot(h2, w, preferred_element_type=F32)

        rowi = lax.broadcasted_iota(jnp.int32, (n_rows, FFN_CHUNK), 0)
        ff = None
        up_next = up_chunk(0)
        for j in range(n_chunks):
            cols = slice(j * FFN_CHUNK, (j + 1) * FFN_CHUNK)
            up = up_next
            if j + 1 < n_chunks:
                up_next = up_chunk(j + 1)
            a = up[:, :FFN_CHUNK]
            g = up[:, FFN_CHUNK:]
            prev1 = pltpu.roll(a, 1, axis=0)
            prev2 = pltpu.roll(a, 2, axis=0)
            if carried:
                t0 = tail_ref[0:1, cols]
                t1 = tail_ref[1:2, cols]
                prev1 = jnp.where(rowi == 0, t1, prev1)
                prev2 = jnp.where(rowi == 0, t0, jnp.where(rowi == 1, t1, prev2))
                tail_ref[:, cols] = a[n_rows - 2:, :]
                conv_ref[0, :, cols] = a[n_rows - 2:, :]
            else:
                for s in range(n_rows // seg_len):
                    h0 = hist_ref[s, 0:1, cols]
                    h1 = hist_ref[s, 1:2, cols]
                    prev1 = jnp.where(rowi == s * seg_len, h1, prev1)
                    prev2 = jnp.where(rowi == s * seg_len, h0, jnp.where(rowi == s * seg_len + 1, h1, prev2))
                    conv_ref[s, :, cols] = a[(s + 1) * seg_len - 2:(s + 1) * seg_len, :]
            c = cb_ref[:, cols] + cw_ref[0:1, cols] * prev2 + cw_ref[1:2, cols] * prev1 + cw_ref[2:3, cols] * a
            act = (jax.nn.gelu(c, approximate=True) * g).astype(BF16)
            part = jnp.dot(act, wdn_ref[cols, :], preferred_element_type=F32)
            ff = part if ff is None else ff + part
        y_ref[rows, :] = x1 + _rms_scale(ff) * npost2_ref[...]

    mixed = [mix_stage(rows) for rows in groups]
    for rows, (x1, h2) in zip(groups, mixed):
        ffn_stage(rows, x1, h2)


def _mixffn(x, o_fox, o_hg, ga, gb, hist, w, *, seg_len):
    rows = x.shape[0]
    tm = min(FFN_TM, rows)
    n = rows // tm
    nseg = hist.shape[0]
    row = lambda width: pl.BlockSpec((tm, width), lambda i: (i, 0))
    weights = [w["bf"], w["bh"], w["out"], w["up"], w["down"]]
    smalls = [w["npost"], w["npre2"], w["npost2"], w["conv_w"], w["conv_b"]]
    hist_spec = pl.BlockSpec(hist.shape, lambda i: (0, 0, 0))
    scratch = [pltpu.VMEM((2, D_FF), F32)]
    return pl.pallas_call(
        functools.partial(_mixffn_kernel, seg_len=seg_len),
        grid=(n,),
        in_specs=[row(D_MODEL), row(FOX_WIDTH), row(HG_WIDTH), row(D_MODEL), row(D_MODEL), hist_spec]
                 + [_const_spec(a.shape) for a in weights] + [_const_spec(a.shape) for a in smalls],
        out_specs=[row(D_MODEL), pl.BlockSpec((nseg, 2, D_FF), lambda i: (0, 0, 0))],
        out_shape=[jax.ShapeDtypeStruct((rows, D_MODEL), F32), jax.ShapeDtypeStruct((nseg, 2, D_FF), F32)],
        scratch_shapes=scratch,
        compiler_params=pltpu.CompilerParams(dimension_semantics=("arbitrary",), vmem_limit_bytes=VMEM_LIMIT),
        name="mixffn",
    )(x, o_fox, o_hg, ga, gb, hist, *weights, *smalls)


def _prep_w_in(w_in, fox_f_bias):
    offs = np.cumsum([0] + IN_SIZES)
    w_in = w_in.astype(BF16)
    seg = [w_in[:, int(offs[i]):int(offs[i + 1])] for i in range(len(IN_SIZES))]
    pad = jnp.zeros((D_MODEL, LANES - 3 * FOX_HEADS), BF16)
    f3 = jnp.concatenate([seg[3], seg[3], seg[3], pad], axis=1)
    w_all = jnp.concatenate(seg[:3] + [f3] + seg[4:], axis=1)
    fb = fox_f_bias.astype(F32)
    fb3 = jnp.concatenate([fb, fb, fb, jnp.zeros((LANES - 3 * FOX_HEADS,), F32)]).reshape(1, LANES)
    return w_all, fb3


def kernel(x_prompt, x_sample, cache_fox_k, cache_fox_v, cache_fox_logf, state_hgrn, state_ffn_conv, norm_mix_pre, norm_mix_post, w_in, fox_f_bias, hgrn_lb_logits, hgrn_norm, w_branch_fox, w_branch_hgrn, w_out, norm_ffn_pre, norm_ffn_post, w_up, ffn_conv_w, ffn_conv_b, w_down):
    depth = w_in.shape[0]
    assert depth == 1 and hgrn_lb_logits.shape[0] == 2
    bp, seq, _ = x_prompt.shape
    assert bp == 1
    nb, t_new, _ = x_sample.shape
    past = cache_fox_k.shape[2]

    w_all, fb3 = _prep_w_in(w_in[0], fox_f_bias[0])
    g_pre = norm_mix_pre[0].reshape(1, D_MODEL)
    lbl = hgrn_lb_logits.astype(F32)
    hnorm = hgrn_norm[0].astype(F32).reshape(1, HG_DIM)
    w = {
        "bf": w_branch_fox[0].astype(BF16), "bh": w_branch_hgrn[0].astype(BF16), "out": w_out[0].astype(BF16),
        "up": w_up[0].astype(BF16), "down": w_down[0].astype(BF16),
        "npost": norm_mix_post[0].reshape(1, D_MODEL), "npre2": norm_ffn_pre[0].reshape(1, D_MODEL),
        "npost2": norm_ffn_post[0].reshape(1, D_MODEL),
        "conv_w": ffn_conv_w[0], "conv_b": ffn_conv_b[0].reshape(1, D_FF),
    }

    xp = x_prompt.reshape(seq, D_MODEL)
    (qt, kh, vt, stats, pk, pv, plf, hq, hk, lfh, hi, hg, ga, gb) = _proj(xp, g_pre, w_all, fb3, lbl, fold=True)
    o_fox = _fox_prompt(qt, kh, vt, stats)
    s0 = jnp.zeros((1, HG_HEADS, HG_DIM, HG_DIM), F32)
    o_hg, p_state = _hgrn(hq, hk, lfh, hi, hg, s0, hnorm, nseq=1)
    hist0 = jnp.zeros((1, 2, D_FF), F32)
    yp, pconv = _mixffn(xp, o_fox, o_hg, ga, gb, hist0, w, seg_len=seq)

    xs = x_sample.reshape(nb * t_new, D_MODEL)
    (qs, sk, sv, slf, hq, hk, lfh, hi, hg, ga, gb) = _proj(xs, g_pre, w_all, fb3, lbl, fold=False)
    lf_all_t = jnp.concatenate([
        jnp.swapaxes(cache_fox_logf[0].astype(F32), 1, 2),
        jnp.swapaxes(slf.reshape(nb, t_new, FOX_HEADS), 1, 2),
        jnp.zeros((nb, FOX_HEADS, LANES - t_new), F32)], axis=2)
    cache_kt = jnp.transpose(cache_fox_k[0], (0, 2, 3, 1))
    cache_vt = jnp.transpose(cache_fox_v[0], (0, 2, 3, 1))
    o_fox_s = _fox_sample(qs, sk, sv, lf_all_t, cache_kt, cache_vt, t_new=t_new)
    o_hg_s, s_state = _hgrn(hq, hk, lfh, hi, hg, state_hgrn[0].astype(F32), hnorm, nseq=nb)
    ys, sconv = _mixffn(xs, o_fox_s, o_hg_s, ga, gb, state_ffn_conv[0], w, seg_len=t_new)

    return (
        yp.reshape(bp, seq, D_MODEL),
        ys.reshape(nb, t_new, D_MODEL),
        pk.reshape(1, bp, seq, FOX_HEADS, FOX_HEAD_DIM),
        pv.reshape(1, bp, seq, FOX_HEADS, FOX_HEAD_DIM),
        plf.reshape(1, bp, seq, FOX_HEADS),
        p_state.reshape(1, bp, HG_HEADS, HG_DIM, HG_DIM),
        pconv.reshape(1, bp, 2, D_FF),
        sk.reshape(1, nb, t_new, FOX_HEADS, FOX_HEAD_DIM),
        sv.reshape(1, nb, t_new, FOX_HEADS, FOX_HEAD_DIM),
        slf.reshape(1, nb, t_new, FOX_HEADS),
        s_state.reshape(1, nb, HG_HEADS, HG_DIM, HG_DIM),
        sconv.reshape(1, nb, 2, D_FF),
    )
```

```python
import functools

import numpy as np
import jax
import jax.numpy as jnp
from jax import lax
from jax.experimental import pallas as pl
from jax.experimental.pallas import tpu as pltpu

F32 = jnp.float32
BF16 = jnp.bfloat16

D_MODEL = 1024
FOX_HEADS = 8
FOX_HEAD_DIM = 64
FOX_WIDTH = FOX_HEADS * FOX_HEAD_DIM
HG_HEADS = 4
HG_DIM = 128
HG_WIDTH = HG_HEADS * HG_DIM
D_FF = 2816
RMS_EPS = 1e-6
NEG_INF = -1e30
LOG2E = 1.4426950408889634
FOX_SKIP_NATS = 110.0
NORM_MARGIN = 1.01
IN_SIZES = [FOX_WIDTH, FOX_WIDTH, FOX_WIDTH, FOX_HEADS, HG_WIDTH, HG_WIDTH, HG_WIDTH, HG_WIDTH, D_MODEL, D_MODEL]

LANES = 128
FOX_PAD = 2 * FOX_HEAD_DIM
FOX_V_ROWS = FOX_HEAD_DIM + 16
HG_BLOCK = 64
HG_SPLIT_MAX = 60.0
VMEM_LIMIT = 56 * 1024 * 1024

PROJ_TM = 512
PROJ_ROWS = 256
FOX_TQ = 512
FOX_TK = 512
FOX_SLOTS = 3
HG_TC = 256
FFN_TM = 512
FFN_ROWS = 256
FFN_CHUNK = 256
SAMPLE_TK = 2048

_C_Q, _C_K, _C_V, _C_F = 0, 512, 1024, 1536
_C_HQ, _C_HF, _C_HI, _C_HG = 1664, 2176, 2688, 3200
_C_GA, _C_GB, _C_END = 3712, 4736, 5760


def _split3(x):
    hi = x.astype(BF16)
    r = x - hi.astype(F32)
    mid = r.astype(BF16)
    lo = (r - mid.astype(F32)).astype(BF16)
    return hi, mid, lo


def _sum_by_01_matrix(mat01, x):
    cat = jnp.concatenate(_split3(x), axis=1)
    y = jnp.dot(mat01, cat, preferred_element_type=F32)
    return y[:, :LANES] + y[:, LANES:2 * LANES] + y[:, 2 * LANES:]


def _rms_scale(x):
    return x * lax.rsqrt(jnp.mean(x * x, axis=-1, keepdims=True) + RMS_EPS)


def _log_sigmoid(x):
    return jnp.minimum(x, 0.0) - jnp.log1p(jnp.exp(-jnp.abs(x)))


def _sigmoid(x):
    return 1.0 / (1.0 + jnp.exp(-x))


def _proj_kernel(*refs, fold):
    x_ref, g_ref, w_ref = refs[:3]
    carry_ref = refs[-1]
    tm = x_ref.shape[0]
    groups = [slice(r, r + PROJ_ROWS) for r in range(0, tm, PROJ_ROWS)] if tm > PROJ_ROWS else [slice(0, tm)]
    if fold:
        @pl.when(pl.program_id(0) == 0)
        def _():
            carry_ref[...] = jnp.zeros_like(carry_ref)

    def project(rows):
        h = (_rms_scale(x_ref[rows, :]) * g_ref[...]).astype(BF16)
        return jnp.dot(h, w_ref[...], preferred_element_type=F32)

    projected = [project(rows) for rows in groups]
    for g, (rows, z) in enumerate(zip(groups, projected)):
        _proj_tail(refs, fold, g, rows, z)


def _proj_tail(refs, fold, g, rows, z):
    if fold:
        (x_ref, g_ref, w_ref, fb_ref, lbl_ref, tri_ref, pq_ref, pk_ref, cq_ref, ck_ref, cv_ref, seg_ref,
         qh_ref, kh_ref, vh_ref, stat_ref, kout_ref, vout_ref, lf_ref, hq_ref, hk_ref, lfh_ref, hi_ref, hg_ref,
         ga_ref, gb_ref, carry_ref) = refs
    else:
        (x_ref, g_ref, w_ref, fb_ref, lbl_ref,
         qs_ref, kout_ref, vout_ref, lf_ref, hq_ref, hk_ref, lfh_ref, hi_ref, hg_ref,
         ga_ref, gb_ref) = refs

    zq = z[:, _C_Q:_C_K] * (FOX_HEAD_DIM ** -0.5)
    zk = z[:, _C_K:_C_V]
    zv = z[:, _C_V:_C_F]
    kout_ref[rows, :] = zk
    vout_ref[rows, :] = zv
    logf = _log_sigmoid(z[:, _C_F:_C_HQ] + fb_ref[...])
    lf_ref[rows, :] = logf[:, :FOX_HEADS]

    l0 = lbl_ref[0:1, :]
    l1 = lbl_ref[1:2, :]
    lmax = jnp.maximum(l0, l1)
    e0 = jnp.exp(l0 - lmax)
    lb = e0 / (e0 + jnp.exp(l1 - lmax))
    f = lb + (1.0 - lb) * _sigmoid(z[:, _C_HF:_C_HI])
    hq_ref[rows, :] = z[:, _C_HQ:_C_HF].astype(BF16)
    hk_ref[rows, :] = (1.0 - f).astype(BF16)
    lfh_ref[rows, :] = jnp.log(f)
    hi_ref[rows, :] = z[:, _C_HI:_C_HG].astype(BF16)
    hg_ref[rows, :] = _sigmoid(z[:, _C_HG:_C_GA]).astype(BF16)
    ga_ref[rows, :] = _sigmoid(z[:, _C_GA:_C_GB]).astype(BF16)
    gb_ref[rows, :] = _sigmoid(z[:, _C_GB:_C_END]).astype(BF16)

    if not fold:
        qs_ref[rows, :] = zq.astype(BF16)
        return

    cum = carry_ref[...] + _sum_by_01_matrix(tri_ref[...], logf)
    carry_ref[...] = cum[-1:, :]

    seg = seg_ref[...]
    qn2 = jnp.dot((zq * zq).astype(BF16), seg, preferred_element_type=F32)
    kn2 = jnp.dot((zk * zk).astype(BF16), seg, preferred_element_type=F32)
    dg = jnp.dot((zq * zk).astype(BF16), seg, preferred_element_type=F32)
    stat_ref[g, 0:1, :] = jnp.max(qn2, axis=0, keepdims=True)
    stat_ref[g, 1:2, :] = jnp.max(kn2, axis=0, keepdims=True)
    stat_ref[g, 2:3, :] = jnp.min(dg, axis=0, keepdims=True)
    stat_ref[g, 3:4, :] = cum[0:1, :]
    stat_ref[g, 4:5, :] = cum[-1:, :]
    stat_ref[g, 5:8, :] = jnp.zeros((3, LANES), F32)

    zq = zq * LOG2E
    c_hi, c_mid, c_lo = _split3(cum * LOG2E)
    lane = lax.broadcasted_iota(jnp.int32, cum.shape, 1)
    pieces = jnp.where(lane < FOX_HEADS, c_hi, jnp.where(lane < 2 * FOX_HEADS, c_mid, c_lo))
    pieces = jnp.where(lane < 3 * FOX_HEADS, pieces, jnp.zeros_like(pieces))
    ex_q = jnp.dot(pieces, pq_ref[...], preferred_element_type=F32) + cq_ref[...]
    ex_k = jnp.dot(pieces, pk_ref[...], preferred_element_type=F32) + ck_ref[...]
    ex_v = cv_ref[...]

    low = lax.broadcasted_iota(jnp.int32, (zq.shape[0], LANES), 1) < FOX_HEAD_DIM
    for src, ex, dst, transposed in ((zq, ex_q, qh_ref, True), (zk, ex_k, kh_ref, False), (zv, ex_v, vh_ref, True)):
        for c in range(FOX_WIDTH // LANES):
            pair = src[:, c * LANES:(c + 1) * LANES]
            swapped = pltpu.roll(pair, FOX_HEAD_DIM, axis=1)
            for j, data in enumerate((pair, swapped)):
                hd = 2 * c + j
                blk = jnp.where(low, data, ex[:, hd * LANES:(hd + 1) * LANES])
                if not transposed:
                    dst[hd, rows, :] = blk.astype(BF16)
                elif len(dst.shape) == 3:
                    dst[hd, :, rows] = blk.T.astype(BF16)
                else:
                    dst[hd, g] = blk.T[:dst.shape[2]].astype(BF16)


def _bias_fold_constants():
    pq = np.zeros((LANES, FOX_HEADS * LANES), np.float32)
    pk = np.zeros((LANES, FOX_HEADS * LANES), np.float32)
    cq = np.zeros((1, FOX_HEADS * LANES), np.float32)
    ck = np.zeros((1, FOX_HEADS * LANES), np.float32)
    cv = np.zeros((1, FOX_HEADS * LANES), np.float32)
    for h in range(FOX_HEADS):
        base = h * LANES + FOX_HEAD_DIM
        for p in range(3):
            pq[p * FOX_HEADS + h, base + p] = 1.0
            ck[0, base + p] = 1.0
            pk[p * FOX_HEADS + h, base + 3 + p] = -1.0
            cq[0, base + 3 + p] = 1.0
        cv[0, base] = 1.0
    return (jnp.asarray(pq, BF16), jnp.asarray(pk, BF16), jnp.asarray(cq), jnp.asarray(ck), jnp.asarray(cv))


def _const_spec(shape, single=True):
    nd = len(shape)
    if single:
        return pl.BlockSpec(shape, lambda *_: (0,) * nd, pipeline_mode=pl.Buffered(1))
    return pl.BlockSpec(shape, lambda *_: (0,) * nd)


def _proj(x, gain, w_all, fb3, lb_logits, *, fold):
    rows = x.shape[0]
    tm = min(PROJ_TM, rows)
    n = rows // tm
    row = lambda width: pl.BlockSpec((tm, width), lambda i: (i, 0))
    in_specs = [row(D_MODEL), _const_spec((1, D_MODEL)), _const_spec(w_all.shape), _const_spec((1, LANES)),
                _const_spec(lb_logits.shape)]
    args = [x, gain, w_all, fb3, lb_logits]
    common_out = [
        (jax.ShapeDtypeStruct((rows, FOX_WIDTH), F32), row(FOX_WIDTH)),
        (jax.ShapeDtypeStruct((rows, FOX_WIDTH), F32), row(FOX_WIDTH)),
        (jax.ShapeDtypeStruct((rows, FOX_HEADS), F32), row(FOX_HEADS)),
        (jax.ShapeDtypeStruct((rows, HG_WIDTH), BF16), row(HG_WIDTH)),
        (jax.ShapeDtypeStruct((rows, HG_WIDTH), BF16), row(HG_WIDTH)),
        (jax.ShapeDtypeStruct((rows, HG_WIDTH), F32), row(HG_WIDTH)),
        (jax.ShapeDtypeStruct((rows, HG_WIDTH), BF16), row(HG_WIDTH)),
        (jax.ShapeDtypeStruct((rows, HG_WIDTH), BF16), row(HG_WIDTH)),
        (jax.ShapeDtypeStruct((rows, D_MODEL), BF16), row(D_MODEL)),
        (jax.ShapeDtypeStruct((rows, D_MODEL), BF16), row(D_MODEL)),
    ]
    scratch = []
    if fold:
        grp = min(PROJ_ROWS, tm)
        per_tile = tm // grp
        tri = jnp.asarray(np.tril(np.ones((grp, grp), np.float32)), BF16)
        seg = np.zeros((FOX_WIDTH, LANES), np.float32)
        seg[np.arange(FOX_WIDTH), np.arange(FOX_WIDTH) // FOX_HEAD_DIM] = 1.0
        consts = _bias_fold_constants() + (jnp.asarray(seg, BF16),)
        in_specs += [_const_spec(tri.shape)] + [_const_spec(c.shape) for c in consts]
        args += [tri, *consts]
        head_major = (jax.ShapeDtypeStruct((FOX_HEADS, rows, FOX_PAD), BF16),
                      pl.BlockSpec((FOX_HEADS, tm, FOX_PAD), lambda i: (0, i, 0)))
        head_major_t = lambda depth: (jax.ShapeDtypeStruct((FOX_HEADS, depth, rows), BF16),
                                      pl.BlockSpec((FOX_HEADS, depth, tm), lambda i: (0, 0, i)))
        stats = (jax.ShapeDtypeStruct((n * per_tile, 8, LANES), F32),
                 pl.BlockSpec((per_tile, 8, LANES), lambda i: (i, 0, 0)))
        v_tiles = (jax.ShapeDtypeStruct((FOX_HEADS, n * per_tile, FOX_V_ROWS, grp), BF16),
                   pl.BlockSpec((FOX_HEADS, per_tile, FOX_V_ROWS, grp), lambda i: (0, i, 0, 0)))
        outs = [head_major_t(FOX_PAD), head_major, v_tiles, stats] + common_out
        scratch = [pltpu.VMEM((1, LANES), F32)]
    else:
        outs = [(jax.ShapeDtypeStruct((rows, FOX_WIDTH), BF16), row(FOX_WIDTH))] + common_out
    return pl.pallas_call(
        functools.partial(_proj_kernel, fold=fold),
        grid=(n,),
        in_specs=in_specs,
        out_specs=[o[1] for o in outs],
        out_shape=[o[0] for o in outs],
        scratch_shapes=scratch,
        compiler_params=pltpu.CompilerParams(dimension_semantics=("arbitrary",), vmem_limit_bytes=VMEM_LIMIT),
        name="proj_fold" if fold else "proj",
    )(*args)


def _fox_kernel(ks_ref, nact_ref, order_ref, qn_ref, kn_ref, cl_ref, qt_ref, k_hbm, vt_hbm, o_ref,
                kbuf, vbuf, sem, slot0_ref, m_ref, acc_ref, *, nq):
    q_blk = pl.program_id(0)
    k_first = ks_ref[q_blk]
    tq = qt_ref.shape[2]
    n_slots, tk = kbuf.shape[0], kbuf.shape[2]
    tiles, tile = vbuf.shape[2], vbuf.shape[4]
    table = lambda ref, blk, h: lax.bitcast_convert_type(ref[blk * FOX_HEADS + h], F32)

    def successor(q, k):
        at_diagonal = k == q
        return (jnp.where(q >= nq, nq, jnp.where(at_diagonal, q + 1, q)),
                jnp.where(at_diagonal, ks_ref[jnp.minimum(q + 1, nq - 1)], k + 1))

    def for_swept_heads(q, k, slot, action):
        n_swept = nact_ref[q * nq + k]
        rows = pl.ds(pl.multiple_of(k * tk, tk), tk)
        for j in range(0, FOX_HEADS, 2):
            @pl.when(j < n_swept)
            def _():
                for h in (order_ref[q * FOX_HEADS + j], order_ref[q * FOX_HEADS + j + 1]):
                    action(pltpu.make_async_copy(k_hbm.at[h, rows, :], kbuf.at[slot, h], sem.at[0, slot]))
                    action(pltpu.make_async_copy(vt_hbm.at[h, pl.ds(k * tiles, tiles)], vbuf.at[slot, h],
                                                 sem.at[1, slot]))

    def start(q, k, slot):
        @pl.when(q < nq)
        def _():
            for_swept_heads(q, k, slot, lambda c: c.start())

    def wait(q, k, slot):
        for_swept_heads(q, k, slot, lambda c: c.wait())

    lead = n_slots - 1

    def request_ahead(k_blk, slot):
        q, k = q_blk, k_blk
        for _ in range(lead):
            q, k = successor(q, k)
        start(q, k, lax.rem(slot + lead, n_slots))

    @pl.when(q_blk == 0)
    def _():
        slot0_ref[0] = 0
        q, k = q_blk, k_first
        for ahead in range(lead):
            start(q, k, ahead)
            q, k = successor(q, k)

    slot0 = slot0_ref[0]
    slot_of = lambda k_blk: lax.rem(slot0 + (k_blk - k_first), n_slots)
    m_ref[...] = jnp.full_like(m_ref, NEG_INF)
    acc_ref[...] = jnp.zeros_like(acc_ref)

    def logits(h, slot):
        return jnp.dot(kbuf[slot, h], qt_ref[h], preferred_element_type=F32)

    def softmax_update(h, s, k_blk, slot, masked):
        if masked:
            key = lax.broadcasted_iota(jnp.int32, (tk, tq), 0)
            qry = lax.broadcasted_iota(jnp.int32, (tk, tq), 1)
            s = jnp.where(key <= qry, s, NEG_INF)
        bound = table(qn_ref, q_blk, h) * table(kn_ref, k_blk, h)
        if not masked:
            cum_q = jnp.sum(qt_ref[h, FOX_HEAD_DIM:FOX_HEAD_DIM + 3, :].astype(F32), axis=0, keepdims=True)
            bound = cum_q + (bound - table(cl_ref, k_blk, h))
        m_prev = m_ref[h]
        m_new = jnp.maximum(m_prev, bound)
        alpha = jnp.exp2(m_prev - m_new)
        p = jnp.exp2(s - m_new).astype(BF16)
        pv = sum(jnp.dot(vbuf[slot, h, j], p[j * tile:(j + 1) * tile], preferred_element_type=F32)
                 for j in range(tiles))
        acc_ref[h] = alpha * acc_ref[h] + pv
        m_ref[h] = m_new

    def sweep(heads, k_blk, slot, masked):
        s_next = logits(heads[0], slot)
        for j, h in enumerate(heads):
            s = s_next
            if j + 1 < len(heads):
                s_next = logits(heads[j + 1], slot)
            softmax_update(h, s, k_blk, slot, masked)

    def past_block(k_blk, carry):
        slot = slot_of(k_blk)
        wait(q_blk, k_blk, slot)
        request_ahead(k_blk, slot)
        n_active = nact_ref[q_blk * nq + k_blk]
        for count in range(2, FOX_HEADS + 1, 2):
            @pl.when(n_active == count)
            def _():
                sweep([order_ref[q_blk * FOX_HEADS + j] for j in range(count)], k_blk, slot, False)
        return carry

    lax.fori_loop(k_first, q_blk, past_block, 0)

    slot = slot_of(q_blk)
    wait(q_blk, q_blk, slot)
    request_ahead(q_blk, slot)
    slot0_ref[0] = lax.rem(slot + 1, n_slots)

    sweep(list(range(FOX_HEADS)), q_blk, slot, True)
    for c in range(FOX_WIDTH // LANES):
        halves = []
        for hd in (2 * c, 2 * c + 1):
            acc = acc_ref[hd]
            halves.append(acc[:FOX_HEAD_DIM, :] / acc[FOX_HEAD_DIM:FOX_HEAD_DIM + 1, :])
        o_ref[:, c * LANES:(c + 1) * LANES] = jnp.concatenate(halves, axis=0).T.astype(o_ref.dtype)


def _fox_schedule(stats, nq):
    st = stats[:, :5, :FOX_HEADS].reshape(nq, -1, 5, FOX_HEADS)
    qn = jnp.sqrt(jnp.max(st[:, :, 0], axis=1)) * NORM_MARGIN
    kn = jnp.sqrt(jnp.max(st[:, :, 1], axis=1)) * NORM_MARGIN
    dmin = jnp.min(st[:, :, 2], axis=1)
    c_first = st[:, 0, 3]
    c_last = st[:, -1, 4]
    bound = qn[:, None] * kn[None, :] + (c_first - dmin)[:, None] - c_last[None, :]
    blk = jnp.arange(nq, dtype=jnp.int32)
    drop = (bound < -FOX_SKIP_NATS) & (blk[None, :] < blk[:, None])[:, :, None]
    prefix = jnp.cumsum(jnp.logical_not(drop).astype(jnp.int32), axis=1) == 0
    kstart_h = jnp.sum(prefix.astype(jnp.int32), axis=1)
    kstart = jnp.min(kstart_h, axis=1)
    hd = jnp.arange(FOX_HEADS, dtype=jnp.int32)
    before = (kstart_h[:, None, :] < kstart_h[:, :, None]) | (
        (kstart_h[:, None, :] == kstart_h[:, :, None]) & (hd[None, None, :] < hd[None, :, None]))
    rank = jnp.sum(before.astype(jnp.int32), axis=2)
    order = jnp.sum(jnp.where(rank[:, None, :] == hd[None, :, None], hd[None, None, :], 0), axis=2)
    n_active = jnp.sum((kstart_h[:, None, :] <= blk[None, :, None]).astype(jnp.int32), axis=2)
    n_active = jnp.minimum((n_active + 1) // 2 * 2, FOX_HEADS)
    as_i32 = lambda a: a.astype(jnp.int32).reshape(-1)
    bits = lambda a: lax.bitcast_convert_type(a.astype(F32), jnp.int32).reshape(-1)
    return (as_i32(kstart), as_i32(n_active), as_i32(order), bits(qn * LOG2E), bits(kn), bits(c_last * LOG2E))


def _fox_prompt(qt, kh, vt, stats):
    seq = kh.shape[1]
    assert FOX_TQ == FOX_TK
    nq = seq // FOX_TQ
    tile = vt.shape[3]
    tables = _fox_schedule(stats, nq)
    grid_spec = pltpu.PrefetchScalarGridSpec(
        num_scalar_prefetch=len(tables),
        grid=(nq,),
        in_specs=[
            pl.BlockSpec((FOX_HEADS, FOX_PAD, FOX_TQ), lambda q, *_: (0, 0, q)),
            pl.BlockSpec(memory_space=pl.ANY),
            pl.BlockSpec(memory_space=pl.ANY),
        ],
        out_specs=pl.BlockSpec((FOX_TQ, FOX_WIDTH), lambda q, *_: (q, 0)),
        scratch_shapes=[pltpu.VMEM((FOX_SLOTS, FOX_HEADS, FOX_TK, FOX_PAD), BF16),
                        pltpu.VMEM((FOX_SLOTS, FOX_HEADS, FOX_TK // tile, FOX_V_ROWS, tile), BF16),
                        pltpu.SemaphoreType.DMA((2, FOX_SLOTS)),
                        pltpu.SMEM((1,), jnp.int32),
                        pltpu.VMEM((FOX_HEADS, 1, FOX_TQ), F32),
                        pltpu.VMEM((FOX_HEADS, FOX_V_ROWS, FOX_TQ), F32)],
    )
    return pl.pallas_call(
        functools.partial(_fox_kernel, nq=nq),
        grid_spec=grid_spec,
        out_shape=jax.ShapeDtypeStruct((seq, FOX_WIDTH), BF16),
        compiler_params=pltpu.CompilerParams(dimension_semantics=("arbitrary",), vmem_limit_bytes=VMEM_LIMIT),
        name="fox_prompt",
    )(*tables, qt, kh, vt)


def _lane_cumsum(x):
    n = x.shape[1]
    lane = lax.broadcasted_iota(jnp.int32, x.shape, 1)
    shift = 1
    while shift < n:
        x = x + jnp.where(lane >= shift, pltpu.roll(x, shift, axis=1), 0.0)
        shift *= 2
    return x


def _fox_sample_kernel(q_ref, kn_ref, vn_ref, lft_ref, ck_ref, cv_ref, o_ref,
                       cq_ref, cum_ref, m_ref, l_ref, acc_ref, *, n_past_chunks, tk, t_new):
    c = pl.program_id(1)
    rows = FOX_HEADS * t_new
    heads = [slice(h * FOX_HEAD_DIM, (h + 1) * FOX_HEAD_DIM) for h in range(FOX_HEADS)]
    head_rows = [slice(h * t_new, (h + 1) * t_new) for h in range(FOX_HEADS)]

    @pl.when(c == 0)
    def _():
        cum = _lane_cumsum(lft_ref[0])
        for j in range(n_past_chunks):
            cum_ref[j] = cum[:, j * tk:(j + 1) * tk]
        new_cum = cum[:, n_past_chunks * tk:n_past_chunks * tk + LANES]
        cum_ref[n_past_chunks, :, :LANES] = new_cum
        new_cum_t = jnp.concatenate([new_cum] * (LANES // FOX_HEADS), axis=0).T
        for h in range(FOX_HEADS):
            cq_ref[head_rows[h], :] = jnp.broadcast_to(new_cum_t[:t_new, h:h + 1], (t_new, LANES))
        m_ref[...] = jnp.full_like(m_ref, NEG_INF)
        l_ref[...] = jnp.zeros_like(l_ref)
        acc_ref[...] = jnp.zeros_like(acc_ref)

    nt = (((1,), (1,)), ((), ()))

    def update(qk, pv_of, ck_rows, mask):
        q = q_ref[...]
        s = jnp.concatenate([qk(q[:, heads[h]], h) for h in range(FOX_HEADS)], axis=0)
        s = s + (cq_ref[:, :1] - ck_rows)
        if mask is not None:
            s = jnp.where(mask, s, NEG_INF)
        m_prev = m_ref[...]
        m_new = jnp.maximum(m_prev, jnp.max(s, axis=1, keepdims=True))
        alpha = jnp.exp(m_prev - m_new)
        p = jnp.exp(s - m_new[:, :1])
        l_ref[...] = alpha * l_ref[...] + jnp.sum(p, axis=1, keepdims=True)
        p = p.astype(BF16)
        pv = jnp.concatenate([pv_of(p[head_rows[h]], h) for h in range(FOX_HEADS)], axis=0)
        acc_ref[...] = alpha[:, :FOX_HEAD_DIM] * acc_ref[...] + pv
        m_ref[...] = m_new

    def expand_rows(x, width):
        return jnp.concatenate([jnp.broadcast_to(x[h:h + 1, :], (t_new, width)) for h in range(FOX_HEADS)], axis=0)

    @pl.when(c < n_past_chunks)
    def _():
        update(lambda qh, h: jnp.dot(qh, ck_ref[0, h].astype(BF16), preferred_element_type=F32),
               lambda ph, h: lax.dot_general(ph, cv_ref[0, h].astype(BF16), nt, preferred_element_type=F32),
               expand_rows(cum_ref[c], tk), None)

    @pl.when(c == n_past_chunks)
    def _():
        ck_rows = expand_rows(cum_ref[n_past_chunks, :, :LANES], LANES)[:, :t_new]
        rowt = lax.broadcasted_iota(jnp.int32, (rows, t_new), 0) % t_new
        coli = lax.broadcasted_iota(jnp.int32, (rows, t_new), 1)
        update(lambda qh, h: lax.dot_general(qh, kn_ref[:, heads[h]].astype(BF16), nt, preferred_element_type=F32),
               lambda ph, h: jnp.dot(ph, vn_ref[:, heads[h]].astype(BF16), preferred_element_type=F32),
               ck_rows, coli <= rowt)
        out = acc_ref[...] / l_ref[:, :FOX_HEAD_DIM]
        o_ref[...] = jnp.concatenate([out[head_rows[h]] for h in range(FOX_HEADS)], axis=1).astype(o_ref.dtype)


def _fox_sample(qs, k_new, v_new, lf_all_t, cache_k, cache_v, *, t_new):
    nb, past = cache_k.shape[0], cache_k.shape[3]
    tk = SAMPLE_TK
    npc = past // tk
    rows = FOX_HEADS * t_new
    last = npc - 1
    kern = functools.partial(_fox_sample_kernel, n_past_chunks=npc, tk=tk, t_new=t_new)
    cache_spec = pl.BlockSpec((1, FOX_HEADS, FOX_HEAD_DIM, tk), lambda b, c: (b, 0, 0, jnp.minimum(c, last)))
    return pl.pallas_call(
        kern,
        grid=(nb, npc + 1),
        in_specs=[
            pl.BlockSpec((t_new, FOX_WIDTH), lambda b, c: (b, 0)),
            pl.BlockSpec((t_new, FOX_WIDTH), lambda b, c: (b, 0)),
            pl.BlockSpec((t_new, FOX_WIDTH), lambda b, c: (b, 0)),
            pl.BlockSpec((1, FOX_HEADS, past + LANES), lambda b, c: (b, 0, 0)),
            cache_spec, cache_spec,
        ],
        out_specs=pl.BlockSpec((t_new, FOX_WIDTH), lambda b, c: (b, 0)),
        out_shape=jax.ShapeDtypeStruct((nb * t_new, FOX_WIDTH), BF16),
        scratch_shapes=[
            pltpu.VMEM((rows, LANES), F32),
            pltpu.VMEM((npc + 1, FOX_HEADS, tk), F32),
            pltpu.VMEM((rows, LANES), F32),
            pltpu.VMEM((rows, LANES), F32),
            pltpu.VMEM((rows, FOX_HEAD_DIM), F32),
        ],
        compiler_params=pltpu.CompilerParams(dimension_semantics=("arbitrary", "arbitrary"),
                                             vmem_limit_bytes=VMEM_LIMIT),
        name="fox_sample",
    )(qs, k_new, v_new, lf_all_t, cache_k, cache_v)


def _hgrn_kernel(hq_ref, hk_ref, lfh_ref, hi_ref, hg_ref, s0_ref, norm_ref, tri_ref,
                 o_ref, sout_ref, st_ref, qq_ref, kk_ref, eb_ref, b_ref, oin_ref, od_ref):
    t = pl.program_id(1)
    tc = hq_ref.shape[0]
    blk = min(HG_BLOCK, tc)
    nblk = tc // blk
    heads = [slice(h * HG_DIM, (h + 1) * HG_DIM) for h in range(HG_HEADS)]

    @pl.when(t == 0)
    def _():
        for h in range(HG_HEADS):
            st_ref[h] = s0_ref[0, h].T

    def at_block_row(x, r):
        x3 = x.reshape(nblk, blk, HG_DIM)
        return jnp.broadcast_to(x3[:, r:r + 1, :], x3.shape).reshape(tc, HG_DIM)

    def rel_to_middle(b):
        return b - at_block_row(b, blk // 2 - 1)

    worst = jnp.zeros((), F32)
    for sl in heads:
        b = _sum_by_01_matrix(tri_ref[...], lfh_ref[:, sl])
        eb = jnp.exp(b)
        qq_ref[:, sl] = (hq_ref[:, sl].astype(F32) * eb).astype(BF16)
        kk_ref[:, sl] = (hk_ref[:, sl].astype(F32) * jnp.exp(at_block_row(b, blk - 1) - b)).astype(BF16)
        eb_ref[:, sl] = eb
        b_ref[:, sl] = b
        worst = jnp.maximum(worst, jnp.max(jnp.abs(rel_to_middle(b))))

    def carried_state_and_output():
        for j in range(nblk):
            rows = slice(j * blk, (j + 1) * blk)
            for h, sl in enumerate(heads):
                st = st_ref[h]
                oin_ref[rows, sl] = lax.dot_general(qq_ref[rows, sl], st.astype(BF16), (((1,), (1,)), ((), ())),
                                                    preferred_element_type=F32)
                upd = lax.dot_general(hi_ref[rows, sl], kk_ref[rows, sl], (((0,), (0,)), ((), ())),
                                      preferred_element_type=F32)
                st_ref[h] = st * eb_ref[(j + 1) * blk - 1:(j + 1) * blk, sl] + upd
        for sl in heads:
            o = oin_ref[:, sl] + od_ref[:, sl]
            y = (_rms_scale(o) * norm_ref[...]) * hg_ref[:, sl].astype(F32)
            o_ref[:, sl] = y.astype(o_ref.dtype)

    splittable = worst <= HG_SPLIT_MAX

    @pl.when(splittable)
    def _():
        row = lax.broadcasted_iota(jnp.int32, (tc, tc), 0)
        col = lax.broadcasted_iota(jnp.int32, (tc, tc), 1)
        pair_in_block = (row // blk == col // blk) & (col <= row)
        for sl in heads:
            b_rel = rel_to_middle(b_ref[:, sl])
            qs = (hq_ref[:, sl].astype(F32) * jnp.exp(b_rel)).astype(BF16)
            ks = (hk_ref[:, sl].astype(F32) * jnp.exp(-b_rel)).astype(BF16)
            a = lax.dot_general(qs, ks, (((1,), (1,)), ((), ())), preferred_element_type=F32)
            a = jnp.where(pair_in_block, a, 0.0).astype(BF16)
            od_ref[:, sl] = jnp.dot(a, hi_ref[:, sl], preferred_element_type=F32)
        carried_state_and_output()

    @pl.when(jnp.logical_not(splittable))
    def _():
        row_in_blk = lax.broadcasted_iota(jnp.int32, (tc, HG_DIM), 0) % blk
        for sl in heads:
            q = hq_ref[:, sl].astype(F32)
            k = hk_ref[:, sl].astype(F32)
            v = hi_ref[:, sl].astype(F32)
            b = b_ref[:, sl]

            def lag_step(lag, od):
                k_l = pltpu.roll(k, lag, axis=0)
                b_l = pltpu.roll(b, lag, axis=0)
                v_l = pltpu.roll(v, lag, axis=0)
                w = q * k_l * jnp.exp(jnp.minimum(b - b_l, 0.0))
                w = jnp.where(row_in_blk >= lag, w, 0.0)
                return od + jnp.sum(w, axis=1, keepdims=True) * v_l

            od_ref[:, sl] = lax.fori_loop(1, blk, lag_step, jnp.sum(q * k, axis=1, keepdims=True) * v)
        carried_state_and_output()

    @pl.when(t == pl.num_programs(1) - 1)
    def _():
        for h in range(HG_HEADS):
            sout_ref[0, h] = st_ref[h].T


def _hgrn_cumsum_matrix(n, blk):
    t = np.arange(n)[:, None]
    s = np.arange(n)[None, :]
    return jnp.asarray((((t // blk) == (s // blk)) & (s <= t)).astype(np.float32), BF16)


def _hgrn(hq, hk, lfh, hi, hg, s0, norm, *, nseq):
    rows = hq.shape[0]
    t_len = rows // nseq
    tc = min(HG_TC, t_len)
    nt = t_len // tc
    blk = min(HG_BLOCK, tc)
    tri = _hgrn_cumsum_matrix(tc, blk)
    row = pl.BlockSpec((tc, HG_WIDTH), lambda b, t: (b * nt + t, 0))
    state = pl.BlockSpec((1, HG_HEADS, HG_DIM, HG_DIM), lambda b, t: (b, 0, 0, 0))
    return pl.pallas_call(
        _hgrn_kernel,
        grid=(nseq, nt),
        in_specs=[row, row, row, row, row, state,
                  pl.BlockSpec((1, HG_DIM), lambda b, t: (0, 0)),
                  pl.BlockSpec((tc, tc), lambda b, t: (0, 0))],
        out_specs=[row, state],
        out_shape=[jax.ShapeDtypeStruct((rows, HG_WIDTH), BF16),
                   jax.ShapeDtypeStruct((nseq, HG_HEADS, HG_DIM, HG_DIM), F32)],
        scratch_shapes=[
            pltpu.VMEM((HG_HEADS, HG_DIM, HG_DIM), F32),
            pltpu.VMEM((tc, HG_WIDTH), BF16),
            pltpu.VMEM((tc, HG_WIDTH), BF16),
            pltpu.VMEM((tc, HG_WIDTH), F32),
            pltpu.VMEM((tc, HG_WIDTH), F32),
            pltpu.VMEM((tc, HG_WIDTH), F32),
            pltpu.VMEM((tc, HG_WIDTH), F32),
        ],
        compiler_params=pltpu.CompilerParams(dimension_semantics=("arbitrary", "arbitrary"),
                                             vmem_limit_bytes=VMEM_LIMIT),
        name="hgrn",
    )(hq, hk, lfh, hi, hg, s0, norm, tri)


def _mixffn_kernel(x_ref, of_ref, oh_ref, ga_ref, gb_ref, hist_ref,
                   wbf_ref, wbh_ref, wout_ref, wup_ref, wdn_ref,
                   npost_ref, npre2_ref, npost2_ref, cw_ref, cb_ref,
                   y_ref, conv_ref, tail_ref, *, seg_len):
    tm = x_ref.shape[0]
    n_chunks = D_FF // FFN_CHUNK
    carried = seg_len >= tm
    if carried:
        @pl.when(pl.program_id(0) == 0)
        def _():
            tail_ref[...] = hist_ref[0]

    groups = [slice(r, r + FFN_ROWS) for r in range(0, tm, FFN_ROWS)] if tm > FFN_ROWS else [slice(0, tm)]

    def mix_stage(rows):
        br_f = jnp.dot(of_ref[rows, :], wbf_ref[...], preferred_element_type=F32)
        br_h = jnp.dot(oh_ref[rows, :], wbh_ref[...], preferred_element_type=F32)
        merged = ga_ref[rows, :].astype(F32) * br_f + gb_ref[rows, :].astype(F32) * br_h
        mix = jnp.dot(merged.astype(BF16), wout_ref[...], preferred_element_type=F32)
        x1 = x_ref[rows, :] + _rms_scale(mix) * npost_ref[...]
        return x1, (_rms_scale(x1) * npre2_ref[...]).astype(BF16)

    def ffn_stage(rows, x1, h2):
        n_rows = rows.stop - rows.start
        def up_chunk(j):
            w = jnp.concatenate([wup_ref[:, j * FFN_CHUNK:(j + 1) * FFN_CHUNK],
                                 wup_ref[:, D_FF + j * FFN_CHUNK:D_FF + (j + 1) * FFN_CHUNK]], axis=1)
            return jnp.dot(h2, w, preferred_element_type=F32)

        rowi = lax.broadcasted_iota(jnp.int32, (n_rows, FFN_CHUNK), 0)
        ff = None
        up_next = up_chunk(0)
        for j in range(n_chunks):
            cols = slice(j * FFN_CHUNK, (j + 1) * FFN_CHUNK)
            up = up_next
            if j + 1 < n_chunks:
                up_next = up_chunk(j + 1)
            a = up[:, :FFN_CHUNK]
            g = up[:, FFN_CHUNK:]
            prev1 = pltpu.roll(a, 1, axis=0)
            prev2 = pltpu.roll(a, 2, axis=0)
            if carried:
                t0 = tail_ref[0:1, cols]
                t1 = tail_ref[1:2, cols]
                prev1 = jnp.where(rowi == 0, t1, prev1)
                prev2 = jnp.where(rowi == 0, t0, jnp.where(rowi == 1, t1, prev2))
                tail_ref[:, cols] = a[n_rows - 2:, :]
                conv_ref[0, :, cols] = a[n_rows - 2:, :]
            else:
                for s in range(n_rows // seg_len):
                    h0 = hist_ref[s, 0:1, cols]
                    h1 = hist_ref[s, 1:2, cols]
                    prev1 = jnp.where(rowi == s * seg_len, h1, prev1)
                    prev2 = jnp.where(rowi == s * seg_len, h0, jnp.where(rowi == s * seg_len + 1, h1, prev2))
                    conv_ref[s, :, cols] = a[(s + 1) * seg_len - 2:(s + 1) * seg_len, :]
            c = cb_ref[:, cols] + cw_ref[0:1, cols] * prev2 + cw_ref[1:2, cols] * prev1 + cw_ref[2:3, cols] * a
            act = (jax.nn.gelu(c, approximate=True) * g).astype(BF16)
            part = jnp.dot(act, wdn_ref[cols, :], preferred_element_type=F32)
            ff = part if ff is None else ff + part
        y_ref[rows, :] = x1 + _rms_scale(ff) * npost2_ref[...]

    mixed = [mix_stage(rows) for rows in groups]
    for rows, (x1, h2) in zip(groups, mixed):
        ffn_stage(rows, x1, h2)


def _mixffn(x, o_fox, o_hg, ga, gb, hist, w, *, seg_len):
    rows = x.shape[0]
    tm = min(FFN_TM, rows)
    n = rows // tm
    nseg = hist.shape[0]
    row = lambda width: pl.BlockSpec((tm, width), lambda i: (i, 0))
    weights = [w["bf"], w["bh"], w["out"], w["up"], w["down"]]
    smalls = [w["npost"], w["npre2"], w["npost2"], w["conv_w"], w["conv_b"]]
    hist_spec = pl.BlockSpec(hist.shape, lambda i: (0, 0, 0))
    scratch = [pltpu.VMEM((2, D_FF), F32)]
    return pl.pallas_call(
        functools.partial(_mixffn_kernel, seg_len=seg_len),
        grid=(n,),
        in_specs=[row(D_MODEL), row(FOX_WIDTH), row(HG_WIDTH), row(D_MODEL), row(D_MODEL), hist_spec]
                 + [_const_spec(a.shape) for a in weights] + [_const_spec(a.shape) for a in smalls],
        out_specs=[row(D_MODEL), pl.BlockSpec((nseg, 2, D_FF), lambda i: (0, 0, 0))],
        out_shape=[jax.ShapeDtypeStruct((rows, D_MODEL), F32), jax.ShapeDtypeStruct((nseg, 2, D_FF), F32)],
        scratch_shapes=scratch,
        compiler_params=pltpu.CompilerParams(dimension_semantics=("arbitrary",), vmem_limit_bytes=VMEM_LIMIT),
        name="mixffn",
    )(x, o_fox, o_hg, ga, gb, hist, *weights, *smalls)


def _prep_w_in(w_in, fox_f_bias):
    offs = np.cumsum([0] + IN_SIZES)
    w_in = w_in.astype(BF16)
    seg = [w_in[:, int(offs[i]):int(offs[i + 1])] for i in range(len(IN_SIZES))]
    pad = jnp.zeros((D_MODEL, LANES - 3 * FOX_HEADS), BF16)
    f3 = jnp.concatenate([seg[3], seg[3], seg[3], pad], axis=1)
    w_all = jnp.concatenate(seg[:3] + [f3] + seg[4:], axis=1)
    fb = fox_f_bias.astype(F32)
    fb3 = jnp.concatenate([fb, fb, fb, jnp.zeros((LANES - 3 * FOX_HEADS,), F32)]).reshape(1, LANES)
    return w_all, fb3


def kernel(x_prompt, x_sample, cache_fox_k, cache_fox_v, cache_fox_logf, state_hgrn, state_ffn_conv, norm_mix_pre, norm_mix_post, w_in, fox_f_bias, hgrn_lb_logits, hgrn_norm, w_branch_fox, w_branch_hgrn, w_out, norm_ffn_pre, norm_ffn_post, w_up, ffn_conv_w, ffn_conv_b, w_down):
    depth = w_in.shape[0]
    assert depth == 1 and hgrn_lb_logits.shape[0] == 2
    bp, seq, _ = x_prompt.shape
    assert bp == 1
    nb, t_new, _ = x_sample.shape
    past = cache_fox_k.shape[2]

    w_all, fb3 = _prep_w_in(w_in[0], fox_f_bias[0])
    g_pre = norm_mix_pre[0].reshape(1, D_MODEL)
    lbl = hgrn_lb_logits.astype(F32)
    hnorm = hgrn_norm[0].astype(F32).reshape(1, HG_DIM)
    w = {
        "bf": w_branch_fox[0].astype(BF16), "bh": w_branch_hgrn[0].astype(BF16), "out": w_out[0].astype(BF16),
        "up": w_up[0].astype(BF16), "down": w_down[0].astype(BF16),
        "npost": norm_mix_post[0].reshape(1, D_MODEL), "npre2": norm_ffn_pre[0].reshape(1, D_MODEL),
        "npost2": norm_ffn_post[0].reshape(1, D_MODEL),
        "conv_w": ffn_conv_w[0], "conv_b": ffn_conv_b[0].reshape(1, D_FF),
    }

    xp = x_prompt.reshape(seq, D_MODEL)
    (qt, kh, vt, stats, pk, pv, plf, hq, hk, lfh, hi, hg, ga, gb) = _proj(xp, g_pre, w_all, fb3, lbl, fold=True)
    o_fox = _fox_prompt(qt, kh, vt, stats)
    s0 = jnp.zeros((1, HG_HEADS, HG_DIM, HG_DIM), F32)
    o_hg, p_state = _hgrn(hq, hk, lfh, hi, hg, s0, hnorm, nseq=1)
    hist0 = jnp.zeros((1, 2, D_FF), F32)
    yp, pconv = _mixffn(xp, o_fox, o_hg, ga, gb, hist0, w, seg_len=seq)

    xs = x_sample.reshape(nb * t_new, D_MODEL)
    (qs, sk, sv, slf, hq, hk, lfh, hi, hg, ga, gb) = _proj(xs, g_pre, w_all, fb3, lbl, fold=False)
    lf_all_t = jnp.concatenate([
        jnp.swapaxes(cache_fox_logf[0].astype(F32), 1, 2),
        jnp.swapaxes(slf.reshape(nb, t_new, FOX_HEADS), 1, 2),
        jnp.zeros((nb, FOX_HEADS, LANES - t_new), F32)], axis=2)
    cache_kt = jnp.transpose(cache_fox_k[0], (0, 2, 3, 1))
    cache_vt = jnp.transpose(cache_fox_v[0], (0, 2, 3, 1))
    o_fox_s = _fox_sample(qs, sk, sv, lf_all_t, cache_kt, cache_vt, t_new=t_new)
    o_hg_s, s_state = _hgrn(hq, hk, lfh, hi, hg, state_hgrn[0].astype(F32), hnorm, nseq=nb)
    ys, sconv = _mixffn(xs, o_fox_s, o_hg_s, ga, gb, state_ffn_conv[0], w, seg_len=t_new)

    return (
        yp.reshape(bp, seq, D_MODEL),
        ys.reshape(nb, t_new, D_MODEL),
        pk.reshape(1, bp, seq, FOX_HEADS, FOX_HEAD_DIM),
        pv.reshape(1, bp, seq, FOX_HEADS, FOX_HEAD_DIM),
        plf.reshape(1, bp, seq, FOX_HEADS),
        p_state.reshape(1, bp, HG_HEADS, HG_DIM, HG_DIM),
        pconv.reshape(1, bp, 2, D_FF),
        sk.reshape(1, nb, t_new, FOX_HEADS, FOX_HEAD_DIM),
        sv.reshape(1, nb, t_new, FOX_HEADS, FOX_HEAD_DIM),
        slf.reshape(1, nb, t_new, FOX_HEADS),
        s_state.reshape(1, nb, HG_HEADS, HG_DIM, HG_DIM),
        sconv.reshape(1, nb, 2, D_FF),
    )
```

```python
import functools

import numpy as np
import jax
import jax.numpy as jnp
from jax import lax
from jax.experimental import pallas as pl
from jax.experimental.pallas import tpu as pltpu

F32 = jnp.float32
BF16 = jnp.bfloat16

D_MODEL = 1024
FOX_HEADS = 8
FOX_HEAD_DIM = 64
FOX_WIDTH = FOX_HEADS * FOX_HEAD_DIM
HG_HEADS = 4
HG_DIM = 128
HG_WIDTH = HG_HEADS * HG_DIM
D_FF = 2816
RMS_EPS = 1e-6
NEG_INF = -1e30
LOG2E = 1.4426950408889634
FOX_SKIP_NATS = 110.0
NORM_MARGIN = 1.01
IN_SIZES = [FOX_WIDTH, FOX_WIDTH, FOX_WIDTH, FOX_HEADS, HG_WIDTH, HG_WIDTH, HG_WIDTH, HG_WIDTH, D_MODEL, D_MODEL]

LANES = 128
FOX_PAD = 2 * FOX_HEAD_DIM
FOX_V_ROWS = FOX_HEAD_DIM + 16
HG_BLOCK = 64
HG_SPLIT_MAX = 60.0
VMEM_LIMIT = 56 * 1024 * 1024

PROJ_TM = 512
PROJ_ROWS = 256
FOX_TQ = 512
FOX_TK = 512
FOX_SLOTS = 3
HG_TC = 256
FFN_TM = 512
FFN_ROWS = 256
FFN_CHUNK = 256
SAMPLE_TK = 4096

_C_Q, _C_K, _C_V, _C_F = 0, 512, 1024, 1536
_C_HQ, _C_HF, _C_HI, _C_HG = 1664, 2176, 2688, 3200
_C_GA, _C_GB, _C_END = 3712, 4736, 5760


def _split3(x):
    hi = x.astype(BF16)
    r = x - hi.astype(F32)
    mid = r.astype(BF16)
    lo = (r - mid.astype(F32)).astype(BF16)
    return hi, mid, lo


def _sum_by_01_matrix(mat01, x):
    cat = jnp.concatenate(_split3(x), axis=1)
    y = jnp.dot(mat01, cat, preferred_element_type=F32)
    return y[:, :LANES] + y[:, LANES:2 * LANES] + y[:, 2 * LANES:]


def _rms_scale(x):
    return x * lax.rsqrt(jnp.mean(x * x, axis=-1, keepdims=True) + RMS_EPS)


def _log_sigmoid(x):
    return jnp.minimum(x, 0.0) - jnp.log1p(jnp.exp(-jnp.abs(x)))


def _sigmoid(x):
    return 1.0 / (1.0 + jnp.exp(-x))


def _proj_kernel(*refs, fold):
    x_ref, g_ref, w_ref = refs[:3]
    carry_ref = refs[-1]
    tm = x_ref.shape[0]
    groups = [slice(r, r + PROJ_ROWS) for r in range(0, tm, PROJ_ROWS)] if tm > PROJ_ROWS else [slice(0, tm)]
    if fold:
        @pl.when(pl.program_id(0) == 0)
        def _():
            carry_ref[...] = jnp.zeros_like(carry_ref)

    def project(rows):
        h = (_rms_scale(x_ref[rows, :]) * g_ref[...]).astype(BF16)
        return jnp.dot(h, w_ref[...], preferred_element_type=F32)

    projected = [project(rows) for rows in groups]
    for g, (rows, z) in enumerate(zip(groups, projected)):
        _proj_tail(refs, fold, g, rows, z)


def _proj_tail(refs, fold, g, rows, z):
    if fold:
        (x_ref, g_ref, w_ref, fb_ref, lbl_ref, tri_ref, pq_ref, pk_ref, cq_ref, ck_ref, cv_ref, seg_ref,
         qh_ref, kh_ref, vh_ref, stat_ref, kout_ref, vout_ref, lf_ref, hq_ref, hk_ref, lfh_ref, hi_ref, hg_ref,
         ga_ref, gb_ref, carry_ref) = refs
    else:
        (x_ref, g_ref, w_ref, fb_ref, lbl_ref,
         qs_ref, kout_ref, vout_ref, lf_ref, hq_ref, hk_ref, lfh_ref, hi_ref, hg_ref,
         ga_ref, gb_ref) = refs

    zq = z[:, _C_Q:_C_K] * (FOX_HEAD_DIM ** -0.5)
    zk = z[:, _C_K:_C_V]
    zv = z[:, _C_V:_C_F]
    kout_ref[rows, :] = zk
    vout_ref[rows, :] = zv
    logf = _log_sigmoid(z[:, _C_F:_C_HQ] + fb_ref[...])
    lf_ref[rows, :] = logf[:, :FOX_HEADS]

    l0 = lbl_ref[0:1, :]
    l1 = lbl_ref[1:2, :]
    lmax = jnp.maximum(l0, l1)
    e0 = jnp.exp(l0 - lmax)
    lb = e0 / (e0 + jnp.exp(l1 - lmax))
    f = lb + (1.0 - lb) * _sigmoid(z[:, _C_HF:_C_HI])
    hq_ref[rows, :] = z[:, _C_HQ:_C_HF].astype(BF16)
    hk_ref[rows, :] = (1.0 - f).astype(BF16)
    lfh_ref[rows, :] = jnp.log(f)
    hi_ref[rows, :] = z[:, _C_HI:_C_HG].astype(BF16)
    hg_ref[rows, :] = _sigmoid(z[:, _C_HG:_C_GA]).astype(BF16)
    ga_ref[rows, :] = _sigmoid(z[:, _C_GA:_C_GB]).astype(BF16)
    gb_ref[rows, :] = _sigmoid(z[:, _C_GB:_C_END]).astype(BF16)

    if not fold:
        qs_ref[rows, :] = zq.astype(BF16)
        return

    cum = carry_ref[...] + _sum_by_01_matrix(tri_ref[...], logf)
    carry_ref[...] = cum[-1:, :]

    seg = seg_ref[...]
    qn2 = jnp.dot((zq * zq).astype(BF16), seg, preferred_element_type=F32)
    kn2 = jnp.dot((zk * zk).astype(BF16), seg, preferred_element_type=F32)
    dg = jnp.dot((zq * zk).astype(BF16), seg, preferred_element_type=F32)
    stat_ref[g, 0:1, :] = jnp.max(qn2, axis=0, keepdims=True)
    stat_ref[g, 1:2, :] = jnp.max(kn2, axis=0, keepdims=True)
    stat_ref[g, 2:3, :] = jnp.min(dg, axis=0, keepdims=True)
    stat_ref[g, 3:4, :] = cum[0:1, :]
    stat_ref[g, 4:5, :] = cum[-1:, :]
    stat_ref[g, 5:8, :] = jnp.zeros((3, LANES), F32)

    zq = zq * LOG2E
    c_hi, c_mid, c_lo = _split3(cum * LOG2E)
    lane = lax.broadcasted_iota(jnp.int32, cum.shape, 1)
    pieces = jnp.where(lane < FOX_HEADS, c_hi, jnp.where(lane < 2 * FOX_HEADS, c_mid, c_lo))
    pieces = jnp.where(lane < 3 * FOX_HEADS, pieces, jnp.zeros_like(pieces))
    ex_q = jnp.dot(pieces, pq_ref[...], preferred_element_type=F32) + cq_ref[...]
    ex_k = jnp.dot(pieces, pk_ref[...], preferred_element_type=F32) + ck_ref[...]
    ex_v = cv_ref[...]

    low = lax.broadcasted_iota(jnp.int32, (zq.shape[0], LANES), 1) < FOX_HEAD_DIM
    for src, ex, dst, transposed in ((zq, ex_q, qh_ref, True), (zk, ex_k, kh_ref, False), (zv, ex_v, vh_ref, True)):
        for c in range(FOX_WIDTH // LANES):
            pair = src[:, c * LANES:(c + 1) * LANES]
            swapped = pltpu.roll(pair, FOX_HEAD_DIM, axis=1)
            for j, data in enumerate((pair, swapped)):
                hd = 2 * c + j
                blk = jnp.where(low, data, ex[:, hd * LANES:(hd + 1) * LANES])
                if not transposed:
                    dst[hd, rows, :] = blk.astype(BF16)
                elif len(dst.shape) == 3:
                    dst[hd, :, rows] = blk.T.astype(BF16)
                else:
                    dst[hd, g] = blk.T[:dst.shape[2]].astype(BF16)


def _bias_fold_constants():
    pq = np.zeros((LANES, FOX_HEADS * LANES), np.float32)
    pk = np.zeros((LANES, FOX_HEADS * LANES), np.float32)
    cq = np.zeros((1, FOX_HEADS * LANES), np.float32)
    ck = np.zeros((1, FOX_HEADS * LANES), np.float32)
    cv = np.zeros((1, FOX_HEADS * LANES), np.float32)
    for h in range(FOX_HEADS):
        base = h * LANES + FOX_HEAD_DIM
        for p in range(3):
            pq[p * FOX_HEADS + h, base + p] = 1.0
            ck[0, base + p] = 1.0
            pk[p * FOX_HEADS + h, base + 3 + p] = -1.0
            cq[0, base + 3 + p] = 1.0
        cv[0, base] = 1.0
    return (jnp.asarray(pq, BF16), jnp.asarray(pk, BF16), jnp.asarray(cq), jnp.asarray(ck), jnp.asarray(cv))


def _const_spec(shape, single=True):
    nd = len(shape)
    if single:
        return pl.BlockSpec(shape, lambda *_: (0,) * nd, pipeline_mode=pl.Buffered(1))
    return pl.BlockSpec(shape, lambda *_: (0,) * nd)


def _proj(x, gain, w_all, fb3, lb_logits, *, fold):
    rows = x.shape[0]
    tm = min(PROJ_TM, rows)
    n = rows // tm
    row = lambda width: pl.BlockSpec((tm, width), lambda i: (i, 0))
    in_specs = [row(D_MODEL), _const_spec((1, D_MODEL)), _const_spec(w_all.shape), _const_spec((1, LANES)),
                _const_spec(lb_logits.shape)]
    args = [x, gain, w_all, fb3, lb_logits]
    common_out = [
        (jax.ShapeDtypeStruct((rows, FOX_WIDTH), F32), row(FOX_WIDTH)),
        (jax.ShapeDtypeStruct((rows, FOX_WIDTH), F32), row(FOX_WIDTH)),
        (jax.ShapeDtypeStruct((rows, FOX_HEADS), F32), row(FOX_HEADS)),
        (jax.ShapeDtypeStruct((rows, HG_WIDTH), BF16), row(HG_WIDTH)),
        (jax.ShapeDtypeStruct((rows, HG_WIDTH), BF16), row(HG_WIDTH)),
        (jax.ShapeDtypeStruct((rows, HG_WIDTH), F32), row(HG_WIDTH)),
        (jax.ShapeDtypeStruct((rows, HG_WIDTH), BF16), row(HG_WIDTH)),
        (jax.ShapeDtypeStruct((rows, HG_WIDTH), BF16), row(HG_WIDTH)),
        (jax.ShapeDtypeStruct((rows, D_MODEL), BF16), row(D_MODEL)),
        (jax.ShapeDtypeStruct((rows, D_MODEL), BF16), row(D_MODEL)),
    ]
    scratch = []
    if fold:
        grp = min(PROJ_ROWS, tm)
        per_tile = tm // grp
        tri = jnp.asarray(np.tril(np.ones((grp, grp), np.float32)), BF16)
        seg = np.zeros((FOX_WIDTH, LANES), np.float32)
        seg[np.arange(FOX_WIDTH), np.arange(FOX_WIDTH) // FOX_HEAD_DIM] = 1.0
        consts = _bias_fold_constants() + (jnp.asarray(seg, BF16),)
        in_specs += [_const_spec(tri.shape)] + [_const_spec(c.shape) for c in consts]
        args += [tri, *consts]
        head_major = (jax.ShapeDtypeStruct((FOX_HEADS, rows, FOX_PAD), BF16),
                      pl.BlockSpec((FOX_HEADS, tm, FOX_PAD), lambda i: (0, i, 0)))
        head_major_t = lambda depth: (jax.ShapeDtypeStruct((FOX_HEADS, depth, rows), BF16),
                                      pl.BlockSpec((FOX_HEADS, depth, tm), lambda i: (0, 0, i)))
        stats = (jax.ShapeDtypeStruct((n * per_tile, 8, LANES), F32),
                 pl.BlockSpec((per_tile, 8, LANES), lambda i: (i, 0, 0)))
        v_tiles = (jax.ShapeDtypeStruct((FOX_HEADS, n * per_tile, FOX_V_ROWS, grp), BF16),
                   pl.BlockSpec((FOX_HEADS, per_tile, FOX_V_ROWS, grp), lambda i: (0, i, 0, 0)))
        outs = [head_major_t(FOX_PAD), head_major, v_tiles, stats] + common_out
        scratch = [pltpu.VMEM((1, LANES), F32)]
    else:
        outs = [(jax.ShapeDtypeStruct((rows, FOX_WIDTH), BF16), row(FOX_WIDTH))] + common_out
    return pl.pallas_call(
        functools.partial(_proj_kernel, fold=fold),
        grid=(n,),
        in_specs=in_specs,
        out_specs=[o[1] for o in outs],
        out_shape=[o[0] for o in outs],
        scratch_shapes=scratch,
        compiler_params=pltpu.CompilerParams(dimension_semantics=("arbitrary",), vmem_limit_bytes=VMEM_LIMIT),
        name="proj_fold" if fold else "proj",
    )(*args)


def _fox_kernel(ks_ref, nact_ref, order_ref, qn_ref, kn_ref, cl_ref, qt_ref, k_hbm, vt_hbm, o_ref,
                kbuf, vbuf, sem, slot0_ref, m_ref, acc_ref, *, nq):
    q_blk = pl.program_id(0)
    k_first = ks_ref[q_blk]
    tq = qt_ref.shape[2]
    n_slots, tk = kbuf.shape[0], kbuf.shape[2]
    tiles, tile = vbuf.shape[2], vbuf.shape[4]
    table = lambda ref, blk, h: lax.bitcast_convert_type(ref[blk * FOX_HEADS + h], F32)

    def successor(q, k):
        at_diagonal = k == q
        return (jnp.where(q >= nq, nq, jnp.where(at_diagonal, q + 1, q)),
                jnp.where(at_diagonal, ks_ref[jnp.minimum(q + 1, nq - 1)], k + 1))

    def for_swept_heads(q, k, slot, action):
        n_swept = nact_ref[q * nq + k]
        rows = pl.ds(pl.multiple_of(k * tk, tk), tk)
        for j in range(0, FOX_HEADS, 2):
            @pl.when(j < n_swept)
            def _():
                for h in (order_ref[q * FOX_HEADS + j], order_ref[q * FOX_HEADS + j + 1]):
                    action(pltpu.make_async_copy(k_hbm.at[h, rows, :], kbuf.at[slot, h], sem.at[0, slot]))
                    action(pltpu.make_async_copy(vt_hbm.at[h, pl.ds(k * tiles, tiles)], vbuf.at[slot, h],
                                                 sem.at[1, slot]))

    def start(q, k, slot):
        @pl.when(q < nq)
        def _():
            for_swept_heads(q, k, slot, lambda c: c.start())

    def wait(q, k, slot):
        for_swept_heads(q, k, slot, lambda c: c.wait())

    lead = n_slots - 1

    def request_ahead(k_blk, slot):
        q, k = q_blk, k_blk
        for _ in range(lead):
            q, k = successor(q, k)
        start(q, k, lax.rem(slot + lead, n_slots))

    @pl.when(q_blk == 0)
    def _():
        slot0_ref[0] = 0
        q, k = q_blk, k_first
        for ahead in range(lead):
            start(q, k, ahead)
            q, k = successor(q, k)

    slot0 = slot0_ref[0]
    slot_of = lambda k_blk: lax.rem(slot0 + (k_blk - k_first), n_slots)
    m_ref[...] = jnp.full_like(m_ref, NEG_INF)
    acc_ref[...] = jnp.zeros_like(acc_ref)

    def logits(h, slot):
        return jnp.dot(kbuf[slot, h], qt_ref[h], preferred_element_type=F32)

    def softmax_update(h, s, k_blk, slot, masked):
        if masked:
            key = lax.broadcasted_iota(jnp.int32, (tk, tq), 0)
            qry = lax.broadcasted_iota(jnp.int32, (tk, tq), 1)
            s = jnp.where(key <= qry, s, NEG_INF)
        bound = table(qn_ref, q_blk, h) * table(kn_ref, k_blk, h)
        if not masked:
            cum_q = jnp.sum(qt_ref[h, FOX_HEAD_DIM:FOX_HEAD_DIM + 3, :].astype(F32), axis=0, keepdims=True)
            bound = cum_q + (bound - table(cl_ref, k_blk, h))
        m_prev = m_ref[h]
        m_new = jnp.maximum(m_prev, bound)
        alpha = jnp.exp2(m_prev - m_new)
        p = jnp.exp2(s - m_new).astype(BF16)
        pv = sum(jnp.dot(vbuf[slot, h, j], p[j * tile:(j + 1) * tile], preferred_element_type=F32)
                 for j in range(tiles))
        acc_ref[h] = alpha * acc_ref[h] + pv
        m_ref[h] = m_new

    def sweep(heads, k_blk, slot, masked):
        s_next = logits(heads[0], slot)
        for j, h in enumerate(heads):
            s = s_next
            if j + 1 < len(heads):
                s_next = logits(heads[j + 1], slot)
            softmax_update(h, s, k_blk, slot, masked)

    def past_block(k_blk, carry):
        slot = slot_of(k_blk)
        wait(q_blk, k_blk, slot)
        request_ahead(k_blk, slot)
        n_active = nact_ref[q_blk * nq + k_blk]
        for count in range(2, FOX_HEADS + 1, 2):
            @pl.when(n_active == count)
            def _():
                sweep([order_ref[q_blk * FOX_HEADS + j] for j in range(count)], k_blk, slot, False)
        return carry

    lax.fori_loop(k_first, q_blk, past_block, 0)

    slot = slot_of(q_blk)
    wait(q_blk, q_blk, slot)
    request_ahead(q_blk, slot)
    slot0_ref[0] = lax.rem(slot + 1, n_slots)

    sweep(list(range(FOX_HEADS)), q_blk, slot, True)
    for c in range(FOX_WIDTH // LANES):
        halves = []
        for hd in (2 * c, 2 * c + 1):
            acc = acc_ref[hd]
            halves.append(acc[:FOX_HEAD_DIM, :] / acc[FOX_HEAD_DIM:FOX_HEAD_DIM + 1, :])
        o_ref[:, c * LANES:(c + 1) * LANES] = jnp.concatenate(halves, axis=0).T.astype(o_ref.dtype)


def _fox_schedule(stats, nq):
    st = stats[:, :5, :FOX_HEADS].reshape(nq, -1, 5, FOX_HEADS)
    qn = jnp.sqrt(jnp.max(st[:, :, 0], axis=1)) * NORM_MARGIN
    kn = jnp.sqrt(jnp.max(st[:, :, 1], axis=1)) * NORM_MARGIN
    dmin = jnp.min(st[:, :, 2], axis=1)
    c_first = st[:, 0, 3]
    c_last = st[:, -1, 4]
    bound = qn[:, None] * kn[None, :] + (c_first - dmin)[:, None] - c_last[None, :]
    blk = jnp.arange(nq, dtype=jnp.int32)
    drop = (bound < -FOX_SKIP_NATS) & (blk[None, :] < blk[:, None])[:, :, None]
    prefix = jnp.cumsum(jnp.logical_not(drop).astype(jnp.int32), axis=1) == 0
    kstart_h = jnp.sum(prefix.astype(jnp.int32), axis=1)
    kstart = jnp.min(kstart_h, axis=1)
    hd = jnp.arange(FOX_HEADS, dtype=jnp.int32)
    before = (kstart_h[:, None, :] < kstart_h[:, :, None]) | (
        (kstart_h[:, None, :] == kstart_h[:, :, None]) & (hd[None, None, :] < hd[None, :, None]))
    rank = jnp.sum(before.astype(jnp.int32), axis=2)
    order = jnp.sum(jnp.where(rank[:, None, :] == hd[None, :, None], hd[None, None, :], 0), axis=2)
    n_active = jnp.sum((kstart_h[:, None, :] <= blk[None, :, None]).astype(jnp.int32), axis=2)
    n_active = jnp.minimum((n_active + 1) // 2 * 2, FOX_HEADS)
    as_i32 = lambda a: a.astype(jnp.int32).reshape(-1)
    bits = lambda a: lax.bitcast_convert_type(a.astype(F32), jnp.int32).reshape(-1)
    return (as_i32(kstart), as_i32(n_active), as_i32(order), bits(qn * LOG2E), bits(kn), bits(c_last * LOG2E))


def _fox_prompt(qt, kh, vt, stats):
    seq = kh.shape[1]
    assert FOX_TQ == FOX_TK
    nq = seq // FOX_TQ
    tile = vt.shape[3]
    tables = _fox_schedule(stats, nq)
    grid_spec = pltpu.PrefetchScalarGridSpec(
        num_scalar_prefetch=len(tables),
        grid=(nq,),
        in_specs=[
            pl.BlockSpec((FOX_HEADS, FOX_PAD, FOX_TQ), lambda q, *_: (0, 0, q)),
            pl.BlockSpec(memory_space=pl.ANY),
            pl.BlockSpec(memory_space=pl.ANY),
        ],
        out_specs=pl.BlockSpec((FOX_TQ, FOX_WIDTH), lambda q, *_: (q, 0)),
        scratch_shapes=[pltpu.VMEM((FOX_SLOTS, FOX_HEADS, FOX_TK, FOX_PAD), BF16),
                        pltpu.VMEM((FOX_SLOTS, FOX_HEADS, FOX_TK // tile, FOX_V_ROWS, tile), BF16),
                        pltpu.SemaphoreType.DMA((2, FOX_SLOTS)),
                        pltpu.SMEM((1,), jnp.int32),
                        pltpu.VMEM((FOX_HEADS, 1, FOX_TQ), F32),
                        pltpu.VMEM((FOX_HEADS, FOX_V_ROWS, FOX_TQ), F32)],
    )
    return pl.pallas_call(
        functools.partial(_fox_kernel, nq=nq),
        grid_spec=grid_spec,
        out_shape=jax.ShapeDtypeStruct((seq, FOX_WIDTH), BF16),
        compiler_params=pltpu.CompilerParams(dimension_semantics=("arbitrary",), vmem_limit_bytes=VMEM_LIMIT),
        name="fox_prompt",
    )(*tables, qt, kh, vt)


def _lane_cumsum(x):
    n = x.shape[1]
    lane = lax.broadcasted_iota(jnp.int32, x.shape, 1)
    shift = 1
    while shift < n:
        x = x + jnp.where(lane >= shift, pltpu.roll(x, shift, axis=1), 0.0)
        shift *= 2
    return x


def _fox_sample_kernel(q_ref, kn_ref, vn_ref, lft_ref, ck_ref, cv_ref, o_ref,
                       cq_ref, cum_ref, m_ref, l_ref, acc_ref, *, n_past_chunks, tk, t_new):
    c = pl.program_id(1)
    rows = FOX_HEADS * t_new
    heads = [slice(h * FOX_HEAD_DIM, (h + 1) * FOX_HEAD_DIM) for h in range(FOX_HEADS)]
    head_rows = [slice(h * t_new, (h + 1) * t_new) for h in range(FOX_HEADS)]

    @pl.when(c == 0)
    def _():
        cum = _lane_cumsum(lft_ref[0])
        for j in range(n_past_chunks):
            cum_ref[j] = cum[:, j * tk:(j + 1) * tk]
        new_cum = cum[:, n_past_chunks * tk:n_past_chunks * tk + LANES]
        cum_ref[n_past_chunks, :, :LANES] = new_cum
        new_cum_t = jnp.concatenate([new_cum] * (LANES // FOX_HEADS), axis=0).T
        for h in range(FOX_HEADS):
            cq_ref[head_rows[h], :] = jnp.broadcast_to(new_cum_t[:t_new, h:h + 1], (t_new, LANES))
        m_ref[...] = jnp.full_like(m_ref, NEG_INF)
        l_ref[...] = jnp.zeros_like(l_ref)
        acc_ref[...] = jnp.zeros_like(acc_ref)

    nt = (((1,), (1,)), ((), ()))

    def update(qk, pv_of, ck_rows, mask):
        q = q_ref[...]
        s = jnp.concatenate([qk(q[:, heads[h]], h) for h in range(FOX_HEADS)], axis=0)
        s = s + (cq_ref[:, :1] - ck_rows)
        if mask is not None:
            s = jnp.where(mask, s, NEG_INF)
        m_prev = m_ref[...]
        m_new = jnp.maximum(m_prev, jnp.max(s, axis=1, keepdims=True))
        alpha = jnp.exp(m_prev - m_new)
        p = jnp.exp(s - m_new[:, :1])
        l_ref[...] = alpha * l_ref[...] + jnp.sum(p, axis=1, keepdims=True)
        p = p.astype(BF16)
        pv = jnp.concatenate([pv_of(p[head_rows[h]], h) for h in range(FOX_HEADS)], axis=0)
        acc_ref[...] = alpha[:, :FOX_HEAD_DIM] * acc_ref[...] + pv
        m_ref[...] = m_new

    def expand_rows(x, width):
        return jnp.concatenate([jnp.broadcast_to(x[h:h + 1, :], (t_new, width)) for h in range(FOX_HEADS)], axis=0)

    @pl.when(c < n_past_chunks)
    def _():
        update(lambda qh, h: jnp.dot(qh, ck_ref[0, h].astype(BF16), preferred_element_type=F32),
               lambda ph, h: lax.dot_general(ph, cv_ref[0, h].astype(BF16), nt, preferred_element_type=F32),
               expand_rows(cum_ref[c], tk), None)

    @pl.when(c == n_past_chunks)
    def _():
        ck_rows = expand_rows(cum_ref[n_past_chunks, :, :LANES], LANES)[:, :t_new]
        rowt = lax.broadcasted_iota(jnp.int32, (rows, t_new), 0) % t_new
        coli = lax.broadcasted_iota(jnp.int32, (rows, t_new), 1)
        update(lambda qh, h: lax.dot_general(qh, kn_ref[:, heads[h]].astype(BF16), nt, preferred_element_type=F32),
               lambda ph, h: jnp.dot(ph, vn_ref[:, heads[h]].astype(BF16), preferred_element_type=F32),
               ck_rows, coli <= rowt)
        out = acc_ref[...] / l_ref[:, :FOX_HEAD_DIM]
        o_ref[...] = jnp.concatenate([out[head_rows[h]] for h in range(FOX_HEADS)], axis=1).astype(o_ref.dtype)


def _fox_sample(qs, k_new, v_new, lf_all_t, cache_k, cache_v, *, t_new):
    nb, past = cache_k.shape[0], cache_k.shape[3]
    tk = SAMPLE_TK
    npc = past // tk
    rows = FOX_HEADS * t_new
    last = npc - 1
    kern = functools.partial(_fox_sample_kernel, n_past_chunks=npc, tk=tk, t_new=t_new)
    cache_spec = pl.BlockSpec((1, FOX_HEADS, FOX_HEAD_DIM, tk), lambda b, c: (b, 0, 0, jnp.minimum(c, last)))
    return pl.pallas_call(
        kern,
        grid=(nb, npc + 1),
        in_specs=[
            pl.BlockSpec((t_new, FOX_WIDTH), lambda b, c: (b, 0)),
            pl.BlockSpec((t_new, FOX_WIDTH), lambda b, c: (b, 0)),
            pl.BlockSpec((t_new, FOX_WIDTH), lambda b, c: (b, 0)),
            pl.BlockSpec((1, FOX_HEADS, past + LANES), lambda b, c: (b, 0, 0)),
            cache_spec, cache_spec,
        ],
        out_specs=pl.BlockSpec((t_new, FOX_WIDTH), lambda b, c: (b, 0)),
        out_shape=jax.ShapeDtypeStruct((nb * t_new, FOX_WIDTH), BF16),
        scratch_shapes=[
            pltpu.VMEM((rows, LANES), F32),
            pltpu.VMEM((npc + 1, FOX_HEADS, tk), F32),
            pltpu.VMEM((rows, LANES), F32),
            pltpu.VMEM((rows, LANES), F32),
            pltpu.VMEM((rows, FOX_HEAD_DIM), F32),
        ],
        compiler_params=pltpu.CompilerParams(dimension_semantics=("arbitrary", "arbitrary"),
                                             vmem_limit_bytes=VMEM_LIMIT),
        name="fox_sample",
    )(qs, k_new, v_new, lf_all_t, cache_k, cache_v)


def _hgrn_kernel(hq_ref, hk_ref, lfh_ref, hi_ref, hg_ref, s0_ref, norm_ref, tri_ref,
                 o_ref, sout_ref, st_ref, qq_ref, kk_ref, eb_ref, b_ref, oin_ref, od_ref):
    t = pl.program_id(1)
    tc = hq_ref.shape[0]
    blk = min(HG_BLOCK, tc)
    nblk = tc // blk
    heads = [slice(h * HG_DIM, (h + 1) * HG_DIM) for h in range(HG_HEADS)]

    @pl.when(t == 0)
    def _():
        for h in range(HG_HEADS):
            st_ref[h] = s0_ref[0, h].T

    def at_block_row(x, r):
        x3 = x.reshape(nblk, blk, HG_DIM)
        return jnp.broadcast_to(x3[:, r:r + 1, :], x3.shape).reshape(tc, HG_DIM)

    def rel_to_middle(b):
        return b - at_block_row(b, blk // 2 - 1)

    worst = jnp.zeros((), F32)
    for sl in heads:
        b = _sum_by_01_matrix(tri_ref[...], lfh_ref[:, sl])
        eb = jnp.exp(b)
        qq_ref[:, sl] = (hq_ref[:, sl].astype(F32) * eb).astype(BF16)
        kk_ref[:, sl] = (hk_ref[:, sl].astype(F32) * jnp.exp(at_block_row(b, blk - 1) - b)).astype(BF16)
        eb_ref[:, sl] = eb
        b_ref[:, sl] = b
        worst = jnp.maximum(worst, jnp.max(jnp.abs(rel_to_middle(b))))

    def carried_state_and_output():
        for j in range(nblk):
            rows = slice(j * blk, (j + 1) * blk)
            for h, sl in enumerate(heads):
                st = st_ref[h]
                oin_ref[rows, sl] = lax.dot_general(qq_ref[rows, sl], st.astype(BF16), (((1,), (1,)), ((), ())),
                                                    preferred_element_type=F32)
                upd = lax.dot_general(hi_ref[rows, sl], kk_ref[rows, sl], (((0,), (0,)), ((), ())),
                                      preferred_element_type=F32)
                st_ref[h] = st * eb_ref[(j + 1) * blk - 1:(j + 1) * blk, sl] + upd
        for sl in heads:
            o = oin_ref[:, sl] + od_ref[:, sl]
            y = (_rms_scale(o) * norm_ref[...]) * hg_ref[:, sl].astype(F32)
            o_ref[:, sl] = y.astype(o_ref.dtype)

    splittable = worst <= HG_SPLIT_MAX

    @pl.when(splittable)
    def _():
        row = lax.broadcasted_iota(jnp.int32, (tc, tc), 0)
        col = lax.broadcasted_iota(jnp.int32, (tc, tc), 1)
        pair_in_block = (row // blk == col // blk) & (col <= row)
        for sl in heads:
            b_rel = rel_to_middle(b_ref[:, sl])
            qs = (hq_ref[:, sl].astype(F32) * jnp.exp(b_rel)).astype(BF16)
            ks = (hk_ref[:, sl].astype(F32) * jnp.exp(-b_rel)).astype(BF16)
            a = lax.dot_general(qs, ks, (((1,), (1,)), ((), ())), preferred_element_type=F32)
            a = jnp.where(pair_in_block, a, 0.0).astype(BF16)
            od_ref[:, sl] = jnp.dot(a, hi_ref[:, sl], preferred_element_type=F32)
        carried_state_and_output()

    @pl.when(jnp.logical_not(splittable))
    def _():
        row_in_blk = lax.broadcasted_iota(jnp.int32, (tc, HG_DIM), 0) % blk
        for sl in heads:
            q = hq_ref[:, sl].astype(F32)
            k = hk_ref[:, sl].astype(F32)
            v = hi_ref[:, sl].astype(F32)
            b = b_ref[:, sl]

            def lag_step(lag, od):
                k_l = pltpu.roll(k, lag, axis=0)
                b_l = pltpu.roll(b, lag, axis=0)
                v_l = pltpu.roll(v, lag, axis=0)
                w = q * k_l * jnp.exp(jnp.minimum(b - b_l, 0.0))
                w = jnp.where(row_in_blk >= lag, w, 0.0)
                return od + jnp.sum(w, axis=1, keepdims=True) * v_l

            od_ref[:, sl] = lax.fori_loop(1, blk, lag_step, jnp.sum(q * k, axis=1, keepdims=True) * v)
        carried_state_and_output()

    @pl.when(t == pl.num_programs(1) - 1)
    def _():
        for h in range(HG_HEADS):
            sout_ref[0, h] = st_ref[h].T


def _hgrn_cumsum_matrix(n, blk):
    t = np.arange(n)[:, None]
    s = np.arange(n)[None, :]
    return jnp.asarray((((t // blk) == (s // blk)) & (s <= t)).astype(np.float32), BF16)


def _hgrn(hq, hk, lfh, hi, hg, s0, norm, *, nseq):
    rows = hq.shape[0]
    t_len = rows // nseq
    tc = min(HG_TC, t_len)
    nt = t_len // tc
    blk = min(HG_BLOCK, tc)
    tri = _hgrn_cumsum_matrix(tc, blk)
    row = pl.BlockSpec((tc, HG_WIDTH), lambda b, t: (b * nt + t, 0))
    state = pl.BlockSpec((1, HG_HEADS, HG_DIM, HG_DIM), lambda b, t: (b, 0, 0, 0))
    return pl.pallas_call(
        _hgrn_kernel,
        grid=(nseq, nt),
        in_specs=[row, row, row, row, row, state,
                  pl.BlockSpec((1, HG_DIM), lambda b, t: (0, 0)),
                  pl.BlockSpec((tc, tc), lambda b, t: (0, 0))],
        out_specs=[row, state],
        out_shape=[jax.ShapeDtypeStruct((rows, HG_WIDTH), BF16),
                   jax.ShapeDtypeStruct((nseq, HG_HEADS, HG_DIM, HG_DIM), F32)],
        scratch_shapes=[
            pltpu.VMEM((HG_HEADS, HG_DIM, HG_DIM), F32),
            pltpu.VMEM((tc, HG_WIDTH), BF16),
            pltpu.VMEM((tc, HG_WIDTH), BF16),
            pltpu.VMEM((tc, HG_WIDTH), F32),
            pltpu.VMEM((tc, HG_WIDTH), F32),
            pltpu.VMEM((tc, HG_WIDTH), F32),
            pltpu.VMEM((tc, HG_WIDTH), F32),
        ],
        compiler_params=pltpu.CompilerParams(dimension_semantics=("arbitrary", "arbitrary"),
                                             vmem_limit_bytes=VMEM_LIMIT),
        name="hgrn",
    )(hq, hk, lfh, hi, hg, s0, norm, tri)


def _mixffn_kernel(x_ref, of_ref, oh_ref, ga_ref, gb_ref, hist_ref,
                   wbf_ref, wbh_ref, wout_ref, wup_ref, wdn_ref,
                   npost_ref, npre2_ref, npost2_ref, cw_ref, cb_ref,
                   y_ref, conv_ref, tail_ref, *, seg_len):
    tm = x_ref.shape[0]
    n_chunks = D_FF // FFN_CHUNK
    carried = seg_len >= tm
    if carried:
        @pl.when(pl.program_id(0) == 0)
        def _():
            tail_ref[...] = hist_ref[0]

    groups = [slice(r, r + FFN_ROWS) for r in range(0, tm, FFN_ROWS)] if tm > FFN_ROWS else [slice(0, tm)]

    def mix_stage(rows):
        br_f = jnp.dot(of_ref[rows, :], wbf_ref[...], preferred_element_type=F32)
        br_h = jnp.dot(oh_ref[rows, :], wbh_ref[...], preferred_element_type=F32)
        merged = ga_ref[rows, :].astype(F32) * br_f + gb_ref[rows, :].astype(F32) * br_h
        mix = jnp.dot(merged.astype(BF16), wout_ref[...], preferred_element_type=F32)
        x1 = x_ref[rows, :] + _rms_scale(mix) * npost_ref[...]
        return x1, (_rms_scale(x1) * npre2_ref[...]).astype(BF16)

    def ffn_stage(rows, x1, h2):
        n_rows = rows.stop - rows.start
        def up_chunk(j):
            w = jnp.concatenate([wup_ref[:, j * FFN_CHUNK:(j + 1) * FFN_CHUNK],
                                 wup_ref[:, D_FF + j * FFN_CHUNK:D_FF + (j + 1) * FFN_CHUNK]], axis=1)
            return jnp.dot(h2, w, preferred_element_type=F32)

        rowi = lax.broadcasted_iota(jnp.int32, (n_rows, FFN_CHUNK), 0)
        ff = None
        up_next = up_chunk(0)
        for j in range(n_chunks):
            cols = slice(j * FFN_CHUNK, (j + 1) * FFN_CHUNK)
            up = up_next
            if j + 1 < n_chunks:
                up_next = up_chunk(j + 1)
            a = up[:, :FFN_CHUNK]
            g = up[:, FFN_CHUNK:]
            prev1 = pltpu.roll(a, 1, axis=0)
            prev2 = pltpu.roll(a, 2, axis=0)
            if carried:
                t0 = tail_ref[0:1, cols]
                t1 = tail_ref[1:2, cols]
                prev1 = jnp.where(rowi == 0, t1, prev1)
                prev2 = jnp.where(rowi == 0, t0, jnp.where(rowi == 1, t1, prev2))
                tail_ref[:, cols] = a[n_rows - 2:, :]
                conv_ref[0, :, cols] = a[n_rows - 2:, :]
            else:
                for s in range(n_rows // seg_len):
                    h0 = hist_ref[s, 0:1, cols]
                    h1 = hist_ref[s, 1:2, cols]
                    prev1 = jnp.where(rowi == s * seg_len, h1, prev1)
                    prev2 = jnp.where(rowi == s * seg_len, h0, jnp.where(rowi == s * seg_len + 1, h1, prev2))
                    conv_ref[s, :, cols] = a[(s + 1) * seg_len - 2:(s + 1) * seg_len, :]
            c = cb_ref[:, cols] + cw_ref[0:1, cols] * prev2 + cw_ref[1:2, cols] * prev1 + cw_ref[2:3, cols] * a
            act = (jax.nn.gelu(c, approximate=True) * g).astype(BF16)
            part = jnp.dot(act, wdn_ref[cols, :], preferred_element_type=F32)
            ff = part if ff is None else ff + part
        y_ref[rows, :] = x1 + _rms_scale(ff) * npost2_ref[...]

    mixed = [mix_stage(rows) for rows in groups]
    for rows, (x1, h2) in zip(groups, mixed):
        ffn_stage(rows, x1, h2)


def _mixffn(x, o_fox, o_hg, ga, gb, hist, w, *, seg_len):
    rows = x.shape[0]
    tm = min(FFN_TM, rows)
    n = rows // tm
    nseg = hist.shape[0]
    row = lambda width: pl.BlockSpec((tm, width), lambda i: (i, 0))
    weights = [w["bf"], w["bh"], w["out"], w["up"], w["down"]]
    smalls = [w["npost"], w["npre2"], w["npost2"], w["conv_w"], w["conv_b"]]
    hist_spec = pl.BlockSpec(hist.shape, lambda i: (0, 0, 0))
    scratch = [pltpu.VMEM((2, D_FF), F32)]
    return pl.pallas_call(
        functools.partial(_mixffn_kernel, seg_len=seg_len),
        grid=(n,),
        in_specs=[row(D_MODEL), row(FOX_WIDTH), row(HG_WIDTH), row(D_MODEL), row(D_MODEL), hist_spec]
                 + [_const_spec(a.shape) for a in weights] + [_const_spec(a.shape) for a in smalls],
        out_specs=[row(D_MODEL), pl.BlockSpec((nseg, 2, D_FF), lambda i: (0, 0, 0))],
        out_shape=[jax.ShapeDtypeStruct((rows, D_MODEL), F32), jax.ShapeDtypeStruct((nseg, 2, D_FF), F32)],
        scratch_shapes=scratch,
        compiler_params=pltpu.CompilerParams(dimension_semantics=("arbitrary",), vmem_limit_bytes=VMEM_LIMIT),
        name="mixffn",
    )(x, o_fox, o_hg, ga, gb, hist, *weights, *smalls)


def _prep_w_in(w_in, fox_f_bias):
    offs = np.cumsum([0] + IN_SIZES)
    w_in = w_in.astype(BF16)
    seg = [w_in[:, int(offs[i]):int(offs[i + 1])] for i in range(len(IN_SIZES))]
    pad = jnp.zeros((D_MODEL, LANES - 3 * FOX_HEADS), BF16)
    f3 = jnp.concatenate([seg[3], seg[3], seg[3], pad], axis=1)
    w_all = jnp.concatenate(seg[:3] + [f3] + seg[4:], axis=1)
    fb = fox_f_bias.astype(F32)
    fb3 = jnp.concatenate([fb, fb, fb, jnp.zeros((LANES - 3 * FOX_HEADS,), F32)]).reshape(1, LANES)
    return w_all, fb3


def kernel(x_prompt, x_sample, cache_fox_k, cache_fox_v, cache_fox_logf, state_hgrn, state_ffn_conv, norm_mix_pre, norm_mix_post, w_in, fox_f_bias, hgrn_lb_logits, hgrn_norm, w_branch_fox, w_branch_hgrn, w_out, norm_ffn_pre, norm_ffn_post, w_up, ffn_conv_w, ffn_conv_b, w_down):
    depth = w_in.shape[0]
    assert depth == 1 and hgrn_lb_logits.shape[0] == 2
    bp, seq, _ = x_prompt.shape
    assert bp == 1
    nb, t_new, _ = x_sample.shape
    past = cache_fox_k.shape[2]

    w_all, fb3 = _prep_w_in(w_in[0], fox_f_bias[0])
    g_pre = norm_mix_pre[0].reshape(1, D_MODEL)
    lbl = hgrn_lb_logits.astype(F32)
    hnorm = hgrn_norm[0].astype(F32).reshape(1, HG_DIM)
    w = {
        "bf": w_branch_fox[0].astype(BF16), "bh": w_branch_hgrn[0].astype(BF16), "out": w_out[0].astype(BF16),
        "up": w_up[0].astype(BF16), "down": w_down[0].astype(BF16),
        "npost": norm_mix_post[0].reshape(1, D_MODEL), "npre2": norm_ffn_pre[0].reshape(1, D_MODEL),
        "npost2": norm_ffn_post[0].reshape(1, D_MODEL),
        "conv_w": ffn_conv_w[0], "conv_b": ffn_conv_b[0].reshape(1, D_FF),
    }

    xp = x_prompt.reshape(seq, D_MODEL)
    (qt, kh, vt, stats, pk, pv, plf, hq, hk, lfh, hi, hg, ga, gb) = _proj(xp, g_pre, w_all, fb3, lbl, fold=True)
    o_fox = _fox_prompt(qt, kh, vt, stats)
    s0 = jnp.zeros((1, HG_HEADS, HG_DIM, HG_DIM), F32)
    o_hg, p_state = _hgrn(hq, hk, lfh, hi, hg, s0, hnorm, nseq=1)
    hist0 = jnp.zeros((1, 2, D_FF), F32)
    yp, pconv = _mixffn(xp, o_fox, o_hg, ga, gb, hist0, w, seg_len=seq)

    xs = x_sample.reshape(nb * t_new, D_MODEL)
    (qs, sk, sv, slf, hq, hk, lfh, hi, hg, ga, gb) = _proj(xs, g_pre, w_all, fb3, lbl, fold=False)
    lf_all_t = jnp.concatenate([
        jnp.swapaxes(cache_fox_logf[0].astype(F32), 1, 2),
        jnp.swapaxes(slf.reshape(nb, t_new, FOX_HEADS), 1, 2),
        jnp.zeros((nb, FOX_HEADS, LANES - t_new), F32)], axis=2)
    cache_kt = jnp.transpose(cache_fox_k[0], (0, 2, 3, 1))
    cache_vt = jnp.transpose(cache_fox_v[0], (0, 2, 3, 1))
    o_fox_s = _fox_sample(qs, sk, sv, lf_all_t, cache_kt, cache_vt, t_new=t_new)
    o_hg_s, s_state = _hgrn(hq, hk, lfh, hi, hg, state_hgrn[0].astype(F32), hnorm, nseq=nb)
    ys, sconv = _mixffn(xs, o_fox_s, o_hg_s, ga, gb, state_ffn_conv[0], w, seg_len=t_new)

    return (
        yp.reshape(bp, seq, D_MODEL),
        ys.reshape(nb, t_new, D_MODEL),
        pk.reshape(1, bp, seq, FOX_HEADS, FOX_HEAD_DIM),
        pv.reshape(1, bp, seq, FOX_HEADS, FOX_HEAD_DIM),
        plf.reshape(1, bp, seq, FOX_HEADS),
        p_state.reshape(1, bp, HG_HEADS, HG_DIM, HG_DIM),
        pconv.reshape(1, bp, 2, D_FF),
        sk.reshape(1, nb, t_new, FOX_HEADS, FOX_HEAD_DIM),
        sv.reshape(1, nb, t_new, FOX_HEADS, FOX_HEAD_DIM),
        slf.reshape(1, nb, t_new, FOX_HEADS),
        s_state.reshape(1, nb, HG_HEADS, HG_DIM, HG_DIM),
        sconv.reshape(1, nb, 2, D_FF),
    )
```

```python
import functools

import numpy as np
import jax
import jax.numpy as jnp
from jax import lax
from jax.experimental import pallas as pl
from jax.experimental.pallas import tpu as pltpu

F32 = jnp.float32
BF16 = jnp.bfloat16

D_MODEL = 1024
FOX_HEADS = 8
FOX_HEAD_DIM = 64
FOX_WIDTH = FOX_HEADS * FOX_HEAD_DIM
HG_HEADS = 4
HG_DIM = 128
HG_WIDTH = HG_HEADS * HG_DIM
D_FF = 2816
RMS_EPS = 1e-6
NEG_INF = -1e30
LOG2E = 1.4426950408889634
FOX_SKIP_NATS = 110.0
NORM_MARGIN = 1.01
FOX_BOUND_SLACK_MAX = 48.0
IN_SIZES = [FOX_WIDTH, FOX_WIDTH, FOX_WIDTH, FOX_HEADS, HG_WIDTH, HG_WIDTH, HG_WIDTH, HG_WIDTH, D_MODEL, D_MODEL]

LANES = 128
FOX_PAD = 2 * FOX_HEAD_DIM
FOX_V_ROWS = FOX_HEAD_DIM + 16
HG_BLOCK = 64
HG_SPLIT_MAX = 60.0
VMEM_LIMIT = 56 * 1024 * 1024

PROJ_TM = 512
PROJ_ROWS = 256
FOX_TQ = 512
FOX_TK = 512
FOX_SLOTS = 3
HG_TC = 256
FFN_TM = 512
FFN_ROWS = 256
FFN_CHUNK = 256
SAMPLE_TK = 2048

_C_Q, _C_K, _C_V, _C_F = 0, 512, 1024, 1536
_C_HQ, _C_HF, _C_HI, _C_HG = 1664, 2176, 2688, 3200
_C_GA, _C_GB, _C_END = 3712, 4736, 5760


def _split3(x):
    hi = x.astype(BF16)
    r = x - hi.astype(F32)
    mid = r.astype(BF16)
    lo = (r - mid.astype(F32)).astype(BF16)
    return hi, mid, lo


def _sum_by_01_matrix(mat01, x):
    cat = jnp.concatenate(_split3(x), axis=1)
    y = jnp.dot(mat01, cat, preferred_element_type=F32)
    return y[:, :LANES] + y[:, LANES:2 * LANES] + y[:, 2 * LANES:]


def _rms_scale(x):
    return x * lax.rsqrt(jnp.mean(x * x, axis=-1, keepdims=True) + RMS_EPS)


def _log_sigmoid(x):
    return jnp.minimum(x, 0.0) - jnp.log1p(jnp.exp(-jnp.abs(x)))


def _sigmoid(x):
    return 1.0 / (1.0 + jnp.exp(-x))


def _proj_kernel(*refs, fold):
    x_ref, g_ref, w_ref = refs[:3]
    carry_ref = refs[-1]
    tm = x_ref.shape[0]
    groups = [slice(r, r + PROJ_ROWS) for r in range(0, tm, PROJ_ROWS)] if tm > PROJ_ROWS else [slice(0, tm)]
    if fold:
        @pl.when(pl.program_id(0) == 0)
        def _():
            carry_ref[...] = jnp.zeros_like(carry_ref)

    def project(rows):
        h = (_rms_scale(x_ref[rows, :]) * g_ref[...]).astype(BF16)
        return jnp.dot(h, w_ref[...], preferred_element_type=F32)

    projected = [project(rows) for rows in groups]
    for g, (rows, z) in enumerate(zip(groups, projected)):
        _proj_tail(refs, fold, g, rows, z)


def _proj_tail(refs, fold, g, rows, z):
    if fold:
        (x_ref, g_ref, w_ref, fb_ref, lbl_ref, tri_ref, pq_ref, pk_ref, cq_ref, ck_ref, cv_ref, seg_ref,
         qh_ref, kh_ref, vh_ref, stat_ref, kout_ref, vout_ref, lf_ref, hq_ref, hk_ref, lfh_ref, hi_ref, hg_ref,
         ga_ref, gb_ref, carry_ref) = refs
    else:
        (x_ref, g_ref, w_ref, fb_ref, lbl_ref,
         qs_ref, kout_ref, vout_ref, lf_ref, hq_ref, hk_ref, lfh_ref, hi_ref, hg_ref,
         ga_ref, gb_ref) = refs

    zq = z[:, _C_Q:_C_K] * (FOX_HEAD_DIM ** -0.5)
    zk = z[:, _C_K:_C_V]
    zv = z[:, _C_V:_C_F]
    kout_ref[rows, :] = zk
    vout_ref[rows, :] = zv
    logf = _log_sigmoid(z[:, _C_F:_C_HQ] + fb_ref[...])
    lf_ref[rows, :] = logf[:, :FOX_HEADS]

    l0 = lbl_ref[0:1, :]
    l1 = lbl_ref[1:2, :]
    lmax = jnp.maximum(l0, l1)
    e0 = jnp.exp(l0 - lmax)
    lb = e0 / (e0 + jnp.exp(l1 - lmax))
    f = lb + (1.0 - lb) * _sigmoid(z[:, _C_HF:_C_HI])
    hq_ref[rows, :] = z[:, _C_HQ:_C_HF].astype(BF16)
    hk_ref[rows, :] = (1.0 - f).astype(BF16)
    lfh_ref[rows, :] = jnp.log(f)
    hi_ref[rows, :] = z[:, _C_HI:_C_HG].astype(BF16)
    hg_ref[rows, :] = _sigmoid(z[:, _C_HG:_C_GA]).astype(BF16)
    ga_ref[rows, :] = _sigmoid(z[:, _C_GA:_C_GB]).astype(BF16)
    gb_ref[rows, :] = _sigmoid(z[:, _C_GB:_C_END]).astype(BF16)

    if not fold:
        qs_ref[rows, :] = zq.astype(BF16)
        return

    cum = carry_ref[...] + _sum_by_01_matrix(tri_ref[...], logf)
    carry_ref[...] = cum[-1:, :]

    seg = seg_ref[...]
    qn2 = jnp.dot((zq * zq).astype(BF16), seg, preferred_element_type=F32)
    kn2 = jnp.dot((zk * zk).astype(BF16), seg, preferred_element_type=F32)
    dg = jnp.dot((zq * zk).astype(BF16), seg, preferred_element_type=F32)
    stat_ref[g, 0:1, :] = jnp.max(qn2, axis=0, keepdims=True)
    stat_ref[g, 1:2, :] = jnp.max(kn2, axis=0, keepdims=True)
    stat_ref[g, 2:3, :] = jnp.min(dg, axis=0, keepdims=True)
    stat_ref[g, 3:4, :] = cum[0:1, :]
    stat_ref[g, 4:5, :] = cum[-1:, :]
    stat_ref[g, 5:8, :] = jnp.zeros((3, LANES), F32)

    zq = zq * LOG2E
    c_hi, c_mid, c_lo = _split3(cum * LOG2E)
    lane = lax.broadcasted_iota(jnp.int32, cum.shape, 1)
    pieces = jnp.where(lane < FOX_HEADS, c_hi, jnp.where(lane < 2 * FOX_HEADS, c_mid, c_lo))
    pieces = jnp.where(lane < 3 * FOX_HEADS, pieces, jnp.zeros_like(pieces))
    ex_q = jnp.dot(pieces, pq_ref[...], preferred_element_type=F32) + cq_ref[...]
    ex_k = jnp.dot(pieces, pk_ref[...], preferred_element_type=F32) + ck_ref[...]
    ex_v = cv_ref[...]

    low = lax.broadcasted_iota(jnp.int32, (zq.shape[0], LANES), 1) < FOX_HEAD_DIM
    for src, ex, dst, transposed in ((zq, ex_q, qh_ref, True), (zk, ex_k, kh_ref, False), (zv, ex_v, vh_ref, True)):
        for c in range(FOX_WIDTH // LANES):
            pair = src[:, c * LANES:(c + 1) * LANES]
            swapped = pltpu.roll(pair, FOX_HEAD_DIM, axis=1)
            for j, data in enumerate((pair, swapped)):
                hd = 2 * c + j
                blk = jnp.where(low, data, ex[:, hd * LANES:(hd + 1) * LANES])
                if not transposed:
                    dst[hd, rows, :] = blk.astype(BF16)
                elif len(dst.shape) == 3:
                    dst[hd, :, rows] = blk.T.astype(BF16)
                else:
                    dst[hd, g] = blk.T[:dst.shape[2]].astype(BF16)


def _bias_fold_constants():
    pq = np.zeros((LANES, FOX_HEADS * LANES), np.float32)
    pk = np.zeros((LANES, FOX_HEADS * LANES), np.float32)
    cq = np.zeros((1, FOX_HEADS * LANES), np.float32)
    ck = np.zeros((1, FOX_HEADS * LANES), np.float32)
    cv = np.zeros((1, FOX_HEADS * LANES), np.float32)
    for h in range(FOX_HEADS):
        base = h * LANES + FOX_HEAD_DIM
        for p in range(3):
            pq[p * FOX_HEADS + h, base + p] = 1.0
            ck[0, base + p] = 1.0
            pk[p * FOX_HEADS + h, base + 3 + p] = -1.0
            cq[0, base + 3 + p] = 1.0
        cv[0, base] = 1.0
    return (jnp.asarray(pq, BF16), jnp.asarray(pk, BF16), jnp.asarray(cq), jnp.asarray(ck), jnp.asarray(cv))


def _const_spec(shape, single=True):
    nd = len(shape)
    if single:
        return pl.BlockSpec(shape, lambda *_: (0,) * nd, pipeline_mode=pl.Buffered(1))
    return pl.BlockSpec(shape, lambda *_: (0,) * nd)


def _proj(x, gain, w_all, fb3, lb_logits, *, fold):
    rows = x.shape[0]
    tm = min(PROJ_TM, rows)
    n = rows // tm
    row = lambda width: pl.BlockSpec((tm, width), lambda i: (i, 0))
    in_specs = [row(D_MODEL), _const_spec((1, D_MODEL)), _const_spec(w_all.shape), _const_spec((1, LANES)),
                _const_spec(lb_logits.shape)]
    args = [x, gain, w_all, fb3, lb_logits]
    common_out = [
        (jax.ShapeDtypeStruct((rows, FOX_WIDTH), F32), row(FOX_WIDTH)),
        (jax.ShapeDtypeStruct((rows, FOX_WIDTH), F32), row(FOX_WIDTH)),
        (jax.ShapeDtypeStruct((rows, FOX_HEADS), F32), row(FOX_HEADS)),
        (jax.ShapeDtypeStruct((rows, HG_WIDTH), BF16), row(HG_WIDTH)),
        (jax.ShapeDtypeStruct((rows, HG_WIDTH), BF16), row(HG_WIDTH)),
        (jax.ShapeDtypeStruct((rows, HG_WIDTH), F32), row(HG_WIDTH)),
        (jax.ShapeDtypeStruct((rows, HG_WIDTH), BF16), row(HG_WIDTH)),
        (jax.ShapeDtypeStruct((rows, HG_WIDTH), BF16), row(HG_WIDTH)),
        (jax.ShapeDtypeStruct((rows, D_MODEL), BF16), row(D_MODEL)),
        (jax.ShapeDtypeStruct((rows, D_MODEL), BF16), row(D_MODEL)),
    ]
    scratch = []
    if fold:
        grp = min(PROJ_ROWS, tm)
        per_tile = tm // grp
        tri = jnp.asarray(np.tril(np.ones((grp, grp), np.float32)), BF16)
        seg = np.zeros((FOX_WIDTH, LANES), np.float32)
        seg[np.arange(FOX_WIDTH), np.arange(FOX_WIDTH) // FOX_HEAD_DIM] = 1.0
        consts = _bias_fold_constants() + (jnp.asarray(seg, BF16),)
        in_specs += [_const_spec(tri.shape)] + [_const_spec(c.shape) for c in consts]
        args += [tri, *consts]
        head_major = (jax.ShapeDtypeStruct((FOX_HEADS, rows, FOX_PAD), BF16),
                      pl.BlockSpec((FOX_HEADS, tm, FOX_PAD), lambda i: (0, i, 0)))
        head_major_t = lambda depth: (jax.ShapeDtypeStruct((FOX_HEADS, depth, rows), BF16),
                                      pl.BlockSpec((FOX_HEADS, depth, tm), lambda i: (0, 0, i)))
        stats = (jax.ShapeDtypeStruct((n * per_tile, 8, LANES), F32),
                 pl.BlockSpec((per_tile, 8, LANES), lambda i: (i, 0, 0)))
        v_tiles = (jax.ShapeDtypeStruct((FOX_HEADS, n * per_tile, FOX_V_ROWS, grp), BF16),
                   pl.BlockSpec((FOX_HEADS, per_tile, FOX_V_ROWS, grp), lambda i: (0, i, 0, 0)))
        outs = [head_major_t(FOX_PAD), head_major, v_tiles, stats] + common_out
        scratch = [pltpu.VMEM((1, LANES), F32)]
    else:
        outs = [(jax.ShapeDtypeStruct((rows, FOX_WIDTH), BF16), row(FOX_WIDTH))] + common_out
    return pl.pallas_call(
        functools.partial(_proj_kernel, fold=fold),
        grid=(n,),
        in_specs=in_specs,
        out_specs=[o[1] for o in outs],
        out_shape=[o[0] for o in outs],
        scratch_shapes=scratch,
        compiler_params=pltpu.CompilerParams(dimension_semantics=("arbitrary",), vmem_limit_bytes=VMEM_LIMIT),
        name="proj_fold" if fold else "proj",
    )(*args)


def _fox_kernel(ks_ref, nact_ref, order_ref, tight_ref, qn_ref, kn_ref, cl_ref, qt_ref, k_hbm, vt_hbm, o_ref,
                kbuf, vbuf, sem, slot0_ref, m_ref, acc_ref, *, nq):
    q_blk = pl.program_id(0)
    k_first = ks_ref[q_blk]
    tq = qt_ref.shape[2]
    n_slots, tk = kbuf.shape[0], kbuf.shape[2]
    tiles, tile = vbuf.shape[2], vbuf.shape[4]
    table = lambda ref, blk, h: lax.bitcast_convert_type(ref[blk * FOX_HEADS + h], F32)

    def successor(q, k):
        at_diagonal = k == q
        return (jnp.where(q >= nq, nq, jnp.where(at_diagonal, q + 1, q)),
                jnp.where(at_diagonal, ks_ref[jnp.minimum(q + 1, nq - 1)], k + 1))

    def for_swept_heads(q, k, slot, action):
        n_swept = nact_ref[q * nq + k]
        rows = pl.ds(pl.multiple_of(k * tk, tk), tk)
        for j in range(0, FOX_HEADS, 2):
            @pl.when(j < n_swept)
            def _():
                for h in (order_ref[q * FOX_HEADS + j], order_ref[q * FOX_HEADS + j + 1]):
                    action(pltpu.make_async_copy(k_hbm.at[h, rows, :], kbuf.at[slot, h], sem.at[0, slot]))
                    action(pltpu.make_async_copy(vt_hbm.at[h, pl.ds(k * tiles, tiles)], vbuf.at[slot, h],
                                                 sem.at[1, slot]))

    def start(q, k, slot):
        @pl.when(q < nq)
        def _():
            for_swept_heads(q, k, slot, lambda c: c.start())

    def wait(q, k, slot):
        for_swept_heads(q, k, slot, lambda c: c.wait())

    lead = n_slots - 1

    def request_ahead(k_blk, slot):
        q, k = q_blk, k_blk
        for _ in range(lead):
            q, k = successor(q, k)
        start(q, k, lax.rem(slot + lead, n_slots))

    @pl.when(q_blk == 0)
    def _():
        slot0_ref[0] = 0
        q, k = q_blk, k_first
        for ahead in range(lead):
            start(q, k, ahead)
            q, k = successor(q, k)

    slot0 = slot0_ref[0]
    slot_of = lambda k_blk: lax.rem(slot0 + (k_blk - k_first), n_slots)
    m_ref[...] = jnp.full_like(m_ref, NEG_INF)
    acc_ref[...] = jnp.zeros_like(acc_ref)

    def logits(h, slot):
        return jnp.dot(kbuf[slot, h], qt_ref[h], preferred_element_type=F32)

    def softmax_update(h, s, k_blk, slot, masked, bounded):
        if masked:
            key = lax.broadcasted_iota(jnp.int32, (tk, tq), 0)
            qry = lax.broadcasted_iota(jnp.int32, (tk, tq), 1)
            s = jnp.where(key <= qry, s, NEG_INF)
        m_prev = m_ref[h]
        if bounded:
            bound = table(qn_ref, q_blk, h) * table(kn_ref, k_blk, h)
            if not masked:
                cum_q = jnp.sum(qt_ref[h, FOX_HEAD_DIM:FOX_HEAD_DIM + 3, :].astype(F32), axis=0, keepdims=True)
                bound = cum_q + (bound - table(cl_ref, k_blk, h))
            m_new = jnp.maximum(m_prev, bound)
        else:
            m_new = jnp.maximum(m_prev, jnp.max(s, axis=0, keepdims=True))
        alpha = jnp.exp2(m_prev - m_new)
        p = jnp.exp2(s - m_new).astype(BF16)
        pv = sum(jnp.dot(vbuf[slot, h, j], p[j * tile:(j + 1) * tile], preferred_element_type=F32)
                 for j in range(tiles))
        acc_ref[h] = alpha * acc_ref[h] + pv
        m_ref[h] = m_new

    def sweep(heads, k_blk, slot, masked, bounded):
        s_next = logits(heads[0], slot)
        for j, h in enumerate(heads):
            s = s_next
            if j + 1 < len(heads):
                s_next = logits(heads[j + 1], slot)
            softmax_update(h, s, k_blk, slot, masked, bounded)

    tight = tight_ref[q_blk] != 0

    def sweep_either(heads, k_blk, slot, masked):
        @pl.when(tight)
        def _():
            sweep(heads, k_blk, slot, masked, True)

        @pl.when(jnp.logical_not(tight))
        def _():
            sweep(heads, k_blk, slot, masked, False)

    def past_block(k_blk, carry):
        slot = slot_of(k_blk)
        wait(q_blk, k_blk, slot)
        request_ahead(k_blk, slot)
        n_active = nact_ref[q_blk * nq + k_blk]
        for count in range(2, FOX_HEADS + 1, 2):
            @pl.when(n_active == count)
            def _():
                sweep_either([order_ref[q_blk * FOX_HEADS + j] for j in range(count)], k_blk, slot, False)
        return carry

    lax.fori_loop(k_first, q_blk, past_block, 0)

    slot = slot_of(q_blk)
    wait(q_blk, q_blk, slot)
    request_ahead(q_blk, slot)
    slot0_ref[0] = lax.rem(slot + 1, n_slots)

    sweep_either(list(range(FOX_HEADS)), q_blk, slot, True)
    for c in range(FOX_WIDTH // LANES):
        halves = []
        for hd in (2 * c, 2 * c + 1):
            acc = acc_ref[hd]
            halves.append(acc[:FOX_HEAD_DIM, :] / acc[FOX_HEAD_DIM:FOX_HEAD_DIM + 1, :])
        o_ref[:, c * LANES:(c + 1) * LANES] = jnp.concatenate(halves, axis=0).T.astype(o_ref.dtype)


def _fox_schedule(stats, nq):
    st = stats[:, :5, :FOX_HEADS].reshape(nq, -1, 5, FOX_HEADS)
    qn = jnp.sqrt(jnp.max(st[:, :, 0], axis=1)) * NORM_MARGIN
    kn = jnp.sqrt(jnp.max(st[:, :, 1], axis=1)) * NORM_MARGIN
    dmin = jnp.min(st[:, :, 2], axis=1)
    c_first = st[:, 0, 3]
    c_last = st[:, -1, 4]
    bound = qn[:, None] * kn[None, :] + (c_first - dmin)[:, None] - c_last[None, :]
    blk = jnp.arange(nq, dtype=jnp.int32)
    drop = (bound < -FOX_SKIP_NATS) & (blk[None, :] < blk[:, None])[:, :, None]
    prefix = jnp.cumsum(jnp.logical_not(drop).astype(jnp.int32), axis=1) == 0
    kstart_h = jnp.sum(prefix.astype(jnp.int32), axis=1)
    kstart = jnp.min(kstart_h, axis=1)
    hd = jnp.arange(FOX_HEADS, dtype=jnp.int32)
    before = (kstart_h[:, None, :] < kstart_h[:, :, None]) | (
        (kstart_h[:, None, :] == kstart_h[:, :, None]) & (hd[None, None, :] < hd[None, :, None]))
    rank = jnp.sum(before.astype(jnp.int32), axis=2)
    order = jnp.sum(jnp.where(rank[:, None, :] == hd[None, :, None], hd[None, None, :], 0), axis=2)
    n_active = jnp.sum((kstart_h[:, None, :] <= blk[None, :, None]).astype(jnp.int32), axis=2)
    n_active = jnp.minimum((n_active + 1) // 2 * 2, FOX_HEADS)
    slack = 2.0 * LOG2E * jnp.max(qn * jnp.max(kn, axis=0, keepdims=True), axis=1)
    tight = slack <= FOX_BOUND_SLACK_MAX
    as_i32 = lambda a: a.astype(jnp.int32).reshape(-1)
    bits = lambda a: lax.bitcast_convert_type(a.astype(F32), jnp.int32).reshape(-1)
    return (as_i32(kstart), as_i32(n_active), as_i32(order), as_i32(tight),
            bits(qn * LOG2E), bits(kn), bits(c_last * LOG2E))


def _fox_prompt(qt, kh, vt, stats):
    seq = kh.shape[1]
    assert FOX_TQ == FOX_TK
    nq = seq // FOX_TQ
    tile = vt.shape[3]
    tables = _fox_schedule(stats, nq)
    grid_spec = pltpu.PrefetchScalarGridSpec(
        num_scalar_prefetch=len(tables),
        grid=(nq,),
        in_specs=[
            pl.BlockSpec((FOX_HEADS, FOX_PAD, FOX_TQ), lambda q, *_: (0, 0, q)),
            pl.BlockSpec(memory_space=pl.ANY),
            pl.BlockSpec(memory_space=pl.ANY),
        ],
        out_specs=pl.BlockSpec((FOX_TQ, FOX_WIDTH), lambda q, *_: (q, 0)),
        scratch_shapes=[pltpu.VMEM((FOX_SLOTS, FOX_HEADS, FOX_TK, FOX_PAD), BF16),
                        pltpu.VMEM((FOX_SLOTS, FOX_HEADS, FOX_TK // tile, FOX_V_ROWS, tile), BF16),
                        pltpu.SemaphoreType.DMA((2, FOX_SLOTS)),
                        pltpu.SMEM((1,), jnp.int32),
                        pltpu.VMEM((FOX_HEADS, 1, FOX_TQ), F32),
                        pltpu.VMEM((FOX_HEADS, FOX_V_ROWS, FOX_TQ), F32)],
    )
    return pl.pallas_call(
        functools.partial(_fox_kernel, nq=nq),
        grid_spec=grid_spec,
        out_shape=jax.ShapeDtypeStruct((seq, FOX_WIDTH), BF16),
        compiler_params=pltpu.CompilerParams(dimension_semantics=("arbitrary",), vmem_limit_bytes=VMEM_LIMIT),
        name="fox_prompt",
    )(*tables, qt, kh, vt)


def _lane_cumsum(x):
    n = x.shape[1]
    lane = lax.broadcasted_iota(jnp.int32, x.shape, 1)
    shift = 1
    while shift < n:
        x = x + jnp.where(lane >= shift, pltpu.roll(x, shift, axis=1), 0.0)
        shift *= 2
    return x


def _fox_sample_kernel(q_ref, kn_ref, vn_ref, lft_ref, ck_ref, cv_ref, o_ref,
                       cq_ref, cum_ref, m_ref, l_ref, acc_ref, *, n_past_chunks, tk, t_new):
    c = pl.program_id(1)
    rows = FOX_HEADS * t_new
    heads = [slice(h * FOX_HEAD_DIM, (h + 1) * FOX_HEAD_DIM) for h in range(FOX_HEADS)]
    head_rows = [slice(h * t_new, (h + 1) * t_new) for h in range(FOX_HEADS)]

    @pl.when(c == 0)
    def _():
        cum = _lane_cumsum(lft_ref[0])
        for j in range(n_past_chunks):
            cum_ref[j] = cum[:, j * tk:(j + 1) * tk]
        new_cum = cum[:, n_past_chunks * tk:n_past_chunks * tk + LANES]
        cum_ref[n_past_chunks, :, :LANES] = new_cum
        new_cum_t = jnp.concatenate([new_cum] * (LANES // FOX_HEADS), axis=0).T
        for h in range(FOX_HEADS):
            cq_ref[head_rows[h], :] = jnp.broadcast_to(new_cum_t[:t_new, h:h + 1], (t_new, LANES))
        m_ref[...] = jnp.full_like(m_ref, NEG_INF)
        l_ref[...] = jnp.zeros_like(l_ref)
        acc_ref[...] = jnp.zeros_like(acc_ref)

    nt = (((1,), (1,)), ((), ()))

    def update(qk, pv_of, ck_rows, mask):
        q = q_ref[...]
        s = jnp.concatenate([qk(q[:, heads[h]], h) for h in range(FOX_HEADS)], axis=0)
        s = s + (cq_ref[:, :1] - ck_rows)
        if mask is not None:
            s = jnp.where(mask, s, NEG_INF)
        m_prev = m_ref[...]
        m_new = jnp.maximum(m_prev, jnp.max(s, axis=1, keepdims=True))
        alpha = jnp.exp(m_prev - m_new)
        p = jnp.exp(s - m_new[:, :1])
        l_ref[...] = alpha * l_ref[...] + jnp.sum(p, axis=1, keepdims=True)
        p = p.astype(BF16)
        pv = jnp.concatenate([pv_of(p[head_rows[h]], h) for h in range(FOX_HEADS)], axis=0)
        acc_ref[...] = alpha[:, :FOX_HEAD_DIM] * acc_ref[...] + pv
        m_ref[...] = m_new

    def expand_rows(x, width):
        return jnp.concatenate([jnp.broadcast_to(x[h:h + 1, :], (t_new, width)) for h in range(FOX_HEADS)], axis=0)

    @pl.when(c < n_past_chunks)
    def _():
        update(lambda qh, h: jnp.dot(qh, ck_ref[0, h].astype(BF16), preferred_element_type=F32),
               lambda ph, h: lax.dot_general(ph, cv_ref[0, h].astype(BF16), nt, preferred_element_type=F32),
               expand_rows(cum_ref[c], tk), None)

    @pl.when(c == n_past_chunks)
    def _():
        ck_rows = expand_rows(cum_ref[n_past_chunks, :, :LANES], LANES)[:, :t_new]
        rowt = lax.broadcasted_iota(jnp.int32, (rows, t_new), 0) % t_new
        coli = lax.broadcasted_iota(jnp.int32, (rows, t_new), 1)
        update(lambda qh, h: lax.dot_general(qh, kn_ref[:, heads[h]].astype(BF16), nt, preferred_element_type=F32),
               lambda ph, h: jnp.dot(ph, vn_ref[:, heads[h]].astype(BF16), preferred_element_type=F32),
               ck_rows, coli <= rowt)
        out = acc_ref[...] / l_ref[:, :FOX_HEAD_DIM]
        o_ref[...] = jnp.concatenate([out[head_rows[h]] for h in range(FOX_HEADS)], axis=1).astype(o_ref.dtype)


def _fox_sample(qs, k_new, v_new, lf_all_t, cache_k, cache_v, *, t_new):
    nb, past = cache_k.shape[0], cache_k.shape[3]
    tk = SAMPLE_TK
    npc = past // tk
    rows = FOX_HEADS * t_new
    last = npc - 1
    kern = functools.partial(_fox_sample_kernel, n_past_chunks=npc, tk=tk, t_new=t_new)
    cache_spec = pl.BlockSpec((1, FOX_HEADS, FOX_HEAD_DIM, tk), lambda b, c: (b, 0, 0, jnp.minimum(c, last)))
    return pl.pallas_call(
        kern,
        grid=(nb, npc + 1),
        in_specs=[
            pl.BlockSpec((t_new, FOX_WIDTH), lambda b, c: (b, 0)),
            pl.BlockSpec((t_new, FOX_WIDTH), lambda b, c: (b, 0)),
            pl.BlockSpec((t_new, FOX_WIDTH), lambda b, c: (b, 0)),
            pl.BlockSpec((1, FOX_HEADS, past + LANES), lambda b, c: (b, 0, 0)),
            cache_spec, cache_spec,
        ],
        out_specs=pl.BlockSpec((t_new, FOX_WIDTH), lambda b, c: (b, 0)),
        out_shape=jax.ShapeDtypeStruct((nb * t_new, FOX_WIDTH), BF16),
        scratch_shapes=[
            pltpu.VMEM((rows, LANES), F32),
            pltpu.VMEM((npc + 1, FOX_HEADS, tk), F32),
            pltpu.VMEM((rows, LANES), F32),
            pltpu.VMEM((rows, LANES), F32),
            pltpu.VMEM((rows, FOX_HEAD_DIM), F32),
        ],
        compiler_params=pltpu.CompilerParams(dimension_semantics=("arbitrary", "arbitrary"),
                                             vmem_limit_bytes=VMEM_LIMIT),
        name="fox_sample",
    )(qs, k_new, v_new, lf_all_t, cache_k, cache_v)


def _hgrn_kernel(hq_ref, hk_ref, lfh_ref, hi_ref, hg_ref, s0_ref, norm_ref, tri_ref,
                 o_ref, sout_ref, st_ref, qq_ref, kk_ref, eb_ref, b_ref, oin_ref, od_ref):
    t = pl.program_id(1)
    tc = hq_ref.shape[0]
    blk = min(HG_BLOCK, tc)
    nblk = tc // blk
    heads = [slice(h * HG_DIM, (h + 1) * HG_DIM) for h in range(HG_HEADS)]

    @pl.when(t == 0)
    def _():
        for h in range(HG_HEADS):
            st_ref[h] = s0_ref[0, h].T

    def at_block_row(x, r):
        x3 = x.reshape(nblk, blk, HG_DIM)
        return jnp.broadcast_to(x3[:, r:r + 1, :], x3.shape).reshape(tc, HG_DIM)

    def rel_to_middle(b):
        return b - at_block_row(b, blk // 2 - 1)

    worst = jnp.zeros((), F32)
    for sl in heads:
        b = _sum_by_01_matrix(tri_ref[...], lfh_ref[:, sl])
        eb = jnp.exp(b)
        qq_ref[:, sl] = (hq_ref[:, sl].astype(F32) * eb).astype(BF16)
        kk_ref[:, sl] = (hk_ref[:, sl].astype(F32) * jnp.exp(at_block_row(b, blk - 1) - b)).astype(BF16)
        eb_ref[:, sl] = eb
        b_ref[:, sl] = b
        worst = jnp.maximum(worst, jnp.max(jnp.abs(rel_to_middle(b))))

    def carried_state_and_output():
        for j in range(nblk):
            rows = slice(j * blk, (j + 1) * blk)
            for h, sl in enumerate(heads):
                st = st_ref[h]
                oin_ref[rows, sl] = lax.dot_general(qq_ref[rows, sl], st.astype(BF16), (((1,), (1,)), ((), ())),
                                                    preferred_element_type=F32)
                upd = lax.dot_general(hi_ref[rows, sl], kk_ref[rows, sl], (((0,), (0,)), ((), ())),
                                      preferred_element_type=F32)
                st_ref[h] = st * eb_ref[(j + 1) * blk - 1:(j + 1) * blk, sl] + upd
        for sl in heads:
            o = oin_ref[:, sl] + od_ref[:, sl]
            y = (_rms_scale(o) * norm_ref[...]) * hg_ref[:, sl].astype(F32)
            o_ref[:, sl] = y.astype(o_ref.dtype)

    splittable = worst <= HG_SPLIT_MAX

    @pl.when(splittable)
    def _():
        row = lax.broadcasted_iota(jnp.int32, (tc, tc), 0)
        col = lax.broadcasted_iota(jnp.int32, (tc, tc), 1)
        pair_in_block = (row // blk == col // blk) & (col <= row)
        for sl in heads:
            b_rel = rel_to_middle(b_ref[:, sl])
            qs = (hq_ref[:, sl].astype(F32) * jnp.exp(b_rel)).astype(BF16)
            ks = (hk_ref[:, sl].astype(F32) * jnp.exp(-b_rel)).astype(BF16)
            a = lax.dot_general(qs, ks, (((1,), (1,)), ((), ())), preferred_element_type=F32)
            a = jnp.where(pair_in_block, a, 0.0).astype(BF16)
            od_ref[:, sl] = jnp.dot(a, hi_ref[:, sl], preferred_element_type=F32)
        carried_state_and_output()

    @pl.when(jnp.logical_not(splittable))
    def _():
        row_in_blk = lax.broadcasted_iota(jnp.int32, (tc, HG_DIM), 0) % blk
        for sl in heads:
            q = hq_ref[:, sl].astype(F32)
            k = hk_ref[:, sl].astype(F32)
            v = hi_ref[:, sl].astype(F32)
            b = b_ref[:, sl]

            def lag_step(lag, od):
                k_l = pltpu.roll(k, lag, axis=0)
                b_l = pltpu.roll(b, lag, axis=0)
                v_l = pltpu.roll(v, lag, axis=0)
                w = q * k_l * jnp.exp(jnp.minimum(b - b_l, 0.0))
                w = jnp.where(row_in_blk >= lag, w, 0.0)
                return od + jnp.sum(w, axis=1, keepdims=True) * v_l

            od_ref[:, sl] = lax.fori_loop(1, blk, lag_step, jnp.sum(q * k, axis=1, keepdims=True) * v)
        carried_state_and_output()

    @pl.when(t == pl.num_programs(1) - 1)
    def _():
        for h in range(HG_HEADS):
            sout_ref[0, h] = st_ref[h].T


def _hgrn_cumsum_matrix(n, blk):
    t = np.arange(n)[:, None]
    s = np.arange(n)[None, :]
    return jnp.asarray((((t // blk) == (s // blk)) & (s <= t)).astype(np.float32), BF16)


def _hgrn(hq, hk, lfh, hi, hg, s0, norm, *, nseq):
    rows = hq.shape[0]
    t_len = rows // nseq
    tc = min(HG_TC, t_len)
    nt = t_len // tc
    blk = min(HG_BLOCK, tc)
    tri = _hgrn_cumsum_matrix(tc, blk)
    row = pl.BlockSpec((tc, HG_WIDTH), lambda b, t: (b * nt + t, 0))
    state = pl.BlockSpec((1, HG_HEADS, HG_DIM, HG_DIM), lambda b, t: (b, 0, 0, 0))
    return pl.pallas_call(
        _hgrn_kernel,
        grid=(nseq, nt),
        in_specs=[row, row, row, row, row, state,
                  pl.BlockSpec((1, HG_DIM), lambda b, t: (0, 0)),
                  pl.BlockSpec((tc, tc), lambda b, t: (0, 0))],
        out_specs=[row, state],
        out_shape=[jax.ShapeDtypeStruct((rows, HG_WIDTH), BF16),
                   jax.ShapeDtypeStruct((nseq, HG_HEADS, HG_DIM, HG_DIM), F32)],
        scratch_shapes=[
            pltpu.VMEM((HG_HEADS, HG_DIM, HG_DIM), F32),
            pltpu.VMEM((tc, HG_WIDTH), BF16),
            pltpu.VMEM((tc, HG_WIDTH), BF16),
            pltpu.VMEM((tc, HG_WIDTH), F32),
            pltpu.VMEM((tc, HG_WIDTH), F32),
            pltpu.VMEM((tc, HG_WIDTH), F32),
            pltpu.VMEM((tc, HG_WIDTH), F32),
        ],
        compiler_params=pltpu.CompilerParams(dimension_semantics=("arbitrary", "arbitrary"),
                                             vmem_limit_bytes=VMEM_LIMIT),
        name="hgrn",
    )(hq, hk, lfh, hi, hg, s0, norm, tri)


def _mixffn_kernel(x_ref, of_ref, oh_ref, ga_ref, gb_ref, hist_ref,
                   wbf_ref, wbh_ref, wout_ref, wup_ref, wdn_ref,
                   npost_ref, npre2_ref, npost2_ref, cw_ref, cb_ref,
                   y_ref, conv_ref, tail_ref, *, seg_len):
    tm = x_ref.shape[0]
    n_chunks = D_FF // FFN_CHUNK
    carried = seg_len >= tm
    if carried:
        @pl.when(pl.program_id(0) == 0)
        def _():
            tail_ref[...] = hist_ref[0]

    groups = [slice(r, r + FFN_ROWS) for r in range(0, tm, FFN_ROWS)] if tm > FFN_ROWS else [slice(0, tm)]

    def mix_stage(rows):
        br_f = jnp.dot(of_ref[rows, :], wbf_ref[...], preferred_element_type=F32)
        br_h = jnp.dot(oh_ref[rows, :], wbh_ref[...], preferred_element_type=F32)
        merged = ga_ref[rows, :].astype(F32) * br_f + gb_ref[rows, :].astype(F32) * br_h
        mix = jnp.dot(merged.astype(BF16), wout_ref[...], preferred_element_type=F32)
        x1 = x_ref[rows, :] + _rms_scale(mix) * npost_ref[...]
        return x1, (_rms_scale(x1) * npre2_ref[...]).astype(BF16)

    def ffn_stage(rows, x1, h2):
        n_rows = rows.stop - rows.start
        def up_chunk(j):
            w = jnp.concatenate([wup_ref[:, j * FFN_CHUNK:(j + 1) * FFN_CHUNK],
                                 wup_ref[:, D_FF + j * FFN_CHUNK:D_FF + (j + 1) * FFN_CHUNK]], axis=1)
            return jnp.dot(h2, w, preferred_element_type=F32)

        rowi = lax.broadcasted_iota(jnp.int32, (n_rows, FFN_CHUNK), 0)
        ff = None
        up_next = up_chunk(0)
        for j in range(n_chunks):
            cols = slice(j * FFN_CHUNK, (j + 1) * FFN_CHUNK)
            up = up_next
            if j + 1 < n_chunks:
                up_next = up_chunk(j + 1)
            a = up[:, :FFN_CHUNK]
            g = up[:, FFN_CHUNK:]
            prev1 = pltpu.roll(a, 1, axis=0)
            prev2 = pltpu.roll(a, 2, axis=0)
            if carried:
                t0 = tail_ref[0:1, cols]
                t1 = tail_ref[1:2, cols]
                prev1 = jnp.where(rowi == 0, t1, prev1)
                prev2 = jnp.where(rowi == 0, t0, jnp.where(rowi == 1, t1, prev2))
                tail_ref[:, cols] = a[n_rows - 2:, :]
                conv_ref[0, :, cols] = a[n_rows - 2:, :]
            else:
                for s in range(n_rows // seg_len):
                    h0 = hist_ref[s, 0:1, cols]
                    h1 = hist_ref[s, 1:2, cols]
                    prev1 = jnp.where(rowi == s * seg_len, h1, prev1)
                    prev2 = jnp.where(rowi == s * seg_len, h0, jnp.where(rowi == s * seg_len + 1, h1, prev2))
                    conv_ref[s, :, cols] = a[(s + 1) * seg_len - 2:(s + 1) * seg_len, :]
            c = cb_ref[:, cols] + cw_ref[0:1, cols] * prev2 + cw_ref[1:2, cols] * prev1 + cw_ref[2:3, cols] * a
            act = (jax.nn.gelu(c, approximate=True) * g).astype(BF16)
            part = jnp.dot(act, wdn_ref[cols, :], preferred_element_type=F32)
            ff = part if ff is None else ff + part
        y_ref[rows, :] = x1 + _rms_scale(ff) * npost2_ref[...]

    mixed = [mix_stage(rows) for rows in groups]
    for rows, (x1, h2) in zip(groups, mixed):
        ffn_stage(rows, x1, h2)


def _mixffn(x, o_fox, o_hg, ga, gb, hist, w, *, seg_len):
    rows = x.shape[0]
    tm = min(FFN_TM, rows)
    n = rows // tm
    nseg = hist.shape[0]
    row = lambda width: pl.BlockSpec((tm, width), lambda i: (i, 0))
    weights = [w["bf"], w["bh"], w["out"], w["up"], w["down"]]
    smalls = [w["npost"], w["npre2"], w["npost2"], w["conv_w"], w["conv_b"]]
    hist_spec = pl.BlockSpec(hist.shape, lambda i: (0, 0, 0))
    scratch = [pltpu.VMEM((2, D_FF), F32)]
    return pl.pallas_call(
        functools.partial(_mixffn_kernel, seg_len=seg_len),
        grid=(n,),
        in_specs=[row(D_MODEL), row(FOX_WIDTH), row(HG_WIDTH), row(D_MODEL), row(D_MODEL), hist_spec]
                 + [_const_spec(a.shape) for a in weights] + [_const_spec(a.shape) for a in smalls],
        out_specs=[row(D_MODEL), pl.BlockSpec((nseg, 2, D_FF), lambda i: (0, 0, 0))],
        out_shape=[jax.ShapeDtypeStruct((rows, D_MODEL), F32), jax.ShapeDtypeStruct((nseg, 2, D_FF), F32)],
        scratch_shapes=scratch,
        compiler_params=pltpu.CompilerParams(dimension_semantics=("arbitrary",), vmem_limit_bytes=VMEM_LIMIT),
        name="mixffn",
    )(x, o_fox, o_hg, ga, gb, hist, *weights, *smalls)


def _prep_w_in(w_in, fox_f_bias):
    offs = np.cumsum([0] + IN_SIZES)
    w_in = w_in.astype(BF16)
    seg = [w_in[:, int(offs[i]):int(offs[i + 1])] for i in range(len(IN_SIZES))]
    pad = jnp.zeros((D_MODEL, LANES - 3 * FOX_HEADS), BF16)
    f3 = jnp.concatenate([seg[3], seg[3], seg[3], pad], axis=1)
    w_all = jnp.concatenate(seg[:3] + [f3] + seg[4:], axis=1)
    fb = fox_f_bias.astype(F32)
    fb3 = jnp.concatenate([fb, fb, fb, jnp.zeros((LANES - 3 * FOX_HEADS,), F32)]).reshape(1, LANES)
    return w_all, fb3


def kernel(x_prompt, x_sample, cache_fox_k, cache_fox_v, cache_fox_logf, state_hgrn, state_ffn_conv, norm_mix_pre, norm_mix_post, w_in, fox_f_bias, hgrn_lb_logits, hgrn_norm, w_branch_fox, w_branch_hgrn, w_out, norm_ffn_pre, norm_ffn_post, w_up, ffn_conv_w, ffn_conv_b, w_down):
    depth = w_in.shape[0]
    assert depth == 1 and hgrn_lb_logits.shape[0] == 2
    bp, seq, _ = x_prompt.shape
    assert bp == 1
    nb, t_new, _ = x_sample.shape
    past = cache_fox_k.shape[2]

    w_all, fb3 = _prep_w_in(w_in[0], fox_f_bias[0])
    g_pre = norm_mix_pre[0].reshape(1, D_MODEL)
    lbl = hgrn_lb_logits.astype(F32)
    hnorm = hgrn_norm[0].astype(F32).reshape(1, HG_DIM)
    w = {
        "bf": w_branch_fox[0].astype(BF16), "bh": w_branch_hgrn[0].astype(BF16), "out": w_out[0].astype(BF16),
        "up": w_up[0].astype(BF16), "down": w_down[0].astype(BF16),
        "npost": norm_mix_post[0].reshape(1, D_MODEL), "npre2": norm_ffn_pre[0].reshape(1, D_MODEL),
        "npost2": norm_ffn_post[0].reshape(1, D_MODEL),
        "conv_w": ffn_conv_w[0], "conv_b": ffn_conv_b[0].reshape(1, D_FF),
    }

    xp = x_prompt.reshape(seq, D_MODEL)
    (qt, kh, vt, stats, pk, pv, plf, hq, hk, lfh, hi, hg, ga, gb) = _proj(xp, g_pre, w_all, fb3, lbl, fold=True)
    o_fox = _fox_prompt(qt, kh, vt, stats)
    s0 = jnp.zeros((1, HG_HEADS, HG_DIM, HG_DIM), F32)
    o_hg, p_state = _hgrn(hq, hk, lfh, hi, hg, s0, hnorm, nseq=1)
    hist0 = jnp.zeros((1, 2, D_FF), F32)
    yp, pconv = _mixffn(xp, o_fox, o_hg, ga, gb, hist0, w, seg_len=seq)

    xs = x_sample.reshape(nb * t_new, D_MODEL)
    (qs, sk, sv, slf, hq, hk, lfh, hi, hg, ga, gb) = _proj(xs, g_pre, w_all, fb3, lbl, fold=False)
    lf_all_t = jnp.concatenate([
        jnp.swapaxes(cache_fox_logf[0].astype(F32), 1, 2),
        jnp.swapaxes(slf.reshape(nb, t_new, FOX_HEADS), 1, 2),
        jnp.zeros((nb, FOX_HEADS, LANES - t_new), F32)], axis=2)
    cache_kt = jnp.transpose(cache_fox_k[0], (0, 2, 3, 1))
    cache_vt = jnp.transpose(cache_fox_v[0], (0, 2, 3, 1))
    o_fox_s = _fox_sample(qs, sk, sv, lf_all_t, cache_kt, cache_vt, t_new=t_new)
    o_hg_s, s_state = _hgrn(hq, hk, lfh, hi, hg, state_hgrn[0].astype(F32), hnorm, nseq=nb)
    ys, sconv = _mixffn(xs, o_fox_s, o_hg_s, ga, gb, state_ffn_conv[0], w, seg_len=t_new)

    return (
        yp.reshape(bp, seq, D_MODEL),
        ys.reshape(nb, t_new, D_MODEL),
        pk.reshape(1, bp, seq, FOX_HEADS, FOX_HEAD_DIM),
        pv.reshape(1, bp, seq, FOX_HEADS, FOX_HEAD_DIM),
        plf.reshape(1, bp, seq, FOX_HEADS),
        p_state.reshape(1, bp, HG_HEADS, HG_DIM, HG_DIM),
        pconv.reshape(1, bp, 2, D_FF),
        sk.reshape(1, nb, t_new, FOX_HEADS, FOX_HEAD_DIM),
        sv.reshape(1, nb, t_new, FOX_HEADS, FOX_HEAD_DIM),
        slf.reshape(1, nb, t_new, FOX_HEADS),
        s_state.reshape(1, nb, HG_HEADS, HG_DIM, HG_DIM),
        sconv.reshape(1, nb, 2, D_FF),
    )
```
